```python
import jax, jax.numpy as jnp
from jax import lax
import numpy as np

D_MODEL = 2048
BATCH = 8
SEQ = 8192
DEPTH = 1

CHUNK = 64
MEM_LEN = 256
D_MIX = D_MODEL
D_POOL = D_MIX // 2
POOL_WINDOWS = (2, 4, 8, 16)
N_POOL_GROUPS = len(POOL_WINDOWS)
POOL_GROUP = D_POOL // N_POOL_GROUPS
D_GLA = D_MIX - D_POOL
GLA_HEADS = 4
GLA_DV = D_GLA // GLA_HEADS
GLA_DK = GLA_DV // 2
D_GLA_K = GLA_HEADS * GLA_DK
GLA_GATE_RANK = 16
GLA_GATE_TEMP = 16.0
XATTN_HEADS = 4
XATTN_HEAD_DIM = D_MODEL // XATTN_HEADS
D_FF = ((8 * D_MODEL // 3 + 255) // 256) * 256
RMS_EPS = 1e-6
IN_SPLITS = (D_POOL,
             D_POOL + D_GLA_K,
             D_POOL + 2 * D_GLA_K,
             D_POOL + 2 * D_GLA_K + D_GLA,
             D_POOL + 2 * D_GLA_K + 2 * D_GLA)
IN_COLS = D_POOL + 2 * D_GLA_K + 2 * D_GLA + GLA_GATE_RANK

kernel_name = "hymba_pool_gla_macaron_memxattn"


def rms_norm(x, gain):
    xf = x.astype(jnp.float32)
    y = xf * lax.rsqrt(jnp.mean(xf * xf, axis=-1, keepdims=True) + RMS_EPS)
    return (y * gain.astype(jnp.float32)).astype(x.dtype)


def swiglu(h, w_gate, w_up, w_down):
    return (jax.nn.silu(h @ w_gate) * (h @ w_up)) @ w_down


def pool_mixer(u, pool_w, pool_scale):
    b, s, _ = u.shape
    uf = u.astype(jnp.float32).reshape(b, s, N_POOL_GROUPS, POOL_GROUP)
    cs = jnp.cumsum(uf, axis=1)
    count = jnp.arange(1, s + 1, dtype=jnp.float32)
    diffs = []
    for g, w in enumerate(POOL_WINDOWS):
        cs_g = cs[:, :, g]
        lagged = jnp.pad(cs_g, ((0, 0), (w, 0), (0, 0)))[:, :s]
        mean = (cs_g - lagged) / jnp.minimum(count, float(w))[None, :, None]
        diffs.append(mean - uf[:, :, g])
    d = jnp.stack(diffs, axis=2).astype(u.dtype)
    y = jnp.einsum('bsgc,gcd->bsgd', d, pool_w)
    return y.reshape(b, s, D_POOL) * pool_scale


def gla_mixer(q, k, v, g, a_lr, w_a2, b_a, head_norm):
    b, s, _ = q.shape
    nc = s // CHUNK
    dt = v.dtype
    log_a = jax.nn.log_sigmoid((a_lr @ w_a2 + b_a).astype(jnp.float32)) / GLA_GATE_TEMP

    def chunks(t, d):
        return t.astype(jnp.float32).reshape(b, nc, CHUNK, GLA_HEADS, d)

    qc = chunks(q, GLA_DK) * (GLA_DK ** -0.5)
    kc = chunks(k, GLA_DK)
    vc = chunks(v, GLA_DV)
    cum = jnp.cumsum(chunks(log_a, GLA_DK), axis=2)
    b_end = cum[:, :, -1:]
    k_dec = kc * jnp.exp(b_end - cum)
    q_dec = qc * jnp.exp(b_end)
    scores = jnp.einsum('bnihk,bnjhk->bnhij', qc, k_dec)
    o_intra = jnp.einsum('bnhij,bnjhv->bnihv', scores, vc)

    def step(state, xs):
        q_c, k_c, v_c, decay_c = xs
        o = jnp.einsum('bihk,bhkv->bihv', q_c, state)
        state = decay_c[..., None] * state + jnp.einsum('bjhk,bjhv->bhkv', k_c, v_c)
        return state, o

    xs = (jnp.moveaxis(q_dec, 1, 0), jnp.moveaxis(k_dec, 1, 0),
          jnp.moveaxis(vc, 1, 0), jnp.moveaxis(jnp.exp(b_end[:, :, 0]), 1, 0))
    s0 = jnp.zeros((b, GLA_HEADS, GLA_DK, GLA_DV), jnp.float32)
    _, o_inter = lax.scan(step, s0, xs)
    o = o_intra + jnp.moveaxis(o_inter, 0, 1)
    o = o * lax.rsqrt(jnp.mean(o * o, axis=-1, keepdims=True) + RMS_EPS)
    o = o.reshape(b, s, D_GLA) * head_norm.astype(jnp.float32)
    return o.astype(dt) * jax.nn.silu(g)


def mem_cross_attention(h, mem_h, w_q, w_kv, w_o):
    b, s, _ = h.shape
    q = (h @ w_q).reshape(b, s, XATTN_HEADS, XATTN_HEAD_DIM)
    kv = (mem_h @ w_kv).reshape(b, mem_h.shape[1], 2, XATTN_HEADS, XATTN_HEAD_DIM)
    k, v = kv[:, :, 0], kv[:, :, 1]
    logits = jnp.einsum('bshd,bmhd->bhsm', q, k).astype(jnp.float32) * (XATTN_HEAD_DIM ** -0.5)
    p = jax.nn.softmax(logits, axis=-1).astype(v.dtype)
    o = jnp.einsum('bhsm,bmhd->bshd', p, v).reshape(b, s, D_MODEL)
    return o @ w_o


def _fwd_setup_inputs(seed: int = 0) -> dict:
    key = jax.random.key(seed)
    ks = jax.random.split(key, 32)

    def dense(k, shape, fan_in):
        return jax.random.normal(k, shape, jnp.float32) * (fan_in ** -0.5)

    def gain(k, shape):
        return 1.0 + 0.02 * jax.random.normal(k, shape, jnp.float32)

    L = DEPTH
    return {
        "x": jax.random.normal(ks[0], (BATCH, SEQ, D_MODEL), jnp.float32),
        "mem": jax.random.normal(ks[1], (BATCH, MEM_LEN, D_MODEL), jnp.float32),
        "ffn1_norm": gain(ks[2], (L, D_MODEL)),
        "ffn1_w_gate": dense(ks[3], (L, D_MODEL, D_FF), D_MODEL),
        "ffn1_w_up": dense(ks[4], (L, D_MODEL, D_FF), D_MODEL),
        "ffn1_w_down": dense(ks[5], (L, D_FF, D_MODEL), D_FF),
        "mix_norm": gain(ks[6], (L, D_MODEL)),
        "w_in": dense(ks[7], (L, D_MODEL, IN_COLS), D_MODEL),
        "pool_w": dense(ks[8], (L, N_POOL_GROUPS, POOL_GROUP, POOL_GROUP), POOL_GROUP),
        "pool_scale": 1.0 + 0.1 * jax.random.normal(ks[9], (L, D_POOL), jnp.float32),
        "gla_w_a2": dense(ks[10], (L, GLA_GATE_RANK, D_GLA_K), GLA_GATE_RANK),
        "gla_b_a": 0.1 * jax.random.normal(ks[11], (L, D_GLA_K), jnp.float32),
        "gla_head_norm": gain(ks[12], (L, D_GLA)),
        "w_out": dense(ks[13], (L, D_MIX, D_MODEL), D_MIX),
        "xattn_norm": gain(ks[14], (L, D_MODEL)),
        "mem_norm": gain(ks[15], (L, D_MODEL)),
        "xattn_w_q": dense(ks[16], (L, D_MODEL, D_MODEL), D_MODEL),
        "xattn_w_kv": dense(ks[17], (L, D_MODEL, 2 * D_MODEL), D_MODEL),
        "xattn_w_o": dense(ks[18], (L, D_MODEL, D_MODEL), D_MODEL),
        "ffn2_norm": gain(ks[19], (L, D_MODEL)),
        "ffn2_w_gate": dense(ks[20], (L, D_MODEL, D_FF), D_MODEL),
        "ffn2_w_up": dense(ks[21], (L, D_MODEL, D_FF), D_MODEL),
        "ffn2_w_down": dense(ks[22], (L, D_FF, D_MODEL), D_FF),
        "final_norm": gain(ks[23], (D_MODEL,)),
    }


def _fwd_reference(x, mem, ffn1_norm, ffn1_w_gate, ffn1_w_up, ffn1_w_down, mix_norm, w_in,
              pool_w, pool_scale, gla_w_a2, gla_b_a, gla_head_norm, w_out,
              xattn_norm, mem_norm, xattn_w_q, xattn_w_kv, xattn_w_o,
              ffn2_norm, ffn2_w_gate, ffn2_w_up, ffn2_w_down, final_norm):
    for l in range(DEPTH):
        x = x + 0.5 * swiglu(rms_norm(x, ffn1_norm[l]), ffn1_w_gate[l], ffn1_w_up[l], ffn1_w_down[l])
        h = rms_norm(x, mix_norm[l])
        proj = h @ w_in[l]
        u, q, k, v, g, a_lr = jnp.split(proj, list(IN_SPLITS), axis=-1)
        y_pool = pool_mixer(u, pool_w[l], pool_scale[l])
        y_gla = gla_mixer(q, k, v, g, a_lr, gla_w_a2[l], gla_b_a[l], gla_head_norm[l])
        x = x + jnp.concatenate([y_pool, y_gla], axis=-1) @ w_out[l]
        x = x + mem_cross_attention(rms_norm(x, xattn_norm[l]), rms_norm(mem, mem_norm[l]),
                                    xattn_w_q[l], xattn_w_kv[l], xattn_w_o[l])
        x = x + 0.5 * swiglu(rms_norm(x, ffn2_norm[l]), ffn2_w_gate[l], ffn2_w_up[l], ffn2_w_down[l])
    return rms_norm(x, final_norm)


import jax as _jax
import jax.numpy as _jnp

TWIN_FORMAT = 'train_step'
FWD_PARAMS = ['x', 'mem', 'ffn1_norm', 'ffn1_w_gate', 'ffn1_w_up', 'ffn1_w_down', 'mix_norm', 'w_in', 'pool_w', 'pool_scale', 'gla_w_a2', 'gla_b_a', 'gla_head_norm', 'w_out', 'xattn_norm', 'mem_norm', 'xattn_w_q', 'xattn_w_kv', 'xattn_w_o', 'ffn2_norm', 'ffn2_w_gate', 'ffn2_w_up', 'ffn2_w_down', 'final_norm']
TWIN_WEIGHTS = ['ffn1_norm', 'ffn1_w_gate', 'ffn1_w_up', 'ffn1_w_down', 'mix_norm', 'w_in', 'pool_w', 'pool_scale', 'gla_w_a2', 'gla_b_a', 'gla_head_norm', 'w_out', 'xattn_norm', 'mem_norm', 'xattn_w_q', 'xattn_w_kv', 'xattn_w_o', 'ffn2_norm', 'ffn2_w_gate', 'ffn2_w_up', 'ffn2_w_down', 'final_norm']
TWIN_DIFF_INPUT = 'x'
TWIN_INPUTS = ['x', 'mem', 'ffn1_norm', 'ffn1_w_gate', 'ffn1_w_up', 'ffn1_w_down', 'mix_norm', 'w_in', 'pool_w', 'pool_scale', 'gla_w_a2', 'gla_b_a', 'gla_head_norm', 'w_out', 'xattn_norm', 'mem_norm', 'xattn_w_q', 'xattn_w_kv', 'xattn_w_o', 'ffn2_norm', 'ffn2_w_gate', 'ffn2_w_up', 'ffn2_w_down', 'final_norm', 'loss_target', 'm_ffn1_norm', 'm_ffn1_w_gate', 'm_ffn1_w_up', 'm_ffn1_w_down', 'm_mix_norm', 'm_w_in', 'm_pool_w', 'm_pool_scale', 'm_gla_w_a2', 'm_gla_b_a', 'm_gla_head_norm', 'm_w_out', 'm_xattn_norm', 'm_mem_norm', 'm_xattn_w_q', 'm_xattn_w_kv', 'm_xattn_w_o', 'm_ffn2_norm', 'm_ffn2_w_gate', 'm_ffn2_w_up', 'm_ffn2_w_down', 'm_final_norm', 'v_ffn1_norm', 'v_ffn1_w_gate', 'v_ffn1_w_up', 'v_ffn1_w_down', 'v_mix_norm', 'v_w_in', 'v_pool_w', 'v_pool_scale', 'v_gla_w_a2', 'v_gla_b_a', 'v_gla_head_norm', 'v_w_out', 'v_xattn_norm', 'v_mem_norm', 'v_xattn_w_q', 'v_xattn_w_kv', 'v_xattn_w_o', 'v_ffn2_norm', 'v_ffn2_w_gate', 'v_ffn2_w_up', 'v_ffn2_w_down', 'v_final_norm']
TWIN_OUTPUTS = ['loss', 'grad_x', 'grad_ffn1_norm', 'grad_ffn1_w_gate', 'grad_ffn1_w_up', 'grad_ffn1_w_down', 'grad_mix_norm', 'grad_w_in', 'grad_pool_w', 'grad_pool_scale', 'grad_gla_w_a2', 'grad_gla_b_a', 'grad_gla_head_norm', 'grad_w_out', 'grad_xattn_norm', 'grad_mem_norm', 'grad_xattn_w_q', 'grad_xattn_w_kv', 'grad_xattn_w_o', 'grad_ffn2_norm', 'grad_ffn2_w_gate', 'grad_ffn2_w_up', 'grad_ffn2_w_down', 'grad_final_norm', 'delta_ffn1_norm', 'delta_ffn1_w_gate', 'delta_ffn1_w_up', 'delta_ffn1_w_down', 'delta_mix_norm', 'delta_w_in', 'delta_pool_w', 'delta_pool_scale', 'delta_gla_w_a2', 'delta_gla_b_a', 'delta_gla_head_norm', 'delta_w_out', 'delta_xattn_norm', 'delta_mem_norm', 'delta_xattn_w_q', 'delta_xattn_w_kv', 'delta_xattn_w_o', 'delta_ffn2_norm', 'delta_ffn2_w_gate', 'delta_ffn2_w_up', 'delta_ffn2_w_down', 'delta_final_norm', 'new_m_ffn1_norm', 'new_m_ffn1_w_gate', 'new_m_ffn1_w_up', 'new_m_ffn1_w_down', 'new_m_mix_norm', 'new_m_w_in', 'new_m_pool_w', 'new_m_pool_scale', 'new_m_gla_w_a2', 'new_m_gla_b_a', 'new_m_gla_head_norm', 'new_m_w_out', 'new_m_xattn_norm', 'new_m_mem_norm', 'new_m_xattn_w_q', 'new_m_xattn_w_kv', 'new_m_xattn_w_o', 'new_m_ffn2_norm', 'new_m_ffn2_w_gate', 'new_m_ffn2_w_up', 'new_m_ffn2_w_down', 'new_m_final_norm', 'new_v_ffn1_norm', 'new_v_ffn1_w_gate', 'new_v_ffn1_w_up', 'new_v_ffn1_w_down', 'new_v_mix_norm', 'new_v_w_in', 'new_v_pool_w', 'new_v_pool_scale', 'new_v_gla_w_a2', 'new_v_gla_b_a', 'new_v_gla_head_norm', 'new_v_w_out', 'new_v_xattn_norm', 'new_v_mem_norm', 'new_v_xattn_w_q', 'new_v_xattn_w_kv', 'new_v_xattn_w_o', 'new_v_ffn2_norm', 'new_v_ffn2_w_gate', 'new_v_ffn2_w_up', 'new_v_ffn2_w_down', 'new_v_final_norm']
TWIN_LEAF_KINDS = {'loss': 'loss', 'grad_x': 'grad_x', 'grad_ffn1_norm': 'grad_w', 'grad_ffn1_w_gate': 'grad_w', 'grad_ffn1_w_up': 'grad_w', 'grad_ffn1_w_down': 'grad_w', 'grad_mix_norm': 'grad_w', 'grad_w_in': 'grad_w', 'grad_pool_w': 'grad_w', 'grad_pool_scale': 'grad_w', 'grad_gla_w_a2': 'grad_w', 'grad_gla_b_a': 'grad_w', 'grad_gla_head_norm': 'grad_w', 'grad_w_out': 'grad_w', 'grad_xattn_norm': 'grad_w', 'grad_mem_norm': 'grad_w', 'grad_xattn_w_q': 'grad_w', 'grad_xattn_w_kv': 'grad_w', 'grad_xattn_w_o': 'grad_w', 'grad_ffn2_norm': 'grad_w', 'grad_ffn2_w_gate': 'grad_w', 'grad_ffn2_w_up': 'grad_w', 'grad_ffn2_w_down': 'grad_w', 'grad_final_norm': 'grad_w', 'delta_ffn1_norm': 'delta_w', 'delta_ffn1_w_gate': 'delta_w', 'delta_ffn1_w_up': 'delta_w', 'delta_ffn1_w_down': 'delta_w', 'delta_mix_norm': 'delta_w', 'delta_w_in': 'delta_w', 'delta_pool_w': 'delta_w', 'delta_pool_scale': 'delta_w', 'delta_gla_w_a2': 'delta_w', 'delta_gla_b_a': 'delta_w', 'delta_gla_head_norm': 'delta_w', 'delta_w_out': 'delta_w', 'delta_xattn_norm': 'delta_w', 'delta_mem_norm': 'delta_w', 'delta_xattn_w_q': 'delta_w', 'delta_xattn_w_kv': 'delta_w', 'delta_xattn_w_o': 'delta_w', 'delta_ffn2_norm': 'delta_w', 'delta_ffn2_w_gate': 'delta_w', 'delta_ffn2_w_up': 'delta_w', 'delta_ffn2_w_down': 'delta_w', 'delta_final_norm': 'delta_w', 'new_m_ffn1_norm': 'new_m', 'new_m_ffn1_w_gate': 'new_m', 'new_m_ffn1_w_up': 'new_m', 'new_m_ffn1_w_down': 'new_m', 'new_m_mix_norm': 'new_m', 'new_m_w_in': 'new_m', 'new_m_pool_w': 'new_m', 'new_m_pool_scale': 'new_m', 'new_m_gla_w_a2': 'new_m', 'new_m_gla_b_a': 'new_m', 'new_m_gla_head_norm': 'new_m', 'new_m_w_out': 'new_m', 'new_m_xattn_norm': 'new_m', 'new_m_mem_norm': 'new_m', 'new_m_xattn_w_q': 'new_m', 'new_m_xattn_w_kv': 'new_m', 'new_m_xattn_w_o': 'new_m', 'new_m_ffn2_norm': 'new_m', 'new_m_ffn2_w_gate': 'new_m', 'new_m_ffn2_w_up': 'new_m', 'new_m_ffn2_w_down': 'new_m', 'new_m_final_norm': 'new_m', 'new_v_ffn1_norm': 'new_v', 'new_v_ffn1_w_gate': 'new_v', 'new_v_ffn1_w_up': 'new_v', 'new_v_ffn1_w_down': 'new_v', 'new_v_mix_norm': 'new_v', 'new_v_w_in': 'new_v', 'new_v_pool_w': 'new_v', 'new_v_pool_scale': 'new_v', 'new_v_gla_w_a2': 'new_v', 'new_v_gla_b_a': 'new_v', 'new_v_gla_head_norm': 'new_v', 'new_v_w_out': 'new_v', 'new_v_xattn_norm': 'new_v', 'new_v_mem_norm': 'new_v', 'new_v_xattn_w_q': 'new_v', 'new_v_xattn_w_kv': 'new_v', 'new_v_xattn_w_o': 'new_v', 'new_v_ffn2_norm': 'new_v', 'new_v_ffn2_w_gate': 'new_v', 'new_v_ffn2_w_up': 'new_v', 'new_v_ffn2_w_down': 'new_v', 'new_v_final_norm': 'new_v'}


def _forward(args):
    return _fwd_reference(*[args[k] for k in FWD_PARAMS])


def _output_shape():
    def fwd():
        inp = _fwd_setup_inputs(0)
        return _fwd_reference(*[inp[k] for k in FWD_PARAMS])
    out = _jax.eval_shape(fwd)
    return out.shape, out.dtype

N_MICROBATCH = 1
ADAM_LR = 0.001
ADAM_B1 = 0.9
ADAM_B2 = 0.999
ADAM_EPS = 1e-08
ADAM_WD = 0.01
ADAM_STEP = 10
PER_EXAMPLE_BATCH_AXIS = {'x': 0, 'mem': 0, 'loss_target': 0}
SHARED_INPUTS = []
_WEIGHT_DTYPES = {'ffn1_norm': _jnp.float32, 'ffn1_w_gate': _jnp.float32, 'ffn1_w_up': _jnp.float32, 'ffn1_w_down': _jnp.float32, 'mix_norm': _jnp.float32, 'w_in': _jnp.float32, 'pool_w': _jnp.float32, 'pool_scale': _jnp.float32, 'gla_w_a2': _jnp.float32, 'gla_b_a': _jnp.float32, 'gla_head_norm': _jnp.float32, 'w_out': _jnp.float32, 'xattn_norm': _jnp.float32, 'mem_norm': _jnp.float32, 'xattn_w_q': _jnp.float32, 'xattn_w_kv': _jnp.float32, 'xattn_w_o': _jnp.float32, 'ffn2_norm': _jnp.float32, 'ffn2_w_gate': _jnp.float32, 'ffn2_w_up': _jnp.float32, 'ffn2_w_down': _jnp.float32, 'final_norm': _jnp.float32}
MOMENT_SCALE = {'ffn1_norm': 6.075968e-02, 'ffn1_w_gate': 2.655032e-02, 'ffn1_w_up': 2.568620e-02, 'ffn1_w_down': 4.264076e-02, 'mix_norm': 1.087036e-01, 'w_in': 7.552864e-02, 'pool_w': 8.883638e-02, 'pool_scale': 9.418821e-02, 'gla_w_a2': 9.873515e-03, 'gla_b_a': 4.278178e-02, 'gla_head_norm': 5.982343e-02, 'w_out': 7.560774e-02, 'xattn_norm': 1.032083e-02, 'mem_norm': 1.495127e-02, 'xattn_w_q': 1.014999e-02, 'xattn_w_kv': 1.023549e-02, 'xattn_w_o': 1.035834e-02, 'ffn2_norm': 4.161042e-02, 'ffn2_w_gate': 1.757804e-02, 'ffn2_w_up': 1.703892e-02, 'ffn2_w_down': 2.827159e-02, 'final_norm': 3.196066e+01}


def _to_microbatches(a, axis):
    t = _jnp.moveaxis(a, axis, 0)
    t = t.reshape((N_MICROBATCH, t.shape[0] // N_MICROBATCH) + t.shape[1:])
    return _jnp.moveaxis(t, 1, axis + 1)


def setup_inputs(seed: int = 0) -> dict:
    inp = _fwd_setup_inputs(seed)
    key = _jax.random.fold_in(_jax.random.key(seed), 7919)
    shape, _ = _output_shape()
    out = dict(inp)
    out["loss_target"] = _jax.random.normal(_jax.random.fold_in(key, 0), shape, _jnp.float32)
    for i, name in enumerate(TWIN_WEIGHTS):
        w = inp[name].astype(_jnp.float32)
        if MOMENT_SCALE is None:
            s = _jnp.sqrt(_jnp.mean(_jnp.square(w)) + 1e-30)
        else:
            s = MOMENT_SCALE[name]
        km, kv = _jax.random.split(_jax.random.fold_in(key, i + 1))
        out[name] = w
        out["m_" + name] = s * _jax.random.normal(km, w.shape, _jnp.float32)
        out["v_" + name] = (s * s) * _jax.random.uniform(kv, w.shape, _jnp.float32, 0.5, 1.5)
    if N_MICROBATCH > 1:
        for name, axis in PER_EXAMPLE_BATCH_AXIS.items():
            out[name] = _to_microbatches(out[name], axis)
    return {'x': out['x'], 'mem': out['mem'], 'ffn1_norm': out['ffn1_norm'], 'ffn1_w_gate': out['ffn1_w_gate'], 'ffn1_w_up': out['ffn1_w_up'], 'ffn1_w_down': out['ffn1_w_down'], 'mix_norm': out['mix_norm'], 'w_in': out['w_in'], 'pool_w': out['pool_w'], 'pool_scale': out['pool_scale'], 'gla_w_a2': out['gla_w_a2'], 'gla_b_a': out['gla_b_a'], 'gla_head_norm': out['gla_head_norm'], 'w_out': out['w_out'], 'xattn_norm': out['xattn_norm'], 'mem_norm': out['mem_norm'], 'xattn_w_q': out['xattn_w_q'], 'xattn_w_kv': out['xattn_w_kv'], 'xattn_w_o': out['xattn_w_o'], 'ffn2_norm': out['ffn2_norm'], 'ffn2_w_gate': out['ffn2_w_gate'], 'ffn2_w_up': out['ffn2_w_up'], 'ffn2_w_down': out['ffn2_w_down'], 'final_norm': out['final_norm'], 'loss_target': out['loss_target'], 'm_ffn1_norm': out['m_ffn1_norm'], 'm_ffn1_w_gate': out['m_ffn1_w_gate'], 'm_ffn1_w_up': out['m_ffn1_w_up'], 'm_ffn1_w_down': out['m_ffn1_w_down'], 'm_mix_norm': out['m_mix_norm'], 'm_w_in': out['m_w_in'], 'm_pool_w': out['m_pool_w'], 'm_pool_scale': out['m_pool_scale'], 'm_gla_w_a2': out['m_gla_w_a2'], 'm_gla_b_a': out['m_gla_b_a'], 'm_gla_head_norm': out['m_gla_head_norm'], 'm_w_out': out['m_w_out'], 'm_xattn_norm': out['m_xattn_norm'], 'm_mem_norm': out['m_mem_norm'], 'm_xattn_w_q': out['m_xattn_w_q'], 'm_xattn_w_kv': out['m_xattn_w_kv'], 'm_xattn_w_o': out['m_xattn_w_o'], 'm_ffn2_norm': out['m_ffn2_norm'], 'm_ffn2_w_gate': out['m_ffn2_w_gate'], 'm_ffn2_w_up': out['m_ffn2_w_up'], 'm_ffn2_w_down': out['m_ffn2_w_down'], 'm_final_norm': out['m_final_norm'], 'v_ffn1_norm': out['v_ffn1_norm'], 'v_ffn1_w_gate': out['v_ffn1_w_gate'], 'v_ffn1_w_up': out['v_ffn1_w_up'], 'v_ffn1_w_down': out['v_ffn1_w_down'], 'v_mix_norm': out['v_mix_norm'], 'v_w_in': out['v_w_in'], 'v_pool_w': out['v_pool_w'], 'v_pool_scale': out['v_pool_scale'], 'v_gla_w_a2': out['v_gla_w_a2'], 'v_gla_b_a': out['v_gla_b_a'], 'v_gla_head_norm': out['v_gla_head_norm'], 'v_w_out': out['v_w_out'], 'v_xattn_norm': out['v_xattn_norm'], 'v_mem_norm': out['v_mem_norm'], 'v_xattn_w_q': out['v_xattn_w_q'], 'v_xattn_w_kv': out['v_xattn_w_kv'], 'v_xattn_w_o': out['v_xattn_w_o'], 'v_ffn2_norm': out['v_ffn2_norm'], 'v_ffn2_w_gate': out['v_ffn2_w_gate'], 'v_ffn2_w_up': out['v_ffn2_w_up'], 'v_ffn2_w_down': out['v_ffn2_w_down'], 'v_final_norm': out['v_final_norm']}


def _loss(weights, diff, rest, loss_target):
    with _jax.named_scope("forward"):
        args = {**rest, TWIN_DIFF_INPUT: diff, **{k: w.astype(_WEIGHT_DTYPES[k]) for k, w in weights.items()}}
        y = _forward(args)
    with _jax.named_scope("loss_head"):
        err = _jnp.square(y.astype(_jnp.float32) - loss_target)
        return 0.5 * _jnp.sum(_jnp.mean(err, axis=-1)) if err.ndim else 0.5 * err


def _adamw(w, g, m, v):
    m = ADAM_B1 * m + (1.0 - ADAM_B1) * g
    v = ADAM_B2 * v + (1.0 - ADAM_B2) * _jnp.square(g)
    m_hat = m / (1.0 - ADAM_B1 ** ADAM_STEP)
    v_hat = v / (1.0 - ADAM_B2 ** ADAM_STEP)
    delta = -ADAM_LR * (m_hat / (_jnp.sqrt(v_hat) + ADAM_EPS) + ADAM_WD * w)
    return delta, m, v


def reference(x, mem, ffn1_norm, ffn1_w_gate, ffn1_w_up, ffn1_w_down, mix_norm, w_in, pool_w, pool_scale, gla_w_a2, gla_b_a, gla_head_norm, w_out, xattn_norm, mem_norm, xattn_w_q, xattn_w_kv, xattn_w_o, ffn2_norm, ffn2_w_gate, ffn2_w_up, ffn2_w_down, final_norm, loss_target, m_ffn1_norm, m_ffn1_w_gate, m_ffn1_w_up, m_ffn1_w_down, m_mix_norm, m_w_in, m_pool_w, m_pool_scale, m_gla_w_a2, m_gla_b_a, m_gla_head_norm, m_w_out, m_xattn_norm, m_mem_norm, m_xattn_w_q, m_xattn_w_kv, m_xattn_w_o, m_ffn2_norm, m_ffn2_w_gate, m_ffn2_w_up, m_ffn2_w_down, m_final_norm, v_ffn1_norm, v_ffn1_w_gate, v_ffn1_w_up, v_ffn1_w_down, v_mix_norm, v_w_in, v_pool_w, v_pool_scale, v_gla_w_a2, v_gla_b_a, v_gla_head_norm, v_w_out, v_xattn_norm, v_mem_norm, v_xattn_w_q, v_xattn_w_kv, v_xattn_w_o, v_ffn2_norm, v_ffn2_w_gate, v_ffn2_w_up, v_ffn2_w_down, v_final_norm):
    given = dict(x=x, mem=mem, ffn1_norm=ffn1_norm, ffn1_w_gate=ffn1_w_gate, ffn1_w_up=ffn1_w_up, ffn1_w_down=ffn1_w_down, mix_norm=mix_norm, w_in=w_in, pool_w=pool_w, pool_scale=pool_scale, gla_w_a2=gla_w_a2, gla_b_a=gla_b_a, gla_head_norm=gla_head_norm, w_out=w_out, xattn_norm=xattn_norm, mem_norm=mem_norm, xattn_w_q=xattn_w_q, xattn_w_kv=xattn_w_kv, xattn_w_o=xattn_w_o, ffn2_norm=ffn2_norm, ffn2_w_gate=ffn2_w_gate, ffn2_w_up=ffn2_w_up, ffn2_w_down=ffn2_w_down, final_norm=final_norm, loss_target=loss_target, m_ffn1_norm=m_ffn1_norm, m_ffn1_w_gate=m_ffn1_w_gate, m_ffn1_w_up=m_ffn1_w_up, m_ffn1_w_down=m_ffn1_w_down, m_mix_norm=m_mix_norm, m_w_in=m_w_in, m_pool_w=m_pool_w, m_pool_scale=m_pool_scale, m_gla_w_a2=m_gla_w_a2, m_gla_b_a=m_gla_b_a, m_gla_head_norm=m_gla_head_norm, m_w_out=m_w_out, m_xattn_norm=m_xattn_norm, m_mem_norm=m_mem_norm, m_xattn_w_q=m_xattn_w_q, m_xattn_w_kv=m_xattn_w_kv, m_xattn_w_o=m_xattn_w_o, m_ffn2_norm=m_ffn2_norm, m_ffn2_w_gate=m_ffn2_w_gate, m_ffn2_w_up=m_ffn2_w_up, m_ffn2_w_down=m_ffn2_w_down, m_final_norm=m_final_norm, v_ffn1_norm=v_ffn1_norm, v_ffn1_w_gate=v_ffn1_w_gate, v_ffn1_w_up=v_ffn1_w_up, v_ffn1_w_down=v_ffn1_w_down, v_mix_norm=v_mix_norm, v_w_in=v_w_in, v_pool_w=v_pool_w, v_pool_scale=v_pool_scale, v_gla_w_a2=v_gla_w_a2, v_gla_b_a=v_gla_b_a, v_gla_head_norm=v_gla_head_norm, v_w_out=v_w_out, v_xattn_norm=v_xattn_norm, v_mem_norm=v_mem_norm, v_xattn_w_q=v_xattn_w_q, v_xattn_w_kv=v_xattn_w_kv, v_xattn_w_o=v_xattn_w_o, v_ffn2_norm=v_ffn2_norm, v_ffn2_w_gate=v_ffn2_w_gate, v_ffn2_w_up=v_ffn2_w_up, v_ffn2_w_down=v_ffn2_w_down, v_final_norm=v_final_norm)
    weights = {n: given[n] for n in TWIN_WEIGHTS}
    shared = {n: given[n] for n in SHARED_INPUTS}
    per_example = {n: given[n] for n in ['x', 'mem']}
    grad_fn = _jax.value_and_grad(_loss, argnums=(0, 1))

    def one_microbatch(ex, loss_target):
        ex = dict(ex)
        diff = ex.pop(TWIN_DIFF_INPUT)
        return grad_fn(weights, diff, {**shared, **ex}, loss_target)

    if N_MICROBATCH == 1:
        loss, (grad_w, grad_x) = one_microbatch(per_example, given["loss_target"])
    else:
        def body(carry, xs):
            loss_sum, grad_sum = carry
            l_k, (gw_k, gx_k) = one_microbatch(xs[0], xs[1])
            with _jax.named_scope("update"):
                return (loss_sum + l_k, _jax.tree.map(_jnp.add, grad_sum, gw_k)), gx_k

        init = (_jnp.zeros((), _jnp.float32), _jax.tree.map(_jnp.zeros_like, weights))
        (loss, grad_w), grad_x = _jax.lax.scan(body, init, (per_example, given["loss_target"]))
    with _jax.named_scope("update"):
        delta_w, new_m, new_v = {}, {}, {}
        for n in TWIN_WEIGHTS:
            delta_w[n], new_m[n], new_v[n] = _adamw(weights[n], grad_w[n], given["m_" + n], given["v_" + n])
    return (loss, grad_x, *[grad_w[n] for n in TWIN_WEIGHTS], *[delta_w[n] for n in TWIN_WEIGHTS],
            *[new_m[n] for n in TWIN_WEIGHTS], *[new_v[n] for n in TWIN_WEIGHTS])
```

```python
import jax
import jax.numpy as jnp
from jax import lax
from jax.experimental import pallas as pl
from jax.experimental.pallas import tpu as pltpu

F32 = jnp.float32
BF16 = jnp.bfloat16
MESH = pl.DeviceIdType.MESH

N_DEV = 8
CHUNK = 64
POOL_WINDOWS = (2, 4, 8, 16)
POOL_HALO = 16
HEADS = 4
GATE_TEMP = 16.0
RMS_EPS = 1e-6
LANE = 128
V7X_VMEM_BYTES = 64 * 1024 * 1024
VMEM_LIMIT = V7X_VMEM_BYTES - 8 * 1024 * 1024
GLA_ROWS = 4 * CHUNK

ADAM_LR = 0.001
ADAM_B1 = 0.9
ADAM_B2 = 0.999
ADAM_EPS = 1e-08
ADAM_WD = 0.01
ADAM_STEP = 10

_NN = (((1,), (0,)), ((), ()))
_NT = (((1,), (1,)), ((), ()))
_TN = (((0,), (0,)), ((), ()))


def _cparams(sem=None):
    return pltpu.CompilerParams(dimension_semantics=sem, vmem_limit_bytes=VMEM_LIMIT)


def _tile(n, pref, align):
    t = (min(pref, n) // align) * align
    while t >= align:
        if n % t == 0:
            return t
        t -= align
    return n


def _dot(a, b, dims=_NN):
    return lax.dot_general(a.astype(BF16), b.astype(BF16), dims, preferred_element_type=F32)


def _silu_parts(z):
    sig = jax.nn.sigmoid(z)
    return z * sig, sig * (1.0 + z * (1.0 - sig))


def _matmul(name, pairs, a_spec, b_spec, out_shape, out_spec, grid, acc_shape, dims,
            res=None, res_spec=None, scale=None):
    n = len(pairs)
    nk = grid[2]

    def body(*refs):
        a_refs, b_refs = refs[:n], refs[n:2 * n]
        pos = 2 * n
        res_ref = None
        if res is not None:
            res_ref = refs[pos]
            pos += 1
        o_ref, acc = refs[pos], refs[pos + 1]
        k = pl.program_id(2)

        @pl.when(k == 0)
        def _():
            acc[...] = jnp.zeros_like(acc)

        part = None
        for a_ref, b_ref in zip(a_refs, b_refs):
            d = _dot(a_ref[...], b_ref[...], dims)
            part = d if part is None else part + d
        acc[...] += part

        @pl.when(k == nk - 1)
        def _():
            r = acc[...]
            if scale is not None:
                r = r * scale
            if res_ref is not None:
                r = r + res_ref[...]
            o_ref[...] = r.astype(o_ref.dtype)

    ops = [p[0] for p in pairs] + [p[1] for p in pairs]
    specs = [a_spec] * n + [b_spec] * n
    if res is not None:
        ops.append(res)
        specs.append(res_spec)
    return pl.pallas_call(
        body, name=name, grid=grid, in_specs=specs, out_specs=out_spec, out_shape=out_shape,
        scratch_shapes=[pltpu.VMEM(acc_shape, F32)],
        compiler_params=_cparams(("parallel", "parallel", "arbitrary")),
    )(*ops)


def _mm_nn(name, pairs, out_dtype, res=None, tm=512, tn=1024, tk=2048):
    m, kd = pairs[0][0].shape
    nd = pairs[0][1].shape[1]
    tm, tn, tk = _tile(m, tm, 16), _tile(nd, tn, LANE), _tile(kd, tk, LANE)
    return _matmul(
        name, pairs,
        pl.BlockSpec((tm, tk), lambda i, j, k: (i, k)),
        pl.BlockSpec((tk, tn), lambda i, j, k: (k, j)),
        jax.ShapeDtypeStruct((m, nd), out_dtype),
        pl.BlockSpec((tm, tn), lambda i, j, k: (i, j)),
        (m // tm, nd // tn, kd // tk), (tm, tn), _NN,
        res=res, res_spec=pl.BlockSpec((tm, tn), lambda i, j, k: (i, j)))


def _mm_nt(name, pairs, out_dtype, tm=512, tn=1024, tk=2048):
    m, kd = pairs[0][0].shape
    nd = pairs[0][1].shape[0]
    tm, tn, tk = _tile(m, tm, 16), _tile(nd, tn, LANE), _tile(kd, tk, LANE)
    return _matmul(
        name, pairs,
        pl.BlockSpec((tm, tk), lambda i, j, k: (i, k)),
        pl.BlockSpec((tn, tk), lambda i, j, k: (j, k)),
        jax.ShapeDtypeStruct((m, nd), out_dtype),
        pl.BlockSpec((tm, tn), lambda i, j, k: (i, j)),
        (m // tm, nd // tn, kd // tk), (tm, tn), _NT)


def _mm_tn(name, pairs, out_dtype, tm=512, tn=1024, tk=1024):
    kd, m = pairs[0][0].shape
    nd = pairs[0][1].shape[1]
    tm, tn, tk = _tile(m, tm, LANE), _tile(nd, tn, LANE), _tile(kd, tk, 16)
    return _matmul(
        name, pairs,
        pl.BlockSpec((tk, tm), lambda i, j, k: (k, i)),
        pl.BlockSpec((tk, tn), lambda i, j, k: (k, j)),
        jax.ShapeDtypeStruct((m, nd), out_dtype),
        pl.BlockSpec((tm, tn), lambda i, j, k: (i, j)),
        (m // tm, nd // tn, kd // tk), (tm, tn), _TN)


def _mm_tn_blocked_a(name, a3, b, scale):
    nb, n, fb = a3.shape
    d = b.shape[1]
    tn, tk = _tile(d, 1024, LANE), _tile(n, 1024, 16)
    return _matmul(
        name, [(a3, b)],
        pl.BlockSpec((None, tk, fb), lambda i, j, k: (i, k, 0)),
        pl.BlockSpec((tk, tn), lambda i, j, k: (k, j)),
        jax.ShapeDtypeStruct((nb, fb, d), BF16),
        pl.BlockSpec((None, fb, tn), lambda i, j, k: (i, 0, j)),
        (nb, d // tn, n // tk), (fb, tn), _TN, scale=scale)


def _mm_tn_blocked_b(name, a, b3):
    n, d = a.shape
    nb, _, fb = b3.shape
    tm, tk = _tile(d, 512, LANE), _tile(n, 1024, 16)
    return _matmul(
        name, [(a, b3)],
        pl.BlockSpec((tk, tm), lambda i, j, k: (k, i)),
        pl.BlockSpec((None, tk, fb), lambda i, j, k: (j, k, 0)),
        jax.ShapeDtypeStruct((nb, d, fb), BF16),
        pl.BlockSpec((None, tm, fb), lambda i, j, k: (j, i, 0)),
        (d // tm, nb, n // tk), (tm, fb), _TN)


def _mm_nt_blocked(name, pairs):
    nb, n, fb = pairs[0][0].shape
    d = pairs[0][1].shape[1]
    tm, tn = _tile(n, 512, 16), _tile(d, 1024, LANE)
    return _matmul(
        name, pairs,
        pl.BlockSpec((None, tm, fb), lambda i, j, k: (k, i, 0)),
        pl.BlockSpec((None, tn, fb), lambda i, j, k: (k, j, 0)),
        jax.ShapeDtypeStruct((n, d), F32),
        pl.BlockSpec((tm, tn), lambda i, j, k: (i, j)),
        (n // tm, d // tn, nb), (tm, tn), _NT)


def _rms_fwd(name, x, gain):
    m, d = x.shape
    tm = _tile(m, 512, 16)

    def body(x_ref, g_ref, o_ref):
        xf = x_ref[...]
        r = lax.rsqrt(jnp.mean(xf * xf, axis=-1, keepdims=True) + RMS_EPS)
        o_ref[...] = ((xf * r) * g_ref[...]).astype(o_ref.dtype)

    return pl.pallas_call(
        body, name=name, grid=(m // tm,),
        in_specs=[pl.BlockSpec((tm, d), lambda i: (i, 0)), pl.BlockSpec((1, d), lambda i: (0, 0))],
        out_specs=pl.BlockSpec((tm, d), lambda i: (i, 0)),
        out_shape=jax.ShapeDtypeStruct((m, d), BF16),
        compiler_params=_cparams(("parallel",)),
    )(x, gain)


def _rms_bwd(name, x, gain, dh, dres=None):
    m, d = x.shape
    tm = _tile(m, 512, 16)
    has_res = dres is not None

    def body(*refs):
        x_ref, g_ref, dh_ref = refs[:3]
        pos = 3
        res_ref = None
        if has_res:
            res_ref = refs[pos]
            pos += 1
        dx_ref, dg_ref = refs[pos], refs[pos + 1]
        i = pl.program_id(0)

        @pl.when(i == 0)
        def _():
            dg_ref[...] = jnp.zeros_like(dg_ref)

        xf = x_ref[...]
        r = lax.rsqrt(jnp.mean(xf * xf, axis=-1, keepdims=True) + RMS_EPS)
        xh = xf * r
        dhf = dh_ref[...].astype(F32)
        t = dhf * g_ref[...]
        dx = r * (t - xh * jnp.mean(t * xh, axis=-1, keepdims=True))
        if res_ref is not None:
            dx = dx + res_ref[...]
        dx_ref[...] = dx
        dg_ref[...] += jnp.sum(dhf * xh, axis=0, keepdims=True)

    row = pl.BlockSpec((tm, d), lambda i: (i, 0))
    vec = pl.BlockSpec((1, d), lambda i: (0, 0))
    ops = [x, gain, dh] + ([dres] if has_res else [])
    return pl.pallas_call(
        body, name=name, grid=(m // tm,),
        in_specs=[row, vec, row] + ([row] if has_res else []),
        out_specs=[row, vec],
        out_shape=[jax.ShapeDtypeStruct((m, d), F32), jax.ShapeDtypeStruct((1, d), F32)],
        compiler_params=_cparams(("arbitrary",)),
    )(*ops)


def _loss_head(x, target, gain):
    m, d = x.shape
    tm = _tile(m, 512, 16)

    def body(x_ref, t_ref, g_ref, dx_ref, dg_ref, loss_ref):
        i = pl.program_id(0)

        @pl.when(i == 0)
        def _():
            dg_ref[...] = jnp.zeros_like(dg_ref)
            loss_ref[...] = jnp.zeros_like(loss_ref)

        xf = x_ref[...]
        r = lax.rsqrt(jnp.mean(xf * xf, axis=-1, keepdims=True) + RMS_EPS)
        xh = xf * r
        g = g_ref[...]
        err = xh * g - t_ref[...]
        loss_ref[...] += jnp.full(loss_ref.shape, (0.5 / d) * jnp.sum(err * err), F32)
        dy = err * (1.0 / d)
        t = dy * g
        dx_ref[...] = r * (t - xh * jnp.mean(t * xh, axis=-1, keepdims=True))
        dg_ref[...] += jnp.sum(dy * xh, axis=0, keepdims=True)

    row = pl.BlockSpec((tm, d), lambda i: (i, 0))
    vec = pl.BlockSpec((1, d), lambda i: (0, 0))
    return pl.pallas_call(
        body, name="loss_head", grid=(m // tm,),
        in_specs=[row, row, vec],
        out_specs=[row, vec, pl.BlockSpec((1, LANE), lambda i: (0, 0))],
        out_shape=[jax.ShapeDtypeStruct((m, d), F32), jax.ShapeDtypeStruct((1, d), F32),
                   jax.ShapeDtypeStruct((1, LANE), F32)],
        compiler_params=_cparams(("arbitrary",)),
    )(x, target, gain)


def _ffn_fwd(name, x, h, wg, wu, wd):
    n, d = x.shape
    nb, _, fb = wg.shape
    tm = _tile(n, 512, 16)

    def body(x_ref, h_ref, wg_ref, wu_ref, wd_ref, xo_ref, a_ref, b_ref, acc):
        j = pl.program_id(1)

        @pl.when(j == 0)
        def _():
            acc[...] = jnp.zeros_like(acc)

        hh = h_ref[...]
        a = _dot(hh, wg_ref[...])
        b = _dot(hh, wu_ref[...])
        a_ref[...] = a.astype(BF16)
        b_ref[...] = b.astype(BF16)
        act = (a * jax.nn.sigmoid(a)) * b
        acc[...] += _dot(act, wd_ref[...])

        @pl.when(j == nb - 1)
        def _():
            xo_ref[...] = x_ref[...] + 0.5 * acc[...]

    row = pl.BlockSpec((tm, d), lambda i, j: (i, 0))
    w_in = pl.BlockSpec((None, d, fb), lambda i, j: (j, 0, 0))
    hid = pl.BlockSpec((None, tm, fb), lambda i, j: (j, i, 0))
    return pl.pallas_call(
        body, name=name, grid=(n // tm, nb),
        in_specs=[row, row, w_in, w_in, pl.BlockSpec((None, fb, d), lambda i, j: (j, 0, 0))],
        out_specs=[row, hid, hid],
        out_shape=[jax.ShapeDtypeStruct((n, d), F32), jax.ShapeDtypeStruct((nb, n, fb), BF16),
                   jax.ShapeDtypeStruct((nb, n, fb), BF16)],
        scratch_shapes=[pltpu.VMEM((tm, d), F32)],
        compiler_params=_cparams(("parallel", "arbitrary")),
    )(x, h, wg, wu, wd)


def _ffn_bwd_act(name, dxo, wd, a, b):
    n, d = dxo.shape
    nb, fb, _ = wd.shape
    tm = _tile(n, 512, 16)

    def body(dx_ref, wd_ref, a_ref, b_ref, da_ref, db_ref, act_ref):
        dact = 0.5 * _dot(dx_ref[...], wd_ref[...], _NT)
        af = a_ref[...].astype(F32)
        bf = b_ref[...].astype(F32)
        sl, dsl = _silu_parts(af)
        act_ref[...] = (sl * bf).astype(BF16)
        db_ref[...] = (dact * sl).astype(BF16)
        da_ref[...] = (dact * bf * dsl).astype(BF16)

    hid = pl.BlockSpec((None, tm, fb), lambda i, j: (j, i, 0))
    shp = jax.ShapeDtypeStruct((nb, n, fb), BF16)
    return pl.pallas_call(
        body, name=name, grid=(n // tm, nb),
        in_specs=[pl.BlockSpec((tm, d), lambda i, j: (i, 0)),
                  pl.BlockSpec((None, fb, d), lambda i, j: (j, 0, 0)), hid, hid],
        out_specs=[hid, hid, hid], out_shape=[shp, shp, shp],
        compiler_params=_cparams(("parallel", "arbitrary")),
    )(dxo, wd, a, b)


def _pool_diff(ext_ref, rows, cols, width, t_idx):
    s = ext_ref[POOL_HALO:POOL_HALO + rows, cols]
    for sft in range(1, width):
        s = s + ext_ref[POOL_HALO - sft:POOL_HALO - sft + rows, cols]
    cnt = jnp.minimum(t_idx + 1, width).astype(F32)
    return s / cnt - ext_ref[POOL_HALO:POOL_HALO + rows, cols]


def _pool_fwd(proj, pool_w, pool_scale):
    n = proj.shape[0]
    dp = pool_scale.shape[1]
    c = dp // len(POOL_WINDOWS)
    tm = _tile(n, 512, POOL_HALO)
    hb = tm // POOL_HALO

    def body(u_ref, halo_ref, pw_ref, sc_ref, y_ref, ext):
        i = pl.program_id(0)
        ext[0:POOL_HALO, :] = jnp.where(i > 0, halo_ref[...], 0.0)
        ext[POOL_HALO:, :] = u_ref[...]
        t_idx = i * tm + lax.broadcasted_iota(jnp.int32, (tm, 1), 0)
        for g, width in enumerate(POOL_WINDOWS):
            cols = slice(g * c, (g + 1) * c)
            dgrp = _pool_diff(ext, tm, cols, width, t_idx)
            y_ref[:, cols] = (_dot(dgrp, pw_ref[g]) * sc_ref[:, cols]).astype(BF16)

    return pl.pallas_call(
        body, name="pool_fwd", grid=(n // tm,),
        in_specs=[pl.BlockSpec((tm, dp), lambda i: (i, 0)),
                  pl.BlockSpec((POOL_HALO, dp), lambda i: (jnp.maximum(i * hb - 1, 0), 0)),
                  pl.BlockSpec((len(POOL_WINDOWS), c, c), lambda i: (0, 0, 0)),
                  pl.BlockSpec((1, dp), lambda i: (0, 0))],
        out_specs=pl.BlockSpec((tm, dp), lambda i: (i, 0)),
        out_shape=jax.ShapeDtypeStruct((n, dp), BF16),
        scratch_shapes=[pltpu.VMEM((tm + POOL_HALO, dp), F32)],
        compiler_params=_cparams(("parallel",)),
    )(proj, proj, pool_w, pool_scale)


def _pool_bwd(proj, dymix, pool_w, pool_scale):
    n = proj.shape[0]
    dp = pool_scale.shape[1]
    ng = len(POOL_WINDOWS)
    c = dp // ng
    tm = _tile(n, 512, POOL_HALO)
    hb = tm // POOL_HALO
    nsteps = n // tm
    last_halo = n // POOL_HALO - 1

    def body(u_ref, halo_ref, dy_ref, dyn_ref, pw_ref, sc_ref, du_ref, dpw_ref, dsc_ref, ext, dyext, e_s):
        i = pl.program_id(0)

        @pl.when(i == 0)
        def _():
            dpw_ref[...] = jnp.zeros_like(dpw_ref)
            dsc_ref[...] = jnp.zeros_like(dsc_ref)

        ext[0:POOL_HALO, :] = jnp.where(i > 0, halo_ref[...], 0.0)
        ext[POOL_HALO:, :] = u_ref[...]
        dyext[0:tm, :] = dy_ref[...]
        dyext[tm:, :] = jnp.where(i < nsteps - 1, dyn_ref[...], 0.0)
        t_idx = i * tm + lax.broadcasted_iota(jnp.int32, (tm, 1), 0)
        te_idx = i * tm + lax.broadcasted_iota(jnp.int32, (tm + POOL_HALO, 1), 0)
        for g, width in enumerate(POOL_WINDOWS):
            cols = slice(g * c, (g + 1) * c)
            dgrp = _pool_diff(ext, tm, cols, width, t_idx)
            w_g = pw_ref[g]
            dys = dyext[:, cols] * sc_ref[:, cols]
            ypre = _dot(dgrp, w_g)
            dsc_ref[:, cols] += jnp.sum(dyext[0:tm, cols] * ypre, axis=0, keepdims=True)
            dpw_ref[g] += _dot(dgrp, dys[0:tm], _TN)
            dd = _dot(dys, w_g, _NT)
            e_s[...] = dd / jnp.minimum(te_idx + 1, width).astype(F32)
            acc = e_s[0:tm, :]
            for sft in range(1, width):
                acc = acc + e_s[sft:sft + tm, :]
            du_ref[:, cols] = (acc - dd[0:tm]).astype(BF16)

    return pl.pallas_call(
        body, name="pool_bwd", grid=(nsteps,),
        in_specs=[pl.BlockSpec((tm, dp), lambda i: (i, 0)),
                  pl.BlockSpec((POOL_HALO, dp), lambda i: (jnp.maximum(i * hb - 1, 0), 0)),
                  pl.BlockSpec((tm, dp), lambda i: (i, 0)),
                  pl.BlockSpec((POOL_HALO, dp), lambda i: (jnp.minimum((i + 1) * hb, last_halo), 0)),
                  pl.BlockSpec((ng, c, c), lambda i: (0, 0, 0)),
                  pl.BlockSpec((1, dp), lambda i: (0, 0))],
        out_specs=[pl.BlockSpec((tm, dp), lambda i: (i, 0)),
                   pl.BlockSpec((ng, c, c), lambda i: (0, 0, 0)),
                   pl.BlockSpec((1, dp), lambda i: (0, 0))],
        out_shape=[jax.ShapeDtypeStruct((n, dp), BF16), jax.ShapeDtypeStruct((ng, c, c), F32),
                   jax.ShapeDtypeStruct((1, dp), F32)],
        scratch_shapes=[pltpu.VMEM((tm + POOL_HALO, dp), F32), pltpu.VMEM((tm + POOL_HALO, dp), F32),
                        pltpu.VMEM((tm + POOL_HALO, c), F32)],
        compiler_params=_cparams(("arbitrary",)),
    )(proj, proj, dymix, dymix, pool_w, pool_scale)


def _chunk_cumsum(x):
    row = lax.broadcasted_iota(jnp.int32, x.shape, 0) % CHUNK
    s = 1
    while s < CHUNK:
        x = x + jnp.where(row >= s, pltpu.roll(x, s, 0), 0.0)
        s *= 2
    return x


def _gate_logits(alr_ref, wa_ref, ba_ref):
    z = _dot(alr_ref[...], wa_ref[...]) + ba_ref[...]
    la = (jnp.minimum(z, 0.0) - jnp.log(1.0 + jnp.exp(-jnp.abs(z)))) * (1.0 / GATE_TEMP)
    return z, la


def _gla_dims(proj, head_norm):
    n, pw = proj.shape
    dv4 = head_norm.shape[1]
    dk4 = dv4 // 2
    return n, pw, dv4, dk4, dk4 // HEADS, dv4 // HEADS


def _gla_in_specs(t, dk4, dv4, rev):
    alr_blk = (2 * dv4 + 2 * dv4) // LANE
    return [pl.BlockSpec((t, dk4), lambda i: (rev(i), 2)),
            pl.BlockSpec((t, dk4), lambda i: (rev(i), 3)),
            pl.BlockSpec((t, dv4), lambda i: (rev(i), 2)),
            pl.BlockSpec((t, dv4), lambda i: (rev(i), 3)),
            pl.BlockSpec((t, LANE), lambda i: (rev(i), alr_blk))]


def _gla_fwd(proj, wa2p, b_a, head_norm):
    n, _, dv4, dk4, dk, dv = _gla_dims(proj, head_norm)
    t = _tile(n, GLA_ROWS, CHUNK)
    nc = t // CHUNK
    qscale = dk ** -0.5

    def body(q_ref, k_ref, v_ref, g_ref, alr_ref, wa_ref, ba_ref, hn_ref, y_ref, st_ref, state, cum_s):
        i = pl.program_id(0)

        @pl.when(i == 0)
        def _():
            state[...] = jnp.zeros_like(state)

        _, la = _gate_logits(alr_ref, wa_ref, ba_ref)
        cum_s[...] = _chunk_cumsum(la)
        for c in range(nc):
            rows = slice(c * CHUNK, (c + 1) * CHUNK)
            bend = cum_s[c * CHUNK + CHUNK - 1:(c + 1) * CHUNK, :]
            dec = jnp.exp(bend - cum_s[rows, :])
            e = jnp.exp(bend)
            kd = (k_ref[rows, :] * dec).astype(BF16)
            qs = (q_ref[rows, :] * qscale).astype(BF16)
            vv = v_ref[rows, :].astype(BF16)
            for h in range(HEADS):
                hk = slice(h * dk, (h + 1) * dk)
                hv = slice(h * dv, (h + 1) * dv)
                s_new = state[h] * e[:, hk] + _dot(vv[:, hv], kd[:, hk], _TN)
                state[h] = s_new
                st_ref[c, h] = s_new
                o = _dot(qs[:, hk], s_new, _NT)
                r = lax.rsqrt(jnp.mean(o * o, axis=-1, keepdims=True) + RMS_EPS)
                gg = g_ref[rows, hv]
                y_ref[rows, hv] = (((o * r) * hn_ref[:, hv]) * (gg * jax.nn.sigmoid(gg))).astype(BF16)

    full = lambda shape: pl.BlockSpec(shape, lambda i: tuple(0 for _ in shape))
    return pl.pallas_call(
        body, name="gla_fwd", grid=(n // t,),
        in_specs=_gla_in_specs(t, dk4, dv4, lambda i: i)
        + [full((LANE, dk4)), full((1, dk4)), full((1, dv4))],
        out_specs=[pl.BlockSpec((t, dv4), lambda i: (i, 0)),
                   pl.BlockSpec((nc, HEADS, dv, dk), lambda i: (i, 0, 0, 0))],
        out_shape=[jax.ShapeDtypeStruct((n, dv4), BF16),
                   jax.ShapeDtypeStruct((n // CHUNK, HEADS, dv, dk), F32)],
        scratch_shapes=[pltpu.VMEM((HEADS, dv, dk), F32), pltpu.VMEM((t, dk4), F32)],
        compiler_params=_cparams(("arbitrary",)),
    )(proj, proj, proj, proj, proj, wa2p, b_a, head_norm)


def _gla_bwd(proj, dymix, du, states, wa2p, b_a, head_norm):
    n, pw, dv4, dk4, dk, dv = _gla_dims(proj, head_norm)
    t = _tile(n, GLA_ROWS, CHUNK)
    nc = t // CHUNK
    nsteps = n // t
    qscale = dk ** -0.5
    rev = lambda i: nsteps - 1 - i

    def body(q_ref, k_ref, v_ref, g_ref, alr_ref, dy_ref, du_ref, st_ref, prev_ref, wa_ref, ba_ref, hn_ref,
             dp_ref, dwa_ref, dba_ref, dhn_ref, carry, cum_s, dz_s):
        i = pl.program_id(0)

        @pl.when(i == 0)
        def _():
            carry[...] = jnp.zeros_like(carry)
            dwa_ref[...] = jnp.zeros_like(dwa_ref)
            dba_ref[...] = jnp.zeros_like(dba_ref)
            dhn_ref[...] = jnp.zeros_like(dhn_ref)

        first_step = i == nsteps - 1
        z, la = _gate_logits(alr_ref, wa_ref, ba_ref)
        one_minus_sig = 1.0 - jax.nn.sigmoid(z)
        cum_s[...] = _chunk_cumsum(la)
        dp_ref[:, 0:dv4] = du_ref[...]
        for c in reversed(range(nc)):
            rows = slice(c * CHUNK, (c + 1) * CHUNK)
            bend = cum_s[c * CHUNK + CHUNK - 1:(c + 1) * CHUNK, :]
            dec = jnp.exp(bend - cum_s[rows, :])
            e = jnp.exp(bend)
            kd_f = k_ref[rows, :] * dec
            kd = kd_f.astype(BF16)
            qs = (q_ref[rows, :] * qscale).astype(BF16)
            vv = v_ref[rows, :].astype(BF16)
            dkd_parts, dee_parts = [], []
            for h in range(HEADS):
                hk = slice(h * dk, (h + 1) * dk)
                hv = slice(h * dv, (h + 1) * dv)
                s_n = st_ref[c, h]
                if c > 0:
                    s_prev = st_ref[c - 1, h]
                else:
                    s_prev = jnp.where(first_step, 0.0, prev_ref[0, h])
                o = _dot(qs[:, hk], s_n, _NT)
                r = lax.rsqrt(jnp.mean(o * o, axis=-1, keepdims=True) + RMS_EPS)
                oh = o * r
                gg = g_ref[rows, hv]
                sl, dsl = _silu_parts(gg)
                dyh = dy_ref[rows, hv]
                hn = hn_ref[:, hv]
                tt = dyh * sl
                dhn_ref[:, hv] += jnp.sum(tt * oh, axis=0, keepdims=True)
                dg = dyh * (oh * hn) * dsl
                tt = tt * hn
                do = (r * (tt - oh * jnp.mean(tt * oh, axis=-1, keepdims=True))).astype(BF16)
                g_n = carry[h] + _dot(do, qs[:, hk], _TN)
                dq = _dot(do, s_n) * qscale
                dkd = _dot(vv[:, hv], g_n)
                dvh = _dot(kd[:, hk], g_n, _NT)
                de = jnp.sum(g_n * s_prev, axis=0, keepdims=True)
                carry[h] = g_n * e[:, hk]
                dp_ref[rows, dv4 + h * dk:dv4 + (h + 1) * dk] = dq.astype(BF16)
                dp_ref[rows, 2 * dv4 + h * dv:2 * dv4 + (h + 1) * dv] = dvh.astype(BF16)
                dp_ref[rows, 3 * dv4 + h * dv:3 * dv4 + (h + 1) * dv] = dg.astype(BF16)
                dkd_parts.append(dkd)
                dee_parts.append(de * e[:, hk])
            dkd_c = jnp.concatenate(dkd_parts, axis=1)
            dee = jnp.concatenate(dee_parts, axis=1)
            dp_ref[rows, dv4 + dk4:dv4 + 2 * dk4] = (dkd_c * dec).astype(BF16)
            w = dkd_c * kd_f
            dla = (_chunk_cumsum(w) - w) + dee
            dz_s[rows, :] = dla * (1.0 / GATE_TEMP)
        dz = dz_s[...] * one_minus_sig
        dp_ref[:, 4 * dv4:4 * dv4 + LANE] = _dot(dz, wa_ref[...], _NT).astype(BF16)
        dwa_ref[...] += _dot(alr_ref[...], dz, _TN)
        dba_ref[...] += jnp.sum(dz, axis=0, keepdims=True)

    full = lambda shape: pl.BlockSpec(shape, lambda i: tuple(0 for _ in shape))
    return pl.pallas_call(
        body, name="gla_bwd", grid=(nsteps,),
        in_specs=_gla_in_specs(t, dk4, dv4, rev)
        + [pl.BlockSpec((t, dv4), lambda i: (rev(i), 1)),
           pl.BlockSpec((t, dv4), lambda i: (rev(i), 0)),
           pl.BlockSpec((nc, HEADS, dv, dk), lambda i: (rev(i), 0, 0, 0)),
           pl.BlockSpec((1, HEADS, dv, dk), lambda i: (jnp.maximum(rev(i) * nc - 1, 0), 0, 0, 0)),
           full((LANE, dk4)), full((1, dk4)), full((1, dv4))],
        out_specs=[pl.BlockSpec((t, pw), lambda i: (rev(i), 0)),
                   full((LANE, dk4)), full((1, dk4)), full((1, dv4))],
        out_shape=[jax.ShapeDtypeStruct((n, pw), BF16), jax.ShapeDtypeStruct((LANE, dk4), F32),
                   jax.ShapeDtypeStruct((1, dk4), F32), jax.ShapeDtypeStruct((1, dv4), F32)],
        scratch_shapes=[pltpu.VMEM((HEADS, dv, dk), F32), pltpu.VMEM((t, dk4), F32),
                        pltpu.VMEM((t, dk4), F32)],
        compiler_params=_cparams(("arbitrary",)),
    )(proj, proj, proj, proj, proj, dymix, du, states, states, wa2p, b_a, head_norm)


def _softmax_rows(q, k, scale):
    s = _dot(q, k, _NT) * scale
    p = jnp.exp(s - jnp.max(s, axis=-1, keepdims=True))
    return p / jnp.sum(p, axis=-1, keepdims=True)


def _attn_fwd(q, kv):
    n, d = q.shape
    m = kv.shape[0]
    dh = d // HEADS
    tm = _tile(n, 512, 16)
    scale = dh ** -0.5

    def body(q_ref, k_ref, v_ref, o_ref):
        for h in range(HEADS):
            hs = slice(h * dh, (h + 1) * dh)
            p = _softmax_rows(q_ref[:, hs], k_ref[:, hs], scale)
            o_ref[:, hs] = _dot(p, v_ref[:, hs]).astype(BF16)

    return pl.pallas_call(
        body, name="attn_fwd", grid=(n // tm,),
        in_specs=[pl.BlockSpec((tm, d), lambda i: (i, 0)), pl.BlockSpec((m, d), lambda i: (0, 0)),
                  pl.BlockSpec((m, d), lambda i: (0, 1))],
        out_specs=pl.BlockSpec((tm, d), lambda i: (i, 0)),
        out_shape=jax.ShapeDtypeStruct((n, d), BF16),
        compiler_params=_cparams(("parallel",)),
    )(q, kv, kv)


def _attn_bwd(q, kv, do):
    n, d = q.shape
    m = kv.shape[0]
    dh = d // HEADS
    tm = _tile(n, 512, 16)
    scale = dh ** -0.5

    def body(q_ref, k_ref, v_ref, do_ref, dq_ref, dk_ref, dv_ref):
        i = pl.program_id(0)

        @pl.when(i == 0)
        def _():
            dk_ref[...] = jnp.zeros_like(dk_ref)
            dv_ref[...] = jnp.zeros_like(dv_ref)

        for h in range(HEADS):
            hs = slice(h * dh, (h + 1) * dh)
            qh, kh, vh, doh = q_ref[:, hs], k_ref[:, hs], v_ref[:, hs], do_ref[:, hs]
            p = _softmax_rows(qh, kh, scale)
            dv_ref[:, hs] += _dot(p, doh, _TN)
            dp = _dot(doh, vh, _NT)
            ds = p * (dp - jnp.sum(dp * p, axis=-1, keepdims=True)) * scale
            dq_ref[:, hs] = _dot(ds, kh).astype(BF16)
            dk_ref[:, hs] += _dot(ds, qh, _TN)

    row = pl.BlockSpec((tm, d), lambda i: (i, 0))
    memb = pl.BlockSpec((m, d), lambda i: (0, 0))
    return pl.pallas_call(
        body, name="attn_bwd", grid=(n // tm,),
        in_specs=[row, memb, pl.BlockSpec((m, d), lambda i: (0, 1)), row],
        out_specs=[row, memb, memb],
        out_shape=[jax.ShapeDtypeStruct((n, d), BF16), jax.ShapeDtypeStruct((m, d), F32),
                   jax.ShapeDtypeStruct((m, d), F32)],
        compiler_params=_cparams(("arbitrary",)),
    )(q, kv, kv, do)


def _adamw(name, w, m, v, parts):
    r, c = w.shape
    tr = _tile(r, max(16, (256 * 1024) // c), 16)
    row = pl.BlockSpec((tr, c), lambda i: (i, 0))
    ops, specs = [w, m, v], [row, row, row]
    for p in parts:
        if p.ndim == 2:
            ops.append(p)
            specs.append(row)
        else:
            for s in range(p.shape[0]):
                ops.append(p)
                specs.append(pl.BlockSpec((None, tr, c), lambda i, s=s: (s, i, 0)))
    n_parts = len(ops) - 3
    c1 = 1.0 - ADAM_B1 ** ADAM_STEP
    c2 = 1.0 - ADAM_B2 ** ADAM_STEP

    def body(*refs):
        w_ref, m_ref, v_ref = refs[:3]
        g_refs = refs[3:3 + n_parts]
        go_ref, d_ref, mo_ref, vo_ref = refs[3 + n_parts:]
        g = g_refs[0][...].astype(F32)
        for g_ref in g_refs[1:]:
            g = g + g_ref[...].astype(F32)
        m_new = ADAM_B1 * m_ref[...] + (1.0 - ADAM_B1) * g
        v_new = ADAM_B2 * v_ref[...] + (1.0 - ADAM_B2) * (g * g)
        m_hat = m_new / c1
        v_hat = v_new / c2
        go_ref[...] = g
        d_ref[...] = -ADAM_LR * (m_hat / (jnp.sqrt(v_hat) + ADAM_EPS) + ADAM_WD * w_ref[...])
        mo_ref[...] = m_new
        vo_ref[...] = v_new

    shp = jax.ShapeDtypeStruct((r, c), F32)
    return pl.pallas_call(
        body, name=name, grid=(r // tr,), in_specs=specs, out_specs=[row] * 4, out_shape=[shp] * 4,
        compiler_params=_cparams(("parallel",)),
    )(*ops)


def _pair_add(name, a, b):
    k, r, c = a.shape
    tr = _tile(r, max(16, (256 * 1024) // c), 16)

    def body(a_ref, b_ref, o_ref):
        o_ref[...] = (a_ref[...].astype(F32) + b_ref[...].astype(F32)).astype(o_ref.dtype)

    blk = pl.BlockSpec((None, tr, c), lambda s, i: (s, i, 0))
    return pl.pallas_call(
        body, name=name, grid=(k, r // tr), in_specs=[blk, blk], out_specs=blk,
        out_shape=jax.ShapeDtypeStruct(a.shape, a.dtype),
        compiler_params=_cparams(("parallel", "parallel")),
    )(a, b)


def _position():
    return lax.axis_index("x"), lax.axis_index("y"), lax.axis_index("c")


_HBM = pl.BlockSpec(memory_space=pltpu.HBM)


def _all_gather(name, shards):
    n = len(shards)

    def body(*refs):
        ins, outs = refs[:n], refs[n:2 * n]
        send_sems, recv_sems, local_sems = refs[2 * n:]
        x, y, c = _position()
        me, sibling = (x, y, c), (x, y, 1 - c)
        chips = [(1 - x, y), (x, 1 - y), (1 - x, 1 - y)]

        def block(ref, px, py, pc):
            return ref.at[4 * px + 2 * py + pc]

        def copy(a, k, owner, to, src=None):
            return pltpu.make_async_remote_copy(
                src_ref=block(outs[a], *owner) if src is None else src, dst_ref=block(outs[a], *owner),
                send_sem=send_sems.at[7 * a + k], recv_sem=recv_sems.at[7 * a + k],
                device_id=to, device_id_type=MESH)

        started = []
        for a in range(n):
            mine = pltpu.make_async_copy(ins[a], block(outs[a], *me), local_sems.at[a])
            mine.start()
            started.append(mine)
        sends = []
        for a in range(n):
            sends.append(copy(a, 0, me, sibling, src=ins[a]))
            sends += [copy(a, 1 + j, me, (*chip, c), src=ins[a]) for j, chip in enumerate(chips)]
        for cp in sends:
            cp.start()
        for j, chip in enumerate(chips):
            for a in range(n):
                copy(a, 1 + j, (*chip, c), me).wait_recv()
                fwd = copy(a, 4 + j, (*chip, c), sibling)
                fwd.start()
                sends.append(fwd)
        for a in range(n):
            copy(a, 0, sibling, me).wait_recv()
            for j, chip in enumerate(chips):
                copy(a, 4 + j, (*chip, 1 - c), me).wait_recv()
        for cp in sends:
            cp.wait_send()
        for mine in started:
            mine.wait()

    return pl.pallas_call(
        body, name=name,
        in_specs=[_HBM] * n, out_specs=[_HBM] * n,
        out_shape=[jax.ShapeDtypeStruct((N_DEV,) + s.shape, s.dtype) for s in shards],
        scratch_shapes=[pltpu.SemaphoreType.DMA((7 * n,)), pltpu.SemaphoreType.DMA((7 * n,)),
                        pltpu.SemaphoreType.DMA((n,))],
    )(*shards)


def _reduce_scatter_pair(name, grads):
    n = len(grads)

    def body(*refs):
        ins, recvd, kept = refs[:n], refs[n:2 * n], refs[2 * n:3 * n]
        send_sems, recv_sems, local_sems = refs[3 * n:]
        x, y, c = _position()
        for a in range(n):
            for chip in range(4):
                pltpu.make_async_remote_copy(
                    src_ref=ins[a].at[2 * chip + (1 - c)], dst_ref=recvd[a].at[chip],
                    send_sem=send_sems.at[a], recv_sem=recv_sems.at[a],
                    device_id=(x, y, 1 - c), device_id_type=MESH).start()
                pltpu.make_async_copy(ins[a].at[2 * chip + c], kept[a].at[chip], local_sems.at[a]).start()
        for a in range(n):
            pltpu.make_async_remote_copy(
                src_ref=recvd[a], dst_ref=recvd[a], send_sem=send_sems.at[a], recv_sem=recv_sems.at[a],
                device_id=(x, y, 1 - c), device_id_type=MESH).wait()
            pltpu.make_async_copy(kept[a], kept[a], local_sems.at[a]).wait()

    half = [jax.ShapeDtypeStruct((4,) + g.shape[1:], g.dtype) for g in grads]
    outs = pl.pallas_call(
        body, name=name, in_specs=[_HBM] * n, out_specs=[_HBM] * (2 * n), out_shape=half + half,
        scratch_shapes=[pltpu.SemaphoreType.DMA((n,)), pltpu.SemaphoreType.DMA((n,)),
                        pltpu.SemaphoreType.DMA((n,))],
    )(*grads)
    return outs[:n], outs[n:]


def _reduce_scatter_chips(name, sums):
    n = len(sums)
    offsets = [(1, 0), (0, 1), (1, 1)]

    def body(*refs):
        ins, recvd, own = refs[:n], refs[n:2 * n], refs[2 * n:3 * n]
        send_sems, recv_sems, local_sems = refs[3 * n:]
        x, y, c = _position()
        for a in range(n):
            for r, (ox, oy) in enumerate(offsets):
                px = 1 - x if ox else x
                py = 1 - y if oy else y
                pltpu.make_async_remote_copy(
                    src_ref=ins[a].at[2 * px + py], dst_ref=recvd[a].at[r],
                    send_sem=send_sems.at[a], recv_sem=recv_sems.at[a],
                    device_id=(px, py, c), device_id_type=MESH).start()
            pltpu.make_async_copy(ins[a].at[2 * x + y], own[a], local_sems.at[a]).start()
        for a in range(n):
            pltpu.make_async_remote_copy(
                src_ref=recvd[a], dst_ref=recvd[a], send_sem=send_sems.at[a], recv_sem=recv_sems.at[a],
                device_id=(x, y, c), device_id_type=MESH).wait()
            pltpu.make_async_copy(own[a], own[a], local_sems.at[a]).wait()

    outs = pl.pallas_call(
        body, name=name, in_specs=[_HBM] * n, out_specs=[_HBM] * (2 * n),
        out_shape=[jax.ShapeDtypeStruct((3,) + s.shape[1:], s.dtype) for s in sums]
        + [jax.ShapeDtypeStruct(s.shape[1:], s.dtype) for s in sums],
        scratch_shapes=[pltpu.SemaphoreType.DMA((n,)), pltpu.SemaphoreType.DMA((n,)),
                        pltpu.SemaphoreType.DMA((n,))],
    )(*sums)
    return outs[:n], outs[n:]


def _all_reduce_small(vec):
    r = vec.shape[0]

    def body(v_ref, o_ref, gbuf, send_sems, recv_sems):
        x, y, c = _position()
        me = 4 * x + 2 * y + c
        gbuf[me] = v_ref[...]
        copies = []
        for k in range(1, N_DEV):
            ox, oy, oc = (k >> 2) & 1, (k >> 1) & 1, k & 1
            peer = (1 - x if ox else x, 1 - y if oy else y, 1 - c if oc else c)
            cp = pltpu.make_async_remote_copy(
                src_ref=gbuf.at[me], dst_ref=gbuf.at[me], send_sem=send_sems.at[k - 1],
                recv_sem=recv_sems.at[k - 1], device_id=peer, device_id_type=MESH)
            cp.start()
            copies.append(cp)
        for cp in copies:
            cp.wait()
        total = gbuf[0]
        for j in range(1, N_DEV):
            total = total + gbuf[j]
        o_ref[...] = total

    return pl.pallas_call(
        body, name="all_reduce_small",
        in_specs=[pl.BlockSpec(memory_space=pltpu.VMEM)],
        out_specs=pl.BlockSpec(memory_space=pltpu.VMEM),
        out_shape=jax.ShapeDtypeStruct(vec.shape, F32),
        scratch_shapes=[pltpu.VMEM((N_DEV, r, LANE), F32), pltpu.SemaphoreType.DMA((N_DEV - 1,)),
                        pltpu.SemaphoreType.DMA((N_DEV - 1,))],
    )(vec)


def _cols_from_blocks(g):
    nb, r, cs = g.shape
    return jnp.transpose(g, (1, 0, 2)).reshape(r, nb * cs)


def _cols_to_blocks(w):
    r, cfull = w.shape
    return jnp.transpose(w.reshape(r, N_DEV, cfull // N_DEV), (1, 0, 2))


def kernel(x, mem, ffn1_norm, ffn1_w_gate, ffn1_w_up, ffn1_w_down, mix_norm, w_in, pool_w, pool_scale, gla_w_a2, gla_b_a, gla_head_norm, w_out, xattn_norm, mem_norm, xattn_w_q, xattn_w_kv, xattn_w_o, ffn2_norm, ffn2_w_gate, ffn2_w_up, ffn2_w_down, final_norm, loss_target, m_ffn1_norm, m_ffn1_w_gate, m_ffn1_w_up, m_ffn1_w_down, m_mix_norm, m_w_in, m_pool_w, m_pool_scale, m_gla_w_a2, m_gla_b_a, m_gla_head_norm, m_w_out, m_xattn_norm, m_mem_norm, m_xattn_w_q, m_xattn_w_kv, m_xattn_w_o, m_ffn2_norm, m_ffn2_w_gate, m_ffn2_w_up, m_ffn2_w_down, m_final_norm, v_ffn1_norm, v_ffn1_w_gate, v_ffn1_w_up, v_ffn1_w_down, v_mix_norm, v_w_in, v_pool_w, v_pool_scale, v_gla_w_a2, v_gla_b_a, v_gla_head_norm, v_w_out, v_xattn_norm, v_mem_norm, v_xattn_w_q, v_xattn_w_kv, v_xattn_w_o, v_ffn2_norm, v_ffn2_w_gate, v_ffn2_w_up, v_ffn2_w_down, v_final_norm):
    weights = dict(ffn1_norm=ffn1_norm, ffn1_w_gate=ffn1_w_gate, ffn1_w_up=ffn1_w_up, ffn1_w_down=ffn1_w_down, mix_norm=mix_norm, w_in=w_in, pool_w=pool_w, pool_scale=pool_scale, gla_w_a2=gla_w_a2, gla_b_a=gla_b_a, gla_head_norm=gla_head_norm, w_out=w_out, xattn_norm=xattn_norm, mem_norm=mem_norm, xattn_w_q=xattn_w_q, xattn_w_kv=xattn_w_kv, xattn_w_o=xattn_w_o, ffn2_norm=ffn2_norm, ffn2_w_gate=ffn2_w_gate, ffn2_w_up=ffn2_w_up, ffn2_w_down=ffn2_w_down, final_norm=final_norm)
    mom1 = dict(ffn1_norm=m_ffn1_norm, ffn1_w_gate=m_ffn1_w_gate, ffn1_w_up=m_ffn1_w_up, ffn1_w_down=m_ffn1_w_down, mix_norm=m_mix_norm, w_in=m_w_in, pool_w=m_pool_w, pool_scale=m_pool_scale, gla_w_a2=m_gla_w_a2, gla_b_a=m_gla_b_a, gla_head_norm=m_gla_head_norm, w_out=m_w_out, xattn_norm=m_xattn_norm, mem_norm=m_mem_norm, xattn_w_q=m_xattn_w_q, xattn_w_kv=m_xattn_w_kv, xattn_w_o=m_xattn_w_o, ffn2_norm=m_ffn2_norm, ffn2_w_gate=m_ffn2_w_gate, ffn2_w_up=m_ffn2_w_up, ffn2_w_down=m_ffn2_w_down, final_norm=m_final_norm)
    mom2 = dict(ffn1_norm=v_ffn1_norm, ffn1_w_gate=v_ffn1_w_gate, ffn1_w_up=v_ffn1_w_up, ffn1_w_down=v_ffn1_w_down, mix_norm=v_mix_norm, w_in=v_w_in, pool_w=v_pool_w, pool_scale=v_pool_scale, gla_w_a2=v_gla_w_a2, gla_b_a=v_gla_b_a, gla_head_norm=v_gla_head_norm, w_out=v_w_out, xattn_norm=v_xattn_norm, mem_norm=v_mem_norm, xattn_w_q=v_xattn_w_q, xattn_w_kv=v_xattn_w_kv, xattn_w_o=v_xattn_w_o, ffn2_norm=v_ffn2_norm, ffn2_w_gate=v_ffn2_w_gate, ffn2_w_up=v_ffn2_w_up, ffn2_w_down=v_ffn2_w_down, final_norm=v_final_norm)
    order = list(weights.keys())

    n, d = x.shape[1], x.shape[2]
    mlen = mem.shape[1]
    x0 = x.reshape(n, d)
    memf = mem.reshape(mlen, d)
    target = loss_target.reshape(n, d)
    dpool = d // 2
    in_cols = w_in.shape[2] * N_DEV
    proj_cols = 2 * d + LANE
    rank = gla_w_a2.shape[1]

    def shard(name):
        return weights[name][0].astype(BF16)

    wg1, wu1, wd1 = _all_gather("ag_ffn1", [shard("ffn1_w_gate"), shard("ffn1_w_up"), shard("ffn1_w_down")])
    win_g, pw_g, wa2_g, wout_g = _all_gather(
        "ag_mix", [shard("w_in"), shard("pool_w"), shard("gla_w_a2"), shard("w_out")])
    wq_g, wkv_g, wo_g = _all_gather("ag_xattn", [shard("xattn_w_q"), shard("xattn_w_kv"), shard("xattn_w_o")])
    wg2, wu2, wd2 = _all_gather("ag_ffn2", [shard("ffn2_w_gate"), shard("ffn2_w_up"), shard("ffn2_w_down")])

    winp = jnp.pad(_cols_from_blocks(win_g), ((0, 0), (0, proj_cols - in_cols)))
    pw = jnp.transpose(pw_g, (1, 0, 2, 3)).reshape(len(POOL_WINDOWS), dpool // 4, dpool // 4)
    wa2p = jnp.pad(_cols_from_blocks(wa2_g), ((0, LANE - rank), (0, 0)))
    wout = wout_g.reshape(d, d)
    wq = wq_g.reshape(d, d)
    wkv = _cols_from_blocks(wkv_g)
    wo = wo_g.reshape(d, d)
    fnorm = final_norm.reshape(1, d)

    h1 = _rms_fwd("rms_ffn1", x0, ffn1_norm)
    x1, a1, b1 = _ffn_fwd("ffn1_fwd", x0, h1, wg1, wu1, wd1)
    h2 = _rms_fwd("rms_mix", x1, mix_norm)
    proj = _mm_nn("mix_in", [(h2, winp)], F32, tn=1408)
    ypool = _pool_fwd(proj, pw, pool_scale)
    ygla, states = _gla_fwd(proj, wa2p, gla_b_a, gla_head_norm)
    x2 = _mm_nn("mix_out", [(ypool, wout[:dpool]), (ygla, wout[dpool:])], F32, res=x1, tk=1024)
    h3 = _rms_fwd("rms_xattn", x2, xattn_norm)
    mh = _rms_fwd("rms_mem", memf, mem_norm)
    q = _mm_nn("xattn_q", [(h3, wq)], BF16)
    kv = _mm_nn("xattn_kv", [(mh, wkv)], BF16)
    att = _attn_fwd(q, kv)
    x3 = _mm_nn("xattn_o", [(att, wo)], F32, res=x2)
    h4 = _rms_fwd("rms_ffn2", x3, ffn2_norm)
    x4, a2, b2 = _ffn_fwd("ffn2_fwd", x3, h4, wg2, wu2, wd2)
    dx4, g_final, loss_part = _loss_head(x4, target, fnorm)

    grads = {}

    da2, db2, act2 = _ffn_bwd_act("ffn2_bwd_act", dx4, wd2, a2, b2)
    grads["ffn2_w_down"] = _mm_tn_blocked_a("ffn2_dwd", act2, dx4, 0.5)
    grads["ffn2_w_gate"] = _mm_tn_blocked_b("ffn2_dwg", h4, da2)
    grads["ffn2_w_up"] = _mm_tn_blocked_b("ffn2_dwu", h4, db2)
    dh4 = _mm_nt_blocked("ffn2_dh", [(da2, wg2), (db2, wu2)])
    dx3, g_ffn2_norm = _rms_bwd("rms_ffn2_bwd", x3, ffn2_norm, dh4, dx4)

    datt = _mm_nt("xattn_do", [(dx3, wo)], BF16)
    grads["xattn_w_o"] = _mm_tn("xattn_dwo", [(att, dx3)], BF16).reshape(N_DEV, d // N_DEV, d)
    dq, dk, dv = _attn_bwd(q, kv, datt)
    dh3 = _mm_nt("xattn_dh", [(dq, wq)], F32)
    grads["xattn_w_q"] = _mm_tn("xattn_dwq", [(h3, dq)], BF16).reshape(N_DEV, d // N_DEV, d)
    dx2, g_xattn_norm = _rms_bwd("rms_xattn_bwd", x2, xattn_norm, dh3, dx3)
    dkv = jnp.concatenate([dk, dv], axis=1)
    dmh = _mm_nt("xattn_dmh", [(dkv, wkv)], F32)
    grads["xattn_w_kv"] = _cols_to_blocks(_mm_tn("xattn_dwkv", [(mh, dkv)], BF16))
    _, g_mem_norm = _rms_bwd("rms_mem_bwd", memf, mem_norm, dmh)

    dymix = _mm_nt("mix_dy", [(dx2, wout)], F32)
    grads["w_out"] = jnp.concatenate(
        [_mm_tn("mix_dwout_pool", [(ypool, dx2)], BF16), _mm_tn("mix_dwout_gla", [(ygla, dx2)], BF16)],
        axis=0).reshape(N_DEV, d // N_DEV, d)
    du, g_pool_w, g_pool_scale = _pool_bwd(proj, dymix, pw, pool_scale)
    dproj, g_wa2p, g_b_a, g_head_norm = _gla_bwd(proj, dymix, du, states, wa2p, gla_b_a, gla_head_norm)
    dh2 = _mm_nt("mix_dh", [(dproj, winp)], F32, tk=1408)
    grads["w_in"] = _cols_to_blocks(_mm_tn("mix_dwin", [(h2, dproj)], BF16, tn=1408)[:, :in_cols])
    grads["pool_w"] = jnp.transpose(
        g_pool_w.reshape(len(POOL_WINDOWS), N_DEV, dpool // 4 // N_DEV, dpool // 4), (1, 0, 2, 3))
    grads["gla_w_a2"] = _cols_to_blocks(g_wa2p[:rank])
    dx1, g_mix_norm = _rms_bwd("rms_mix_bwd", x1, mix_norm, dh2, dx2)

    da1, db1, act1 = _ffn_bwd_act("ffn1_bwd_act", dx1, wd1, a1, b1)
    grads["ffn1_w_down"] = _mm_tn_blocked_a("ffn1_dwd", act1, dx1, 0.5)
    grads["ffn1_w_gate"] = _mm_tn_blocked_b("ffn1_dwg", h1, da1)
    grads["ffn1_w_up"] = _mm_tn_blocked_b("ffn1_dwu", h1, db1)
    dh1 = _mm_nt_blocked("ffn1_dh", [(da1, wg1), (db1, wu1)])
    dx0, g_ffn1_norm = _rms_bwd("rms_ffn1_bwd", x0, ffn1_norm, dh1, dx1)

    small = [("ffn1_norm", g_ffn1_norm), ("mix_norm", g_mix_norm), ("pool_scale", g_pool_scale),
             ("gla_b_a", g_b_a), ("gla_head_norm", g_head_norm), ("xattn_norm", g_xattn_norm),
             ("mem_norm", g_mem_norm), ("ffn2_norm", g_ffn2_norm), ("final_norm", g_final)]
    packed = jnp.concatenate([g.reshape(-1) for _, g in small] + [loss_part.reshape(-1)])
    slab = 8 * LANE
    padded = -(-packed.shape[0] // slab) * slab
    packed = jnp.pad(packed, (0, padded - packed.shape[0])).reshape(padded // LANE, LANE)
    reduced = _all_reduce_small(packed).reshape(-1)
    small_grads = {}
    off = 0
    for name, g in small:
        small_grads[name] = reduced[off:off + g.size]
        off += g.size
    loss = reduced[off]

    sharded = [k for k in order if k not in small_grads]
    gl = []
    for k in sharded:
        g = grads[k]
        gl.append(g.reshape(N_DEV, -1, g.shape[-1]))
    recvd1, kept1 = _reduce_scatter_pair("rs_pair", gl)
    sums = [_pair_add("rs_add_" + k, kept1[i], recvd1[i]) for i, k in enumerate(sharded)]
    recvd2, own2 = _reduce_scatter_chips("rs_chips", sums)

    out_g, out_d, out_m, out_v = {}, {}, {}, {}
    for k in order:
        w = weights[k]
        if k in small_grads:
            w2 = w.reshape(1, -1)
            parts = [small_grads[k].reshape(1, -1)]
        else:
            i = sharded.index(k)
            w2 = w.reshape(-1, w.shape[-1])
            parts = [own2[i], recvd2[i]]
        res = _adamw("adamw_" + k, w2, mom1[k].reshape(w2.shape), mom2[k].reshape(w2.shape), parts)
        out_g[k], out_d[k], out_m[k], out_v[k] = [r.reshape(w.shape) for r in res]

    return (loss, dx0.reshape(x.shape), *[out_g[k] for k in order], *[out_d[k] for k in order],
            *[out_m[k] for k in order], *[out_v[k] for k in order])
```

```python
import jax
import jax.numpy as jnp
from jax import lax
from jax.experimental import pallas as pl
from jax.experimental.pallas import tpu as pltpu

F32 = jnp.float32
BF16 = jnp.bfloat16
MESH = pl.DeviceIdType.MESH

N_DEV = 8
CHUNK = 64
POOL_WINDOWS = (2, 4, 8, 16)
POOL_HALO = 16
HEADS = 4
GATE_TEMP = 16.0
RMS_EPS = 1e-6
LANE = 128
V7X_VMEM_BYTES = 64 * 1024 * 1024
VMEM_LIMIT = V7X_VMEM_BYTES - 8 * 1024 * 1024
GLA_ROWS = 4 * CHUNK

ADAM_LR = 0.001
ADAM_B1 = 0.9
ADAM_B2 = 0.999
ADAM_EPS = 1e-08
ADAM_WD = 0.01
ADAM_STEP = 10

_NN = (((1,), (0,)), ((), ()))
_NT = (((1,), (1,)), ((), ()))
_TN = (((0,), (0,)), ((), ()))


def _cparams(sem=None):
    return pltpu.CompilerParams(dimension_semantics=sem, vmem_limit_bytes=VMEM_LIMIT)


def _tile(n, pref, align):
    t = (min(pref, n) // align) * align
    while t >= align:
        if n % t == 0:
            return t
        t -= align
    return n


def _dot(a, b, dims=_NN):
    return lax.dot_general(a.astype(BF16), b.astype(BF16), dims, preferred_element_type=F32)


def _silu_parts(z):
    sig = jax.nn.sigmoid(z)
    return z * sig, sig * (1.0 + z * (1.0 - sig))


def _matmul(name, pairs, a_spec, b_spec, out_shape, out_spec, grid, acc_shape, dims,
            res=None, res_spec=None, scale=None):
    n = len(pairs)
    nk = grid[2]

    def body(*refs):
        a_refs, b_refs = refs[:n], refs[n:2 * n]
        pos = 2 * n
        res_ref = None
        if res is not None:
            res_ref = refs[pos]
            pos += 1
        o_ref, acc = refs[pos], refs[pos + 1]
        k = pl.program_id(2)

        @pl.when(k == 0)
        def _():
            acc[...] = jnp.zeros_like(acc)

        part = None
        for a_ref, b_ref in zip(a_refs, b_refs):
            d = _dot(a_ref[...], b_ref[...], dims)
            part = d if part is None else part + d
        acc[...] += part

        @pl.when(k == nk - 1)
        def _():
            r = acc[...]
            if scale is not None:
                r = r * scale
            if res_ref is not None:
                r = r + res_ref[...]
            o_ref[...] = r.astype(o_ref.dtype)

    ops = [p[0] for p in pairs] + [p[1] for p in pairs]
    specs = [a_spec] * n + [b_spec] * n
    if res is not None:
        ops.append(res)
        specs.append(res_spec)
    return pl.pallas_call(
        body, name=name, grid=grid, in_specs=specs, out_specs=out_spec, out_shape=out_shape,
        scratch_shapes=[pltpu.VMEM(acc_shape, F32)],
        compiler_params=_cparams(("parallel", "parallel", "arbitrary")),
    )(*ops)


def _mm_nn(name, pairs, out_dtype, res=None, tm=512, tn=1024, tk=2048):
    m, kd = pairs[0][0].shape
    nd = pairs[0][1].shape[1]
    tm, tn, tk = _tile(m, tm, 16), _tile(nd, tn, LANE), _tile(kd, tk, LANE)
    return _matmul(
        name, pairs,
        pl.BlockSpec((tm, tk), lambda i, j, k: (i, k)),
        pl.BlockSpec((tk, tn), lambda i, j, k: (k, j)),
        jax.ShapeDtypeStruct((m, nd), out_dtype),
        pl.BlockSpec((tm, tn), lambda i, j, k: (i, j)),
        (m // tm, nd // tn, kd // tk), (tm, tn), _NN,
        res=res, res_spec=pl.BlockSpec((tm, tn), lambda i, j, k: (i, j)))


def _mm_nt(name, pairs, out_dtype, tm=512, tn=1024, tk=2048):
    m, kd = pairs[0][0].shape
    nd = pairs[0][1].shape[0]
    tm, tn, tk = _tile(m, tm, 16), _tile(nd, tn, LANE), _tile(kd, tk, LANE)
    return _matmul(
        name, pairs,
        pl.BlockSpec((tm, tk), lambda i, j, k: (i, k)),
        pl.BlockSpec((tn, tk), lambda i, j, k: (j, k)),
        jax.ShapeDtypeStruct((m, nd), out_dtype),
        pl.BlockSpec((tm, tn), lambda i, j, k: (i, j)),
        (m // tm, nd // tn, kd // tk), (tm, tn), _NT)


def _mm_tn(name, pairs, out_dtype, tm=512, tn=1024, tk=1024):
    kd, m = pairs[0][0].shape
    nd = pairs[0][1].shape[1]
    tm, tn, tk = _tile(m, tm, LANE), _tile(nd, tn, LANE), _tile(kd, tk, 16)
    return _matmul(
        name, pairs,
        pl.BlockSpec((tk, tm), lambda i, j, k: (k, i)),
        pl.BlockSpec((tk, tn), lambda i, j, k: (k, j)),
        jax.ShapeDtypeStruct((m, nd), out_dtype),
        pl.BlockSpec((tm, tn), lambda i, j, k: (i, j)),
        (m // tm, nd // tn, kd // tk), (tm, tn), _TN)


def _rms_fwd(name, x, gain):
    m, d = x.shape
    tm = _tile(m, 512, 16)

    def body(x_ref, g_ref, o_ref):
        xf = x_ref[...]
        r = lax.rsqrt(jnp.mean(xf * xf, axis=-1, keepdims=True) + RMS_EPS)
        o_ref[...] = ((xf * r) * g_ref[...]).astype(o_ref.dtype)

    return pl.pallas_call(
        body, name=name, grid=(m // tm,),
        in_specs=[pl.BlockSpec((tm, d), lambda i: (i, 0)), pl.BlockSpec((1, d), lambda i: (0, 0))],
        out_specs=pl.BlockSpec((tm, d), lambda i: (i, 0)),
        out_shape=jax.ShapeDtypeStruct((m, d), BF16),
        compiler_params=_cparams(("parallel",)),
    )(x, gain)


def _rms_bwd(name, x, gain, dh, dres=None):
    m, d = x.shape
    tm = _tile(m, 512, 16)
    has_res = dres is not None

    def body(*refs):
        x_ref, g_ref, dh_ref = refs[:3]
        pos = 3
        res_ref = None
        if has_res:
            res_ref = refs[pos]
            pos += 1
        dx_ref, dxb_ref, dg_ref = refs[pos], refs[pos + 1], refs[pos + 2]
        i = pl.program_id(0)

        @pl.when(i == 0)
        def _():
            dg_ref[...] = jnp.zeros_like(dg_ref)

        xf = x_ref[...]
        r = lax.rsqrt(jnp.mean(xf * xf, axis=-1, keepdims=True) + RMS_EPS)
        xh = xf * r
        dhf = dh_ref[...].astype(F32)
        t = dhf * g_ref[...]
        dx = r * (t - xh * jnp.mean(t * xh, axis=-1, keepdims=True))
        if res_ref is not None:
            dx = dx + res_ref[...]
        dx_ref[...] = dx
        dxb_ref[...] = dx.astype(BF16)
        dg_ref[...] += jnp.sum(dhf * xh, axis=0, keepdims=True)

    row = pl.BlockSpec((tm, d), lambda i: (i, 0))
    vec = pl.BlockSpec((1, d), lambda i: (0, 0))
    ops = [x, gain, dh] + ([dres] if has_res else [])
    return pl.pallas_call(
        body, name=name, grid=(m // tm,),
        in_specs=[row, vec, row] + ([row] if has_res else []),
        out_specs=[row, row, vec],
        out_shape=[jax.ShapeDtypeStruct((m, d), F32), jax.ShapeDtypeStruct((m, d), BF16),
                   jax.ShapeDtypeStruct((1, d), F32)],
        compiler_params=_cparams(("arbitrary",)),
    )(*ops)


def _loss_head(x, target, gain):
    m, d = x.shape
    tm = _tile(m, 512, 16)

    def body(x_ref, t_ref, g_ref, dx_ref, dxb_ref, dg_ref, loss_ref):
        i = pl.program_id(0)

        @pl.when(i == 0)
        def _():
            dg_ref[...] = jnp.zeros_like(dg_ref)
            loss_ref[...] = jnp.zeros_like(loss_ref)

        xf = x_ref[...]
        r = lax.rsqrt(jnp.mean(xf * xf, axis=-1, keepdims=True) + RMS_EPS)
        xh = xf * r
        g = g_ref[...]
        err = xh * g - t_ref[...]
        loss_ref[...] += jnp.full(loss_ref.shape, (0.5 / d) * jnp.sum(err * err), F32)
        dy = err * (1.0 / d)
        t = dy * g
        dx = r * (t - xh * jnp.mean(t * xh, axis=-1, keepdims=True))
        dx_ref[...] = dx
        dxb_ref[...] = dx.astype(BF16)
        dg_ref[...] += jnp.sum(dy * xh, axis=0, keepdims=True)

    row = pl.BlockSpec((tm, d), lambda i: (i, 0))
    vec = pl.BlockSpec((1, d), lambda i: (0, 0))
    return pl.pallas_call(
        body, name="loss_head", grid=(m // tm,),
        in_specs=[row, row, vec],
        out_specs=[row, row, vec, pl.BlockSpec((1, LANE), lambda i: (0, 0))],
        out_shape=[jax.ShapeDtypeStruct((m, d), F32), jax.ShapeDtypeStruct((m, d), BF16),
                   jax.ShapeDtypeStruct((1, d), F32), jax.ShapeDtypeStruct((1, LANE), F32)],
        compiler_params=_cparams(("arbitrary",)),
    )(x, target, gain)


def _ffn_fwd(name, x, h, wg, wu, wd):
    n, d = x.shape
    nb, _, fb = wg.shape
    tm = _tile(n, 512, 16)

    def body(x_ref, h_ref, wg_ref, wu_ref, wd_ref, xo_ref, a_ref, b_ref, acc):
        j = pl.program_id(1)

        @pl.when(j == 0)
        def _():
            acc[...] = jnp.zeros_like(acc)

        hh = h_ref[...]
        a = _dot(hh, wg_ref[...])
        b = _dot(hh, wu_ref[...])
        a_ref[...] = a.astype(BF16)
        b_ref[...] = b.astype(BF16)
        act = (a * jax.nn.sigmoid(a)) * b
        acc[...] += _dot(act, wd_ref[...])

        @pl.when(j == nb - 1)
        def _():
            xo_ref[...] = x_ref[...] + 0.5 * acc[...]

    row = pl.BlockSpec((tm, d), lambda i, j: (i, 0))
    w_in = pl.BlockSpec((None, d, fb), lambda i, j: (j, 0, 0))
    hid = pl.BlockSpec((None, tm, fb), lambda i, j: (j, i, 0))
    return pl.pallas_call(
        body, name=name, grid=(n // tm, nb),
        in_specs=[row, row, w_in, w_in, pl.BlockSpec((None, fb, d), lambda i, j: (j, 0, 0))],
        out_specs=[row, hid, hid],
        out_shape=[jax.ShapeDtypeStruct((n, d), F32), jax.ShapeDtypeStruct((nb, n, fb), BF16),
                   jax.ShapeDtypeStruct((nb, n, fb), BF16)],
        scratch_shapes=[pltpu.VMEM((tm, d), F32)],
        compiler_params=_cparams(("parallel", "arbitrary")),
    )(x, h, wg, wu, wd)


def _ffn_bwd_tokens(name, dxb, wg, wu, wd, a, b):
    n, d = dxb.shape
    nb, fb, _ = wd.shape
    tm = _tile(n, 512, 16)

    def body(dx_ref, wg_ref, wu_ref, wd_ref, a_ref, b_ref, da_ref, db_ref, act_ref, dh_ref, acc):
        j = pl.program_id(1)

        @pl.when(j == 0)
        def _():
            acc[...] = jnp.zeros_like(acc)

        dact = 0.5 * _dot(dx_ref[...], wd_ref[...], _NT)
        af = a_ref[...].astype(F32)
        bf = b_ref[...].astype(F32)
        sl, dsl = _silu_parts(af)
        act_ref[...] = (sl * bf).astype(BF16)
        d_b = (dact * sl).astype(BF16)
        d_a = (dact * bf * dsl).astype(BF16)
        db_ref[...] = d_b
        da_ref[...] = d_a
        acc[...] += _dot(d_a, wg_ref[...], _NT) + _dot(d_b, wu_ref[...], _NT)

        @pl.when(j == nb - 1)
        def _():
            dh_ref[...] = acc[...]

    row = pl.BlockSpec((tm, d), lambda i, j: (i, 0))
    w_in = pl.BlockSpec((None, d, fb), lambda i, j: (j, 0, 0))
    hid = pl.BlockSpec((None, tm, fb), lambda i, j: (j, i, 0))
    shp = jax.ShapeDtypeStruct((nb, n, fb), BF16)
    return pl.pallas_call(
        body, name=name, grid=(n // tm, nb),
        in_specs=[row, w_in, w_in, pl.BlockSpec((None, fb, d), lambda i, j: (j, 0, 0)), hid, hid],
        out_specs=[hid, hid, hid, row],
        out_shape=[shp, shp, shp, jax.ShapeDtypeStruct((n, d), F32)],
        scratch_shapes=[pltpu.VMEM((tm, d), F32)],
        compiler_params=_cparams(("parallel", "arbitrary")),
    )(dxb, wg, wu, wd, a, b)


def _ffn_bwd_weights(name, h, dxb, da, db, act):
    n, d = h.shape
    nb, _, fb = da.shape
    tk = _tile(n, 512, 16)
    nk = n // tk

    def body(h_ref, dx_ref, da_ref, db_ref, act_ref, dwg_ref, dwu_ref, dwd_ref, accg, accu, accd):
        k = pl.program_id(1)

        @pl.when(k == 0)
        def _():
            accg[...] = jnp.zeros_like(accg)
            accu[...] = jnp.zeros_like(accu)
            accd[...] = jnp.zeros_like(accd)

        hh = h_ref[...]
        accg[...] += _dot(hh, da_ref[...], _TN)
        accu[...] += _dot(hh, db_ref[...], _TN)
        accd[...] += _dot(act_ref[...], dx_ref[...], _TN)

        @pl.when(k == nk - 1)
        def _():
            dwg_ref[...] = accg[...].astype(BF16)
            dwu_ref[...] = accu[...].astype(BF16)
            dwd_ref[...] = (0.5 * accd[...]).astype(BF16)

    row = pl.BlockSpec((tk, d), lambda j, k: (k, 0))
    hid = pl.BlockSpec((None, tk, fb), lambda j, k: (j, k, 0))
    w_in = pl.BlockSpec((None, d, fb), lambda j, k: (j, 0, 0))
    w_out = pl.BlockSpec((None, fb, d), lambda j, k: (j, 0, 0))
    return pl.pallas_call(
        body, name=name, grid=(nb, nk),
        in_specs=[row, row, hid, hid, hid], out_specs=[w_in, w_in, w_out],
        out_shape=[jax.ShapeDtypeStruct((nb, d, fb), BF16), jax.ShapeDtypeStruct((nb, d, fb), BF16),
                   jax.ShapeDtypeStruct((nb, fb, d), BF16)],
        scratch_shapes=[pltpu.VMEM((d, fb), F32), pltpu.VMEM((d, fb), F32), pltpu.VMEM((fb, d), F32)],
        compiler_params=_cparams(("parallel", "arbitrary")),
    )(h, dxb, da, db, act)


def _pool_diff(ext_ref, rows, cols, width, t_idx):
    s = ext_ref[POOL_HALO:POOL_HALO + rows, cols]
    for sft in range(1, width):
        s = s + ext_ref[POOL_HALO - sft:POOL_HALO - sft + rows, cols]
    cnt = jnp.minimum(t_idx + 1, width).astype(F32)
    return s / cnt - ext_ref[POOL_HALO:POOL_HALO + rows, cols]


def _pool_fwd(proj, pool_w, pool_scale):
    n = proj.shape[0]
    dp = pool_scale.shape[1]
    c = dp // len(POOL_WINDOWS)
    tm = _tile(n, 512, POOL_HALO)
    hb = tm // POOL_HALO

    def body(u_ref, halo_ref, pw_ref, sc_ref, y_ref, ext):
        i = pl.program_id(0)
        ext[0:POOL_HALO, :] = jnp.where(i > 0, halo_ref[...], 0.0)
        ext[POOL_HALO:, :] = u_ref[...]
        t_idx = i * tm + lax.broadcasted_iota(jnp.int32, (tm, 1), 0)
        for g, width in enumerate(POOL_WINDOWS):
            cols = slice(g * c, (g + 1) * c)
            dgrp = _pool_diff(ext, tm, cols, width, t_idx)
            y_ref[:, cols] = (_dot(dgrp, pw_ref[g]) * sc_ref[:, cols]).astype(BF16)

    return pl.pallas_call(
        body, name="pool_fwd", grid=(n // tm,),
        in_specs=[pl.BlockSpec((tm, dp), lambda i: (i, 0)),
                  pl.BlockSpec((POOL_HALO, dp), lambda i: (jnp.maximum(i * hb - 1, 0), 0)),
                  pl.BlockSpec((len(POOL_WINDOWS), c, c), lambda i: (0, 0, 0)),
                  pl.BlockSpec((1, dp), lambda i: (0, 0))],
        out_specs=pl.BlockSpec((tm, dp), lambda i: (i, 0)),
        out_shape=jax.ShapeDtypeStruct((n, dp), BF16),
        scratch_shapes=[pltpu.VMEM((tm + POOL_HALO, dp), F32)],
        compiler_params=_cparams(("parallel",)),
    )(proj, proj, pool_w, pool_scale)


def _pool_bwd(proj, dymix, pool_w, pool_scale):
    n = proj.shape[0]
    dp = pool_scale.shape[1]
    ng = len(POOL_WINDOWS)
    c = dp // ng
    tm = _tile(n, 512, POOL_HALO)
    hb = tm // POOL_HALO
    nsteps = n // tm
    last_halo = n // POOL_HALO - 1

    def body(u_ref, halo_ref, dy_ref, dyn_ref, pw_ref, sc_ref, du_ref, dpw_ref, dsc_ref, ext, dyext, e_s):
        i = pl.program_id(0)

        @pl.when(i == 0)
        def _():
            dpw_ref[...] = jnp.zeros_like(dpw_ref)
            dsc_ref[...] = jnp.zeros_like(dsc_ref)

        ext[0:POOL_HALO, :] = jnp.where(i > 0, halo_ref[...], 0.0)
        ext[POOL_HALO:, :] = u_ref[...]
        dyext[0:tm, :] = dy_ref[...]
        dyext[tm:, :] = jnp.where(i < nsteps - 1, dyn_ref[...], 0.0)
        t_idx = i * tm + lax.broadcasted_iota(jnp.int32, (tm, 1), 0)
        te_idx = i * tm + lax.broadcasted_iota(jnp.int32, (tm + POOL_HALO, 1), 0)
        for g, width in enumerate(POOL_WINDOWS):
            cols = slice(g * c, (g + 1) * c)
            dgrp = _pool_diff(ext, tm, cols, width, t_idx)
            w_g = pw_ref[g]
            dys = dyext[:, cols] * sc_ref[:, cols]
            ypre = _dot(dgrp, w_g)
            dsc_ref[:, cols] += jnp.sum(dyext[0:tm, cols] * ypre, axis=0, keepdims=True)
            dpw_ref[g] += _dot(dgrp, dys[0:tm], _TN)
            dd = _dot(dys, w_g, _NT)
            e_s[...] = dd / jnp.minimum(te_idx + 1, width).astype(F32)
            acc = e_s[0:tm, :]
            for sft in range(1, width):
                acc = acc + e_s[sft:sft + tm, :]
            du_ref[:, cols] = (acc - dd[0:tm]).astype(BF16)

    return pl.pallas_call(
        body, name="pool_bwd", grid=(nsteps,),
        in_specs=[pl.BlockSpec((tm, dp), lambda i: (i, 0)),
                  pl.BlockSpec((POOL_HALO, dp), lambda i: (jnp.maximum(i * hb - 1, 0), 0)),
                  pl.BlockSpec((tm, dp), lambda i: (i, 0)),
                  pl.BlockSpec((POOL_HALO, dp), lambda i: (jnp.minimum((i + 1) * hb, last_halo), 0)),
                  pl.BlockSpec((ng, c, c), lambda i: (0, 0, 0)),
                  pl.BlockSpec((1, dp), lambda i: (0, 0))],
        out_specs=[pl.BlockSpec((tm, dp), lambda i: (i, 0)),
                   pl.BlockSpec((ng, c, c), lambda i: (0, 0, 0)),
                   pl.BlockSpec((1, dp), lambda i: (0, 0))],
        out_shape=[jax.ShapeDtypeStruct((n, dp), BF16), jax.ShapeDtypeStruct((ng, c, c), F32),
                   jax.ShapeDtypeStruct((1, dp), F32)],
        scratch_shapes=[pltpu.VMEM((tm + POOL_HALO, dp), F32), pltpu.VMEM((tm + POOL_HALO, dp), F32),
                        pltpu.VMEM((tm + POOL_HALO, c), F32)],
        compiler_params=_cparams(("arbitrary",)),
    )(proj, proj, dymix, dymix, pool_w, pool_scale)


def _chunk_cumsum(x):
    row = lax.broadcasted_iota(jnp.int32, x.shape, 0) % CHUNK
    s = 1
    while s < CHUNK:
        x = x + jnp.where(row >= s, pltpu.roll(x, s, 0), 0.0)
        s *= 2
    return x


def _gate_logits(alr_ref, wa_ref, ba_ref):
    z = _dot(alr_ref[...], wa_ref[...]) + ba_ref[...]
    la = (jnp.minimum(z, 0.0) - jnp.log(1.0 + jnp.exp(-jnp.abs(z)))) * (1.0 / GATE_TEMP)
    return z, la


def _gla_dims(proj, head_norm):
    n, pw = proj.shape
    dv4 = head_norm.shape[1]
    dk4 = dv4 // 2
    return n, pw, dv4, dk4, dk4 // HEADS, dv4 // HEADS


def _gla_in_specs(t, dk4, dv4, rev):
    alr_blk = (2 * dv4 + 2 * dv4) // LANE
    return [pl.BlockSpec((t, dk4), lambda i: (rev(i), 2)),
            pl.BlockSpec((t, dk4), lambda i: (rev(i), 3)),
            pl.BlockSpec((t, dv4), lambda i: (rev(i), 2)),
            pl.BlockSpec((t, dv4), lambda i: (rev(i), 3)),
            pl.BlockSpec((t, LANE), lambda i: (rev(i), alr_blk))]


def _gla_fwd(proj, wa2p, b_a, head_norm):
    n, _, dv4, dk4, dk, dv = _gla_dims(proj, head_norm)
    t = _tile(n, GLA_ROWS, CHUNK)
    nc = t // CHUNK
    qscale = dk ** -0.5

    def body(q_ref, k_ref, v_ref, g_ref, alr_ref, wa_ref, ba_ref, hn_ref, y_ref, st_ref, state, cum_s):
        i = pl.program_id(0)

        @pl.when(i == 0)
        def _():
            state[...] = jnp.zeros_like(state)

        _, la = _gate_logits(alr_ref, wa_ref, ba_ref)
        cum_s[...] = _chunk_cumsum(la)
        for c in range(nc):
            rows = slice(c * CHUNK, (c + 1) * CHUNK)
            bend = cum_s[c * CHUNK + CHUNK - 1:(c + 1) * CHUNK, :]
            dec = jnp.exp(bend - cum_s[rows, :])
            e = jnp.exp(bend)
            kd = (k_ref[rows, :] * dec).astype(BF16)
            qs = (q_ref[rows, :] * qscale).astype(BF16)
            vv = v_ref[rows, :].astype(BF16)
            for h in range(HEADS):
                hk = slice(h * dk, (h + 1) * dk)
                hv = slice(h * dv, (h + 1) * dv)
                s_new = state[h] * e[:, hk] + _dot(vv[:, hv], kd[:, hk], _TN)
                state[h] = s_new
                st_ref[c, h] = s_new
                o = _dot(qs[:, hk], s_new, _NT)
                r = lax.rsqrt(jnp.mean(o * o, axis=-1, keepdims=True) + RMS_EPS)
                gg = g_ref[rows, hv]
                y_ref[rows, hv] = (((o * r) * hn_ref[:, hv]) * (gg * jax.nn.sigmoid(gg))).astype(BF16)

    full = lambda shape: pl.BlockSpec(shape, lambda i: tuple(0 for _ in shape))
    return pl.pallas_call(
        body, name="gla_fwd", grid=(n // t,),
        in_specs=_gla_in_specs(t, dk4, dv4, lambda i: i)
        + [full((LANE, dk4)), full((1, dk4)), full((1, dv4))],
        out_specs=[pl.BlockSpec((t, dv4), lambda i: (i, 0)),
                   pl.BlockSpec((nc, HEADS, dv, dk), lambda i: (i, 0, 0, 0))],
        out_shape=[jax.ShapeDtypeStruct((n, dv4), BF16),
                   jax.ShapeDtypeStruct((n // CHUNK, HEADS, dv, dk), F32)],
        scratch_shapes=[pltpu.VMEM((HEADS, dv, dk), F32), pltpu.VMEM((t, dk4), F32)],
        compiler_params=_cparams(("arbitrary",)),
    )(proj, proj, proj, proj, proj, wa2p, b_a, head_norm)


def _gla_bwd(proj, dymix, du, states, wa2p, b_a, head_norm):
    n, pw, dv4, dk4, dk, dv = _gla_dims(proj, head_norm)
    t = _tile(n, GLA_ROWS, CHUNK)
    nc = t // CHUNK
    nsteps = n // t
    qscale = dk ** -0.5
    rev = lambda i: nsteps - 1 - i

    def body(q_ref, k_ref, v_ref, g_ref, alr_ref, dy_ref, du_ref, st_ref, prev_ref, wa_ref, ba_ref, hn_ref,
             dp_ref, dwa_ref, dba_ref, dhn_ref, carry, cum_s, dz_s):
        i = pl.program_id(0)

        @pl.when(i == 0)
        def _():
            carry[...] = jnp.zeros_like(carry)
            dwa_ref[...] = jnp.zeros_like(dwa_ref)
            dba_ref[...] = jnp.zeros_like(dba_ref)
            dhn_ref[...] = jnp.zeros_like(dhn_ref)

        first_step = i == nsteps - 1
        z, la = _gate_logits(alr_ref, wa_ref, ba_ref)
        one_minus_sig = 1.0 - jax.nn.sigmoid(z)
        cum_s[...] = _chunk_cumsum(la)
        dp_ref[:, 0:dv4] = du_ref[...]
        for c in reversed(range(nc)):
            rows = slice(c * CHUNK, (c + 1) * CHUNK)
            bend = cum_s[c * CHUNK + CHUNK - 1:(c + 1) * CHUNK, :]
            dec = jnp.exp(bend - cum_s[rows, :])
            e = jnp.exp(bend)
            kd_f = k_ref[rows, :] * dec
            kd = kd_f.astype(BF16)
            qs = (q_ref[rows, :] * qscale).astype(BF16)
            vv = v_ref[rows, :].astype(BF16)
            dkd_parts, dee_parts = [], []
            for h in range(HEADS):
                hk = slice(h * dk, (h + 1) * dk)
                hv = slice(h * dv, (h + 1) * dv)
                s_n = st_ref[c, h]
                if c > 0:
                    s_prev = st_ref[c - 1, h]
                else:
                    s_prev = jnp.where(first_step, 0.0, prev_ref[0, h])
                o = _dot(qs[:, hk], s_n, _NT)
                r = lax.rsqrt(jnp.mean(o * o, axis=-1, keepdims=True) + RMS_EPS)
                oh = o * r
                gg = g_ref[rows, hv]
                sl, dsl = _silu_parts(gg)
                dyh = dy_ref[rows, hv]
                hn = hn_ref[:, hv]
                tt = dyh * sl
                dhn_ref[:, hv] += jnp.sum(tt * oh, axis=0, keepdims=True)
                dg = dyh * (oh * hn) * dsl
                tt = tt * hn
                do = (r * (tt - oh * jnp.mean(tt * oh, axis=-1, keepdims=True))).astype(BF16)
                g_n = carry[h] + _dot(do, qs[:, hk], _TN)
                dq = _dot(do, s_n) * qscale
                dkd = _dot(vv[:, hv], g_n)
                dvh = _dot(kd[:, hk], g_n, _NT)
                de = jnp.sum(g_n * s_prev, axis=0, keepdims=True)
                carry[h] = g_n * e[:, hk]
                dp_ref[rows, dv4 + h * dk:dv4 + (h + 1) * dk] = dq.astype(BF16)
                dp_ref[rows, 2 * dv4 + h * dv:2 * dv4 + (h + 1) * dv] = dvh.astype(BF16)
                dp_ref[rows, 3 * dv4 + h * dv:3 * dv4 + (h + 1) * dv] = dg.astype(BF16)
                dkd_parts.append(dkd)
                dee_parts.append(de * e[:, hk])
            dkd_c = jnp.concatenate(dkd_parts, axis=1)
            dee = jnp.concatenate(dee_parts, axis=1)
            dp_ref[rows, dv4 + dk4:dv4 + 2 * dk4] = (dkd_c * dec).astype(BF16)
            w = dkd_c * kd_f
            dla = (_chunk_cumsum(w) - w) + dee
            dz_s[rows, :] = dla * (1.0 / GATE_TEMP)
        dz = dz_s[...] * one_minus_sig
        dp_ref[:, 4 * dv4:4 * dv4 + LANE] = _dot(dz, wa_ref[...], _NT).astype(BF16)
        dwa_ref[...] += _dot(alr_ref[...], dz, _TN)
        dba_ref[...] += jnp.sum(dz, axis=0, keepdims=True)

    full = lambda shape: pl.BlockSpec(shape, lambda i: tuple(0 for _ in shape))
    return pl.pallas_call(
        body, name="gla_bwd", grid=(nsteps,),
        in_specs=_gla_in_specs(t, dk4, dv4, rev)
        + [pl.BlockSpec((t, dv4), lambda i: (rev(i), 1)),
           pl.BlockSpec((t, dv4), lambda i: (rev(i), 0)),
           pl.BlockSpec((nc, HEADS, dv, dk), lambda i: (rev(i), 0, 0, 0)),
           pl.BlockSpec((1, HEADS, dv, dk), lambda i: (jnp.maximum(rev(i) * nc - 1, 0), 0, 0, 0)),
           full((LANE, dk4)), full((1, dk4)), full((1, dv4))],
        out_specs=[pl.BlockSpec((t, pw), lambda i: (rev(i), 0)),
                   full((LANE, dk4)), full((1, dk4)), full((1, dv4))],
        out_shape=[jax.ShapeDtypeStruct((n, pw), BF16), jax.ShapeDtypeStruct((LANE, dk4), F32),
                   jax.ShapeDtypeStruct((1, dk4), F32), jax.ShapeDtypeStruct((1, dv4), F32)],
        scratch_shapes=[pltpu.VMEM((HEADS, dv, dk), F32), pltpu.VMEM((t, dk4), F32),
                        pltpu.VMEM((t, dk4), F32)],
        compiler_params=_cparams(("arbitrary",)),
    )(proj, proj, proj, proj, proj, dymix, du, states, states, wa2p, b_a, head_norm)


def _softmax_rows(q, k, scale):
    s = _dot(q, k, _NT) * scale
    p = jnp.exp(s - jnp.max(s, axis=-1, keepdims=True))
    return p / jnp.sum(p, axis=-1, keepdims=True)


def _attn_fwd(q, kv):
    n, d = q.shape
    m = kv.shape[0]
    dh = d // HEADS
    tm = _tile(n, 512, 16)
    scale = dh ** -0.5

    def body(q_ref, k_ref, v_ref, o_ref):
        for h in range(HEADS):
            hs = slice(h * dh, (h + 1) * dh)
            p = _softmax_rows(q_ref[:, hs], k_ref[:, hs], scale)
            o_ref[:, hs] = _dot(p, v_ref[:, hs]).astype(BF16)

    return pl.pallas_call(
        body, name="attn_fwd", grid=(n // tm,),
        in_specs=[pl.BlockSpec((tm, d), lambda i: (i, 0)), pl.BlockSpec((m, d), lambda i: (0, 0)),
                  pl.BlockSpec((m, d), lambda i: (0, 1))],
        out_specs=pl.BlockSpec((tm, d), lambda i: (i, 0)),
        out_shape=jax.ShapeDtypeStruct((n, d), BF16),
        compiler_params=_cparams(("parallel",)),
    )(q, kv, kv)


def _attn_bwd(q, kv, do):
    n, d = q.shape
    m = kv.shape[0]
    dh = d // HEADS
    tm = _tile(n, 512, 16)
    scale = dh ** -0.5

    def body(q_ref, k_ref, v_ref, do_ref, dq_ref, dk_ref, dv_ref):
        i = pl.program_id(0)

        @pl.when(i == 0)
        def _():
            dk_ref[...] = jnp.zeros_like(dk_ref)
            dv_ref[...] = jnp.zeros_like(dv_ref)

        for h in range(HEADS):
            hs = slice(h * dh, (h + 1) * dh)
            qh, kh, vh, doh = q_ref[:, hs], k_ref[:, hs], v_ref[:, hs], do_ref[:, hs]
            p = _softmax_rows(qh, kh, scale)
            dv_ref[:, hs] += _dot(p, doh, _TN)
            dp = _dot(doh, vh, _NT)
            ds = p * (dp - jnp.sum(dp * p, axis=-1, keepdims=True)) * scale
            dq_ref[:, hs] = _dot(ds, kh).astype(BF16)
            dk_ref[:, hs] += _dot(ds, qh, _TN)

    row = pl.BlockSpec((tm, d), lambda i: (i, 0))
    memb = pl.BlockSpec((m, d), lambda i: (0, 0))
    return pl.pallas_call(
        body, name="attn_bwd", grid=(n // tm,),
        in_specs=[row, memb, pl.BlockSpec((m, d), lambda i: (0, 1)), row],
        out_specs=[row, memb, memb],
        out_shape=[jax.ShapeDtypeStruct((n, d), BF16), jax.ShapeDtypeStruct((m, d), F32),
                   jax.ShapeDtypeStruct((m, d), F32)],
        compiler_params=_cparams(("arbitrary",)),
    )(q, kv, kv, do)


def _adamw(name, w, m, v, parts, own_block):
    r, c = w.shape
    tr = _tile(r, max(16, (256 * 1024) // c), 16)
    row = pl.BlockSpec((tr, c), lambda i, o: (i, 0))
    ops, specs = [w, m, v], [row, row, row]
    for p in parts:
        if p.ndim == 2:
            ops.append(p)
            specs.append(row)
        elif p.shape[0] == 4:
            ops.append(p)
            specs.append(pl.BlockSpec((None, tr, c), lambda i, o: (o[0], i, 0)))
        else:
            for s in range(p.shape[0]):
                ops.append(p)
                specs.append(pl.BlockSpec((None, tr, c), lambda i, o, s=s: (s, i, 0)))
    n_parts = len(ops) - 3
    c1 = 1.0 - ADAM_B1 ** ADAM_STEP
    c2 = 1.0 - ADAM_B2 ** ADAM_STEP

    def body(o_ref, *refs):
        del o_ref
        w_ref, m_ref, v_ref = refs[:3]
        g_refs = refs[3:3 + n_parts]
        go_ref, d_ref, mo_ref, vo_ref = refs[3 + n_parts:]
        g = g_refs[0][...].astype(F32)
        for g_ref in g_refs[1:]:
            g = g + g_ref[...].astype(F32)
        m_new = ADAM_B1 * m_ref[...] + (1.0 - ADAM_B1) * g
        v_new = ADAM_B2 * v_ref[...] + (1.0 - ADAM_B2) * (g * g)
        m_hat = m_new / c1
        v_hat = v_new / c2
        go_ref[...] = g
        d_ref[...] = -ADAM_LR * (m_hat / (jnp.sqrt(v_hat) + ADAM_EPS) + ADAM_WD * w_ref[...])
        mo_ref[...] = m_new
        vo_ref[...] = v_new

    shp = jax.ShapeDtypeStruct((r, c), F32)
    return pl.pallas_call(
        body, name=name,
        grid_spec=pltpu.PrefetchScalarGridSpec(
            num_scalar_prefetch=1, grid=(r // tr,), in_specs=specs, out_specs=[row] * 4),
        out_shape=[shp] * 4,
        compiler_params=_cparams(("parallel",)),
    )(own_block, *ops)


def _pair_add(name, g, recvd, core):
    _, r, c = g.shape
    tr = _tile(r, max(16, (256 * 1024) // c), 16)

    def body(core_ref, a_ref, b_ref, o_ref):
        del core_ref
        o_ref[...] = (a_ref[...].astype(F32) + b_ref[...].astype(F32)).astype(o_ref.dtype)

    blk = pl.BlockSpec((None, tr, c), lambda s, i, core_ref: (s, i, 0))
    mine = pl.BlockSpec((None, tr, c), lambda s, i, core_ref: (2 * s + core_ref[0], i, 0))
    return pl.pallas_call(
        body, name=name,
        grid_spec=pltpu.PrefetchScalarGridSpec(
            num_scalar_prefetch=1, grid=(4, r // tr), in_specs=[mine, blk], out_specs=blk),
        out_shape=jax.ShapeDtypeStruct(recvd.shape, g.dtype),
        compiler_params=_cparams(("parallel", "parallel")),
    )(core, g, recvd)


def _position():
    return lax.axis_index("x"), lax.axis_index("y"), lax.axis_index("c")


_HBM = pl.BlockSpec(memory_space=pltpu.HBM)


def _all_gather(name, shards):
    n = len(shards)

    def body(*refs):
        ins, outs = refs[:n], refs[n:2 * n]
        send_sems, recv_sems, local_sems = refs[2 * n:]
        x, y, c = _position()
        me, sibling = (x, y, c), (x, y, 1 - c)
        chips = [(1 - x, y), (x, 1 - y), (1 - x, 1 - y)]

        def block(ref, px, py, pc):
            return ref.at[4 * px + 2 * py + pc]

        def copy(a, k, owner, to, src=None):
            return pltpu.make_async_remote_copy(
                src_ref=block(outs[a], *owner) if src is None else src, dst_ref=block(outs[a], *owner),
                send_sem=send_sems.at[7 * a + k], recv_sem=recv_sems.at[7 * a + k],
                device_id=to, device_id_type=MESH)

        started = []
        for a in range(n):
            mine = pltpu.make_async_copy(ins[a], block(outs[a], *me), local_sems.at[a])
            mine.start()
            started.append(mine)
        sends = []
        for a in range(n):
            sends.append(copy(a, 0, me, sibling, src=ins[a]))
            sends += [copy(a, 1 + j, me, (*chip, c), src=ins[a]) for j, chip in enumerate(chips)]
        for cp in sends:
            cp.start()
        for j, chip in enumerate(chips):
            for a in range(n):
                copy(a, 1 + j, (*chip, c), me).wait_recv()
                fwd = copy(a, 4 + j, (*chip, c), sibling)
                fwd.start()
                sends.append(fwd)
        for a in range(n):
            copy(a, 0, sibling, me).wait_recv()
            for j, chip in enumerate(chips):
                copy(a, 4 + j, (*chip, 1 - c), me).wait_recv()
        for cp in sends:
            cp.wait_send()
        for mine in started:
            mine.wait()

    return pl.pallas_call(
        body, name=name,
        in_specs=[_HBM] * n, out_specs=[_HBM] * n,
        out_shape=[jax.ShapeDtypeStruct((N_DEV,) + s.shape, s.dtype) for s in shards],
        scratch_shapes=[pltpu.SemaphoreType.DMA((7 * n,)), pltpu.SemaphoreType.DMA((7 * n,)),
                        pltpu.SemaphoreType.DMA((n,))],
    )(*shards)


def _reduce_scatter_pair(name, grads):
    n = len(grads)

    def body(*refs):
        ins, recvd = refs[:n], refs[n:2 * n]
        send_sems, recv_sems = refs[2 * n:]
        x, y, c = _position()
        for a in range(n):
            for chip in range(4):
                pltpu.make_async_remote_copy(
                    src_ref=ins[a].at[2 * chip + (1 - c)], dst_ref=recvd[a].at[chip],
                    send_sem=send_sems.at[a], recv_sem=recv_sems.at[a],
                    device_id=(x, y, 1 - c), device_id_type=MESH).start()
        for a in range(n):
            pltpu.make_async_remote_copy(
                src_ref=recvd[a], dst_ref=recvd[a], send_sem=send_sems.at[a], recv_sem=recv_sems.at[a],
                device_id=(x, y, 1 - c), device_id_type=MESH).wait()

    return pl.pallas_call(
        body, name=name, in_specs=[_HBM] * n, out_specs=[_HBM] * n,
        out_shape=[jax.ShapeDtypeStruct((4,) + g.shape[1:], g.dtype) for g in grads],
        scratch_shapes=[pltpu.SemaphoreType.DMA((n,)), pltpu.SemaphoreType.DMA((n,))],
    )(*grads)


def _reduce_scatter_chips(name, sums):
    n = len(sums)
    offsets = [(1, 0), (0, 1), (1, 1)]

    def body(*refs):
        ins, recvd = refs[:n], refs[n:2 * n]
        send_sems, recv_sems = refs[2 * n:]
        x, y, c = _position()
        for a in range(n):
            for r, (ox, oy) in enumerate(offsets):
                px = 1 - x if ox else x
                py = 1 - y if oy else y
                pltpu.make_async_remote_copy(
                    src_ref=ins[a].at[2 * px + py], dst_ref=recvd[a].at[r],
                    send_sem=send_sems.at[a], recv_sem=recv_sems.at[a],
                    device_id=(px, py, c), device_id_type=MESH).start()
        for a in range(n):
            pltpu.make_async_remote_copy(
                src_ref=recvd[a], dst_ref=recvd[a], send_sem=send_sems.at[a], recv_sem=recv_sems.at[a],
                device_id=(x, y, c), device_id_type=MESH).wait()

    return pl.pallas_call(
        body, name=name, in_specs=[_HBM] * n, out_specs=[_HBM] * n,
        out_shape=[jax.ShapeDtypeStruct((3,) + s.shape[1:], s.dtype) for s in sums],
        scratch_shapes=[pltpu.SemaphoreType.DMA((n,)), pltpu.SemaphoreType.DMA((n,))],
    )(*sums)


def _all_reduce_small(vec):
    r = vec.shape[0]

    def body(v_ref, o_ref, gbuf, send_sems, recv_sems):
        x, y, c = _position()
        me = 4 * x + 2 * y + c
        gbuf[me] = v_ref[...]
        copies = []
        for k in range(1, N_DEV):
            ox, oy, oc = (k >> 2) & 1, (k >> 1) & 1, k & 1
            peer = (1 - x if ox else x, 1 - y if oy else y, 1 - c if oc else c)
            cp = pltpu.make_async_remote_copy(
                src_ref=gbuf.at[me], dst_ref=gbuf.at[me], send_sem=send_sems.at[k - 1],
                recv_sem=recv_sems.at[k - 1], device_id=peer, device_id_type=MESH)
            cp.start()
            copies.append(cp)
        for cp in copies:
            cp.wait()
        total = gbuf[0]
        for j in range(1, N_DEV):
            total = total + gbuf[j]
        o_ref[...] = total

    return pl.pallas_call(
        body, name="all_reduce_small",
        in_specs=[pl.BlockSpec(memory_space=pltpu.VMEM)],
        out_specs=pl.BlockSpec(memory_space=pltpu.VMEM),
        out_shape=jax.ShapeDtypeStruct(vec.shape, F32),
        scratch_shapes=[pltpu.VMEM((N_DEV, r, LANE), F32), pltpu.SemaphoreType.DMA((N_DEV - 1,)),
                        pltpu.SemaphoreType.DMA((N_DEV - 1,))],
    )(vec)


def _cols_from_blocks(g):
    nb, r, cs = g.shape
    return jnp.transpose(g, (1, 0, 2)).reshape(r, nb * cs)


def _cols_to_blocks(w):
    r, cfull = w.shape
    return jnp.transpose(w.reshape(r, N_DEV, cfull // N_DEV), (1, 0, 2))


def kernel(x, mem, ffn1_norm, ffn1_w_gate, ffn1_w_up, ffn1_w_down, mix_norm, w_in, pool_w, pool_scale, gla_w_a2, gla_b_a, gla_head_norm, w_out, xattn_norm, mem_norm, xattn_w_q, xattn_w_kv, xattn_w_o, ffn2_norm, ffn2_w_gate, ffn2_w_up, ffn2_w_down, final_norm, loss_target, m_ffn1_norm, m_ffn1_w_gate, m_ffn1_w_up, m_ffn1_w_down, m_mix_norm, m_w_in, m_pool_w, m_pool_scale, m_gla_w_a2, m_gla_b_a, m_gla_head_norm, m_w_out, m_xattn_norm, m_mem_norm, m_xattn_w_q, m_xattn_w_kv, m_xattn_w_o, m_ffn2_norm, m_ffn2_w_gate, m_ffn2_w_up, m_ffn2_w_down, m_final_norm, v_ffn1_norm, v_ffn1_w_gate, v_ffn1_w_up, v_ffn1_w_down, v_mix_norm, v_w_in, v_pool_w, v_pool_scale, v_gla_w_a2, v_gla_b_a, v_gla_head_norm, v_w_out, v_xattn_norm, v_mem_norm, v_xattn_w_q, v_xattn_w_kv, v_xattn_w_o, v_ffn2_norm, v_ffn2_w_gate, v_ffn2_w_up, v_ffn2_w_down, v_final_norm):
    weights = dict(ffn1_norm=ffn1_norm, ffn1_w_gate=ffn1_w_gate, ffn1_w_up=ffn1_w_up, ffn1_w_down=ffn1_w_down, mix_norm=mix_norm, w_in=w_in, pool_w=pool_w, pool_scale=pool_scale, gla_w_a2=gla_w_a2, gla_b_a=gla_b_a, gla_head_norm=gla_head_norm, w_out=w_out, xattn_norm=xattn_norm, mem_norm=mem_norm, xattn_w_q=xattn_w_q, xattn_w_kv=xattn_w_kv, xattn_w_o=xattn_w_o, ffn2_norm=ffn2_norm, ffn2_w_gate=ffn2_w_gate, ffn2_w_up=ffn2_w_up, ffn2_w_down=ffn2_w_down, final_norm=final_norm)
    mom1 = dict(ffn1_norm=m_ffn1_norm, ffn1_w_gate=m_ffn1_w_gate, ffn1_w_up=m_ffn1_w_up, ffn1_w_down=m_ffn1_w_down, mix_norm=m_mix_norm, w_in=m_w_in, pool_w=m_pool_w, pool_scale=m_pool_scale, gla_w_a2=m_gla_w_a2, gla_b_a=m_gla_b_a, gla_head_norm=m_gla_head_norm, w_out=m_w_out, xattn_norm=m_xattn_norm, mem_norm=m_mem_norm, xattn_w_q=m_xattn_w_q, xattn_w_kv=m_xattn_w_kv, xattn_w_o=m_xattn_w_o, ffn2_norm=m_ffn2_norm, ffn2_w_gate=m_ffn2_w_gate, ffn2_w_up=m_ffn2_w_up, ffn2_w_down=m_ffn2_w_down, final_norm=m_final_norm)
    mom2 = dict(ffn1_norm=v_ffn1_norm, ffn1_w_gate=v_ffn1_w_gate, ffn1_w_up=v_ffn1_w_up, ffn1_w_down=v_ffn1_w_down, mix_norm=v_mix_norm, w_in=v_w_in, pool_w=v_pool_w, pool_scale=v_pool_scale, gla_w_a2=v_gla_w_a2, gla_b_a=v_gla_b_a, gla_head_norm=v_gla_head_norm, w_out=v_w_out, xattn_norm=v_xattn_norm, mem_norm=v_mem_norm, xattn_w_q=v_xattn_w_q, xattn_w_kv=v_xattn_w_kv, xattn_w_o=v_xattn_w_o, ffn2_norm=v_ffn2_norm, ffn2_w_gate=v_ffn2_w_gate, ffn2_w_up=v_ffn2_w_up, ffn2_w_down=v_ffn2_w_down, final_norm=v_final_norm)
    order = list(weights.keys())

    n, d = x.shape[1], x.shape[2]
    mlen = mem.shape[1]
    x0 = x.reshape(n, d)
    memf = mem.reshape(mlen, d)
    target = loss_target.reshape(n, d)
    dpool = d // 2
    in_cols = w_in.shape[2] * N_DEV
    proj_cols = 2 * d + LANE
    rank = gla_w_a2.shape[1]

    def shard(name):
        return weights[name][0].astype(BF16)

    wg1, wu1, wd1 = _all_gather("ag_ffn1", [shard("ffn1_w_gate"), shard("ffn1_w_up"), shard("ffn1_w_down")])
    win_g, pw_g, wa2_g, wout_g = _all_gather(
        "ag_mix", [shard("w_in"), shard("pool_w"), shard("gla_w_a2"), shard("w_out")])
    wq_g, wkv_g, wo_g = _all_gather("ag_xattn", [shard("xattn_w_q"), shard("xattn_w_kv"), shard("xattn_w_o")])
    wg2, wu2, wd2 = _all_gather("ag_ffn2", [shard("ffn2_w_gate"), shard("ffn2_w_up"), shard("ffn2_w_down")])

    winp = jnp.pad(_cols_from_blocks(win_g), ((0, 0), (0, proj_cols - in_cols)))
    pw = jnp.transpose(pw_g, (1, 0, 2, 3)).reshape(len(POOL_WINDOWS), dpool // 4, dpool // 4)
    wa2p = jnp.pad(_cols_from_blocks(wa2_g), ((0, LANE - rank), (0, 0)))
    wout = wout_g.reshape(d, d)
    wq = wq_g.reshape(d, d)
    wkv = _cols_from_blocks(wkv_g)
    wo = wo_g.reshape(d, d)
    fnorm = final_norm.reshape(1, d)

    h1 = _rms_fwd("rms_ffn1", x0, ffn1_norm)
    x1, a1, b1 = _ffn_fwd("ffn1_fwd", x0, h1, wg1, wu1, wd1)
    h2 = _rms_fwd("rms_mix", x1, mix_norm)
    proj = _mm_nn("mix_in", [(h2, winp)], F32, tn=1408)
    ypool = _pool_fwd(proj, pw, pool_scale)
    ygla, states = _gla_fwd(proj, wa2p, gla_b_a, gla_head_norm)
    x2 = _mm_nn("mix_out", [(ypool, wout[:dpool]), (ygla, wout[dpool:])], F32, res=x1, tk=1024)
    h3 = _rms_fwd("rms_xattn", x2, xattn_norm)
    mh = _rms_fwd("rms_mem", memf, mem_norm)
    q = _mm_nn("xattn_q", [(h3, wq)], BF16)
    kv = _mm_nn("xattn_kv", [(mh, wkv)], BF16)
    att = _attn_fwd(q, kv)
    x3 = _mm_nn("xattn_o", [(att, wo)], F32, res=x2)
    h4 = _rms_fwd("rms_ffn2", x3, ffn2_norm)
    x4, a2, b2 = _ffn_fwd("ffn2_fwd", x3, h4, wg2, wu2, wd2)
    dx4, dx4b, g_final, loss_part = _loss_head(x4, target, fnorm)

    grads = {}

    da2, db2, act2, dh4 = _ffn_bwd_tokens("ffn2_bwd_tokens", dx4b, wg2, wu2, wd2, a2, b2)
    grads["ffn2_w_gate"], grads["ffn2_w_up"], grads["ffn2_w_down"] = _ffn_bwd_weights(
        "ffn2_bwd_weights", h4, dx4b, da2, db2, act2)
    dx3, dx3b, g_ffn2_norm = _rms_bwd("rms_ffn2_bwd", x3, ffn2_norm, dh4, dx4)

    datt = _mm_nt("xattn_do", [(dx3b, wo)], BF16)
    grads["xattn_w_o"] = _mm_tn("xattn_dwo", [(att, dx3b)], BF16).reshape(N_DEV, d // N_DEV, d)
    dq, dk, dv = _attn_bwd(q, kv, datt)
    dh3 = _mm_nt("xattn_dh", [(dq, wq)], F32)
    grads["xattn_w_q"] = _mm_tn("xattn_dwq", [(h3, dq)], BF16).reshape(N_DEV, d // N_DEV, d)
    dx2, dx2b, g_xattn_norm = _rms_bwd("rms_xattn_bwd", x2, xattn_norm, dh3, dx3)
    dkv = jnp.concatenate([dk, dv], axis=1)
    dmh = _mm_nt("xattn_dmh", [(dkv, wkv)], F32)
    grads["xattn_w_kv"] = _cols_to_blocks(_mm_tn("xattn_dwkv", [(mh, dkv)], BF16))
    _, _, g_mem_norm = _rms_bwd("rms_mem_bwd", memf, mem_norm, dmh)

    dymix = _mm_nt("mix_dy", [(dx2b, wout)], F32)
    grads["w_out"] = jnp.concatenate(
        [_mm_tn("mix_dwout_pool", [(ypool, dx2b)], BF16), _mm_tn("mix_dwout_gla", [(ygla, dx2b)], BF16)],
        axis=0).reshape(N_DEV, d // N_DEV, d)
    du, g_pool_w, g_pool_scale = _pool_bwd(proj, dymix, pw, pool_scale)
    dproj, g_wa2p, g_b_a, g_head_norm = _gla_bwd(proj, dymix, du, states, wa2p, gla_b_a, gla_head_norm)
    dh2 = _mm_nt("mix_dh", [(dproj, winp)], F32, tk=1408)
    grads["w_in"] = _cols_to_blocks(_mm_tn("mix_dwin", [(h2, dproj)], BF16, tn=1408)[:, :in_cols])
    grads["pool_w"] = jnp.transpose(
        g_pool_w.reshape(len(POOL_WINDOWS), N_DEV, dpool // 4 // N_DEV, dpool // 4), (1, 0, 2, 3))
    grads["gla_w_a2"] = _cols_to_blocks(g_wa2p[:rank])
    dx1, dx1b, g_mix_norm = _rms_bwd("rms_mix_bwd", x1, mix_norm, dh2, dx2)

    da1, db1, act1, dh1 = _ffn_bwd_tokens("ffn1_bwd_tokens", dx1b, wg1, wu1, wd1, a1, b1)
    grads["ffn1_w_gate"], grads["ffn1_w_up"], grads["ffn1_w_down"] = _ffn_bwd_weights(
        "ffn1_bwd_weights", h1, dx1b, da1, db1, act1)
    dx0, _, g_ffn1_norm = _rms_bwd("rms_ffn1_bwd", x0, ffn1_norm, dh1, dx1)

    small = [("ffn1_norm", g_ffn1_norm), ("mix_norm", g_mix_norm), ("pool_scale", g_pool_scale),
             ("gla_b_a", g_b_a), ("gla_head_norm", g_head_norm), ("xattn_norm", g_xattn_norm),
             ("mem_norm", g_mem_norm), ("ffn2_norm", g_ffn2_norm), ("final_norm", g_final)]
    packed = jnp.concatenate([g.reshape(-1) for _, g in small] + [loss_part.reshape(-1)])
    slab = 8 * LANE
    padded = -(-packed.shape[0] // slab) * slab
    packed = jnp.pad(packed, (0, padded - packed.shape[0])).reshape(padded // LANE, LANE)
    reduced = _all_reduce_small(packed).reshape(-1)
    small_grads = {}
    off = 0
    for name, g in small:
        small_grads[name] = reduced[off:off + g.size]
        off += g.size
    loss = reduced[off]

    sharded = [k for k in order if k not in small_grads]
    gl = []
    for k in sharded:
        g = grads[k]
        gl.append(g.reshape(N_DEV, -1, g.shape[-1]))
    core = lax.axis_index("c").astype(jnp.int32).reshape(1)
    chip = (2 * lax.axis_index("x") + lax.axis_index("y")).astype(jnp.int32).reshape(1)
    recvd1 = _reduce_scatter_pair("rs_pair", gl)
    sums = [_pair_add("rs_add_" + k, gl[i], recvd1[i], core) for i, k in enumerate(sharded)]
    recvd2 = _reduce_scatter_chips("rs_chips", sums)

    out_g, out_d, out_m, out_v = {}, {}, {}, {}
    for k in order:
        w = weights[k]
        if k in small_grads:
            w2 = w.reshape(1, -1)
            parts = [small_grads[k].reshape(1, -1)]
            own_block = jnp.zeros((1,), jnp.int32)
        else:
            i = sharded.index(k)
            w2 = w.reshape(-1, w.shape[-1])
            parts = [sums[i], recvd2[i]]
            own_block = chip
        res = _adamw("adamw_" + k, w2, mom1[k].reshape(w2.shape), mom2[k].reshape(w2.shape), parts, own_block)
        out_g[k], out_d[k], out_m[k], out_v[k] = [r.reshape(w.shape) for r in res]

    return (loss, dx0.reshape(x.shape), *[out_g[k] for k in order], *[out_d[k] for k in order],
            *[out_m[k] for k in order], *[out_v[k] for k in order])
```

```python
import collections

import jax
import jax.numpy as jnp
from jax import lax
from jax.experimental import pallas as pl
from jax.experimental.pallas import tpu as pltpu

F32 = jnp.float32
BF16 = jnp.bfloat16
MESH = pl.DeviceIdType.MESH

N_DEV = 8
CHUNK = 64
POOL_WINDOWS = (2, 4, 8, 16)
POOL_HALO = 16
HEADS = 4
GATE_TEMP = 16.0
RMS_EPS = 1e-6
LANE = 128
V7X_VMEM_BYTES = 64 * 1024 * 1024
VMEM_LIMIT = V7X_VMEM_BYTES - 8 * 1024 * 1024
GLA_ROWS = 4 * CHUNK

ADAM_LR = 0.001
ADAM_B1 = 0.9
ADAM_B2 = 0.999
ADAM_EPS = 1e-08
ADAM_WD = 0.01
ADAM_STEP = 10

_NN = (((1,), (0,)), ((), ()))
_NT = (((1,), (1,)), ((), ()))
_TN = (((0,), (0,)), ((), ()))


def _cparams(sem=None):
    return pltpu.CompilerParams(dimension_semantics=sem, vmem_limit_bytes=VMEM_LIMIT)


def _tile(n, pref, align):
    t = (min(pref, n) // align) * align
    while t >= align:
        if n % t == 0:
            return t
        t -= align
    return n


def _dot(a, b, dims=_NN):
    return lax.dot_general(a.astype(BF16), b.astype(BF16), dims, preferred_element_type=F32)


def _silu_parts(z):
    sig = jax.nn.sigmoid(z)
    return z * sig, sig * (1.0 + z * (1.0 - sig))


def _matmul(name, pairs, a_spec, b_spec, out_shape, out_spec, grid, acc_shape, dims,
            res=None, res_spec=None, scale=None):
    n = len(pairs)
    nk = grid[2]

    def body(*refs):
        a_refs, b_refs = refs[:n], refs[n:2 * n]
        pos = 2 * n
        res_ref = None
        if res is not None:
            res_ref = refs[pos]
            pos += 1
        o_ref, acc = refs[pos], refs[pos + 1]
        k = pl.program_id(2)

        @pl.when(k == 0)
        def _():
            acc[...] = jnp.zeros_like(acc)

        part = None
        for a_ref, b_ref in zip(a_refs, b_refs):
            d = _dot(a_ref[...], b_ref[...], dims)
            part = d if part is None else part + d
        acc[...] += part

        @pl.when(k == nk - 1)
        def _():
            r = acc[...]
            if scale is not None:
                r = r * scale
            if res_ref is not None:
                r = r + res_ref[...]
            o_ref[...] = r.astype(o_ref.dtype)

    ops = [p[0] for p in pairs] + [p[1] for p in pairs]
    specs = [a_spec] * n + [b_spec] * n
    if res is not None:
        ops.append(res)
        specs.append(res_spec)
    return pl.pallas_call(
        body, name=name, grid=grid, in_specs=specs, out_specs=out_spec, out_shape=out_shape,
        scratch_shapes=[pltpu.VMEM(acc_shape, F32)],
        compiler_params=_cparams(("parallel", "parallel", "arbitrary")),
    )(*ops)


def _mm_nn(name, pairs, out_dtype, res=None, tm=512, tn=1024, tk=2048):
    m, kd = pairs[0][0].shape
    nd = pairs[0][1].shape[1]
    tm, tn, tk = _tile(m, tm, 16), _tile(nd, tn, LANE), _tile(kd, tk, LANE)
    return _matmul(
        name, pairs,
        pl.BlockSpec((tm, tk), lambda i, j, k: (i, k)),
        pl.BlockSpec((tk, tn), lambda i, j, k: (k, j)),
        jax.ShapeDtypeStruct((m, nd), out_dtype),
        pl.BlockSpec((tm, tn), lambda i, j, k: (i, j)),
        (m // tm, nd // tn, kd // tk), (tm, tn), _NN,
        res=res, res_spec=pl.BlockSpec((tm, tn), lambda i, j, k: (i, j)))


def _mm_nt(name, pairs, out_dtype, tm=512, tn=1024, tk=2048):
    m, kd = pairs[0][0].shape
    nd = pairs[0][1].shape[0]
    tm, tn, tk = _tile(m, tm, 16), _tile(nd, tn, LANE), _tile(kd, tk, LANE)
    return _matmul(
        name, pairs,
        pl.BlockSpec((tm, tk), lambda i, j, k: (i, k)),
        pl.BlockSpec((tn, tk), lambda i, j, k: (j, k)),
        jax.ShapeDtypeStruct((m, nd), out_dtype),
        pl.BlockSpec((tm, tn), lambda i, j, k: (i, j)),
        (m // tm, nd // tn, kd // tk), (tm, tn), _NT)


def _mm_tn(name, pairs, out_dtype, tm=512, tn=1024, tk=1024):
    kd, m = pairs[0][0].shape
    nd = pairs[0][1].shape[1]
    tm, tn, tk = _tile(m, tm, LANE), _tile(nd, tn, LANE), _tile(kd, tk, 16)
    return _matmul(
        name, pairs,
        pl.BlockSpec((tk, tm), lambda i, j, k: (k, i)),
        pl.BlockSpec((tk, tn), lambda i, j, k: (k, j)),
        jax.ShapeDtypeStruct((m, nd), out_dtype),
        pl.BlockSpec((tm, tn), lambda i, j, k: (i, j)),
        (m // tm, nd // tn, kd // tk), (tm, tn), _TN)


def _rms_fwd(name, x, gain):
    m, d = x.shape
    tm = _tile(m, 512, 16)

    def body(x_ref, g_ref, o_ref):
        xf = x_ref[...]
        r = lax.rsqrt(jnp.mean(xf * xf, axis=-1, keepdims=True) + RMS_EPS)
        o_ref[...] = ((xf * r) * g_ref[...]).astype(o_ref.dtype)

    return pl.pallas_call(
        body, name=name, grid=(m // tm,),
        in_specs=[pl.BlockSpec((tm, d), lambda i: (i, 0)), pl.BlockSpec((1, d), lambda i: (0, 0))],
        out_specs=pl.BlockSpec((tm, d), lambda i: (i, 0)),
        out_shape=jax.ShapeDtypeStruct((m, d), BF16),
        compiler_params=_cparams(("parallel",)),
    )(x, gain)


def _rms_bwd(name, x, gain, dh, dres=None):
    m, d = x.shape
    tm = _tile(m, 512, 16)
    has_res = dres is not None

    def body(*refs):
        x_ref, g_ref, dh_ref = refs[:3]
        pos = 3
        res_ref = None
        if has_res:
            res_ref = refs[pos]
            pos += 1
        dx_ref, dxb_ref, dg_ref = refs[pos], refs[pos + 1], refs[pos + 2]
        i = pl.program_id(0)

        @pl.when(i == 0)
        def _():
            dg_ref[...] = jnp.zeros_like(dg_ref)

        xf = x_ref[...]
        r = lax.rsqrt(jnp.mean(xf * xf, axis=-1, keepdims=True) + RMS_EPS)
        xh = xf * r
        dhf = dh_ref[...].astype(F32)
        t = dhf * g_ref[...]
        dx = r * (t - xh * jnp.mean(t * xh, axis=-1, keepdims=True))
        if res_ref is not None:
            dx = dx + res_ref[...]
        dx_ref[...] = dx
        dxb_ref[...] = dx.astype(BF16)
        dg_ref[...] += jnp.sum(dhf * xh, axis=0, keepdims=True)

    row = pl.BlockSpec((tm, d), lambda i: (i, 0))
    vec = pl.BlockSpec((1, d), lambda i: (0, 0))
    ops = [x, gain, dh] + ([dres] if has_res else [])
    return pl.pallas_call(
        body, name=name, grid=(m // tm,),
        in_specs=[row, vec, row] + ([row] if has_res else []),
        out_specs=[row, row, vec],
        out_shape=[jax.ShapeDtypeStruct((m, d), F32), jax.ShapeDtypeStruct((m, d), BF16),
                   jax.ShapeDtypeStruct((1, d), F32)],
        compiler_params=_cparams(("arbitrary",)),
    )(*ops)


def _loss_head(x, target, gain):
    m, d = x.shape
    tm = _tile(m, 512, 16)

    def body(x_ref, t_ref, g_ref, dx_ref, dxb_ref, dg_ref, loss_ref):
        i = pl.program_id(0)

        @pl.when(i == 0)
        def _():
            dg_ref[...] = jnp.zeros_like(dg_ref)
            loss_ref[...] = jnp.zeros_like(loss_ref)

        xf = x_ref[...]
        r = lax.rsqrt(jnp.mean(xf * xf, axis=-1, keepdims=True) + RMS_EPS)
        xh = xf * r
        g = g_ref[...]
        err = xh * g - t_ref[...]
        loss_ref[...] += jnp.full(loss_ref.shape, (0.5 / d) * jnp.sum(err * err), F32)
        dy = err * (1.0 / d)
        t = dy * g
        dx = r * (t - xh * jnp.mean(t * xh, axis=-1, keepdims=True))
        dx_ref[...] = dx
        dxb_ref[...] = dx.astype(BF16)
        dg_ref[...] += jnp.sum(dy * xh, axis=0, keepdims=True)

    row = pl.BlockSpec((tm, d), lambda i: (i, 0))
    vec = pl.BlockSpec((1, d), lambda i: (0, 0))
    return pl.pallas_call(
        body, name="loss_head", grid=(m // tm,),
        in_specs=[row, row, vec],
        out_specs=[row, row, vec, pl.BlockSpec((1, LANE), lambda i: (0, 0))],
        out_shape=[jax.ShapeDtypeStruct((m, d), F32), jax.ShapeDtypeStruct((m, d), BF16),
                   jax.ShapeDtypeStruct((1, d), F32), jax.ShapeDtypeStruct((1, LANE), F32)],
        compiler_params=_cparams(("arbitrary",)),
    )(x, target, gain)


def _hosted_call(name, body, grid, in_specs, out_specs, out_shape, scratch, operands, exchange):
    n_in, n_out, n_scr = len(in_specs), len(out_specs), len(scratch)
    n_xin, n_xout = len(exchange.ins), len(exchange.out_shape)

    def full_body(*refs):
        pos = 0
        parts = []
        for cnt in (n_in, n_xin, n_out, n_xout, n_scr):
            parts.append(refs[pos:pos + cnt])
            pos += cnt
        ins, x_ins, outs, x_outs, scr = parts
        sems = refs[pos:]
        first = pl.program_id(0) == 0
        last = pl.program_id(0) == grid[0] - 1
        for ax in range(1, len(grid)):
            first = jnp.logical_and(first, pl.program_id(ax) == 0)
            last = jnp.logical_and(last, pl.program_id(ax) == grid[ax] - 1)

        @pl.when(first)
        def _():
            exchange.start(x_ins, x_outs, sems)

        body(ins, outs, scr)

        @pl.when(last)
        def _():
            exchange.finish(x_ins, x_outs, sems)

    res = pl.pallas_call(
        full_body, name=name, grid=grid,
        in_specs=list(in_specs) + [_HBM] * n_xin, out_specs=list(out_specs) + [_HBM] * n_xout,
        out_shape=list(out_shape) + list(exchange.out_shape),
        scratch_shapes=list(scratch) + list(exchange.sems),
        compiler_params=_cparams(("arbitrary",) * len(grid)),
    )(*operands, *exchange.ins)
    return res[:n_out], res[n_out:]


def _ffn_fwd(name, x, h, wg, wu, wd, exchange):
    n, d = x.shape
    nb, _, fb = wg.shape
    tm = _tile(n, 512, 16)

    def body(ins, outs, scr):
        x_ref, h_ref, wg_ref, wu_ref, wd_ref = ins
        xo_ref, a_ref, b_ref = outs
        acc, = scr
        j = pl.program_id(1)

        @pl.when(j == 0)
        def _():
            acc[...] = jnp.zeros_like(acc)

        hh = h_ref[...]
        a = _dot(hh, wg_ref[...])
        b = _dot(hh, wu_ref[...])
        a_ref[...] = a.astype(BF16)
        b_ref[...] = b.astype(BF16)
        act = (a * jax.nn.sigmoid(a)) * b
        acc[...] += _dot(act, wd_ref[...])

        @pl.when(j == nb - 1)
        def _():
            xo_ref[...] = x_ref[...] + 0.5 * acc[...]

    row = pl.BlockSpec((tm, d), lambda i, j: (i, 0))
    w_in = pl.BlockSpec((None, d, fb), lambda i, j: (j, 0, 0))
    hid = pl.BlockSpec((None, tm, fb), lambda i, j: (j, i, 0))
    return _hosted_call(
        name, body, (n // tm, nb),
        [row, row, w_in, w_in, pl.BlockSpec((None, fb, d), lambda i, j: (j, 0, 0))],
        [row, hid, hid],
        [jax.ShapeDtypeStruct((n, d), F32), jax.ShapeDtypeStruct((nb, n, fb), BF16),
         jax.ShapeDtypeStruct((nb, n, fb), BF16)],
        [pltpu.VMEM((tm, d), F32)], (x, h, wg, wu, wd), exchange)


def _ffn_bwd_tokens(name, dxb, wg, wu, wd, a, b, exchange):
    n, d = dxb.shape
    nb, fb, _ = wd.shape
    tm = _tile(n, 512, 16)

    def body(ins, outs, scr):
        dx_ref, wg_ref, wu_ref, wd_ref, a_ref, b_ref = ins
        da_ref, db_ref, act_ref, dh_ref = outs
        acc, = scr
        j = pl.program_id(1)

        @pl.when(j == 0)
        def _():
            acc[...] = jnp.zeros_like(acc)

        dact = 0.5 * _dot(dx_ref[...], wd_ref[...], _NT)
        af = a_ref[...].astype(F32)
        bf = b_ref[...].astype(F32)
        sl, dsl = _silu_parts(af)
        act_ref[...] = (sl * bf).astype(BF16)
        d_b = (dact * sl).astype(BF16)
        d_a = (dact * bf * dsl).astype(BF16)
        db_ref[...] = d_b
        da_ref[...] = d_a
        acc[...] += _dot(d_a, wg_ref[...], _NT) + _dot(d_b, wu_ref[...], _NT)

        @pl.when(j == nb - 1)
        def _():
            dh_ref[...] = acc[...]

    row = pl.BlockSpec((tm, d), lambda i, j: (i, 0))
    w_in = pl.BlockSpec((None, d, fb), lambda i, j: (j, 0, 0))
    hid = pl.BlockSpec((None, tm, fb), lambda i, j: (j, i, 0))
    shp = jax.ShapeDtypeStruct((nb, n, fb), BF16)
    return _hosted_call(
        name, body, (n // tm, nb),
        [row, w_in, w_in, pl.BlockSpec((None, fb, d), lambda i, j: (j, 0, 0)), hid, hid],
        [hid, hid, hid, row],
        [shp, shp, shp, jax.ShapeDtypeStruct((n, d), F32)],
        [pltpu.VMEM((tm, d), F32)], (dxb, wg, wu, wd, a, b), exchange)


def _ffn_bwd_weights(name, h, dxb, da, db, act):
    n, d = h.shape
    nb, _, fb = da.shape
    tk = _tile(n, 512, 16)
    nk = n // tk

    def body(h_ref, dx_ref, da_ref, db_ref, act_ref, dwg_ref, dwu_ref, dwd_ref, accg, accu, accd):
        k = pl.program_id(1)

        @pl.when(k == 0)
        def _():
            accg[...] = jnp.zeros_like(accg)
            accu[...] = jnp.zeros_like(accu)
            accd[...] = jnp.zeros_like(accd)

        hh = h_ref[...]
        accg[...] += _dot(hh, da_ref[...], _TN)
        accu[...] += _dot(hh, db_ref[...], _TN)
        accd[...] += _dot(act_ref[...], dx_ref[...], _TN)

        @pl.when(k == nk - 1)
        def _():
            dwg_ref[...] = accg[...].astype(BF16)
            dwu_ref[...] = accu[...].astype(BF16)
            dwd_ref[...] = (0.5 * accd[...]).astype(BF16)

    row = pl.BlockSpec((tk, d), lambda j, k: (k, 0))
    hid = pl.BlockSpec((None, tk, fb), lambda j, k: (j, k, 0))
    w_in = pl.BlockSpec((None, d, fb), lambda j, k: (j, 0, 0))
    w_out = pl.BlockSpec((None, fb, d), lambda j, k: (j, 0, 0))
    return pl.pallas_call(
        body, name=name, grid=(nb, nk),
        in_specs=[row, row, hid, hid, hid], out_specs=[w_in, w_in, w_out],
        out_shape=[jax.ShapeDtypeStruct((nb, d, fb), BF16), jax.ShapeDtypeStruct((nb, d, fb), BF16),
                   jax.ShapeDtypeStruct((nb, fb, d), BF16)],
        scratch_shapes=[pltpu.VMEM((d, fb), F32), pltpu.VMEM((d, fb), F32), pltpu.VMEM((fb, d), F32)],
        compiler_params=_cparams(("parallel", "arbitrary")),
    )(h, dxb, da, db, act)


def _pool_diff(ext_ref, rows, cols, width, t_idx):
    s = ext_ref[POOL_HALO:POOL_HALO + rows, cols]
    for sft in range(1, width):
        s = s + ext_ref[POOL_HALO - sft:POOL_HALO - sft + rows, cols]
    cnt = jnp.minimum(t_idx + 1, width).astype(F32)
    return s / cnt - ext_ref[POOL_HALO:POOL_HALO + rows, cols]


def _pool_fwd(proj, pool_w, pool_scale):
    n = proj.shape[0]
    dp = pool_scale.shape[1]
    c = dp // len(POOL_WINDOWS)
    tm = _tile(n, 512, POOL_HALO)
    hb = tm // POOL_HALO

    def body(u_ref, halo_ref, pw_ref, sc_ref, y_ref, ext):
        i = pl.program_id(0)
        ext[0:POOL_HALO, :] = jnp.where(i > 0, halo_ref[...], 0.0)
        ext[POOL_HALO:, :] = u_ref[...]
        t_idx = i * tm + lax.broadcasted_iota(jnp.int32, (tm, 1), 0)
        for g, width in enumerate(POOL_WINDOWS):
            cols = slice(g * c, (g + 1) * c)
            dgrp = _pool_diff(ext, tm, cols, width, t_idx)
            y_ref[:, cols] = (_dot(dgrp, pw_ref[g]) * sc_ref[:, cols]).astype(BF16)

    return pl.pallas_call(
        body, name="pool_fwd", grid=(n // tm,),
        in_specs=[pl.BlockSpec((tm, dp), lambda i: (i, 0)),
                  pl.BlockSpec((POOL_HALO, dp), lambda i: (jnp.maximum(i * hb - 1, 0), 0)),
                  pl.BlockSpec((len(POOL_WINDOWS), c, c), lambda i: (0, 0, 0)),
                  pl.BlockSpec((1, dp), lambda i: (0, 0))],
        out_specs=pl.BlockSpec((tm, dp), lambda i: (i, 0)),
        out_shape=jax.ShapeDtypeStruct((n, dp), BF16),
        scratch_shapes=[pltpu.VMEM((tm + POOL_HALO, dp), F32)],
        compiler_params=_cparams(("parallel",)),
    )(proj, proj, pool_w, pool_scale)


def _pool_bwd(proj, dymix, pool_w, pool_scale):
    n = proj.shape[0]
    dp = pool_scale.shape[1]
    ng = len(POOL_WINDOWS)
    c = dp // ng
    tm = _tile(n, 512, POOL_HALO)
    hb = tm // POOL_HALO
    nsteps = n // tm
    last_halo = n // POOL_HALO - 1

    def body(u_ref, halo_ref, dy_ref, dyn_ref, pw_ref, sc_ref, du_ref, dpw_ref, dsc_ref, ext, dyext, e_s):
        i = pl.program_id(0)

        @pl.when(i == 0)
        def _():
            dpw_ref[...] = jnp.zeros_like(dpw_ref)
            dsc_ref[...] = jnp.zeros_like(dsc_ref)

        ext[0:POOL_HALO, :] = jnp.where(i > 0, halo_ref[...], 0.0)
        ext[POOL_HALO:, :] = u_ref[...]
        dyext[0:tm, :] = dy_ref[...]
        dyext[tm:, :] = jnp.where(i < nsteps - 1, dyn_ref[...], 0.0)
        t_idx = i * tm + lax.broadcasted_iota(jnp.int32, (tm, 1), 0)
        te_idx = i * tm + lax.broadcasted_iota(jnp.int32, (tm + POOL_HALO, 1), 0)
        for g, width in enumerate(POOL_WINDOWS):
            cols = slice(g * c, (g + 1) * c)
            dgrp = _pool_diff(ext, tm, cols, width, t_idx)
            w_g = pw_ref[g]
            dys = dyext[:, cols] * sc_ref[:, cols]
            ypre = _dot(dgrp, w_g)
            dsc_ref[:, cols] += jnp.sum(dyext[0:tm, cols] * ypre, axis=0, keepdims=True)
            dpw_ref[g] += _dot(dgrp, dys[0:tm], _TN)
            dd = _dot(dys, w_g, _NT)
            e_s[...] = dd / jnp.minimum(te_idx + 1, width).astype(F32)
            acc = e_s[0:tm, :]
            for sft in range(1, width):
                acc = acc + e_s[sft:sft + tm, :]
            du_ref[:, cols] = (acc - dd[0:tm]).astype(BF16)

    return pl.pallas_call(
        body, name="pool_bwd", grid=(nsteps,),
        in_specs=[pl.BlockSpec((tm, dp), lambda i: (i, 0)),
                  pl.BlockSpec((POOL_HALO, dp), lambda i: (jnp.maximum(i * hb - 1, 0), 0)),
                  pl.BlockSpec((tm, dp), lambda i: (i, 0)),
                  pl.BlockSpec((POOL_HALO, dp), lambda i: (jnp.minimum((i + 1) * hb, last_halo), 0)),
                  pl.BlockSpec((ng, c, c), lambda i: (0, 0, 0)),
                  pl.BlockSpec((1, dp), lambda i: (0, 0))],
        out_specs=[pl.BlockSpec((tm, dp), lambda i: (i, 0)),
                   pl.BlockSpec((ng, c, c), lambda i: (0, 0, 0)),
                   pl.BlockSpec((1, dp), lambda i: (0, 0))],
        out_shape=[jax.ShapeDtypeStruct((n, dp), BF16), jax.ShapeDtypeStruct((ng, c, c), F32),
                   jax.ShapeDtypeStruct((1, dp), F32)],
        scratch_shapes=[pltpu.VMEM((tm + POOL_HALO, dp), F32), pltpu.VMEM((tm + POOL_HALO, dp), F32),
                        pltpu.VMEM((tm + POOL_HALO, c), F32)],
        compiler_params=_cparams(("arbitrary",)),
    )(proj, proj, dymix, dymix, pool_w, pool_scale)


def _chunk_cumsum(x):
    row = lax.broadcasted_iota(jnp.int32, x.shape, 0) % CHUNK
    s = 1
    while s < CHUNK:
        x = x + jnp.where(row >= s, pltpu.roll(x, s, 0), 0.0)
        s *= 2
    return x


def _gate_logits(alr_ref, wa_ref, ba_ref):
    z = _dot(alr_ref[...], wa_ref[...]) + ba_ref[...]
    la = (jnp.minimum(z, 0.0) - jnp.log(1.0 + jnp.exp(-jnp.abs(z)))) * (1.0 / GATE_TEMP)
    return z, la


def _gla_dims(proj, head_norm):
    n, pw = proj.shape
    dv4 = head_norm.shape[1]
    dk4 = dv4 // 2
    return n, pw, dv4, dk4, dk4 // HEADS, dv4 // HEADS


def _gla_in_specs(t, dk4, dv4, rev):
    alr_blk = (2 * dv4 + 2 * dv4) // LANE
    return [pl.BlockSpec((t, dk4), lambda i: (rev(i), 2)),
            pl.BlockSpec((t, dk4), lambda i: (rev(i), 3)),
            pl.BlockSpec((t, dv4), lambda i: (rev(i), 2)),
            pl.BlockSpec((t, dv4), lambda i: (rev(i), 3)),
            pl.BlockSpec((t, LANE), lambda i: (rev(i), alr_blk))]


def _gla_fwd(proj, wa2p, b_a, head_norm):
    n, _, dv4, dk4, dk, dv = _gla_dims(proj, head_norm)
    t = _tile(n, GLA_ROWS, CHUNK)
    nc = t // CHUNK
    qscale = dk ** -0.5

    def body(q_ref, k_ref, v_ref, g_ref, alr_ref, wa_ref, ba_ref, hn_ref, y_ref, st_ref, state, cum_s):
        i = pl.program_id(0)

        @pl.when(i == 0)
        def _():
            state[...] = jnp.zeros_like(state)

        _, la = _gate_logits(alr_ref, wa_ref, ba_ref)
        cum_s[...] = _chunk_cumsum(la)
        for c in range(nc):
            rows = slice(c * CHUNK, (c + 1) * CHUNK)
            bend = cum_s[c * CHUNK + CHUNK - 1:(c + 1) * CHUNK, :]
            dec = jnp.exp(bend - cum_s[rows, :])
            e = jnp.exp(bend)
            kd = (k_ref[rows, :] * dec).astype(BF16)
            qs = (q_ref[rows, :] * qscale).astype(BF16)
            vv = v_ref[rows, :].astype(BF16)
            for h in range(HEADS):
                hk = slice(h * dk, (h + 1) * dk)
                hv = slice(h * dv, (h + 1) * dv)
                s_new = state[h] * e[:, hk] + _dot(vv[:, hv], kd[:, hk], _TN)
                state[h] = s_new
                st_ref[c, h] = s_new
                o = _dot(qs[:, hk], s_new, _NT)
                r = lax.rsqrt(jnp.mean(o * o, axis=-1, keepdims=True) + RMS_EPS)
                gg = g_ref[rows, hv]
                y_ref[rows, hv] = (((o * r) * hn_ref[:, hv]) * (gg * jax.nn.sigmoid(gg))).astype(BF16)

    full = lambda shape: pl.BlockSpec(shape, lambda i: tuple(0 for _ in shape))
    return pl.pallas_call(
        body, name="gla_fwd", grid=(n // t,),
        in_specs=_gla_in_specs(t, dk4, dv4, lambda i: i)
        + [full((LANE, dk4)), full((1, dk4)), full((1, dv4))],
        out_specs=[pl.BlockSpec((t, dv4), lambda i: (i, 0)),
                   pl.BlockSpec((nc, HEADS, dv, dk), lambda i: (i, 0, 0, 0))],
        out_shape=[jax.ShapeDtypeStruct((n, dv4), BF16),
                   jax.ShapeDtypeStruct((n // CHUNK, HEADS, dv, dk), F32)],
        scratch_shapes=[pltpu.VMEM((HEADS, dv, dk), F32), pltpu.VMEM((t, dk4), F32)],
        compiler_params=_cparams(("arbitrary",)),
    )(proj, proj, proj, proj, proj, wa2p, b_a, head_norm)


def _gla_bwd(proj, dymix, du, states, wa2p, b_a, head_norm):
    n, pw, dv4, dk4, dk, dv = _gla_dims(proj, head_norm)
    t = _tile(n, GLA_ROWS, CHUNK)
    nc = t // CHUNK
    nsteps = n // t
    qscale = dk ** -0.5
    rev = lambda i: nsteps - 1 - i

    def body(q_ref, k_ref, v_ref, g_ref, alr_ref, dy_ref, du_ref, st_ref, prev_ref, wa_ref, ba_ref, hn_ref,
             dp_ref, dwa_ref, dba_ref, dhn_ref, carry, cum_s, dz_s):
        i = pl.program_id(0)

        @pl.when(i == 0)
        def _():
            carry[...] = jnp.zeros_like(carry)
            dwa_ref[...] = jnp.zeros_like(dwa_ref)
            dba_ref[...] = jnp.zeros_like(dba_ref)
            dhn_ref[...] = jnp.zeros_like(dhn_ref)

        first_step = i == nsteps - 1
        z, la = _gate_logits(alr_ref, wa_ref, ba_ref)
        one_minus_sig = 1.0 - jax.nn.sigmoid(z)
        cum_s[...] = _chunk_cumsum(la)
        dp_ref[:, 0:dv4] = du_ref[...]
        for c in reversed(range(nc)):
            rows = slice(c * CHUNK, (c + 1) * CHUNK)
            bend = cum_s[c * CHUNK + CHUNK - 1:(c + 1) * CHUNK, :]
            dec = jnp.exp(bend - cum_s[rows, :])
            e = jnp.exp(bend)
            kd_f = k_ref[rows, :] * dec
            kd = kd_f.astype(BF16)
            qs = (q_ref[rows, :] * qscale).astype(BF16)
            vv = v_ref[rows, :].astype(BF16)
            dkd_parts, dee_parts = [], []
            for h in range(HEADS):
                hk = slice(h * dk, (h + 1) * dk)
                hv = slice(h * dv, (h + 1) * dv)
                s_n = st_ref[c, h]
                if c > 0:
                    s_prev = st_ref[c - 1, h]
                else:
                    s_prev = jnp.where(first_step, 0.0, prev_ref[0, h])
                o = _dot(qs[:, hk], s_n, _NT)
                r = lax.rsqrt(jnp.mean(o * o, axis=-1, keepdims=True) + RMS_EPS)
                oh = o * r
                gg = g_ref[rows, hv]
                sl, dsl = _silu_parts(gg)
                dyh = dy_ref[rows, hv]
                hn = hn_ref[:, hv]
                tt = dyh * sl
                dhn_ref[:, hv] += jnp.sum(tt * oh, axis=0, keepdims=True)
                dg = dyh * (oh * hn) * dsl
                tt = tt * hn
                do = (r * (tt - oh * jnp.mean(tt * oh, axis=-1, keepdims=True))).astype(BF16)
                g_n = carry[h] + _dot(do, qs[:, hk], _TN)
                dq = _dot(do, s_n) * qscale
                dkd = _dot(vv[:, hv], g_n)
                dvh = _dot(kd[:, hk], g_n, _NT)
                de = jnp.sum(g_n * s_prev, axis=0, keepdims=True)
                carry[h] = g_n * e[:, hk]
                dp_ref[rows, dv4 + h * dk:dv4 + (h + 1) * dk] = dq.astype(BF16)
                dp_ref[rows, 2 * dv4 + h * dv:2 * dv4 + (h + 1) * dv] = dvh.astype(BF16)
                dp_ref[rows, 3 * dv4 + h * dv:3 * dv4 + (h + 1) * dv] = dg.astype(BF16)
                dkd_parts.append(dkd)
                dee_parts.append(de * e[:, hk])
            dkd_c = jnp.concatenate(dkd_parts, axis=1)
            dee = jnp.concatenate(dee_parts, axis=1)
            dp_ref[rows, dv4 + dk4:dv4 + 2 * dk4] = (dkd_c * dec).astype(BF16)
            w = dkd_c * kd_f
            dla = (_chunk_cumsum(w) - w) + dee
            dz_s[rows, :] = dla * (1.0 / GATE_TEMP)
        dz = dz_s[...] * one_minus_sig
        dp_ref[:, 4 * dv4:4 * dv4 + LANE] = _dot(dz, wa_ref[...], _NT).astype(BF16)
        dwa_ref[...] += _dot(alr_ref[...], dz, _TN)
        dba_ref[...] += jnp.sum(dz, axis=0, keepdims=True)

    full = lambda shape: pl.BlockSpec(shape, lambda i: tuple(0 for _ in shape))
    return pl.pallas_call(
        body, name="gla_bwd", grid=(nsteps,),
        in_specs=_gla_in_specs(t, dk4, dv4, rev)
        + [pl.BlockSpec((t, dv4), lambda i: (rev(i), 1)),
           pl.BlockSpec((t, dv4), lambda i: (rev(i), 0)),
           pl.BlockSpec((nc, HEADS, dv, dk), lambda i: (rev(i), 0, 0, 0)),
           pl.BlockSpec((1, HEADS, dv, dk), lambda i: (jnp.maximum(rev(i) * nc - 1, 0), 0, 0, 0)),
           full((LANE, dk4)), full((1, dk4)), full((1, dv4))],
        out_specs=[pl.BlockSpec((t, pw), lambda i: (rev(i), 0)),
                   full((LANE, dk4)), full((1, dk4)), full((1, dv4))],
        out_shape=[jax.ShapeDtypeStruct((n, pw), BF16), jax.ShapeDtypeStruct((LANE, dk4), F32),
                   jax.ShapeDtypeStruct((1, dk4), F32), jax.ShapeDtypeStruct((1, dv4), F32)],
        scratch_shapes=[pltpu.VMEM((HEADS, dv, dk), F32), pltpu.VMEM((t, dk4), F32),
                        pltpu.VMEM((t, dk4), F32)],
        compiler_params=_cparams(("arbitrary",)),
    )(proj, proj, proj, proj, proj, dymix, du, states, states, wa2p, b_a, head_norm)


def _softmax_rows(q, k, scale):
    s = _dot(q, k, _NT) * scale
    p = jnp.exp(s - jnp.max(s, axis=-1, keepdims=True))
    return p / jnp.sum(p, axis=-1, keepdims=True)


def _attn_fwd(q, kv):
    n, d = q.shape
    m = kv.shape[0]
    dh = d // HEADS
    tm = _tile(n, 512, 16)
    scale = dh ** -0.5

    def body(q_ref, k_ref, v_ref, o_ref):
        for h in range(HEADS):
            hs = slice(h * dh, (h + 1) * dh)
            p = _softmax_rows(q_ref[:, hs], k_ref[:, hs], scale)
            o_ref[:, hs] = _dot(p, v_ref[:, hs]).astype(BF16)

    return pl.pallas_call(
        body, name="attn_fwd", grid=(n // tm,),
        in_specs=[pl.BlockSpec((tm, d), lambda i: (i, 0)), pl.BlockSpec((m, d), lambda i: (0, 0)),
                  pl.BlockSpec((m, d), lambda i: (0, 1))],
        out_specs=pl.BlockSpec((tm, d), lambda i: (i, 0)),
        out_shape=jax.ShapeDtypeStruct((n, d), BF16),
        compiler_params=_cparams(("parallel",)),
    )(q, kv, kv)


def _attn_bwd(q, kv, do):
    n, d = q.shape
    m = kv.shape[0]
    dh = d // HEADS
    tm = _tile(n, 512, 16)
    scale = dh ** -0.5

    def body(q_ref, k_ref, v_ref, do_ref, dq_ref, dk_ref, dv_ref):
        i = pl.program_id(0)

        @pl.when(i == 0)
        def _():
            dk_ref[...] = jnp.zeros_like(dk_ref)
            dv_ref[...] = jnp.zeros_like(dv_ref)

        for h in range(HEADS):
            hs = slice(h * dh, (h + 1) * dh)
            qh, kh, vh, doh = q_ref[:, hs], k_ref[:, hs], v_ref[:, hs], do_ref[:, hs]
            p = _softmax_rows(qh, kh, scale)
            dv_ref[:, hs] += _dot(p, doh, _TN)
            dp = _dot(doh, vh, _NT)
            ds = p * (dp - jnp.sum(dp * p, axis=-1, keepdims=True)) * scale
            dq_ref[:, hs] = _dot(ds, kh).astype(BF16)
            dk_ref[:, hs] += _dot(ds, qh, _TN)

    row = pl.BlockSpec((tm, d), lambda i: (i, 0))
    memb = pl.BlockSpec((m, d), lambda i: (0, 0))
    return pl.pallas_call(
        body, name="attn_bwd", grid=(n // tm,),
        in_specs=[row, memb, pl.BlockSpec((m, d), lambda i: (0, 1)), row],
        out_specs=[row, memb, memb],
        out_shape=[jax.ShapeDtypeStruct((n, d), BF16), jax.ShapeDtypeStruct((m, d), F32),
                   jax.ShapeDtypeStruct((m, d), F32)],
        compiler_params=_cparams(("arbitrary",)),
    )(q, kv, kv, do)


def _adamw(name, w, m, v, parts, own_block):
    r, c = w.shape
    tr = _tile(r, max(16, (256 * 1024) // c), 16)
    row = pl.BlockSpec((tr, c), lambda i, o: (i, 0))
    ops, specs = [w, m, v], [row, row, row]
    for p in parts:
        if p.ndim == 2:
            ops.append(p)
            specs.append(row)
        elif p.shape[0] == 4:
            ops.append(p)
            specs.append(pl.BlockSpec((None, tr, c), lambda i, o: (o[0], i, 0)))
        else:
            for s in range(p.shape[0]):
                ops.append(p)
                specs.append(pl.BlockSpec((None, tr, c), lambda i, o, s=s: (s, i, 0)))
    n_parts = len(ops) - 3
    c1 = 1.0 - ADAM_B1 ** ADAM_STEP
    c2 = 1.0 - ADAM_B2 ** ADAM_STEP

    def body(o_ref, *refs):
        del o_ref
        w_ref, m_ref, v_ref = refs[:3]
        g_refs = refs[3:3 + n_parts]
        go_ref, d_ref, mo_ref, vo_ref = refs[3 + n_parts:]
        g = g_refs[0][...].astype(F32)
        for g_ref in g_refs[1:]:
            g = g + g_ref[...].astype(F32)
        m_new = ADAM_B1 * m_ref[...] + (1.0 - ADAM_B1) * g
        v_new = ADAM_B2 * v_ref[...] + (1.0 - ADAM_B2) * (g * g)
        m_hat = m_new / c1
        v_hat = v_new / c2
        go_ref[...] = g
        d_ref[...] = -ADAM_LR * (m_hat / (jnp.sqrt(v_hat) + ADAM_EPS) + ADAM_WD * w_ref[...])
        mo_ref[...] = m_new
        vo_ref[...] = v_new

    shp = jax.ShapeDtypeStruct((r, c), F32)
    return pl.pallas_call(
        body, name=name,
        grid_spec=pltpu.PrefetchScalarGridSpec(
            num_scalar_prefetch=1, grid=(r // tr,), in_specs=specs, out_specs=[row] * 4),
        out_shape=[shp] * 4,
        compiler_params=_cparams(("parallel",)),
    )(own_block, *ops)


def _pair_add(name, g, recvd, core):
    _, r, c = g.shape
    tr = _tile(r, max(16, (256 * 1024) // c), 16)

    def body(core_ref, a_ref, b_ref, o_ref):
        del core_ref
        o_ref[...] = (a_ref[...].astype(F32) + b_ref[...].astype(F32)).astype(o_ref.dtype)

    blk = pl.BlockSpec((None, tr, c), lambda s, i, core_ref: (s, i, 0))
    mine = pl.BlockSpec((None, tr, c), lambda s, i, core_ref: (2 * s + core_ref[0], i, 0))
    return pl.pallas_call(
        body, name=name,
        grid_spec=pltpu.PrefetchScalarGridSpec(
            num_scalar_prefetch=1, grid=(4, r // tr), in_specs=[mine, blk], out_specs=blk),
        out_shape=jax.ShapeDtypeStruct(recvd.shape, g.dtype),
        compiler_params=_cparams(("parallel", "parallel")),
    )(core, g, recvd)


def _position():
    return lax.axis_index("x"), lax.axis_index("y"), lax.axis_index("c")


_HBM = pl.BlockSpec(memory_space=pltpu.HBM)

_Exchange = collections.namedtuple("_Exchange", "ins out_shape sems start finish")
_NO_EXCHANGE = _Exchange((), (), (), lambda ins, outs, sems: None, lambda ins, outs, sems: None)


def _all_gather(name, shards):
    n = len(shards)

    def body(*refs):
        ins, outs = refs[:n], refs[n:2 * n]
        send_sems, recv_sems, local_sems = refs[2 * n:]
        x, y, c = _position()
        me, sibling = (x, y, c), (x, y, 1 - c)
        chips = [(1 - x, y), (x, 1 - y), (1 - x, 1 - y)]

        def block(ref, px, py, pc):
            return ref.at[4 * px + 2 * py + pc]

        def copy(a, k, owner, to, src=None):
            return pltpu.make_async_remote_copy(
                src_ref=block(outs[a], *owner) if src is None else src, dst_ref=block(outs[a], *owner),
                send_sem=send_sems.at[7 * a + k], recv_sem=recv_sems.at[7 * a + k],
                device_id=to, device_id_type=MESH)

        started = []
        for a in range(n):
            mine = pltpu.make_async_copy(ins[a], block(outs[a], *me), local_sems.at[a])
            mine.start()
            started.append(mine)
        sends = []
        for a in range(n):
            sends.append(copy(a, 0, me, sibling, src=ins[a]))
            sends += [copy(a, 1 + j, me, (*chip, c), src=ins[a]) for j, chip in enumerate(chips)]
        for cp in sends:
            cp.start()
        for j, chip in enumerate(chips):
            for a in range(n):
                copy(a, 1 + j, (*chip, c), me).wait_recv()
                fwd = copy(a, 4 + j, (*chip, c), sibling)
                fwd.start()
                sends.append(fwd)
        for a in range(n):
            copy(a, 0, sibling, me).wait_recv()
            for j, chip in enumerate(chips):
                copy(a, 4 + j, (*chip, 1 - c), me).wait_recv()
        for cp in sends:
            cp.wait_send()
        for mine in started:
            mine.wait()

    return pl.pallas_call(
        body, name=name,
        in_specs=[_HBM] * n, out_specs=[_HBM] * n,
        out_shape=[jax.ShapeDtypeStruct((N_DEV,) + s.shape, s.dtype) for s in shards],
        scratch_shapes=[pltpu.SemaphoreType.DMA((7 * n,)), pltpu.SemaphoreType.DMA((7 * n,)),
                        pltpu.SemaphoreType.DMA((n,))],
    )(*shards)


def _reduce_scatter_pair(name, grads):
    n = len(grads)

    def body(*refs):
        ins, recvd = refs[:n], refs[n:2 * n]
        send_sems, recv_sems = refs[2 * n:]
        x, y, c = _position()
        for a in range(n):
            for chip in range(4):
                pltpu.make_async_remote_copy(
                    src_ref=ins[a].at[2 * chip + (1 - c)], dst_ref=recvd[a].at[chip],
                    send_sem=send_sems.at[a], recv_sem=recv_sems.at[a],
                    device_id=(x, y, 1 - c), device_id_type=MESH).start()
        for a in range(n):
            pltpu.make_async_remote_copy(
                src_ref=recvd[a], dst_ref=recvd[a], send_sem=send_sems.at[a], recv_sem=recv_sems.at[a],
                device_id=(x, y, 1 - c), device_id_type=MESH).wait()

    return pl.pallas_call(
        body, name=name, in_specs=[_HBM] * n, out_specs=[_HBM] * n,
        out_shape=[jax.ShapeDtypeStruct((4,) + g.shape[1:], g.dtype) for g in grads],
        scratch_shapes=[pltpu.SemaphoreType.DMA((n,)), pltpu.SemaphoreType.DMA((n,))],
    )(*grads)


def _chip_scatter_exchange(sums):
    n = len(sums)
    offsets = [(1, 0), (0, 1), (1, 1)]

    def start(ins, recvd, sems):
        send_sems, recv_sems = sems
        x, y, c = _position()
        for a in range(n):
            for r, (ox, oy) in enumerate(offsets):
                px = 1 - x if ox else x
                py = 1 - y if oy else y
                pltpu.make_async_remote_copy(
                    src_ref=ins[a].at[2 * px + py], dst_ref=recvd[a].at[r],
                    send_sem=send_sems.at[a], recv_sem=recv_sems.at[a],
                    device_id=(px, py, c), device_id_type=MESH).start()

    def finish(ins, recvd, sems):
        del ins
        send_sems, recv_sems = sems
        x, y, c = _position()
        for a in range(n):
            pltpu.make_async_remote_copy(
                src_ref=recvd[a], dst_ref=recvd[a], send_sem=send_sems.at[a], recv_sem=recv_sems.at[a],
                device_id=(x, y, c), device_id_type=MESH).wait()

    return _Exchange(
        ins=tuple(sums),
        out_shape=tuple(jax.ShapeDtypeStruct((3,) + s.shape[1:], s.dtype) for s in sums),
        sems=(pltpu.SemaphoreType.DMA((n,)), pltpu.SemaphoreType.DMA((n,))),
        start=start, finish=finish)


def _gather_spread_exchange(shards):
    n = len(shards)

    def peers():
        x, y, c = _position()
        return (x, y, c), [(x, y, 1 - c), (1 - x, y, c), (x, 1 - y, c), (1 - x, 1 - y, c)]

    def block(ref, px, py, pc):
        return ref.at[4 * px + 2 * py + pc]

    def copy(ins, outs, sems, a, k, owner, to):
        send_sems, recv_sems, _ = sems
        return pltpu.make_async_remote_copy(
            src_ref=ins[a], dst_ref=block(outs[a], *owner),
            send_sem=send_sems.at[4 * a + k], recv_sem=recv_sems.at[4 * a + k],
            device_id=to, device_id_type=MESH)

    def start(ins, outs, sems):
        me, others = peers()
        for a in range(n):
            pltpu.make_async_copy(ins[a], block(outs[a], *me), sems[2].at[a]).start()
            for k, to in enumerate(others):
                copy(ins, outs, sems, a, k, me, to).start()

    def finish(ins, outs, sems):
        me, others = peers()
        for a in range(n):
            for k, peer in enumerate(others):
                cp = copy(ins, outs, sems, a, k, peer, peer)
                cp.wait_recv()
                cp.wait_send()
            pltpu.make_async_copy(ins[a], block(outs[a], *me), sems[2].at[a]).wait()

    return _Exchange(
        ins=tuple(shards),
        out_shape=tuple(jax.ShapeDtypeStruct((N_DEV,) + s.shape, s.dtype) for s in shards),
        sems=(pltpu.SemaphoreType.DMA((4 * n,)), pltpu.SemaphoreType.DMA((4 * n,)),
              pltpu.SemaphoreType.DMA((n,))),
        start=start, finish=finish)


def _gather_forward(name, partial):
    n = len(partial)

    def body(*refs):
        ins, outs = refs[:n], refs[n:2 * n]
        send_sems, recv_sems = refs[2 * n:]
        x, y, c = _position()
        chips = [(1 - x, y), (x, 1 - y), (1 - x, 1 - y)]

        def copy(a, j, pc):
            blk = 4 * chips[j][0] + 2 * chips[j][1] + pc
            return pltpu.make_async_remote_copy(
                src_ref=ins[a].at[blk], dst_ref=outs[a].at[blk],
                send_sem=send_sems.at[3 * a + j], recv_sem=recv_sems.at[3 * a + j],
                device_id=(x, y, 1 - c), device_id_type=MESH)

        for a in range(n):
            for j in range(3):
                copy(a, j, c).start()
        for a in range(n):
            for j in range(3):
                copy(a, j, 1 - c).wait_recv()
                copy(a, j, c).wait_send()

    return pl.pallas_call(
        body, name=name, in_specs=[_HBM] * n, out_specs=[_HBM] * n,
        out_shape=[jax.ShapeDtypeStruct(p.shape, p.dtype) for p in partial],
        input_output_aliases={a: a for a in range(n)},
        scratch_shapes=[pltpu.SemaphoreType.DMA((3 * n,)), pltpu.SemaphoreType.DMA((3 * n,))],
    )(*partial)


def _run_exchange(name, exchange):
    n_in, n_out = len(exchange.ins), len(exchange.out_shape)

    def body(*refs):
        ins, outs, sems = refs[:n_in], refs[n_in:n_in + n_out], refs[n_in + n_out:]
        exchange.start(ins, outs, sems)
        exchange.finish(ins, outs, sems)

    return pl.pallas_call(
        body, name=name, in_specs=[_HBM] * n_in, out_specs=[_HBM] * n_out,
        out_shape=list(exchange.out_shape), scratch_shapes=list(exchange.sems),
    )(*exchange.ins)


def _all_reduce_small(vec):
    r = vec.shape[0]

    def body(v_ref, o_ref, gbuf, send_sems, recv_sems):
        x, y, c = _position()
        me = 4 * x + 2 * y + c
        gbuf[me] = v_ref[...]
        copies = []
        for k in range(1, N_DEV):
            ox, oy, oc = (k >> 2) & 1, (k >> 1) & 1, k & 1
            peer = (1 - x if ox else x, 1 - y if oy else y, 1 - c if oc else c)
            cp = pltpu.make_async_remote_copy(
                src_ref=gbuf.at[me], dst_ref=gbuf.at[me], send_sem=send_sems.at[k - 1],
                recv_sem=recv_sems.at[k - 1], device_id=peer, device_id_type=MESH)
            cp.start()
            copies.append(cp)
        for cp in copies:
            cp.wait()
        total = gbuf[0]
        for j in range(1, N_DEV):
            total = total + gbuf[j]
        o_ref[...] = total

    return pl.pallas_call(
        body, name="all_reduce_small",
        in_specs=[pl.BlockSpec(memory_space=pltpu.VMEM)],
        out_specs=pl.BlockSpec(memory_space=pltpu.VMEM),
        out_shape=jax.ShapeDtypeStruct(vec.shape, F32),
        scratch_shapes=[pltpu.VMEM((N_DEV, r, LANE), F32), pltpu.SemaphoreType.DMA((N_DEV - 1,)),
                        pltpu.SemaphoreType.DMA((N_DEV - 1,))],
    )(vec)


def _cols_from_blocks(g):
    nb, r, cs = g.shape
    return jnp.transpose(g, (1, 0, 2)).reshape(r, nb * cs)


def _cols_to_blocks(w):
    r, cfull = w.shape
    return jnp.transpose(w.reshape(r, N_DEV, cfull // N_DEV), (1, 0, 2))


def kernel(x, mem, ffn1_norm, ffn1_w_gate, ffn1_w_up, ffn1_w_down, mix_norm, w_in, pool_w, pool_scale, gla_w_a2, gla_b_a, gla_head_norm, w_out, xattn_norm, mem_norm, xattn_w_q, xattn_w_kv, xattn_w_o, ffn2_norm, ffn2_w_gate, ffn2_w_up, ffn2_w_down, final_norm, loss_target, m_ffn1_norm, m_ffn1_w_gate, m_ffn1_w_up, m_ffn1_w_down, m_mix_norm, m_w_in, m_pool_w, m_pool_scale, m_gla_w_a2, m_gla_b_a, m_gla_head_norm, m_w_out, m_xattn_norm, m_mem_norm, m_xattn_w_q, m_xattn_w_kv, m_xattn_w_o, m_ffn2_norm, m_ffn2_w_gate, m_ffn2_w_up, m_ffn2_w_down, m_final_norm, v_ffn1_norm, v_ffn1_w_gate, v_ffn1_w_up, v_ffn1_w_down, v_mix_norm, v_w_in, v_pool_w, v_pool_scale, v_gla_w_a2, v_gla_b_a, v_gla_head_norm, v_w_out, v_xattn_norm, v_mem_norm, v_xattn_w_q, v_xattn_w_kv, v_xattn_w_o, v_ffn2_norm, v_ffn2_w_gate, v_ffn2_w_up, v_ffn2_w_down, v_final_norm):
    weights = dict(ffn1_norm=ffn1_norm, ffn1_w_gate=ffn1_w_gate, ffn1_w_up=ffn1_w_up, ffn1_w_down=ffn1_w_down, mix_norm=mix_norm, w_in=w_in, pool_w=pool_w, pool_scale=pool_scale, gla_w_a2=gla_w_a2, gla_b_a=gla_b_a, gla_head_norm=gla_head_norm, w_out=w_out, xattn_norm=xattn_norm, mem_norm=mem_norm, xattn_w_q=xattn_w_q, xattn_w_kv=xattn_w_kv, xattn_w_o=xattn_w_o, ffn2_norm=ffn2_norm, ffn2_w_gate=ffn2_w_gate, ffn2_w_up=ffn2_w_up, ffn2_w_down=ffn2_w_down, final_norm=final_norm)
    mom1 = dict(ffn1_norm=m_ffn1_norm, ffn1_w_gate=m_ffn1_w_gate, ffn1_w_up=m_ffn1_w_up, ffn1_w_down=m_ffn1_w_down, mix_norm=m_mix_norm, w_in=m_w_in, pool_w=m_pool_w, pool_scale=m_pool_scale, gla_w_a2=m_gla_w_a2, gla_b_a=m_gla_b_a, gla_head_norm=m_gla_head_norm, w_out=m_w_out, xattn_norm=m_xattn_norm, mem_norm=m_mem_norm, xattn_w_q=m_xattn_w_q, xattn_w_kv=m_xattn_w_kv, xattn_w_o=m_xattn_w_o, ffn2_norm=m_ffn2_norm, ffn2_w_gate=m_ffn2_w_gate, ffn2_w_up=m_ffn2_w_up, ffn2_w_down=m_ffn2_w_down, final_norm=m_final_norm)
    mom2 = dict(ffn1_norm=v_ffn1_norm, ffn1_w_gate=v_ffn1_w_gate, ffn1_w_up=v_ffn1_w_up, ffn1_w_down=v_ffn1_w_down, mix_norm=v_mix_norm, w_in=v_w_in, pool_w=v_pool_w, pool_scale=v_pool_scale, gla_w_a2=v_gla_w_a2, gla_b_a=v_gla_b_a, gla_head_norm=v_gla_head_norm, w_out=v_w_out, xattn_norm=v_xattn_norm, mem_norm=v_mem_norm, xattn_w_q=v_xattn_w_q, xattn_w_kv=v_xattn_w_kv, xattn_w_o=v_xattn_w_o, ffn2_norm=v_ffn2_norm, ffn2_w_gate=v_ffn2_w_gate, ffn2_w_up=v_ffn2_w_up, ffn2_w_down=v_ffn2_w_down, final_norm=v_final_norm)
    order = list(weights.keys())

    n, d = x.shape[1], x.shape[2]
    mlen = mem.shape[1]
    x0 = x.reshape(n, d)
    memf = mem.reshape(mlen, d)
    target = loss_target.reshape(n, d)
    dpool = d // 2
    in_cols = w_in.shape[2] * N_DEV
    proj_cols = 2 * d + LANE
    rank = gla_w_a2.shape[1]

    def shard(name):
        return weights[name][0].astype(BF16)

    wg1, wu1, wd1 = _all_gather("ag_ffn1", [shard("ffn1_w_gate"), shard("ffn1_w_up"), shard("ffn1_w_down")])
    later = ["w_in", "pool_w", "gla_w_a2", "w_out", "xattn_w_q", "xattn_w_kv", "xattn_w_o",
             "ffn2_w_gate", "ffn2_w_up", "ffn2_w_down"]
    h1 = _rms_fwd("rms_ffn1", x0, ffn1_norm)
    (x1, a1, b1), spread = _ffn_fwd(
        "ffn1_fwd", x0, h1, wg1, wu1, wd1, _gather_spread_exchange([shard(k) for k in later]))
    win_g, pw_g, wa2_g, wout_g, wq_g, wkv_g, wo_g, wg2, wu2, wd2 = _gather_forward("ag_forward", spread)

    winp = jnp.pad(_cols_from_blocks(win_g), ((0, 0), (0, proj_cols - in_cols)))
    pw = jnp.transpose(pw_g, (1, 0, 2, 3)).reshape(len(POOL_WINDOWS), dpool // 4, dpool // 4)
    wa2p = jnp.pad(_cols_from_blocks(wa2_g), ((0, LANE - rank), (0, 0)))
    wout = wout_g.reshape(d, d)
    wq = wq_g.reshape(d, d)
    wkv = _cols_from_blocks(wkv_g)
    wo = wo_g.reshape(d, d)
    fnorm = final_norm.reshape(1, d)

    h2 = _rms_fwd("rms_mix", x1, mix_norm)
    proj = _mm_nn("mix_in", [(h2, winp)], F32, tn=1408)
    ypool = _pool_fwd(proj, pw, pool_scale)
    ygla, states = _gla_fwd(proj, wa2p, gla_b_a, gla_head_norm)
    x2 = _mm_nn("mix_out", [(ypool, wout[:dpool]), (ygla, wout[dpool:])], F32, res=x1, tk=1024)
    h3 = _rms_fwd("rms_xattn", x2, xattn_norm)
    mh = _rms_fwd("rms_mem", memf, mem_norm)
    q = _mm_nn("xattn_q", [(h3, wq)], BF16)
    kv = _mm_nn("xattn_kv", [(mh, wkv)], BF16)
    att = _attn_fwd(q, kv)
    x3 = _mm_nn("xattn_o", [(att, wo)], F32, res=x2)
    h4 = _rms_fwd("rms_ffn2", x3, ffn2_norm)
    (x4, a2, b2), _ = _ffn_fwd("ffn2_fwd", x3, h4, wg2, wu2, wd2, _NO_EXCHANGE)
    dx4, dx4b, g_final, loss_part = _loss_head(x4, target, fnorm)

    grads = {}
    core = lax.axis_index("c").astype(jnp.int32).reshape(1)
    chip = (2 * lax.axis_index("x") + lax.axis_index("y")).astype(jnp.int32).reshape(1)
    chip_sums = {}

    def pair_stage(tag, names):
        blocks = [grads[k].reshape(N_DEV, -1, grads[k].shape[-1]) for k in names]
        recvd = _reduce_scatter_pair("rs_pair_" + tag, blocks)
        for k, g, r in zip(names, blocks, recvd):
            chip_sums[k] = _pair_add("rs_add_" + k, g, r, core)

    (da2, db2, act2, dh4), _ = _ffn_bwd_tokens("ffn2_bwd_tokens", dx4b, wg2, wu2, wd2, a2, b2, _NO_EXCHANGE)
    grads["ffn2_w_gate"], grads["ffn2_w_up"], grads["ffn2_w_down"] = _ffn_bwd_weights(
        "ffn2_bwd_weights", h4, dx4b, da2, db2, act2)
    pair_stage("ffn2", ["ffn2_w_gate", "ffn2_w_up", "ffn2_w_down"])
    dx3, dx3b, g_ffn2_norm = _rms_bwd("rms_ffn2_bwd", x3, ffn2_norm, dh4, dx4)

    datt = _mm_nt("xattn_do", [(dx3b, wo)], BF16)
    grads["xattn_w_o"] = _mm_tn("xattn_dwo", [(att, dx3b)], BF16).reshape(N_DEV, d // N_DEV, d)
    dq, dk, dv = _attn_bwd(q, kv, datt)
    dh3 = _mm_nt("xattn_dh", [(dq, wq)], F32)
    grads["xattn_w_q"] = _mm_tn("xattn_dwq", [(h3, dq)], BF16).reshape(N_DEV, d // N_DEV, d)
    dx2, dx2b, g_xattn_norm = _rms_bwd("rms_xattn_bwd", x2, xattn_norm, dh3, dx3)
    dkv = jnp.concatenate([dk, dv], axis=1)
    dmh = _mm_nt("xattn_dmh", [(dkv, wkv)], F32)
    grads["xattn_w_kv"] = _cols_to_blocks(_mm_tn("xattn_dwkv", [(mh, dkv)], BF16))
    pair_stage("xattn", ["xattn_w_q", "xattn_w_kv", "xattn_w_o"])
    _, _, g_mem_norm = _rms_bwd("rms_mem_bwd", memf, mem_norm, dmh)

    dymix = _mm_nt("mix_dy", [(dx2b, wout)], F32)
    grads["w_out"] = jnp.concatenate(
        [_mm_tn("mix_dwout_pool", [(ypool, dx2b)], BF16), _mm_tn("mix_dwout_gla", [(ygla, dx2b)], BF16)],
        axis=0).reshape(N_DEV, d // N_DEV, d)
    du, g_pool_w, g_pool_scale = _pool_bwd(proj, dymix, pw, pool_scale)
    dproj, g_wa2p, g_b_a, g_head_norm = _gla_bwd(proj, dymix, du, states, wa2p, gla_b_a, gla_head_norm)
    dh2 = _mm_nt("mix_dh", [(dproj, winp)], F32, tk=1408)
    grads["w_in"] = _cols_to_blocks(_mm_tn("mix_dwin", [(h2, dproj)], BF16, tn=1408)[:, :in_cols])
    grads["pool_w"] = jnp.transpose(
        g_pool_w.reshape(len(POOL_WINDOWS), N_DEV, dpool // 4 // N_DEV, dpool // 4), (1, 0, 2, 3))
    grads["gla_w_a2"] = _cols_to_blocks(g_wa2p[:rank])
    pair_stage("mix", ["w_in", "pool_w", "gla_w_a2", "w_out"])
    dx1, dx1b, g_mix_norm = _rms_bwd("rms_mix_bwd", x1, mix_norm, dh2, dx2)

    early = ["ffn2_w_gate", "ffn2_w_up", "ffn2_w_down", "xattn_w_q", "xattn_w_kv", "xattn_w_o",
             "w_in", "pool_w", "gla_w_a2", "w_out"]
    (da1, db1, act1, dh1), early_recvd = _ffn_bwd_tokens(
        "ffn1_bwd_tokens", dx1b, wg1, wu1, wd1, a1, b1, _chip_scatter_exchange([chip_sums[k] for k in early]))
    chip_recvd = dict(zip(early, early_recvd))
    grads["ffn1_w_gate"], grads["ffn1_w_up"], grads["ffn1_w_down"] = _ffn_bwd_weights(
        "ffn1_bwd_weights", h1, dx1b, da1, db1, act1)
    last = ["ffn1_w_gate", "ffn1_w_up", "ffn1_w_down"]
    pair_stage("ffn1", last)
    chip_recvd.update(zip(last, _run_exchange(
        "rs_chips_ffn1", _chip_scatter_exchange([chip_sums[k] for k in last]))))
    dx0, _, g_ffn1_norm = _rms_bwd("rms_ffn1_bwd", x0, ffn1_norm, dh1, dx1)

    small = [("ffn1_norm", g_ffn1_norm), ("mix_norm", g_mix_norm), ("pool_scale", g_pool_scale),
             ("gla_b_a", g_b_a), ("gla_head_norm", g_head_norm), ("xattn_norm", g_xattn_norm),
             ("mem_norm", g_mem_norm), ("ffn2_norm", g_ffn2_norm), ("final_norm", g_final)]
    packed = jnp.concatenate([g.reshape(-1) for _, g in small] + [loss_part.reshape(-1)])
    slab = 8 * LANE
    padded = -(-packed.shape[0] // slab) * slab
    packed = jnp.pad(packed, (0, padded - packed.shape[0])).reshape(padded // LANE, LANE)
    reduced = _all_reduce_small(packed).reshape(-1)
    small_grads = {}
    off = 0
    for name, g in small:
        small_grads[name] = reduced[off:off + g.size]
        off += g.size
    loss = reduced[off]

    out_g, out_d, out_m, out_v = {}, {}, {}, {}
    for k in order:
        w = weights[k]
        if k in small_grads:
            w2 = w.reshape(1, -1)
            parts = [small_grads[k].reshape(1, -1)]
            own_block = jnp.zeros((1,), jnp.int32)
        else:
            w2 = w.reshape(-1, w.shape[-1])
            parts = [chip_sums[k], chip_recvd[k]]
            own_block = chip
        res = _adamw("adamw_" + k, w2, mom1[k].reshape(w2.shape), mom2[k].reshape(w2.shape), parts, own_block)
        out_g[k], out_d[k], out_m[k], out_v[k] = [r.reshape(w.shape) for r in res]

    return (loss, dx0.reshape(x.shape), *[out_g[k] for k in order], *[out_d[k] for k in order],
            *[out_m[k] for k in order], *[out_v[k] for k in order])
```

```python
import collections

import jax
import jax.numpy as jnp
from jax import lax
from jax.experimental import pallas as pl
from jax.experimental.pallas import tpu as pltpu

F32 = jnp.float32
BF16 = jnp.bfloat16
MESH = pl.DeviceIdType.MESH

N_DEV = 8
CHUNK = 64
POOL_WINDOWS = (2, 4, 8, 16)
POOL_HALO = 16
HEADS = 4
GATE_TEMP = 16.0
RMS_EPS = 1e-6
LANE = 128
V7X_VMEM_BYTES = 64 * 1024 * 1024
VMEM_LIMIT = V7X_VMEM_BYTES - 8 * 1024 * 1024
GLA_ROWS = 4 * CHUNK
FFN_SUBTILES = 2

ADAM_LR = 0.001
ADAM_B1 = 0.9
ADAM_B2 = 0.999
ADAM_EPS = 1e-08
ADAM_WD = 0.01
ADAM_STEP = 10

_NN = (((1,), (0,)), ((), ()))
_NT = (((1,), (1,)), ((), ()))
_TN = (((0,), (0,)), ((), ()))


def _cparams(sem=None):
    return pltpu.CompilerParams(dimension_semantics=sem, vmem_limit_bytes=VMEM_LIMIT)


def _tile(n, pref, align):
    t = (min(pref, n) // align) * align
    while t >= align:
        if n % t == 0:
            return t
        t -= align
    return n


def _dot(a, b, dims=_NN):
    return lax.dot_general(a.astype(BF16), b.astype(BF16), dims, preferred_element_type=F32)


def _silu_parts(z):
    sig = jax.nn.sigmoid(z)
    return z * sig, sig * (1.0 + z * (1.0 - sig))


def _matmul(name, pairs, a_spec, b_spec, out_shape, out_spec, grid, acc_shape, dims,
            res=None, res_spec=None, scale=None):
    n = len(pairs)
    nk = grid[2]

    def body(*refs):
        a_refs, b_refs = refs[:n], refs[n:2 * n]
        pos = 2 * n
        res_ref = None
        if res is not None:
            res_ref = refs[pos]
            pos += 1
        o_ref, acc = refs[pos], refs[pos + 1]
        k = pl.program_id(2)

        @pl.when(k == 0)
        def _():
            acc[...] = jnp.zeros_like(acc)

        part = None
        for a_ref, b_ref in zip(a_refs, b_refs):
            d = _dot(a_ref[...], b_ref[...], dims)
            part = d if part is None else part + d
        acc[...] += part

        @pl.when(k == nk - 1)
        def _():
            r = acc[...]
            if scale is not None:
                r = r * scale
            if res_ref is not None:
                r = r + res_ref[...]
            o_ref[...] = r.astype(o_ref.dtype)

    ops = [p[0] for p in pairs] + [p[1] for p in pairs]
    specs = [a_spec] * n + [b_spec] * n
    if res is not None:
        ops.append(res)
        specs.append(res_spec)
    return pl.pallas_call(
        body, name=name, grid=grid, in_specs=specs, out_specs=out_spec, out_shape=out_shape,
        scratch_shapes=[pltpu.VMEM(acc_shape, F32)],
        compiler_params=_cparams(("parallel", "parallel", "arbitrary")),
    )(*ops)


def _mm_nn(name, pairs, out_dtype, res=None, tm=1024, tn=1024, tk=2048):
    m, kd = pairs[0][0].shape
    nd = pairs[0][1].shape[1]
    tm, tn, tk = _tile(m, tm, 16), _tile(nd, tn, LANE), _tile(kd, tk, LANE)
    return _matmul(
        name, pairs,
        pl.BlockSpec((tm, tk), lambda i, j, k: (i, k)),
        pl.BlockSpec((tk, tn), lambda i, j, k: (k, j)),
        jax.ShapeDtypeStruct((m, nd), out_dtype),
        pl.BlockSpec((tm, tn), lambda i, j, k: (i, j)),
        (m // tm, nd // tn, kd // tk), (tm, tn), _NN,
        res=res, res_spec=pl.BlockSpec((tm, tn), lambda i, j, k: (i, j)))


def _mm_nt(name, pairs, out_dtype, tm=1024, tn=1024, tk=2048):
    m, kd = pairs[0][0].shape
    nd = pairs[0][1].shape[0]
    tm, tn, tk = _tile(m, tm, 16), _tile(nd, tn, LANE), _tile(kd, tk, LANE)
    return _matmul(
        name, pairs,
        pl.BlockSpec((tm, tk), lambda i, j, k: (i, k)),
        pl.BlockSpec((tn, tk), lambda i, j, k: (j, k)),
        jax.ShapeDtypeStruct((m, nd), out_dtype),
        pl.BlockSpec((tm, tn), lambda i, j, k: (i, j)),
        (m // tm, nd // tn, kd // tk), (tm, tn), _NT)


def _mm_tn(name, pairs, out_dtype, tm=1024, tn=2048, tk=512):
    kd, m = pairs[0][0].shape
    nd = pairs[0][1].shape[1]
    tm, tn, tk = _tile(m, tm, LANE), _tile(nd, tn, LANE), _tile(kd, tk, 16)
    return _matmul(
        name, pairs,
        pl.BlockSpec((tk, tm), lambda i, j, k: (k, i)),
        pl.BlockSpec((tk, tn), lambda i, j, k: (k, j)),
        jax.ShapeDtypeStruct((m, nd), out_dtype),
        pl.BlockSpec((tm, tn), lambda i, j, k: (i, j)),
        (m // tm, nd // tn, kd // tk), (tm, tn), _TN)


def _rms_fwd(name, x, gain):
    m, d = x.shape
    tm = _tile(m, 512, 16)

    def body(x_ref, g_ref, o_ref):
        xf = x_ref[...]
        r = lax.rsqrt(jnp.mean(xf * xf, axis=-1, keepdims=True) + RMS_EPS)
        o_ref[...] = ((xf * r) * g_ref[...]).astype(o_ref.dtype)

    return pl.pallas_call(
        body, name=name, grid=(m // tm,),
        in_specs=[pl.BlockSpec((tm, d), lambda i: (i, 0)), pl.BlockSpec((1, d), lambda i: (0, 0))],
        out_specs=pl.BlockSpec((tm, d), lambda i: (i, 0)),
        out_shape=jax.ShapeDtypeStruct((m, d), BF16),
        compiler_params=_cparams(("parallel",)),
    )(x, gain)


def _rms_bwd(name, x, gain, dh, dres=None):
    m, d = x.shape
    tm = _tile(m, 512, 16)
    has_res = dres is not None

    def body(*refs):
        x_ref, g_ref, dh_ref = refs[:3]
        pos = 3
        res_ref = None
        if has_res:
            res_ref = refs[pos]
            pos += 1
        dx_ref, dxb_ref, dg_ref = refs[pos], refs[pos + 1], refs[pos + 2]
        i = pl.program_id(0)

        @pl.when(i == 0)
        def _():
            dg_ref[...] = jnp.zeros_like(dg_ref)

        xf = x_ref[...]
        r = lax.rsqrt(jnp.mean(xf * xf, axis=-1, keepdims=True) + RMS_EPS)
        xh = xf * r
        dhf = dh_ref[...].astype(F32)
        t = dhf * g_ref[...]
        dx = r * (t - xh * jnp.mean(t * xh, axis=-1, keepdims=True))
        if res_ref is not None:
            dx = dx + res_ref[...]
        dx_ref[...] = dx
        dxb_ref[...] = dx.astype(BF16)
        dg_ref[...] += jnp.sum(dhf * xh, axis=0, keepdims=True)

    row = pl.BlockSpec((tm, d), lambda i: (i, 0))
    vec = pl.BlockSpec((1, d), lambda i: (0, 0))
    ops = [x, gain, dh] + ([dres] if has_res else [])
    return pl.pallas_call(
        body, name=name, grid=(m // tm,),
        in_specs=[row, vec, row] + ([row] if has_res else []),
        out_specs=[row, row, vec],
        out_shape=[jax.ShapeDtypeStruct((m, d), F32), jax.ShapeDtypeStruct((m, d), BF16),
                   jax.ShapeDtypeStruct((1, d), F32)],
        compiler_params=_cparams(("arbitrary",)),
    )(*ops)


def _loss_head(x, target, gain):
    m, d = x.shape
    tm = _tile(m, 512, 16)

    def body(x_ref, t_ref, g_ref, dx_ref, dxb_ref, dg_ref, loss_ref):
        i = pl.program_id(0)

        @pl.when(i == 0)
        def _():
            dg_ref[...] = jnp.zeros_like(dg_ref)
            loss_ref[...] = jnp.zeros_like(loss_ref)

        xf = x_ref[...]
        r = lax.rsqrt(jnp.mean(xf * xf, axis=-1, keepdims=True) + RMS_EPS)
        xh = xf * r
        g = g_ref[...]
        err = xh * g - t_ref[...]
        loss_ref[...] += jnp.full(loss_ref.shape, (0.5 / d) * jnp.sum(err * err), F32)
        dy = err * (1.0 / d)
        t = dy * g
        dx = r * (t - xh * jnp.mean(t * xh, axis=-1, keepdims=True))
        dx_ref[...] = dx
        dxb_ref[...] = dx.astype(BF16)
        dg_ref[...] += jnp.sum(dy * xh, axis=0, keepdims=True)

    row = pl.BlockSpec((tm, d), lambda i: (i, 0))
    vec = pl.BlockSpec((1, d), lambda i: (0, 0))
    return pl.pallas_call(
        body, name="loss_head", grid=(m // tm,),
        in_specs=[row, row, vec],
        out_specs=[row, row, vec, pl.BlockSpec((1, LANE), lambda i: (0, 0))],
        out_shape=[jax.ShapeDtypeStruct((m, d), F32), jax.ShapeDtypeStruct((m, d), BF16),
                   jax.ShapeDtypeStruct((1, d), F32), jax.ShapeDtypeStruct((1, LANE), F32)],
        compiler_params=_cparams(("arbitrary",)),
    )(x, target, gain)


def _hosted_call(name, body, grid, in_specs, out_specs, out_shape, scratch, operands, exchange):
    n_in, n_out, n_scr = len(in_specs), len(out_specs), len(scratch)
    n_xin, n_xout = len(exchange.ins), len(exchange.out_shape)

    def full_body(*refs):
        pos = 0
        parts = []
        for cnt in (n_in, n_xin, n_out, n_xout, n_scr):
            parts.append(refs[pos:pos + cnt])
            pos += cnt
        ins, x_ins, outs, x_outs, scr = parts
        sems = refs[pos:]
        first = pl.program_id(0) == 0
        last = pl.program_id(0) == grid[0] - 1
        for ax in range(1, len(grid)):
            first = jnp.logical_and(first, pl.program_id(ax) == 0)
            last = jnp.logical_and(last, pl.program_id(ax) == grid[ax] - 1)

        @pl.when(first)
        def _():
            exchange.start(x_ins, x_outs, sems)

        body(ins, outs, scr)

        @pl.when(last)
        def _():
            exchange.finish(x_ins, x_outs, sems)

    res = pl.pallas_call(
        full_body, name=name, grid=grid,
        in_specs=list(in_specs) + [_HBM] * n_xin, out_specs=list(out_specs) + [_HBM] * n_xout,
        out_shape=list(out_shape) + list(exchange.out_shape),
        scratch_shapes=list(scratch) + list(exchange.sems),
        compiler_params=_cparams(("arbitrary",) * len(grid)),
    )(*operands, *exchange.ins)
    return res[:n_out], res[n_out:]


def _ffn_fwd(name, x, h, wg, wu, wd, exchange):
    n, d = x.shape
    nb, _, fb = wg.shape
    tm = _tile(n, 512, 16 * FFN_SUBTILES)
    ts = tm // FFN_SUBTILES

    def body(ins, outs, scr):
        x_ref, h_ref, wg_ref, wu_ref, wd_ref = ins
        xo_ref, a_ref, b_ref = outs
        acc, = scr
        j = pl.program_id(1)

        @pl.when(j == 0)
        def _():
            acc[...] = jnp.zeros_like(acc)

        for s in range(FFN_SUBTILES):
            rows = slice(s * ts, (s + 1) * ts)
            hh = h_ref[rows, :]
            a = _dot(hh, wg_ref[...])
            b = _dot(hh, wu_ref[...])
            a_ref[rows, :] = a.astype(BF16)
            b_ref[rows, :] = b.astype(BF16)
            act = (a * jax.nn.sigmoid(a)) * b
            acc[rows, :] += _dot(act, wd_ref[...])

        @pl.when(j == nb - 1)
        def _():
            xo_ref[...] = x_ref[...] + 0.5 * acc[...]

    row = pl.BlockSpec((tm, d), lambda i, j: (i, 0))
    w_in = pl.BlockSpec((None, d, fb), lambda i, j: (j, 0, 0))
    hid = pl.BlockSpec((None, tm, fb), lambda i, j: (j, i, 0))
    return _hosted_call(
        name, body, (n // tm, nb),
        [row, row, w_in, w_in, pl.BlockSpec((None, fb, d), lambda i, j: (j, 0, 0))],
        [row, hid, hid],
        [jax.ShapeDtypeStruct((n, d), F32), jax.ShapeDtypeStruct((nb, n, fb), BF16),
         jax.ShapeDtypeStruct((nb, n, fb), BF16)],
        [pltpu.VMEM((tm, d), F32)], (x, h, wg, wu, wd), exchange)


def _ffn_bwd_tokens(name, dxb, wd, a, b, gate_up, exchange):
    n, d = dxb.shape
    nb, fb, _ = wd.shape
    tm = _tile(n, 512, 16 * FFN_SUBTILES)
    ts = tm // FFN_SUBTILES
    with_dh = gate_up is not None

    def body(ins, outs, scr):
        dx_ref, wd_ref, a_ref, b_ref = ins[:4]
        da_ref, db_ref, act_ref = outs[:3]
        j = pl.program_id(1)
        if with_dh:
            wg_ref, wu_ref = ins[4:]
            dh_ref, acc = outs[3], scr[0]

            @pl.when(j == 0)
            def _():
                acc[...] = jnp.zeros_like(acc)

        for s in range(FFN_SUBTILES):
            rows = slice(s * ts, (s + 1) * ts)
            dact = 0.5 * _dot(dx_ref[rows, :], wd_ref[...], _NT)
            af = a_ref[rows, :].astype(F32)
            bf = b_ref[rows, :].astype(F32)
            sl, dsl = _silu_parts(af)
            act_ref[rows, :] = (sl * bf).astype(BF16)
            d_b = (dact * sl).astype(BF16)
            d_a = (dact * bf * dsl).astype(BF16)
            db_ref[rows, :] = d_b
            da_ref[rows, :] = d_a
            if with_dh:
                acc[rows, :] += _dot(d_a, wg_ref[...], _NT) + _dot(d_b, wu_ref[...], _NT)

        if with_dh:
            @pl.when(j == nb - 1)
            def _():
                dh_ref[...] = acc[...]

    row = pl.BlockSpec((tm, d), lambda i, j: (i, 0))
    w_in = pl.BlockSpec((None, d, fb), lambda i, j: (j, 0, 0))
    hid = pl.BlockSpec((None, tm, fb), lambda i, j: (j, i, 0))
    shp = jax.ShapeDtypeStruct((nb, n, fb), BF16)
    return _hosted_call(
        name, body, (n // tm, nb),
        [row, pl.BlockSpec((None, fb, d), lambda i, j: (j, 0, 0)), hid, hid] + ([w_in, w_in] if with_dh else []),
        [hid, hid, hid] + ([row] if with_dh else []),
        [shp, shp, shp] + ([jax.ShapeDtypeStruct((n, d), F32)] if with_dh else []),
        [pltpu.VMEM((tm, d), F32)] if with_dh else [],
        (dxb, wd, a, b) + (tuple(gate_up) if with_dh else ()), exchange)


def _ffn_bwd_dh(name, da, db, wg, wu, exchange):
    nb, n, fb = da.shape
    d = wg.shape[1]
    tm = _tile(n, 512, 16)

    def body(ins, outs, scr):
        da_ref, db_ref, wg_ref, wu_ref = ins
        dh_ref, = outs
        acc, = scr
        j = pl.program_id(1)

        @pl.when(j == 0)
        def _():
            acc[...] = jnp.zeros_like(acc)

        acc[...] += _dot(da_ref[...], wg_ref[...], _NT) + _dot(db_ref[...], wu_ref[...], _NT)

        @pl.when(j == nb - 1)
        def _():
            dh_ref[...] = acc[...]

    row = pl.BlockSpec((tm, d), lambda i, j: (i, 0))
    w_in = pl.BlockSpec((None, d, fb), lambda i, j: (j, 0, 0))
    hid = pl.BlockSpec((None, tm, fb), lambda i, j: (j, i, 0))
    return _hosted_call(
        name, body, (n // tm, nb), [hid, hid, w_in, w_in], [row],
        [jax.ShapeDtypeStruct((n, d), F32)], [pltpu.VMEM((tm, d), F32)], (da, db, wg, wu), exchange)


def _ffn_bwd_weights(name, h, dxb, da, db, act, exchange):
    n, d = h.shape
    nb, _, fb = da.shape
    tk = _tile(n, 512, 16)
    nk = n // tk

    def body(ins, outs, scr):
        h_ref, dx_ref, da_ref, db_ref, act_ref = ins
        dwg_ref, dwu_ref, dwd_ref = outs
        accg, accu, accd = scr
        k = pl.program_id(1)

        @pl.when(k == 0)
        def _():
            accg[...] = jnp.zeros_like(accg)
            accu[...] = jnp.zeros_like(accu)
            accd[...] = jnp.zeros_like(accd)

        hh = h_ref[...]
        accg[...] += _dot(hh, da_ref[...], _TN)
        accu[...] += _dot(hh, db_ref[...], _TN)
        accd[...] += _dot(act_ref[...], dx_ref[...], _TN)

        @pl.when(k == nk - 1)
        def _():
            dwg_ref[...] = accg[...].astype(BF16)
            dwu_ref[...] = accu[...].astype(BF16)
            dwd_ref[...] = (0.5 * accd[...]).astype(BF16)

    row = pl.BlockSpec((tk, d), lambda j, k: (k, 0))
    hid = pl.BlockSpec((None, tk, fb), lambda j, k: (j, k, 0))
    w_in = pl.BlockSpec((None, d, fb), lambda j, k: (j, 0, 0))
    w_out = pl.BlockSpec((None, fb, d), lambda j, k: (j, 0, 0))
    return _hosted_call(
        name, body, (nb, nk), [row, row, hid, hid, hid], [w_in, w_in, w_out],
        [jax.ShapeDtypeStruct((nb, d, fb), BF16), jax.ShapeDtypeStruct((nb, d, fb), BF16),
         jax.ShapeDtypeStruct((nb, fb, d), BF16)],
        [pltpu.VMEM((d, fb), F32), pltpu.VMEM((d, fb), F32), pltpu.VMEM((fb, d), F32)],
        (h, dxb, da, db, act), exchange)


def _pool_diff(ext_ref, rows, cols, width, t_idx):
    s = ext_ref[POOL_HALO:POOL_HALO + rows, cols]
    for sft in range(1, width):
        s = s + ext_ref[POOL_HALO - sft:POOL_HALO - sft + rows, cols]
    cnt = jnp.minimum(t_idx + 1, width).astype(F32)
    return s / cnt - ext_ref[POOL_HALO:POOL_HALO + rows, cols]


def _pool_fwd(proj, pool_w, pool_scale):
    n = proj.shape[0]
    dp = pool_scale.shape[1]
    c = dp // len(POOL_WINDOWS)
    tm = _tile(n, 512, POOL_HALO)
    hb = tm // POOL_HALO

    def body(u_ref, halo_ref, pw_ref, sc_ref, y_ref, ext):
        i = pl.program_id(0)
        ext[0:POOL_HALO, :] = jnp.where(i > 0, halo_ref[...], 0.0)
        ext[POOL_HALO:, :] = u_ref[...]
        t_idx = i * tm + lax.broadcasted_iota(jnp.int32, (tm, 1), 0)
        for g, width in enumerate(POOL_WINDOWS):
            cols = slice(g * c, (g + 1) * c)
            dgrp = _pool_diff(ext, tm, cols, width, t_idx)
            y_ref[:, cols] = (_dot(dgrp, pw_ref[g]) * sc_ref[:, cols]).astype(BF16)

    return pl.pallas_call(
        body, name="pool_fwd", grid=(n // tm,),
        in_specs=[pl.BlockSpec((tm, dp), lambda i: (i, 0)),
                  pl.BlockSpec((POOL_HALO, dp), lambda i: (jnp.maximum(i * hb - 1, 0), 0)),
                  pl.BlockSpec((len(POOL_WINDOWS), c, c), lambda i: (0, 0, 0)),
                  pl.BlockSpec((1, dp), lambda i: (0, 0))],
        out_specs=pl.BlockSpec((tm, dp), lambda i: (i, 0)),
        out_shape=jax.ShapeDtypeStruct((n, dp), BF16),
        scratch_shapes=[pltpu.VMEM((tm + POOL_HALO, dp), F32)],
        compiler_params=_cparams(("parallel",)),
    )(proj, proj, pool_w, pool_scale)


def _pool_bwd(proj, dymix, pool_w, pool_scale):
    n = proj.shape[0]
    dp = pool_scale.shape[1]
    ng = len(POOL_WINDOWS)
    c = dp // ng
    tm = _tile(n, 512, POOL_HALO)
    hb = tm // POOL_HALO
    nsteps = n // tm
    last_halo = n // POOL_HALO - 1

    def body(u_ref, halo_ref, dy_ref, dyn_ref, pw_ref, sc_ref, du_ref, dpw_ref, dsc_ref, ext, dyext, e_s):
        i = pl.program_id(0)

        @pl.when(i == 0)
        def _():
            dpw_ref[...] = jnp.zeros_like(dpw_ref)
            dsc_ref[...] = jnp.zeros_like(dsc_ref)

        ext[0:POOL_HALO, :] = jnp.where(i > 0, halo_ref[...], 0.0)
        ext[POOL_HALO:, :] = u_ref[...]
        dyext[0:tm, :] = dy_ref[...]
        dyext[tm:, :] = jnp.where(i < nsteps - 1, dyn_ref[...], 0.0)
        t_idx = i * tm + lax.broadcasted_iota(jnp.int32, (tm, 1), 0)
        te_idx = i * tm + lax.broadcasted_iota(jnp.int32, (tm + POOL_HALO, 1), 0)
        for g, width in enumerate(POOL_WINDOWS):
            cols = slice(g * c, (g + 1) * c)
            dgrp = _pool_diff(ext, tm, cols, width, t_idx)
            w_g = pw_ref[g]
            dys = dyext[:, cols] * sc_ref[:, cols]
            ypre = _dot(dgrp, w_g)
            dsc_ref[:, cols] += jnp.sum(dyext[0:tm, cols] * ypre, axis=0, keepdims=True)
            dpw_ref[g] += _dot(dgrp, dys[0:tm], _TN)
            dd = _dot(dys, w_g, _NT)
            e_s[...] = dd / jnp.minimum(te_idx + 1, width).astype(F32)
            acc = e_s[0:tm, :]
            for sft in range(1, width):
                acc = acc + e_s[sft:sft + tm, :]
            du_ref[:, cols] = (acc - dd[0:tm]).astype(BF16)

    return pl.pallas_call(
        body, name="pool_bwd", grid=(nsteps,),
        in_specs=[pl.BlockSpec((tm, dp), lambda i: (i, 0)),
                  pl.BlockSpec((POOL_HALO, dp), lambda i: (jnp.maximum(i * hb - 1, 0), 0)),
                  pl.BlockSpec((tm, dp), lambda i: (i, 0)),
                  pl.BlockSpec((POOL_HALO, dp), lambda i: (jnp.minimum((i + 1) * hb, last_halo), 0)),
                  pl.BlockSpec((ng, c, c), lambda i: (0, 0, 0)),
                  pl.BlockSpec((1, dp), lambda i: (0, 0))],
        out_specs=[pl.BlockSpec((tm, dp), lambda i: (i, 0)),
                   pl.BlockSpec((ng, c, c), lambda i: (0, 0, 0)),
                   pl.BlockSpec((1, dp), lambda i: (0, 0))],
        out_shape=[jax.ShapeDtypeStruct((n, dp), BF16), jax.ShapeDtypeStruct((ng, c, c), F32),
                   jax.ShapeDtypeStruct((1, dp), F32)],
        scratch_shapes=[pltpu.VMEM((tm + POOL_HALO, dp), F32), pltpu.VMEM((tm + POOL_HALO, dp), F32),
                        pltpu.VMEM((tm + POOL_HALO, c), F32)],
        compiler_params=_cparams(("arbitrary",)),
    )(proj, proj, dymix, dymix, pool_w, pool_scale)


def _chunk_cumsum(x):
    row = lax.broadcasted_iota(jnp.int32, x.shape, 0) % CHUNK
    s = 1
    while s < CHUNK:
        x = x + jnp.where(row >= s, pltpu.roll(x, s, 0), 0.0)
        s *= 2
    return x


def _gate_logits(alr_ref, wa_ref, ba_ref):
    z = _dot(alr_ref[...], wa_ref[...]) + ba_ref[...]
    la = (jnp.minimum(z, 0.0) - jnp.log(1.0 + jnp.exp(-jnp.abs(z)))) * (1.0 / GATE_TEMP)
    return z, la


def _gla_dims(proj, head_norm):
    n, pw = proj.shape
    dv4 = head_norm.shape[1]
    dk4 = dv4 // 2
    return n, pw, dv4, dk4, dk4 // HEADS, dv4 // HEADS


def _gla_in_specs(t, dk4, dv4, rev):
    alr_blk = (2 * dv4 + 2 * dv4) // LANE
    return [pl.BlockSpec((t, dk4), lambda i: (rev(i), 2)),
            pl.BlockSpec((t, dk4), lambda i: (rev(i), 3)),
            pl.BlockSpec((t, dv4), lambda i: (rev(i), 2)),
            pl.BlockSpec((t, dv4), lambda i: (rev(i), 3)),
            pl.BlockSpec((t, LANE), lambda i: (rev(i), alr_blk))]


def _gla_fwd(proj, wa2p, b_a, head_norm):
    n, _, dv4, dk4, dk, dv = _gla_dims(proj, head_norm)
    t = _tile(n, GLA_ROWS, CHUNK)
    nc = t // CHUNK
    qscale = dk ** -0.5

    def body(q_ref, k_ref, v_ref, g_ref, alr_ref, wa_ref, ba_ref, hn_ref, y_ref, st_ref, state, cum_s):
        i = pl.program_id(0)

        @pl.when(i == 0)
        def _():
            state[...] = jnp.zeros_like(state)

        _, la = _gate_logits(alr_ref, wa_ref, ba_ref)
        cum_s[...] = _chunk_cumsum(la)
        for c in range(nc):
            rows = slice(c * CHUNK, (c + 1) * CHUNK)
            bend = cum_s[c * CHUNK + CHUNK - 1:(c + 1) * CHUNK, :]
            dec = jnp.exp(bend - cum_s[rows, :])
            e = jnp.exp(bend)
            kd = (k_ref[rows, :] * dec).astype(BF16)
            qs = (q_ref[rows, :] * qscale).astype(BF16)
            vv = v_ref[rows, :].astype(BF16)
            for h in range(HEADS):
                hk = slice(h * dk, (h + 1) * dk)
                hv = slice(h * dv, (h + 1) * dv)
                s_new = state[h] * e[:, hk] + _dot(vv[:, hv], kd[:, hk], _TN)
                state[h] = s_new
                st_ref[c, h] = s_new
                o = _dot(qs[:, hk], s_new, _NT)
                r = lax.rsqrt(jnp.mean(o * o, axis=-1, keepdims=True) + RMS_EPS)
                gg = g_ref[rows, hv]
                y_ref[rows, hv] = (((o * r) * hn_ref[:, hv]) * (gg * jax.nn.sigmoid(gg))).astype(BF16)

    full = lambda shape: pl.BlockSpec(shape, lambda i: tuple(0 for _ in shape))
    return pl.pallas_call(
        body, name="gla_fwd", grid=(n // t,),
        in_specs=_gla_in_specs(t, dk4, dv4, lambda i: i)
        + [full((LANE, dk4)), full((1, dk4)), full((1, dv4))],
        out_specs=[pl.BlockSpec((t, dv4), lambda i: (i, 0)),
                   pl.BlockSpec((nc, HEADS, dv, dk), lambda i: (i, 0, 0, 0))],
        out_shape=[jax.ShapeDtypeStruct((n, dv4), BF16),
                   jax.ShapeDtypeStruct((n // CHUNK, HEADS, dv, dk), F32)],
        scratch_shapes=[pltpu.VMEM((HEADS, dv, dk), F32), pltpu.VMEM((t, dk4), F32)],
        compiler_params=_cparams(("arbitrary",)),
    )(proj, proj, proj, proj, proj, wa2p, b_a, head_norm)


def _gla_bwd(proj, dymix, du, states, wa2p, b_a, head_norm):
    n, pw, dv4, dk4, dk, dv = _gla_dims(proj, head_norm)
    t = _tile(n, GLA_ROWS, CHUNK)
    nc = t // CHUNK
    nsteps = n // t
    qscale = dk ** -0.5
    rev = lambda i: nsteps - 1 - i

    def body(q_ref, k_ref, v_ref, g_ref, alr_ref, dy_ref, du_ref, st_ref, prev_ref, wa_ref, ba_ref, hn_ref,
             dp_ref, dwa_ref, dba_ref, dhn_ref, carry, cum_s, dz_s):
        i = pl.program_id(0)

        @pl.when(i == 0)
        def _():
            carry[...] = jnp.zeros_like(carry)
            dwa_ref[...] = jnp.zeros_like(dwa_ref)
            dba_ref[...] = jnp.zeros_like(dba_ref)
            dhn_ref[...] = jnp.zeros_like(dhn_ref)

        first_step = i == nsteps - 1
        z, la = _gate_logits(alr_ref, wa_ref, ba_ref)
        one_minus_sig = 1.0 - jax.nn.sigmoid(z)
        cum_s[...] = _chunk_cumsum(la)
        dp_ref[:, 0:dv4] = du_ref[...]
        for c in reversed(range(nc)):
            rows = slice(c * CHUNK, (c + 1) * CHUNK)
            bend = cum_s[c * CHUNK + CHUNK - 1:(c + 1) * CHUNK, :]
            dec = jnp.exp(bend - cum_s[rows, :])
            e = jnp.exp(bend)
            kd_f = k_ref[rows, :] * dec
            kd = kd_f.astype(BF16)
            qs = (q_ref[rows, :] * qscale).astype(BF16)
            vv = v_ref[rows, :].astype(BF16)
            dkd_parts, dee_parts = [], []
            for h in range(HEADS):
                hk = slice(h * dk, (h + 1) * dk)
                hv = slice(h * dv, (h + 1) * dv)
                s_n = st_ref[c, h]
                if c > 0:
                    s_prev = st_ref[c - 1, h]
                else:
                    s_prev = jnp.where(first_step, 0.0, prev_ref[0, h])
                o = _dot(qs[:, hk], s_n, _NT)
                r = lax.rsqrt(jnp.mean(o * o, axis=-1, keepdims=True) + RMS_EPS)
                oh = o * r
                gg = g_ref[rows, hv]
                sl, dsl = _silu_parts(gg)
                dyh = dy_ref[rows, hv]
                hn = hn_ref[:, hv]
                tt = dyh * sl
                dhn_ref[:, hv] += jnp.sum(tt * oh, axis=0, keepdims=True)
                dg = dyh * (oh * hn) * dsl
                tt = tt * hn
                do = (r * (tt - oh * jnp.mean(tt * oh, axis=-1, keepdims=True))).astype(BF16)
                g_n = carry[h] + _dot(do, qs[:, hk], _TN)
                dq = _dot(do, s_n) * qscale
                dkd = _dot(vv[:, hv], g_n)
                dvh = _dot(kd[:, hk], g_n, _NT)
                de = jnp.sum(g_n * s_prev, axis=0, keepdims=True)
                carry[h] = g_n * e[:, hk]
                dp_ref[rows, dv4 + h * dk:dv4 + (h + 1) * dk] = dq.astype(BF16)
                dp_ref[rows, 2 * dv4 + h * dv:2 * dv4 + (h + 1) * dv] = dvh.astype(BF16)
                dp_ref[rows, 3 * dv4 + h * dv:3 * dv4 + (h + 1) * dv] = dg.astype(BF16)
                dkd_parts.append(dkd)
                dee_parts.append(de * e[:, hk])
            dkd_c = jnp.concatenate(dkd_parts, axis=1)
            dee = jnp.concatenate(dee_parts, axis=1)
            dp_ref[rows, dv4 + dk4:dv4 + 2 * dk4] = (dkd_c * dec).astype(BF16)
            w = dkd_c * kd_f
            dla = (_chunk_cumsum(w) - w) + dee
            dz_s[rows, :] = dla * (1.0 / GATE_TEMP)
        dz = dz_s[...] * one_minus_sig
        dp_ref[:, 4 * dv4:4 * dv4 + LANE] = _dot(dz, wa_ref[...], _NT).astype(BF16)
        dwa_ref[...] += _dot(alr_ref[...], dz, _TN)
        dba_ref[...] += jnp.sum(dz, axis=0, keepdims=True)

    full = lambda shape: pl.BlockSpec(shape, lambda i: tuple(0 for _ in shape))
    return pl.pallas_call(
        body, name="gla_bwd", grid=(nsteps,),
        in_specs=_gla_in_specs(t, dk4, dv4, rev)
        + [pl.BlockSpec((t, dv4), lambda i: (rev(i), 1)),
           pl.BlockSpec((t, dv4), lambda i: (rev(i), 0)),
           pl.BlockSpec((nc, HEADS, dv, dk), lambda i: (rev(i), 0, 0, 0)),
           pl.BlockSpec((1, HEADS, dv, dk), lambda i: (jnp.maximum(rev(i) * nc - 1, 0), 0, 0, 0)),
           full((LANE, dk4)), full((1, dk4)), full((1, dv4))],
        out_specs=[pl.BlockSpec((t, pw), lambda i: (rev(i), 0)),
                   full((LANE, dk4)), full((1, dk4)), full((1, dv4))],
        out_shape=[jax.ShapeDtypeStruct((n, pw), BF16), jax.ShapeDtypeStruct((LANE, dk4), F32),
                   jax.ShapeDtypeStruct((1, dk4), F32), jax.ShapeDtypeStruct((1, dv4), F32)],
        scratch_shapes=[pltpu.VMEM((HEADS, dv, dk), F32), pltpu.VMEM((t, dk4), F32),
                        pltpu.VMEM((t, dk4), F32)],
        compiler_params=_cparams(("arbitrary",)),
    )(proj, proj, proj, proj, proj, dymix, du, states, states, wa2p, b_a, head_norm)


def _softmax_rows(q, k, scale):
    s = _dot(q, k, _NT) * scale
    p = jnp.exp(s - jnp.max(s, axis=-1, keepdims=True))
    return p / jnp.sum(p, axis=-1, keepdims=True)


def _attn_fwd(q, kv):
    n, d = q.shape
    m = kv.shape[0]
    dh = d // HEADS
    tm = _tile(n, 512, 16)
    scale = dh ** -0.5

    def body(q_ref, k_ref, v_ref, o_ref):
        for h in range(HEADS):
            hs = slice(h * dh, (h + 1) * dh)
            p = _softmax_rows(q_ref[:, hs], k_ref[:, hs], scale)
            o_ref[:, hs] = _dot(p, v_ref[:, hs]).astype(BF16)

    return pl.pallas_call(
        body, name="attn_fwd", grid=(n // tm,),
        in_specs=[pl.BlockSpec((tm, d), lambda i: (i, 0)), pl.BlockSpec((m, d), lambda i: (0, 0)),
                  pl.BlockSpec((m, d), lambda i: (0, 1))],
        out_specs=pl.BlockSpec((tm, d), lambda i: (i, 0)),
        out_shape=jax.ShapeDtypeStruct((n, d), BF16),
        compiler_params=_cparams(("parallel",)),
    )(q, kv, kv)


def _attn_bwd(q, kv, do):
    n, d = q.shape
    m = kv.shape[0]
    dh = d // HEADS
    tm = _tile(n, 512, 16)
    scale = dh ** -0.5

    def body(q_ref, k_ref, v_ref, do_ref, dq_ref, dk_ref, dv_ref):
        i = pl.program_id(0)

        @pl.when(i == 0)
        def _():
            dk_ref[...] = jnp.zeros_like(dk_ref)
            dv_ref[...] = jnp.zeros_like(dv_ref)

        for h in range(HEADS):
            hs = slice(h * dh, (h + 1) * dh)
            qh, kh, vh, doh = q_ref[:, hs], k_ref[:, hs], v_ref[:, hs], do_ref[:, hs]
            p = _softmax_rows(qh, kh, scale)
            dv_ref[:, hs] += _dot(p, doh, _TN)
            dp = _dot(doh, vh, _NT)
            ds = p * (dp - jnp.sum(dp * p, axis=-1, keepdims=True)) * scale
            dq_ref[:, hs] = _dot(ds, kh).astype(BF16)
            dk_ref[:, hs] += _dot(ds, qh, _TN)

    row = pl.BlockSpec((tm, d), lambda i: (i, 0))
    memb = pl.BlockSpec((m, d), lambda i: (0, 0))
    return pl.pallas_call(
        body, name="attn_bwd", grid=(n // tm,),
        in_specs=[row, memb, pl.BlockSpec((m, d), lambda i: (0, 1)), row],
        out_specs=[row, memb, memb],
        out_shape=[jax.ShapeDtypeStruct((n, d), BF16), jax.ShapeDtypeStruct((m, d), F32),
                   jax.ShapeDtypeStruct((m, d), F32)],
        compiler_params=_cparams(("arbitrary",)),
    )(q, kv, kv, do)


def _adamw(name, w, m, v, parts, own_block):
    r, c = w.shape
    tr = _tile(r, max(16, (256 * 1024) // c), 16)
    row = pl.BlockSpec((tr, c), lambda i, o: (i, 0))
    ops, specs = [w, m, v], [row, row, row]
    for p in parts:
        if p.ndim == 2:
            ops.append(p)
            specs.append(row)
        elif p.shape[0] == 4:
            ops.append(p)
            specs.append(pl.BlockSpec((None, tr, c), lambda i, o: (o[0], i, 0)))
        else:
            for s in range(p.shape[0]):
                ops.append(p)
                specs.append(pl.BlockSpec((None, tr, c), lambda i, o, s=s: (s, i, 0)))
    n_parts = len(ops) - 3
    c1 = 1.0 - ADAM_B1 ** ADAM_STEP
    c2 = 1.0 - ADAM_B2 ** ADAM_STEP

    def body(o_ref, *refs):
        del o_ref
        w_ref, m_ref, v_ref = refs[:3]
        g_refs = refs[3:3 + n_parts]
        go_ref, d_ref, mo_ref, vo_ref = refs[3 + n_parts:]
        g = g_refs[0][...].astype(F32)
        for g_ref in g_refs[1:]:
            g = g + g_ref[...].astype(F32)
        m_new = ADAM_B1 * m_ref[...] + (1.0 - ADAM_B1) * g
        v_new = ADAM_B2 * v_ref[...] + (1.0 - ADAM_B2) * (g * g)
        m_hat = m_new / c1
        v_hat = v_new / c2
        go_ref[...] = g
        d_ref[...] = -ADAM_LR * (m_hat / (jnp.sqrt(v_hat) + ADAM_EPS) + ADAM_WD * w_ref[...])
        mo_ref[...] = m_new
        vo_ref[...] = v_new

    shp = jax.ShapeDtypeStruct((r, c), F32)
    return pl.pallas_call(
        body, name=name,
        grid_spec=pltpu.PrefetchScalarGridSpec(
            num_scalar_prefetch=1, grid=(r // tr,), in_specs=specs, out_specs=[row] * 4),
        out_shape=[shp] * 4,
        compiler_params=_cparams(("parallel",)),
    )(own_block, *ops)


def _pair_add(name, g, recvd, core):
    _, r, c = g.shape
    tr = _tile(r, max(16, (256 * 1024) // c), 16)

    def body(core_ref, a_ref, b_ref, o_ref):
        del core_ref
        o_ref[...] = (a_ref[...].astype(F32) + b_ref[...].astype(F32)).astype(o_ref.dtype)

    blk = pl.BlockSpec((None, tr, c), lambda s, i, core_ref: (s, i, 0))
    mine = pl.BlockSpec((None, tr, c), lambda s, i, core_ref: (2 * s + core_ref[0], i, 0))
    return pl.pallas_call(
        body, name=name,
        grid_spec=pltpu.PrefetchScalarGridSpec(
            num_scalar_prefetch=1, grid=(4, r // tr), in_specs=[mine, blk], out_specs=blk),
        out_shape=jax.ShapeDtypeStruct(recvd.shape, g.dtype),
        compiler_params=_cparams(("parallel", "parallel")),
    )(core, g, recvd)


def _position():
    return lax.axis_index("x"), lax.axis_index("y"), lax.axis_index("c")


_HBM = pl.BlockSpec(memory_space=pltpu.HBM)

_Exchange = collections.namedtuple("_Exchange", "ins out_shape sems start finish")
_NO_EXCHANGE = _Exchange((), (), (), lambda ins, outs, sems: None, lambda ins, outs, sems: None)


def _all_gather(name, shards):
    n = len(shards)

    def body(*refs):
        ins, outs = refs[:n], refs[n:2 * n]
        send_sems, recv_sems, local_sems = refs[2 * n:]
        x, y, c = _position()
        me, sibling = (x, y, c), (x, y, 1 - c)
        chips = [(1 - x, y), (x, 1 - y), (1 - x, 1 - y)]

        def block(ref, px, py, pc):
            return ref.at[4 * px + 2 * py + pc]

        def copy(a, k, owner, to, src=None):
            return pltpu.make_async_remote_copy(
                src_ref=block(outs[a], *owner) if src is None else src, dst_ref=block(outs[a], *owner),
                send_sem=send_sems.at[7 * a + k], recv_sem=recv_sems.at[7 * a + k],
                device_id=to, device_id_type=MESH)

        started = []
        for a in range(n):
            mine = pltpu.make_async_copy(ins[a], block(outs[a], *me), local_sems.at[a])
            mine.start()
            started.append(mine)
        sends = []
        for a in range(n):
            sends.append(copy(a, 0, me, sibling, src=ins[a]))
            sends += [copy(a, 1 + j, me, (*chip, c), src=ins[a]) for j, chip in enumerate(chips)]
        for cp in sends:
            cp.start()
        for j, chip in enumerate(chips):
            for a in range(n):
                copy(a, 1 + j, (*chip, c), me).wait_recv()
                fwd = copy(a, 4 + j, (*chip, c), sibling)
                fwd.start()
                sends.append(fwd)
        for a in range(n):
            copy(a, 0, sibling, me).wait_recv()
            for j, chip in enumerate(chips):
                copy(a, 4 + j, (*chip, 1 - c), me).wait_recv()
        for cp in sends:
            cp.wait_send()
        for mine in started:
            mine.wait()

    return pl.pallas_call(
        body, name=name,
        in_specs=[_HBM] * n, out_specs=[_HBM] * n,
        out_shape=[jax.ShapeDtypeStruct((N_DEV,) + s.shape, s.dtype) for s in shards],
        scratch_shapes=[pltpu.SemaphoreType.DMA((7 * n,)), pltpu.SemaphoreType.DMA((7 * n,)),
                        pltpu.SemaphoreType.DMA((n,))],
    )(*shards)


def _reduce_scatter_pair(name, grads):
    n = len(grads)

    def body(*refs):
        ins, recvd = refs[:n], refs[n:2 * n]
        send_sems, recv_sems = refs[2 * n:]
        x, y, c = _position()
        for a in range(n):
            for chip in range(4):
                pltpu.make_async_remote_copy(
                    src_ref=ins[a].at[2 * chip + (1 - c)], dst_ref=recvd[a].at[chip],
                    send_sem=send_sems.at[a], recv_sem=recv_sems.at[a],
                    device_id=(x, y, 1 - c), device_id_type=MESH).start()
        for a in range(n):
            pltpu.make_async_remote_copy(
                src_ref=recvd[a], dst_ref=recvd[a], send_sem=send_sems.at[a], recv_sem=recv_sems.at[a],
                device_id=(x, y, 1 - c), device_id_type=MESH).wait()

    return pl.pallas_call(
        body, name=name, in_specs=[_HBM] * n, out_specs=[_HBM] * n,
        out_shape=[jax.ShapeDtypeStruct((4,) + g.shape[1:], g.dtype) for g in grads],
        scratch_shapes=[pltpu.SemaphoreType.DMA((n,)), pltpu.SemaphoreType.DMA((n,))],
    )(*grads)


def _chip_scatter_exchange(sums):
    n = len(sums)
    offsets = [(1, 0), (0, 1), (1, 1)]

    def start(ins, recvd, sems):
        send_sems, recv_sems = sems
        x, y, c = _position()
        for a in range(n):
            for r, (ox, oy) in enumerate(offsets):
                px = 1 - x if ox else x
                py = 1 - y if oy else y
                pltpu.make_async_remote_copy(
                    src_ref=ins[a].at[2 * px + py], dst_ref=recvd[a].at[r],
                    send_sem=send_sems.at[a], recv_sem=recv_sems.at[a],
                    device_id=(px, py, c), device_id_type=MESH).start()

    def finish(ins, recvd, sems):
        del ins
        send_sems, recv_sems = sems
        x, y, c = _position()
        for a in range(n):
            pltpu.make_async_remote_copy(
                src_ref=recvd[a], dst_ref=recvd[a], send_sem=send_sems.at[a], recv_sem=recv_sems.at[a],
                device_id=(x, y, c), device_id_type=MESH).wait()

    return _Exchange(
        ins=tuple(sums),
        out_shape=tuple(jax.ShapeDtypeStruct((3,) + s.shape[1:], s.dtype) for s in sums),
        sems=(pltpu.SemaphoreType.DMA((n,)), pltpu.SemaphoreType.DMA((n,))),
        start=start, finish=finish)


def _gather_spread_exchange(shards):
    n = len(shards)

    def peers():
        x, y, c = _position()
        return (x, y, c), [(x, y, 1 - c), (1 - x, y, c), (x, 1 - y, c), (1 - x, 1 - y, c)]

    def block(ref, px, py, pc):
        return ref.at[4 * px + 2 * py + pc]

    def copy(ins, outs, sems, a, k, owner, to):
        send_sems, recv_sems, _ = sems
        return pltpu.make_async_remote_copy(
            src_ref=ins[a], dst_ref=block(outs[a], *owner),
            send_sem=send_sems.at[4 * a + k], recv_sem=recv_sems.at[4 * a + k],
            device_id=to, device_id_type=MESH)

    def start(ins, outs, sems):
        me, others = peers()
        for a in range(n):
            pltpu.make_async_copy(ins[a], block(outs[a], *me), sems[2].at[a]).start()
            for k, to in enumerate(others):
                copy(ins, outs, sems, a, k, me, to).start()

    def finish(ins, outs, sems):
        me, others = peers()
        for a in range(n):
            for k, peer in enumerate(others):
                cp = copy(ins, outs, sems, a, k, peer, peer)
                cp.wait_recv()
                cp.wait_send()
            pltpu.make_async_copy(ins[a], block(outs[a], *me), sems[2].at[a]).wait()

    return _Exchange(
        ins=tuple(shards),
        out_shape=tuple(jax.ShapeDtypeStruct((N_DEV,) + s.shape, s.dtype) for s in shards),
        sems=(pltpu.SemaphoreType.DMA((4 * n,)), pltpu.SemaphoreType.DMA((4 * n,)),
              pltpu.SemaphoreType.DMA((n,))),
        start=start, finish=finish)


def _gather_forward(name, partial):
    n = len(partial)

    def body(*refs):
        ins, outs = refs[:n], refs[n:2 * n]
        send_sems, recv_sems = refs[2 * n:]
        x, y, c = _position()
        chips = [(1 - x, y), (x, 1 - y), (1 - x, 1 - y)]

        def copy(a, j, pc):
            blk = 4 * chips[j][0] + 2 * chips[j][1] + pc
            return pltpu.make_async_remote_copy(
                src_ref=ins[a].at[blk], dst_ref=outs[a].at[blk],
                send_sem=send_sems.at[3 * a + j], recv_sem=recv_sems.at[3 * a + j],
                device_id=(x, y, 1 - c), device_id_type=MESH)

        for a in range(n):
            for j in range(3):
                copy(a, j, c).start()
        for a in range(n):
            for j in range(3):
                copy(a, j, 1 - c).wait_recv()
                copy(a, j, c).wait_send()

    return pl.pallas_call(
        body, name=name, in_specs=[_HBM] * n, out_specs=[_HBM] * n,
        out_shape=[jax.ShapeDtypeStruct(p.shape, p.dtype) for p in partial],
        input_output_aliases={a: a for a in range(n)},
        scratch_shapes=[pltpu.SemaphoreType.DMA((3 * n,)), pltpu.SemaphoreType.DMA((3 * n,))],
    )(*partial)


def _all_reduce_small(vec):
    r = vec.shape[0]

    def body(v_ref, o_ref, gbuf, send_sems, recv_sems):
        x, y, c = _position()
        me = 4 * x + 2 * y + c
        gbuf[me] = v_ref[...]
        copies = []
        for k in range(1, N_DEV):
            ox, oy, oc = (k >> 2) & 1, (k >> 1) & 1, k & 1
            peer = (1 - x if ox else x, 1 - y if oy else y, 1 - c if oc else c)
            cp = pltpu.make_async_remote_copy(
                src_ref=gbuf.at[me], dst_ref=gbuf.at[me], send_sem=send_sems.at[k - 1],
                recv_sem=recv_sems.at[k - 1], device_id=peer, device_id_type=MESH)
            cp.start()
            copies.append(cp)
        for cp in copies:
            cp.wait()
        total = gbuf[0]
        for j in range(1, N_DEV):
            total = total + gbuf[j]
        o_ref[...] = total

    return pl.pallas_call(
        body, name="all_reduce_small",
        in_specs=[pl.BlockSpec(memory_space=pltpu.VMEM)],
        out_specs=pl.BlockSpec(memory_space=pltpu.VMEM),
        out_shape=jax.ShapeDtypeStruct(vec.shape, F32),
        scratch_shapes=[pltpu.VMEM((N_DEV, r, LANE), F32), pltpu.SemaphoreType.DMA((N_DEV - 1,)),
                        pltpu.SemaphoreType.DMA((N_DEV - 1,))],
    )(vec)


def _cols_from_blocks(g):
    nb, r, cs = g.shape
    return jnp.transpose(g, (1, 0, 2)).reshape(r, nb * cs)


def _cols_to_blocks(w):
    r, cfull = w.shape
    return jnp.transpose(w.reshape(r, N_DEV, cfull // N_DEV), (1, 0, 2))


def kernel(x, mem, ffn1_norm, ffn1_w_gate, ffn1_w_up, ffn1_w_down, mix_norm, w_in, pool_w, pool_scale, gla_w_a2, gla_b_a, gla_head_norm, w_out, xattn_norm, mem_norm, xattn_w_q, xattn_w_kv, xattn_w_o, ffn2_norm, ffn2_w_gate, ffn2_w_up, ffn2_w_down, final_norm, loss_target, m_ffn1_norm, m_ffn1_w_gate, m_ffn1_w_up, m_ffn1_w_down, m_mix_norm, m_w_in, m_pool_w, m_pool_scale, m_gla_w_a2, m_gla_b_a, m_gla_head_norm, m_w_out, m_xattn_norm, m_mem_norm, m_xattn_w_q, m_xattn_w_kv, m_xattn_w_o, m_ffn2_norm, m_ffn2_w_gate, m_ffn2_w_up, m_ffn2_w_down, m_final_norm, v_ffn1_norm, v_ffn1_w_gate, v_ffn1_w_up, v_ffn1_w_down, v_mix_norm, v_w_in, v_pool_w, v_pool_scale, v_gla_w_a2, v_gla_b_a, v_gla_head_norm, v_w_out, v_xattn_norm, v_mem_norm, v_xattn_w_q, v_xattn_w_kv, v_xattn_w_o, v_ffn2_norm, v_ffn2_w_gate, v_ffn2_w_up, v_ffn2_w_down, v_final_norm):
    weights = dict(ffn1_norm=ffn1_norm, ffn1_w_gate=ffn1_w_gate, ffn1_w_up=ffn1_w_up, ffn1_w_down=ffn1_w_down, mix_norm=mix_norm, w_in=w_in, pool_w=pool_w, pool_scale=pool_scale, gla_w_a2=gla_w_a2, gla_b_a=gla_b_a, gla_head_norm=gla_head_norm, w_out=w_out, xattn_norm=xattn_norm, mem_norm=mem_norm, xattn_w_q=xattn_w_q, xattn_w_kv=xattn_w_kv, xattn_w_o=xattn_w_o, ffn2_norm=ffn2_norm, ffn2_w_gate=ffn2_w_gate, ffn2_w_up=ffn2_w_up, ffn2_w_down=ffn2_w_down, final_norm=final_norm)
    mom1 = dict(ffn1_norm=m_ffn1_norm, ffn1_w_gate=m_ffn1_w_gate, ffn1_w_up=m_ffn1_w_up, ffn1_w_down=m_ffn1_w_down, mix_norm=m_mix_norm, w_in=m_w_in, pool_w=m_pool_w, pool_scale=m_pool_scale, gla_w_a2=m_gla_w_a2, gla_b_a=m_gla_b_a, gla_head_norm=m_gla_head_norm, w_out=m_w_out, xattn_norm=m_xattn_norm, mem_norm=m_mem_norm, xattn_w_q=m_xattn_w_q, xattn_w_kv=m_xattn_w_kv, xattn_w_o=m_xattn_w_o, ffn2_norm=m_ffn2_norm, ffn2_w_gate=m_ffn2_w_gate, ffn2_w_up=m_ffn2_w_up, ffn2_w_down=m_ffn2_w_down, final_norm=m_final_norm)
    mom2 = dict(ffn1_norm=v_ffn1_norm, ffn1_w_gate=v_ffn1_w_gate, ffn1_w_up=v_ffn1_w_up, ffn1_w_down=v_ffn1_w_down, mix_norm=v_mix_norm, w_in=v_w_in, pool_w=v_pool_w, pool_scale=v_pool_scale, gla_w_a2=v_gla_w_a2, gla_b_a=v_gla_b_a, gla_head_norm=v_gla_head_norm, w_out=v_w_out, xattn_norm=v_xattn_norm, mem_norm=v_mem_norm, xattn_w_q=v_xattn_w_q, xattn_w_kv=v_xattn_w_kv, xattn_w_o=v_xattn_w_o, ffn2_norm=v_ffn2_norm, ffn2_w_gate=v_ffn2_w_gate, ffn2_w_up=v_ffn2_w_up, ffn2_w_down=v_ffn2_w_down, final_norm=v_final_norm)
    order = list(weights.keys())

    n, d = x.shape[1], x.shape[2]
    mlen = mem.shape[1]
    x0 = x.reshape(n, d)
    memf = mem.reshape(mlen, d)
    target = loss_target.reshape(n, d)
    dpool = d // 2
    in_cols = w_in.shape[2] * N_DEV
    proj_cols = 2 * d + LANE
    rank = gla_w_a2.shape[1]

    def shard(name):
        return weights[name][0].astype(BF16)

    wg1, wu1, wd1 = _all_gather("ag_ffn1", [shard("ffn1_w_gate"), shard("ffn1_w_up"), shard("ffn1_w_down")])
    later = ["w_in", "pool_w", "gla_w_a2", "w_out", "xattn_w_q", "xattn_w_kv", "xattn_w_o",
             "ffn2_w_gate", "ffn2_w_up", "ffn2_w_down"]
    h1 = _rms_fwd("rms_ffn1", x0, ffn1_norm)
    (x1, a1, b1), spread = _ffn_fwd(
        "ffn1_fwd", x0, h1, wg1, wu1, wd1, _gather_spread_exchange([shard(k) for k in later]))
    win_g, pw_g, wa2_g, wout_g, wq_g, wkv_g, wo_g, wg2, wu2, wd2 = _gather_forward("ag_forward", spread)

    winp = jnp.pad(_cols_from_blocks(win_g), ((0, 0), (0, proj_cols - in_cols)))
    pw = jnp.transpose(pw_g, (1, 0, 2, 3)).reshape(len(POOL_WINDOWS), dpool // 4, dpool // 4)
    wa2p = jnp.pad(_cols_from_blocks(wa2_g), ((0, LANE - rank), (0, 0)))
    wout = wout_g.reshape(d, d)
    wq = wq_g.reshape(d, d)
    wkv = _cols_from_blocks(wkv_g)
    wo = wo_g.reshape(d, d)
    fnorm = final_norm.reshape(1, d)

    h2 = _rms_fwd("rms_mix", x1, mix_norm)
    proj = _mm_nn("mix_in", [(h2, winp)], F32, tn=1408)
    ypool = _pool_fwd(proj, pw, pool_scale)
    ygla, states = _gla_fwd(proj, wa2p, gla_b_a, gla_head_norm)
    x2 = _mm_nn("mix_out", [(ypool, wout[:dpool]), (ygla, wout[dpool:])], F32, res=x1, tk=1024)
    h3 = _rms_fwd("rms_xattn", x2, xattn_norm)
    mh = _rms_fwd("rms_mem", memf, mem_norm)
    q = _mm_nn("xattn_q", [(h3, wq)], BF16)
    kv = _mm_nn("xattn_kv", [(mh, wkv)], BF16)
    att = _attn_fwd(q, kv)
    x3 = _mm_nn("xattn_o", [(att, wo)], F32, res=x2)
    h4 = _rms_fwd("rms_ffn2", x3, ffn2_norm)
    (x4, a2, b2), _ = _ffn_fwd("ffn2_fwd", x3, h4, wg2, wu2, wd2, _NO_EXCHANGE)
    dx4, dx4b, g_final, loss_part = _loss_head(x4, target, fnorm)

    grads = {}
    core = lax.axis_index("c").astype(jnp.int32).reshape(1)
    chip = (2 * lax.axis_index("x") + lax.axis_index("y")).astype(jnp.int32).reshape(1)
    chip_sums = {}

    def pair_stage(tag, names):
        blocks = [grads[k].reshape(N_DEV, -1, grads[k].shape[-1]) for k in names]
        recvd = _reduce_scatter_pair("rs_pair_" + tag, blocks)
        for k, g, r in zip(names, blocks, recvd):
            chip_sums[k] = _pair_add("rs_add_" + k, g, r, core)

    (da2, db2, act2, dh4), _ = _ffn_bwd_tokens("ffn2_bwd_tokens", dx4b, wd2, a2, b2, (wg2, wu2), _NO_EXCHANGE)
    (grads["ffn2_w_gate"], grads["ffn2_w_up"], grads["ffn2_w_down"]), _ = _ffn_bwd_weights(
        "ffn2_bwd_weights", h4, dx4b, da2, db2, act2, _NO_EXCHANGE)
    pair_stage("ffn2", ["ffn2_w_gate", "ffn2_w_up", "ffn2_w_down"])
    dx3, dx3b, g_ffn2_norm = _rms_bwd("rms_ffn2_bwd", x3, ffn2_norm, dh4, dx4)

    datt = _mm_nt("xattn_do", [(dx3b, wo)], BF16)
    grads["xattn_w_o"] = _mm_tn("xattn_dwo", [(att, dx3b)], BF16).reshape(N_DEV, d // N_DEV, d)
    dq, dk, dv = _attn_bwd(q, kv, datt)
    dh3 = _mm_nt("xattn_dh", [(dq, wq)], F32)
    grads["xattn_w_q"] = _mm_tn("xattn_dwq", [(h3, dq)], BF16).reshape(N_DEV, d // N_DEV, d)
    dx2, dx2b, g_xattn_norm = _rms_bwd("rms_xattn_bwd", x2, xattn_norm, dh3, dx3)
    dkv = jnp.concatenate([dk, dv], axis=1)
    dmh = _mm_nt("xattn_dmh", [(dkv, wkv)], F32)
    grads["xattn_w_kv"] = _cols_to_blocks(_mm_tn("xattn_dwkv", [(mh, dkv)], BF16))
    pair_stage("xattn", ["xattn_w_q", "xattn_w_kv", "xattn_w_o"])
    _, _, g_mem_norm = _rms_bwd("rms_mem_bwd", memf, mem_norm, dmh)

    dymix = _mm_nt("mix_dy", [(dx2b, wout)], F32)
    grads["w_out"] = jnp.concatenate(
        [_mm_tn("mix_dwout_pool", [(ypool, dx2b)], BF16), _mm_tn("mix_dwout_gla", [(ygla, dx2b)], BF16)],
        axis=0).reshape(N_DEV, d // N_DEV, d)
    du, g_pool_w, g_pool_scale = _pool_bwd(proj, dymix, pw, pool_scale)
    dproj, g_wa2p, g_b_a, g_head_norm = _gla_bwd(proj, dymix, du, states, wa2p, gla_b_a, gla_head_norm)
    dh2 = _mm_nt("mix_dh", [(dproj, winp)], F32, tk=1408)
    grads["w_in"] = _cols_to_blocks(_mm_tn("mix_dwin", [(h2, dproj)], BF16, tn=1408)[:, :in_cols])
    grads["pool_w"] = jnp.transpose(
        g_pool_w.reshape(len(POOL_WINDOWS), N_DEV, dpool // 4 // N_DEV, dpool // 4), (1, 0, 2, 3))
    grads["gla_w_a2"] = _cols_to_blocks(g_wa2p[:rank])
    pair_stage("mix", ["w_in", "pool_w", "gla_w_a2", "w_out"])
    dx1, dx1b, g_mix_norm = _rms_bwd("rms_mix_bwd", x1, mix_norm, dh2, dx2)

    early = ["ffn2_w_gate", "ffn2_w_up", "ffn2_w_down", "xattn_w_q", "xattn_w_kv", "xattn_w_o",
             "w_in", "pool_w", "gla_w_a2", "w_out"]
    last = ["ffn1_w_gate", "ffn1_w_up", "ffn1_w_down"]
    (da1, db1, act1), _ = _ffn_bwd_tokens("ffn1_bwd_act", dx1b, wd1, a1, b1, None, _NO_EXCHANGE)
    (grads["ffn1_w_gate"], grads["ffn1_w_up"], grads["ffn1_w_down"]), early_recvd = _ffn_bwd_weights(
        "ffn1_bwd_weights", h1, dx1b, da1, db1, act1, _chip_scatter_exchange([chip_sums[k] for k in early]))
    chip_recvd = dict(zip(early, early_recvd))
    pair_stage("ffn1", last)
    (dh1,), last_recvd = _ffn_bwd_dh(
        "ffn1_bwd_dh", da1, db1, wg1, wu1, _chip_scatter_exchange([chip_sums[k] for k in last]))
    chip_recvd.update(zip(last, last_recvd))
    dx0, _, g_ffn1_norm = _rms_bwd("rms_ffn1_bwd", x0, ffn1_norm, dh1, dx1)

    small = [("ffn1_norm", g_ffn1_norm), ("mix_norm", g_mix_norm), ("pool_scale", g_pool_scale),
             ("gla_b_a", g_b_a), ("gla_head_norm", g_head_norm), ("xattn_norm", g_xattn_norm),
             ("mem_norm", g_mem_norm), ("ffn2_norm", g_ffn2_norm), ("final_norm", g_final)]
    packed = jnp.concatenate([g.reshape(-1) for _, g in small] + [loss_part.reshape(-1)])
    slab = 8 * LANE
    padded = -(-packed.shape[0] // slab) * slab
    packed = jnp.pad(packed, (0, padded - packed.shape[0])).reshape(padded // LANE, LANE)
    reduced = _all_reduce_small(packed).reshape(-1)
    small_grads = {}
    off = 0
    for name, g in small:
        small_grads[name] = reduced[off:off + g.size]
        off += g.size
    loss = reduced[off]

    out_g, out_d, out_m, out_v = {}, {}, {}, {}
    for k in order:
        w = weights[k]
        if k in small_grads:
            w2 = w.reshape(1, -1)
            parts = [small_grads[k].reshape(1, -1)]
            own_block = jnp.zeros((1,), jnp.int32)
        else:
            w2 = w.reshape(-1, w.shape[-1])
            parts = [chip_sums[k], chip_recvd[k]]
            own_block = chip
        res = _adamw("adamw_" + k, w2, mom1[k].reshape(w2.shape), mom2[k].reshape(w2.shape), parts, own_block)
        out_g[k], out_d[k], out_m[k], out_v[k] = [r.reshape(w.shape) for r in res]

    return (loss, dx0.reshape(x.shape), *[out_g[k] for k in order], *[out_d[k] for k in order],
            *[out_m[k] for k in order], *[out_v[k] for k in order])
```

```python
import collections

import jax
import jax.numpy as jnp
from jax import lax
from jax.experimental import pallas as pl
from jax.experimental.pallas import tpu as pltpu

F32 = jnp.float32
BF16 = jnp.bfloat16
MESH = pl.DeviceIdType.MESH

N_DEV = 8
CHUNK = 64
POOL_WINDOWS = (2, 4, 8, 16)
POOL_HALO = 16
HEADS = 4
GATE_TEMP = 16.0
RMS_EPS = 1e-6
LANE = 128
V7X_VMEM_BYTES = 64 * 1024 * 1024
VMEM_LIMIT = V7X_VMEM_BYTES - 8 * 1024 * 1024
GLA_ROWS = 4 * CHUNK
ELEMENTWISE_BLOCK = 512 * 1024
FFN_SUBTILES = 2

ADAM_LR = 0.001
ADAM_B1 = 0.9
ADAM_B2 = 0.999
ADAM_EPS = 1e-08
ADAM_WD = 0.01
ADAM_STEP = 10

_NN = (((1,), (0,)), ((), ()))
_NT = (((1,), (1,)), ((), ()))
_TN = (((0,), (0,)), ((), ()))


def _cparams(sem=None):
    return pltpu.CompilerParams(dimension_semantics=sem, vmem_limit_bytes=VMEM_LIMIT)


def _tile(n, pref, align):
    t = (min(pref, n) // align) * align
    while t >= align:
        if n % t == 0:
            return t
        t -= align
    return n


def _dot(a, b, dims=_NN):
    return lax.dot_general(a.astype(BF16), b.astype(BF16), dims, preferred_element_type=F32)


def _silu_parts(z):
    sig = jax.nn.sigmoid(z)
    return z * sig, sig * (1.0 + z * (1.0 - sig))


def _rms_scale(xf):
    return lax.rsqrt(jnp.mean(xf * xf, axis=-1, keepdims=True) + RMS_EPS)


def _matmul(name, pairs, a_spec, b_spec, out_shape, out_spec, grid, acc_shape, dims,
            res=None, res_spec=None, scale=None, norm_gain=None):
    n = len(pairs)
    nk = grid[2]

    def body(*refs):
        a_refs, b_refs = refs[:n], refs[n:2 * n]
        pos = 2 * n
        res_ref = gain_ref = h_ref = None
        if res is not None:
            res_ref = refs[pos]
            pos += 1
        if norm_gain is not None:
            gain_ref = refs[pos]
            pos += 1
        o_ref = refs[pos]
        pos += 1
        if norm_gain is not None:
            h_ref = refs[pos]
            pos += 1
        acc = refs[pos]
        k = pl.program_id(2)

        @pl.when(k == 0)
        def _():
            acc[...] = jnp.zeros_like(acc)

        part = None
        for a_ref, b_ref in zip(a_refs, b_refs):
            d = _dot(a_ref[...], b_ref[...], dims)
            part = d if part is None else part + d
        acc[...] += part

        @pl.when(k == nk - 1)
        def _():
            r = acc[...]
            if scale is not None:
                r = r * scale
            if res_ref is not None:
                r = r + res_ref[...]
            o_ref[...] = r.astype(o_ref.dtype)
            if h_ref is not None:
                h_ref[...] = ((r * _rms_scale(r)) * gain_ref[...]).astype(BF16)

    ops = [p[0] for p in pairs] + [p[1] for p in pairs]
    specs = [a_spec] * n + [b_spec] * n
    if res is not None:
        ops.append(res)
        specs.append(res_spec)
    if norm_gain is not None:
        ops.append(norm_gain)
        specs.append(pl.BlockSpec(norm_gain.shape, lambda i, j, k: (0, 0)))
        out_spec = [out_spec, out_spec]
        out_shape = [out_shape, jax.ShapeDtypeStruct(out_shape.shape, BF16)]
    return pl.pallas_call(
        body, name=name, grid=grid, in_specs=specs, out_specs=out_spec, out_shape=out_shape,
        scratch_shapes=[pltpu.VMEM(acc_shape, F32)],
        compiler_params=_cparams(("parallel", "parallel", "arbitrary")),
    )(*ops)


def _mm_nn(name, pairs, out_dtype, res=None, norm_gain=None, tm=1024, tn=1024, tk=2048):
    m, kd = pairs[0][0].shape
    nd = pairs[0][1].shape[1]
    if norm_gain is not None:
        tm, tn = 512, nd
    tm, tn, tk = _tile(m, tm, 16), _tile(nd, tn, LANE), _tile(kd, tk, LANE)
    return _matmul(
        name, pairs,
        pl.BlockSpec((tm, tk), lambda i, j, k: (i, k)),
        pl.BlockSpec((tk, tn), lambda i, j, k: (k, j)),
        jax.ShapeDtypeStruct((m, nd), out_dtype),
        pl.BlockSpec((tm, tn), lambda i, j, k: (i, j)),
        (m // tm, nd // tn, kd // tk), (tm, tn), _NN,
        res=res, res_spec=pl.BlockSpec((tm, tn), lambda i, j, k: (i, j)), norm_gain=norm_gain)


def _mm_nt(name, pairs, out_dtype, tm=1024, tn=1024, tk=2048):
    m, kd = pairs[0][0].shape
    nd = pairs[0][1].shape[0]
    tm, tn, tk = _tile(m, tm, 16), _tile(nd, tn, LANE), _tile(kd, tk, LANE)
    return _matmul(
        name, pairs,
        pl.BlockSpec((tm, tk), lambda i, j, k: (i, k)),
        pl.BlockSpec((tn, tk), lambda i, j, k: (j, k)),
        jax.ShapeDtypeStruct((m, nd), out_dtype),
        pl.BlockSpec((tm, tn), lambda i, j, k: (i, j)),
        (m // tm, nd // tn, kd // tk), (tm, tn), _NT)


def _mm_tn(name, pairs, out_dtype, tm=1024, tn=2048, tk=512):
    kd, m = pairs[0][0].shape
    nd = pairs[0][1].shape[1]
    tm, tn, tk = _tile(m, tm, LANE), _tile(nd, tn, LANE), _tile(kd, tk, 16)
    return _matmul(
        name, pairs,
        pl.BlockSpec((tk, tm), lambda i, j, k: (k, i)),
        pl.BlockSpec((tk, tn), lambda i, j, k: (k, j)),
        jax.ShapeDtypeStruct((m, nd), out_dtype),
        pl.BlockSpec((tm, tn), lambda i, j, k: (i, j)),
        (m // tm, nd // tn, kd // tk), (tm, tn), _TN)


def _rms_fwd(name, x, gain):
    m, d = x.shape
    tm = _tile(m, 512, 16)

    def body(x_ref, g_ref, o_ref):
        xf = x_ref[...]
        r = lax.rsqrt(jnp.mean(xf * xf, axis=-1, keepdims=True) + RMS_EPS)
        o_ref[...] = ((xf * r) * g_ref[...]).astype(o_ref.dtype)

    return pl.pallas_call(
        body, name=name, grid=(m // tm,),
        in_specs=[pl.BlockSpec((tm, d), lambda i: (i, 0)), pl.BlockSpec((1, d), lambda i: (0, 0))],
        out_specs=pl.BlockSpec((tm, d), lambda i: (i, 0)),
        out_shape=jax.ShapeDtypeStruct((m, d), BF16),
        compiler_params=_cparams(("parallel",)),
    )(x, gain)


def _rms_bwd_rows(xf, gain, dhf):
    r = _rms_scale(xf)
    xh = xf * r
    t = dhf * gain
    dx = r * (t - xh * jnp.mean(t * xh, axis=-1, keepdims=True))
    return dx, jnp.sum(dhf * xh, axis=0, keepdims=True)


def _rms_gain_grad(name, x, gain, dh):
    m, d = x.shape
    tm = _tile(m, 512, 16)

    def body(x_ref, g_ref, dh_ref, dg_ref):
        @pl.when(pl.program_id(0) == 0)
        def _():
            dg_ref[...] = jnp.zeros_like(dg_ref)

        dg_ref[...] += _rms_bwd_rows(x_ref[...], g_ref[...], dh_ref[...])[1]

    row = pl.BlockSpec((tm, d), lambda i: (i, 0))
    vec = pl.BlockSpec((1, d), lambda i: (0, 0))
    return pl.pallas_call(
        body, name=name, grid=(m // tm,), in_specs=[row, vec, row], out_specs=vec,
        out_shape=jax.ShapeDtypeStruct((1, d), F32),
        compiler_params=_cparams(("arbitrary",)),
    )(x, gain, dh)


def _loss_head(x, target, gain):
    m, d = x.shape
    tm = _tile(m, 512, 16)

    def body(x_ref, t_ref, g_ref, dx_ref, dxb_ref, dg_ref, loss_ref):
        i = pl.program_id(0)

        @pl.when(i == 0)
        def _():
            dg_ref[...] = jnp.zeros_like(dg_ref)
            loss_ref[...] = jnp.zeros_like(loss_ref)

        xf = x_ref[...]
        r = lax.rsqrt(jnp.mean(xf * xf, axis=-1, keepdims=True) + RMS_EPS)
        xh = xf * r
        g = g_ref[...]
        err = xh * g - t_ref[...]
        loss_ref[...] += jnp.full(loss_ref.shape, (0.5 / d) * jnp.sum(err * err), F32)
        dy = err * (1.0 / d)
        t = dy * g
        dx = r * (t - xh * jnp.mean(t * xh, axis=-1, keepdims=True))
        dx_ref[...] = dx
        dxb_ref[...] = dx.astype(BF16)
        dg_ref[...] += jnp.sum(dy * xh, axis=0, keepdims=True)

    row = pl.BlockSpec((tm, d), lambda i: (i, 0))
    vec = pl.BlockSpec((1, d), lambda i: (0, 0))
    return pl.pallas_call(
        body, name="loss_head", grid=(m // tm,),
        in_specs=[row, row, vec],
        out_specs=[row, row, vec, pl.BlockSpec((1, LANE), lambda i: (0, 0))],
        out_shape=[jax.ShapeDtypeStruct((m, d), F32), jax.ShapeDtypeStruct((m, d), BF16),
                   jax.ShapeDtypeStruct((1, d), F32), jax.ShapeDtypeStruct((1, LANE), F32)],
        compiler_params=_cparams(("arbitrary",)),
    )(x, target, gain)


def _hosted_call(name, body, grid, in_specs, out_specs, out_shape, scratch, operands, exchange):
    n_in, n_out, n_scr = len(in_specs), len(out_specs), len(scratch)
    n_xin, n_xout = len(exchange.ins), len(exchange.out_shape)

    def full_body(*refs):
        pos = 0
        parts = []
        for cnt in (n_in, n_xin, n_out, n_xout, n_scr):
            parts.append(refs[pos:pos + cnt])
            pos += cnt
        ins, x_ins, outs, x_outs, scr = parts
        sems = refs[pos:]
        first = pl.program_id(0) == 0
        last = pl.program_id(0) == grid[0] - 1
        for ax in range(1, len(grid)):
            first = jnp.logical_and(first, pl.program_id(ax) == 0)
            last = jnp.logical_and(last, pl.program_id(ax) == grid[ax] - 1)

        @pl.when(first)
        def _():
            exchange.start(x_ins, x_outs, sems)

        body(ins, outs, scr)

        @pl.when(last)
        def _():
            exchange.finish(x_ins, x_outs, sems)

    res = pl.pallas_call(
        full_body, name=name, grid=grid,
        in_specs=list(in_specs) + [_HBM] * n_xin, out_specs=list(out_specs) + [_HBM] * n_xout,
        out_shape=list(out_shape) + list(exchange.out_shape),
        scratch_shapes=list(scratch) + list(exchange.sems),
        compiler_params=_cparams(("arbitrary",) * len(grid)),
    )(*operands, *exchange.ins)
    return res[:n_out], res[n_out:]


def _ffn_fwd(name, x, h, wg, wu, wd, next_gain, exchange):
    n, d = x.shape
    nb, _, fb = wg.shape
    tm = _tile(n, 512, 16 * FFN_SUBTILES)
    ts = tm // FFN_SUBTILES
    with_norm = next_gain is not None

    def body(ins, outs, scr):
        x_ref, h_ref, wg_ref, wu_ref, wd_ref = ins[:5]
        xo_ref, a_ref, b_ref = outs[:3]
        acc, = scr
        j = pl.program_id(1)

        @pl.when(j == 0)
        def _():
            acc[...] = jnp.zeros_like(acc)

        for s in range(FFN_SUBTILES):
            rows = slice(s * ts, (s + 1) * ts)
            hh = h_ref[rows, :]
            a = _dot(hh, wg_ref[...])
            b = _dot(hh, wu_ref[...])
            a_ref[rows, :] = a.astype(BF16)
            b_ref[rows, :] = b.astype(BF16)
            act = (a * jax.nn.sigmoid(a)) * b
            acc[rows, :] += _dot(act, wd_ref[...])

        @pl.when(j == nb - 1)
        def _():
            xo = x_ref[...] + 0.5 * acc[...]
            xo_ref[...] = xo
            if with_norm:
                outs[3][...] = ((xo * _rms_scale(xo)) * ins[5][...]).astype(BF16)

    row = pl.BlockSpec((tm, d), lambda i, j: (i, 0))
    w_in = pl.BlockSpec((None, d, fb), lambda i, j: (j, 0, 0))
    hid = pl.BlockSpec((None, tm, fb), lambda i, j: (j, i, 0))
    return _hosted_call(
        name, body, (n // tm, nb),
        [row, row, w_in, w_in, pl.BlockSpec((None, fb, d), lambda i, j: (j, 0, 0))]
        + ([pl.BlockSpec((1, d), lambda i, j: (0, 0))] if with_norm else []),
        [row, hid, hid] + ([row] if with_norm else []),
        [jax.ShapeDtypeStruct((n, d), F32), jax.ShapeDtypeStruct((nb, n, fb), BF16),
         jax.ShapeDtypeStruct((nb, n, fb), BF16)]
        + ([jax.ShapeDtypeStruct((n, d), BF16)] if with_norm else []),
        [pltpu.VMEM((tm, d), F32)], (x, h, wg, wu, wd) + ((next_gain,) if with_norm else ()), exchange)


def _ffn_bwd_act(name, dxb, wd, a, b, exchange):
    n, d = dxb.shape
    nb, fb, _ = wd.shape
    tm = _tile(n, 512, 16 * FFN_SUBTILES)
    ts = tm // FFN_SUBTILES

    def body(ins, outs, scr):
        del scr
        dx_ref, wd_ref, a_ref, b_ref = ins
        da_ref, db_ref, act_ref = outs
        for s in range(FFN_SUBTILES):
            rows = slice(s * ts, (s + 1) * ts)
            dact = 0.5 * _dot(dx_ref[rows, :], wd_ref[...], _NT)
            af = a_ref[rows, :].astype(F32)
            bf = b_ref[rows, :].astype(F32)
            sl, dsl = _silu_parts(af)
            act_ref[rows, :] = (sl * bf).astype(BF16)
            db_ref[rows, :] = (dact * sl).astype(BF16)
            da_ref[rows, :] = (dact * bf * dsl).astype(BF16)

    hid = pl.BlockSpec((None, tm, fb), lambda i, j: (j, i, 0))
    shp = jax.ShapeDtypeStruct((nb, n, fb), BF16)
    return _hosted_call(
        name, body, (n // tm, nb),
        [pl.BlockSpec((tm, d), lambda i, j: (i, 0)), pl.BlockSpec((None, fb, d), lambda i, j: (j, 0, 0)), hid, hid],
        [hid, hid, hid], [shp, shp, shp], [], (dxb, wd, a, b), exchange)


def _proj_rms_bwd(name, pairs, a_spec, b_spec, nk, tm, x, gain, dres, exchange):
    n, d = x.shape
    n_pairs = len(pairs)
    slab = _tile(tm, 128, 16)

    def body(ins, outs, scr):
        a_refs, b_refs = ins[:n_pairs], ins[n_pairs:2 * n_pairs]
        x_ref, g_ref, res_ref = ins[2 * n_pairs:]
        dx_ref, dxb_ref, dg_ref = outs
        del scr
        i, k = pl.program_id(0), pl.program_id(1)

        @pl.when(jnp.logical_and(i == 0, k == 0))
        def _():
            dg_ref[...] = jnp.zeros_like(dg_ref)

        @pl.when(k == 0)
        def _():
            dx_ref[...] = jnp.zeros_like(dx_ref)

        for a_ref, b_ref in zip(a_refs, b_refs):
            dx_ref[...] += _dot(a_ref[...], b_ref[...], _NT)

        @pl.when(k == nk - 1)
        def _():
            for s in range(tm // slab):
                rows = slice(s * slab, (s + 1) * slab)
                dx, dg = _rms_bwd_rows(x_ref[rows, :], g_ref[...], dx_ref[rows, :])
                dx = dx + res_ref[rows, :]
                dx_ref[rows, :] = dx
                dxb_ref[rows, :] = dx.astype(BF16)
                dg_ref[...] += dg

    row = pl.BlockSpec((tm, d), lambda i, k: (i, 0))
    vec = pl.BlockSpec((1, d), lambda i, k: (0, 0))
    return _hosted_call(
        name, body, (n // tm, nk), [a_spec] * n_pairs + [b_spec] * n_pairs + [row, vec, row],
        [row, row, vec],
        [jax.ShapeDtypeStruct((n, d), F32), jax.ShapeDtypeStruct((n, d), BF16),
         jax.ShapeDtypeStruct((1, d), F32)],
        [], tuple(p[0] for p in pairs) + tuple(p[1] for p in pairs) + (x, gain, dres), exchange)


def _ffn_bwd_dh(name, da, db, wg, wu, x, gain, dres, exchange):
    nb, n, fb = da.shape
    d = wg.shape[1]
    tm = _tile(n, 512, 16)
    return _proj_rms_bwd(
        name, [(da, wg), (db, wu)],
        pl.BlockSpec((None, tm, fb), lambda i, k: (k, i, 0)),
        pl.BlockSpec((None, d, fb), lambda i, k: (k, 0, 0)), nb, tm, x, gain, dres, exchange)


def _mm_nt_rms_bwd(name, a, w, x, gain, dres, exchange, tk=2048):
    n, kd = a.shape
    d = w.shape[0]
    tm, tk = _tile(n, 512, 16), _tile(kd, tk, LANE)
    return _proj_rms_bwd(
        name, [(a, w)],
        pl.BlockSpec((tm, tk), lambda i, k: (i, k)),
        pl.BlockSpec((d, tk), lambda i, k: (0, k)), kd // tk, tm, x, gain, dres, exchange)


def _ffn_bwd_weights(name, h, dxb, da, db, act, exchange):
    n, d = h.shape
    nb, _, fb = da.shape
    tk = _tile(n, 512, 16)
    nk = n // tk

    def body(ins, outs, scr):
        h_ref, dx_ref, da_ref, db_ref, act_ref = ins
        dwg_ref, dwu_ref, dwd_ref = outs
        accg, accu, accd = scr
        k = pl.program_id(1)

        @pl.when(k == 0)
        def _():
            accg[...] = jnp.zeros_like(accg)
            accu[...] = jnp.zeros_like(accu)
            accd[...] = jnp.zeros_like(accd)

        hh = h_ref[...]
        accg[...] += _dot(hh, da_ref[...], _TN)
        accu[...] += _dot(hh, db_ref[...], _TN)
        accd[...] += _dot(act_ref[...], dx_ref[...], _TN)

        @pl.when(k == nk - 1)
        def _():
            dwg_ref[...] = accg[...].astype(BF16)
            dwu_ref[...] = accu[...].astype(BF16)
            dwd_ref[...] = (0.5 * accd[...]).astype(BF16)

    row = pl.BlockSpec((tk, d), lambda j, k: (k, 0))
    hid = pl.BlockSpec((None, tk, fb), lambda j, k: (j, k, 0))
    w_in = pl.BlockSpec((None, d, fb), lambda j, k: (j, 0, 0))
    w_out = pl.BlockSpec((None, fb, d), lambda j, k: (j, 0, 0))
    return _hosted_call(
        name, body, (nb, nk), [row, row, hid, hid, hid], [w_in, w_in, w_out],
        [jax.ShapeDtypeStruct((nb, d, fb), BF16), jax.ShapeDtypeStruct((nb, d, fb), BF16),
         jax.ShapeDtypeStruct((nb, fb, d), BF16)],
        [pltpu.VMEM((d, fb), F32), pltpu.VMEM((d, fb), F32), pltpu.VMEM((fb, d), F32)],
        (h, dxb, da, db, act), exchange)


def _pool_diff(ext_ref, rows, cols, width, t_idx):
    s = ext_ref[POOL_HALO:POOL_HALO + rows, cols]
    for sft in range(1, width):
        s = s + ext_ref[POOL_HALO - sft:POOL_HALO - sft + rows, cols]
    cnt = jnp.minimum(t_idx + 1, width).astype(F32)
    return s / cnt - ext_ref[POOL_HALO:POOL_HALO + rows, cols]


def _pool_fwd(proj, pool_w, pool_scale):
    n = proj.shape[0]
    dp = pool_scale.shape[1]
    c = dp // len(POOL_WINDOWS)
    tm = _tile(n, 512, POOL_HALO)
    hb = tm // POOL_HALO

    def body(u_ref, halo_ref, pw_ref, sc_ref, y_ref, ext):
        i = pl.program_id(0)
        ext[0:POOL_HALO, :] = jnp.where(i > 0, halo_ref[...], 0.0)
        ext[POOL_HALO:, :] = u_ref[...]
        t_idx = i * tm + lax.broadcasted_iota(jnp.int32, (tm, 1), 0)
        for g, width in enumerate(POOL_WINDOWS):
            cols = slice(g * c, (g + 1) * c)
            dgrp = _pool_diff(ext, tm, cols, width, t_idx)
            y_ref[:, cols] = (_dot(dgrp, pw_ref[g]) * sc_ref[:, cols]).astype(BF16)

    return pl.pallas_call(
        body, name="pool_fwd", grid=(n // tm,),
        in_specs=[pl.BlockSpec((tm, dp), lambda i: (i, 0)),
                  pl.BlockSpec((POOL_HALO, dp), lambda i: (jnp.maximum(i * hb - 1, 0), 0)),
                  pl.BlockSpec((len(POOL_WINDOWS), c, c), lambda i: (0, 0, 0)),
                  pl.BlockSpec((1, dp), lambda i: (0, 0))],
        out_specs=pl.BlockSpec((tm, dp), lambda i: (i, 0)),
        out_shape=jax.ShapeDtypeStruct((n, dp), BF16),
        scratch_shapes=[pltpu.VMEM((tm + POOL_HALO, dp), F32)],
        compiler_params=_cparams(("parallel",)),
    )(proj, proj, pool_w, pool_scale)


def _pool_bwd(proj, dymix, pool_w, pool_scale):
    n = proj.shape[0]
    dp = pool_scale.shape[1]
    ng = len(POOL_WINDOWS)
    c = dp // ng
    tm = _tile(n, 512, POOL_HALO)
    hb = tm // POOL_HALO
    nsteps = n // tm
    last_halo = n // POOL_HALO - 1

    def body(u_ref, halo_ref, dy_ref, dyn_ref, pw_ref, sc_ref, du_ref, dpw_ref, dsc_ref, ext, dyext, e_s):
        i = pl.program_id(0)

        @pl.when(i == 0)
        def _():
            dpw_ref[...] = jnp.zeros_like(dpw_ref)
            dsc_ref[...] = jnp.zeros_like(dsc_ref)

        ext[0:POOL_HALO, :] = jnp.where(i > 0, halo_ref[...], 0.0)
        ext[POOL_HALO:, :] = u_ref[...]
        dyext[0:tm, :] = dy_ref[...]
        dyext[tm:, :] = jnp.where(i < nsteps - 1, dyn_ref[...], 0.0)
        t_idx = i * tm + lax.broadcasted_iota(jnp.int32, (tm, 1), 0)
        te_idx = i * tm + lax.broadcasted_iota(jnp.int32, (tm + POOL_HALO, 1), 0)
        for g, width in enumerate(POOL_WINDOWS):
            cols = slice(g * c, (g + 1) * c)
            dgrp = _pool_diff(ext, tm, cols, width, t_idx)
            w_g = pw_ref[g]
            dys = dyext[:, cols] * sc_ref[:, cols]
            ypre = _dot(dgrp, w_g)
            dsc_ref[:, cols] += jnp.sum(dyext[0:tm, cols] * ypre, axis=0, keepdims=True)
            dpw_ref[g] += _dot(dgrp, dys[0:tm], _TN)
            dd = _dot(dys, w_g, _NT)
            e_s[...] = dd / jnp.minimum(te_idx + 1, width).astype(F32)
            acc = e_s[0:tm, :]
            for sft in range(1, width):
                acc = acc + e_s[sft:sft + tm, :]
            du_ref[:, cols] = (acc - dd[0:tm]).astype(BF16)

    return pl.pallas_call(
        body, name="pool_bwd", grid=(nsteps,),
        in_specs=[pl.BlockSpec((tm, dp), lambda i: (i, 0)),
                  pl.BlockSpec((POOL_HALO, dp), lambda i: (jnp.maximum(i * hb - 1, 0), 0)),
                  pl.BlockSpec((tm, dp), lambda i: (i, 0)),
                  pl.BlockSpec((POOL_HALO, dp), lambda i: (jnp.minimum((i + 1) * hb, last_halo), 0)),
                  pl.BlockSpec((ng, c, c), lambda i: (0, 0, 0)),
                  pl.BlockSpec((1, dp), lambda i: (0, 0))],
        out_specs=[pl.BlockSpec((tm, dp), lambda i: (i, 0)),
                   pl.BlockSpec((ng, c, c), lambda i: (0, 0, 0)),
                   pl.BlockSpec((1, dp), lambda i: (0, 0))],
        out_shape=[jax.ShapeDtypeStruct((n, dp), BF16), jax.ShapeDtypeStruct((ng, c, c), F32),
                   jax.ShapeDtypeStruct((1, dp), F32)],
        scratch_shapes=[pltpu.VMEM((tm + POOL_HALO, dp), F32), pltpu.VMEM((tm + POOL_HALO, dp), F32),
                        pltpu.VMEM((tm + POOL_HALO, c), F32)],
        compiler_params=_cparams(("arbitrary",)),
    )(proj, proj, dymix, dymix, pool_w, pool_scale)


def _chunk_cumsum(x):
    row = lax.broadcasted_iota(jnp.int32, x.shape, 0) % CHUNK
    s = 1
    while s < CHUNK:
        x = x + jnp.where(row >= s, pltpu.roll(x, s, 0), 0.0)
        s *= 2
    return x


def _gate_logits(alr_ref, wa_ref, ba_ref):
    z = _dot(alr_ref[...], wa_ref[...]) + ba_ref[...]
    la = (jnp.minimum(z, 0.0) - jnp.log(1.0 + jnp.exp(-jnp.abs(z)))) * (1.0 / GATE_TEMP)
    return z, la


def _gla_dims(proj, head_norm):
    n, pw = proj.shape
    dv4 = head_norm.shape[1]
    dk4 = dv4 // 2
    return n, pw, dv4, dk4, dk4 // HEADS, dv4 // HEADS


def _gla_in_specs(t, dk4, dv4, rev):
    alr_blk = (2 * dv4 + 2 * dv4) // LANE
    return [pl.BlockSpec((t, dk4), lambda i: (rev(i), 2)),
            pl.BlockSpec((t, dk4), lambda i: (rev(i), 3)),
            pl.BlockSpec((t, dv4), lambda i: (rev(i), 2)),
            pl.BlockSpec((t, dv4), lambda i: (rev(i), 3)),
            pl.BlockSpec((t, LANE), lambda i: (rev(i), alr_blk))]


def _gla_fwd(proj, wa2p, b_a, head_norm):
    n, _, dv4, dk4, dk, dv = _gla_dims(proj, head_norm)
    t = _tile(n, GLA_ROWS, CHUNK)
    nc = t // CHUNK
    qscale = dk ** -0.5

    def body(q_ref, k_ref, v_ref, g_ref, alr_ref, wa_ref, ba_ref, hn_ref, y_ref, st_ref, state, cum_s):
        i = pl.program_id(0)

        @pl.when(i == 0)
        def _():
            state[...] = jnp.zeros_like(state)

        _, la = _gate_logits(alr_ref, wa_ref, ba_ref)
        cum_s[...] = _chunk_cumsum(la)
        for c in range(nc):
            rows = slice(c * CHUNK, (c + 1) * CHUNK)
            bend = cum_s[c * CHUNK + CHUNK - 1:(c + 1) * CHUNK, :]
            dec = jnp.exp(bend - cum_s[rows, :])
            e = jnp.exp(bend)
            kd = (k_ref[rows, :] * dec).astype(BF16)
            qs = (q_ref[rows, :] * qscale).astype(BF16)
            vv = v_ref[rows, :].astype(BF16)
            for h in range(HEADS):
                hk = slice(h * dk, (h + 1) * dk)
                hv = slice(h * dv, (h + 1) * dv)
                s_new = state[h] * e[:, hk] + _dot(vv[:, hv], kd[:, hk], _TN)
                state[h] = s_new
                st_ref[c, h] = s_new
                o = _dot(qs[:, hk], s_new, _NT)
                r = lax.rsqrt(jnp.mean(o * o, axis=-1, keepdims=True) + RMS_EPS)
                gg = g_ref[rows, hv]
                y_ref[rows, hv] = (((o * r) * hn_ref[:, hv]) * (gg * jax.nn.sigmoid(gg))).astype(BF16)

    full = lambda shape: pl.BlockSpec(shape, lambda i: tuple(0 for _ in shape))
    return pl.pallas_call(
        body, name="gla_fwd", grid=(n // t,),
        in_specs=_gla_in_specs(t, dk4, dv4, lambda i: i)
        + [full((LANE, dk4)), full((1, dk4)), full((1, dv4))],
        out_specs=[pl.BlockSpec((t, dv4), lambda i: (i, 0)),
                   pl.BlockSpec((nc, HEADS, dv, dk), lambda i: (i, 0, 0, 0))],
        out_shape=[jax.ShapeDtypeStruct((n, dv4), BF16),
                   jax.ShapeDtypeStruct((n // CHUNK, HEADS, dv, dk), F32)],
        scratch_shapes=[pltpu.VMEM((HEADS, dv, dk), F32), pltpu.VMEM((t, dk4), F32)],
        compiler_params=_cparams(("arbitrary",)),
    )(proj, proj, proj, proj, proj, wa2p, b_a, head_norm)


def _gla_bwd(proj, dymix, du, states, wa2p, b_a, head_norm):
    n, pw, dv4, dk4, dk, dv = _gla_dims(proj, head_norm)
    t = _tile(n, GLA_ROWS, CHUNK)
    nc = t // CHUNK
    nsteps = n // t
    qscale = dk ** -0.5
    rev = lambda i: nsteps - 1 - i

    def body(q_ref, k_ref, v_ref, g_ref, alr_ref, dy_ref, du_ref, st_ref, prev_ref, wa_ref, ba_ref, hn_ref,
             dp_ref, dwa_ref, dba_ref, dhn_ref, carry, cum_s, dz_s):
        i = pl.program_id(0)

        @pl.when(i == 0)
        def _():
            carry[...] = jnp.zeros_like(carry)
            dwa_ref[...] = jnp.zeros_like(dwa_ref)
            dba_ref[...] = jnp.zeros_like(dba_ref)
            dhn_ref[...] = jnp.zeros_like(dhn_ref)

        first_step = i == nsteps - 1
        z, la = _gate_logits(alr_ref, wa_ref, ba_ref)
        one_minus_sig = 1.0 - jax.nn.sigmoid(z)
        cum_s[...] = _chunk_cumsum(la)
        dp_ref[:, 0:dv4] = du_ref[...]
        for c in reversed(range(nc)):
            rows = slice(c * CHUNK, (c + 1) * CHUNK)
            bend = cum_s[c * CHUNK + CHUNK - 1:(c + 1) * CHUNK, :]
            dec = jnp.exp(bend - cum_s[rows, :])
            e = jnp.exp(bend)
            kd_f = k_ref[rows, :] * dec
            kd = kd_f.astype(BF16)
            qs = (q_ref[rows, :] * qscale).astype(BF16)
            vv = v_ref[rows, :].astype(BF16)
            dkd_parts, dee_parts = [], []
            for h in range(HEADS):
                hk = slice(h * dk, (h + 1) * dk)
                hv = slice(h * dv, (h + 1) * dv)
                s_n = st_ref[c, h]
                if c > 0:
                    s_prev = st_ref[c - 1, h]
                else:
                    s_prev = jnp.where(first_step, 0.0, prev_ref[0, h])
                o = _dot(qs[:, hk], s_n, _NT)
                r = lax.rsqrt(jnp.mean(o * o, axis=-1, keepdims=True) + RMS_EPS)
                oh = o * r
                gg = g_ref[rows, hv]
                sl, dsl = _silu_parts(gg)
                dyh = dy_ref[rows, hv]
                hn = hn_ref[:, hv]
                tt = dyh * sl
                dhn_ref[:, hv] += jnp.sum(tt * oh, axis=0, keepdims=True)
                dg = dyh * (oh * hn) * dsl
                tt = tt * hn
                do = (r * (tt - oh * jnp.mean(tt * oh, axis=-1, keepdims=True))).astype(BF16)
                g_n = carry[h] + _dot(do, qs[:, hk], _TN)
                dq = _dot(do, s_n) * qscale
                dkd = _dot(vv[:, hv], g_n)
                dvh = _dot(kd[:, hk], g_n, _NT)
                de = jnp.sum(g_n * s_prev, axis=0, keepdims=True)
                carry[h] = g_n * e[:, hk]
                dp_ref[rows, dv4 + h * dk:dv4 + (h + 1) * dk] = dq.astype(BF16)
                dp_ref[rows, 2 * dv4 + h * dv:2 * dv4 + (h + 1) * dv] = dvh.astype(BF16)
                dp_ref[rows, 3 * dv4 + h * dv:3 * dv4 + (h + 1) * dv] = dg.astype(BF16)
                dkd_parts.append(dkd)
                dee_parts.append(de * e[:, hk])
            dkd_c = jnp.concatenate(dkd_parts, axis=1)
            dee = jnp.concatenate(dee_parts, axis=1)
            dp_ref[rows, dv4 + dk4:dv4 + 2 * dk4] = (dkd_c * dec).astype(BF16)
            w = dkd_c * kd_f
            dla = (_chunk_cumsum(w) - w) + dee
            dz_s[rows, :] = dla * (1.0 / GATE_TEMP)
        dz = dz_s[...] * one_minus_sig
        dp_ref[:, 4 * dv4:4 * dv4 + LANE] = _dot(dz, wa_ref[...], _NT).astype(BF16)
        dwa_ref[...] += _dot(alr_ref[...], dz, _TN)
        dba_ref[...] += jnp.sum(dz, axis=0, keepdims=True)

    full = lambda shape: pl.BlockSpec(shape, lambda i: tuple(0 for _ in shape))
    return pl.pallas_call(
        body, name="gla_bwd", grid=(nsteps,),
        in_specs=_gla_in_specs(t, dk4, dv4, rev)
        + [pl.BlockSpec((t, dv4), lambda i: (rev(i), 1)),
           pl.BlockSpec((t, dv4), lambda i: (rev(i), 0)),
           pl.BlockSpec((nc, HEADS, dv, dk), lambda i: (rev(i), 0, 0, 0)),
           pl.BlockSpec((1, HEADS, dv, dk), lambda i: (jnp.maximum(rev(i) * nc - 1, 0), 0, 0, 0)),
           full((LANE, dk4)), full((1, dk4)), full((1, dv4))],
        out_specs=[pl.BlockSpec((t, pw), lambda i: (rev(i), 0)),
                   full((LANE, dk4)), full((1, dk4)), full((1, dv4))],
        out_shape=[jax.ShapeDtypeStruct((n, pw), BF16), jax.ShapeDtypeStruct((LANE, dk4), F32),
                   jax.ShapeDtypeStruct((1, dk4), F32), jax.ShapeDtypeStruct((1, dv4), F32)],
        scratch_shapes=[pltpu.VMEM((HEADS, dv, dk), F32), pltpu.VMEM((t, dk4), F32),
                        pltpu.VMEM((t, dk4), F32)],
        compiler_params=_cparams(("arbitrary",)),
    )(proj, proj, proj, proj, proj, dymix, du, states, states, wa2p, b_a, head_norm)


def _softmax_rows(q, k, scale):
    s = _dot(q, k, _NT) * scale
    p = jnp.exp(s - jnp.max(s, axis=-1, keepdims=True))
    return p / jnp.sum(p, axis=-1, keepdims=True)


def _attn_fwd(q, kv):
    n, d = q.shape
    m = kv.shape[0]
    dh = d // HEADS
    tm = _tile(n, 512, 16)
    scale = dh ** -0.5

    def body(q_ref, k_ref, v_ref, o_ref):
        for h in range(HEADS):
            hs = slice(h * dh, (h + 1) * dh)
            p = _softmax_rows(q_ref[:, hs], k_ref[:, hs], scale)
            o_ref[:, hs] = _dot(p, v_ref[:, hs]).astype(BF16)

    return pl.pallas_call(
        body, name="attn_fwd", grid=(n // tm,),
        in_specs=[pl.BlockSpec((tm, d), lambda i: (i, 0)), pl.BlockSpec((m, d), lambda i: (0, 0)),
                  pl.BlockSpec((m, d), lambda i: (0, 1))],
        out_specs=pl.BlockSpec((tm, d), lambda i: (i, 0)),
        out_shape=jax.ShapeDtypeStruct((n, d), BF16),
        compiler_params=_cparams(("parallel",)),
    )(q, kv, kv)


def _attn_bwd(q, kv, do):
    n, d = q.shape
    m = kv.shape[0]
    dh = d // HEADS
    tm = _tile(n, 512, 16)
    scale = dh ** -0.5

    def body(q_ref, k_ref, v_ref, do_ref, dq_ref, dk_ref, dv_ref):
        i = pl.program_id(0)

        @pl.when(i == 0)
        def _():
            dk_ref[...] = jnp.zeros_like(dk_ref)
            dv_ref[...] = jnp.zeros_like(dv_ref)

        for h in range(HEADS):
            hs = slice(h * dh, (h + 1) * dh)
            qh, kh, vh, doh = q_ref[:, hs], k_ref[:, hs], v_ref[:, hs], do_ref[:, hs]
            p = _softmax_rows(qh, kh, scale)
            dv_ref[:, hs] += _dot(p, doh, _TN)
            dp = _dot(doh, vh, _NT)
            ds = p * (dp - jnp.sum(dp * p, axis=-1, keepdims=True)) * scale
            dq_ref[:, hs] = _dot(ds, kh).astype(BF16)
            dk_ref[:, hs] += _dot(ds, qh, _TN)

    row = pl.BlockSpec((tm, d), lambda i: (i, 0))
    memb = pl.BlockSpec((m, d), lambda i: (0, 0))
    return pl.pallas_call(
        body, name="attn_bwd", grid=(n // tm,),
        in_specs=[row, memb, pl.BlockSpec((m, d), lambda i: (0, 1)), row],
        out_specs=[row, memb, memb],
        out_shape=[jax.ShapeDtypeStruct((n, d), BF16), jax.ShapeDtypeStruct((m, d), F32),
                   jax.ShapeDtypeStruct((m, d), F32)],
        compiler_params=_cparams(("arbitrary",)),
    )(q, kv, kv, do)


def _adamw(name, w, m, v, parts, own_block):
    r, c = w.shape
    tr = _tile(r, max(16, ELEMENTWISE_BLOCK // c), 16)
    row = pl.BlockSpec((tr, c), lambda i, o: (i, 0))
    ops, specs = [w, m, v], [row, row, row]
    for p in parts:
        if p.ndim == 2:
            ops.append(p)
            specs.append(row)
        elif p.shape[0] == 4:
            ops.append(p)
            specs.append(pl.BlockSpec((None, tr, c), lambda i, o: (o[0], i, 0)))
        else:
            for s in range(p.shape[0]):
                ops.append(p)
                specs.append(pl.BlockSpec((None, tr, c), lambda i, o, s=s: (s, i, 0)))
    n_parts = len(ops) - 3
    c1 = 1.0 - ADAM_B1 ** ADAM_STEP
    c2 = 1.0 - ADAM_B2 ** ADAM_STEP

    def body(o_ref, *refs):
        del o_ref
        w_ref, m_ref, v_ref = refs[:3]
        g_refs = refs[3:3 + n_parts]
        go_ref, d_ref, mo_ref, vo_ref = refs[3 + n_parts:]
        g = g_refs[0][...].astype(F32)
        for g_ref in g_refs[1:]:
            g = g + g_ref[...].astype(F32)
        m_new = ADAM_B1 * m_ref[...] + (1.0 - ADAM_B1) * g
        v_new = ADAM_B2 * v_ref[...] + (1.0 - ADAM_B2) * (g * g)
        m_hat = m_new / c1
        v_hat = v_new / c2
        go_ref[...] = g
        d_ref[...] = -ADAM_LR * (m_hat / (jnp.sqrt(v_hat) + ADAM_EPS) + ADAM_WD * w_ref[...])
        mo_ref[...] = m_new
        vo_ref[...] = v_new

    shp = jax.ShapeDtypeStruct((r, c), F32)
    return pl.pallas_call(
        body, name=name,
        grid_spec=pltpu.PrefetchScalarGridSpec(
            num_scalar_prefetch=1, grid=(r // tr,), in_specs=specs, out_specs=[row] * 4),
        out_shape=[shp] * 4,
        compiler_params=_cparams(("parallel",)),
    )(own_block, *ops)


def _pair_add(name, g, recvd, core):
    _, r, c = g.shape
    tr = _tile(r, max(16, ELEMENTWISE_BLOCK // c), 16)

    def body(core_ref, a_ref, b_ref, o_ref):
        del core_ref
        o_ref[...] = (a_ref[...].astype(F32) + b_ref[...].astype(F32)).astype(o_ref.dtype)

    blk = pl.BlockSpec((None, tr, c), lambda s, i, core_ref: (s, i, 0))
    mine = pl.BlockSpec((None, tr, c), lambda s, i, core_ref: (2 * s + core_ref[0], i, 0))
    return pl.pallas_call(
        body, name=name,
        grid_spec=pltpu.PrefetchScalarGridSpec(
            num_scalar_prefetch=1, grid=(4, r // tr), in_specs=[mine, blk], out_specs=blk),
        out_shape=jax.ShapeDtypeStruct(recvd.shape, g.dtype),
        compiler_params=_cparams(("parallel", "parallel")),
    )(core, g, recvd)


def _position():
    return lax.axis_index("x"), lax.axis_index("y"), lax.axis_index("c")


_HBM = pl.BlockSpec(memory_space=pltpu.HBM)

_Exchange = collections.namedtuple("_Exchange", "ins out_shape sems start finish")
_NO_EXCHANGE = _Exchange((), (), (), lambda ins, outs, sems: None, lambda ins, outs, sems: None)


def _all_gather(name, shards):
    n = len(shards)

    def body(*refs):
        ins, outs = refs[:n], refs[n:2 * n]
        send_sems, recv_sems, local_sems = refs[2 * n:]
        x, y, c = _position()
        me, sibling = (x, y, c), (x, y, 1 - c)
        chips = [(1 - x, y), (x, 1 - y), (1 - x, 1 - y)]

        def block(ref, px, py, pc):
            return ref.at[4 * px + 2 * py + pc]

        def copy(a, k, owner, to, src=None):
            return pltpu.make_async_remote_copy(
                src_ref=block(outs[a], *owner) if src is None else src, dst_ref=block(outs[a], *owner),
                send_sem=send_sems.at[7 * a + k], recv_sem=recv_sems.at[7 * a + k],
                device_id=to, device_id_type=MESH)

        started = []
        for a in range(n):
            mine = pltpu.make_async_copy(ins[a], block(outs[a], *me), local_sems.at[a])
            mine.start()
            started.append(mine)
        sends = []
        for a in range(n):
            sends.append(copy(a, 0, me, sibling, src=ins[a]))
            sends += [copy(a, 1 + j, me, (*chip, c), src=ins[a]) for j, chip in enumerate(chips)]
        for cp in sends:
            cp.start()
        for j, chip in enumerate(chips):
            for a in range(n):
                copy(a, 1 + j, (*chip, c), me).wait_recv()
                fwd = copy(a, 4 + j, (*chip, c), sibling)
                fwd.start()
                sends.append(fwd)
        for a in range(n):
            copy(a, 0, sibling, me).wait_recv()
            for j, chip in enumerate(chips):
                copy(a, 4 + j, (*chip, 1 - c), me).wait_recv()
        for cp in sends:
            cp.wait_send()
        for mine in started:
            mine.wait()

    return pl.pallas_call(
        body, name=name,
        in_specs=[_HBM] * n, out_specs=[_HBM] * n,
        out_shape=[jax.ShapeDtypeStruct((N_DEV,) + s.shape, s.dtype) for s in shards],
        scratch_shapes=[pltpu.SemaphoreType.DMA((7 * n,)), pltpu.SemaphoreType.DMA((7 * n,)),
                        pltpu.SemaphoreType.DMA((n,))],
    )(*shards)


def _pair_exchange(grads):
    n = len(grads)

    def start(ins, recvd, sems):
        send_sems, recv_sems = sems
        x, y, c = _position()
        for a in range(n):
            for chip in range(4):
                pltpu.make_async_remote_copy(
                    src_ref=ins[a].at[2 * chip + (1 - c)], dst_ref=recvd[a].at[chip],
                    send_sem=send_sems.at[a], recv_sem=recv_sems.at[a],
                    device_id=(x, y, 1 - c), device_id_type=MESH).start()

    def finish(ins, recvd, sems):
        del ins
        send_sems, recv_sems = sems
        x, y, c = _position()
        for a in range(n):
            pltpu.make_async_remote_copy(
                src_ref=recvd[a], dst_ref=recvd[a], send_sem=send_sems.at[a], recv_sem=recv_sems.at[a],
                device_id=(x, y, 1 - c), device_id_type=MESH).wait()

    return _Exchange(
        ins=tuple(grads),
        out_shape=tuple(jax.ShapeDtypeStruct((4,) + g.shape[1:], g.dtype) for g in grads),
        sems=(pltpu.SemaphoreType.DMA((n,)), pltpu.SemaphoreType.DMA((n,))),
        start=start, finish=finish)


def _run_exchange(name, exchange):
    n_in, n_out = len(exchange.ins), len(exchange.out_shape)

    def body(*refs):
        ins, outs, sems = refs[:n_in], refs[n_in:n_in + n_out], refs[n_in + n_out:]
        exchange.start(ins, outs, sems)
        exchange.finish(ins, outs, sems)

    return pl.pallas_call(
        body, name=name, in_specs=[_HBM] * n_in, out_specs=[_HBM] * n_out,
        out_shape=list(exchange.out_shape), scratch_shapes=list(exchange.sems),
    )(*exchange.ins)


def _chip_scatter_exchange(sums):
    n = len(sums)
    offsets = [(1, 0), (0, 1), (1, 1)]

    def start(ins, recvd, sems):
        send_sems, recv_sems = sems
        x, y, c = _position()
        for a in range(n):
            for r, (ox, oy) in enumerate(offsets):
                px = 1 - x if ox else x
                py = 1 - y if oy else y
                pltpu.make_async_remote_copy(
                    src_ref=ins[a].at[2 * px + py], dst_ref=recvd[a].at[r],
                    send_sem=send_sems.at[a], recv_sem=recv_sems.at[a],
                    device_id=(px, py, c), device_id_type=MESH).start()

    def finish(ins, recvd, sems):
        del ins
        send_sems, recv_sems = sems
        x, y, c = _position()
        for a in range(n):
            pltpu.make_async_remote_copy(
                src_ref=recvd[a], dst_ref=recvd[a], send_sem=send_sems.at[a], recv_sem=recv_sems.at[a],
                device_id=(x, y, c), device_id_type=MESH).wait()

    return _Exchange(
        ins=tuple(sums),
        out_shape=tuple(jax.ShapeDtypeStruct((3,) + s.shape[1:], s.dtype) for s in sums),
        sems=(pltpu.SemaphoreType.DMA((n,)), pltpu.SemaphoreType.DMA((n,))),
        start=start, finish=finish)


def _gather_spread_exchange(shards):
    n = len(shards)

    def peers():
        x, y, c = _position()
        return (x, y, c), [(x, y, 1 - c), (1 - x, y, c), (x, 1 - y, c), (1 - x, 1 - y, c)]

    def block(ref, px, py, pc):
        return ref.at[4 * px + 2 * py + pc]

    def copy(ins, outs, sems, a, k, owner, to):
        send_sems, recv_sems, _ = sems
        return pltpu.make_async_remote_copy(
            src_ref=ins[a], dst_ref=block(outs[a], *owner),
            send_sem=send_sems.at[4 * a + k], recv_sem=recv_sems.at[4 * a + k],
            device_id=to, device_id_type=MESH)

    def start(ins, outs, sems):
        me, others = peers()
        for a in range(n):
            pltpu.make_async_copy(ins[a], block(outs[a], *me), sems[2].at[a]).start()
            for k, to in enumerate(others):
                copy(ins, outs, sems, a, k, me, to).start()

    def finish(ins, outs, sems):
        me, others = peers()
        for a in range(n):
            for k, peer in enumerate(others):
                cp = copy(ins, outs, sems, a, k, peer, peer)
                cp.wait_recv()
                cp.wait_send()
            pltpu.make_async_copy(ins[a], block(outs[a], *me), sems[2].at[a]).wait()

    return _Exchange(
        ins=tuple(shards),
        out_shape=tuple(jax.ShapeDtypeStruct((N_DEV,) + s.shape, s.dtype) for s in shards),
        sems=(pltpu.SemaphoreType.DMA((4 * n,)), pltpu.SemaphoreType.DMA((4 * n,)),
              pltpu.SemaphoreType.DMA((n,))),
        start=start, finish=finish)


def _gather_forward(name, partial):
    n = len(partial)

    def body(*refs):
        ins, outs = refs[:n], refs[n:2 * n]
        send_sems, recv_sems = refs[2 * n:]
        x, y, c = _position()
        chips = [(1 - x, y), (x, 1 - y), (1 - x, 1 - y)]

        def copy(a, j, pc):
            blk = 4 * chips[j][0] + 2 * chips[j][1] + pc
            return pltpu.make_async_remote_copy(
                src_ref=ins[a].at[blk], dst_ref=outs[a].at[blk],
                send_sem=send_sems.at[3 * a + j], recv_sem=recv_sems.at[3 * a + j],
                device_id=(x, y, 1 - c), device_id_type=MESH)

        for a in range(n):
            for j in range(3):
                copy(a, j, c).start()
        for a in range(n):
            for j in range(3):
                copy(a, j, 1 - c).wait_recv()
                copy(a, j, c).wait_send()

    return pl.pallas_call(
        body, name=name, in_specs=[_HBM] * n, out_specs=[_HBM] * n,
        out_shape=[jax.ShapeDtypeStruct(p.shape, p.dtype) for p in partial],
        input_output_aliases={a: a for a in range(n)},
        scratch_shapes=[pltpu.SemaphoreType.DMA((3 * n,)), pltpu.SemaphoreType.DMA((3 * n,))],
    )(*partial)


def _all_reduce_small(vec):
    r = vec.shape[0]

    def body(v_ref, o_ref, gbuf, send_sems, recv_sems):
        x, y, c = _position()
        me = 4 * x + 2 * y + c
        gbuf[me] = v_ref[...]
        copies = []
        for k in range(1, N_DEV):
            ox, oy, oc = (k >> 2) & 1, (k >> 1) & 1, k & 1
            peer = (1 - x if ox else x, 1 - y if oy else y, 1 - c if oc else c)
            cp = pltpu.make_async_remote_copy(
                src_ref=gbuf.at[me], dst_ref=gbuf.at[me], send_sem=send_sems.at[k - 1],
                recv_sem=recv_sems.at[k - 1], device_id=peer, device_id_type=MESH)
            cp.start()
            copies.append(cp)
        for cp in copies:
            cp.wait()
        total = gbuf[0]
        for j in range(1, N_DEV):
            total = total + gbuf[j]
        o_ref[...] = total

    return pl.pallas_call(
        body, name="all_reduce_small",
        in_specs=[pl.BlockSpec(memory_space=pltpu.VMEM)],
        out_specs=pl.BlockSpec(memory_space=pltpu.VMEM),
        out_shape=jax.ShapeDtypeStruct(vec.shape, F32),
        scratch_shapes=[pltpu.VMEM((N_DEV, r, LANE), F32), pltpu.SemaphoreType.DMA((N_DEV - 1,)),
                        pltpu.SemaphoreType.DMA((N_DEV - 1,))],
    )(vec)


def _cols_from_blocks(g):
    nb, r, cs = g.shape
    return jnp.transpose(g, (1, 0, 2)).reshape(r, nb * cs)


def _cols_to_blocks(w):
    r, cfull = w.shape
    return jnp.transpose(w.reshape(r, N_DEV, cfull // N_DEV), (1, 0, 2))


def kernel(x, mem, ffn1_norm, ffn1_w_gate, ffn1_w_up, ffn1_w_down, mix_norm, w_in, pool_w, pool_scale, gla_w_a2, gla_b_a, gla_head_norm, w_out, xattn_norm, mem_norm, xattn_w_q, xattn_w_kv, xattn_w_o, ffn2_norm, ffn2_w_gate, ffn2_w_up, ffn2_w_down, final_norm, loss_target, m_ffn1_norm, m_ffn1_w_gate, m_ffn1_w_up, m_ffn1_w_down, m_mix_norm, m_w_in, m_pool_w, m_pool_scale, m_gla_w_a2, m_gla_b_a, m_gla_head_norm, m_w_out, m_xattn_norm, m_mem_norm, m_xattn_w_q, m_xattn_w_kv, m_xattn_w_o, m_ffn2_norm, m_ffn2_w_gate, m_ffn2_w_up, m_ffn2_w_down, m_final_norm, v_ffn1_norm, v_ffn1_w_gate, v_ffn1_w_up, v_ffn1_w_down, v_mix_norm, v_w_in, v_pool_w, v_pool_scale, v_gla_w_a2, v_gla_b_a, v_gla_head_norm, v_w_out, v_xattn_norm, v_mem_norm, v_xattn_w_q, v_xattn_w_kv, v_xattn_w_o, v_ffn2_norm, v_ffn2_w_gate, v_ffn2_w_up, v_ffn2_w_down, v_final_norm):
    weights = dict(ffn1_norm=ffn1_norm, ffn1_w_gate=ffn1_w_gate, ffn1_w_up=ffn1_w_up, ffn1_w_down=ffn1_w_down, mix_norm=mix_norm, w_in=w_in, pool_w=pool_w, pool_scale=pool_scale, gla_w_a2=gla_w_a2, gla_b_a=gla_b_a, gla_head_norm=gla_head_norm, w_out=w_out, xattn_norm=xattn_norm, mem_norm=mem_norm, xattn_w_q=xattn_w_q, xattn_w_kv=xattn_w_kv, xattn_w_o=xattn_w_o, ffn2_norm=ffn2_norm, ffn2_w_gate=ffn2_w_gate, ffn2_w_up=ffn2_w_up, ffn2_w_down=ffn2_w_down, final_norm=final_norm)
    mom1 = dict(ffn1_norm=m_ffn1_norm, ffn1_w_gate=m_ffn1_w_gate, ffn1_w_up=m_ffn1_w_up, ffn1_w_down=m_ffn1_w_down, mix_norm=m_mix_norm, w_in=m_w_in, pool_w=m_pool_w, pool_scale=m_pool_scale, gla_w_a2=m_gla_w_a2, gla_b_a=m_gla_b_a, gla_head_norm=m_gla_head_norm, w_out=m_w_out, xattn_norm=m_xattn_norm, mem_norm=m_mem_norm, xattn_w_q=m_xattn_w_q, xattn_w_kv=m_xattn_w_kv, xattn_w_o=m_xattn_w_o, ffn2_norm=m_ffn2_norm, ffn2_w_gate=m_ffn2_w_gate, ffn2_w_up=m_ffn2_w_up, ffn2_w_down=m_ffn2_w_down, final_norm=m_final_norm)
    mom2 = dict(ffn1_norm=v_ffn1_norm, ffn1_w_gate=v_ffn1_w_gate, ffn1_w_up=v_ffn1_w_up, ffn1_w_down=v_ffn1_w_down, mix_norm=v_mix_norm, w_in=v_w_in, pool_w=v_pool_w, pool_scale=v_pool_scale, gla_w_a2=v_gla_w_a2, gla_b_a=v_gla_b_a, gla_head_norm=v_gla_head_norm, w_out=v_w_out, xattn_norm=v_xattn_norm, mem_norm=v_mem_norm, xattn_w_q=v_xattn_w_q, xattn_w_kv=v_xattn_w_kv, xattn_w_o=v_xattn_w_o, ffn2_norm=v_ffn2_norm, ffn2_w_gate=v_ffn2_w_gate, ffn2_w_up=v_ffn2_w_up, ffn2_w_down=v_ffn2_w_down, final_norm=v_final_norm)
    order = list(weights.keys())

    n, d = x.shape[1], x.shape[2]
    mlen = mem.shape[1]
    x0 = x.reshape(n, d)
    memf = mem.reshape(mlen, d)
    target = loss_target.reshape(n, d)
    dpool = d // 2
    in_cols = w_in.shape[2] * N_DEV
    proj_cols = 2 * d + LANE
    rank = gla_w_a2.shape[1]

    def shard(name):
        return weights[name][0].astype(BF16)

    wg1, wu1, wd1 = _all_gather("ag_ffn1", [shard("ffn1_w_gate"), shard("ffn1_w_up"), shard("ffn1_w_down")])
    later = ["w_in", "pool_w", "gla_w_a2", "w_out", "xattn_w_q", "xattn_w_kv", "xattn_w_o",
             "ffn2_w_gate", "ffn2_w_up", "ffn2_w_down"]
    h1 = _rms_fwd("rms_ffn1", x0, ffn1_norm)
    (x1, a1, b1, h2), spread = _ffn_fwd(
        "ffn1_fwd", x0, h1, wg1, wu1, wd1, mix_norm, _gather_spread_exchange([shard(k) for k in later]))
    win_g, pw_g, wa2_g, wout_g, wq_g, wkv_g, wo_g, wg2, wu2, wd2 = _gather_forward("ag_forward", spread)

    winp = jnp.pad(_cols_from_blocks(win_g), ((0, 0), (0, proj_cols - in_cols)))
    pw = jnp.transpose(pw_g, (1, 0, 2, 3)).reshape(len(POOL_WINDOWS), dpool // 4, dpool // 4)
    wa2p = jnp.pad(_cols_from_blocks(wa2_g), ((0, LANE - rank), (0, 0)))
    wout = wout_g.reshape(d, d)
    wq = wq_g.reshape(d, d)
    wkv = _cols_from_blocks(wkv_g)
    wo = wo_g.reshape(d, d)
    fnorm = final_norm.reshape(1, d)

    proj = _mm_nn("mix_in", [(h2, winp)], F32, tn=1408)
    ypool = _pool_fwd(proj, pw, pool_scale)
    ygla, states = _gla_fwd(proj, wa2p, gla_b_a, gla_head_norm)
    x2, h3 = _mm_nn("mix_out", [(ypool, wout[:dpool]), (ygla, wout[dpool:])], F32, res=x1,
                    norm_gain=xattn_norm, tk=1024)
    mh = _rms_fwd("rms_mem", memf, mem_norm)
    q = _mm_nn("xattn_q", [(h3, wq)], BF16)
    kv = _mm_nn("xattn_kv", [(mh, wkv)], BF16)
    att = _attn_fwd(q, kv)
    x3, h4 = _mm_nn("xattn_o", [(att, wo)], F32, res=x2, norm_gain=ffn2_norm)
    (x4, a2, b2), _ = _ffn_fwd("ffn2_fwd", x3, h4, wg2, wu2, wd2, None, _NO_EXCHANGE)
    dx4, dx4b, g_final, loss_part = _loss_head(x4, target, fnorm)

    grads = {}
    core = lax.axis_index("c").astype(jnp.int32).reshape(1)
    chip = (2 * lax.axis_index("x") + lax.axis_index("y")).astype(jnp.int32).reshape(1)
    chip_sums = {}

    def pair_blocks(names):
        return [grads[k].reshape(N_DEV, -1, grads[k].shape[-1]) for k in names]

    def pair_sums(names, blocks, recvd):
        for k, g, r in zip(names, blocks, recvd):
            chip_sums[k] = _pair_add("rs_add_" + k, g, r, core)

    ffn2_names = ["ffn2_w_gate", "ffn2_w_up", "ffn2_w_down"]
    (da2, db2, act2), _ = _ffn_bwd_act("ffn2_bwd_act", dx4b, wd2, a2, b2, _NO_EXCHANGE)
    (grads["ffn2_w_gate"], grads["ffn2_w_up"], grads["ffn2_w_down"]), _ = _ffn_bwd_weights(
        "ffn2_bwd_weights", h4, dx4b, da2, db2, act2, _NO_EXCHANGE)
    blocks = pair_blocks(ffn2_names)
    (dx3, dx3b, g_ffn2_norm), recvd = _ffn_bwd_dh(
        "ffn2_bwd_dh", da2, db2, wg2, wu2, x3, ffn2_norm, dx4, _pair_exchange(blocks))
    pair_sums(ffn2_names, blocks, recvd)

    xattn_names = ["xattn_w_q", "xattn_w_kv", "xattn_w_o"]
    datt = _mm_nt("xattn_do", [(dx3b, wo)], BF16)
    grads["xattn_w_o"] = _mm_tn("xattn_dwo", [(att, dx3b)], BF16).reshape(N_DEV, d // N_DEV, d)
    dq, dk, dv = _attn_bwd(q, kv, datt)
    grads["xattn_w_q"] = _mm_tn("xattn_dwq", [(h3, dq)], BF16).reshape(N_DEV, d // N_DEV, d)
    dkv = jnp.concatenate([dk, dv], axis=1)
    dmh = _mm_nt("xattn_dmh", [(dkv, wkv)], F32)
    grads["xattn_w_kv"] = _cols_to_blocks(_mm_tn("xattn_dwkv", [(mh, dkv)], BF16))
    g_mem_norm = _rms_gain_grad("rms_mem_bwd", memf, mem_norm, dmh)
    blocks = pair_blocks(xattn_names)
    (dx2, dx2b, g_xattn_norm), recvd = _mm_nt_rms_bwd(
        "xattn_dh", dq, wq, x2, xattn_norm, dx3, _pair_exchange(blocks), tk=1024)
    pair_sums(xattn_names, blocks, recvd)

    dymix = _mm_nt("mix_dy", [(dx2b, wout)], F32)
    grads["w_out"] = jnp.concatenate(
        [_mm_tn("mix_dwout_pool", [(ypool, dx2b)], BF16), _mm_tn("mix_dwout_gla", [(ygla, dx2b)], BF16)],
        axis=0).reshape(N_DEV, d // N_DEV, d)
    du, g_pool_w, g_pool_scale = _pool_bwd(proj, dymix, pw, pool_scale)
    dproj, g_wa2p, g_b_a, g_head_norm = _gla_bwd(proj, dymix, du, states, wa2p, gla_b_a, gla_head_norm)
    grads["w_in"] = _cols_to_blocks(_mm_tn("mix_dwin", [(h2, dproj)], BF16, tn=1408)[:, :in_cols])
    grads["pool_w"] = jnp.transpose(
        g_pool_w.reshape(len(POOL_WINDOWS), N_DEV, dpool // 4 // N_DEV, dpool // 4), (1, 0, 2, 3))
    grads["gla_w_a2"] = _cols_to_blocks(g_wa2p[:rank])
    mix_names = ["w_in", "pool_w", "gla_w_a2", "w_out"]
    blocks = pair_blocks(mix_names)
    (dx1, dx1b, g_mix_norm), recvd = _mm_nt_rms_bwd(
        "mix_dh", dproj, winp, x1, mix_norm, dx2, _pair_exchange(blocks), tk=1408)
    pair_sums(mix_names, blocks, recvd)

    ffn1_names = ["ffn1_w_gate", "ffn1_w_up", "ffn1_w_down"]
    (da1, db1, act1), recvd = _ffn_bwd_act(
        "ffn1_bwd_act", dx1b, wd1, a1, b1, _chip_scatter_exchange([chip_sums[k] for k in ffn2_names]))
    chip_recvd = dict(zip(ffn2_names, recvd))
    (grads["ffn1_w_gate"], grads["ffn1_w_up"], grads["ffn1_w_down"]), recvd = _ffn_bwd_weights(
        "ffn1_bwd_weights", h1, dx1b, da1, db1, act1,
        _chip_scatter_exchange([chip_sums[k] for k in xattn_names + mix_names]))
    chip_recvd.update(zip(xattn_names + mix_names, recvd))
    blocks = pair_blocks(ffn1_names)
    pair_sums(ffn1_names, blocks, _run_exchange("rs_pair_ffn1", _pair_exchange(blocks)))
    (dx0, _, g_ffn1_norm), recvd = _ffn_bwd_dh(
        "ffn1_bwd_dh", da1, db1, wg1, wu1, x0, ffn1_norm, dx1,
        _chip_scatter_exchange([chip_sums[k] for k in ffn1_names]))
    chip_recvd.update(zip(ffn1_names, recvd))

    small = [("ffn1_norm", g_ffn1_norm), ("mix_norm", g_mix_norm), ("pool_scale", g_pool_scale),
             ("gla_b_a", g_b_a), ("gla_head_norm", g_head_norm), ("xattn_norm", g_xattn_norm),
             ("mem_norm", g_mem_norm), ("ffn2_norm", g_ffn2_norm), ("final_norm", g_final)]
    packed = jnp.concatenate([g.reshape(-1) for _, g in small] + [loss_part.reshape(-1)])
    slab = 8 * LANE
    padded = -(-packed.shape[0] // slab) * slab
    packed = jnp.pad(packed, (0, padded - packed.shape[0])).reshape(padded // LANE, LANE)
    reduced = _all_reduce_small(packed).reshape(-1)
    small_grads = {}
    off = 0
    for name, g in small:
        small_grads[name] = reduced[off:off + g.size]
        off += g.size
    loss = reduced[off]

    out_g, out_d, out_m, out_v = {}, {}, {}, {}
    for k in order:
        w = weights[k]
        if k in small_grads:
            w2 = w.reshape(1, -1)
            parts = [small_grads[k].reshape(1, -1)]
            own_block = jnp.zeros((1,), jnp.int32)
        else:
            w2 = w.reshape(-1, w.shape[-1])
            parts = [chip_sums[k], chip_recvd[k]]
            own_block = chip
        res = _adamw("adamw_" + k, w2, mom1[k].reshape(w2.shape), mom2[k].reshape(w2.shape), parts, own_block)
        out_g[k], out_d[k], out_m[k], out_v[k] = [r.reshape(w.shape) for r in res]

    return (loss, dx0.reshape(x.shape), *[out_g[k] for k in order], *[out_d[k] for k in order],
            *[out_m[k] for k in order], *[out_v[k] for k in order])
```

```python
import collections

import jax
import jax.numpy as jnp
from jax import lax
from jax.experimental import pallas as pl
from jax.experimental.pallas import tpu as pltpu

F32 = jnp.float32
BF16 = jnp.bfloat16
MESH = pl.DeviceIdType.MESH

N_DEV = 8
CHUNK = 64
POOL_WINDOWS = (2, 4, 8, 16)
POOL_HALO = 16
HEADS = 4
GATE_TEMP = 16.0
RMS_EPS = 1e-6
LANE = 128
V7X_VMEM_BYTES = 64 * 1024 * 1024
VMEM_LIMIT = V7X_VMEM_BYTES - 8 * 1024 * 1024
GLA_ROWS = 4 * CHUNK
ELEMENTWISE_BLOCK = 512 * 1024
FFN_SUBTILES = 2

ADAM_LR = 0.001
ADAM_B1 = 0.9
ADAM_B2 = 0.999
ADAM_EPS = 1e-08
ADAM_WD = 0.01
ADAM_STEP = 10

_NN = (((1,), (0,)), ((), ()))
_NT = (((1,), (1,)), ((), ()))
_TN = (((0,), (0,)), ((), ()))


def _cparams(sem=None):
    return pltpu.CompilerParams(dimension_semantics=sem, vmem_limit_bytes=VMEM_LIMIT)


def _tile(n, pref, align):
    t = (min(pref, n) // align) * align
    while t >= align:
        if n % t == 0:
            return t
        t -= align
    return n


def _dot(a, b, dims=_NN):
    return lax.dot_general(a.astype(BF16), b.astype(BF16), dims, preferred_element_type=F32)


def _silu_parts(z):
    sig = jax.nn.sigmoid(z)
    return z * sig, sig * (1.0 + z * (1.0 - sig))


def _rms_scale(xf):
    return lax.rsqrt(jnp.mean(xf * xf, axis=-1, keepdims=True) + RMS_EPS)


def _matmul(name, pairs, a_spec, b_spec, out_shape, out_spec, grid, acc_shape, dims,
            res=None, res_spec=None, scale=None, norm_gain=None):
    n = len(pairs)
    nk = grid[2]

    def body(*refs):
        a_refs, b_refs = refs[:n], refs[n:2 * n]
        pos = 2 * n
        res_ref = gain_ref = h_ref = None
        if res is not None:
            res_ref = refs[pos]
            pos += 1
        if norm_gain is not None:
            gain_ref = refs[pos]
            pos += 1
        o_ref = refs[pos]
        pos += 1
        if norm_gain is not None:
            h_ref = refs[pos]
            pos += 1
        acc = refs[pos]
        k = pl.program_id(2)

        @pl.when(k == 0)
        def _():
            acc[...] = jnp.zeros_like(acc)

        part = None
        for a_ref, b_ref in zip(a_refs, b_refs):
            d = _dot(a_ref[...], b_ref[...], dims)
            part = d if part is None else part + d
        acc[...] += part

        @pl.when(k == nk - 1)
        def _():
            r = acc[...]
            if scale is not None:
                r = r * scale
            if res_ref is not None:
                r = r + res_ref[...]
            o_ref[...] = r.astype(o_ref.dtype)
            if h_ref is not None:
                h_ref[...] = ((r * _rms_scale(r)) * gain_ref[...]).astype(BF16)

    ops = [p[0] for p in pairs] + [p[1] for p in pairs]
    specs = [a_spec] * n + [b_spec] * n
    if res is not None:
        ops.append(res)
        specs.append(res_spec)
    if norm_gain is not None:
        ops.append(norm_gain)
        specs.append(pl.BlockSpec(norm_gain.shape, lambda i, j, k: (0, 0)))
        out_spec = [out_spec, out_spec]
        out_shape = [out_shape, jax.ShapeDtypeStruct(out_shape.shape, BF16)]
    return pl.pallas_call(
        body, name=name, grid=grid, in_specs=specs, out_specs=out_spec, out_shape=out_shape,
        scratch_shapes=[pltpu.VMEM(acc_shape, F32)],
        compiler_params=_cparams(("parallel", "parallel", "arbitrary")),
    )(*ops)


def _mm_nn(name, pairs, out_dtype, res=None, norm_gain=None, tm=1024, tn=1024, tk=2048):
    m, kd = pairs[0][0].shape
    nd = pairs[0][1].shape[1]
    if norm_gain is not None:
        tm, tn = 512, nd
    tm, tn, tk = _tile(m, tm, 16), _tile(nd, tn, LANE), _tile(kd, tk, LANE)
    return _matmul(
        name, pairs,
        pl.BlockSpec((tm, tk), lambda i, j, k: (i, k)),
        pl.BlockSpec((tk, tn), lambda i, j, k: (k, j)),
        jax.ShapeDtypeStruct((m, nd), out_dtype),
        pl.BlockSpec((tm, tn), lambda i, j, k: (i, j)),
        (m // tm, nd // tn, kd // tk), (tm, tn), _NN,
        res=res, res_spec=pl.BlockSpec((tm, tn), lambda i, j, k: (i, j)), norm_gain=norm_gain)


def _mm_nt(name, pairs, out_dtype, tm=1024, tn=1024, tk=2048):
    m, kd = pairs[0][0].shape
    nd = pairs[0][1].shape[0]
    tm, tn, tk = _tile(m, tm, 16), _tile(nd, tn, LANE), _tile(kd, tk, LANE)
    return _matmul(
        name, pairs,
        pl.BlockSpec((tm, tk), lambda i, j, k: (i, k)),
        pl.BlockSpec((tn, tk), lambda i, j, k: (j, k)),
        jax.ShapeDtypeStruct((m, nd), out_dtype),
        pl.BlockSpec((tm, tn), lambda i, j, k: (i, j)),
        (m // tm, nd // tn, kd // tk), (tm, tn), _NT)


def _mm_tn(name, pairs, out_dtype, tm=1024, tn=2048, tk=512):
    kd, m = pairs[0][0].shape
    nd = pairs[0][1].shape[1]
    tm, tn, tk = _tile(m, tm, LANE), _tile(nd, tn, LANE), _tile(kd, tk, 16)
    return _matmul(
        name, pairs,
        pl.BlockSpec((tk, tm), lambda i, j, k: (k, i)),
        pl.BlockSpec((tk, tn), lambda i, j, k: (k, j)),
        jax.ShapeDtypeStruct((m, nd), out_dtype),
        pl.BlockSpec((tm, tn), lambda i, j, k: (i, j)),
        (m // tm, nd // tn, kd // tk), (tm, tn), _TN)


def _rms_fwd(name, x, gain):
    m, d = x.shape
    tm = _tile(m, 512, 16)

    def body(x_ref, g_ref, o_ref):
        xf = x_ref[...]
        r = lax.rsqrt(jnp.mean(xf * xf, axis=-1, keepdims=True) + RMS_EPS)
        o_ref[...] = ((xf * r) * g_ref[...]).astype(o_ref.dtype)

    return pl.pallas_call(
        body, name=name, grid=(m // tm,),
        in_specs=[pl.BlockSpec((tm, d), lambda i: (i, 0)), pl.BlockSpec((1, d), lambda i: (0, 0))],
        out_specs=pl.BlockSpec((tm, d), lambda i: (i, 0)),
        out_shape=jax.ShapeDtypeStruct((m, d), BF16),
        compiler_params=_cparams(("parallel",)),
    )(x, gain)


def _rms_bwd_rows(xf, gain, dhf):
    r = _rms_scale(xf)
    xh = xf * r
    t = dhf * gain
    dx = r * (t - xh * jnp.mean(t * xh, axis=-1, keepdims=True))
    return dx, jnp.sum(dhf * xh, axis=0, keepdims=True)


def _rms_gain_grad(name, x, gain, dh):
    m, d = x.shape
    tm = _tile(m, 512, 16)

    def body(x_ref, g_ref, dh_ref, dg_ref):
        @pl.when(pl.program_id(0) == 0)
        def _():
            dg_ref[...] = jnp.zeros_like(dg_ref)

        dg_ref[...] += _rms_bwd_rows(x_ref[...], g_ref[...], dh_ref[...])[1]

    row = pl.BlockSpec((tm, d), lambda i: (i, 0))
    vec = pl.BlockSpec((1, d), lambda i: (0, 0))
    return pl.pallas_call(
        body, name=name, grid=(m // tm,), in_specs=[row, vec, row], out_specs=vec,
        out_shape=jax.ShapeDtypeStruct((1, d), F32),
        compiler_params=_cparams(("arbitrary",)),
    )(x, gain, dh)


def _loss_head(x, target, gain):
    m, d = x.shape
    tm = _tile(m, 512, 16)

    def body(x_ref, t_ref, g_ref, dx_ref, dxb_ref, dg_ref, loss_ref):
        i = pl.program_id(0)

        @pl.when(i == 0)
        def _():
            dg_ref[...] = jnp.zeros_like(dg_ref)
            loss_ref[...] = jnp.zeros_like(loss_ref)

        xf = x_ref[...]
        r = lax.rsqrt(jnp.mean(xf * xf, axis=-1, keepdims=True) + RMS_EPS)
        xh = xf * r
        g = g_ref[...]
        err = xh * g - t_ref[...]
        loss_ref[...] += jnp.full(loss_ref.shape, (0.5 / d) * jnp.sum(err * err), F32)
        dy = err * (1.0 / d)
        t = dy * g
        dx = r * (t - xh * jnp.mean(t * xh, axis=-1, keepdims=True))
        dx_ref[...] = dx
        dxb_ref[...] = dx.astype(BF16)
        dg_ref[...] += jnp.sum(dy * xh, axis=0, keepdims=True)

    row = pl.BlockSpec((tm, d), lambda i: (i, 0))
    vec = pl.BlockSpec((1, d), lambda i: (0, 0))
    return pl.pallas_call(
        body, name="loss_head", grid=(m // tm,),
        in_specs=[row, row, vec],
        out_specs=[row, row, vec, pl.BlockSpec((1, LANE), lambda i: (0, 0))],
        out_shape=[jax.ShapeDtypeStruct((m, d), F32), jax.ShapeDtypeStruct((m, d), BF16),
                   jax.ShapeDtypeStruct((1, d), F32), jax.ShapeDtypeStruct((1, LANE), F32)],
        compiler_params=_cparams(("arbitrary",)),
    )(x, target, gain)


def _hosted_call(name, body, grid, in_specs, out_specs, out_shape, scratch, operands, exchange):
    n_in, n_out, n_scr = len(in_specs), len(out_specs), len(scratch)
    n_xin, n_xout = len(exchange.ins), len(exchange.out_shape)

    def full_body(*refs):
        pos = 0
        parts = []
        for cnt in (n_in, n_xin, n_out, n_xout, n_scr):
            parts.append(refs[pos:pos + cnt])
            pos += cnt
        ins, x_ins, outs, x_outs, scr = parts
        sems = refs[pos:]
        first = pl.program_id(0) == 0
        last = pl.program_id(0) == grid[0] - 1
        for ax in range(1, len(grid)):
            first = jnp.logical_and(first, pl.program_id(ax) == 0)
            last = jnp.logical_and(last, pl.program_id(ax) == grid[ax] - 1)

        @pl.when(first)
        def _():
            exchange.start(x_ins, x_outs, sems)

        body(ins, outs, scr)

        @pl.when(last)
        def _():
            exchange.finish(x_ins, x_outs, sems)

    res = pl.pallas_call(
        full_body, name=name, grid=grid,
        in_specs=list(in_specs) + [_HBM] * n_xin, out_specs=list(out_specs) + [_HBM] * n_xout,
        out_shape=list(out_shape) + list(exchange.out_shape),
        scratch_shapes=list(scratch) + list(exchange.sems),
        compiler_params=_cparams(("arbitrary",) * len(grid)),
    )(*operands, *exchange.ins)
    return res[:n_out], res[n_out:]


def _ffn_fwd(name, x, h, wg, wu, wd, next_gain, exchange):
    n, d = x.shape
    nb, _, fb = wg.shape
    tm = _tile(n, 512, 16 * FFN_SUBTILES)
    ts = tm // FFN_SUBTILES
    with_norm = next_gain is not None

    def body(ins, outs, scr):
        x_ref, h_ref, wg_ref, wu_ref, wd_ref = ins[:5]
        xo_ref, a_ref, b_ref = outs[:3]
        acc, = scr
        j = pl.program_id(1)

        @pl.when(j == 0)
        def _():
            acc[...] = jnp.zeros_like(acc)

        for s in range(FFN_SUBTILES):
            rows = slice(s * ts, (s + 1) * ts)
            hh = h_ref[rows, :]
            a = _dot(hh, wg_ref[...])
            b = _dot(hh, wu_ref[...])
            a_ref[rows, :] = a.astype(BF16)
            b_ref[rows, :] = b.astype(BF16)
            act = (a * jax.nn.sigmoid(a)) * b
            acc[rows, :] += _dot(act, wd_ref[...])

        @pl.when(j == nb - 1)
        def _():
            xo = x_ref[...] + 0.5 * acc[...]
            xo_ref[...] = xo
            if with_norm:
                outs[3][...] = ((xo * _rms_scale(xo)) * ins[5][...]).astype(BF16)

    row = pl.BlockSpec((tm, d), lambda i, j: (i, 0))
    w_in = pl.BlockSpec((None, d, fb), lambda i, j: (j, 0, 0))
    hid = pl.BlockSpec((None, tm, fb), lambda i, j: (j, i, 0))
    return _hosted_call(
        name, body, (n // tm, nb),
        [row, row, w_in, w_in, pl.BlockSpec((None, fb, d), lambda i, j: (j, 0, 0))]
        + ([pl.BlockSpec((1, d), lambda i, j: (0, 0))] if with_norm else []),
        [row, hid, hid] + ([row] if with_norm else []),
        [jax.ShapeDtypeStruct((n, d), F32), jax.ShapeDtypeStruct((nb, n, fb), BF16),
         jax.ShapeDtypeStruct((nb, n, fb), BF16)]
        + ([jax.ShapeDtypeStruct((n, d), BF16)] if with_norm else []),
        [pltpu.VMEM((tm, d), F32)], (x, h, wg, wu, wd) + ((next_gain,) if with_norm else ()), exchange)


def _ffn_bwd_act(name, dxb, wd, a, b, exchange):
    n, d = dxb.shape
    nb, fb, _ = wd.shape
    tm = _tile(n, 512, 16 * FFN_SUBTILES)
    ts = tm // FFN_SUBTILES

    def body(ins, outs, scr):
        del scr
        dx_ref, wd_ref, a_ref, b_ref = ins
        da_ref, db_ref, act_ref = outs
        for s in range(FFN_SUBTILES):
            rows = slice(s * ts, (s + 1) * ts)
            dact = 0.5 * _dot(dx_ref[rows, :], wd_ref[...], _NT)
            af = a_ref[rows, :].astype(F32)
            bf = b_ref[rows, :].astype(F32)
            sl, dsl = _silu_parts(af)
            act_ref[rows, :] = (sl * bf).astype(BF16)
            db_ref[rows, :] = (dact * sl).astype(BF16)
            da_ref[rows, :] = (dact * bf * dsl).astype(BF16)

    hid = pl.BlockSpec((None, tm, fb), lambda i, j: (j, i, 0))
    shp = jax.ShapeDtypeStruct((nb, n, fb), BF16)
    return _hosted_call(
        name, body, (n // tm, nb),
        [pl.BlockSpec((tm, d), lambda i, j: (i, 0)), pl.BlockSpec((None, fb, d), lambda i, j: (j, 0, 0)), hid, hid],
        [hid, hid, hid], [shp, shp, shp], [], (dxb, wd, a, b), exchange)


def _proj_rms_bwd(name, pairs, a_spec, b_spec, nk, tm, x, gain, dres, exchange):
    n, d = x.shape
    n_pairs = len(pairs)
    slab = _tile(tm, 128, 16)

    def body(ins, outs, scr):
        a_refs, b_refs = ins[:n_pairs], ins[n_pairs:2 * n_pairs]
        x_ref, g_ref, res_ref = ins[2 * n_pairs:]
        dx_ref, dxb_ref, dg_ref = outs
        del scr
        i, k = pl.program_id(0), pl.program_id(1)

        @pl.when(jnp.logical_and(i == 0, k == 0))
        def _():
            dg_ref[...] = jnp.zeros_like(dg_ref)

        @pl.when(k == 0)
        def _():
            dx_ref[...] = jnp.zeros_like(dx_ref)

        for a_ref, b_ref in zip(a_refs, b_refs):
            dx_ref[...] += _dot(a_ref[...], b_ref[...], _NT)

        @pl.when(k == nk - 1)
        def _():
            for s in range(tm // slab):
                rows = slice(s * slab, (s + 1) * slab)
                dx, dg = _rms_bwd_rows(x_ref[rows, :], g_ref[...], dx_ref[rows, :])
                dx = dx + res_ref[rows, :]
                dx_ref[rows, :] = dx
                dxb_ref[rows, :] = dx.astype(BF16)
                dg_ref[...] += dg

    row = pl.BlockSpec((tm, d), lambda i, k: (i, 0))
    vec = pl.BlockSpec((1, d), lambda i, k: (0, 0))
    return _hosted_call(
        name, body, (n // tm, nk), [a_spec] * n_pairs + [b_spec] * n_pairs + [row, vec, row],
        [row, row, vec],
        [jax.ShapeDtypeStruct((n, d), F32), jax.ShapeDtypeStruct((n, d), BF16),
         jax.ShapeDtypeStruct((1, d), F32)],
        [], tuple(p[0] for p in pairs) + tuple(p[1] for p in pairs) + (x, gain, dres), exchange)


def _ffn_bwd_dh(name, da, db, wg, wu, x, gain, dres, exchange):
    nb, n, fb = da.shape
    d = wg.shape[1]
    tm = _tile(n, 512, 16)
    return _proj_rms_bwd(
        name, [(da, wg), (db, wu)],
        pl.BlockSpec((None, tm, fb), lambda i, k: (k, i, 0)),
        pl.BlockSpec((None, d, fb), lambda i, k: (k, 0, 0)), nb, tm, x, gain, dres, exchange)


def _mm_nt_rms_bwd(name, a, w, x, gain, dres, exchange, tk=2048):
    n, kd = a.shape
    d = w.shape[0]
    tm, tk = _tile(n, 512, 16), _tile(kd, tk, LANE)
    return _proj_rms_bwd(
        name, [(a, w)],
        pl.BlockSpec((tm, tk), lambda i, k: (i, k)),
        pl.BlockSpec((d, tk), lambda i, k: (0, k)), kd // tk, tm, x, gain, dres, exchange)


def _ffn_bwd_weights(name, h, dxb, da, db, act, exchange):
    n, d = h.shape
    nb, _, fb = da.shape
    tk = _tile(n, 512, 16)
    nk = n // tk

    def body(ins, outs, scr):
        h_ref, dx_ref, da_ref, db_ref, act_ref = ins
        dwg_ref, dwu_ref, dwd_ref = outs
        accg, accu, accd = scr
        k = pl.program_id(1)

        @pl.when(k == 0)
        def _():
            accg[...] = jnp.zeros_like(accg)
            accu[...] = jnp.zeros_like(accu)
            accd[...] = jnp.zeros_like(accd)

        hh = h_ref[...]
        accg[...] += _dot(hh, da_ref[...], _TN)
        accu[...] += _dot(hh, db_ref[...], _TN)
        accd[...] += _dot(act_ref[...], dx_ref[...], _TN)

        @pl.when(k == nk - 1)
        def _():
            dwg_ref[...] = accg[...].astype(BF16)
            dwu_ref[...] = accu[...].astype(BF16)
            dwd_ref[...] = (0.5 * accd[...]).astype(BF16)

    row = pl.BlockSpec((tk, d), lambda j, k: (k, 0))
    hid = pl.BlockSpec((None, tk, fb), lambda j, k: (j, k, 0))
    w_in = pl.BlockSpec((None, d, fb), lambda j, k: (j, 0, 0))
    w_out = pl.BlockSpec((None, fb, d), lambda j, k: (j, 0, 0))
    return _hosted_call(
        name, body, (nb, nk), [row, row, hid, hid, hid], [w_in, w_in, w_out],
        [jax.ShapeDtypeStruct((nb, d, fb), BF16), jax.ShapeDtypeStruct((nb, d, fb), BF16),
         jax.ShapeDtypeStruct((nb, fb, d), BF16)],
        [pltpu.VMEM((d, fb), F32), pltpu.VMEM((d, fb), F32), pltpu.VMEM((fb, d), F32)],
        (h, dxb, da, db, act), exchange)


def _pool_diff(ext_ref, rows, cols, width, t_idx):
    s = ext_ref[POOL_HALO:POOL_HALO + rows, cols]
    for sft in range(1, width):
        s = s + ext_ref[POOL_HALO - sft:POOL_HALO - sft + rows, cols]
    cnt = jnp.minimum(t_idx + 1, width).astype(F32)
    return s / cnt - ext_ref[POOL_HALO:POOL_HALO + rows, cols]


def _pool_fwd(proj, pool_w, pool_scale):
    n = proj.shape[0]
    dp = pool_scale.shape[1]
    c = dp // len(POOL_WINDOWS)
    tm = _tile(n, 512, POOL_HALO)
    hb = tm // POOL_HALO

    def body(u_ref, halo_ref, pw_ref, sc_ref, y_ref, ext):
        i = pl.program_id(0)
        ext[0:POOL_HALO, :] = jnp.where(i > 0, halo_ref[...], 0.0)
        ext[POOL_HALO:, :] = u_ref[...]
        t_idx = i * tm + lax.broadcasted_iota(jnp.int32, (tm, 1), 0)
        for g, width in enumerate(POOL_WINDOWS):
            cols = slice(g * c, (g + 1) * c)
            dgrp = _pool_diff(ext, tm, cols, width, t_idx)
            y_ref[:, cols] = (_dot(dgrp, pw_ref[g]) * sc_ref[:, cols]).astype(BF16)

    return pl.pallas_call(
        body, name="pool_fwd", grid=(n // tm,),
        in_specs=[pl.BlockSpec((tm, dp), lambda i: (i, 0)),
                  pl.BlockSpec((POOL_HALO, dp), lambda i: (jnp.maximum(i * hb - 1, 0), 0)),
                  pl.BlockSpec((len(POOL_WINDOWS), c, c), lambda i: (0, 0, 0)),
                  pl.BlockSpec((1, dp), lambda i: (0, 0))],
        out_specs=pl.BlockSpec((tm, dp), lambda i: (i, 0)),
        out_shape=jax.ShapeDtypeStruct((n, dp), BF16),
        scratch_shapes=[pltpu.VMEM((tm + POOL_HALO, dp), F32)],
        compiler_params=_cparams(("parallel",)),
    )(proj, proj, pool_w, pool_scale)


def _pool_bwd(proj, dymix, pool_w, pool_scale):
    n = proj.shape[0]
    dp = pool_scale.shape[1]
    ng = len(POOL_WINDOWS)
    c = dp // ng
    tm = _tile(n, 512, POOL_HALO)
    hb = tm // POOL_HALO
    nsteps = n // tm
    last_halo = n // POOL_HALO - 1

    def body(u_ref, halo_ref, dy_ref, dyn_ref, pw_ref, sc_ref, du_ref, dpw_ref, dsc_ref, ext, dyext, e_s):
        i = pl.program_id(0)

        @pl.when(i == 0)
        def _():
            dpw_ref[...] = jnp.zeros_like(dpw_ref)
            dsc_ref[...] = jnp.zeros_like(dsc_ref)

        ext[0:POOL_HALO, :] = jnp.where(i > 0, halo_ref[...], 0.0)
        ext[POOL_HALO:, :] = u_ref[...]
        dyext[0:tm, :] = dy_ref[...]
        dyext[tm:, :] = jnp.where(i < nsteps - 1, dyn_ref[...], 0.0)
        t_idx = i * tm + lax.broadcasted_iota(jnp.int32, (tm, 1), 0)
        te_idx = i * tm + lax.broadcasted_iota(jnp.int32, (tm + POOL_HALO, 1), 0)
        for g, width in enumerate(POOL_WINDOWS):
            cols = slice(g * c, (g + 1) * c)
            dgrp = _pool_diff(ext, tm, cols, width, t_idx)
            w_g = pw_ref[g]
            dys = dyext[:, cols] * sc_ref[:, cols]
            ypre = _dot(dgrp, w_g)
            dsc_ref[:, cols] += jnp.sum(dyext[0:tm, cols] * ypre, axis=0, keepdims=True)
            dpw_ref[g] += _dot(dgrp, dys[0:tm], _TN)
            dd = _dot(dys, w_g, _NT)
            e_s[...] = dd / jnp.minimum(te_idx + 1, width).astype(F32)
            acc = e_s[0:tm, :]
            for sft in range(1, width):
                acc = acc + e_s[sft:sft + tm, :]
            du_ref[:, cols] = (acc - dd[0:tm]).astype(BF16)

    return pl.pallas_call(
        body, name="pool_bwd", grid=(nsteps,),
        in_specs=[pl.BlockSpec((tm, dp), lambda i: (i, 0)),
                  pl.BlockSpec((POOL_HALO, dp), lambda i: (jnp.maximum(i * hb - 1, 0), 0)),
                  pl.BlockSpec((tm, dp), lambda i: (i, 0)),
                  pl.BlockSpec((POOL_HALO, dp), lambda i: (jnp.minimum((i + 1) * hb, last_halo), 0)),
                  pl.BlockSpec((ng, c, c), lambda i: (0, 0, 0)),
                  pl.BlockSpec((1, dp), lambda i: (0, 0))],
        out_specs=[pl.BlockSpec((tm, dp), lambda i: (i, 0)),
                   pl.BlockSpec((ng, c, c), lambda i: (0, 0, 0)),
                   pl.BlockSpec((1, dp), lambda i: (0, 0))],
        out_shape=[jax.ShapeDtypeStruct((n, dp), BF16), jax.ShapeDtypeStruct((ng, c, c), F32),
                   jax.ShapeDtypeStruct((1, dp), F32)],
        scratch_shapes=[pltpu.VMEM((tm + POOL_HALO, dp), F32), pltpu.VMEM((tm + POOL_HALO, dp), F32),
                        pltpu.VMEM((tm + POOL_HALO, c), F32)],
        compiler_params=_cparams(("arbitrary",)),
    )(proj, proj, dymix, dymix, pool_w, pool_scale)


def _chunk_cumsum(x):
    row = lax.broadcasted_iota(jnp.int32, x.shape, 0) % CHUNK
    s = 1
    while s < CHUNK:
        x = x + jnp.where(row >= s, pltpu.roll(x, s, 0), 0.0)
        s *= 2
    return x


def _chunk_ends(cum_ref, bend_ref, nc):
    for c in range(nc):
        last = cum_ref[c * CHUNK + CHUNK - 1:(c + 1) * CHUNK, :]
        bend_ref[c * CHUNK:(c + 1) * CHUNK, :] = jnp.broadcast_to(last, (CHUNK, last.shape[1]))


def _gate_logits(alr_ref, wa_ref, ba_ref):
    z = _dot(alr_ref[...], wa_ref[...]) + ba_ref[...]
    la = (jnp.minimum(z, 0.0) - jnp.log(1.0 + jnp.exp(-jnp.abs(z)))) * (1.0 / GATE_TEMP)
    return z, la


def _gla_dims(proj, head_norm):
    n, pw = proj.shape
    dv4 = head_norm.shape[1]
    dk4 = dv4 // 2
    return n, pw, dv4, dk4, dk4 // HEADS, dv4 // HEADS


def _gla_in_specs(t, dk4, dv4, rev):
    alr_blk = (2 * dv4 + 2 * dv4) // LANE
    return [pl.BlockSpec((t, dk4), lambda i: (rev(i), 2)),
            pl.BlockSpec((t, dk4), lambda i: (rev(i), 3)),
            pl.BlockSpec((t, dv4), lambda i: (rev(i), 2)),
            pl.BlockSpec((t, dv4), lambda i: (rev(i), 3)),
            pl.BlockSpec((t, LANE), lambda i: (rev(i), alr_blk))]


def _gla_fwd(proj, wa2p, b_a, head_norm):
    n, _, dv4, dk4, dk, dv = _gla_dims(proj, head_norm)
    t = _tile(n, GLA_ROWS, CHUNK)
    nc = t // CHUNK
    qscale = dk ** -0.5

    def body(q_ref, k_ref, v_ref, g_ref, alr_ref, wa_ref, ba_ref, hn_ref, y_ref, st_ref,
             state, cum_s, bend_s, o_s):
        i = pl.program_id(0)

        @pl.when(i == 0)
        def _():
            state[...] = jnp.zeros_like(state)

        _, la = _gate_logits(alr_ref, wa_ref, ba_ref)
        cum_s[...] = _chunk_cumsum(la)
        _chunk_ends(cum_s, bend_s, nc)
        kd = (k_ref[...] * jnp.exp(bend_s[...] - cum_s[...])).astype(BF16)
        qs = (q_ref[...] * qscale).astype(BF16)
        vv = v_ref[...].astype(BF16)
        units = [(c, h) for c in range(nc) for h in range(HEADS)]
        for c, h in units:
            rows = slice(c * CHUNK, (c + 1) * CHUNK)
            st_ref[c, h] = _dot(vv[rows, h * dv:(h + 1) * dv], kd[rows, h * dk:(h + 1) * dk], _TN)
        for c, h in units:
            e = jnp.exp(bend_s[c * CHUNK:c * CHUNK + 1, h * dk:(h + 1) * dk])
            s_new = state[h] * e + st_ref[c, h]
            state[h] = s_new
            st_ref[c, h] = s_new
        for c, h in units:
            rows = slice(c * CHUNK, (c + 1) * CHUNK)
            o_s[rows, h * dv:(h + 1) * dv] = _dot(qs[rows, h * dk:(h + 1) * dk], st_ref[c, h], _NT)
        for h in range(HEADS):
            hv = slice(h * dv, (h + 1) * dv)
            o = o_s[:, hv]
            gg = g_ref[:, hv]
            y_ref[:, hv] = (((o * _rms_scale(o)) * hn_ref[:, hv]) * (gg * jax.nn.sigmoid(gg))).astype(BF16)

    full = lambda shape: pl.BlockSpec(shape, lambda i: tuple(0 for _ in shape))
    return pl.pallas_call(
        body, name="gla_fwd", grid=(n // t,),
        in_specs=_gla_in_specs(t, dk4, dv4, lambda i: i)
        + [full((LANE, dk4)), full((1, dk4)), full((1, dv4))],
        out_specs=[pl.BlockSpec((t, dv4), lambda i: (i, 0)),
                   pl.BlockSpec((nc, HEADS, dv, dk), lambda i: (i, 0, 0, 0))],
        out_shape=[jax.ShapeDtypeStruct((n, dv4), BF16),
                   jax.ShapeDtypeStruct((n // CHUNK, HEADS, dv, dk), F32)],
        scratch_shapes=[pltpu.VMEM((HEADS, dv, dk), F32), pltpu.VMEM((t, dk4), F32),
                        pltpu.VMEM((t, dk4), F32), pltpu.VMEM((t, dv4), F32)],
        compiler_params=_cparams(("arbitrary",)),
    )(proj, proj, proj, proj, proj, wa2p, b_a, head_norm)


def _gla_bwd(proj, dymix, du, states, wa2p, b_a, head_norm):
    n, pw, dv4, dk4, dk, dv = _gla_dims(proj, head_norm)
    t = _tile(n, GLA_ROWS, CHUNK)
    nc = t // CHUNK
    nsteps = n // t
    qscale = dk ** -0.5
    rev = lambda i: nsteps - 1 - i

    def body(q_ref, k_ref, v_ref, g_ref, alr_ref, dy_ref, du_ref, st_ref, prev_ref, wa_ref, ba_ref, hn_ref,
             dp_ref, dwa_ref, dba_ref, dhn_ref, carry, cum_s, bend_s, gst_s, dkd_s, dee_s, o_s, do_s):
        i = pl.program_id(0)

        @pl.when(i == 0)
        def _():
            carry[...] = jnp.zeros_like(carry)
            dwa_ref[...] = jnp.zeros_like(dwa_ref)
            dba_ref[...] = jnp.zeros_like(dba_ref)
            dhn_ref[...] = jnp.zeros_like(dhn_ref)

        first_step = i == nsteps - 1
        z, la = _gate_logits(alr_ref, wa_ref, ba_ref)
        cum_s[...] = _chunk_cumsum(la)
        _chunk_ends(cum_s, bend_s, nc)
        dec = jnp.exp(bend_s[...] - cum_s[...])
        kd_f = k_ref[...] * dec
        kd = kd_f.astype(BF16)
        qs = (q_ref[...] * qscale).astype(BF16)
        vv = v_ref[...].astype(BF16)
        dp_ref[:, 0:dv4] = du_ref[...]
        units = [(c, h) for c in range(nc) for h in range(HEADS)]

        for c, h in units:
            rows = slice(c * CHUNK, (c + 1) * CHUNK)
            o_s[rows, h * dv:(h + 1) * dv] = _dot(qs[rows, h * dk:(h + 1) * dk], st_ref[c, h], _NT)
        for h in range(HEADS):
            hv = slice(h * dv, (h + 1) * dv)
            o = o_s[:, hv]
            r = _rms_scale(o)
            oh = o * r
            sl, dsl = _silu_parts(g_ref[:, hv])
            dyh = dy_ref[:, hv]
            hn = hn_ref[:, hv]
            tt = dyh * sl
            dhn_ref[:, hv] += jnp.sum(tt * oh, axis=0, keepdims=True)
            dp_ref[:, 3 * dv4 + h * dv:3 * dv4 + (h + 1) * dv] = (dyh * (oh * hn) * dsl).astype(BF16)
            tt = tt * hn
            do_s[:, hv] = (r * (tt - oh * jnp.mean(tt * oh, axis=-1, keepdims=True))).astype(BF16)
        for c, h in units:
            rows = slice(c * CHUNK, (c + 1) * CHUNK)
            hk = slice(h * dk, (h + 1) * dk)
            do = do_s[rows, h * dv:(h + 1) * dv]
            gst_s[c, h] = _dot(do, qs[rows, hk], _TN)
            dp_ref[rows, dv4 + h * dk:dv4 + (h + 1) * dk] = (_dot(do, st_ref[c, h]) * qscale).astype(BF16)

        for c, h in reversed(units):
            g_n = carry[h] + gst_s[c, h]
            gst_s[c, h] = g_n
            carry[h] = g_n * jnp.exp(bend_s[c * CHUNK:c * CHUNK + 1, h * dk:(h + 1) * dk])

        for c, h in units:
            rows = slice(c * CHUNK, (c + 1) * CHUNK)
            hk = slice(h * dk, (h + 1) * dk)
            hv = slice(h * dv, (h + 1) * dv)
            g_n = gst_s[c, h]
            if c > 0:
                s_prev = st_ref[c - 1, h]
            else:
                s_prev = jnp.where(first_step, 0.0, prev_ref[0, h])
            dkd_s[rows, hk] = _dot(vv[rows, hv], g_n)
            dp_ref[rows, 2 * dv4 + h * dv:2 * dv4 + (h + 1) * dv] = _dot(kd[rows, hk], g_n, _NT).astype(BF16)
            dee = jnp.sum(g_n * s_prev, axis=0, keepdims=True) * jnp.exp(bend_s[c * CHUNK:c * CHUNK + 1, hk])
            dee_s[rows, hk] = jnp.broadcast_to(dee, (CHUNK, dk))

        dkd = dkd_s[...]
        dp_ref[:, dv4 + dk4:dv4 + 2 * dk4] = (dkd * dec).astype(BF16)
        w = dkd * kd_f
        dla = (_chunk_cumsum(w) - w) + dee_s[...]
        dz = dla * (1.0 / GATE_TEMP) * (1.0 - jax.nn.sigmoid(z))
        dp_ref[:, 4 * dv4:4 * dv4 + LANE] = _dot(dz, wa_ref[...], _NT).astype(BF16)
        dwa_ref[...] += _dot(alr_ref[...], dz, _TN)
        dba_ref[...] += jnp.sum(dz, axis=0, keepdims=True)

    full = lambda shape: pl.BlockSpec(shape, lambda i: tuple(0 for _ in shape))
    return pl.pallas_call(
        body, name="gla_bwd", grid=(nsteps,),
        in_specs=_gla_in_specs(t, dk4, dv4, rev)
        + [pl.BlockSpec((t, dv4), lambda i: (rev(i), 1)),
           pl.BlockSpec((t, dv4), lambda i: (rev(i), 0)),
           pl.BlockSpec((nc, HEADS, dv, dk), lambda i: (rev(i), 0, 0, 0)),
           pl.BlockSpec((1, HEADS, dv, dk), lambda i: (jnp.maximum(rev(i) * nc - 1, 0), 0, 0, 0)),
           full((LANE, dk4)), full((1, dk4)), full((1, dv4))],
        out_specs=[pl.BlockSpec((t, pw), lambda i: (rev(i), 0)),
                   full((LANE, dk4)), full((1, dk4)), full((1, dv4))],
        out_shape=[jax.ShapeDtypeStruct((n, pw), BF16), jax.ShapeDtypeStruct((LANE, dk4), F32),
                   jax.ShapeDtypeStruct((1, dk4), F32), jax.ShapeDtypeStruct((1, dv4), F32)],
        scratch_shapes=[pltpu.VMEM((HEADS, dv, dk), F32), pltpu.VMEM((t, dk4), F32),
                        pltpu.VMEM((t, dk4), F32), pltpu.VMEM((nc, HEADS, dv, dk), F32),
                        pltpu.VMEM((t, dk4), F32), pltpu.VMEM((t, dk4), F32),
                        pltpu.VMEM((t, dv4), F32), pltpu.VMEM((t, dv4), BF16)],
        compiler_params=_cparams(("arbitrary",)),
    )(proj, proj, proj, proj, proj, dymix, du, states, states, wa2p, b_a, head_norm)


def _softmax_rows(q, k, scale):
    s = _dot(q, k, _NT) * scale
    p = jnp.exp(s - jnp.max(s, axis=-1, keepdims=True))
    return p / jnp.sum(p, axis=-1, keepdims=True)


def _attn_fwd(q, kv):
    n, d = q.shape
    m = kv.shape[0]
    dh = d // HEADS
    tm = _tile(n, 512, 16)
    scale = dh ** -0.5

    def body(q_ref, k_ref, v_ref, o_ref):
        for h in range(HEADS):
            hs = slice(h * dh, (h + 1) * dh)
            p = _softmax_rows(q_ref[:, hs], k_ref[:, hs], scale)
            o_ref[:, hs] = _dot(p, v_ref[:, hs]).astype(BF16)

    return pl.pallas_call(
        body, name="attn_fwd", grid=(n // tm,),
        in_specs=[pl.BlockSpec((tm, d), lambda i: (i, 0)), pl.BlockSpec((m, d), lambda i: (0, 0)),
                  pl.BlockSpec((m, d), lambda i: (0, 1))],
        out_specs=pl.BlockSpec((tm, d), lambda i: (i, 0)),
        out_shape=jax.ShapeDtypeStruct((n, d), BF16),
        compiler_params=_cparams(("parallel",)),
    )(q, kv, kv)


def _attn_bwd(q, kv, do):
    n, d = q.shape
    m = kv.shape[0]
    dh = d // HEADS
    tm = _tile(n, 512, 16)
    scale = dh ** -0.5

    def body(q_ref, k_ref, v_ref, do_ref, dq_ref, dk_ref, dv_ref):
        i = pl.program_id(0)

        @pl.when(i == 0)
        def _():
            dk_ref[...] = jnp.zeros_like(dk_ref)
            dv_ref[...] = jnp.zeros_like(dv_ref)

        for h in range(HEADS):
            hs = slice(h * dh, (h + 1) * dh)
            qh, kh, vh, doh = q_ref[:, hs], k_ref[:, hs], v_ref[:, hs], do_ref[:, hs]
            p = _softmax_rows(qh, kh, scale)
            dv_ref[:, hs] += _dot(p, doh, _TN)
            dp = _dot(doh, vh, _NT)
            ds = p * (dp - jnp.sum(dp * p, axis=-1, keepdims=True)) * scale
            dq_ref[:, hs] = _dot(ds, kh).astype(BF16)
            dk_ref[:, hs] += _dot(ds, qh, _TN)

    row = pl.BlockSpec((tm, d), lambda i: (i, 0))
    memb = pl.BlockSpec((m, d), lambda i: (0, 0))
    return pl.pallas_call(
        body, name="attn_bwd", grid=(n // tm,),
        in_specs=[row, memb, pl.BlockSpec((m, d), lambda i: (0, 1)), row],
        out_specs=[row, memb, memb],
        out_shape=[jax.ShapeDtypeStruct((n, d), BF16), jax.ShapeDtypeStruct((m, d), F32),
                   jax.ShapeDtypeStruct((m, d), F32)],
        compiler_params=_cparams(("arbitrary",)),
    )(q, kv, kv, do)


def _adamw(name, w, m, v, parts, own_block):
    r, c = w.shape
    tr = _tile(r, max(16, ELEMENTWISE_BLOCK // c), 16)
    row = pl.BlockSpec((tr, c), lambda i, o: (i, 0))
    ops, specs = [w, m, v], [row, row, row]
    for p in parts:
        if p.ndim == 2:
            ops.append(p)
            specs.append(row)
        elif p.shape[0] == 4:
            ops.append(p)
            specs.append(pl.BlockSpec((None, tr, c), lambda i, o: (o[0], i, 0)))
        else:
            for s in range(p.shape[0]):
                ops.append(p)
                specs.append(pl.BlockSpec((None, tr, c), lambda i, o, s=s: (s, i, 0)))
    n_parts = len(ops) - 3
    c1 = 1.0 - ADAM_B1 ** ADAM_STEP
    c2 = 1.0 - ADAM_B2 ** ADAM_STEP

    def body(o_ref, *refs):
        del o_ref
        w_ref, m_ref, v_ref = refs[:3]
        g_refs = refs[3:3 + n_parts]
        go_ref, d_ref, mo_ref, vo_ref = refs[3 + n_parts:]
        g = g_refs[0][...].astype(F32)
        for g_ref in g_refs[1:]:
            g = g + g_ref[...].astype(F32)
        m_new = ADAM_B1 * m_ref[...] + (1.0 - ADAM_B1) * g
        v_new = ADAM_B2 * v_ref[...] + (1.0 - ADAM_B2) * (g * g)
        m_hat = m_new / c1
        v_hat = v_new / c2
        go_ref[...] = g
        d_ref[...] = -ADAM_LR * (m_hat / (jnp.sqrt(v_hat) + ADAM_EPS) + ADAM_WD * w_ref[...])
        mo_ref[...] = m_new
        vo_ref[...] = v_new

    shp = jax.ShapeDtypeStruct((r, c), F32)
    return pl.pallas_call(
        body, name=name,
        grid_spec=pltpu.PrefetchScalarGridSpec(
            num_scalar_prefetch=1, grid=(r // tr,), in_specs=specs, out_specs=[row] * 4),
        out_shape=[shp] * 4,
        compiler_params=_cparams(("parallel",)),
    )(own_block, *ops)


def _pair_add(name, g, recvd, core):
    _, r, c = g.shape
    tr = _tile(r, max(16, ELEMENTWISE_BLOCK // c), 16)

    def body(core_ref, a_ref, b_ref, o_ref):
        del core_ref
        o_ref[...] = (a_ref[...].astype(F32) + b_ref[...].astype(F32)).astype(o_ref.dtype)

    blk = pl.BlockSpec((None, tr, c), lambda s, i, core_ref: (s, i, 0))
    mine = pl.BlockSpec((None, tr, c), lambda s, i, core_ref: (2 * s + core_ref[0], i, 0))
    return pl.pallas_call(
        body, name=name,
        grid_spec=pltpu.PrefetchScalarGridSpec(
            num_scalar_prefetch=1, grid=(4, r // tr), in_specs=[mine, blk], out_specs=blk),
        out_shape=jax.ShapeDtypeStruct(recvd.shape, g.dtype),
        compiler_params=_cparams(("parallel", "parallel")),
    )(core, g, recvd)


def _position():
    return lax.axis_index("x"), lax.axis_index("y"), lax.axis_index("c")


_HBM = pl.BlockSpec(memory_space=pltpu.HBM)

_Exchange = collections.namedtuple("_Exchange", "ins out_shape sems start finish")
_NO_EXCHANGE = _Exchange((), (), (), lambda ins, outs, sems: None, lambda ins, outs, sems: None)


def _all_gather(name, shards):
    n = len(shards)

    def body(*refs):
        ins, outs = refs[:n], refs[n:2 * n]
        send_sems, recv_sems, local_sems = refs[2 * n:]
        x, y, c = _position()
        me, sibling = (x, y, c), (x, y, 1 - c)
        chips = [(1 - x, y), (x, 1 - y), (1 - x, 1 - y)]

        def block(ref, px, py, pc):
            return ref.at[4 * px + 2 * py + pc]

        def copy(a, k, owner, to, src=None):
            return pltpu.make_async_remote_copy(
                src_ref=block(outs[a], *owner) if src is None else src, dst_ref=block(outs[a], *owner),
                send_sem=send_sems.at[7 * a + k], recv_sem=recv_sems.at[7 * a + k],
                device_id=to, device_id_type=MESH)

        started = []
        for a in range(n):
            mine = pltpu.make_async_copy(ins[a], block(outs[a], *me), local_sems.at[a])
            mine.start()
            started.append(mine)
        sends = []
        for a in range(n):
            sends.append(copy(a, 0, me, sibling, src=ins[a]))
            sends += [copy(a, 1 + j, me, (*chip, c), src=ins[a]) for j, chip in enumerate(chips)]
        for cp in sends:
            cp.start()
        for j, chip in enumerate(chips):
            for a in range(n):
                copy(a, 1 + j, (*chip, c), me).wait_recv()
                fwd = copy(a, 4 + j, (*chip, c), sibling)
                fwd.start()
                sends.append(fwd)
        for a in range(n):
            copy(a, 0, sibling, me).wait_recv()
            for j, chip in enumerate(chips):
                copy(a, 4 + j, (*chip, 1 - c), me).wait_recv()
        for cp in sends:
            cp.wait_send()
        for mine in started:
            mine.wait()

    return pl.pallas_call(
        body, name=name,
        in_specs=[_HBM] * n, out_specs=[_HBM] * n,
        out_shape=[jax.ShapeDtypeStruct((N_DEV,) + s.shape, s.dtype) for s in shards],
        scratch_shapes=[pltpu.SemaphoreType.DMA((7 * n,)), pltpu.SemaphoreType.DMA((7 * n,)),
                        pltpu.SemaphoreType.DMA((n,))],
    )(*shards)


def _pair_exchange(grads):
    n = len(grads)

    def start(ins, recvd, sems):
        send_sems, recv_sems = sems
        x, y, c = _position()
        for a in range(n):
            for chip in range(4):
                pltpu.make_async_remote_copy(
                    src_ref=ins[a].at[2 * chip + (1 - c)], dst_ref=recvd[a].at[chip],
                    send_sem=send_sems.at[a], recv_sem=recv_sems.at[a],
                    device_id=(x, y, 1 - c), device_id_type=MESH).start()

    def finish(ins, recvd, sems):
        del ins
        send_sems, recv_sems = sems
        x, y, c = _position()
        for a in range(n):
            pltpu.make_async_remote_copy(
                src_ref=recvd[a], dst_ref=recvd[a], send_sem=send_sems.at[a], recv_sem=recv_sems.at[a],
                device_id=(x, y, 1 - c), device_id_type=MESH).wait()

    return _Exchange(
        ins=tuple(grads),
        out_shape=tuple(jax.ShapeDtypeStruct((4,) + g.shape[1:], g.dtype) for g in grads),
        sems=(pltpu.SemaphoreType.DMA((n,)), pltpu.SemaphoreType.DMA((n,))),
        start=start, finish=finish)


def _run_exchange(name, exchange):
    n_in, n_out = len(exchange.ins), len(exchange.out_shape)

    def body(*refs):
        ins, outs, sems = refs[:n_in], refs[n_in:n_in + n_out], refs[n_in + n_out:]
        exchange.start(ins, outs, sems)
        exchange.finish(ins, outs, sems)

    return pl.pallas_call(
        body, name=name, in_specs=[_HBM] * n_in, out_specs=[_HBM] * n_out,
        out_shape=list(exchange.out_shape), scratch_shapes=list(exchange.sems),
    )(*exchange.ins)


def _chip_scatter_exchange(sums):
    n = len(sums)
    offsets = [(1, 0), (0, 1), (1, 1)]

    def start(ins, recvd, sems):
        send_sems, recv_sems = sems
        x, y, c = _position()
        for a in range(n):
            for r, (ox, oy) in enumerate(offsets):
                px = 1 - x if ox else x
                py = 1 - y if oy else y
                pltpu.make_async_remote_copy(
                    src_ref=ins[a].at[2 * px + py], dst_ref=recvd[a].at[r],
                    send_sem=send_sems.at[a], recv_sem=recv_sems.at[a],
                    device_id=(px, py, c), device_id_type=MESH).start()

    def finish(ins, recvd, sems):
        del ins
        send_sems, recv_sems = sems
        x, y, c = _position()
        for a in range(n):
            pltpu.make_async_remote_copy(
                src_ref=recvd[a], dst_ref=recvd[a], send_sem=send_sems.at[a], recv_sem=recv_sems.at[a],
                device_id=(x, y, c), device_id_type=MESH).wait()

    return _Exchange(
        ins=tuple(sums),
        out_shape=tuple(jax.ShapeDtypeStruct((3,) + s.shape[1:], s.dtype) for s in sums),
        sems=(pltpu.SemaphoreType.DMA((n,)), pltpu.SemaphoreType.DMA((n,))),
        start=start, finish=finish)


def _gather_spread_exchange(shards):
    n = len(shards)

    def peers():
        x, y, c = _position()
        return (x, y, c), [(x, y, 1 - c), (1 - x, y, c), (x, 1 - y, c), (1 - x, 1 - y, c)]

    def block(ref, px, py, pc):
        return ref.at[4 * px + 2 * py + pc]

    def copy(ins, outs, sems, a, k, owner, to):
        send_sems, recv_sems, _ = sems
        return pltpu.make_async_remote_copy(
            src_ref=ins[a], dst_ref=block(outs[a], *owner),
            send_sem=send_sems.at[4 * a + k], recv_sem=recv_sems.at[4 * a + k],
            device_id=to, device_id_type=MESH)

    def start(ins, outs, sems):
        me, others = peers()
        for a in range(n):
            pltpu.make_async_copy(ins[a], block(outs[a], *me), sems[2].at[a]).start()
            for k, to in enumerate(others):
                copy(ins, outs, sems, a, k, me, to).start()

    def finish(ins, outs, sems):
        me, others = peers()
        for a in range(n):
            for k, peer in enumerate(others):
                cp = copy(ins, outs, sems, a, k, peer, peer)
                cp.wait_recv()
                cp.wait_send()
            pltpu.make_async_copy(ins[a], block(outs[a], *me), sems[2].at[a]).wait()

    return _Exchange(
        ins=tuple(shards),
        out_shape=tuple(jax.ShapeDtypeStruct((N_DEV,) + s.shape, s.dtype) for s in shards),
        sems=(pltpu.SemaphoreType.DMA((4 * n,)), pltpu.SemaphoreType.DMA((4 * n,)),
              pltpu.SemaphoreType.DMA((n,))),
        start=start, finish=finish)


def _gather_forward(name, partial):
    n = len(partial)

    def body(*refs):
        ins, outs = refs[:n], refs[n:2 * n]
        send_sems, recv_sems = refs[2 * n:]
        x, y, c = _position()
        chips = [(1 - x, y), (x, 1 - y), (1 - x, 1 - y)]

        def copy(a, j, pc):
            blk = 4 * chips[j][0] + 2 * chips[j][1] + pc
            return pltpu.make_async_remote_copy(
                src_ref=ins[a].at[blk], dst_ref=outs[a].at[blk],
                send_sem=send_sems.at[3 * a + j], recv_sem=recv_sems.at[3 * a + j],
                device_id=(x, y, 1 - c), device_id_type=MESH)

        for a in range(n):
            for j in range(3):
                copy(a, j, c).start()
        for a in range(n):
            for j in range(3):
                copy(a, j, 1 - c).wait_recv()
                copy(a, j, c).wait_send()

    return pl.pallas_call(
        body, name=name, in_specs=[_HBM] * n, out_specs=[_HBM] * n,
        out_shape=[jax.ShapeDtypeStruct(p.shape, p.dtype) for p in partial],
        input_output_aliases={a: a for a in range(n)},
        scratch_shapes=[pltpu.SemaphoreType.DMA((3 * n,)), pltpu.SemaphoreType.DMA((3 * n,))],
    )(*partial)


def _all_reduce_small(vec):
    r = vec.shape[0]

    def body(v_ref, o_ref, gbuf, send_sems, recv_sems):
        x, y, c = _position()
        me = 4 * x + 2 * y + c
        gbuf[me] = v_ref[...]
        copies = []
        for k in range(1, N_DEV):
            ox, oy, oc = (k >> 2) & 1, (k >> 1) & 1, k & 1
            peer = (1 - x if ox else x, 1 - y if oy else y, 1 - c if oc else c)
            cp = pltpu.make_async_remote_copy(
                src_ref=gbuf.at[me], dst_ref=gbuf.at[me], send_sem=send_sems.at[k - 1],
                recv_sem=recv_sems.at[k - 1], device_id=peer, device_id_type=MESH)
            cp.start()
            copies.append(cp)
        for cp in copies:
            cp.wait()
        total = gbuf[0]
        for j in range(1, N_DEV):
            total = total + gbuf[j]
        o_ref[...] = total

    return pl.pallas_call(
        body, name="all_reduce_small",
        in_specs=[pl.BlockSpec(memory_space=pltpu.VMEM)],
        out_specs=pl.BlockSpec(memory_space=pltpu.VMEM),
        out_shape=jax.ShapeDtypeStruct(vec.shape, F32),
        scratch_shapes=[pltpu.VMEM((N_DEV, r, LANE), F32), pltpu.SemaphoreType.DMA((N_DEV - 1,)),
                        pltpu.SemaphoreType.DMA((N_DEV - 1,))],
    )(vec)


def _cols_from_blocks(g):
    nb, r, cs = g.shape
    return jnp.transpose(g, (1, 0, 2)).reshape(r, nb * cs)


def _cols_to_blocks(w):
    r, cfull = w.shape
    return jnp.transpose(w.reshape(r, N_DEV, cfull // N_DEV), (1, 0, 2))


def kernel(x, mem, ffn1_norm, ffn1_w_gate, ffn1_w_up, ffn1_w_down, mix_norm, w_in, pool_w, pool_scale, gla_w_a2, gla_b_a, gla_head_norm, w_out, xattn_norm, mem_norm, xattn_w_q, xattn_w_kv, xattn_w_o, ffn2_norm, ffn2_w_gate, ffn2_w_up, ffn2_w_down, final_norm, loss_target, m_ffn1_norm, m_ffn1_w_gate, m_ffn1_w_up, m_ffn1_w_down, m_mix_norm, m_w_in, m_pool_w, m_pool_scale, m_gla_w_a2, m_gla_b_a, m_gla_head_norm, m_w_out, m_xattn_norm, m_mem_norm, m_xattn_w_q, m_xattn_w_kv, m_xattn_w_o, m_ffn2_norm, m_ffn2_w_gate, m_ffn2_w_up, m_ffn2_w_down, m_final_norm, v_ffn1_norm, v_ffn1_w_gate, v_ffn1_w_up, v_ffn1_w_down, v_mix_norm, v_w_in, v_pool_w, v_pool_scale, v_gla_w_a2, v_gla_b_a, v_gla_head_norm, v_w_out, v_xattn_norm, v_mem_norm, v_xattn_w_q, v_xattn_w_kv, v_xattn_w_o, v_ffn2_norm, v_ffn2_w_gate, v_ffn2_w_up, v_ffn2_w_down, v_final_norm):
    weights = dict(ffn1_norm=ffn1_norm, ffn1_w_gate=ffn1_w_gate, ffn1_w_up=ffn1_w_up, ffn1_w_down=ffn1_w_down, mix_norm=mix_norm, w_in=w_in, pool_w=pool_w, pool_scale=pool_scale, gla_w_a2=gla_w_a2, gla_b_a=gla_b_a, gla_head_norm=gla_head_norm, w_out=w_out, xattn_norm=xattn_norm, mem_norm=mem_norm, xattn_w_q=xattn_w_q, xattn_w_kv=xattn_w_kv, xattn_w_o=xattn_w_o, ffn2_norm=ffn2_norm, ffn2_w_gate=ffn2_w_gate, ffn2_w_up=ffn2_w_up, ffn2_w_down=ffn2_w_down, final_norm=final_norm)
    mom1 = dict(ffn1_norm=m_ffn1_norm, ffn1_w_gate=m_ffn1_w_gate, ffn1_w_up=m_ffn1_w_up, ffn1_w_down=m_ffn1_w_down, mix_norm=m_mix_norm, w_in=m_w_in, pool_w=m_pool_w, pool_scale=m_pool_scale, gla_w_a2=m_gla_w_a2, gla_b_a=m_gla_b_a, gla_head_norm=m_gla_head_norm, w_out=m_w_out, xattn_norm=m_xattn_norm, mem_norm=m_mem_norm, xattn_w_q=m_xattn_w_q, xattn_w_kv=m_xattn_w_kv, xattn_w_o=m_xattn_w_o, ffn2_norm=m_ffn2_norm, ffn2_w_gate=m_ffn2_w_gate, ffn2_w_up=m_ffn2_w_up, ffn2_w_down=m_ffn2_w_down, final_norm=m_final_norm)
    mom2 = dict(ffn1_norm=v_ffn1_norm, ffn1_w_gate=v_ffn1_w_gate, ffn1_w_up=v_ffn1_w_up, ffn1_w_down=v_ffn1_w_down, mix_norm=v_mix_norm, w_in=v_w_in, pool_w=v_pool_w, pool_scale=v_pool_scale, gla_w_a2=v_gla_w_a2, gla_b_a=v_gla_b_a, gla_head_norm=v_gla_head_norm, w_out=v_w_out, xattn_norm=v_xattn_norm, mem_norm=v_mem_norm, xattn_w_q=v_xattn_w_q, xattn_w_kv=v_xattn_w_kv, xattn_w_o=v_xattn_w_o, ffn2_norm=v_ffn2_norm, ffn2_w_gate=v_ffn2_w_gate, ffn2_w_up=v_ffn2_w_up, ffn2_w_down=v_ffn2_w_down, final_norm=v_final_norm)
    order = list(weights.keys())

    n, d = x.shape[1], x.shape[2]
    mlen = mem.shape[1]
    x0 = x.reshape(n, d)
    memf = mem.reshape(mlen, d)
    target = loss_target.reshape(n, d)
    dpool = d // 2
    in_cols = w_in.shape[2] * N_DEV
    proj_cols = 2 * d + LANE
    rank = gla_w_a2.shape[1]

    def shard(name):
        return weights[name][0].astype(BF16)

    wg1, wu1, wd1 = _all_gather("ag_ffn1", [shard("ffn1_w_gate"), shard("ffn1_w_up"), shard("ffn1_w_down")])
    later = ["w_in", "pool_w", "gla_w_a2", "w_out", "xattn_w_q", "xattn_w_kv", "xattn_w_o",
             "ffn2_w_gate", "ffn2_w_up", "ffn2_w_down"]
    h1 = _rms_fwd("rms_ffn1", x0, ffn1_norm)
    (x1, a1, b1, h2), spread = _ffn_fwd(
        "ffn1_fwd", x0, h1, wg1, wu1, wd1, mix_norm, _gather_spread_exchange([shard(k) for k in later]))
    win_g, pw_g, wa2_g, wout_g, wq_g, wkv_g, wo_g, wg2, wu2, wd2 = _gather_forward("ag_forward", spread)

    winp = jnp.pad(_cols_from_blocks(win_g), ((0, 0), (0, proj_cols - in_cols)))
    pw = jnp.transpose(pw_g, (1, 0, 2, 3)).reshape(len(POOL_WINDOWS), dpool // 4, dpool // 4)
    wa2p = jnp.pad(_cols_from_blocks(wa2_g), ((0, LANE - rank), (0, 0)))
    wout = wout_g.reshape(d, d)
    wq = wq_g.reshape(d, d)
    wkv = _cols_from_blocks(wkv_g)
    wo = wo_g.reshape(d, d)
    fnorm = final_norm.reshape(1, d)

    proj = _mm_nn("mix_in", [(h2, winp)], F32, tn=1408)
    ypool = _pool_fwd(proj, pw, pool_scale)
    ygla, states = _gla_fwd(proj, wa2p, gla_b_a, gla_head_norm)
    x2, h3 = _mm_nn("mix_out", [(ypool, wout[:dpool]), (ygla, wout[dpool:])], F32, res=x1,
                    norm_gain=xattn_norm, tk=1024)
    mh = _rms_fwd("rms_mem", memf, mem_norm)
    q = _mm_nn("xattn_q", [(h3, wq)], BF16)
    kv = _mm_nn("xattn_kv", [(mh, wkv)], BF16)
    att = _attn_fwd(q, kv)
    x3, h4 = _mm_nn("xattn_o", [(att, wo)], F32, res=x2, norm_gain=ffn2_norm)
    (x4, a2, b2), _ = _ffn_fwd("ffn2_fwd", x3, h4, wg2, wu2, wd2, None, _NO_EXCHANGE)
    dx4, dx4b, g_final, loss_part = _loss_head(x4, target, fnorm)

    grads = {}
    core = lax.axis_index("c").astype(jnp.int32).reshape(1)
    chip = (2 * lax.axis_index("x") + lax.axis_index("y")).astype(jnp.int32).reshape(1)
    chip_sums = {}

    def pair_blocks(names):
        return [grads[k].reshape(N_DEV, -1, grads[k].shape[-1]) for k in names]

    def pair_sums(names, blocks, recvd):
        for k, g, r in zip(names, blocks, recvd):
            chip_sums[k] = _pair_add("rs_add_" + k, g, r, core)

    ffn2_names = ["ffn2_w_gate", "ffn2_w_up", "ffn2_w_down"]
    (da2, db2, act2), _ = _ffn_bwd_act("ffn2_bwd_act", dx4b, wd2, a2, b2, _NO_EXCHANGE)
    (grads["ffn2_w_gate"], grads["ffn2_w_up"], grads["ffn2_w_down"]), _ = _ffn_bwd_weights(
        "ffn2_bwd_weights", h4, dx4b, da2, db2, act2, _NO_EXCHANGE)
    blocks = pair_blocks(ffn2_names)
    (dx3, dx3b, g_ffn2_norm), recvd = _ffn_bwd_dh(
        "ffn2_bwd_dh", da2, db2, wg2, wu2, x3, ffn2_norm, dx4, _pair_exchange(blocks))
    pair_sums(ffn2_names, blocks, recvd)

    xattn_names = ["xattn_w_q", "xattn_w_kv", "xattn_w_o"]
    datt = _mm_nt("xattn_do", [(dx3b, wo)], BF16)
    grads["xattn_w_o"] = _mm_tn("xattn_dwo", [(att, dx3b)], BF16).reshape(N_DEV, d // N_DEV, d)
    dq, dk, dv = _attn_bwd(q, kv, datt)
    grads["xattn_w_q"] = _mm_tn("xattn_dwq", [(h3, dq)], BF16).reshape(N_DEV, d // N_DEV, d)
    dkv = jnp.concatenate([dk, dv], axis=1)
    dmh = _mm_nt("xattn_dmh", [(dkv, wkv)], F32)
    grads["xattn_w_kv"] = _cols_to_blocks(_mm_tn("xattn_dwkv", [(mh, dkv)], BF16))
    g_mem_norm = _rms_gain_grad("rms_mem_bwd", memf, mem_norm, dmh)
    blocks = pair_blocks(xattn_names)
    (dx2, dx2b, g_xattn_norm), recvd = _mm_nt_rms_bwd(
        "xattn_dh", dq, wq, x2, xattn_norm, dx3, _pair_exchange(blocks), tk=1024)
    pair_sums(xattn_names, blocks, recvd)

    dymix = _mm_nt("mix_dy", [(dx2b, wout)], F32)
    grads["w_out"] = jnp.concatenate(
        [_mm_tn("mix_dwout_pool", [(ypool, dx2b)], BF16), _mm_tn("mix_dwout_gla", [(ygla, dx2b)], BF16)],
        axis=0).reshape(N_DEV, d // N_DEV, d)
    du, g_pool_w, g_pool_scale = _pool_bwd(proj, dymix, pw, pool_scale)
    dproj, g_wa2p, g_b_a, g_head_norm = _gla_bwd(proj, dymix, du, states, wa2p, gla_b_a, gla_head_norm)
    grads["w_in"] = _cols_to_blocks(_mm_tn("mix_dwin", [(h2, dproj)], BF16, tn=1408)[:, :in_cols])
    grads["pool_w"] = jnp.transpose(
        g_pool_w.reshape(len(POOL_WINDOWS), N_DEV, dpool // 4 // N_DEV, dpool // 4), (1, 0, 2, 3))
    grads["gla_w_a2"] = _cols_to_blocks(g_wa2p[:rank])
    mix_names = ["w_in", "pool_w", "gla_w_a2", "w_out"]
    blocks = pair_blocks(mix_names)
    (dx1, dx1b, g_mix_norm), recvd = _mm_nt_rms_bwd(
        "mix_dh", dproj, winp, x1, mix_norm, dx2, _pair_exchange(blocks), tk=1408)
    pair_sums(mix_names, blocks, recvd)

    ffn1_names = ["ffn1_w_gate", "ffn1_w_up", "ffn1_w_down"]
    (da1, db1, act1), recvd = _ffn_bwd_act(
        "ffn1_bwd_act", dx1b, wd1, a1, b1, _chip_scatter_exchange([chip_sums[k] for k in ffn2_names]))
    chip_recvd = dict(zip(ffn2_names, recvd))
    (grads["ffn1_w_gate"], grads["ffn1_w_up"], grads["ffn1_w_down"]), recvd = _ffn_bwd_weights(
        "ffn1_bwd_weights", h1, dx1b, da1, db1, act1,
        _chip_scatter_exchange([chip_sums[k] for k in xattn_names + mix_names]))
    chip_recvd.update(zip(xattn_names + mix_names, recvd))
    blocks = pair_blocks(ffn1_names)
    pair_sums(ffn1_names, blocks, _run_exchange("rs_pair_ffn1", _pair_exchange(blocks)))
    (dx0, _, g_ffn1_norm), recvd = _ffn_bwd_dh(
        "ffn1_bwd_dh", da1, db1, wg1, wu1, x0, ffn1_norm, dx1,
        _chip_scatter_exchange([chip_sums[k] for k in ffn1_names]))
    chip_recvd.update(zip(ffn1_names, recvd))

    small = [("ffn1_norm", g_ffn1_norm), ("mix_norm", g_mix_norm), ("pool_scale", g_pool_scale),
             ("gla_b_a", g_b_a), ("gla_head_norm", g_head_norm), ("xattn_norm", g_xattn_norm),
             ("mem_norm", g_mem_norm), ("ffn2_norm", g_ffn2_norm), ("final_norm", g_final)]
    packed = jnp.concatenate([g.reshape(-1) for _, g in small] + [loss_part.reshape(-1)])
    slab = 8 * LANE
    padded = -(-packed.shape[0] // slab) * slab
    packed = jnp.pad(packed, (0, padded - packed.shape[0])).reshape(padded // LANE, LANE)
    reduced = _all_reduce_small(packed).reshape(-1)
    small_grads = {}
    off = 0
    for name, g in small:
        small_grads[name] = reduced[off:off + g.size]
        off += g.size
    loss = reduced[off]

    out_g, out_d, out_m, out_v = {}, {}, {}, {}
    for k in order:
        w = weights[k]
        if k in small_grads:
            w2 = w.reshape(1, -1)
            parts = [small_grads[k].reshape(1, -1)]
            own_block = jnp.zeros((1,), jnp.int32)
        else:
            w2 = w.reshape(-1, w.shape[-1])
            parts = [chip_sums[k], chip_recvd[k]]
            own_block = chip
        res = _adamw("adamw_" + k, w2, mom1[k].reshape(w2.shape), mom2[k].reshape(w2.shape), parts, own_block)
        out_g[k], out_d[k], out_m[k], out_v[k] = [r.reshape(w.shape) for r in res]

    return (loss, dx0.reshape(x.shape), *[out_g[k] for k in order], *[out_d[k] for k in order],
            *[out_m[k] for k in order], *[out_v[k] for k in order])
```

```python
import collections

import jax
import jax.numpy as jnp
from jax import lax
from jax.experimental import pallas as pl
from jax.experimental.pallas import tpu as pltpu

F32 = jnp.float32
BF16 = jnp.bfloat16
MESH = pl.DeviceIdType.MESH

N_DEV = 8
CHUNK = 64
POOL_WINDOWS = (2, 4, 8, 16)
POOL_HALO = 16
HEADS = 4
GATE_TEMP = 16.0
RMS_EPS = 1e-6
LANE = 128
V7X_VMEM_BYTES = 64 * 1024 * 1024
VMEM_LIMIT = V7X_VMEM_BYTES - 8 * 1024 * 1024
GLA_ROWS = 4 * CHUNK
ELEMENTWISE_BLOCK = 512 * 1024
FFN_SUBTILES = 2

ADAM_LR = 0.001
ADAM_B1 = 0.9
ADAM_B2 = 0.999
ADAM_EPS = 1e-08
ADAM_WD = 0.01
ADAM_STEP = 10

_NN = (((1,), (0,)), ((), ()))
_NT = (((1,), (1,)), ((), ()))
_TN = (((0,), (0,)), ((), ()))


def _cparams(sem=None):
    return pltpu.CompilerParams(dimension_semantics=sem, vmem_limit_bytes=VMEM_LIMIT)


def _tile(n, pref, align):
    t = (min(pref, n) // align) * align
    while t >= align:
        if n % t == 0:
            return t
        t -= align
    return n


def _dot(a, b, dims=_NN):
    return lax.dot_general(a.astype(BF16), b.astype(BF16), dims, preferred_element_type=F32)


def _silu_parts(z):
    sig = jax.nn.sigmoid(z)
    return z * sig, sig * (1.0 + z * (1.0 - sig))


def _rms_scale(xf):
    return lax.rsqrt(jnp.mean(xf * xf, axis=-1, keepdims=True) + RMS_EPS)


def _matmul(name, pairs, a_spec, b_spec, out_shape, out_spec, grid, acc_shape, dims,
            res=None, res_spec=None, scale=None, norm_gain=None, exchange=None):
    n = len(pairs)
    nk = grid[2]

    def body(ins, outs, scr):
        a_refs, b_refs = ins[:n], ins[n:2 * n]
        pos = 2 * n
        res_ref = gain_ref = h_ref = None
        if res is not None:
            res_ref = ins[pos]
            pos += 1
        if norm_gain is not None:
            gain_ref = ins[pos]
            h_ref = outs[1]
        o_ref = outs[0]
        acc, = scr
        k = pl.program_id(2)

        @pl.when(k == 0)
        def _():
            acc[...] = jnp.zeros_like(acc)

        part = None
        for a_ref, b_ref in zip(a_refs, b_refs):
            d = _dot(a_ref[...], b_ref[...], dims)
            part = d if part is None else part + d
        acc[...] += part

        @pl.when(k == nk - 1)
        def _():
            r = acc[...]
            if scale is not None:
                r = r * scale
            if res_ref is not None:
                r = r + res_ref[...]
            o_ref[...] = r.astype(o_ref.dtype)
            if h_ref is not None:
                h_ref[...] = ((r * _rms_scale(r)) * gain_ref[...]).astype(BF16)

    ops = [p[0] for p in pairs] + [p[1] for p in pairs]
    specs = [a_spec] * n + [b_spec] * n
    if res is not None:
        ops.append(res)
        specs.append(res_spec)
    out_specs, out_shapes = [out_spec], [out_shape]
    if norm_gain is not None:
        ops.append(norm_gain)
        specs.append(pl.BlockSpec(norm_gain.shape, lambda i, j, k: (0, 0)))
        out_specs.append(out_spec)
        out_shapes.append(jax.ShapeDtypeStruct(out_shape.shape, BF16))
    outs, carried = _hosted_call(
        name, body, grid, specs, out_specs, out_shapes, [pltpu.VMEM(acc_shape, F32)], ops,
        _NO_EXCHANGE if exchange is None else exchange)
    result = outs[0] if norm_gain is None else tuple(outs)
    return result if exchange is None else (result, carried)


def _mm_nn(name, pairs, out_dtype, res=None, norm_gain=None, exchange=None, tm=1024, tn=1024, tk=2048):
    m, kd = pairs[0][0].shape
    nd = pairs[0][1].shape[1]
    if norm_gain is not None:
        tm, tn = 512, nd
    tm, tn, tk = _tile(m, tm, 16), _tile(nd, tn, LANE), _tile(kd, tk, LANE)
    return _matmul(
        name, pairs,
        pl.BlockSpec((tm, tk), lambda i, j, k: (i, k)),
        pl.BlockSpec((tk, tn), lambda i, j, k: (k, j)),
        jax.ShapeDtypeStruct((m, nd), out_dtype),
        pl.BlockSpec((tm, tn), lambda i, j, k: (i, j)),
        (m // tm, nd // tn, kd // tk), (tm, tn), _NN,
        res=res, res_spec=pl.BlockSpec((tm, tn), lambda i, j, k: (i, j)), norm_gain=norm_gain,
        exchange=exchange)


def _mm_nt(name, pairs, out_dtype, tm=1024, tn=1024, tk=2048):
    m, kd = pairs[0][0].shape
    nd = pairs[0][1].shape[0]
    tm, tn, tk = _tile(m, tm, 16), _tile(nd, tn, LANE), _tile(kd, tk, LANE)
    return _matmul(
        name, pairs,
        pl.BlockSpec((tm, tk), lambda i, j, k: (i, k)),
        pl.BlockSpec((tn, tk), lambda i, j, k: (j, k)),
        jax.ShapeDtypeStruct((m, nd), out_dtype),
        pl.BlockSpec((tm, tn), lambda i, j, k: (i, j)),
        (m // tm, nd // tn, kd // tk), (tm, tn), _NT)


def _mm_tn(name, pairs, out_dtype, tm=1024, tn=2048, tk=1024):
    kd, m = pairs[0][0].shape
    nd = pairs[0][1].shape[1]
    tm, tn, tk = _tile(m, tm, LANE), _tile(nd, tn, LANE), _tile(kd, tk, 16)
    return _matmul(
        name, pairs,
        pl.BlockSpec((tk, tm), lambda i, j, k: (k, i)),
        pl.BlockSpec((tk, tn), lambda i, j, k: (k, j)),
        jax.ShapeDtypeStruct((m, nd), out_dtype),
        pl.BlockSpec((tm, tn), lambda i, j, k: (i, j)),
        (m // tm, nd // tn, kd // tk), (tm, tn), _TN)


def _rms_fwd(name, x, gain):
    m, d = x.shape
    tm = _tile(m, 512, 16)

    def body(x_ref, g_ref, o_ref):
        xf = x_ref[...]
        r = lax.rsqrt(jnp.mean(xf * xf, axis=-1, keepdims=True) + RMS_EPS)
        o_ref[...] = ((xf * r) * g_ref[...]).astype(o_ref.dtype)

    return pl.pallas_call(
        body, name=name, grid=(m // tm,),
        in_specs=[pl.BlockSpec((tm, d), lambda i: (i, 0)), pl.BlockSpec((1, d), lambda i: (0, 0))],
        out_specs=pl.BlockSpec((tm, d), lambda i: (i, 0)),
        out_shape=jax.ShapeDtypeStruct((m, d), BF16),
        compiler_params=_cparams(("parallel",)),
    )(x, gain)


def _rms_bwd_rows(xf, gain, dhf):
    r = _rms_scale(xf)
    xh = xf * r
    t = dhf * gain
    dx = r * (t - xh * jnp.mean(t * xh, axis=-1, keepdims=True))
    return dx, jnp.sum(dhf * xh, axis=0, keepdims=True)


def _rms_gain_grad(name, x, gain, dh):
    m, d = x.shape
    tm = _tile(m, 512, 16)

    def body(x_ref, g_ref, dh_ref, dg_ref):
        @pl.when(pl.program_id(0) == 0)
        def _():
            dg_ref[...] = jnp.zeros_like(dg_ref)

        dg_ref[...] += _rms_bwd_rows(x_ref[...], g_ref[...], dh_ref[...])[1]

    row = pl.BlockSpec((tm, d), lambda i: (i, 0))
    vec = pl.BlockSpec((1, d), lambda i: (0, 0))
    return pl.pallas_call(
        body, name=name, grid=(m // tm,), in_specs=[row, vec, row], out_specs=vec,
        out_shape=jax.ShapeDtypeStruct((1, d), F32),
        compiler_params=_cparams(("arbitrary",)),
    )(x, gain, dh)


def _loss_head(x, target, gain):
    m, d = x.shape
    tm = _tile(m, 512, 16)

    def body(x_ref, t_ref, g_ref, dx_ref, dxb_ref, dg_ref, loss_ref):
        i = pl.program_id(0)

        @pl.when(i == 0)
        def _():
            dg_ref[...] = jnp.zeros_like(dg_ref)
            loss_ref[...] = jnp.zeros_like(loss_ref)

        xf = x_ref[...]
        r = lax.rsqrt(jnp.mean(xf * xf, axis=-1, keepdims=True) + RMS_EPS)
        xh = xf * r
        g = g_ref[...]
        err = xh * g - t_ref[...]
        loss_ref[...] += jnp.full(loss_ref.shape, (0.5 / d) * jnp.sum(err * err), F32)
        dy = err * (1.0 / d)
        t = dy * g
        dx = r * (t - xh * jnp.mean(t * xh, axis=-1, keepdims=True))
        dx_ref[...] = dx
        dxb_ref[...] = dx.astype(BF16)
        dg_ref[...] += jnp.sum(dy * xh, axis=0, keepdims=True)

    row = pl.BlockSpec((tm, d), lambda i: (i, 0))
    vec = pl.BlockSpec((1, d), lambda i: (0, 0))
    return pl.pallas_call(
        body, name="loss_head", grid=(m // tm,),
        in_specs=[row, row, vec],
        out_specs=[row, row, vec, pl.BlockSpec((1, LANE), lambda i: (0, 0))],
        out_shape=[jax.ShapeDtypeStruct((m, d), F32), jax.ShapeDtypeStruct((m, d), BF16),
                   jax.ShapeDtypeStruct((1, d), F32), jax.ShapeDtypeStruct((1, LANE), F32)],
        compiler_params=_cparams(("arbitrary",)),
    )(x, target, gain)


def _hosted_call(name, body, grid, in_specs, out_specs, out_shape, scratch, operands, exchange):
    n_in, n_out, n_scr = len(in_specs), len(out_specs), len(scratch)
    n_xin, n_xout = len(exchange.ins), len(exchange.out_shape)

    def full_body(*refs):
        pos = 0
        parts = []
        for cnt in (n_in, n_xin, n_out, n_xout, n_scr):
            parts.append(refs[pos:pos + cnt])
            pos += cnt
        ins, x_ins, outs, x_outs, scr = parts
        sems = refs[pos:]
        first = pl.program_id(0) == 0
        last = pl.program_id(0) == grid[0] - 1
        for ax in range(1, len(grid)):
            first = jnp.logical_and(first, pl.program_id(ax) == 0)
            last = jnp.logical_and(last, pl.program_id(ax) == grid[ax] - 1)

        @pl.when(first)
        def _():
            exchange.start(x_ins, x_outs, sems)

        body(ins, outs, scr)

        @pl.when(last)
        def _():
            exchange.finish(x_ins, x_outs, sems)

    aliases = {n_in + k: n_out + k for k in range(n_xin)} if exchange.in_place else {}
    res = pl.pallas_call(
        full_body, name=name, grid=grid,
        in_specs=list(in_specs) + [_HBM] * n_xin, out_specs=list(out_specs) + [_HBM] * n_xout,
        out_shape=list(out_shape) + list(exchange.out_shape),
        scratch_shapes=list(scratch) + list(exchange.sems),
        input_output_aliases=aliases,
        compiler_params=_cparams(("arbitrary",) * len(grid)),
    )(*operands, *exchange.ins)
    return res[:n_out], res[n_out:]


def _ffn_fwd(name, x, h, wg, wu, wd, next_gain, exchange):
    n, d = x.shape
    nb, _, fb = wg.shape
    tm = _tile(n, 512, 16 * FFN_SUBTILES)
    ts = tm // FFN_SUBTILES
    with_norm = next_gain is not None

    def body(ins, outs, scr):
        x_ref, h_ref, wg_ref, wu_ref, wd_ref = ins[:5]
        xo_ref, a_ref, b_ref = outs[:3]
        acc, = scr
        j = pl.program_id(1)

        @pl.when(j == 0)
        def _():
            acc[...] = jnp.zeros_like(acc)

        for s in range(FFN_SUBTILES):
            rows = slice(s * ts, (s + 1) * ts)
            hh = h_ref[rows, :]
            a = _dot(hh, wg_ref[...])
            b = _dot(hh, wu_ref[...])
            a_ref[rows, :] = a.astype(BF16)
            b_ref[rows, :] = b.astype(BF16)
            act = (a * jax.nn.sigmoid(a)) * b
            acc[rows, :] += _dot(act, wd_ref[...])

        @pl.when(j == nb - 1)
        def _():
            xo = x_ref[...] + 0.5 * acc[...]
            xo_ref[...] = xo
            if with_norm:
                outs[3][...] = ((xo * _rms_scale(xo)) * ins[5][...]).astype(BF16)

    row = pl.BlockSpec((tm, d), lambda i, j: (i, 0))
    w_in = pl.BlockSpec((None, d, fb), lambda i, j: (j, 0, 0))
    hid = pl.BlockSpec((None, tm, fb), lambda i, j: (j, i, 0))
    return _hosted_call(
        name, body, (n // tm, nb),
        [row, row, w_in, w_in, pl.BlockSpec((None, fb, d), lambda i, j: (j, 0, 0))]
        + ([pl.BlockSpec((1, d), lambda i, j: (0, 0))] if with_norm else []),
        [row, hid, hid] + ([row] if with_norm else []),
        [jax.ShapeDtypeStruct((n, d), F32), jax.ShapeDtypeStruct((nb, n, fb), BF16),
         jax.ShapeDtypeStruct((nb, n, fb), BF16)]
        + ([jax.ShapeDtypeStruct((n, d), BF16)] if with_norm else []),
        [pltpu.VMEM((tm, d), F32)], (x, h, wg, wu, wd) + ((next_gain,) if with_norm else ()), exchange)


def _ffn_bwd_act(name, dxb, wd, a, b, exchange):
    n, d = dxb.shape
    nb, fb, _ = wd.shape
    subtiles = 2 * FFN_SUBTILES
    tm = _tile(n, 1024, 16 * subtiles)
    ts = tm // subtiles

    def body(ins, outs, scr):
        del scr
        dx_ref, wd_ref, a_ref, b_ref = ins
        da_ref, db_ref, act_ref = outs
        for s in range(subtiles):
            rows = slice(s * ts, (s + 1) * ts)
            dact = 0.5 * _dot(dx_ref[rows, :], wd_ref[...], _NT)
            af = a_ref[rows, :].astype(F32)
            bf = b_ref[rows, :].astype(F32)
            sl, dsl = _silu_parts(af)
            act_ref[rows, :] = (sl * bf).astype(BF16)
            db_ref[rows, :] = (dact * sl).astype(BF16)
            da_ref[rows, :] = (dact * bf * dsl).astype(BF16)

    hid = pl.BlockSpec((None, tm, fb), lambda i, j: (j, i, 0))
    shp = jax.ShapeDtypeStruct((nb, n, fb), BF16)
    return _hosted_call(
        name, body, (n // tm, nb),
        [pl.BlockSpec((tm, d), lambda i, j: (i, 0)), pl.BlockSpec((None, fb, d), lambda i, j: (j, 0, 0)), hid, hid],
        [hid, hid, hid], [shp, shp, shp], [], (dxb, wd, a, b), exchange)


def _proj_rms_bwd(name, pairs, a_spec, b_spec, nk, tm, x, gain, dres, exchange):
    n, d = x.shape
    n_pairs = len(pairs)
    slab = _tile(tm, 128, 16)

    def body(ins, outs, scr):
        a_refs, b_refs = ins[:n_pairs], ins[n_pairs:2 * n_pairs]
        x_ref, g_ref, res_ref = ins[2 * n_pairs:]
        dx_ref, dxb_ref, dg_ref = outs
        del scr
        i, k = pl.program_id(0), pl.program_id(1)

        @pl.when(jnp.logical_and(i == 0, k == 0))
        def _():
            dg_ref[...] = jnp.zeros_like(dg_ref)

        @pl.when(k == 0)
        def _():
            dx_ref[...] = jnp.zeros_like(dx_ref)

        for a_ref, b_ref in zip(a_refs, b_refs):
            dx_ref[...] += _dot(a_ref[...], b_ref[...], _NT)

        @pl.when(k == nk - 1)
        def _():
            for s in range(tm // slab):
                rows = slice(s * slab, (s + 1) * slab)
                dx, dg = _rms_bwd_rows(x_ref[rows, :], g_ref[...], dx_ref[rows, :])
                dx = dx + res_ref[rows, :]
                dx_ref[rows, :] = dx
                dxb_ref[rows, :] = dx.astype(BF16)
                dg_ref[...] += dg

    row = pl.BlockSpec((tm, d), lambda i, k: (i, 0))
    vec = pl.BlockSpec((1, d), lambda i, k: (0, 0))
    return _hosted_call(
        name, body, (n // tm, nk), [a_spec] * n_pairs + [b_spec] * n_pairs + [row, vec, row],
        [row, row, vec],
        [jax.ShapeDtypeStruct((n, d), F32), jax.ShapeDtypeStruct((n, d), BF16),
         jax.ShapeDtypeStruct((1, d), F32)],
        [], tuple(p[0] for p in pairs) + tuple(p[1] for p in pairs) + (x, gain, dres), exchange)


def _ffn_bwd_dh(name, da, db, wg, wu, x, gain, dres, exchange):
    nb, n, fb = da.shape
    d = wg.shape[1]
    tm = _tile(n, 512, 16)
    return _proj_rms_bwd(
        name, [(da, wg), (db, wu)],
        pl.BlockSpec((None, tm, fb), lambda i, k: (k, i, 0)),
        pl.BlockSpec((None, d, fb), lambda i, k: (k, 0, 0)), nb, tm, x, gain, dres, exchange)


def _mm_nt_rms_bwd(name, a, w, x, gain, dres, exchange, tk=2048):
    n, kd = a.shape
    d = w.shape[0]
    tm, tk = _tile(n, 512, 16), _tile(kd, tk, LANE)
    return _proj_rms_bwd(
        name, [(a, w)],
        pl.BlockSpec((tm, tk), lambda i, k: (i, k)),
        pl.BlockSpec((d, tk), lambda i, k: (0, k)), kd // tk, tm, x, gain, dres, exchange)


def _ffn_bwd_weights(name, h, dxb, da, db, act, exchange):
    n, d = h.shape
    nb, _, fb = da.shape
    tk = _tile(n, 512, 16)
    nk = n // tk

    def body(ins, outs, scr):
        h_ref, dx_ref, da_ref, db_ref, act_ref = ins
        dwg_ref, dwu_ref, dwd_ref = outs
        accg, accu, accd = scr
        k = pl.program_id(1)

        @pl.when(k == 0)
        def _():
            accg[...] = jnp.zeros_like(accg)
            accu[...] = jnp.zeros_like(accu)
            accd[...] = jnp.zeros_like(accd)

        hh = h_ref[...]
        accg[...] += _dot(hh, da_ref[...], _TN)
        accu[...] += _dot(hh, db_ref[...], _TN)
        accd[...] += _dot(act_ref[...], dx_ref[...], _TN)

        @pl.when(k == nk - 1)
        def _():
            dwg_ref[...] = accg[...].astype(BF16)
            dwu_ref[...] = accu[...].astype(BF16)
            dwd_ref[...] = (0.5 * accd[...]).astype(BF16)

    row = pl.BlockSpec((tk, d), lambda j, k: (k, 0))
    hid = pl.BlockSpec((None, tk, fb), lambda j, k: (j, k, 0))
    w_in = pl.BlockSpec((None, d, fb), lambda j, k: (j, 0, 0))
    w_out = pl.BlockSpec((None, fb, d), lambda j, k: (j, 0, 0))
    return _hosted_call(
        name, body, (nb, nk), [row, row, hid, hid, hid], [w_in, w_in, w_out],
        [jax.ShapeDtypeStruct((nb, d, fb), BF16), jax.ShapeDtypeStruct((nb, d, fb), BF16),
         jax.ShapeDtypeStruct((nb, fb, d), BF16)],
        [pltpu.VMEM((d, fb), F32), pltpu.VMEM((d, fb), F32), pltpu.VMEM((fb, d), F32)],
        (h, dxb, da, db, act), exchange)


def _pool_diff(ext_ref, rows, cols, width, t_idx):
    s = ext_ref[POOL_HALO:POOL_HALO + rows, cols]
    for sft in range(1, width):
        s = s + ext_ref[POOL_HALO - sft:POOL_HALO - sft + rows, cols]
    cnt = jnp.minimum(t_idx + 1, width).astype(F32)
    return s / cnt - ext_ref[POOL_HALO:POOL_HALO + rows, cols]


def _pool_fwd(proj, pool_w, pool_scale):
    n = proj.shape[0]
    dp = pool_scale.shape[1]
    c = dp // len(POOL_WINDOWS)
    tm = _tile(n, 512, POOL_HALO)
    hb = tm // POOL_HALO

    def body(u_ref, halo_ref, pw_ref, sc_ref, y_ref, ext):
        i = pl.program_id(0)
        ext[0:POOL_HALO, :] = jnp.where(i > 0, halo_ref[...], 0.0)
        ext[POOL_HALO:, :] = u_ref[...]
        t_idx = i * tm + lax.broadcasted_iota(jnp.int32, (tm, 1), 0)
        for g, width in enumerate(POOL_WINDOWS):
            cols = slice(g * c, (g + 1) * c)
            dgrp = _pool_diff(ext, tm, cols, width, t_idx)
            y_ref[:, cols] = (_dot(dgrp, pw_ref[g]) * sc_ref[:, cols]).astype(BF16)

    return pl.pallas_call(
        body, name="pool_fwd", grid=(n // tm,),
        in_specs=[pl.BlockSpec((tm, dp), lambda i: (i, 0)),
                  pl.BlockSpec((POOL_HALO, dp), lambda i: (jnp.maximum(i * hb - 1, 0), 0)),
                  pl.BlockSpec((len(POOL_WINDOWS), c, c), lambda i: (0, 0, 0)),
                  pl.BlockSpec((1, dp), lambda i: (0, 0))],
        out_specs=pl.BlockSpec((tm, dp), lambda i: (i, 0)),
        out_shape=jax.ShapeDtypeStruct((n, dp), BF16),
        scratch_shapes=[pltpu.VMEM((tm + POOL_HALO, dp), F32)],
        compiler_params=_cparams(("parallel",)),
    )(proj, proj, pool_w, pool_scale)


def _pool_bwd(proj, dymix, pool_w, pool_scale):
    n = proj.shape[0]
    dp = pool_scale.shape[1]
    ng = len(POOL_WINDOWS)
    c = dp // ng
    tm = _tile(n, 512, POOL_HALO)
    hb = tm // POOL_HALO
    nsteps = n // tm
    last_halo = n // POOL_HALO - 1

    def body(u_ref, halo_ref, dy_ref, dyn_ref, pw_ref, sc_ref, du_ref, dpw_ref, dsc_ref, ext, dyext, e_s):
        i = pl.program_id(0)

        @pl.when(i == 0)
        def _():
            dpw_ref[...] = jnp.zeros_like(dpw_ref)
            dsc_ref[...] = jnp.zeros_like(dsc_ref)

        ext[0:POOL_HALO, :] = jnp.where(i > 0, halo_ref[...], 0.0)
        ext[POOL_HALO:, :] = u_ref[...]
        dyext[0:tm, :] = dy_ref[...]
        dyext[tm:, :] = jnp.where(i < nsteps - 1, dyn_ref[...], 0.0)
        t_idx = i * tm + lax.broadcasted_iota(jnp.int32, (tm, 1), 0)
        te_idx = i * tm + lax.broadcasted_iota(jnp.int32, (tm + POOL_HALO, 1), 0)
        for g, width in enumerate(POOL_WINDOWS):
            cols = slice(g * c, (g + 1) * c)
            dgrp = _pool_diff(ext, tm, cols, width, t_idx)
            w_g = pw_ref[g]
            dys = dyext[:, cols] * sc_ref[:, cols]
            ypre = _dot(dgrp, w_g)
            dsc_ref[:, cols] += jnp.sum(dyext[0:tm, cols] * ypre, axis=0, keepdims=True)
            dpw_ref[g] += _dot(dgrp, dys[0:tm], _TN)
            dd = _dot(dys, w_g, _NT)
            e_s[...] = dd / jnp.minimum(te_idx + 1, width).astype(F32)
            acc = e_s[0:tm, :]
            for sft in range(1, width):
                acc = acc + e_s[sft:sft + tm, :]
            du_ref[:, cols] = (acc - dd[0:tm]).astype(BF16)

    return pl.pallas_call(
        body, name="pool_bwd", grid=(nsteps,),
        in_specs=[pl.BlockSpec((tm, dp), lambda i: (i, 0)),
                  pl.BlockSpec((POOL_HALO, dp), lambda i: (jnp.maximum(i * hb - 1, 0), 0)),
                  pl.BlockSpec((tm, dp), lambda i: (i, 0)),
                  pl.BlockSpec((POOL_HALO, dp), lambda i: (jnp.minimum((i + 1) * hb, last_halo), 0)),
                  pl.BlockSpec((ng, c, c), lambda i: (0, 0, 0)),
                  pl.BlockSpec((1, dp), lambda i: (0, 0))],
        out_specs=[pl.BlockSpec((tm, dp), lambda i: (i, 0)),
                   pl.BlockSpec((ng, c, c), lambda i: (0, 0, 0)),
                   pl.BlockSpec((1, dp), lambda i: (0, 0))],
        out_shape=[jax.ShapeDtypeStruct((n, dp), BF16), jax.ShapeDtypeStruct((ng, c, c), F32),
                   jax.ShapeDtypeStruct((1, dp), F32)],
        scratch_shapes=[pltpu.VMEM((tm + POOL_HALO, dp), F32), pltpu.VMEM((tm + POOL_HALO, dp), F32),
                        pltpu.VMEM((tm + POOL_HALO, c), F32)],
        compiler_params=_cparams(("arbitrary",)),
    )(proj, proj, dymix, dymix, pool_w, pool_scale)


def _chunk_cumsum(x):
    row = lax.broadcasted_iota(jnp.int32, x.shape, 0) % CHUNK
    s = 1
    while s < CHUNK:
        x = x + jnp.where(row >= s, pltpu.roll(x, s, 0), 0.0)
        s *= 2
    return x


def _chunk_ends(cum_ref, bend_ref, nc):
    for c in range(nc):
        last = cum_ref[c * CHUNK + CHUNK - 1:(c + 1) * CHUNK, :]
        bend_ref[c * CHUNK:(c + 1) * CHUNK, :] = jnp.broadcast_to(last, (CHUNK, last.shape[1]))


def _gate_logits(alr_ref, wa_ref, ba_ref):
    z = _dot(alr_ref[...], wa_ref[...]) + ba_ref[...]
    la = (jnp.minimum(z, 0.0) - jnp.log(1.0 + jnp.exp(-jnp.abs(z)))) * (1.0 / GATE_TEMP)
    return z, la


def _gla_dims(proj, head_norm):
    n, pw = proj.shape
    dv4 = head_norm.shape[1]
    dk4 = dv4 // 2
    return n, pw, dv4, dk4, dk4 // HEADS, dv4 // HEADS


def _gla_in_specs(t, dk4, dv4, rev):
    alr_blk = (2 * dv4 + 2 * dv4) // LANE
    return [pl.BlockSpec((t, dk4), lambda i: (rev(i), 2)),
            pl.BlockSpec((t, dk4), lambda i: (rev(i), 3)),
            pl.BlockSpec((t, dv4), lambda i: (rev(i), 2)),
            pl.BlockSpec((t, dv4), lambda i: (rev(i), 3)),
            pl.BlockSpec((t, LANE), lambda i: (rev(i), alr_blk))]


def _gla_fwd(proj, wa2p, b_a, head_norm):
    n, _, dv4, dk4, dk, dv = _gla_dims(proj, head_norm)
    t = _tile(n, GLA_ROWS, CHUNK)
    nc = t // CHUNK
    qscale = dk ** -0.5

    def body(q_ref, k_ref, v_ref, g_ref, alr_ref, wa_ref, ba_ref, hn_ref, y_ref, st_ref,
             state, cum_s, bend_s, o_s):
        i = pl.program_id(0)

        @pl.when(i == 0)
        def _():
            state[...] = jnp.zeros_like(state)

        _, la = _gate_logits(alr_ref, wa_ref, ba_ref)
        cum_s[...] = _chunk_cumsum(la)
        _chunk_ends(cum_s, bend_s, nc)
        kd = (k_ref[...] * jnp.exp(bend_s[...] - cum_s[...])).astype(BF16)
        qs = (q_ref[...] * qscale).astype(BF16)
        vv = v_ref[...].astype(BF16)
        units = [(c, h) for c in range(nc) for h in range(HEADS)]
        for c, h in units:
            rows = slice(c * CHUNK, (c + 1) * CHUNK)
            st_ref[c, h] = _dot(vv[rows, h * dv:(h + 1) * dv], kd[rows, h * dk:(h + 1) * dk], _TN)
        for c, h in units:
            e = jnp.exp(bend_s[c * CHUNK:c * CHUNK + 1, h * dk:(h + 1) * dk])
            s_new = state[h] * e + st_ref[c, h]
            state[h] = s_new
            st_ref[c, h] = s_new
        for c, h in units:
            rows = slice(c * CHUNK, (c + 1) * CHUNK)
            o_s[rows, h * dv:(h + 1) * dv] = _dot(qs[rows, h * dk:(h + 1) * dk], st_ref[c, h], _NT)
        for h in range(HEADS):
            hv = slice(h * dv, (h + 1) * dv)
            o = o_s[:, hv]
            gg = g_ref[:, hv]
            y_ref[:, hv] = (((o * _rms_scale(o)) * hn_ref[:, hv]) * (gg * jax.nn.sigmoid(gg))).astype(BF16)

    full = lambda shape: pl.BlockSpec(shape, lambda i: tuple(0 for _ in shape))
    return pl.pallas_call(
        body, name="gla_fwd", grid=(n // t,),
        in_specs=_gla_in_specs(t, dk4, dv4, lambda i: i)
        + [full((LANE, dk4)), full((1, dk4)), full((1, dv4))],
        out_specs=[pl.BlockSpec((t, dv4), lambda i: (i, 0)),
                   pl.BlockSpec((nc, HEADS, dv, dk), lambda i: (i, 0, 0, 0))],
        out_shape=[jax.ShapeDtypeStruct((n, dv4), BF16),
                   jax.ShapeDtypeStruct((n // CHUNK, HEADS, dv, dk), F32)],
        scratch_shapes=[pltpu.VMEM((HEADS, dv, dk), F32), pltpu.VMEM((t, dk4), F32),
                        pltpu.VMEM((t, dk4), F32), pltpu.VMEM((t, dv4), F32)],
        compiler_params=_cparams(("arbitrary",)),
    )(proj, proj, proj, proj, proj, wa2p, b_a, head_norm)


def _gla_bwd(proj, dymix, du, states, wa2p, b_a, head_norm):
    n, pw, dv4, dk4, dk, dv = _gla_dims(proj, head_norm)
    t = _tile(n, GLA_ROWS, CHUNK)
    nc = t // CHUNK
    nsteps = n // t
    qscale = dk ** -0.5
    rev = lambda i: nsteps - 1 - i

    def body(q_ref, k_ref, v_ref, g_ref, alr_ref, dy_ref, du_ref, st_ref, prev_ref, wa_ref, ba_ref, hn_ref,
             dp_ref, dwa_ref, dba_ref, dhn_ref, carry, cum_s, bend_s, gst_s, dkd_s, dee_s, o_s, do_s):
        i = pl.program_id(0)

        @pl.when(i == 0)
        def _():
            carry[...] = jnp.zeros_like(carry)
            dwa_ref[...] = jnp.zeros_like(dwa_ref)
            dba_ref[...] = jnp.zeros_like(dba_ref)
            dhn_ref[...] = jnp.zeros_like(dhn_ref)

        first_step = i == nsteps - 1
        z, la = _gate_logits(alr_ref, wa_ref, ba_ref)
        cum_s[...] = _chunk_cumsum(la)
        _chunk_ends(cum_s, bend_s, nc)
        dec = jnp.exp(bend_s[...] - cum_s[...])
        kd_f = k_ref[...] * dec
        kd = kd_f.astype(BF16)
        qs = (q_ref[...] * qscale).astype(BF16)
        vv = v_ref[...].astype(BF16)
        dp_ref[:, 0:dv4] = du_ref[...]
        units = [(c, h) for c in range(nc) for h in range(HEADS)]

        for c, h in units:
            rows = slice(c * CHUNK, (c + 1) * CHUNK)
            o_s[rows, h * dv:(h + 1) * dv] = _dot(qs[rows, h * dk:(h + 1) * dk], st_ref[c, h], _NT)
        for h in range(HEADS):
            hv = slice(h * dv, (h + 1) * dv)
            o = o_s[:, hv]
            r = _rms_scale(o)
            oh = o * r
            sl, dsl = _silu_parts(g_ref[:, hv])
            dyh = dy_ref[:, hv]
            hn = hn_ref[:, hv]
            tt = dyh * sl
            dhn_ref[:, hv] += jnp.sum(tt * oh, axis=0, keepdims=True)
            dp_ref[:, 3 * dv4 + h * dv:3 * dv4 + (h + 1) * dv] = (dyh * (oh * hn) * dsl).astype(BF16)
            tt = tt * hn
            do_s[:, hv] = (r * (tt - oh * jnp.mean(tt * oh, axis=-1, keepdims=True))).astype(BF16)
        for c, h in units:
            rows = slice(c * CHUNK, (c + 1) * CHUNK)
            hk = slice(h * dk, (h + 1) * dk)
            do = do_s[rows, h * dv:(h + 1) * dv]
            gst_s[c, h] = _dot(do, qs[rows, hk], _TN)
            dp_ref[rows, dv4 + h * dk:dv4 + (h + 1) * dk] = (_dot(do, st_ref[c, h]) * qscale).astype(BF16)

        for c, h in reversed(units):
            g_n = carry[h] + gst_s[c, h]
            gst_s[c, h] = g_n
            carry[h] = g_n * jnp.exp(bend_s[c * CHUNK:c * CHUNK + 1, h * dk:(h + 1) * dk])

        for c, h in units:
            rows = slice(c * CHUNK, (c + 1) * CHUNK)
            hk = slice(h * dk, (h + 1) * dk)
            hv = slice(h * dv, (h + 1) * dv)
            g_n = gst_s[c, h]
            if c > 0:
                s_prev = st_ref[c - 1, h]
            else:
                s_prev = jnp.where(first_step, 0.0, prev_ref[0, h])
            dkd_s[rows, hk] = _dot(vv[rows, hv], g_n)
            dp_ref[rows, 2 * dv4 + h * dv:2 * dv4 + (h + 1) * dv] = _dot(kd[rows, hk], g_n, _NT).astype(BF16)
            dee = jnp.sum(g_n * s_prev, axis=0, keepdims=True) * jnp.exp(bend_s[c * CHUNK:c * CHUNK + 1, hk])
            dee_s[rows, hk] = jnp.broadcast_to(dee, (CHUNK, dk))

        dkd = dkd_s[...]
        dp_ref[:, dv4 + dk4:dv4 + 2 * dk4] = (dkd * dec).astype(BF16)
        w = dkd * kd_f
        dla = (_chunk_cumsum(w) - w) + dee_s[...]
        dz = dla * (1.0 / GATE_TEMP) * (1.0 - jax.nn.sigmoid(z))
        dp_ref[:, 4 * dv4:4 * dv4 + LANE] = _dot(dz, wa_ref[...], _NT).astype(BF16)
        dwa_ref[...] += _dot(alr_ref[...], dz, _TN)
        dba_ref[...] += jnp.sum(dz, axis=0, keepdims=True)

    full = lambda shape: pl.BlockSpec(shape, lambda i: tuple(0 for _ in shape))
    return pl.pallas_call(
        body, name="gla_bwd", grid=(nsteps,),
        in_specs=_gla_in_specs(t, dk4, dv4, rev)
        + [pl.BlockSpec((t, dv4), lambda i: (rev(i), 1)),
           pl.BlockSpec((t, dv4), lambda i: (rev(i), 0)),
           pl.BlockSpec((nc, HEADS, dv, dk), lambda i: (rev(i), 0, 0, 0)),
           pl.BlockSpec((1, HEADS, dv, dk), lambda i: (jnp.maximum(rev(i) * nc - 1, 0), 0, 0, 0)),
           full((LANE, dk4)), full((1, dk4)), full((1, dv4))],
        out_specs=[pl.BlockSpec((t, pw), lambda i: (rev(i), 0)),
                   full((LANE, dk4)), full((1, dk4)), full((1, dv4))],
        out_shape=[jax.ShapeDtypeStruct((n, pw), BF16), jax.ShapeDtypeStruct((LANE, dk4), F32),
                   jax.ShapeDtypeStruct((1, dk4), F32), jax.ShapeDtypeStruct((1, dv4), F32)],
        scratch_shapes=[pltpu.VMEM((HEADS, dv, dk), F32), pltpu.VMEM((t, dk4), F32),
                        pltpu.VMEM((t, dk4), F32), pltpu.VMEM((nc, HEADS, dv, dk), F32),
                        pltpu.VMEM((t, dk4), F32), pltpu.VMEM((t, dk4), F32),
                        pltpu.VMEM((t, dv4), F32), pltpu.VMEM((t, dv4), BF16)],
        compiler_params=_cparams(("arbitrary",)),
    )(proj, proj, proj, proj, proj, dymix, du, states, states, wa2p, b_a, head_norm)


def _softmax_rows(q, k, scale):
    s = _dot(q, k, _NT) * scale
    p = jnp.exp(s - jnp.max(s, axis=-1, keepdims=True))
    return p / jnp.sum(p, axis=-1, keepdims=True)


def _attn_fwd(q, kv):
    n, d = q.shape
    m = kv.shape[0]
    dh = d // HEADS
    tm = _tile(n, 512, 16)
    scale = dh ** -0.5

    def body(q_ref, k_ref, v_ref, o_ref):
        for h in range(HEADS):
            hs = slice(h * dh, (h + 1) * dh)
            p = _softmax_rows(q_ref[:, hs], k_ref[:, hs], scale)
            o_ref[:, hs] = _dot(p, v_ref[:, hs]).astype(BF16)

    return pl.pallas_call(
        body, name="attn_fwd", grid=(n // tm,),
        in_specs=[pl.BlockSpec((tm, d), lambda i: (i, 0)), pl.BlockSpec((m, d), lambda i: (0, 0)),
                  pl.BlockSpec((m, d), lambda i: (0, 1))],
        out_specs=pl.BlockSpec((tm, d), lambda i: (i, 0)),
        out_shape=jax.ShapeDtypeStruct((n, d), BF16),
        compiler_params=_cparams(("parallel",)),
    )(q, kv, kv)


def _attn_bwd(q, kv, do):
    n, d = q.shape
    m = kv.shape[0]
    dh = d // HEADS
    tm = _tile(n, 512, 16)
    scale = dh ** -0.5

    def body(q_ref, k_ref, v_ref, do_ref, dq_ref, dk_ref, dv_ref):
        i = pl.program_id(0)

        @pl.when(i == 0)
        def _():
            dk_ref[...] = jnp.zeros_like(dk_ref)
            dv_ref[...] = jnp.zeros_like(dv_ref)

        for h in range(HEADS):
            hs = slice(h * dh, (h + 1) * dh)
            qh, kh, vh, doh = q_ref[:, hs], k_ref[:, hs], v_ref[:, hs], do_ref[:, hs]
            p = _softmax_rows(qh, kh, scale)
            dv_ref[:, hs] += _dot(p, doh, _TN)
            dp = _dot(doh, vh, _NT)
            ds = p * (dp - jnp.sum(dp * p, axis=-1, keepdims=True)) * scale
            dq_ref[:, hs] = _dot(ds, kh).astype(BF16)
            dk_ref[:, hs] += _dot(ds, qh, _TN)

    row = pl.BlockSpec((tm, d), lambda i: (i, 0))
    memb = pl.BlockSpec((m, d), lambda i: (0, 0))
    return pl.pallas_call(
        body, name="attn_bwd", grid=(n // tm,),
        in_specs=[row, memb, pl.BlockSpec((m, d), lambda i: (0, 1)), row],
        out_specs=[row, memb, memb],
        out_shape=[jax.ShapeDtypeStruct((n, d), BF16), jax.ShapeDtypeStruct((m, d), F32),
                   jax.ShapeDtypeStruct((m, d), F32)],
        compiler_params=_cparams(("arbitrary",)),
    )(q, kv, kv, do)


def _adamw(name, w, m, v, parts, own_block):
    r, c = w.shape
    tr = _tile(r, max(16, ELEMENTWISE_BLOCK // c), 16)
    row = pl.BlockSpec((tr, c), lambda i, o: (i, 0))
    ops, specs = [w, m, v], [row, row, row]
    for p in parts:
        if p.ndim == 2:
            ops.append(p)
            specs.append(row)
        elif p.shape[0] == 4:
            ops.append(p)
            specs.append(pl.BlockSpec((None, tr, c), lambda i, o: (o[0], i, 0)))
        else:
            for s in range(p.shape[0]):
                ops.append(p)
                specs.append(pl.BlockSpec((None, tr, c), lambda i, o, s=s: (s, i, 0)))
    n_parts = len(ops) - 3
    c1 = 1.0 - ADAM_B1 ** ADAM_STEP
    c2 = 1.0 - ADAM_B2 ** ADAM_STEP

    def body(o_ref, *refs):
        del o_ref
        w_ref, m_ref, v_ref = refs[:3]
        g_refs = refs[3:3 + n_parts]
        go_ref, d_ref, mo_ref, vo_ref = refs[3 + n_parts:]
        g = g_refs[0][...].astype(F32)
        for g_ref in g_refs[1:]:
            g = g + g_ref[...].astype(F32)
        m_new = ADAM_B1 * m_ref[...] + (1.0 - ADAM_B1) * g
        v_new = ADAM_B2 * v_ref[...] + (1.0 - ADAM_B2) * (g * g)
        m_hat = m_new / c1
        v_hat = v_new / c2
        go_ref[...] = g
        d_ref[...] = -ADAM_LR * (m_hat / (jnp.sqrt(v_hat) + ADAM_EPS) + ADAM_WD * w_ref[...])
        mo_ref[...] = m_new
        vo_ref[...] = v_new

    shp = jax.ShapeDtypeStruct((r, c), F32)
    return pl.pallas_call(
        body, name=name,
        grid_spec=pltpu.PrefetchScalarGridSpec(
            num_scalar_prefetch=1, grid=(r // tr,), in_specs=specs, out_specs=[row] * 4),
        out_shape=[shp] * 4,
        compiler_params=_cparams(("parallel",)),
    )(own_block, *ops)


def _pair_add(name, g, recvd, core):
    _, r, c = g.shape
    tr = _tile(r, max(16, ELEMENTWISE_BLOCK // c), 16)

    def body(core_ref, a_ref, b_ref, o_ref):
        del core_ref
        o_ref[...] = (a_ref[...].astype(F32) + b_ref[...].astype(F32)).astype(o_ref.dtype)

    blk = pl.BlockSpec((None, tr, c), lambda s, i, core_ref: (s, i, 0))
    mine = pl.BlockSpec((None, tr, c), lambda s, i, core_ref: (2 * s + core_ref[0], i, 0))
    return pl.pallas_call(
        body, name=name,
        grid_spec=pltpu.PrefetchScalarGridSpec(
            num_scalar_prefetch=1, grid=(4, r // tr), in_specs=[mine, blk], out_specs=blk),
        out_shape=jax.ShapeDtypeStruct(recvd.shape, g.dtype),
        compiler_params=_cparams(("parallel", "parallel")),
    )(core, g, recvd)


def _position():
    return lax.axis_index("x"), lax.axis_index("y"), lax.axis_index("c")


_HBM = pl.BlockSpec(memory_space=pltpu.HBM)

_Exchange = collections.namedtuple("_Exchange", "ins out_shape sems start finish in_place", defaults=(False,))
_NO_EXCHANGE = _Exchange((), (), (), lambda ins, outs, sems: None, lambda ins, outs, sems: None)


def _all_gather(name, shards):
    n = len(shards)

    def body(*refs):
        ins, outs = refs[:n], refs[n:2 * n]
        send_sems, recv_sems, local_sems = refs[2 * n:]
        x, y, c = _position()
        me, sibling = (x, y, c), (x, y, 1 - c)
        chips = [(1 - x, y), (x, 1 - y), (1 - x, 1 - y)]

        def block(ref, px, py, pc):
            return ref.at[4 * px + 2 * py + pc]

        def copy(a, k, owner, to, src=None):
            return pltpu.make_async_remote_copy(
                src_ref=block(outs[a], *owner) if src is None else src, dst_ref=block(outs[a], *owner),
                send_sem=send_sems.at[7 * a + k], recv_sem=recv_sems.at[7 * a + k],
                device_id=to, device_id_type=MESH)

        started = []
        for a in range(n):
            mine = pltpu.make_async_copy(ins[a], block(outs[a], *me), local_sems.at[a])
            mine.start()
            started.append(mine)
        sends = []
        for a in range(n):
            sends.append(copy(a, 0, me, sibling, src=ins[a]))
            sends += [copy(a, 1 + j, me, (*chip, c), src=ins[a]) for j, chip in enumerate(chips)]
        for cp in sends:
            cp.start()
        for j, chip in enumerate(chips):
            for a in range(n):
                copy(a, 1 + j, (*chip, c), me).wait_recv()
                fwd = copy(a, 4 + j, (*chip, c), sibling)
                fwd.start()
                sends.append(fwd)
        for a in range(n):
            copy(a, 0, sibling, me).wait_recv()
            for j, chip in enumerate(chips):
                copy(a, 4 + j, (*chip, 1 - c), me).wait_recv()
        for cp in sends:
            cp.wait_send()
        for mine in started:
            mine.wait()

    return pl.pallas_call(
        body, name=name,
        in_specs=[_HBM] * n, out_specs=[_HBM] * n,
        out_shape=[jax.ShapeDtypeStruct((N_DEV,) + s.shape, s.dtype) for s in shards],
        scratch_shapes=[pltpu.SemaphoreType.DMA((7 * n,)), pltpu.SemaphoreType.DMA((7 * n,)),
                        pltpu.SemaphoreType.DMA((n,))],
    )(*shards)


def _pair_exchange(grads):
    n = len(grads)

    def start(ins, recvd, sems):
        send_sems, recv_sems = sems
        x, y, c = _position()
        for a in range(n):
            for chip in range(4):
                pltpu.make_async_remote_copy(
                    src_ref=ins[a].at[2 * chip + (1 - c)], dst_ref=recvd[a].at[chip],
                    send_sem=send_sems.at[a], recv_sem=recv_sems.at[a],
                    device_id=(x, y, 1 - c), device_id_type=MESH).start()

    def finish(ins, recvd, sems):
        del ins
        send_sems, recv_sems = sems
        x, y, c = _position()
        for a in range(n):
            pltpu.make_async_remote_copy(
                src_ref=recvd[a], dst_ref=recvd[a], send_sem=send_sems.at[a], recv_sem=recv_sems.at[a],
                device_id=(x, y, 1 - c), device_id_type=MESH).wait()

    return _Exchange(
        ins=tuple(grads),
        out_shape=tuple(jax.ShapeDtypeStruct((4,) + g.shape[1:], g.dtype) for g in grads),
        sems=(pltpu.SemaphoreType.DMA((n,)), pltpu.SemaphoreType.DMA((n,))),
        start=start, finish=finish)


def _run_exchange(name, exchange):
    n_in, n_out = len(exchange.ins), len(exchange.out_shape)

    def body(*refs):
        ins, outs, sems = refs[:n_in], refs[n_in:n_in + n_out], refs[n_in + n_out:]
        exchange.start(ins, outs, sems)
        exchange.finish(ins, outs, sems)

    return pl.pallas_call(
        body, name=name, in_specs=[_HBM] * n_in, out_specs=[_HBM] * n_out,
        out_shape=list(exchange.out_shape), scratch_shapes=list(exchange.sems),
        input_output_aliases={k: k for k in range(n_in)} if exchange.in_place else {},
    )(*exchange.ins)


def _chip_scatter_exchange(sums):
    n = len(sums)
    offsets = [(1, 0), (0, 1), (1, 1)]

    def start(ins, recvd, sems):
        send_sems, recv_sems = sems
        x, y, c = _position()
        for a in range(n):
            for r, (ox, oy) in enumerate(offsets):
                px = 1 - x if ox else x
                py = 1 - y if oy else y
                pltpu.make_async_remote_copy(
                    src_ref=ins[a].at[2 * px + py], dst_ref=recvd[a].at[r],
                    send_sem=send_sems.at[a], recv_sem=recv_sems.at[a],
                    device_id=(px, py, c), device_id_type=MESH).start()

    def finish(ins, recvd, sems):
        del ins
        send_sems, recv_sems = sems
        x, y, c = _position()
        for a in range(n):
            pltpu.make_async_remote_copy(
                src_ref=recvd[a], dst_ref=recvd[a], send_sem=send_sems.at[a], recv_sem=recv_sems.at[a],
                device_id=(x, y, c), device_id_type=MESH).wait()

    return _Exchange(
        ins=tuple(sums),
        out_shape=tuple(jax.ShapeDtypeStruct((3,) + s.shape[1:], s.dtype) for s in sums),
        sems=(pltpu.SemaphoreType.DMA((n,)), pltpu.SemaphoreType.DMA((n,))),
        start=start, finish=finish)


def _gather_spread_exchange(shards):
    n = len(shards)

    def peers():
        x, y, c = _position()
        return (x, y, c), [(x, y, 1 - c), (1 - x, y, c), (x, 1 - y, c), (1 - x, 1 - y, c)]

    def block(ref, px, py, pc):
        return ref.at[4 * px + 2 * py + pc]

    def copy(ins, outs, sems, a, k, owner, to):
        send_sems, recv_sems, _ = sems
        return pltpu.make_async_remote_copy(
            src_ref=ins[a], dst_ref=block(outs[a], *owner),
            send_sem=send_sems.at[4 * a + k], recv_sem=recv_sems.at[4 * a + k],
            device_id=to, device_id_type=MESH)

    def start(ins, outs, sems):
        me, others = peers()
        for a in range(n):
            pltpu.make_async_copy(ins[a], block(outs[a], *me), sems[2].at[a]).start()
            for k, to in enumerate(others):
                copy(ins, outs, sems, a, k, me, to).start()

    def finish(ins, outs, sems):
        me, others = peers()
        for a in range(n):
            for k, peer in enumerate(others):
                cp = copy(ins, outs, sems, a, k, peer, peer)
                cp.wait_recv()
                cp.wait_send()
            pltpu.make_async_copy(ins[a], block(outs[a], *me), sems[2].at[a]).wait()

    return _Exchange(
        ins=tuple(shards),
        out_shape=tuple(jax.ShapeDtypeStruct((N_DEV,) + s.shape, s.dtype) for s in shards),
        sems=(pltpu.SemaphoreType.DMA((4 * n,)), pltpu.SemaphoreType.DMA((4 * n,)),
              pltpu.SemaphoreType.DMA((n,))),
        start=start, finish=finish)


def _gather_forward_exchange(partial):
    n = len(partial)

    def copy(ins, outs, sems, a, j, pc):
        x, y, c = _position()
        chips = [(1 - x, y), (x, 1 - y), (1 - x, 1 - y)]
        blk = 4 * chips[j][0] + 2 * chips[j][1] + pc
        return pltpu.make_async_remote_copy(
            src_ref=ins[a].at[blk], dst_ref=outs[a].at[blk],
            send_sem=sems[0].at[3 * a + j], recv_sem=sems[1].at[3 * a + j],
            device_id=(x, y, 1 - c), device_id_type=MESH)

    def start(ins, outs, sems):
        c = lax.axis_index("c")
        for a in range(n):
            for j in range(3):
                copy(ins, outs, sems, a, j, c).start()

    def finish(ins, outs, sems):
        c = lax.axis_index("c")
        for a in range(n):
            for j in range(3):
                copy(ins, outs, sems, a, j, 1 - c).wait_recv()
                copy(ins, outs, sems, a, j, c).wait_send()

    return _Exchange(
        ins=tuple(partial), out_shape=tuple(jax.ShapeDtypeStruct(p.shape, p.dtype) for p in partial),
        sems=(pltpu.SemaphoreType.DMA((3 * n,)), pltpu.SemaphoreType.DMA((3 * n,))),
        start=start, finish=finish, in_place=True)


def _all_reduce_small(vec):
    r = vec.shape[0]

    def body(v_ref, o_ref, gbuf, send_sems, recv_sems):
        x, y, c = _position()
        me = 4 * x + 2 * y + c
        gbuf[me] = v_ref[...]
        copies = []
        for k in range(1, N_DEV):
            ox, oy, oc = (k >> 2) & 1, (k >> 1) & 1, k & 1
            peer = (1 - x if ox else x, 1 - y if oy else y, 1 - c if oc else c)
            cp = pltpu.make_async_remote_copy(
                src_ref=gbuf.at[me], dst_ref=gbuf.at[me], send_sem=send_sems.at[k - 1],
                recv_sem=recv_sems.at[k - 1], device_id=peer, device_id_type=MESH)
            cp.start()
            copies.append(cp)
        for cp in copies:
            cp.wait()
        total = gbuf[0]
        for j in range(1, N_DEV):
            total = total + gbuf[j]
        o_ref[...] = total

    return pl.pallas_call(
        body, name="all_reduce_small",
        in_specs=[pl.BlockSpec(memory_space=pltpu.VMEM)],
        out_specs=pl.BlockSpec(memory_space=pltpu.VMEM),
        out_shape=jax.ShapeDtypeStruct(vec.shape, F32),
        scratch_shapes=[pltpu.VMEM((N_DEV, r, LANE), F32), pltpu.SemaphoreType.DMA((N_DEV - 1,)),
                        pltpu.SemaphoreType.DMA((N_DEV - 1,))],
    )(vec)


def _cols_from_blocks(g):
    nb, r, cs = g.shape
    return jnp.transpose(g, (1, 0, 2)).reshape(r, nb * cs)


def _cols_to_blocks(w):
    r, cfull = w.shape
    return jnp.transpose(w.reshape(r, N_DEV, cfull // N_DEV), (1, 0, 2))


def kernel(x, mem, ffn1_norm, ffn1_w_gate, ffn1_w_up, ffn1_w_down, mix_norm, w_in, pool_w, pool_scale, gla_w_a2, gla_b_a, gla_head_norm, w_out, xattn_norm, mem_norm, xattn_w_q, xattn_w_kv, xattn_w_o, ffn2_norm, ffn2_w_gate, ffn2_w_up, ffn2_w_down, final_norm, loss_target, m_ffn1_norm, m_ffn1_w_gate, m_ffn1_w_up, m_ffn1_w_down, m_mix_norm, m_w_in, m_pool_w, m_pool_scale, m_gla_w_a2, m_gla_b_a, m_gla_head_norm, m_w_out, m_xattn_norm, m_mem_norm, m_xattn_w_q, m_xattn_w_kv, m_xattn_w_o, m_ffn2_norm, m_ffn2_w_gate, m_ffn2_w_up, m_ffn2_w_down, m_final_norm, v_ffn1_norm, v_ffn1_w_gate, v_ffn1_w_up, v_ffn1_w_down, v_mix_norm, v_w_in, v_pool_w, v_pool_scale, v_gla_w_a2, v_gla_b_a, v_gla_head_norm, v_w_out, v_xattn_norm, v_mem_norm, v_xattn_w_q, v_xattn_w_kv, v_xattn_w_o, v_ffn2_norm, v_ffn2_w_gate, v_ffn2_w_up, v_ffn2_w_down, v_final_norm):
    weights = dict(ffn1_norm=ffn1_norm, ffn1_w_gate=ffn1_w_gate, ffn1_w_up=ffn1_w_up, ffn1_w_down=ffn1_w_down, mix_norm=mix_norm, w_in=w_in, pool_w=pool_w, pool_scale=pool_scale, gla_w_a2=gla_w_a2, gla_b_a=gla_b_a, gla_head_norm=gla_head_norm, w_out=w_out, xattn_norm=xattn_norm, mem_norm=mem_norm, xattn_w_q=xattn_w_q, xattn_w_kv=xattn_w_kv, xattn_w_o=xattn_w_o, ffn2_norm=ffn2_norm, ffn2_w_gate=ffn2_w_gate, ffn2_w_up=ffn2_w_up, ffn2_w_down=ffn2_w_down, final_norm=final_norm)
    mom1 = dict(ffn1_norm=m_ffn1_norm, ffn1_w_gate=m_ffn1_w_gate, ffn1_w_up=m_ffn1_w_up, ffn1_w_down=m_ffn1_w_down, mix_norm=m_mix_norm, w_in=m_w_in, pool_w=m_pool_w, pool_scale=m_pool_scale, gla_w_a2=m_gla_w_a2, gla_b_a=m_gla_b_a, gla_head_norm=m_gla_head_norm, w_out=m_w_out, xattn_norm=m_xattn_norm, mem_norm=m_mem_norm, xattn_w_q=m_xattn_w_q, xattn_w_kv=m_xattn_w_kv, xattn_w_o=m_xattn_w_o, ffn2_norm=m_ffn2_norm, ffn2_w_gate=m_ffn2_w_gate, ffn2_w_up=m_ffn2_w_up, ffn2_w_down=m_ffn2_w_down, final_norm=m_final_norm)
    mom2 = dict(ffn1_norm=v_ffn1_norm, ffn1_w_gate=v_ffn1_w_gate, ffn1_w_up=v_ffn1_w_up, ffn1_w_down=v_ffn1_w_down, mix_norm=v_mix_norm, w_in=v_w_in, pool_w=v_pool_w, pool_scale=v_pool_scale, gla_w_a2=v_gla_w_a2, gla_b_a=v_gla_b_a, gla_head_norm=v_gla_head_norm, w_out=v_w_out, xattn_norm=v_xattn_norm, mem_norm=v_mem_norm, xattn_w_q=v_xattn_w_q, xattn_w_kv=v_xattn_w_kv, xattn_w_o=v_xattn_w_o, ffn2_norm=v_ffn2_norm, ffn2_w_gate=v_ffn2_w_gate, ffn2_w_up=v_ffn2_w_up, ffn2_w_down=v_ffn2_w_down, final_norm=v_final_norm)
    order = list(weights.keys())

    n, d = x.shape[1], x.shape[2]
    mlen = mem.shape[1]
    x0 = x.reshape(n, d)
    memf = mem.reshape(mlen, d)
    target = loss_target.reshape(n, d)
    dpool = d // 2
    in_cols = w_in.shape[2] * N_DEV
    proj_cols = 2 * d + LANE
    rank = gla_w_a2.shape[1]

    def shard(name):
        return weights[name][0].astype(BF16)

    wg1, wu1, wd1 = _all_gather("ag_ffn1", [shard("ffn1_w_gate"), shard("ffn1_w_up"), shard("ffn1_w_down")])
    later = ["w_in", "pool_w", "gla_w_a2", "w_out", "xattn_w_q", "xattn_w_kv", "xattn_w_o",
             "ffn2_w_gate", "ffn2_w_up", "ffn2_w_down"]
    h1 = _rms_fwd("rms_ffn1", x0, ffn1_norm)
    (x1, a1, b1, h2), spread = _ffn_fwd(
        "ffn1_fwd", x0, h1, wg1, wu1, wd1, mix_norm, _gather_spread_exchange([shard(k) for k in later]))
    win_g, = _run_exchange("ag_forward_w_in", _gather_forward_exchange(spread[:1]))
    winp = jnp.pad(_cols_from_blocks(win_g), ((0, 0), (0, proj_cols - in_cols)))
    proj, (pw_g, wa2_g, wout_g, wq_g, wkv_g, wo_g, wg2, wu2, wd2) = _mm_nn(
        "mix_in", [(h2, winp)], F32, exchange=_gather_forward_exchange(spread[1:]), tn=1408)

    pw = jnp.transpose(pw_g, (1, 0, 2, 3)).reshape(len(POOL_WINDOWS), dpool // 4, dpool // 4)
    wa2p = jnp.pad(_cols_from_blocks(wa2_g), ((0, LANE - rank), (0, 0)))
    wout = wout_g.reshape(d, d)
    wq = wq_g.reshape(d, d)
    wkv = _cols_from_blocks(wkv_g)
    wo = wo_g.reshape(d, d)
    fnorm = final_norm.reshape(1, d)

    ypool = _pool_fwd(proj, pw, pool_scale)
    ygla, states = _gla_fwd(proj, wa2p, gla_b_a, gla_head_norm)
    x2, h3 = _mm_nn("mix_out", [(ypool, wout[:dpool]), (ygla, wout[dpool:])], F32, res=x1,
                    norm_gain=xattn_norm, tk=1024)
    mh = _rms_fwd("rms_mem", memf, mem_norm)
    q = _mm_nn("xattn_q", [(h3, wq)], BF16)
    kv = _mm_nn("xattn_kv", [(mh, wkv)], BF16)
    att = _attn_fwd(q, kv)
    x3, h4 = _mm_nn("xattn_o", [(att, wo)], F32, res=x2, norm_gain=ffn2_norm)
    (x4, a2, b2), _ = _ffn_fwd("ffn2_fwd", x3, h4, wg2, wu2, wd2, None, _NO_EXCHANGE)
    dx4, dx4b, g_final, loss_part = _loss_head(x4, target, fnorm)

    grads = {}
    core = lax.axis_index("c").astype(jnp.int32).reshape(1)
    chip = (2 * lax.axis_index("x") + lax.axis_index("y")).astype(jnp.int32).reshape(1)
    chip_sums = {}

    def pair_blocks(names):
        return [grads[k].reshape(N_DEV, -1, grads[k].shape[-1]) for k in names]

    def pair_sums(names, blocks, recvd):
        for k, g, r in zip(names, blocks, recvd):
            chip_sums[k] = _pair_add("rs_add_" + k, g, r, core)

    ffn2_names = ["ffn2_w_gate", "ffn2_w_up", "ffn2_w_down"]
    (da2, db2, act2), _ = _ffn_bwd_act("ffn2_bwd_act", dx4b, wd2, a2, b2, _NO_EXCHANGE)
    (grads["ffn2_w_gate"], grads["ffn2_w_up"], grads["ffn2_w_down"]), _ = _ffn_bwd_weights(
        "ffn2_bwd_weights", h4, dx4b, da2, db2, act2, _NO_EXCHANGE)
    blocks = pair_blocks(ffn2_names)
    (dx3, dx3b, g_ffn2_norm), recvd = _ffn_bwd_dh(
        "ffn2_bwd_dh", da2, db2, wg2, wu2, x3, ffn2_norm, dx4, _pair_exchange(blocks))
    pair_sums(ffn2_names, blocks, recvd)

    xattn_names = ["xattn_w_q", "xattn_w_kv", "xattn_w_o"]
    datt = _mm_nt("xattn_do", [(dx3b, wo)], BF16)
    grads["xattn_w_o"] = _mm_tn("xattn_dwo", [(att, dx3b)], BF16).reshape(N_DEV, d // N_DEV, d)
    dq, dk, dv = _attn_bwd(q, kv, datt)
    grads["xattn_w_q"] = _mm_tn("xattn_dwq", [(h3, dq)], BF16).reshape(N_DEV, d // N_DEV, d)
    dkv = jnp.concatenate([dk, dv], axis=1)
    dmh = _mm_nt("xattn_dmh", [(dkv, wkv)], F32)
    grads["xattn_w_kv"] = _cols_to_blocks(_mm_tn("xattn_dwkv", [(mh, dkv)], BF16))
    g_mem_norm = _rms_gain_grad("rms_mem_bwd", memf, mem_norm, dmh)
    blocks = pair_blocks(xattn_names)
    (dx2, dx2b, g_xattn_norm), recvd = _mm_nt_rms_bwd(
        "xattn_dh", dq, wq, x2, xattn_norm, dx3, _pair_exchange(blocks), tk=2048)
    pair_sums(xattn_names, blocks, recvd)

    dymix = _mm_nt("mix_dy", [(dx2b, wout)], F32)
    grads["w_out"] = jnp.concatenate(
        [_mm_tn("mix_dwout_pool", [(ypool, dx2b)], BF16), _mm_tn("mix_dwout_gla", [(ygla, dx2b)], BF16)],
        axis=0).reshape(N_DEV, d // N_DEV, d)
    du, g_pool_w, g_pool_scale = _pool_bwd(proj, dymix, pw, pool_scale)
    dproj, g_wa2p, g_b_a, g_head_norm = _gla_bwd(proj, dymix, du, states, wa2p, gla_b_a, gla_head_norm)
    grads["w_in"] = _cols_to_blocks(_mm_tn("mix_dwin", [(h2, dproj)], BF16, tn=1408)[:, :in_cols])
    grads["pool_w"] = jnp.transpose(
        g_pool_w.reshape(len(POOL_WINDOWS), N_DEV, dpool // 4 // N_DEV, dpool // 4), (1, 0, 2, 3))
    grads["gla_w_a2"] = _cols_to_blocks(g_wa2p[:rank])
    mix_names = ["w_in", "pool_w", "gla_w_a2", "w_out"]
    blocks = pair_blocks(mix_names)
    (dx1, dx1b, g_mix_norm), recvd = _mm_nt_rms_bwd(
        "mix_dh", dproj, winp, x1, mix_norm, dx2, _pair_exchange(blocks), tk=1408)
    pair_sums(mix_names, blocks, recvd)

    ffn1_names = ["ffn1_w_gate", "ffn1_w_up", "ffn1_w_down"]
    (da1, db1, act1), recvd = _ffn_bwd_act(
        "ffn1_bwd_act", dx1b, wd1, a1, b1, _chip_scatter_exchange([chip_sums[k] for k in ffn2_names]))
    chip_recvd = dict(zip(ffn2_names, recvd))
    (grads["ffn1_w_gate"], grads["ffn1_w_up"], grads["ffn1_w_down"]), recvd = _ffn_bwd_weights(
        "ffn1_bwd_weights", h1, dx1b, da1, db1, act1,
        _chip_scatter_exchange([chip_sums[k] for k in xattn_names + mix_names]))
    chip_recvd.update(zip(xattn_names + mix_names, recvd))
    blocks = pair_blocks(ffn1_names)
    pair_sums(ffn1_names, blocks, _run_exchange("rs_pair_ffn1", _pair_exchange(blocks)))
    (dx0, _, g_ffn1_norm), recvd = _ffn_bwd_dh(
        "ffn1_bwd_dh", da1, db1, wg1, wu1, x0, ffn1_norm, dx1,
        _chip_scatter_exchange([chip_sums[k] for k in ffn1_names]))
    chip_recvd.update(zip(ffn1_names, recvd))

    small = [("ffn1_norm", g_ffn1_norm), ("mix_norm", g_mix_norm), ("pool_scale", g_pool_scale),
             ("gla_b_a", g_b_a), ("gla_head_norm", g_head_norm), ("xattn_norm", g_xattn_norm),
             ("mem_norm", g_mem_norm), ("ffn2_norm", g_ffn2_norm), ("final_norm", g_final)]
    packed = jnp.concatenate([g.reshape(-1) for _, g in small] + [loss_part.reshape(-1)])
    slab = 8 * LANE
    padded = -(-packed.shape[0] // slab) * slab
    packed = jnp.pad(packed, (0, padded - packed.shape[0])).reshape(padded // LANE, LANE)
    reduced = _all_reduce_small(packed).reshape(-1)
    small_grads = {}
    off = 0
    for name, g in small:
        small_grads[name] = reduced[off:off + g.size]
        off += g.size
    loss = reduced[off]

    out_g, out_d, out_m, out_v = {}, {}, {}, {}
    for k in order:
        w = weights[k]
        if k in small_grads:
            w2 = w.reshape(1, -1)
            parts = [small_grads[k].reshape(1, -1)]
            own_block = jnp.zeros((1,), jnp.int32)
        else:
            w2 = w.reshape(-1, w.shape[-1])
            parts = [chip_sums[k], chip_recvd[k]]
            own_block = chip
        res = _adamw("adamw_" + k, w2, mom1[k].reshape(w2.shape), mom2[k].reshape(w2.shape), parts, own_block)
        out_g[k], out_d[k], out_m[k], out_v[k] = [r.reshape(w.shape) for r in res]

    return (loss, dx0.reshape(x.shape), *[out_g[k] for k in order], *[out_d[k] for k in order],
            *[out_m[k] for k in order], *[out_v[k] for k in order])
```

```python
import collections

import jax
import jax.numpy as jnp
from jax import lax
from jax.experimental import pallas as pl
from jax.experimental.pallas import tpu as pltpu

F32 = jnp.float32
BF16 = jnp.bfloat16
MESH = pl.DeviceIdType.MESH

N_DEV = 8
CHUNK = 64
POOL_WINDOWS = (2, 4, 8, 16)
POOL_HALO = 16
HEADS = 4
GATE_TEMP = 16.0
RMS_EPS = 1e-6
LANE = 128
V7X_VMEM_BYTES = 64 * 1024 * 1024
VMEM_LIMIT = V7X_VMEM_BYTES - 8 * 1024 * 1024
GLA_ROWS = 4 * CHUNK
ELEMENTWISE_BLOCK = 512 * 1024
FFN_SUBTILES = 2

ADAM_LR = 0.001
ADAM_B1 = 0.9
ADAM_B2 = 0.999
ADAM_EPS = 1e-08
ADAM_WD = 0.01
ADAM_STEP = 10

_NN = (((1,), (0,)), ((), ()))
_NT = (((1,), (1,)), ((), ()))
_TN = (((0,), (0,)), ((), ()))


def _cparams(sem=None):
    return pltpu.CompilerParams(dimension_semantics=sem, vmem_limit_bytes=VMEM_LIMIT)


def _tile(n, pref, align):
    t = (min(pref, n) // align) * align
    while t >= align:
        if n % t == 0:
            return t
        t -= align
    return n


def _dot(a, b, dims=_NN):
    return lax.dot_general(a.astype(BF16), b.astype(BF16), dims, preferred_element_type=F32)


def _silu_parts(z):
    sig = jax.nn.sigmoid(z)
    return z * sig, sig * (1.0 + z * (1.0 - sig))


def _rms_scale(xf):
    return lax.rsqrt(jnp.mean(xf * xf, axis=-1, keepdims=True) + RMS_EPS)


def _matmul(name, pairs, a_spec, b_spec, out_shape, out_spec, grid, acc_shape, dims,
            res=None, res_spec=None, scale=None, norm_gain=None, exchange=None):
    n = len(pairs)
    nk = grid[2]

    def body(ins, outs, scr):
        a_refs, b_refs = ins[:n], ins[n:2 * n]
        pos = 2 * n
        res_ref = gain_ref = h_ref = None
        if res is not None:
            res_ref = ins[pos]
            pos += 1
        if norm_gain is not None:
            gain_ref = ins[pos]
            h_ref = outs[1]
        o_ref = outs[0]
        acc, = scr
        k = pl.program_id(2)

        @pl.when(k == 0)
        def _():
            acc[...] = jnp.zeros_like(acc)

        part = None
        for a_ref, b_ref in zip(a_refs, b_refs):
            d = _dot(a_ref[...], b_ref[...], dims)
            part = d if part is None else part + d
        acc[...] += part

        @pl.when(k == nk - 1)
        def _():
            r = acc[...]
            if scale is not None:
                r = r * scale
            if res_ref is not None:
                r = r + res_ref[...]
            o_ref[...] = r.astype(o_ref.dtype)
            if h_ref is not None:
                h_ref[...] = ((r * _rms_scale(r)) * gain_ref[...]).astype(BF16)

    ops = [p[0] for p in pairs] + [p[1] for p in pairs]
    specs = [a_spec] * n + [b_spec] * n
    if res is not None:
        ops.append(res)
        specs.append(res_spec)
    out_specs, out_shapes = [out_spec], [out_shape]
    if norm_gain is not None:
        ops.append(norm_gain)
        specs.append(pl.BlockSpec(norm_gain.shape, lambda i, j, k: (0, 0)))
        out_specs.append(out_spec)
        out_shapes.append(jax.ShapeDtypeStruct(out_shape.shape, BF16))
    outs, carried = _hosted_call(
        name, body, grid, specs, out_specs, out_shapes, [pltpu.VMEM(acc_shape, F32)], ops,
        _NO_EXCHANGE if exchange is None else exchange)
    result = outs[0] if norm_gain is None else tuple(outs)
    return result if exchange is None else (result, carried)


def _mm_nn(name, pairs, out_dtype, res=None, norm_gain=None, exchange=None, tm=1024, tn=1024, tk=2048):
    m, kd = pairs[0][0].shape
    nd = pairs[0][1].shape[1]
    if norm_gain is not None:
        tm, tn = 512, nd
    tm, tn, tk = _tile(m, tm, 16), _tile(nd, tn, LANE), _tile(kd, tk, LANE)
    return _matmul(
        name, pairs,
        pl.BlockSpec((tm, tk), lambda i, j, k: (i, k)),
        pl.BlockSpec((tk, tn), lambda i, j, k: (k, j)),
        jax.ShapeDtypeStruct((m, nd), out_dtype),
        pl.BlockSpec((tm, tn), lambda i, j, k: (i, j)),
        (m // tm, nd // tn, kd // tk), (tm, tn), _NN,
        res=res, res_spec=pl.BlockSpec((tm, tn), lambda i, j, k: (i, j)), norm_gain=norm_gain,
        exchange=exchange)


def _mm_nt(name, pairs, out_dtype, tm=1024, tn=1024, tk=2048):
    m, kd = pairs[0][0].shape
    nd = pairs[0][1].shape[0]
    tm, tn, tk = _tile(m, tm, 16), _tile(nd, tn, LANE), _tile(kd, tk, LANE)
    return _matmul(
        name, pairs,
        pl.BlockSpec((tm, tk), lambda i, j, k: (i, k)),
        pl.BlockSpec((tn, tk), lambda i, j, k: (j, k)),
        jax.ShapeDtypeStruct((m, nd), out_dtype),
        pl.BlockSpec((tm, tn), lambda i, j, k: (i, j)),
        (m // tm, nd // tn, kd // tk), (tm, tn), _NT)


def _mm_tn(name, pairs, out_dtype, exchange=None, tm=1024, tn=2048, tk=1024):
    kd, m = pairs[0][0].shape
    nd = pairs[0][1].shape[1]
    tm, tn, tk = _tile(m, tm, LANE), _tile(nd, tn, LANE), _tile(kd, tk, 16)
    return _matmul(
        name, pairs,
        pl.BlockSpec((tk, tm), lambda i, j, k: (k, i)),
        pl.BlockSpec((tk, tn), lambda i, j, k: (k, j)),
        jax.ShapeDtypeStruct((m, nd), out_dtype),
        pl.BlockSpec((tm, tn), lambda i, j, k: (i, j)),
        (m // tm, nd // tn, kd // tk), (tm, tn), _TN, exchange=exchange)


def _rms_fwd(name, x, gain):
    m, d = x.shape
    tm = _tile(m, 512, 16)

    def body(x_ref, g_ref, o_ref):
        xf = x_ref[...]
        r = lax.rsqrt(jnp.mean(xf * xf, axis=-1, keepdims=True) + RMS_EPS)
        o_ref[...] = ((xf * r) * g_ref[...]).astype(o_ref.dtype)

    return pl.pallas_call(
        body, name=name, grid=(m // tm,),
        in_specs=[pl.BlockSpec((tm, d), lambda i: (i, 0)), pl.BlockSpec((1, d), lambda i: (0, 0))],
        out_specs=pl.BlockSpec((tm, d), lambda i: (i, 0)),
        out_shape=jax.ShapeDtypeStruct((m, d), BF16),
        compiler_params=_cparams(("parallel",)),
    )(x, gain)


def _rms_bwd_rows(xf, gain, dhf):
    r = _rms_scale(xf)
    xh = xf * r
    t = dhf * gain
    dx = r * (t - xh * jnp.mean(t * xh, axis=-1, keepdims=True))
    return dx, jnp.sum(dhf * xh, axis=0, keepdims=True)


def _rms_gain_grad(name, x, gain, dh):
    m, d = x.shape
    tm = _tile(m, 512, 16)

    def body(x_ref, g_ref, dh_ref, dg_ref):
        @pl.when(pl.program_id(0) == 0)
        def _():
            dg_ref[...] = jnp.zeros_like(dg_ref)

        dg_ref[...] += _rms_bwd_rows(x_ref[...], g_ref[...], dh_ref[...])[1]

    row = pl.BlockSpec((tm, d), lambda i: (i, 0))
    vec = pl.BlockSpec((1, d), lambda i: (0, 0))
    return pl.pallas_call(
        body, name=name, grid=(m // tm,), in_specs=[row, vec, row], out_specs=vec,
        out_shape=jax.ShapeDtypeStruct((1, d), F32),
        compiler_params=_cparams(("arbitrary",)),
    )(x, gain, dh)


def _loss_head(x, target, gain):
    m, d = x.shape
    tm = _tile(m, 512, 16)

    def body(x_ref, t_ref, g_ref, dx_ref, dxb_ref, dg_ref, loss_ref):
        i = pl.program_id(0)

        @pl.when(i == 0)
        def _():
            dg_ref[...] = jnp.zeros_like(dg_ref)
            loss_ref[...] = jnp.zeros_like(loss_ref)

        xf = x_ref[...]
        r = lax.rsqrt(jnp.mean(xf * xf, axis=-1, keepdims=True) + RMS_EPS)
        xh = xf * r
        g = g_ref[...]
        err = xh * g - t_ref[...]
        loss_ref[...] += jnp.full(loss_ref.shape, (0.5 / d) * jnp.sum(err * err), F32)
        dy = err * (1.0 / d)
        t = dy * g
        dx = r * (t - xh * jnp.mean(t * xh, axis=-1, keepdims=True))
        dx_ref[...] = dx
        dxb_ref[...] = dx.astype(BF16)
        dg_ref[...] += jnp.sum(dy * xh, axis=0, keepdims=True)

    row = pl.BlockSpec((tm, d), lambda i: (i, 0))
    vec = pl.BlockSpec((1, d), lambda i: (0, 0))
    return pl.pallas_call(
        body, name="loss_head", grid=(m // tm,),
        in_specs=[row, row, vec],
        out_specs=[row, row, vec, pl.BlockSpec((1, LANE), lambda i: (0, 0))],
        out_shape=[jax.ShapeDtypeStruct((m, d), F32), jax.ShapeDtypeStruct((m, d), BF16),
                   jax.ShapeDtypeStruct((1, d), F32), jax.ShapeDtypeStruct((1, LANE), F32)],
        compiler_params=_cparams(("arbitrary",)),
    )(x, target, gain)


def _hosted_call(name, body, grid, in_specs, out_specs, out_shape, scratch, operands, exchange):
    n_in, n_out, n_scr = len(in_specs), len(out_specs), len(scratch)
    n_xin, n_xout = len(exchange.ins), len(exchange.out_shape)

    def full_body(*refs):
        pos = 0
        parts = []
        for cnt in (n_in, n_xin, n_out, n_xout, n_scr):
            parts.append(refs[pos:pos + cnt])
            pos += cnt
        ins, x_ins, outs, x_outs, scr = parts
        sems = refs[pos:]
        first = pl.program_id(0) == 0
        last = pl.program_id(0) == grid[0] - 1
        for ax in range(1, len(grid)):
            first = jnp.logical_and(first, pl.program_id(ax) == 0)
            last = jnp.logical_and(last, pl.program_id(ax) == grid[ax] - 1)

        @pl.when(first)
        def _():
            exchange.start(x_ins, x_outs, sems)

        body(ins, outs, scr)

        @pl.when(last)
        def _():
            exchange.finish(x_ins, x_outs, sems)

    aliases = {n_in + k: n_out + k for k in range(n_xin)} if exchange.in_place else {}
    res = pl.pallas_call(
        full_body, name=name, grid=grid,
        in_specs=list(in_specs) + [_HBM] * n_xin, out_specs=list(out_specs) + [_HBM] * n_xout,
        out_shape=list(out_shape) + list(exchange.out_shape),
        scratch_shapes=list(scratch) + list(exchange.sems),
        input_output_aliases=aliases,
        compiler_params=_cparams(("arbitrary",) * len(grid)),
    )(*operands, *exchange.ins)
    return res[:n_out], res[n_out:]


def _ffn_fwd(name, x, h, wg, wu, wd, next_gain, exchange):
    n, d = x.shape
    nb, _, fb = wg.shape
    tm = _tile(n, 512, 16 * FFN_SUBTILES)
    ts = tm // FFN_SUBTILES
    with_norm = next_gain is not None

    def body(ins, outs, scr):
        x_ref, h_ref, wg_ref, wu_ref, wd_ref = ins[:5]
        xo_ref, a_ref, b_ref = outs[:3]
        acc, = scr
        j = pl.program_id(1)

        @pl.when(j == 0)
        def _():
            acc[...] = jnp.zeros_like(acc)

        for s in range(FFN_SUBTILES):
            rows = slice(s * ts, (s + 1) * ts)
            hh = h_ref[rows, :]
            a = _dot(hh, wg_ref[...])
            b = _dot(hh, wu_ref[...])
            a_ref[rows, :] = a.astype(BF16)
            b_ref[rows, :] = b.astype(BF16)
            act = (a * jax.nn.sigmoid(a)) * b
            acc[rows, :] += _dot(act, wd_ref[...])

        @pl.when(j == nb - 1)
        def _():
            xo = x_ref[...] + 0.5 * acc[...]
            xo_ref[...] = xo
            if with_norm:
                outs[3][...] = ((xo * _rms_scale(xo)) * ins[5][...]).astype(BF16)

    row = pl.BlockSpec((tm, d), lambda i, j: (i, 0))
    w_in = pl.BlockSpec((None, d, fb), lambda i, j: (j, 0, 0))
    hid = pl.BlockSpec((None, tm, fb), lambda i, j: (j, i, 0))
    return _hosted_call(
        name, body, (n // tm, nb),
        [row, row, w_in, w_in, pl.BlockSpec((None, fb, d), lambda i, j: (j, 0, 0))]
        + ([pl.BlockSpec((1, d), lambda i, j: (0, 0))] if with_norm else []),
        [row, hid, hid] + ([row] if with_norm else []),
        [jax.ShapeDtypeStruct((n, d), F32), jax.ShapeDtypeStruct((nb, n, fb), BF16),
         jax.ShapeDtypeStruct((nb, n, fb), BF16)]
        + ([jax.ShapeDtypeStruct((n, d), BF16)] if with_norm else []),
        [pltpu.VMEM((tm, d), F32)], (x, h, wg, wu, wd) + ((next_gain,) if with_norm else ()), exchange)


def _ffn_bwd_act(name, dxb, wd, a, b, exchange):
    n, d = dxb.shape
    nb, fb, _ = wd.shape
    subtiles = 2 * FFN_SUBTILES
    tm = _tile(n, 1024, 16 * subtiles)
    ts = tm // subtiles

    def body(ins, outs, scr):
        del scr
        dx_ref, wd_ref, a_ref, b_ref = ins
        da_ref, db_ref, act_ref = outs
        for s in range(subtiles):
            rows = slice(s * ts, (s + 1) * ts)
            dact = 0.5 * _dot(dx_ref[rows, :], wd_ref[...], _NT)
            af = a_ref[rows, :].astype(F32)
            bf = b_ref[rows, :].astype(F32)
            sl, dsl = _silu_parts(af)
            act_ref[rows, :] = (sl * bf).astype(BF16)
            db_ref[rows, :] = (dact * sl).astype(BF16)
            da_ref[rows, :] = (dact * bf * dsl).astype(BF16)

    hid = pl.BlockSpec((None, tm, fb), lambda i, j: (j, i, 0))
    shp = jax.ShapeDtypeStruct((nb, n, fb), BF16)
    return _hosted_call(
        name, body, (n // tm, nb),
        [pl.BlockSpec((tm, d), lambda i, j: (i, 0)), pl.BlockSpec((None, fb, d), lambda i, j: (j, 0, 0)), hid, hid],
        [hid, hid, hid], [shp, shp, shp], [], (dxb, wd, a, b), exchange)


def _proj_rms_bwd(name, pairs, a_spec, b_spec, nk, tm, x, gain, dres, exchange):
    n, d = x.shape
    n_pairs = len(pairs)
    slab = _tile(tm, 128, 16)

    def body(ins, outs, scr):
        a_refs, b_refs = ins[:n_pairs], ins[n_pairs:2 * n_pairs]
        x_ref, g_ref, res_ref = ins[2 * n_pairs:]
        dx_ref, dxb_ref, dg_ref = outs
        del scr
        i, k = pl.program_id(0), pl.program_id(1)

        @pl.when(jnp.logical_and(i == 0, k == 0))
        def _():
            dg_ref[...] = jnp.zeros_like(dg_ref)

        @pl.when(k == 0)
        def _():
            dx_ref[...] = jnp.zeros_like(dx_ref)

        part = None
        for a_ref, b_ref in zip(a_refs, b_refs):
            t = _dot(a_ref[...], b_ref[...], _NT)
            part = t if part is None else part + t
        dx_ref[...] += part

        @pl.when(k == nk - 1)
        def _():
            for s in range(tm // slab):
                rows = slice(s * slab, (s + 1) * slab)
                dx, dg = _rms_bwd_rows(x_ref[rows, :], g_ref[...], dx_ref[rows, :])
                dx = dx + res_ref[rows, :]
                dx_ref[rows, :] = dx
                dxb_ref[rows, :] = dx.astype(BF16)
                dg_ref[...] += dg

    row = pl.BlockSpec((tm, d), lambda i, k: (i, 0))
    vec = pl.BlockSpec((1, d), lambda i, k: (0, 0))
    return _hosted_call(
        name, body, (n // tm, nk), [a_spec] * n_pairs + [b_spec] * n_pairs + [row, vec, row],
        [row, row, vec],
        [jax.ShapeDtypeStruct((n, d), F32), jax.ShapeDtypeStruct((n, d), BF16),
         jax.ShapeDtypeStruct((1, d), F32)],
        [], tuple(p[0] for p in pairs) + tuple(p[1] for p in pairs) + (x, gain, dres), exchange)


def _ffn_bwd_dh(name, da, db, wg, wu, x, gain, dres, exchange):
    nb, n, fb = da.shape
    d = wg.shape[1]
    tm = _tile(n, 512, 16)
    return _proj_rms_bwd(
        name, [(da, wg), (db, wu)],
        pl.BlockSpec((None, tm, fb), lambda i, k: (k, i, 0)),
        pl.BlockSpec((None, d, fb), lambda i, k: (k, 0, 0)), nb, tm, x, gain, dres, exchange)


def _mm_nt_rms_bwd(name, a, w, x, gain, dres, exchange, tk=2048):
    n, kd = a.shape
    d = w.shape[0]
    tm, tk = _tile(n, 512, 16), _tile(kd, tk, LANE)
    return _proj_rms_bwd(
        name, [(a, w)],
        pl.BlockSpec((tm, tk), lambda i, k: (i, k)),
        pl.BlockSpec((d, tk), lambda i, k: (0, k)), kd // tk, tm, x, gain, dres, exchange)


def _ffn_bwd_weights(name, h, dxb, da, db, act, exchange):
    n, d = h.shape
    nb, _, fb = da.shape
    tk = _tile(n, 512, 16)
    nk = n // tk

    def body(ins, outs, scr):
        h_ref, dx_ref, da_ref, db_ref, act_ref = ins
        dwg_ref, dwu_ref, dwd_ref = outs
        accg, accu, accd = scr
        k = pl.program_id(1)

        @pl.when(k == 0)
        def _():
            accg[...] = jnp.zeros_like(accg)
            accu[...] = jnp.zeros_like(accu)
            accd[...] = jnp.zeros_like(accd)

        hh = h_ref[...]
        accg[...] += _dot(hh, da_ref[...], _TN)
        accu[...] += _dot(hh, db_ref[...], _TN)
        accd[...] += _dot(act_ref[...], dx_ref[...], _TN)

        @pl.when(k == nk - 1)
        def _():
            dwg_ref[...] = accg[...].astype(BF16)
            dwu_ref[...] = accu[...].astype(BF16)
            dwd_ref[...] = (0.5 * accd[...]).astype(BF16)

    row = pl.BlockSpec((tk, d), lambda j, k: (k, 0))
    hid = pl.BlockSpec((None, tk, fb), lambda j, k: (j, k, 0))
    w_in = pl.BlockSpec((None, d, fb), lambda j, k: (j, 0, 0))
    w_out = pl.BlockSpec((None, fb, d), lambda j, k: (j, 0, 0))
    return _hosted_call(
        name, body, (nb, nk), [row, row, hid, hid, hid], [w_in, w_in, w_out],
        [jax.ShapeDtypeStruct((nb, d, fb), BF16), jax.ShapeDtypeStruct((nb, d, fb), BF16),
         jax.ShapeDtypeStruct((nb, fb, d), BF16)],
        [pltpu.VMEM((d, fb), F32), pltpu.VMEM((d, fb), F32), pltpu.VMEM((fb, d), F32)],
        (h, dxb, da, db, act), exchange)


def _pool_diff(ext_ref, rows, cols, width, t_idx):
    s = ext_ref[POOL_HALO:POOL_HALO + rows, cols]
    for sft in range(1, width):
        s = s + ext_ref[POOL_HALO - sft:POOL_HALO - sft + rows, cols]
    cnt = jnp.minimum(t_idx + 1, width).astype(F32)
    return s / cnt - ext_ref[POOL_HALO:POOL_HALO + rows, cols]


def _pool_fwd(proj, pool_w, pool_scale):
    n = proj.shape[0]
    dp = pool_scale.shape[1]
    c = dp // len(POOL_WINDOWS)
    tm = _tile(n, 512, POOL_HALO)
    hb = tm // POOL_HALO

    def body(u_ref, halo_ref, pw_ref, sc_ref, y_ref, ext):
        i = pl.program_id(0)
        ext[0:POOL_HALO, :] = jnp.where(i > 0, halo_ref[...], 0.0)
        ext[POOL_HALO:, :] = u_ref[...]
        t_idx = i * tm + lax.broadcasted_iota(jnp.int32, (tm, 1), 0)
        for g, width in enumerate(POOL_WINDOWS):
            cols = slice(g * c, (g + 1) * c)
            dgrp = _pool_diff(ext, tm, cols, width, t_idx)
            y_ref[:, cols] = (_dot(dgrp, pw_ref[g]) * sc_ref[:, cols]).astype(BF16)

    return pl.pallas_call(
        body, name="pool_fwd", grid=(n // tm,),
        in_specs=[pl.BlockSpec((tm, dp), lambda i: (i, 0)),
                  pl.BlockSpec((POOL_HALO, dp), lambda i: (jnp.maximum(i * hb - 1, 0), 0)),
                  pl.BlockSpec((len(POOL_WINDOWS), c, c), lambda i: (0, 0, 0)),
                  pl.BlockSpec((1, dp), lambda i: (0, 0))],
        out_specs=pl.BlockSpec((tm, dp), lambda i: (i, 0)),
        out_shape=jax.ShapeDtypeStruct((n, dp), BF16),
        scratch_shapes=[pltpu.VMEM((tm + POOL_HALO, dp), F32)],
        compiler_params=_cparams(("parallel",)),
    )(proj, proj, pool_w, pool_scale)


def _pool_bwd(proj, dymix, pool_w, pool_scale):
    n = proj.shape[0]
    dp = pool_scale.shape[1]
    ng = len(POOL_WINDOWS)
    c = dp // ng
    tm = _tile(n, 512, POOL_HALO)
    hb = tm // POOL_HALO
    nsteps = n // tm
    last_halo = n // POOL_HALO - 1

    def body(u_ref, halo_ref, dy_ref, dyn_ref, pw_ref, sc_ref, du_ref, dpw_ref, dsc_ref, ext, dyext, e_s):
        i = pl.program_id(0)

        @pl.when(i == 0)
        def _():
            dpw_ref[...] = jnp.zeros_like(dpw_ref)
            dsc_ref[...] = jnp.zeros_like(dsc_ref)

        ext[0:POOL_HALO, :] = jnp.where(i > 0, halo_ref[...], 0.0)
        ext[POOL_HALO:, :] = u_ref[...]
        dyext[0:tm, :] = dy_ref[...]
        dyext[tm:, :] = jnp.where(i < nsteps - 1, dyn_ref[...], 0.0)
        t_idx = i * tm + lax.broadcasted_iota(jnp.int32, (tm, 1), 0)
        te_idx = i * tm + lax.broadcasted_iota(jnp.int32, (tm + POOL_HALO, 1), 0)
        for g, width in enumerate(POOL_WINDOWS):
            cols = slice(g * c, (g + 1) * c)
            dgrp = _pool_diff(ext, tm, cols, width, t_idx)
            w_g = pw_ref[g]
            dys = dyext[:, cols] * sc_ref[:, cols]
            ypre = _dot(dgrp, w_g)
            dsc_ref[:, cols] += jnp.sum(dyext[0:tm, cols] * ypre, axis=0, keepdims=True)
            dpw_ref[g] += _dot(dgrp, dys[0:tm], _TN)
            dd = _dot(dys, w_g, _NT)
            e_s[...] = dd / jnp.minimum(te_idx + 1, width).astype(F32)
            acc = e_s[0:tm, :]
            for sft in range(1, width):
                acc = acc + e_s[sft:sft + tm, :]
            du_ref[:, cols] = (acc - dd[0:tm]).astype(BF16)

    return pl.pallas_call(
        body, name="pool_bwd", grid=(nsteps,),
        in_specs=[pl.BlockSpec((tm, dp), lambda i: (i, 0)),
                  pl.BlockSpec((POOL_HALO, dp), lambda i: (jnp.maximum(i * hb - 1, 0), 0)),
                  pl.BlockSpec((tm, dp), lambda i: (i, 0)),
                  pl.BlockSpec((POOL_HALO, dp), lambda i: (jnp.minimum((i + 1) * hb, last_halo), 0)),
                  pl.BlockSpec((ng, c, c), lambda i: (0, 0, 0)),
                  pl.BlockSpec((1, dp), lambda i: (0, 0))],
        out_specs=[pl.BlockSpec((tm, dp), lambda i: (i, 0)),
                   pl.BlockSpec((ng, c, c), lambda i: (0, 0, 0)),
                   pl.BlockSpec((1, dp), lambda i: (0, 0))],
        out_shape=[jax.ShapeDtypeStruct((n, dp), BF16), jax.ShapeDtypeStruct((ng, c, c), F32),
                   jax.ShapeDtypeStruct((1, dp), F32)],
        scratch_shapes=[pltpu.VMEM((tm + POOL_HALO, dp), F32), pltpu.VMEM((tm + POOL_HALO, dp), F32),
                        pltpu.VMEM((tm + POOL_HALO, c), F32)],
        compiler_params=_cparams(("arbitrary",)),
    )(proj, proj, dymix, dymix, pool_w, pool_scale)


def _chunk_cumsum(x):
    row = lax.broadcasted_iota(jnp.int32, x.shape, 0) % CHUNK
    s = 1
    while s < CHUNK:
        x = x + jnp.where(row >= s, pltpu.roll(x, s, 0), 0.0)
        s *= 2
    return x


def _chunk_ends(cum_ref, bend_ref, nc):
    for c in range(nc):
        last = cum_ref[c * CHUNK + CHUNK - 1:(c + 1) * CHUNK, :]
        bend_ref[c * CHUNK:(c + 1) * CHUNK, :] = jnp.broadcast_to(last, (CHUNK, last.shape[1]))


def _gate_logits(alr_ref, wa_ref, ba_ref):
    z = _dot(alr_ref[...], wa_ref[...]) + ba_ref[...]
    la = (jnp.minimum(z, 0.0) - jnp.log(1.0 + jnp.exp(-jnp.abs(z)))) * (1.0 / GATE_TEMP)
    return z, la


def _gla_dims(proj, head_norm):
    n, pw = proj.shape
    dv4 = head_norm.shape[1]
    dk4 = dv4 // 2
    return n, pw, dv4, dk4, dk4 // HEADS, dv4 // HEADS


def _gla_in_specs(t, dk4, dv4, rev):
    alr_blk = (2 * dv4 + 2 * dv4) // LANE
    return [pl.BlockSpec((t, dk4), lambda i: (rev(i), 2)),
            pl.BlockSpec((t, dk4), lambda i: (rev(i), 3)),
            pl.BlockSpec((t, dv4), lambda i: (rev(i), 2)),
            pl.BlockSpec((t, dv4), lambda i: (rev(i), 3)),
            pl.BlockSpec((t, LANE), lambda i: (rev(i), alr_blk))]


def _gla_fwd(proj, wa2p, b_a, head_norm):
    n, _, dv4, dk4, dk, dv = _gla_dims(proj, head_norm)
    t = _tile(n, GLA_ROWS, CHUNK)
    nc = t // CHUNK
    qscale = dk ** -0.5

    def body(q_ref, k_ref, v_ref, g_ref, alr_ref, wa_ref, ba_ref, hn_ref, y_ref, st_ref,
             state, cum_s, bend_s, o_s):
        i = pl.program_id(0)

        @pl.when(i == 0)
        def _():
            state[...] = jnp.zeros_like(state)

        _, la = _gate_logits(alr_ref, wa_ref, ba_ref)
        cum_s[...] = _chunk_cumsum(la)
        _chunk_ends(cum_s, bend_s, nc)
        kd = (k_ref[...] * jnp.exp(bend_s[...] - cum_s[...])).astype(BF16)
        qs = (q_ref[...] * qscale).astype(BF16)
        vv = v_ref[...].astype(BF16)
        units = [(c, h) for c in range(nc) for h in range(HEADS)]
        for c, h in units:
            rows = slice(c * CHUNK, (c + 1) * CHUNK)
            st_ref[c, h] = _dot(vv[rows, h * dv:(h + 1) * dv], kd[rows, h * dk:(h + 1) * dk], _TN)
        for c, h in units:
            e = jnp.exp(bend_s[c * CHUNK:c * CHUNK + 1, h * dk:(h + 1) * dk])
            s_new = state[h] * e + st_ref[c, h]
            state[h] = s_new
            st_ref[c, h] = s_new
        for c, h in units:
            rows = slice(c * CHUNK, (c + 1) * CHUNK)
            o_s[rows, h * dv:(h + 1) * dv] = _dot(qs[rows, h * dk:(h + 1) * dk], st_ref[c, h], _NT)
        for h in range(HEADS):
            hv = slice(h * dv, (h + 1) * dv)
            o = o_s[:, hv]
            gg = g_ref[:, hv]
            y_ref[:, hv] = (((o * _rms_scale(o)) * hn_ref[:, hv]) * (gg * jax.nn.sigmoid(gg))).astype(BF16)

    full = lambda shape: pl.BlockSpec(shape, lambda i: tuple(0 for _ in shape))
    return pl.pallas_call(
        body, name="gla_fwd", grid=(n // t,),
        in_specs=_gla_in_specs(t, dk4, dv4, lambda i: i)
        + [full((LANE, dk4)), full((1, dk4)), full((1, dv4))],
        out_specs=[pl.BlockSpec((t, dv4), lambda i: (i, 0)),
                   pl.BlockSpec((nc, HEADS, dv, dk), lambda i: (i, 0, 0, 0))],
        out_shape=[jax.ShapeDtypeStruct((n, dv4), BF16),
                   jax.ShapeDtypeStruct((n // CHUNK, HEADS, dv, dk), F32)],
        scratch_shapes=[pltpu.VMEM((HEADS, dv, dk), F32), pltpu.VMEM((t, dk4), F32),
                        pltpu.VMEM((t, dk4), F32), pltpu.VMEM((t, dv4), F32)],
        compiler_params=_cparams(("arbitrary",)),
    )(proj, proj, proj, proj, proj, wa2p, b_a, head_norm)


def _gla_bwd(proj, dymix, du, states, wa2p, b_a, head_norm):
    n, pw, dv4, dk4, dk, dv = _gla_dims(proj, head_norm)
    t = _tile(n, GLA_ROWS, CHUNK)
    nc = t // CHUNK
    nsteps = n // t
    qscale = dk ** -0.5
    rev = lambda i: nsteps - 1 - i

    def body(q_ref, k_ref, v_ref, g_ref, alr_ref, dy_ref, du_ref, st_ref, prev_ref, wa_ref, ba_ref, hn_ref,
             dp_ref, dwa_ref, dba_ref, dhn_ref, carry, cum_s, bend_s, gst_s, dkd_s, dee_s, o_s, do_s):
        i = pl.program_id(0)

        @pl.when(i == 0)
        def _():
            carry[...] = jnp.zeros_like(carry)
            dwa_ref[...] = jnp.zeros_like(dwa_ref)
            dba_ref[...] = jnp.zeros_like(dba_ref)
            dhn_ref[...] = jnp.zeros_like(dhn_ref)

        first_step = i == nsteps - 1
        z, la = _gate_logits(alr_ref, wa_ref, ba_ref)
        cum_s[...] = _chunk_cumsum(la)
        _chunk_ends(cum_s, bend_s, nc)
        dec = jnp.exp(bend_s[...] - cum_s[...])
        kd_f = k_ref[...] * dec
        kd = kd_f.astype(BF16)
        qs = (q_ref[...] * qscale).astype(BF16)
        vv = v_ref[...].astype(BF16)
        dp_ref[:, 0:dv4] = du_ref[...]
        units = [(c, h) for c in range(nc) for h in range(HEADS)]

        for c, h in units:
            rows = slice(c * CHUNK, (c + 1) * CHUNK)
            o_s[rows, h * dv:(h + 1) * dv] = _dot(qs[rows, h * dk:(h + 1) * dk], st_ref[c, h], _NT)
        for h in range(HEADS):
            hv = slice(h * dv, (h + 1) * dv)
            o = o_s[:, hv]
            r = _rms_scale(o)
            oh = o * r
            sl, dsl = _silu_parts(g_ref[:, hv])
            dyh = dy_ref[:, hv]
            hn = hn_ref[:, hv]
            tt = dyh * sl
            dhn_ref[:, hv] += jnp.sum(tt * oh, axis=0, keepdims=True)
            dp_ref[:, 3 * dv4 + h * dv:3 * dv4 + (h + 1) * dv] = (dyh * (oh * hn) * dsl).astype(BF16)
            tt = tt * hn
            do_s[:, hv] = (r * (tt - oh * jnp.mean(tt * oh, axis=-1, keepdims=True))).astype(BF16)
        for c, h in units:
            rows = slice(c * CHUNK, (c + 1) * CHUNK)
            hk = slice(h * dk, (h + 1) * dk)
            do = do_s[rows, h * dv:(h + 1) * dv]
            gst_s[c, h] = _dot(do, qs[rows, hk], _TN)
            dp_ref[rows, dv4 + h * dk:dv4 + (h + 1) * dk] = (_dot(do, st_ref[c, h]) * qscale).astype(BF16)

        for c, h in reversed(units):
            g_n = carry[h] + gst_s[c, h]
            gst_s[c, h] = g_n
            carry[h] = g_n * jnp.exp(bend_s[c * CHUNK:c * CHUNK + 1, h * dk:(h + 1) * dk])

        for c, h in units:
            rows = slice(c * CHUNK, (c + 1) * CHUNK)
            hk = slice(h * dk, (h + 1) * dk)
            hv = slice(h * dv, (h + 1) * dv)
            g_n = gst_s[c, h]
            if c > 0:
                s_prev = st_ref[c - 1, h]
            else:
                s_prev = jnp.where(first_step, 0.0, prev_ref[0, h])
            dkd_s[rows, hk] = _dot(vv[rows, hv], g_n)
            dp_ref[rows, 2 * dv4 + h * dv:2 * dv4 + (h + 1) * dv] = _dot(kd[rows, hk], g_n, _NT).astype(BF16)
            dee = jnp.sum(g_n * s_prev, axis=0, keepdims=True) * jnp.exp(bend_s[c * CHUNK:c * CHUNK + 1, hk])
            dee_s[rows, hk] = jnp.broadcast_to(dee, (CHUNK, dk))

        dkd = dkd_s[...]
        dp_ref[:, dv4 + dk4:dv4 + 2 * dk4] = (dkd * dec).astype(BF16)
        w = dkd * kd_f
        dla = (_chunk_cumsum(w) - w) + dee_s[...]
        dz = dla * (1.0 / GATE_TEMP) * (1.0 - jax.nn.sigmoid(z))
        dp_ref[:, 4 * dv4:4 * dv4 + LANE] = _dot(dz, wa_ref[...], _NT).astype(BF16)
        dwa_ref[...] += _dot(alr_ref[...], dz, _TN)
        dba_ref[...] += jnp.sum(dz, axis=0, keepdims=True)

    full = lambda shape: pl.BlockSpec(shape, lambda i: tuple(0 for _ in shape))
    return pl.pallas_call(
        body, name="gla_bwd", grid=(nsteps,),
        in_specs=_gla_in_specs(t, dk4, dv4, rev)
        + [pl.BlockSpec((t, dv4), lambda i: (rev(i), 1)),
           pl.BlockSpec((t, dv4), lambda i: (rev(i), 0)),
           pl.BlockSpec((nc, HEADS, dv, dk), lambda i: (rev(i), 0, 0, 0)),
           pl.BlockSpec((1, HEADS, dv, dk), lambda i: (jnp.maximum(rev(i) * nc - 1, 0), 0, 0, 0)),
           full((LANE, dk4)), full((1, dk4)), full((1, dv4))],
        out_specs=[pl.BlockSpec((t, pw), lambda i: (rev(i), 0)),
                   full((LANE, dk4)), full((1, dk4)), full((1, dv4))],
        out_shape=[jax.ShapeDtypeStruct((n, pw), BF16), jax.ShapeDtypeStruct((LANE, dk4), F32),
                   jax.ShapeDtypeStruct((1, dk4), F32), jax.ShapeDtypeStruct((1, dv4), F32)],
        scratch_shapes=[pltpu.VMEM((HEADS, dv, dk), F32), pltpu.VMEM((t, dk4), F32),
                        pltpu.VMEM((t, dk4), F32), pltpu.VMEM((nc, HEADS, dv, dk), F32),
                        pltpu.VMEM((t, dk4), F32), pltpu.VMEM((t, dk4), F32),
                        pltpu.VMEM((t, dv4), F32), pltpu.VMEM((t, dv4), BF16)],
        compiler_params=_cparams(("arbitrary",)),
    )(proj, proj, proj, proj, proj, dymix, du, states, states, wa2p, b_a, head_norm)


def _softmax_rows(q, k, scale):
    s = _dot(q, k, _NT) * scale
    p = jnp.exp(s - jnp.max(s, axis=-1, keepdims=True))
    return p / jnp.sum(p, axis=-1, keepdims=True)


def _attn_fwd(q, kv):
    n, d = q.shape
    m = kv.shape[0]
    dh = d // HEADS
    tm = _tile(n, 512, 16)
    scale = dh ** -0.5

    def body(q_ref, k_ref, v_ref, o_ref):
        for h in range(HEADS):
            hs = slice(h * dh, (h + 1) * dh)
            p = _softmax_rows(q_ref[:, hs], k_ref[:, hs], scale)
            o_ref[:, hs] = _dot(p, v_ref[:, hs]).astype(BF16)

    return pl.pallas_call(
        body, name="attn_fwd", grid=(n // tm,),
        in_specs=[pl.BlockSpec((tm, d), lambda i: (i, 0)), pl.BlockSpec((m, d), lambda i: (0, 0)),
                  pl.BlockSpec((m, d), lambda i: (0, 1))],
        out_specs=pl.BlockSpec((tm, d), lambda i: (i, 0)),
        out_shape=jax.ShapeDtypeStruct((n, d), BF16),
        compiler_params=_cparams(("parallel",)),
    )(q, kv, kv)


def _attn_bwd(q, kv, do):
    n, d = q.shape
    m = kv.shape[0]
    dh = d // HEADS
    tm = _tile(n, 512, 16)
    scale = dh ** -0.5

    def body(q_ref, k_ref, v_ref, do_ref, dq_ref, dk_ref, dv_ref):
        i = pl.program_id(0)

        @pl.when(i == 0)
        def _():
            dk_ref[...] = jnp.zeros_like(dk_ref)
            dv_ref[...] = jnp.zeros_like(dv_ref)

        for h in range(HEADS):
            hs = slice(h * dh, (h + 1) * dh)
            qh, kh, vh, doh = q_ref[:, hs], k_ref[:, hs], v_ref[:, hs], do_ref[:, hs]
            p = _softmax_rows(qh, kh, scale)
            dv_ref[:, hs] += _dot(p, doh, _TN)
            dp = _dot(doh, vh, _NT)
            ds = p * (dp - jnp.sum(dp * p, axis=-1, keepdims=True)) * scale
            dq_ref[:, hs] = _dot(ds, kh).astype(BF16)
            dk_ref[:, hs] += _dot(ds, qh, _TN)

    row = pl.BlockSpec((tm, d), lambda i: (i, 0))
    memb = pl.BlockSpec((m, d), lambda i: (0, 0))
    return pl.pallas_call(
        body, name="attn_bwd", grid=(n // tm,),
        in_specs=[row, memb, pl.BlockSpec((m, d), lambda i: (0, 1)), row],
        out_specs=[row, memb, memb],
        out_shape=[jax.ShapeDtypeStruct((n, d), BF16), jax.ShapeDtypeStruct((m, d), F32),
                   jax.ShapeDtypeStruct((m, d), F32)],
        compiler_params=_cparams(("arbitrary",)),
    )(q, kv, kv, do)


def _adamw(name, w, m, v, parts, own_block):
    r, c = w.shape
    tr = _tile(r, max(16, ELEMENTWISE_BLOCK // c), 16)
    row = pl.BlockSpec((tr, c), lambda i, o: (i, 0))
    ops, specs = [w, m, v], [row, row, row]
    for p in parts:
        if p.ndim == 2:
            ops.append(p)
            specs.append(row)
        elif p.shape[0] == 4:
            ops.append(p)
            specs.append(pl.BlockSpec((None, tr, c), lambda i, o: (o[0], i, 0)))
        else:
            for s in range(p.shape[0]):
                ops.append(p)
                specs.append(pl.BlockSpec((None, tr, c), lambda i, o, s=s: (s, i, 0)))
    n_parts = len(ops) - 3
    c1 = 1.0 - ADAM_B1 ** ADAM_STEP
    c2 = 1.0 - ADAM_B2 ** ADAM_STEP

    def body(o_ref, *refs):
        del o_ref
        w_ref, m_ref, v_ref = refs[:3]
        g_refs = refs[3:3 + n_parts]
        go_ref, d_ref, mo_ref, vo_ref = refs[3 + n_parts:]
        g = g_refs[0][...].astype(F32)
        for g_ref in g_refs[1:]:
            g = g + g_ref[...].astype(F32)
        m_new = ADAM_B1 * m_ref[...] + (1.0 - ADAM_B1) * g
        v_new = ADAM_B2 * v_ref[...] + (1.0 - ADAM_B2) * (g * g)
        m_hat = m_new / c1
        v_hat = v_new / c2
        go_ref[...] = g
        d_ref[...] = -ADAM_LR * (m_hat / (jnp.sqrt(v_hat) + ADAM_EPS) + ADAM_WD * w_ref[...])
        mo_ref[...] = m_new
        vo_ref[...] = v_new

    shp = jax.ShapeDtypeStruct((r, c), F32)
    return pl.pallas_call(
        body, name=name,
        grid_spec=pltpu.PrefetchScalarGridSpec(
            num_scalar_prefetch=1, grid=(r // tr,), in_specs=specs, out_specs=[row] * 4),
        out_shape=[shp] * 4,
        compiler_params=_cparams(("parallel",)),
    )(own_block, *ops)


def _pair_add(name, g, recvd, core):
    _, r, c = g.shape
    tr = _tile(r, max(16, ELEMENTWISE_BLOCK // c), 16)

    def body(core_ref, a_ref, b_ref, o_ref):
        del core_ref
        o_ref[...] = (a_ref[...].astype(F32) + b_ref[...].astype(F32)).astype(o_ref.dtype)

    blk = pl.BlockSpec((None, tr, c), lambda s, i, core_ref: (s, i, 0))
    mine = pl.BlockSpec((None, tr, c), lambda s, i, core_ref: (2 * s + core_ref[0], i, 0))
    return pl.pallas_call(
        body, name=name,
        grid_spec=pltpu.PrefetchScalarGridSpec(
            num_scalar_prefetch=1, grid=(4, r // tr), in_specs=[mine, blk], out_specs=blk),
        out_shape=jax.ShapeDtypeStruct(recvd.shape, g.dtype),
        compiler_params=_cparams(("parallel", "parallel")),
    )(core, g, recvd)


def _position():
    return lax.axis_index("x"), lax.axis_index("y"), lax.axis_index("c")


_HBM = pl.BlockSpec(memory_space=pltpu.HBM)

_Exchange = collections.namedtuple("_Exchange", "ins out_shape sems start finish in_place", defaults=(False,))
_NO_EXCHANGE = _Exchange((), (), (), lambda ins, outs, sems: None, lambda ins, outs, sems: None)


def _all_gather(name, shards):
    n = len(shards)

    def body(*refs):
        ins, outs = refs[:n], refs[n:2 * n]
        send_sems, recv_sems, local_sems = refs[2 * n:]
        x, y, c = _position()
        me, sibling = (x, y, c), (x, y, 1 - c)
        chips = [(1 - x, y), (x, 1 - y), (1 - x, 1 - y)]

        def block(ref, px, py, pc):
            return ref.at[4 * px + 2 * py + pc]

        def copy(a, k, owner, to, src=None):
            return pltpu.make_async_remote_copy(
                src_ref=block(outs[a], *owner) if src is None else src, dst_ref=block(outs[a], *owner),
                send_sem=send_sems.at[7 * a + k], recv_sem=recv_sems.at[7 * a + k],
                device_id=to, device_id_type=MESH)

        started = []
        for a in range(n):
            mine = pltpu.make_async_copy(ins[a], block(outs[a], *me), local_sems.at[a])
            mine.start()
            started.append(mine)
        sends = []
        for a in range(n):
            sends.append(copy(a, 0, me, sibling, src=ins[a]))
            sends += [copy(a, 1 + j, me, (*chip, c), src=ins[a]) for j, chip in enumerate(chips)]
        for cp in sends:
            cp.start()
        for j, chip in enumerate(chips):
            for a in range(n):
                copy(a, 1 + j, (*chip, c), me).wait_recv()
                fwd = copy(a, 4 + j, (*chip, c), sibling)
                fwd.start()
                sends.append(fwd)
        for a in range(n):
            copy(a, 0, sibling, me).wait_recv()
            for j, chip in enumerate(chips):
                copy(a, 4 + j, (*chip, 1 - c), me).wait_recv()
        for cp in sends:
            cp.wait_send()
        for mine in started:
            mine.wait()

    return pl.pallas_call(
        body, name=name,
        in_specs=[_HBM] * n, out_specs=[_HBM] * n,
        out_shape=[jax.ShapeDtypeStruct((N_DEV,) + s.shape, s.dtype) for s in shards],
        scratch_shapes=[pltpu.SemaphoreType.DMA((7 * n,)), pltpu.SemaphoreType.DMA((7 * n,)),
                        pltpu.SemaphoreType.DMA((n,))],
    )(*shards)


def _pair_exchange(grads):
    n = len(grads)

    def start(ins, recvd, sems):
        send_sems, recv_sems = sems
        x, y, c = _position()
        for a in range(n):
            for chip in range(4):
                pltpu.make_async_remote_copy(
                    src_ref=ins[a].at[2 * chip + (1 - c)], dst_ref=recvd[a].at[chip],
                    send_sem=send_sems.at[a], recv_sem=recv_sems.at[a],
                    device_id=(x, y, 1 - c), device_id_type=MESH).start()

    def finish(ins, recvd, sems):
        del ins
        send_sems, recv_sems = sems
        x, y, c = _position()
        for a in range(n):
            pltpu.make_async_remote_copy(
                src_ref=recvd[a], dst_ref=recvd[a], send_sem=send_sems.at[a], recv_sem=recv_sems.at[a],
                device_id=(x, y, 1 - c), device_id_type=MESH).wait()

    return _Exchange(
        ins=tuple(grads),
        out_shape=tuple(jax.ShapeDtypeStruct((4,) + g.shape[1:], g.dtype) for g in grads),
        sems=(pltpu.SemaphoreType.DMA((n,)), pltpu.SemaphoreType.DMA((n,))),
        start=start, finish=finish)


def _run_exchange(name, exchange):
    n_in, n_out = len(exchange.ins), len(exchange.out_shape)

    def body(*refs):
        ins, outs, sems = refs[:n_in], refs[n_in:n_in + n_out], refs[n_in + n_out:]
        exchange.start(ins, outs, sems)
        exchange.finish(ins, outs, sems)

    return pl.pallas_call(
        body, name=name, in_specs=[_HBM] * n_in, out_specs=[_HBM] * n_out,
        out_shape=list(exchange.out_shape), scratch_shapes=list(exchange.sems),
        input_output_aliases={k: k for k in range(n_in)} if exchange.in_place else {},
    )(*exchange.ins)


def _chip_scatter_exchange(sums):
    n = len(sums)
    offsets = [(1, 0), (0, 1), (1, 1)]

    def start(ins, recvd, sems):
        send_sems, recv_sems = sems
        x, y, c = _position()
        for a in range(n):
            for r, (ox, oy) in enumerate(offsets):
                px = 1 - x if ox else x
                py = 1 - y if oy else y
                pltpu.make_async_remote_copy(
                    src_ref=ins[a].at[2 * px + py], dst_ref=recvd[a].at[r],
                    send_sem=send_sems.at[a], recv_sem=recv_sems.at[a],
                    device_id=(px, py, c), device_id_type=MESH).start()

    def finish(ins, recvd, sems):
        del ins
        send_sems, recv_sems = sems
        x, y, c = _position()
        for a in range(n):
            pltpu.make_async_remote_copy(
                src_ref=recvd[a], dst_ref=recvd[a], send_sem=send_sems.at[a], recv_sem=recv_sems.at[a],
                device_id=(x, y, c), device_id_type=MESH).wait()

    return _Exchange(
        ins=tuple(sums),
        out_shape=tuple(jax.ShapeDtypeStruct((3,) + s.shape[1:], s.dtype) for s in sums),
        sems=(pltpu.SemaphoreType.DMA((n,)), pltpu.SemaphoreType.DMA((n,))),
        start=start, finish=finish)


def _gather_spread_exchange(shards):
    n = len(shards)

    def peers():
        x, y, c = _position()
        return (x, y, c), [(x, y, 1 - c), (1 - x, y, c), (x, 1 - y, c), (1 - x, 1 - y, c)]

    def block(ref, px, py, pc):
        return ref.at[4 * px + 2 * py + pc]

    def copy(ins, outs, sems, a, k, owner, to):
        send_sems, recv_sems, _ = sems
        return pltpu.make_async_remote_copy(
            src_ref=ins[a], dst_ref=block(outs[a], *owner),
            send_sem=send_sems.at[4 * a + k], recv_sem=recv_sems.at[4 * a + k],
            device_id=to, device_id_type=MESH)

    def start(ins, outs, sems):
        me, others = peers()
        for a in range(n):
            pltpu.make_async_copy(ins[a], block(outs[a], *me), sems[2].at[a]).start()
            for k, to in enumerate(others):
                copy(ins, outs, sems, a, k, me, to).start()

    def finish(ins, outs, sems):
        me, others = peers()
        for a in range(n):
            for k, peer in enumerate(others):
                cp = copy(ins, outs, sems, a, k, peer, peer)
                cp.wait_recv()
                cp.wait_send()
            pltpu.make_async_copy(ins[a], block(outs[a], *me), sems[2].at[a]).wait()

    return _Exchange(
        ins=tuple(shards),
        out_shape=tuple(jax.ShapeDtypeStruct((N_DEV,) + s.shape, s.dtype) for s in shards),
        sems=(pltpu.SemaphoreType.DMA((4 * n,)), pltpu.SemaphoreType.DMA((4 * n,)),
              pltpu.SemaphoreType.DMA((n,))),
        start=start, finish=finish)


def _gather_forward_exchange(partial):
    n = len(partial)

    def copy(ins, outs, sems, a, j, pc):
        x, y, c = _position()
        chips = [(1 - x, y), (x, 1 - y), (1 - x, 1 - y)]
        blk = 4 * chips[j][0] + 2 * chips[j][1] + pc
        return pltpu.make_async_remote_copy(
            src_ref=ins[a].at[blk], dst_ref=outs[a].at[blk],
            send_sem=sems[0].at[3 * a + j], recv_sem=sems[1].at[3 * a + j],
            device_id=(x, y, 1 - c), device_id_type=MESH)

    def start(ins, outs, sems):
        c = lax.axis_index("c")
        for a in range(n):
            for j in range(3):
                copy(ins, outs, sems, a, j, c).start()

    def finish(ins, outs, sems):
        c = lax.axis_index("c")
        for a in range(n):
            for j in range(3):
                copy(ins, outs, sems, a, j, 1 - c).wait_recv()
                copy(ins, outs, sems, a, j, c).wait_send()

    return _Exchange(
        ins=tuple(partial), out_shape=tuple(jax.ShapeDtypeStruct(p.shape, p.dtype) for p in partial),
        sems=(pltpu.SemaphoreType.DMA((3 * n,)), pltpu.SemaphoreType.DMA((3 * n,))),
        start=start, finish=finish, in_place=True)


def _all_reduce_small(vec):
    r = vec.shape[0]

    def body(v_ref, o_ref, gbuf, send_sems, recv_sems):
        x, y, c = _position()
        me = 4 * x + 2 * y + c
        gbuf[me] = v_ref[...]
        copies = []
        for k in range(1, N_DEV):
            ox, oy, oc = (k >> 2) & 1, (k >> 1) & 1, k & 1
            peer = (1 - x if ox else x, 1 - y if oy else y, 1 - c if oc else c)
            cp = pltpu.make_async_remote_copy(
                src_ref=gbuf.at[me], dst_ref=gbuf.at[me], send_sem=send_sems.at[k - 1],
                recv_sem=recv_sems.at[k - 1], device_id=peer, device_id_type=MESH)
            cp.start()
            copies.append(cp)
        for cp in copies:
            cp.wait()
        total = gbuf[0]
        for j in range(1, N_DEV):
            total = total + gbuf[j]
        o_ref[...] = total

    return pl.pallas_call(
        body, name="all_reduce_small",
        in_specs=[pl.BlockSpec(memory_space=pltpu.VMEM)],
        out_specs=pl.BlockSpec(memory_space=pltpu.VMEM),
        out_shape=jax.ShapeDtypeStruct(vec.shape, F32),
        scratch_shapes=[pltpu.VMEM((N_DEV, r, LANE), F32), pltpu.SemaphoreType.DMA((N_DEV - 1,)),
                        pltpu.SemaphoreType.DMA((N_DEV - 1,))],
    )(vec)


def _cols_from_blocks(g):
    nb, r, cs = g.shape
    return jnp.transpose(g, (1, 0, 2)).reshape(r, nb * cs)


def _cols_to_blocks(w):
    r, cfull = w.shape
    return jnp.transpose(w.reshape(r, N_DEV, cfull // N_DEV), (1, 0, 2))


def kernel(x, mem, ffn1_norm, ffn1_w_gate, ffn1_w_up, ffn1_w_down, mix_norm, w_in, pool_w, pool_scale, gla_w_a2, gla_b_a, gla_head_norm, w_out, xattn_norm, mem_norm, xattn_w_q, xattn_w_kv, xattn_w_o, ffn2_norm, ffn2_w_gate, ffn2_w_up, ffn2_w_down, final_norm, loss_target, m_ffn1_norm, m_ffn1_w_gate, m_ffn1_w_up, m_ffn1_w_down, m_mix_norm, m_w_in, m_pool_w, m_pool_scale, m_gla_w_a2, m_gla_b_a, m_gla_head_norm, m_w_out, m_xattn_norm, m_mem_norm, m_xattn_w_q, m_xattn_w_kv, m_xattn_w_o, m_ffn2_norm, m_ffn2_w_gate, m_ffn2_w_up, m_ffn2_w_down, m_final_norm, v_ffn1_norm, v_ffn1_w_gate, v_ffn1_w_up, v_ffn1_w_down, v_mix_norm, v_w_in, v_pool_w, v_pool_scale, v_gla_w_a2, v_gla_b_a, v_gla_head_norm, v_w_out, v_xattn_norm, v_mem_norm, v_xattn_w_q, v_xattn_w_kv, v_xattn_w_o, v_ffn2_norm, v_ffn2_w_gate, v_ffn2_w_up, v_ffn2_w_down, v_final_norm):
    weights = dict(ffn1_norm=ffn1_norm, ffn1_w_gate=ffn1_w_gate, ffn1_w_up=ffn1_w_up, ffn1_w_down=ffn1_w_down, mix_norm=mix_norm, w_in=w_in, pool_w=pool_w, pool_scale=pool_scale, gla_w_a2=gla_w_a2, gla_b_a=gla_b_a, gla_head_norm=gla_head_norm, w_out=w_out, xattn_norm=xattn_norm, mem_norm=mem_norm, xattn_w_q=xattn_w_q, xattn_w_kv=xattn_w_kv, xattn_w_o=xattn_w_o, ffn2_norm=ffn2_norm, ffn2_w_gate=ffn2_w_gate, ffn2_w_up=ffn2_w_up, ffn2_w_down=ffn2_w_down, final_norm=final_norm)
    mom1 = dict(ffn1_norm=m_ffn1_norm, ffn1_w_gate=m_ffn1_w_gate, ffn1_w_up=m_ffn1_w_up, ffn1_w_down=m_ffn1_w_down, mix_norm=m_mix_norm, w_in=m_w_in, pool_w=m_pool_w, pool_scale=m_pool_scale, gla_w_a2=m_gla_w_a2, gla_b_a=m_gla_b_a, gla_head_norm=m_gla_head_norm, w_out=m_w_out, xattn_norm=m_xattn_norm, mem_norm=m_mem_norm, xattn_w_q=m_xattn_w_q, xattn_w_kv=m_xattn_w_kv, xattn_w_o=m_xattn_w_o, ffn2_norm=m_ffn2_norm, ffn2_w_gate=m_ffn2_w_gate, ffn2_w_up=m_ffn2_w_up, ffn2_w_down=m_ffn2_w_down, final_norm=m_final_norm)
    mom2 = dict(ffn1_norm=v_ffn1_norm, ffn1_w_gate=v_ffn1_w_gate, ffn1_w_up=v_ffn1_w_up, ffn1_w_down=v_ffn1_w_down, mix_norm=v_mix_norm, w_in=v_w_in, pool_w=v_pool_w, pool_scale=v_pool_scale, gla_w_a2=v_gla_w_a2, gla_b_a=v_gla_b_a, gla_head_norm=v_gla_head_norm, w_out=v_w_out, xattn_norm=v_xattn_norm, mem_norm=v_mem_norm, xattn_w_q=v_xattn_w_q, xattn_w_kv=v_xattn_w_kv, xattn_w_o=v_xattn_w_o, ffn2_norm=v_ffn2_norm, ffn2_w_gate=v_ffn2_w_gate, ffn2_w_up=v_ffn2_w_up, ffn2_w_down=v_ffn2_w_down, final_norm=v_final_norm)
    order = list(weights.keys())

    n, d = x.shape[1], x.shape[2]
    mlen = mem.shape[1]
    x0 = x.reshape(n, d)
    memf = mem.reshape(mlen, d)
    target = loss_target.reshape(n, d)
    dpool = d // 2
    in_cols = w_in.shape[2] * N_DEV
    proj_cols = 2 * d + LANE
    rank = gla_w_a2.shape[1]

    def shard(name):
        return weights[name][0].astype(BF16)

    wg1, wu1, wd1 = _all_gather("ag_ffn1", [shard("ffn1_w_gate"), shard("ffn1_w_up"), shard("ffn1_w_down")])
    later = ["w_in", "pool_w", "gla_w_a2", "w_out", "xattn_w_q", "xattn_w_kv", "xattn_w_o",
             "ffn2_w_gate", "ffn2_w_up", "ffn2_w_down"]
    h1 = _rms_fwd("rms_ffn1", x0, ffn1_norm)
    (x1, a1, b1, h2), spread = _ffn_fwd(
        "ffn1_fwd", x0, h1, wg1, wu1, wd1, mix_norm, _gather_spread_exchange([shard(k) for k in later]))
    win_g, = _run_exchange("ag_forward_w_in", _gather_forward_exchange(spread[:1]))
    winp = jnp.pad(_cols_from_blocks(win_g), ((0, 0), (0, proj_cols - in_cols)))
    proj, (pw_g, wa2_g, wout_g, wq_g, wkv_g, wo_g, wg2, wu2, wd2) = _mm_nn(
        "mix_in", [(h2, winp)], F32, exchange=_gather_forward_exchange(spread[1:]), tn=1408)

    pw = jnp.transpose(pw_g, (1, 0, 2, 3)).reshape(len(POOL_WINDOWS), dpool // 4, dpool // 4)
    wa2p = jnp.pad(_cols_from_blocks(wa2_g), ((0, LANE - rank), (0, 0)))
    wout = wout_g.reshape(d, d)
    wq = wq_g.reshape(d, d)
    wkv = _cols_from_blocks(wkv_g)
    wo = wo_g.reshape(d, d)
    fnorm = final_norm.reshape(1, d)

    ypool = _pool_fwd(proj, pw, pool_scale)
    ygla, states = _gla_fwd(proj, wa2p, gla_b_a, gla_head_norm)
    x2, h3 = _mm_nn("mix_out", [(ypool, wout[:dpool]), (ygla, wout[dpool:])], F32, res=x1,
                    norm_gain=xattn_norm, tk=1024)
    mh = _rms_fwd("rms_mem", memf, mem_norm)
    q = _mm_nn("xattn_q", [(h3, wq)], BF16)
    kv = _mm_nn("xattn_kv", [(mh, wkv)], BF16)
    att = _attn_fwd(q, kv)
    x3, h4 = _mm_nn("xattn_o", [(att, wo)], F32, res=x2, norm_gain=ffn2_norm)
    (x4, a2, b2), _ = _ffn_fwd("ffn2_fwd", x3, h4, wg2, wu2, wd2, None, _NO_EXCHANGE)
    dx4, dx4b, g_final, loss_part = _loss_head(x4, target, fnorm)

    grads = {}
    core = lax.axis_index("c").astype(jnp.int32).reshape(1)
    chip = (2 * lax.axis_index("x") + lax.axis_index("y")).astype(jnp.int32).reshape(1)
    chip_sums = {}

    def pair_blocks(names):
        return [grads[k].reshape(N_DEV, -1, grads[k].shape[-1]) for k in names]

    def pair_sums(names, blocks, recvd):
        for k, g, r in zip(names, blocks, recvd):
            chip_sums[k] = _pair_add("rs_add_" + k, g, r, core)

    ffn2_names = ["ffn2_w_gate", "ffn2_w_up", "ffn2_w_down"]
    (da2, db2, act2), _ = _ffn_bwd_act("ffn2_bwd_act", dx4b, wd2, a2, b2, _NO_EXCHANGE)
    (grads["ffn2_w_gate"], grads["ffn2_w_up"], grads["ffn2_w_down"]), _ = _ffn_bwd_weights(
        "ffn2_bwd_weights", h4, dx4b, da2, db2, act2, _NO_EXCHANGE)
    blocks = pair_blocks(ffn2_names)
    (dx3, dx3b, g_ffn2_norm), recvd = _ffn_bwd_dh(
        "ffn2_bwd_dh", da2, db2, wg2, wu2, x3, ffn2_norm, dx4, _pair_exchange(blocks))
    pair_sums(ffn2_names, blocks, recvd)

    xattn_names = ["xattn_w_q", "xattn_w_kv", "xattn_w_o"]
    datt = _mm_nt("xattn_do", [(dx3b, wo)], BF16)
    grads["xattn_w_o"] = _mm_tn("xattn_dwo", [(att, dx3b)], BF16).reshape(N_DEV, d // N_DEV, d)
    dq, dk, dv = _attn_bwd(q, kv, datt)
    grads["xattn_w_q"] = _mm_tn("xattn_dwq", [(h3, dq)], BF16).reshape(N_DEV, d // N_DEV, d)
    dkv = jnp.concatenate([dk, dv], axis=1)
    dmh = _mm_nt("xattn_dmh", [(dkv, wkv)], F32)
    grads["xattn_w_kv"] = _cols_to_blocks(_mm_tn("xattn_dwkv", [(mh, dkv)], BF16))
    g_mem_norm = _rms_gain_grad("rms_mem_bwd", memf, mem_norm, dmh)
    blocks = pair_blocks(xattn_names)
    (dx2, dx2b, g_xattn_norm), recvd = _mm_nt_rms_bwd(
        "xattn_dh", dq, wq, x2, xattn_norm, dx3, _pair_exchange(blocks), tk=2048)
    pair_sums(xattn_names, blocks, recvd)

    dymix = _mm_nt("mix_dy", [(dx2b, wout)], F32)
    grads["w_out"] = jnp.concatenate(
        [_mm_tn("mix_dwout_pool", [(ypool, dx2b)], BF16), _mm_tn("mix_dwout_gla", [(ygla, dx2b)], BF16)],
        axis=0).reshape(N_DEV, d // N_DEV, d)
    du, g_pool_w, g_pool_scale = _pool_bwd(proj, dymix, pw, pool_scale)
    dproj, g_wa2p, g_b_a, g_head_norm = _gla_bwd(proj, dymix, du, states, wa2p, gla_b_a, gla_head_norm)
    dwin, recvd = _mm_tn("mix_dwin", [(h2, dproj)], BF16,
                         exchange=_chip_scatter_exchange([chip_sums[ffn2_names[0]]]), tn=1408)
    chip_recvd = {ffn2_names[0]: recvd[0]}
    grads["w_in"] = _cols_to_blocks(dwin[:, :in_cols])
    grads["pool_w"] = jnp.transpose(
        g_pool_w.reshape(len(POOL_WINDOWS), N_DEV, dpool // 4 // N_DEV, dpool // 4), (1, 0, 2, 3))
    grads["gla_w_a2"] = _cols_to_blocks(g_wa2p[:rank])
    mix_names = ["w_in", "pool_w", "gla_w_a2", "w_out"]
    blocks = pair_blocks(mix_names)
    (dx1, dx1b, g_mix_norm), recvd = _mm_nt_rms_bwd(
        "mix_dh", dproj, winp, x1, mix_norm, dx2, _pair_exchange(blocks), tk=1408)
    pair_sums(mix_names, blocks, recvd)

    ffn1_names = ["ffn1_w_gate", "ffn1_w_up", "ffn1_w_down"]
    (da1, db1, act1), recvd = _ffn_bwd_act(
        "ffn1_bwd_act", dx1b, wd1, a1, b1, _chip_scatter_exchange([chip_sums[k] for k in ffn2_names[1:]]))
    chip_recvd.update(zip(ffn2_names[1:], recvd))
    (grads["ffn1_w_gate"], grads["ffn1_w_up"], grads["ffn1_w_down"]), recvd = _ffn_bwd_weights(
        "ffn1_bwd_weights", h1, dx1b, da1, db1, act1,
        _chip_scatter_exchange([chip_sums[k] for k in xattn_names + mix_names]))
    chip_recvd.update(zip(xattn_names + mix_names, recvd))
    blocks = pair_blocks(ffn1_names)
    pair_sums(ffn1_names, blocks, _run_exchange("rs_pair_ffn1", _pair_exchange(blocks)))
    (dx0, _, g_ffn1_norm), recvd = _ffn_bwd_dh(
        "ffn1_bwd_dh", da1, db1, wg1, wu1, x0, ffn1_norm, dx1,
        _chip_scatter_exchange([chip_sums[k] for k in ffn1_names]))
    chip_recvd.update(zip(ffn1_names, recvd))

    small = [("ffn1_norm", g_ffn1_norm), ("mix_norm", g_mix_norm), ("pool_scale", g_pool_scale),
             ("gla_b_a", g_b_a), ("gla_head_norm", g_head_norm), ("xattn_norm", g_xattn_norm),
             ("mem_norm", g_mem_norm), ("ffn2_norm", g_ffn2_norm), ("final_norm", g_final)]
    packed = jnp.concatenate([g.reshape(-1) for _, g in small] + [loss_part.reshape(-1)])
    slab = 8 * LANE
    padded = -(-packed.shape[0] // slab) * slab
    packed = jnp.pad(packed, (0, padded - packed.shape[0])).reshape(padded // LANE, LANE)
    reduced = _all_reduce_small(packed).reshape(-1)
    small_grads = {}
    off = 0
    for name, g in small:
        small_grads[name] = reduced[off:off + g.size]
        off += g.size
    loss = reduced[off]

    out_g, out_d, out_m, out_v = {}, {}, {}, {}
    for k in order:
        w = weights[k]
        if k in small_grads:
            w2 = w.reshape(1, -1)
            parts = [small_grads[k].reshape(1, -1)]
            own_block = jnp.zeros((1,), jnp.int32)
        else:
            w2 = w.reshape(-1, w.shape[-1])
            parts = [chip_sums[k], chip_recvd[k]]
            own_block = chip
        res = _adamw("adamw_" + k, w2, mom1[k].reshape(w2.shape), mom2[k].reshape(w2.shape), parts, own_block)
        out_g[k], out_d[k], out_m[k], out_v[k] = [r.reshape(w.shape) for r in res]

    return (loss, dx0.reshape(x.shape), *[out_g[k] for k in order], *[out_d[k] for k in order],
            *[out_m[k] for k in order], *[out_v[k] for k in order])
```

```python
import collections

import jax
import jax.numpy as jnp
from jax import lax
from jax.experimental import pallas as pl
from jax.experimental.pallas import tpu as pltpu

F32 = jnp.float32
BF16 = jnp.bfloat16
MESH = pl.DeviceIdType.MESH

N_DEV = 8
CHUNK = 64
POOL_WINDOWS = (2, 4, 8, 16)
POOL_HALO = 16
HEADS = 4
GATE_TEMP = 16.0
RMS_EPS = 1e-6
LANE = 128
V7X_VMEM_BYTES = 64 * 1024 * 1024
VMEM_LIMIT = V7X_VMEM_BYTES - 8 * 1024 * 1024
GLA_ROWS = 4 * CHUNK
ELEMENTWISE_BLOCK = 512 * 1024
FFN_SUBTILES = 2
PROJ_SUBTILE_ROWS = 256

ADAM_LR = 0.001
ADAM_B1 = 0.9
ADAM_B2 = 0.999
ADAM_EPS = 1e-08
ADAM_WD = 0.01
ADAM_STEP = 10

_NN = (((1,), (0,)), ((), ()))
_NT = (((1,), (1,)), ((), ()))
_TN = (((0,), (0,)), ((), ()))


def _cparams(sem=None):
    return pltpu.CompilerParams(dimension_semantics=sem, vmem_limit_bytes=VMEM_LIMIT)


def _tile(n, pref, align):
    t = (min(pref, n) // align) * align
    while t >= align:
        if n % t == 0:
            return t
        t -= align
    return n


def _dot(a, b, dims=_NN):
    return lax.dot_general(a.astype(BF16), b.astype(BF16), dims, preferred_element_type=F32)


def _silu_parts(z):
    sig = jax.nn.sigmoid(z)
    return z * sig, sig * (1.0 + z * (1.0 - sig))


def _rms_scale(xf):
    return lax.rsqrt(jnp.mean(xf * xf, axis=-1, keepdims=True) + RMS_EPS)


def _matmul(name, pairs, a_spec, b_spec, out_shape, out_spec, grid, acc_shape, dims,
            res=None, res_spec=None, scale=None, norm_gain=None, exchange=None):
    n = len(pairs)
    nk = grid[2]

    def body(ins, outs, scr):
        a_refs, b_refs = ins[:n], ins[n:2 * n]
        pos = 2 * n
        res_ref = gain_ref = h_ref = None
        if res is not None:
            res_ref = ins[pos]
            pos += 1
        if norm_gain is not None:
            gain_ref = ins[pos]
            h_ref = outs[1]
        o_ref = outs[0]
        acc, = scr
        k = pl.program_id(2)

        @pl.when(k == 0)
        def _():
            acc[...] = jnp.zeros_like(acc)

        part = None
        for a_ref, b_ref in zip(a_refs, b_refs):
            d = _dot(a_ref[...], b_ref[...], dims)
            part = d if part is None else part + d
        acc[...] += part

        @pl.when(k == nk - 1)
        def _():
            r = acc[...]
            if scale is not None:
                r = r * scale
            if res_ref is not None:
                r = r + res_ref[...]
            o_ref[...] = r.astype(o_ref.dtype)
            if h_ref is not None:
                h_ref[...] = ((r * _rms_scale(r)) * gain_ref[...]).astype(BF16)

    ops = [p[0] for p in pairs] + [p[1] for p in pairs]
    specs = [a_spec] * n + [b_spec] * n
    if res is not None:
        ops.append(res)
        specs.append(res_spec)
    out_specs, out_shapes = [out_spec], [out_shape]
    if norm_gain is not None:
        ops.append(norm_gain)
        specs.append(pl.BlockSpec(norm_gain.shape, lambda i, j, k: (0, 0)))
        out_specs.append(out_spec)
        out_shapes.append(jax.ShapeDtypeStruct(out_shape.shape, BF16))
    outs, carried = _hosted_call(
        name, body, grid, specs, out_specs, out_shapes, [pltpu.VMEM(acc_shape, F32)], ops,
        _NO_EXCHANGE if exchange is None else exchange)
    result = outs[0] if norm_gain is None else tuple(outs)
    return result if exchange is None else (result, carried)


def _mm_nn(name, pairs, out_dtype, res=None, norm_gain=None, exchange=None, tm=1024, tn=1024, tk=2048):
    m, kd = pairs[0][0].shape
    nd = pairs[0][1].shape[1]
    if norm_gain is not None:
        tm, tn = 512, nd
    tm, tn, tk = _tile(m, tm, 16), _tile(nd, tn, LANE), _tile(kd, tk, LANE)
    return _matmul(
        name, pairs,
        pl.BlockSpec((tm, tk), lambda i, j, k: (i, k)),
        pl.BlockSpec((tk, tn), lambda i, j, k: (k, j)),
        jax.ShapeDtypeStruct((m, nd), out_dtype),
        pl.BlockSpec((tm, tn), lambda i, j, k: (i, j)),
        (m // tm, nd // tn, kd // tk), (tm, tn), _NN,
        res=res, res_spec=pl.BlockSpec((tm, tn), lambda i, j, k: (i, j)), norm_gain=norm_gain,
        exchange=exchange)


def _mm_nt(name, pairs, out_dtype, tm=1024, tn=1024, tk=2048):
    m, kd = pairs[0][0].shape
    nd = pairs[0][1].shape[0]
    tm, tn, tk = _tile(m, tm, 16), _tile(nd, tn, LANE), _tile(kd, tk, LANE)
    return _matmul(
        name, pairs,
        pl.BlockSpec((tm, tk), lambda i, j, k: (i, k)),
        pl.BlockSpec((tn, tk), lambda i, j, k: (j, k)),
        jax.ShapeDtypeStruct((m, nd), out_dtype),
        pl.BlockSpec((tm, tn), lambda i, j, k: (i, j)),
        (m // tm, nd // tn, kd // tk), (tm, tn), _NT)


def _mm_tn(name, pairs, out_dtype, exchange=None, tm=1024, tn=2048, tk=1024):
    kd, m = pairs[0][0].shape
    nd = pairs[0][1].shape[1]
    tm, tn, tk = _tile(m, tm, LANE), _tile(nd, tn, LANE), _tile(kd, tk, 16)
    return _matmul(
        name, pairs,
        pl.BlockSpec((tk, tm), lambda i, j, k: (k, i)),
        pl.BlockSpec((tk, tn), lambda i, j, k: (k, j)),
        jax.ShapeDtypeStruct((m, nd), out_dtype),
        pl.BlockSpec((tm, tn), lambda i, j, k: (i, j)),
        (m // tm, nd // tn, kd // tk), (tm, tn), _TN, exchange=exchange)


def _rms_fwd(name, x, gain):
    m, d = x.shape
    tm = _tile(m, 512, 16)

    def body(x_ref, g_ref, o_ref):
        xf = x_ref[...]
        r = lax.rsqrt(jnp.mean(xf * xf, axis=-1, keepdims=True) + RMS_EPS)
        o_ref[...] = ((xf * r) * g_ref[...]).astype(o_ref.dtype)

    return pl.pallas_call(
        body, name=name, grid=(m // tm,),
        in_specs=[pl.BlockSpec((tm, d), lambda i: (i, 0)), pl.BlockSpec((1, d), lambda i: (0, 0))],
        out_specs=pl.BlockSpec((tm, d), lambda i: (i, 0)),
        out_shape=jax.ShapeDtypeStruct((m, d), BF16),
        compiler_params=_cparams(("parallel",)),
    )(x, gain)


def _rms_bwd_rows(xf, gain, dhf):
    r = _rms_scale(xf)
    xh = xf * r
    t = dhf * gain
    dx = r * (t - xh * jnp.mean(t * xh, axis=-1, keepdims=True))
    return dx, jnp.sum(dhf * xh, axis=0, keepdims=True)


def _rms_gain_grad(name, x, gain, dh):
    m, d = x.shape
    tm = _tile(m, 512, 16)

    def body(x_ref, g_ref, dh_ref, dg_ref):
        @pl.when(pl.program_id(0) == 0)
        def _():
            dg_ref[...] = jnp.zeros_like(dg_ref)

        dg_ref[...] += _rms_bwd_rows(x_ref[...], g_ref[...], dh_ref[...])[1]

    row = pl.BlockSpec((tm, d), lambda i: (i, 0))
    vec = pl.BlockSpec((1, d), lambda i: (0, 0))
    return pl.pallas_call(
        body, name=name, grid=(m // tm,), in_specs=[row, vec, row], out_specs=vec,
        out_shape=jax.ShapeDtypeStruct((1, d), F32),
        compiler_params=_cparams(("arbitrary",)),
    )(x, gain, dh)


def _rms_bwd(name, x, gain, dh, dres, exchange):
    m, d = x.shape
    tm = _tile(m, 512, 16)

    def body(ins, outs, scr):
        del scr
        x_ref, g_ref, dh_ref, res_ref = ins
        dx_ref, dxb_ref, dg_ref = outs

        @pl.when(pl.program_id(0) == 0)
        def _():
            dg_ref[...] = jnp.zeros_like(dg_ref)

        dx, dg = _rms_bwd_rows(x_ref[...], g_ref[...], dh_ref[...])
        dx = dx + res_ref[...]
        dx_ref[...] = dx
        dxb_ref[...] = dx.astype(BF16)
        dg_ref[...] += dg

    row = pl.BlockSpec((tm, d), lambda i: (i, 0))
    vec = pl.BlockSpec((1, d), lambda i: (0, 0))
    return _hosted_call(
        name, body, (m // tm,), [row, vec, row, row], [row, row, vec],
        [jax.ShapeDtypeStruct((m, d), F32), jax.ShapeDtypeStruct((m, d), BF16),
         jax.ShapeDtypeStruct((1, d), F32)], [], (x, gain, dh, dres), exchange)


def _loss_head(x, target, gain):
    m, d = x.shape
    tm = _tile(m, 512, 16)

    def body(x_ref, t_ref, g_ref, dx_ref, dxb_ref, dg_ref, loss_ref):
        i = pl.program_id(0)

        @pl.when(i == 0)
        def _():
            dg_ref[...] = jnp.zeros_like(dg_ref)
            loss_ref[...] = jnp.zeros_like(loss_ref)

        xf = x_ref[...]
        r = lax.rsqrt(jnp.mean(xf * xf, axis=-1, keepdims=True) + RMS_EPS)
        xh = xf * r
        g = g_ref[...]
        err = xh * g - t_ref[...]
        loss_ref[...] += jnp.full(loss_ref.shape, (0.5 / d) * jnp.sum(err * err), F32)
        dy = err * (1.0 / d)
        t = dy * g
        dx = r * (t - xh * jnp.mean(t * xh, axis=-1, keepdims=True))
        dx_ref[...] = dx
        dxb_ref[...] = dx.astype(BF16)
        dg_ref[...] += jnp.sum(dy * xh, axis=0, keepdims=True)

    row = pl.BlockSpec((tm, d), lambda i: (i, 0))
    vec = pl.BlockSpec((1, d), lambda i: (0, 0))
    return pl.pallas_call(
        body, name="loss_head", grid=(m // tm,),
        in_specs=[row, row, vec],
        out_specs=[row, row, vec, pl.BlockSpec((1, LANE), lambda i: (0, 0))],
        out_shape=[jax.ShapeDtypeStruct((m, d), F32), jax.ShapeDtypeStruct((m, d), BF16),
                   jax.ShapeDtypeStruct((1, d), F32), jax.ShapeDtypeStruct((1, LANE), F32)],
        compiler_params=_cparams(("arbitrary",)),
    )(x, target, gain)


def _hosted_call(name, body, grid, in_specs, out_specs, out_shape, scratch, operands, exchange):
    n_in, n_out, n_scr = len(in_specs), len(out_specs), len(scratch)
    n_xin, n_xout = len(exchange.ins), len(exchange.out_shape)

    def full_body(*refs):
        pos = 0
        parts = []
        for cnt in (n_in, n_xin, n_out, n_xout, n_scr):
            parts.append(refs[pos:pos + cnt])
            pos += cnt
        ins, x_ins, outs, x_outs, scr = parts
        sems = refs[pos:]
        first = pl.program_id(0) == 0
        last = pl.program_id(0) == grid[0] - 1
        for ax in range(1, len(grid)):
            first = jnp.logical_and(first, pl.program_id(ax) == 0)
            last = jnp.logical_and(last, pl.program_id(ax) == grid[ax] - 1)

        @pl.when(first)
        def _():
            exchange.start(x_ins, x_outs, sems)

        body(ins, outs, scr)

        @pl.when(last)
        def _():
            exchange.finish(x_ins, x_outs, sems)

    aliases = {n_in + k: n_out + k for k in range(n_xin)} if exchange.in_place else {}
    res = pl.pallas_call(
        full_body, name=name, grid=grid,
        in_specs=list(in_specs) + [_HBM] * n_xin, out_specs=list(out_specs) + [_HBM] * n_xout,
        out_shape=list(out_shape) + list(exchange.out_shape),
        scratch_shapes=list(scratch) + list(exchange.sems),
        input_output_aliases=aliases,
        compiler_params=_cparams(("arbitrary",) * len(grid)),
    )(*operands, *exchange.ins)
    return res[:n_out], res[n_out:]


def _ffn_fwd(name, x, h, wg, wu, wd, next_gain, exchange):
    n, d = x.shape
    nb, _, fb = wg.shape
    tm = _tile(n, 512, 16 * FFN_SUBTILES)
    ts = tm // FFN_SUBTILES
    with_norm = next_gain is not None

    def body(ins, outs, scr):
        x_ref, h_ref, wg_ref, wu_ref, wd_ref = ins[:5]
        xo_ref, a_ref, b_ref = outs[:3]
        acc, = scr
        j = pl.program_id(1)

        @pl.when(j == 0)
        def _():
            acc[...] = jnp.zeros_like(acc)

        for s in range(FFN_SUBTILES):
            rows = slice(s * ts, (s + 1) * ts)
            hh = h_ref[rows, :]
            a = _dot(hh, wg_ref[...])
            b = _dot(hh, wu_ref[...])
            a_ref[rows, :] = a.astype(BF16)
            b_ref[rows, :] = b.astype(BF16)
            act = (a * jax.nn.sigmoid(a)) * b
            acc[rows, :] += _dot(act, wd_ref[...])

        @pl.when(j == nb - 1)
        def _():
            xo = x_ref[...] + 0.5 * acc[...]
            xo_ref[...] = xo
            if with_norm:
                outs[3][...] = ((xo * _rms_scale(xo)) * ins[5][...]).astype(BF16)

    row = pl.BlockSpec((tm, d), lambda i, j: (i, 0))
    w_in = pl.BlockSpec((None, d, fb), lambda i, j: (j, 0, 0))
    hid = pl.BlockSpec((None, tm, fb), lambda i, j: (j, i, 0))
    return _hosted_call(
        name, body, (n // tm, nb),
        [row, row, w_in, w_in, pl.BlockSpec((None, fb, d), lambda i, j: (j, 0, 0))]
        + ([pl.BlockSpec((1, d), lambda i, j: (0, 0))] if with_norm else []),
        [row, hid, hid] + ([row] if with_norm else []),
        [jax.ShapeDtypeStruct((n, d), F32), jax.ShapeDtypeStruct((nb, n, fb), BF16),
         jax.ShapeDtypeStruct((nb, n, fb), BF16)]
        + ([jax.ShapeDtypeStruct((n, d), BF16)] if with_norm else []),
        [pltpu.VMEM((tm, d), F32)], (x, h, wg, wu, wd) + ((next_gain,) if with_norm else ()), exchange)


def _ffn_bwd_act(name, dxb, wd, a, b, exchange):
    n, d = dxb.shape
    nb, fb, _ = wd.shape
    subtiles = 2 * FFN_SUBTILES
    tm = _tile(n, 1024, 16 * subtiles)
    ts = tm // subtiles

    def body(ins, outs, scr):
        del scr
        dx_ref, wd_ref, a_ref, b_ref = ins
        da_ref, db_ref, act_ref = outs
        for s in range(subtiles):
            rows = slice(s * ts, (s + 1) * ts)
            dact = 0.5 * _dot(dx_ref[rows, :], wd_ref[...], _NT)
            af = a_ref[rows, :].astype(F32)
            bf = b_ref[rows, :].astype(F32)
            sl, dsl = _silu_parts(af)
            act_ref[rows, :] = (sl * bf).astype(BF16)
            db_ref[rows, :] = (dact * sl).astype(BF16)
            da_ref[rows, :] = (dact * bf * dsl).astype(BF16)

    hid = pl.BlockSpec((None, tm, fb), lambda i, j: (j, i, 0))
    shp = jax.ShapeDtypeStruct((nb, n, fb), BF16)
    return _hosted_call(
        name, body, (n // tm, nb),
        [pl.BlockSpec((tm, d), lambda i, j: (i, 0)), pl.BlockSpec((None, fb, d), lambda i, j: (j, 0, 0)), hid, hid],
        [hid, hid, hid], [shp, shp, shp], [], (dxb, wd, a, b), exchange)


def _proj_rms_bwd(name, pairs, a_spec, b_spec, nk, tm, x, gain, dres, exchange):
    n, d = x.shape
    n_pairs = len(pairs)
    slab = _tile(tm, 128, 16)
    sub = _tile(tm, PROJ_SUBTILE_ROWS, 16)

    def body(ins, outs, scr):
        a_refs, b_refs = ins[:n_pairs], ins[n_pairs:2 * n_pairs]
        x_ref, g_ref, res_ref = ins[2 * n_pairs:]
        dx_ref, dxb_ref, dg_ref = outs
        del scr
        i, k = pl.program_id(0), pl.program_id(1)

        @pl.when(jnp.logical_and(i == 0, k == 0))
        def _():
            dg_ref[...] = jnp.zeros_like(dg_ref)

        @pl.when(k == 0)
        def _():
            dx_ref[...] = jnp.zeros_like(dx_ref)

        for s in range(tm // sub):
            rows = slice(s * sub, (s + 1) * sub)
            part = None
            for a_ref, b_ref in zip(a_refs, b_refs):
                t = _dot(a_ref[rows, :], b_ref[...], _NT)
                part = t if part is None else part + t
            dx_ref[rows, :] += part

        @pl.when(k == nk - 1)
        def _():
            for s in range(tm // slab):
                rows = slice(s * slab, (s + 1) * slab)
                dx, dg = _rms_bwd_rows(x_ref[rows, :], g_ref[...], dx_ref[rows, :])
                dx = dx + res_ref[rows, :]
                dx_ref[rows, :] = dx
                dxb_ref[rows, :] = dx.astype(BF16)
                dg_ref[...] += dg

    row = pl.BlockSpec((tm, d), lambda i, k: (i, 0))
    vec = pl.BlockSpec((1, d), lambda i, k: (0, 0))
    return _hosted_call(
        name, body, (n // tm, nk), [a_spec] * n_pairs + [b_spec] * n_pairs + [row, vec, row],
        [row, row, vec],
        [jax.ShapeDtypeStruct((n, d), F32), jax.ShapeDtypeStruct((n, d), BF16),
         jax.ShapeDtypeStruct((1, d), F32)],
        [], tuple(p[0] for p in pairs) + tuple(p[1] for p in pairs) + (x, gain, dres), exchange)


def _ffn_bwd_dh(name, da, db, wg, wu, exchange):
    nb, n, fb = da.shape
    d = wg.shape[1]
    tm = _tile(n, 1024, 16)
    return _matmul(
        name, [(da, wg), (db, wu)],
        pl.BlockSpec((None, tm, fb), lambda i, j, k: (k, i, 0)),
        pl.BlockSpec((None, d, fb), lambda i, j, k: (k, 0, 0)),
        jax.ShapeDtypeStruct((n, d), F32), pl.BlockSpec((tm, d), lambda i, j, k: (i, 0)),
        (n // tm, 1, nb), (tm, d), _NT, exchange=exchange)


def _mm_nt_rms_bwd(name, a, w, x, gain, dres, exchange, tk=2048):
    n, kd = a.shape
    d = w.shape[0]
    tm, tk = _tile(n, 512, 16), _tile(kd, tk, LANE)
    return _proj_rms_bwd(
        name, [(a, w)],
        pl.BlockSpec((tm, tk), lambda i, k: (i, k)),
        pl.BlockSpec((d, tk), lambda i, k: (0, k)), kd // tk, tm, x, gain, dres, exchange)


def _ffn_bwd_weights(name, h, dxb, da, db, act, exchange):
    n, d = h.shape
    nb, _, fb = da.shape
    tk = _tile(n, 512, 16)
    nk = n // tk

    def body(ins, outs, scr):
        h_ref, dx_ref, da_ref, db_ref, act_ref = ins
        dwg_ref, dwu_ref, dwd_ref = outs
        accg, accu, accd = scr
        k = pl.program_id(1)

        @pl.when(k == 0)
        def _():
            accg[...] = jnp.zeros_like(accg)
            accu[...] = jnp.zeros_like(accu)
            accd[...] = jnp.zeros_like(accd)

        hh = h_ref[...]
        accg[...] += _dot(hh, da_ref[...], _TN)
        accu[...] += _dot(hh, db_ref[...], _TN)
        accd[...] += _dot(act_ref[...], dx_ref[...], _TN)

        @pl.when(k == nk - 1)
        def _():
            dwg_ref[...] = accg[...].astype(BF16)
            dwu_ref[...] = accu[...].astype(BF16)
            dwd_ref[...] = (0.5 * accd[...]).astype(BF16)

    row = pl.BlockSpec((tk, d), lambda j, k: (k, 0))
    hid = pl.BlockSpec((None, tk, fb), lambda j, k: (j, k, 0))
    w_in = pl.BlockSpec((None, d, fb), lambda j, k: (j, 0, 0))
    w_out = pl.BlockSpec((None, fb, d), lambda j, k: (j, 0, 0))
    return _hosted_call(
        name, body, (nb, nk), [row, row, hid, hid, hid], [w_in, w_in, w_out],
        [jax.ShapeDtypeStruct((nb, d, fb), BF16), jax.ShapeDtypeStruct((nb, d, fb), BF16),
         jax.ShapeDtypeStruct((nb, fb, d), BF16)],
        [pltpu.VMEM((d, fb), F32), pltpu.VMEM((d, fb), F32), pltpu.VMEM((fb, d), F32)],
        (h, dxb, da, db, act), exchange)


def _pool_diff(ext_ref, rows, cols, width, t_idx):
    s = ext_ref[POOL_HALO:POOL_HALO + rows, cols]
    for sft in range(1, width):
        s = s + ext_ref[POOL_HALO - sft:POOL_HALO - sft + rows, cols]
    cnt = jnp.minimum(t_idx + 1, width).astype(F32)
    return s / cnt - ext_ref[POOL_HALO:POOL_HALO + rows, cols]


def _pool_fwd(proj, pool_w, pool_scale):
    n = proj.shape[0]
    dp = pool_scale.shape[1]
    c = dp // len(POOL_WINDOWS)
    tm = _tile(n, 512, POOL_HALO)
    hb = tm // POOL_HALO

    def body(u_ref, halo_ref, pw_ref, sc_ref, y_ref, ext):
        i = pl.program_id(0)
        ext[0:POOL_HALO, :] = jnp.where(i > 0, halo_ref[...], 0.0)
        ext[POOL_HALO:, :] = u_ref[...]
        t_idx = i * tm + lax.broadcasted_iota(jnp.int32, (tm, 1), 0)
        for g, width in enumerate(POOL_WINDOWS):
            cols = slice(g * c, (g + 1) * c)
            dgrp = _pool_diff(ext, tm, cols, width, t_idx)
            y_ref[:, cols] = (_dot(dgrp, pw_ref[g]) * sc_ref[:, cols]).astype(BF16)

    return pl.pallas_call(
        body, name="pool_fwd", grid=(n // tm,),
        in_specs=[pl.BlockSpec((tm, dp), lambda i: (i, 0)),
                  pl.BlockSpec((POOL_HALO, dp), lambda i: (jnp.maximum(i * hb - 1, 0), 0)),
                  pl.BlockSpec((len(POOL_WINDOWS), c, c), lambda i: (0, 0, 0)),
                  pl.BlockSpec((1, dp), lambda i: (0, 0))],
        out_specs=pl.BlockSpec((tm, dp), lambda i: (i, 0)),
        out_shape=jax.ShapeDtypeStruct((n, dp), BF16),
        scratch_shapes=[pltpu.VMEM((tm + POOL_HALO, dp), F32)],
        compiler_params=_cparams(("parallel",)),
    )(proj, proj, pool_w, pool_scale)


def _pool_bwd(proj, dymix, pool_w, pool_scale):
    n = proj.shape[0]
    dp = pool_scale.shape[1]
    ng = len(POOL_WINDOWS)
    c = dp // ng
    tm = _tile(n, 512, POOL_HALO)
    hb = tm // POOL_HALO
    nsteps = n // tm
    last_halo = n // POOL_HALO - 1

    def body(u_ref, halo_ref, dy_ref, dyn_ref, pw_ref, sc_ref, du_ref, dpw_ref, dsc_ref, ext, dyext, e_s):
        i = pl.program_id(0)

        @pl.when(i == 0)
        def _():
            dpw_ref[...] = jnp.zeros_like(dpw_ref)
            dsc_ref[...] = jnp.zeros_like(dsc_ref)

        ext[0:POOL_HALO, :] = jnp.where(i > 0, halo_ref[...], 0.0)
        ext[POOL_HALO:, :] = u_ref[...]
        dyext[0:tm, :] = dy_ref[...]
        dyext[tm:, :] = jnp.where(i < nsteps - 1, dyn_ref[...], 0.0)
        t_idx = i * tm + lax.broadcasted_iota(jnp.int32, (tm, 1), 0)
        te_idx = i * tm + lax.broadcasted_iota(jnp.int32, (tm + POOL_HALO, 1), 0)
        for g, width in enumerate(POOL_WINDOWS):
            cols = slice(g * c, (g + 1) * c)
            dgrp = _pool_diff(ext, tm, cols, width, t_idx)
            w_g = pw_ref[g]
            dys = dyext[:, cols] * sc_ref[:, cols]
            ypre = _dot(dgrp, w_g)
            dsc_ref[:, cols] += jnp.sum(dyext[0:tm, cols] * ypre, axis=0, keepdims=True)
            dpw_ref[g] += _dot(dgrp, dys[0:tm], _TN)
            dd = _dot(dys, w_g, _NT)
            e_s[...] = dd / jnp.minimum(te_idx + 1, width).astype(F32)
            acc = e_s[0:tm, :]
            for sft in range(1, width):
                acc = acc + e_s[sft:sft + tm, :]
            du_ref[:, cols] = (acc - dd[0:tm]).astype(BF16)

    return pl.pallas_call(
        body, name="pool_bwd", grid=(nsteps,),
        in_specs=[pl.BlockSpec((tm, dp), lambda i: (i, 0)),
                  pl.BlockSpec((POOL_HALO, dp), lambda i: (jnp.maximum(i * hb - 1, 0), 0)),
                  pl.BlockSpec((tm, dp), lambda i: (i, 0)),
                  pl.BlockSpec((POOL_HALO, dp), lambda i: (jnp.minimum((i + 1) * hb, last_halo), 0)),
                  pl.BlockSpec((ng, c, c), lambda i: (0, 0, 0)),
                  pl.BlockSpec((1, dp), lambda i: (0, 0))],
        out_specs=[pl.BlockSpec((tm, dp), lambda i: (i, 0)),
                   pl.BlockSpec((ng, c, c), lambda i: (0, 0, 0)),
                   pl.BlockSpec((1, dp), lambda i: (0, 0))],
        out_shape=[jax.ShapeDtypeStruct((n, dp), BF16), jax.ShapeDtypeStruct((ng, c, c), F32),
                   jax.ShapeDtypeStruct((1, dp), F32)],
        scratch_shapes=[pltpu.VMEM((tm + POOL_HALO, dp), F32), pltpu.VMEM((tm + POOL_HALO, dp), F32),
                        pltpu.VMEM((tm + POOL_HALO, c), F32)],
        compiler_params=_cparams(("arbitrary",)),
    )(proj, proj, dymix, dymix, pool_w, pool_scale)


def _chunk_cumsum(x):
    row = lax.broadcasted_iota(jnp.int32, x.shape, 0) % CHUNK
    s = 1
    while s < CHUNK:
        x = x + jnp.where(row >= s, pltpu.roll(x, s, 0), 0.0)
        s *= 2
    return x


def _chunk_ends(cum_ref, bend_ref, nc):
    for c in range(nc):
        last = cum_ref[c * CHUNK + CHUNK - 1:(c + 1) * CHUNK, :]
        bend_ref[c * CHUNK:(c + 1) * CHUNK, :] = jnp.broadcast_to(last, (CHUNK, last.shape[1]))


def _gate_logits(alr_ref, wa_ref, ba_ref):
    z = _dot(alr_ref[...], wa_ref[...]) + ba_ref[...]
    la = (jnp.minimum(z, 0.0) - jnp.log(1.0 + jnp.exp(-jnp.abs(z)))) * (1.0 / GATE_TEMP)
    return z, la


def _gla_dims(proj, head_norm):
    n, pw = proj.shape
    dv4 = head_norm.shape[1]
    dk4 = dv4 // 2
    return n, pw, dv4, dk4, dk4 // HEADS, dv4 // HEADS


def _gla_in_specs(t, dk4, dv4, rev):
    alr_blk = (2 * dv4 + 2 * dv4) // LANE
    return [pl.BlockSpec((t, dk4), lambda i: (rev(i), 2)),
            pl.BlockSpec((t, dk4), lambda i: (rev(i), 3)),
            pl.BlockSpec((t, dv4), lambda i: (rev(i), 2)),
            pl.BlockSpec((t, dv4), lambda i: (rev(i), 3)),
            pl.BlockSpec((t, LANE), lambda i: (rev(i), alr_blk))]


def _gla_fwd(proj, wa2p, b_a, head_norm):
    n, _, dv4, dk4, dk, dv = _gla_dims(proj, head_norm)
    t = _tile(n, GLA_ROWS, CHUNK)
    nc = t // CHUNK
    qscale = dk ** -0.5

    def body(q_ref, k_ref, v_ref, g_ref, alr_ref, wa_ref, ba_ref, hn_ref, y_ref, st_ref,
             state, cum_s, bend_s, o_s):
        i = pl.program_id(0)

        @pl.when(i == 0)
        def _():
            state[...] = jnp.zeros_like(state)

        _, la = _gate_logits(alr_ref, wa_ref, ba_ref)
        cum_s[...] = _chunk_cumsum(la)
        _chunk_ends(cum_s, bend_s, nc)
        kd = (k_ref[...] * jnp.exp(bend_s[...] - cum_s[...])).astype(BF16)
        qs = (q_ref[...] * qscale).astype(BF16)
        vv = v_ref[...].astype(BF16)
        units = [(c, h) for c in range(nc) for h in range(HEADS)]
        for c, h in units:
            rows = slice(c * CHUNK, (c + 1) * CHUNK)
            st_ref[c, h] = _dot(vv[rows, h * dv:(h + 1) * dv], kd[rows, h * dk:(h + 1) * dk], _TN)
        for c, h in units:
            e = jnp.exp(bend_s[c * CHUNK:c * CHUNK + 1, h * dk:(h + 1) * dk])
            s_new = state[h] * e + st_ref[c, h]
            state[h] = s_new
            st_ref[c, h] = s_new
        for c, h in units:
            rows = slice(c * CHUNK, (c + 1) * CHUNK)
            o_s[rows, h * dv:(h + 1) * dv] = _dot(qs[rows, h * dk:(h + 1) * dk], st_ref[c, h], _NT)
        for h in range(HEADS):
            hv = slice(h * dv, (h + 1) * dv)
            o = o_s[:, hv]
            gg = g_ref[:, hv]
            y_ref[:, hv] = (((o * _rms_scale(o)) * hn_ref[:, hv]) * (gg * jax.nn.sigmoid(gg))).astype(BF16)

    full = lambda shape: pl.BlockSpec(shape, lambda i: tuple(0 for _ in shape))
    return pl.pallas_call(
        body, name="gla_fwd", grid=(n // t,),
        in_specs=_gla_in_specs(t, dk4, dv4, lambda i: i)
        + [full((LANE, dk4)), full((1, dk4)), full((1, dv4))],
        out_specs=[pl.BlockSpec((t, dv4), lambda i: (i, 0)),
                   pl.BlockSpec((nc, HEADS, dv, dk), lambda i: (i, 0, 0, 0))],
        out_shape=[jax.ShapeDtypeStruct((n, dv4), BF16),
                   jax.ShapeDtypeStruct((n // CHUNK, HEADS, dv, dk), F32)],
        scratch_shapes=[pltpu.VMEM((HEADS, dv, dk), F32), pltpu.VMEM((t, dk4), F32),
                        pltpu.VMEM((t, dk4), F32), pltpu.VMEM((t, dv4), F32)],
        compiler_params=_cparams(("arbitrary",)),
    )(proj, proj, proj, proj, proj, wa2p, b_a, head_norm)


def _gla_bwd(proj, dymix, du, states, wa2p, b_a, head_norm):
    n, pw, dv4, dk4, dk, dv = _gla_dims(proj, head_norm)
    t = _tile(n, GLA_ROWS, CHUNK)
    nc = t // CHUNK
    nsteps = n // t
    qscale = dk ** -0.5
    rev = lambda i: nsteps - 1 - i

    def body(q_ref, k_ref, v_ref, g_ref, alr_ref, dy_ref, du_ref, st_ref, prev_ref, wa_ref, ba_ref, hn_ref,
             dp_ref, dwa_ref, dba_ref, dhn_ref, carry, cum_s, bend_s, gst_s, dkd_s, dee_s, o_s, do_s):
        i = pl.program_id(0)

        @pl.when(i == 0)
        def _():
            carry[...] = jnp.zeros_like(carry)
            dwa_ref[...] = jnp.zeros_like(dwa_ref)
            dba_ref[...] = jnp.zeros_like(dba_ref)
            dhn_ref[...] = jnp.zeros_like(dhn_ref)

        first_step = i == nsteps - 1
        z, la = _gate_logits(alr_ref, wa_ref, ba_ref)
        cum_s[...] = _chunk_cumsum(la)
        _chunk_ends(cum_s, bend_s, nc)
        dec = jnp.exp(bend_s[...] - cum_s[...])
        kd_f = k_ref[...] * dec
        kd = kd_f.astype(BF16)
        qs = (q_ref[...] * qscale).astype(BF16)
        vv = v_ref[...].astype(BF16)
        dp_ref[:, 0:dv4] = du_ref[...]
        units = [(c, h) for c in range(nc) for h in range(HEADS)]

        for c, h in units:
            rows = slice(c * CHUNK, (c + 1) * CHUNK)
            o_s[rows, h * dv:(h + 1) * dv] = _dot(qs[rows, h * dk:(h + 1) * dk], st_ref[c, h], _NT)
        for h in range(HEADS):
            hv = slice(h * dv, (h + 1) * dv)
            o = o_s[:, hv]
            r = _rms_scale(o)
            oh = o * r
            sl, dsl = _silu_parts(g_ref[:, hv])
            dyh = dy_ref[:, hv]
            hn = hn_ref[:, hv]
            tt = dyh * sl
            dhn_ref[:, hv] += jnp.sum(tt * oh, axis=0, keepdims=True)
            dp_ref[:, 3 * dv4 + h * dv:3 * dv4 + (h + 1) * dv] = (dyh * (oh * hn) * dsl).astype(BF16)
            tt = tt * hn
            do_s[:, hv] = (r * (tt - oh * jnp.mean(tt * oh, axis=-1, keepdims=True))).astype(BF16)
        for c, h in units:
            rows = slice(c * CHUNK, (c + 1) * CHUNK)
            hk = slice(h * dk, (h + 1) * dk)
            do = do_s[rows, h * dv:(h + 1) * dv]
            gst_s[c, h] = _dot(do, qs[rows, hk], _TN)
            dp_ref[rows, dv4 + h * dk:dv4 + (h + 1) * dk] = (_dot(do, st_ref[c, h]) * qscale).astype(BF16)

        for c, h in reversed(units):
            g_n = carry[h] + gst_s[c, h]
            gst_s[c, h] = g_n
            carry[h] = g_n * jnp.exp(bend_s[c * CHUNK:c * CHUNK + 1, h * dk:(h + 1) * dk])

        for c, h in units:
            rows = slice(c * CHUNK, (c + 1) * CHUNK)
            hk = slice(h * dk, (h + 1) * dk)
            hv = slice(h * dv, (h + 1) * dv)
            g_n = gst_s[c, h]
            if c > 0:
                s_prev = st_ref[c - 1, h]
            else:
                s_prev = jnp.where(first_step, 0.0, prev_ref[0, h])
            dkd_s[rows, hk] = _dot(vv[rows, hv], g_n)
            dp_ref[rows, 2 * dv4 + h * dv:2 * dv4 + (h + 1) * dv] = _dot(kd[rows, hk], g_n, _NT).astype(BF16)
            dee = jnp.sum(g_n * s_prev, axis=0, keepdims=True) * jnp.exp(bend_s[c * CHUNK:c * CHUNK + 1, hk])
            dee_s[rows, hk] = jnp.broadcast_to(dee, (CHUNK, dk))

        dkd = dkd_s[...]
        dp_ref[:, dv4 + dk4:dv4 + 2 * dk4] = (dkd * dec).astype(BF16)
        w = dkd * kd_f
        dla = (_chunk_cumsum(w) - w) + dee_s[...]
        dz = dla * (1.0 / GATE_TEMP) * (1.0 - jax.nn.sigmoid(z))
        dp_ref[:, 4 * dv4:4 * dv4 + LANE] = _dot(dz, wa_ref[...], _NT).astype(BF16)
        dwa_ref[...] += _dot(alr_ref[...], dz, _TN)
        dba_ref[...] += jnp.sum(dz, axis=0, keepdims=True)

    full = lambda shape: pl.BlockSpec(shape, lambda i: tuple(0 for _ in shape))
    return pl.pallas_call(
        body, name="gla_bwd", grid=(nsteps,),
        in_specs=_gla_in_specs(t, dk4, dv4, rev)
        + [pl.BlockSpec((t, dv4), lambda i: (rev(i), 1)),
           pl.BlockSpec((t, dv4), lambda i: (rev(i), 0)),
           pl.BlockSpec((nc, HEADS, dv, dk), lambda i: (rev(i), 0, 0, 0)),
           pl.BlockSpec((1, HEADS, dv, dk), lambda i: (jnp.maximum(rev(i) * nc - 1, 0), 0, 0, 0)),
           full((LANE, dk4)), full((1, dk4)), full((1, dv4))],
        out_specs=[pl.BlockSpec((t, pw), lambda i: (rev(i), 0)),
                   full((LANE, dk4)), full((1, dk4)), full((1, dv4))],
        out_shape=[jax.ShapeDtypeStruct((n, pw), BF16), jax.ShapeDtypeStruct((LANE, dk4), F32),
                   jax.ShapeDtypeStruct((1, dk4), F32), jax.ShapeDtypeStruct((1, dv4), F32)],
        scratch_shapes=[pltpu.VMEM((HEADS, dv, dk), F32), pltpu.VMEM((t, dk4), F32),
                        pltpu.VMEM((t, dk4), F32), pltpu.VMEM((nc, HEADS, dv, dk), F32),
                        pltpu.VMEM((t, dk4), F32), pltpu.VMEM((t, dk4), F32),
                        pltpu.VMEM((t, dv4), F32), pltpu.VMEM((t, dv4), BF16)],
        compiler_params=_cparams(("arbitrary",)),
    )(proj, proj, proj, proj, proj, dymix, du, states, states, wa2p, b_a, head_norm)


def _softmax_rows(q, k, scale):
    s = _dot(q, k, _NT) * scale
    p = jnp.exp(s - jnp.max(s, axis=-1, keepdims=True))
    return p / jnp.sum(p, axis=-1, keepdims=True)


def _attn_fwd(q, kv):
    n, d = q.shape
    m = kv.shape[0]
    dh = d // HEADS
    tm = _tile(n, 512, 16)
    scale = dh ** -0.5

    def body(q_ref, k_ref, v_ref, o_ref):
        for h in range(HEADS):
            hs = slice(h * dh, (h + 1) * dh)
            p = _softmax_rows(q_ref[:, hs], k_ref[:, hs], scale)
            o_ref[:, hs] = _dot(p, v_ref[:, hs]).astype(BF16)

    return pl.pallas_call(
        body, name="attn_fwd", grid=(n // tm,),
        in_specs=[pl.BlockSpec((tm, d), lambda i: (i, 0)), pl.BlockSpec((m, d), lambda i: (0, 0)),
                  pl.BlockSpec((m, d), lambda i: (0, 1))],
        out_specs=pl.BlockSpec((tm, d), lambda i: (i, 0)),
        out_shape=jax.ShapeDtypeStruct((n, d), BF16),
        compiler_params=_cparams(("parallel",)),
    )(q, kv, kv)


def _attn_bwd(q, kv, do):
    n, d = q.shape
    m = kv.shape[0]
    dh = d // HEADS
    tm = _tile(n, 512, 16)
    scale = dh ** -0.5

    def body(q_ref, k_ref, v_ref, do_ref, dq_ref, dk_ref, dv_ref):
        i = pl.program_id(0)

        @pl.when(i == 0)
        def _():
            dk_ref[...] = jnp.zeros_like(dk_ref)
            dv_ref[...] = jnp.zeros_like(dv_ref)

        for h in range(HEADS):
            hs = slice(h * dh, (h + 1) * dh)
            qh, kh, vh, doh = q_ref[:, hs], k_ref[:, hs], v_ref[:, hs], do_ref[:, hs]
            p = _softmax_rows(qh, kh, scale)
            dv_ref[:, hs] += _dot(p, doh, _TN)
            dp = _dot(doh, vh, _NT)
            ds = p * (dp - jnp.sum(dp * p, axis=-1, keepdims=True)) * scale
            dq_ref[:, hs] = _dot(ds, kh).astype(BF16)
            dk_ref[:, hs] += _dot(ds, qh, _TN)

    row = pl.BlockSpec((tm, d), lambda i: (i, 0))
    memb = pl.BlockSpec((m, d), lambda i: (0, 0))
    return pl.pallas_call(
        body, name="attn_bwd", grid=(n // tm,),
        in_specs=[row, memb, pl.BlockSpec((m, d), lambda i: (0, 1)), row],
        out_specs=[row, memb, memb],
        out_shape=[jax.ShapeDtypeStruct((n, d), BF16), jax.ShapeDtypeStruct((m, d), F32),
                   jax.ShapeDtypeStruct((m, d), F32)],
        compiler_params=_cparams(("arbitrary",)),
    )(q, kv, kv, do)


def _adamw(name, w, m, v, parts, own_block):
    r, c = w.shape
    tr = _tile(r, max(16, ELEMENTWISE_BLOCK // c), 16)
    row = pl.BlockSpec((tr, c), lambda i, o: (i, 0))
    ops, specs = [w, m, v], [row, row, row]
    for p in parts:
        if p.ndim == 2:
            ops.append(p)
            specs.append(row)
        elif p.shape[0] == 4:
            ops.append(p)
            specs.append(pl.BlockSpec((None, tr, c), lambda i, o: (o[0], i, 0)))
        else:
            for s in range(p.shape[0]):
                ops.append(p)
                specs.append(pl.BlockSpec((None, tr, c), lambda i, o, s=s: (s, i, 0)))
    n_parts = len(ops) - 3
    c1 = 1.0 - ADAM_B1 ** ADAM_STEP
    c2 = 1.0 - ADAM_B2 ** ADAM_STEP

    def body(o_ref, *refs):
        del o_ref
        w_ref, m_ref, v_ref = refs[:3]
        g_refs = refs[3:3 + n_parts]
        go_ref, d_ref, mo_ref, vo_ref = refs[3 + n_parts:]
        g = g_refs[0][...].astype(F32)
        for g_ref in g_refs[1:]:
            g = g + g_ref[...].astype(F32)
        m_new = ADAM_B1 * m_ref[...] + (1.0 - ADAM_B1) * g
        v_new = ADAM_B2 * v_ref[...] + (1.0 - ADAM_B2) * (g * g)
        m_hat = m_new / c1
        v_hat = v_new / c2
        go_ref[...] = g
        d_ref[...] = -ADAM_LR * (m_hat / (jnp.sqrt(v_hat) + ADAM_EPS) + ADAM_WD * w_ref[...])
        mo_ref[...] = m_new
        vo_ref[...] = v_new

    shp = jax.ShapeDtypeStruct((r, c), F32)
    return pl.pallas_call(
        body, name=name,
        grid_spec=pltpu.PrefetchScalarGridSpec(
            num_scalar_prefetch=1, grid=(r // tr,), in_specs=specs, out_specs=[row] * 4),
        out_shape=[shp] * 4,
        compiler_params=_cparams(("parallel",)),
    )(own_block, *ops)


def _pair_add(name, g, recvd, core):
    _, r, c = g.shape
    tr = _tile(r, max(16, ELEMENTWISE_BLOCK // c), 16)

    def body(core_ref, a_ref, b_ref, o_ref):
        del core_ref
        o_ref[...] = (a_ref[...].astype(F32) + b_ref[...].astype(F32)).astype(o_ref.dtype)

    blk = pl.BlockSpec((None, tr, c), lambda s, i, core_ref: (s, i, 0))
    mine = pl.BlockSpec((None, tr, c), lambda s, i, core_ref: (2 * s + core_ref[0], i, 0))
    return pl.pallas_call(
        body, name=name,
        grid_spec=pltpu.PrefetchScalarGridSpec(
            num_scalar_prefetch=1, grid=(4, r // tr), in_specs=[mine, blk], out_specs=blk),
        out_shape=jax.ShapeDtypeStruct(recvd.shape, g.dtype),
        compiler_params=_cparams(("parallel", "parallel")),
    )(core, g, recvd)


def _position():
    return lax.axis_index("x"), lax.axis_index("y"), lax.axis_index("c")


_HBM = pl.BlockSpec(memory_space=pltpu.HBM)

_Exchange = collections.namedtuple("_Exchange", "ins out_shape sems start finish in_place", defaults=(False,))
_NO_EXCHANGE = _Exchange((), (), (), lambda ins, outs, sems: None, lambda ins, outs, sems: None)


def _all_gather(name, shards):
    n = len(shards)

    def body(*refs):
        ins, outs = refs[:n], refs[n:2 * n]
        send_sems, recv_sems, local_sems = refs[2 * n:]
        x, y, c = _position()
        me, sibling = (x, y, c), (x, y, 1 - c)
        chips = [(1 - x, y), (x, 1 - y), (1 - x, 1 - y)]

        def block(ref, px, py, pc):
            return ref.at[4 * px + 2 * py + pc]

        def copy(a, k, owner, to, src=None):
            return pltpu.make_async_remote_copy(
                src_ref=block(outs[a], *owner) if src is None else src, dst_ref=block(outs[a], *owner),
                send_sem=send_sems.at[7 * a + k], recv_sem=recv_sems.at[7 * a + k],
                device_id=to, device_id_type=MESH)

        started = []
        for a in range(n):
            mine = pltpu.make_async_copy(ins[a], block(outs[a], *me), local_sems.at[a])
            mine.start()
            started.append(mine)
        sends = []
        for a in range(n):
            sends.append(copy(a, 0, me, sibling, src=ins[a]))
            sends += [copy(a, 1 + j, me, (*chip, c), src=ins[a]) for j, chip in enumerate(chips)]
        for cp in sends:
            cp.start()
        for j, chip in enumerate(chips):
            for a in range(n):
                copy(a, 1 + j, (*chip, c), me).wait_recv()
                fwd = copy(a, 4 + j, (*chip, c), sibling)
                fwd.start()
                sends.append(fwd)
        for a in range(n):
            copy(a, 0, sibling, me).wait_recv()
            for j, chip in enumerate(chips):
                copy(a, 4 + j, (*chip, 1 - c), me).wait_recv()
        for cp in sends:
            cp.wait_send()
        for mine in started:
            mine.wait()

    return pl.pallas_call(
        body, name=name,
        in_specs=[_HBM] * n, out_specs=[_HBM] * n,
        out_shape=[jax.ShapeDtypeStruct((N_DEV,) + s.shape, s.dtype) for s in shards],
        scratch_shapes=[pltpu.SemaphoreType.DMA((7 * n,)), pltpu.SemaphoreType.DMA((7 * n,)),
                        pltpu.SemaphoreType.DMA((n,))],
    )(*shards)


def _pair_exchange(grads):
    n = len(grads)

    def start(ins, recvd, sems):
        send_sems, recv_sems = sems
        x, y, c = _position()
        for a in range(n):
            for chip in range(4):
                pltpu.make_async_remote_copy(
                    src_ref=ins[a].at[2 * chip + (1 - c)], dst_ref=recvd[a].at[chip],
                    send_sem=send_sems.at[a], recv_sem=recv_sems.at[a],
                    device_id=(x, y, 1 - c), device_id_type=MESH).start()

    def finish(ins, recvd, sems):
        del ins
        send_sems, recv_sems = sems
        x, y, c = _position()
        for a in range(n):
            pltpu.make_async_remote_copy(
                src_ref=recvd[a], dst_ref=recvd[a], send_sem=send_sems.at[a], recv_sem=recv_sems.at[a],
                device_id=(x, y, 1 - c), device_id_type=MESH).wait()

    return _Exchange(
        ins=tuple(grads),
        out_shape=tuple(jax.ShapeDtypeStruct((4,) + g.shape[1:], g.dtype) for g in grads),
        sems=(pltpu.SemaphoreType.DMA((n,)), pltpu.SemaphoreType.DMA((n,))),
        start=start, finish=finish)


def _run_exchange(name, exchange):
    n_in, n_out = len(exchange.ins), len(exchange.out_shape)

    def body(*refs):
        ins, outs, sems = refs[:n_in], refs[n_in:n_in + n_out], refs[n_in + n_out:]
        exchange.start(ins, outs, sems)
        exchange.finish(ins, outs, sems)

    return pl.pallas_call(
        body, name=name, in_specs=[_HBM] * n_in, out_specs=[_HBM] * n_out,
        out_shape=list(exchange.out_shape), scratch_shapes=list(exchange.sems),
        input_output_aliases={k: k for k in range(n_in)} if exchange.in_place else {},
    )(*exchange.ins)


def _chip_scatter_exchange(sums):
    n = len(sums)
    offsets = [(1, 0), (0, 1), (1, 1)]

    def start(ins, recvd, sems):
        send_sems, recv_sems = sems
        x, y, c = _position()
        for a in range(n):
            for r, (ox, oy) in enumerate(offsets):
                px = 1 - x if ox else x
                py = 1 - y if oy else y
                pltpu.make_async_remote_copy(
                    src_ref=ins[a].at[2 * px + py], dst_ref=recvd[a].at[r],
                    send_sem=send_sems.at[a], recv_sem=recv_sems.at[a],
                    device_id=(px, py, c), device_id_type=MESH).start()

    def finish(ins, recvd, sems):
        del ins
        send_sems, recv_sems = sems
        x, y, c = _position()
        for a in range(n):
            pltpu.make_async_remote_copy(
                src_ref=recvd[a], dst_ref=recvd[a], send_sem=send_sems.at[a], recv_sem=recv_sems.at[a],
                device_id=(x, y, c), device_id_type=MESH).wait()

    return _Exchange(
        ins=tuple(sums),
        out_shape=tuple(jax.ShapeDtypeStruct((3,) + s.shape[1:], s.dtype) for s in sums),
        sems=(pltpu.SemaphoreType.DMA((n,)), pltpu.SemaphoreType.DMA((n,))),
        start=start, finish=finish)


def _gather_spread_exchange(shards):
    n = len(shards)

    def peers():
        x, y, c = _position()
        return (x, y, c), [(x, y, 1 - c), (1 - x, y, c), (x, 1 - y, c), (1 - x, 1 - y, c)]

    def block(ref, px, py, pc):
        return ref.at[4 * px + 2 * py + pc]

    def copy(ins, outs, sems, a, k, owner, to):
        send_sems, recv_sems, _ = sems
        return pltpu.make_async_remote_copy(
            src_ref=ins[a], dst_ref=block(outs[a], *owner),
            send_sem=send_sems.at[4 * a + k], recv_sem=recv_sems.at[4 * a + k],
            device_id=to, device_id_type=MESH)

    def start(ins, outs, sems):
        me, others = peers()
        for a in range(n):
            pltpu.make_async_copy(ins[a], block(outs[a], *me), sems[2].at[a]).start()
            for k, to in enumerate(others):
                copy(ins, outs, sems, a, k, me, to).start()

    def finish(ins, outs, sems):
        me, others = peers()
        for a in range(n):
            for k, peer in enumerate(others):
                cp = copy(ins, outs, sems, a, k, peer, peer)
                cp.wait_recv()
                cp.wait_send()
            pltpu.make_async_copy(ins[a], block(outs[a], *me), sems[2].at[a]).wait()

    return _Exchange(
        ins=tuple(shards),
        out_shape=tuple(jax.ShapeDtypeStruct((N_DEV,) + s.shape, s.dtype) for s in shards),
        sems=(pltpu.SemaphoreType.DMA((4 * n,)), pltpu.SemaphoreType.DMA((4 * n,)),
              pltpu.SemaphoreType.DMA((n,))),
        start=start, finish=finish)


def _gather_forward_exchange(partial):
    n = len(partial)

    def copy(ins, outs, sems, a, j, pc):
        x, y, c = _position()
        chips = [(1 - x, y), (x, 1 - y), (1 - x, 1 - y)]
        blk = 4 * chips[j][0] + 2 * chips[j][1] + pc
        return pltpu.make_async_remote_copy(
            src_ref=ins[a].at[blk], dst_ref=outs[a].at[blk],
            send_sem=sems[0].at[3 * a + j], recv_sem=sems[1].at[3 * a + j],
            device_id=(x, y, 1 - c), device_id_type=MESH)

    def start(ins, outs, sems):
        c = lax.axis_index("c")
        for a in range(n):
            for j in range(3):
                copy(ins, outs, sems, a, j, c).start()

    def finish(ins, outs, sems):
        c = lax.axis_index("c")
        for a in range(n):
            for j in range(3):
                copy(ins, outs, sems, a, j, 1 - c).wait_recv()
                copy(ins, outs, sems, a, j, c).wait_send()

    return _Exchange(
        ins=tuple(partial), out_shape=tuple(jax.ShapeDtypeStruct(p.shape, p.dtype) for p in partial),
        sems=(pltpu.SemaphoreType.DMA((3 * n,)), pltpu.SemaphoreType.DMA((3 * n,))),
        start=start, finish=finish, in_place=True)


def _all_reduce_small(vec):
    r = vec.shape[0]

    def body(v_ref, o_ref, gbuf, send_sems, recv_sems):
        x, y, c = _position()
        me = 4 * x + 2 * y + c
        gbuf[me] = v_ref[...]
        copies = []
        for k in range(1, N_DEV):
            ox, oy, oc = (k >> 2) & 1, (k >> 1) & 1, k & 1
            peer = (1 - x if ox else x, 1 - y if oy else y, 1 - c if oc else c)
            cp = pltpu.make_async_remote_copy(
                src_ref=gbuf.at[me], dst_ref=gbuf.at[me], send_sem=send_sems.at[k - 1],
                recv_sem=recv_sems.at[k - 1], device_id=peer, device_id_type=MESH)
            cp.start()
            copies.append(cp)
        for cp in copies:
            cp.wait()
        total = gbuf[0]
        for j in range(1, N_DEV):
            total = total + gbuf[j]
        o_ref[...] = total

    return pl.pallas_call(
        body, name="all_reduce_small",
        in_specs=[pl.BlockSpec(memory_space=pltpu.VMEM)],
        out_specs=pl.BlockSpec(memory_space=pltpu.VMEM),
        out_shape=jax.ShapeDtypeStruct(vec.shape, F32),
        scratch_shapes=[pltpu.VMEM((N_DEV, r, LANE), F32), pltpu.SemaphoreType.DMA((N_DEV - 1,)),
                        pltpu.SemaphoreType.DMA((N_DEV - 1,))],
    )(vec)


def _cols_from_blocks(g):
    nb, r, cs = g.shape
    return jnp.transpose(g, (1, 0, 2)).reshape(r, nb * cs)


def _cols_to_blocks(w):
    r, cfull = w.shape
    return jnp.transpose(w.reshape(r, N_DEV, cfull // N_DEV), (1, 0, 2))


def kernel(x, mem, ffn1_norm, ffn1_w_gate, ffn1_w_up, ffn1_w_down, mix_norm, w_in, pool_w, pool_scale, gla_w_a2, gla_b_a, gla_head_norm, w_out, xattn_norm, mem_norm, xattn_w_q, xattn_w_kv, xattn_w_o, ffn2_norm, ffn2_w_gate, ffn2_w_up, ffn2_w_down, final_norm, loss_target, m_ffn1_norm, m_ffn1_w_gate, m_ffn1_w_up, m_ffn1_w_down, m_mix_norm, m_w_in, m_pool_w, m_pool_scale, m_gla_w_a2, m_gla_b_a, m_gla_head_norm, m_w_out, m_xattn_norm, m_mem_norm, m_xattn_w_q, m_xattn_w_kv, m_xattn_w_o, m_ffn2_norm, m_ffn2_w_gate, m_ffn2_w_up, m_ffn2_w_down, m_final_norm, v_ffn1_norm, v_ffn1_w_gate, v_ffn1_w_up, v_ffn1_w_down, v_mix_norm, v_w_in, v_pool_w, v_pool_scale, v_gla_w_a2, v_gla_b_a, v_gla_head_norm, v_w_out, v_xattn_norm, v_mem_norm, v_xattn_w_q, v_xattn_w_kv, v_xattn_w_o, v_ffn2_norm, v_ffn2_w_gate, v_ffn2_w_up, v_ffn2_w_down, v_final_norm):
    weights = dict(ffn1_norm=ffn1_norm, ffn1_w_gate=ffn1_w_gate, ffn1_w_up=ffn1_w_up, ffn1_w_down=ffn1_w_down, mix_norm=mix_norm, w_in=w_in, pool_w=pool_w, pool_scale=pool_scale, gla_w_a2=gla_w_a2, gla_b_a=gla_b_a, gla_head_norm=gla_head_norm, w_out=w_out, xattn_norm=xattn_norm, mem_norm=mem_norm, xattn_w_q=xattn_w_q, xattn_w_kv=xattn_w_kv, xattn_w_o=xattn_w_o, ffn2_norm=ffn2_norm, ffn2_w_gate=ffn2_w_gate, ffn2_w_up=ffn2_w_up, ffn2_w_down=ffn2_w_down, final_norm=final_norm)
    mom1 = dict(ffn1_norm=m_ffn1_norm, ffn1_w_gate=m_ffn1_w_gate, ffn1_w_up=m_ffn1_w_up, ffn1_w_down=m_ffn1_w_down, mix_norm=m_mix_norm, w_in=m_w_in, pool_w=m_pool_w, pool_scale=m_pool_scale, gla_w_a2=m_gla_w_a2, gla_b_a=m_gla_b_a, gla_head_norm=m_gla_head_norm, w_out=m_w_out, xattn_norm=m_xattn_norm, mem_norm=m_mem_norm, xattn_w_q=m_xattn_w_q, xattn_w_kv=m_xattn_w_kv, xattn_w_o=m_xattn_w_o, ffn2_norm=m_ffn2_norm, ffn2_w_gate=m_ffn2_w_gate, ffn2_w_up=m_ffn2_w_up, ffn2_w_down=m_ffn2_w_down, final_norm=m_final_norm)
    mom2 = dict(ffn1_norm=v_ffn1_norm, ffn1_w_gate=v_ffn1_w_gate, ffn1_w_up=v_ffn1_w_up, ffn1_w_down=v_ffn1_w_down, mix_norm=v_mix_norm, w_in=v_w_in, pool_w=v_pool_w, pool_scale=v_pool_scale, gla_w_a2=v_gla_w_a2, gla_b_a=v_gla_b_a, gla_head_norm=v_gla_head_norm, w_out=v_w_out, xattn_norm=v_xattn_norm, mem_norm=v_mem_norm, xattn_w_q=v_xattn_w_q, xattn_w_kv=v_xattn_w_kv, xattn_w_o=v_xattn_w_o, ffn2_norm=v_ffn2_norm, ffn2_w_gate=v_ffn2_w_gate, ffn2_w_up=v_ffn2_w_up, ffn2_w_down=v_ffn2_w_down, final_norm=v_final_norm)
    order = list(weights.keys())

    n, d = x.shape[1], x.shape[2]
    mlen = mem.shape[1]
    x0 = x.reshape(n, d)
    memf = mem.reshape(mlen, d)
    target = loss_target.reshape(n, d)
    dpool = d // 2
    in_cols = w_in.shape[2] * N_DEV
    proj_cols = 2 * d + LANE
    rank = gla_w_a2.shape[1]

    def shard(name):
        return weights[name][0].astype(BF16)

    wg1, wu1, wd1 = _all_gather("ag_ffn1", [shard("ffn1_w_gate"), shard("ffn1_w_up"), shard("ffn1_w_down")])
    later = ["w_in", "pool_w", "gla_w_a2", "w_out", "xattn_w_q", "xattn_w_kv", "xattn_w_o",
             "ffn2_w_gate", "ffn2_w_up", "ffn2_w_down"]
    h1 = _rms_fwd("rms_ffn1", x0, ffn1_norm)
    (x1, a1, b1, h2), spread = _ffn_fwd(
        "ffn1_fwd", x0, h1, wg1, wu1, wd1, mix_norm, _gather_spread_exchange([shard(k) for k in later]))
    win_g, = _run_exchange("ag_forward_w_in", _gather_forward_exchange(spread[:1]))
    winp = jnp.pad(_cols_from_blocks(win_g), ((0, 0), (0, proj_cols - in_cols)))
    proj, (pw_g, wa2_g, wout_g, wq_g, wkv_g, wo_g, wg2, wu2, wd2) = _mm_nn(
        "mix_in", [(h2, winp)], F32, exchange=_gather_forward_exchange(spread[1:]), tn=1408)

    pw = jnp.transpose(pw_g, (1, 0, 2, 3)).reshape(len(POOL_WINDOWS), dpool // 4, dpool // 4)
    wa2p = jnp.pad(_cols_from_blocks(wa2_g), ((0, LANE - rank), (0, 0)))
    wout = wout_g.reshape(d, d)
    wq = wq_g.reshape(d, d)
    wkv = _cols_from_blocks(wkv_g)
    wo = wo_g.reshape(d, d)
    fnorm = final_norm.reshape(1, d)

    ypool = _pool_fwd(proj, pw, pool_scale)
    ygla, states = _gla_fwd(proj, wa2p, gla_b_a, gla_head_norm)
    x2, h3 = _mm_nn("mix_out", [(ypool, wout[:dpool]), (ygla, wout[dpool:])], F32, res=x1,
                    norm_gain=xattn_norm, tk=1024)
    mh = _rms_fwd("rms_mem", memf, mem_norm)
    q = _mm_nn("xattn_q", [(h3, wq)], BF16)
    kv = _mm_nn("xattn_kv", [(mh, wkv)], BF16)
    att = _attn_fwd(q, kv)
    x3, h4 = _mm_nn("xattn_o", [(att, wo)], F32, res=x2, norm_gain=ffn2_norm)
    (x4, a2, b2), _ = _ffn_fwd("ffn2_fwd", x3, h4, wg2, wu2, wd2, None, _NO_EXCHANGE)
    dx4, dx4b, g_final, loss_part = _loss_head(x4, target, fnorm)

    grads = {}
    core = lax.axis_index("c").astype(jnp.int32).reshape(1)
    chip = (2 * lax.axis_index("x") + lax.axis_index("y")).astype(jnp.int32).reshape(1)
    chip_sums = {}

    def pair_blocks(names):
        return [grads[k].reshape(N_DEV, -1, grads[k].shape[-1]) for k in names]

    def pair_sums(names, blocks, recvd):
        for k, g, r in zip(names, blocks, recvd):
            chip_sums[k] = _pair_add("rs_add_" + k, g, r, core)

    ffn2_names = ["ffn2_w_gate", "ffn2_w_up", "ffn2_w_down"]
    (da2, db2, act2), _ = _ffn_bwd_act("ffn2_bwd_act", dx4b, wd2, a2, b2, _NO_EXCHANGE)
    (grads["ffn2_w_gate"], grads["ffn2_w_up"], grads["ffn2_w_down"]), _ = _ffn_bwd_weights(
        "ffn2_bwd_weights", h4, dx4b, da2, db2, act2, _NO_EXCHANGE)
    blocks = pair_blocks(ffn2_names)
    dh4, _ = _ffn_bwd_dh("ffn2_bwd_dh", da2, db2, wg2, wu2, _NO_EXCHANGE)
    (dx3, dx3b, g_ffn2_norm), recvd = _rms_bwd(
        "rms_ffn2_bwd", x3, ffn2_norm, dh4, dx4, _pair_exchange(blocks))
    pair_sums(ffn2_names, blocks, recvd)

    xattn_names = ["xattn_w_q", "xattn_w_kv", "xattn_w_o"]
    datt = _mm_nt("xattn_do", [(dx3b, wo)], BF16)
    grads["xattn_w_o"] = _mm_tn("xattn_dwo", [(att, dx3b)], BF16).reshape(N_DEV, d // N_DEV, d)
    dq, dk, dv = _attn_bwd(q, kv, datt)
    grads["xattn_w_q"] = _mm_tn("xattn_dwq", [(h3, dq)], BF16).reshape(N_DEV, d // N_DEV, d)
    dkv = jnp.concatenate([dk, dv], axis=1)
    dmh = _mm_nt("xattn_dmh", [(dkv, wkv)], F32)
    grads["xattn_w_kv"] = _cols_to_blocks(_mm_tn("xattn_dwkv", [(mh, dkv)], BF16))
    g_mem_norm = _rms_gain_grad("rms_mem_bwd", memf, mem_norm, dmh)
    blocks = pair_blocks(xattn_names)
    (dx2, dx2b, g_xattn_norm), recvd = _mm_nt_rms_bwd(
        "xattn_dh", dq, wq, x2, xattn_norm, dx3, _pair_exchange(blocks), tk=2048)
    pair_sums(xattn_names, blocks, recvd)

    dymix = _mm_nt("mix_dy", [(dx2b, wout)], F32)
    grads["w_out"] = jnp.concatenate(
        [_mm_tn("mix_dwout_pool", [(ypool, dx2b)], BF16), _mm_tn("mix_dwout_gla", [(ygla, dx2b)], BF16)],
        axis=0).reshape(N_DEV, d // N_DEV, d)
    du, g_pool_w, g_pool_scale = _pool_bwd(proj, dymix, pw, pool_scale)
    dproj, g_wa2p, g_b_a, g_head_norm = _gla_bwd(proj, dymix, du, states, wa2p, gla_b_a, gla_head_norm)
    dwin, recvd = _mm_tn("mix_dwin", [(h2, dproj)], BF16,
                         exchange=_chip_scatter_exchange([chip_sums[ffn2_names[0]]]), tn=1408)
    chip_recvd = {ffn2_names[0]: recvd[0]}
    grads["w_in"] = _cols_to_blocks(dwin[:, :in_cols])
    grads["pool_w"] = jnp.transpose(
        g_pool_w.reshape(len(POOL_WINDOWS), N_DEV, dpool // 4 // N_DEV, dpool // 4), (1, 0, 2, 3))
    grads["gla_w_a2"] = _cols_to_blocks(g_wa2p[:rank])
    mix_names = ["w_in", "pool_w", "gla_w_a2", "w_out"]
    blocks = pair_blocks(mix_names)
    (dx1, dx1b, g_mix_norm), recvd = _mm_nt_rms_bwd(
        "mix_dh", dproj, winp, x1, mix_norm, dx2, _pair_exchange(blocks), tk=1408)
    pair_sums(mix_names, blocks, recvd)

    ffn1_names = ["ffn1_w_gate", "ffn1_w_up", "ffn1_w_down"]
    (da1, db1, act1), recvd = _ffn_bwd_act(
        "ffn1_bwd_act", dx1b, wd1, a1, b1, _chip_scatter_exchange([chip_sums[k] for k in ffn2_names[1:]]))
    chip_recvd.update(zip(ffn2_names[1:], recvd))
    (grads["ffn1_w_gate"], grads["ffn1_w_up"], grads["ffn1_w_down"]), recvd = _ffn_bwd_weights(
        "ffn1_bwd_weights", h1, dx1b, da1, db1, act1,
        _chip_scatter_exchange([chip_sums[k] for k in xattn_names + mix_names]))
    chip_recvd.update(zip(xattn_names + mix_names, recvd))
    blocks = pair_blocks(ffn1_names)
    pair_sums(ffn1_names, blocks, _run_exchange("rs_pair_ffn1", _pair_exchange(blocks)))
    dh1, recvd = _ffn_bwd_dh(
        "ffn1_bwd_dh", da1, db1, wg1, wu1, _chip_scatter_exchange([chip_sums[k] for k in ffn1_names]))
    chip_recvd.update(zip(ffn1_names, recvd))
    (dx0, _, g_ffn1_norm), _ = _rms_bwd("rms_ffn1_bwd", x0, ffn1_norm, dh1, dx1, _NO_EXCHANGE)

    small = [("ffn1_norm", g_ffn1_norm), ("mix_norm", g_mix_norm), ("pool_scale", g_pool_scale),
             ("gla_b_a", g_b_a), ("gla_head_norm", g_head_norm), ("xattn_norm", g_xattn_norm),
             ("mem_norm", g_mem_norm), ("ffn2_norm", g_ffn2_norm), ("final_norm", g_final)]
    packed = jnp.concatenate([g.reshape(-1) for _, g in small] + [loss_part.reshape(-1)])
    slab = 8 * LANE
    padded = -(-packed.shape[0] // slab) * slab
    packed = jnp.pad(packed, (0, padded - packed.shape[0])).reshape(padded // LANE, LANE)
    reduced = _all_reduce_small(packed).reshape(-1)
    small_grads = {}
    off = 0
    for name, g in small:
        small_grads[name] = reduced[off:off + g.size]
        off += g.size
    loss = reduced[off]

    out_g, out_d, out_m, out_v = {}, {}, {}, {}
    for k in order:
        w = weights[k]
        if k in small_grads:
            w2 = w.reshape(1, -1)
            parts = [small_grads[k].reshape(1, -1)]
            own_block = jnp.zeros((1,), jnp.int32)
        else:
            w2 = w.reshape(-1, w.shape[-1])
            parts = [chip_sums[k], chip_recvd[k]]
            own_block = chip
        res = _adamw("adamw_" + k, w2, mom1[k].reshape(w2.shape), mom2[k].reshape(w2.shape), parts, own_block)
        out_g[k], out_d[k], out_m[k], out_v[k] = [r.reshape(w.shape) for r in res]

    return (loss, dx0.reshape(x.shape), *[out_g[k] for k in order], *[out_d[k] for k in order],
            *[out_m[k] for k in order], *[out_v[k] for k in order])
```

```python
import collections

import jax
import jax.numpy as jnp
from jax import lax
from jax.experimental import pallas as pl
from jax.experimental.pallas import tpu as pltpu

F32 = jnp.float32
BF16 = jnp.bfloat16
MESH = pl.DeviceIdType.MESH

N_DEV = 8
CHUNK = 64
POOL_WINDOWS = (2, 4, 8, 16)
POOL_HALO = 16
HEADS = 4
GATE_TEMP = 16.0
RMS_EPS = 1e-6
LANE = 128
V7X_VMEM_BYTES = 64 * 1024 * 1024
VMEM_LIMIT = V7X_VMEM_BYTES - 8 * 1024 * 1024
GLA_ROWS = 4 * CHUNK
ELEMENTWISE_BLOCK = 512 * 1024
FFN_SUBTILES = 2
PROJ_SUBTILE_ROWS = 256

ADAM_LR = 0.001
ADAM_B1 = 0.9
ADAM_B2 = 0.999
ADAM_EPS = 1e-08
ADAM_WD = 0.01
ADAM_STEP = 10

_NN = (((1,), (0,)), ((), ()))
_NT = (((1,), (1,)), ((), ()))
_TN = (((0,), (0,)), ((), ()))


def _cparams(sem=None):
    return pltpu.CompilerParams(dimension_semantics=sem, vmem_limit_bytes=VMEM_LIMIT)


def _tile(n, pref, align):
    t = (min(pref, n) // align) * align
    while t >= align:
        if n % t == 0:
            return t
        t -= align
    return n


def _dot(a, b, dims=_NN):
    return lax.dot_general(a.astype(BF16), b.astype(BF16), dims, preferred_element_type=F32)


def _silu_parts(z):
    sig = jax.nn.sigmoid(z)
    return z * sig, sig * (1.0 + z * (1.0 - sig))


def _rms_scale(xf):
    return lax.rsqrt(jnp.mean(xf * xf, axis=-1, keepdims=True) + RMS_EPS)


def _matmul(name, pairs, a_spec, b_spec, out_shape, out_spec, grid, acc_shape, dims,
            res=None, res_spec=None, scale=None, norm_gain=None, exchange=None):
    n = len(pairs)
    nk = grid[2]

    def body(ins, outs, scr):
        a_refs, b_refs = ins[:n], ins[n:2 * n]
        pos = 2 * n
        res_ref = gain_ref = h_ref = None
        if res is not None:
            res_ref = ins[pos]
            pos += 1
        if norm_gain is not None:
            gain_ref = ins[pos]
            h_ref = outs[1]
        o_ref = outs[0]
        acc, = scr
        k = pl.program_id(2)

        @pl.when(k == 0)
        def _():
            acc[...] = jnp.zeros_like(acc)

        part = None
        for a_ref, b_ref in zip(a_refs, b_refs):
            d = _dot(a_ref[...], b_ref[...], dims)
            part = d if part is None else part + d
        acc[...] += part

        @pl.when(k == nk - 1)
        def _():
            r = acc[...]
            if scale is not None:
                r = r * scale
            if res_ref is not None:
                r = r + res_ref[...]
            o_ref[...] = r.astype(o_ref.dtype)
            if h_ref is not None:
                h_ref[...] = ((r * _rms_scale(r)) * gain_ref[...]).astype(BF16)

    ops = [p[0] for p in pairs] + [p[1] for p in pairs]
    specs = [a_spec] * n + [b_spec] * n
    if res is not None:
        ops.append(res)
        specs.append(res_spec)
    out_specs, out_shapes = [out_spec], [out_shape]
    if norm_gain is not None:
        ops.append(norm_gain)
        specs.append(pl.BlockSpec(norm_gain.shape, lambda i, j, k: (0, 0)))
        out_specs.append(out_spec)
        out_shapes.append(jax.ShapeDtypeStruct(out_shape.shape, BF16))
    outs, carried = _hosted_call(
        name, body, grid, specs, out_specs, out_shapes, [pltpu.VMEM(acc_shape, F32)], ops,
        _NO_EXCHANGE if exchange is None else exchange)
    result = outs[0] if norm_gain is None else tuple(outs)
    return result if exchange is None else (result, carried)


def _mm_nn(name, pairs, out_dtype, res=None, norm_gain=None, exchange=None, tm=1024, tn=1024, tk=2048):
    m, kd = pairs[0][0].shape
    nd = pairs[0][1].shape[1]
    if norm_gain is not None:
        tm, tn = 512, nd
    tm, tn, tk = _tile(m, tm, 16), _tile(nd, tn, LANE), _tile(kd, tk, LANE)
    return _matmul(
        name, pairs,
        pl.BlockSpec((tm, tk), lambda i, j, k: (i, k)),
        pl.BlockSpec((tk, tn), lambda i, j, k: (k, j)),
        jax.ShapeDtypeStruct((m, nd), out_dtype),
        pl.BlockSpec((tm, tn), lambda i, j, k: (i, j)),
        (m // tm, nd // tn, kd // tk), (tm, tn), _NN,
        res=res, res_spec=pl.BlockSpec((tm, tn), lambda i, j, k: (i, j)), norm_gain=norm_gain,
        exchange=exchange)


def _mm_nt(name, pairs, out_dtype, tm=1024, tn=1024, tk=2048):
    m, kd = pairs[0][0].shape
    nd = pairs[0][1].shape[0]
    tm, tn, tk = _tile(m, tm, 16), _tile(nd, tn, LANE), _tile(kd, tk, LANE)
    return _matmul(
        name, pairs,
        pl.BlockSpec((tm, tk), lambda i, j, k: (i, k)),
        pl.BlockSpec((tn, tk), lambda i, j, k: (j, k)),
        jax.ShapeDtypeStruct((m, nd), out_dtype),
        pl.BlockSpec((tm, tn), lambda i, j, k: (i, j)),
        (m // tm, nd // tn, kd // tk), (tm, tn), _NT)


def _mm_tn(name, pairs, out_dtype, exchange=None, tm=1024, tn=2048, tk=1024):
    kd, m = pairs[0][0].shape
    nd = pairs[0][1].shape[1]
    tm, tn, tk = _tile(m, tm, LANE), _tile(nd, tn, LANE), _tile(kd, tk, 16)
    return _matmul(
        name, pairs,
        pl.BlockSpec((tk, tm), lambda i, j, k: (k, i)),
        pl.BlockSpec((tk, tn), lambda i, j, k: (k, j)),
        jax.ShapeDtypeStruct((m, nd), out_dtype),
        pl.BlockSpec((tm, tn), lambda i, j, k: (i, j)),
        (m // tm, nd // tn, kd // tk), (tm, tn), _TN, exchange=exchange)


def _rms_fwd(name, x, gain):
    m, d = x.shape
    tm = _tile(m, 512, 16)

    def body(x_ref, g_ref, o_ref):
        xf = x_ref[...]
        r = lax.rsqrt(jnp.mean(xf * xf, axis=-1, keepdims=True) + RMS_EPS)
        o_ref[...] = ((xf * r) * g_ref[...]).astype(o_ref.dtype)

    return pl.pallas_call(
        body, name=name, grid=(m // tm,),
        in_specs=[pl.BlockSpec((tm, d), lambda i: (i, 0)), pl.BlockSpec((1, d), lambda i: (0, 0))],
        out_specs=pl.BlockSpec((tm, d), lambda i: (i, 0)),
        out_shape=jax.ShapeDtypeStruct((m, d), BF16),
        compiler_params=_cparams(("parallel",)),
    )(x, gain)


def _rms_bwd_rows(xf, gain, dhf):
    r = _rms_scale(xf)
    xh = xf * r
    t = dhf * gain
    dx = r * (t - xh * jnp.mean(t * xh, axis=-1, keepdims=True))
    return dx, jnp.sum(dhf * xh, axis=0, keepdims=True)


def _rms_gain_grad(name, x, gain, dh):
    m, d = x.shape
    tm = _tile(m, 512, 16)

    def body(x_ref, g_ref, dh_ref, dg_ref):
        @pl.when(pl.program_id(0) == 0)
        def _():
            dg_ref[...] = jnp.zeros_like(dg_ref)

        dg_ref[...] += _rms_bwd_rows(x_ref[...], g_ref[...], dh_ref[...])[1]

    row = pl.BlockSpec((tm, d), lambda i: (i, 0))
    vec = pl.BlockSpec((1, d), lambda i: (0, 0))
    return pl.pallas_call(
        body, name=name, grid=(m // tm,), in_specs=[row, vec, row], out_specs=vec,
        out_shape=jax.ShapeDtypeStruct((1, d), F32),
        compiler_params=_cparams(("arbitrary",)),
    )(x, gain, dh)


def _rms_bwd(name, x, gain, dh, dres, exchange):
    m, d = x.shape
    tm = _tile(m, 512, 16)

    def body(ins, outs, scr):
        del scr
        x_ref, g_ref, dh_ref, res_ref = ins
        dx_ref, dxb_ref, dg_ref = outs

        @pl.when(pl.program_id(0) == 0)
        def _():
            dg_ref[...] = jnp.zeros_like(dg_ref)

        dx, dg = _rms_bwd_rows(x_ref[...], g_ref[...], dh_ref[...])
        dx = dx + res_ref[...]
        dx_ref[...] = dx
        dxb_ref[...] = dx.astype(BF16)
        dg_ref[...] += dg

    row = pl.BlockSpec((tm, d), lambda i: (i, 0))
    vec = pl.BlockSpec((1, d), lambda i: (0, 0))
    return _hosted_call(
        name, body, (m // tm,), [row, vec, row, row], [row, row, vec],
        [jax.ShapeDtypeStruct((m, d), F32), jax.ShapeDtypeStruct((m, d), BF16),
         jax.ShapeDtypeStruct((1, d), F32)], [], (x, gain, dh, dres), exchange)


def _loss_head(x, target, gain):
    m, d = x.shape
    tm = _tile(m, 512, 16)

    def body(x_ref, t_ref, g_ref, dx_ref, dxb_ref, dg_ref, loss_ref):
        i = pl.program_id(0)

        @pl.when(i == 0)
        def _():
            dg_ref[...] = jnp.zeros_like(dg_ref)
            loss_ref[...] = jnp.zeros_like(loss_ref)

        xf = x_ref[...]
        r = lax.rsqrt(jnp.mean(xf * xf, axis=-1, keepdims=True) + RMS_EPS)
        xh = xf * r
        g = g_ref[...]
        err = xh * g - t_ref[...]
        loss_ref[...] += jnp.full(loss_ref.shape, (0.5 / d) * jnp.sum(err * err), F32)
        dy = err * (1.0 / d)
        t = dy * g
        dx = r * (t - xh * jnp.mean(t * xh, axis=-1, keepdims=True))
        dx_ref[...] = dx
        dxb_ref[...] = dx.astype(BF16)
        dg_ref[...] += jnp.sum(dy * xh, axis=0, keepdims=True)

    row = pl.BlockSpec((tm, d), lambda i: (i, 0))
    vec = pl.BlockSpec((1, d), lambda i: (0, 0))
    return pl.pallas_call(
        body, name="loss_head", grid=(m // tm,),
        in_specs=[row, row, vec],
        out_specs=[row, row, vec, pl.BlockSpec((1, LANE), lambda i: (0, 0))],
        out_shape=[jax.ShapeDtypeStruct((m, d), F32), jax.ShapeDtypeStruct((m, d), BF16),
                   jax.ShapeDtypeStruct((1, d), F32), jax.ShapeDtypeStruct((1, LANE), F32)],
        compiler_params=_cparams(("arbitrary",)),
    )(x, target, gain)


def _hosted_call(name, body, grid, in_specs, out_specs, out_shape, scratch, operands, exchange):
    n_in, n_out, n_scr = len(in_specs), len(out_specs), len(scratch)
    n_xin, n_xout = len(exchange.ins), len(exchange.out_shape)

    def full_body(*refs):
        pos = 0
        parts = []
        for cnt in (n_in, n_xin, n_out, n_xout, n_scr):
            parts.append(refs[pos:pos + cnt])
            pos += cnt
        ins, x_ins, outs, x_outs, scr = parts
        sems = refs[pos:]
        first = pl.program_id(0) == 0
        last = pl.program_id(0) == grid[0] - 1
        for ax in range(1, len(grid)):
            first = jnp.logical_and(first, pl.program_id(ax) == 0)
            last = jnp.logical_and(last, pl.program_id(ax) == grid[ax] - 1)

        @pl.when(first)
        def _():
            exchange.start(x_ins, x_outs, sems)

        body(ins, outs, scr)

        @pl.when(last)
        def _():
            exchange.finish(x_ins, x_outs, sems)

    aliases = {n_in + k: n_out + k for k in range(n_xin)} if exchange.in_place else {}
    res = pl.pallas_call(
        full_body, name=name, grid=grid,
        in_specs=list(in_specs) + [_HBM] * n_xin, out_specs=list(out_specs) + [_HBM] * n_xout,
        out_shape=list(out_shape) + list(exchange.out_shape),
        scratch_shapes=list(scratch) + list(exchange.sems),
        input_output_aliases=aliases,
        compiler_params=_cparams(("arbitrary",) * len(grid)),
    )(*operands, *exchange.ins)
    return res[:n_out], res[n_out:]


def _ffn_fwd(name, x, h, wg, wu, wd, next_gain, exchange):
    n, d = x.shape
    nb, _, fb = wg.shape
    tm = _tile(n, 512, 16 * FFN_SUBTILES)
    ts = tm // FFN_SUBTILES
    with_norm = next_gain is not None

    def body(ins, outs, scr):
        x_ref, h_ref, wg_ref, wu_ref, wd_ref = ins[:5]
        xo_ref, a_ref, b_ref = outs[:3]
        acc, = scr
        j = pl.program_id(1)

        @pl.when(j == 0)
        def _():
            acc[...] = jnp.zeros_like(acc)

        for s in range(FFN_SUBTILES):
            rows = slice(s * ts, (s + 1) * ts)
            hh = h_ref[rows, :]
            a = _dot(hh, wg_ref[...])
            b = _dot(hh, wu_ref[...])
            a_ref[rows, :] = a.astype(BF16)
            b_ref[rows, :] = b.astype(BF16)
            act = (a * jax.nn.sigmoid(a)) * b
            acc[rows, :] += _dot(act, wd_ref[...])

        @pl.when(j == nb - 1)
        def _():
            xo = x_ref[...] + 0.5 * acc[...]
            xo_ref[...] = xo
            if with_norm:
                outs[3][...] = ((xo * _rms_scale(xo)) * ins[5][...]).astype(BF16)

    row = pl.BlockSpec((tm, d), lambda i, j: (i, 0))
    w_in = pl.BlockSpec((None, d, fb), lambda i, j: (j, 0, 0))
    hid = pl.BlockSpec((None, tm, fb), lambda i, j: (j, i, 0))
    return _hosted_call(
        name, body, (n // tm, nb),
        [row, row, w_in, w_in, pl.BlockSpec((None, fb, d), lambda i, j: (j, 0, 0))]
        + ([pl.BlockSpec((1, d), lambda i, j: (0, 0))] if with_norm else []),
        [row, hid, hid] + ([row] if with_norm else []),
        [jax.ShapeDtypeStruct((n, d), F32), jax.ShapeDtypeStruct((nb, n, fb), BF16),
         jax.ShapeDtypeStruct((nb, n, fb), BF16)]
        + ([jax.ShapeDtypeStruct((n, d), BF16)] if with_norm else []),
        [pltpu.VMEM((tm, d), F32)], (x, h, wg, wu, wd) + ((next_gain,) if with_norm else ()), exchange)


def _ffn_bwd_act(name, dxb, wd, a, b, exchange):
    n, d = dxb.shape
    nb, fb, _ = wd.shape
    subtiles = 2 * FFN_SUBTILES
    tm = _tile(n, 1024, 16 * subtiles)
    ts = tm // subtiles

    def body(ins, outs, scr):
        del scr
        dx_ref, wd_ref, a_ref, b_ref = ins
        da_ref, db_ref, act_ref = outs
        for s in range(subtiles):
            rows = slice(s * ts, (s + 1) * ts)
            dact = 0.5 * _dot(dx_ref[rows, :], wd_ref[...], _NT)
            af = a_ref[rows, :].astype(F32)
            bf = b_ref[rows, :].astype(F32)
            sl, dsl = _silu_parts(af)
            act_ref[rows, :] = (sl * bf).astype(BF16)
            db_ref[rows, :] = (dact * sl).astype(BF16)
            da_ref[rows, :] = (dact * bf * dsl).astype(BF16)

    hid = pl.BlockSpec((None, tm, fb), lambda i, j: (j, i, 0))
    shp = jax.ShapeDtypeStruct((nb, n, fb), BF16)
    return _hosted_call(
        name, body, (n // tm, nb),
        [pl.BlockSpec((tm, d), lambda i, j: (i, 0)), pl.BlockSpec((None, fb, d), lambda i, j: (j, 0, 0)), hid, hid],
        [hid, hid, hid], [shp, shp, shp], [], (dxb, wd, a, b), exchange)


def _proj_rms_bwd(name, pairs, a_spec, b_spec, nk, tm, x, gain, dres, exchange):
    n, d = x.shape
    n_pairs = len(pairs)
    slab = _tile(tm, 128, 16)
    sub = _tile(tm, PROJ_SUBTILE_ROWS, 16)

    def body(ins, outs, scr):
        a_refs, b_refs = ins[:n_pairs], ins[n_pairs:2 * n_pairs]
        x_ref, g_ref, res_ref = ins[2 * n_pairs:]
        dx_ref, dxb_ref, dg_ref = outs
        del scr
        i, k = pl.program_id(0), pl.program_id(1)

        @pl.when(jnp.logical_and(i == 0, k == 0))
        def _():
            dg_ref[...] = jnp.zeros_like(dg_ref)

        @pl.when(k == 0)
        def _():
            dx_ref[...] = jnp.zeros_like(dx_ref)

        for s in range(tm // sub):
            rows = slice(s * sub, (s + 1) * sub)
            part = None
            for a_ref, b_ref in zip(a_refs, b_refs):
                t = _dot(a_ref[rows, :], b_ref[...], _NT)
                part = t if part is None else part + t
            dx_ref[rows, :] += part

        @pl.when(k == nk - 1)
        def _():
            for s in range(tm // slab):
                rows = slice(s * slab, (s + 1) * slab)
                dx, dg = _rms_bwd_rows(x_ref[rows, :], g_ref[...], dx_ref[rows, :])
                dx = dx + res_ref[rows, :]
                dx_ref[rows, :] = dx
                dxb_ref[rows, :] = dx.astype(BF16)
                dg_ref[...] += dg

    row = pl.BlockSpec((tm, d), lambda i, k: (i, 0))
    vec = pl.BlockSpec((1, d), lambda i, k: (0, 0))
    return _hosted_call(
        name, body, (n // tm, nk), [a_spec] * n_pairs + [b_spec] * n_pairs + [row, vec, row],
        [row, row, vec],
        [jax.ShapeDtypeStruct((n, d), F32), jax.ShapeDtypeStruct((n, d), BF16),
         jax.ShapeDtypeStruct((1, d), F32)],
        [], tuple(p[0] for p in pairs) + tuple(p[1] for p in pairs) + (x, gain, dres), exchange)


def _ffn_bwd_dh(name, da, db, wg, wu, exchange):
    nb, n, fb = da.shape
    d = wg.shape[1]
    tm = _tile(n, 1024, 16)
    return _matmul(
        name, [(da, wg), (db, wu)],
        pl.BlockSpec((None, tm, fb), lambda i, j, k: (k, i, 0)),
        pl.BlockSpec((None, d, fb), lambda i, j, k: (k, 0, 0)),
        jax.ShapeDtypeStruct((n, d), F32), pl.BlockSpec((tm, d), lambda i, j, k: (i, 0)),
        (n // tm, 1, nb), (tm, d), _NT, exchange=exchange)


def _mm_nt_rms_bwd(name, a, w, x, gain, dres, exchange, tk=2048):
    n, kd = a.shape
    d = w.shape[0]
    tm, tk = _tile(n, 512, 16), _tile(kd, tk, LANE)
    return _proj_rms_bwd(
        name, [(a, w)],
        pl.BlockSpec((tm, tk), lambda i, k: (i, k)),
        pl.BlockSpec((d, tk), lambda i, k: (0, k)), kd // tk, tm, x, gain, dres, exchange)


def _ffn_bwd_weights(name, h, dxb, da, db, act, exchange):
    n, d = h.shape
    nb, _, fb = da.shape
    tk = _tile(n, 512, 16)
    nk = n // tk

    def body(ins, outs, scr):
        h_ref, dx_ref, da_ref, db_ref, act_ref = ins
        dwg_ref, dwu_ref, dwd_ref = outs
        accg, accu, accd = scr
        k = pl.program_id(1)

        @pl.when(k == 0)
        def _():
            accg[...] = jnp.zeros_like(accg)
            accu[...] = jnp.zeros_like(accu)
            accd[...] = jnp.zeros_like(accd)

        hh = h_ref[...]
        accg[...] += _dot(hh, da_ref[...], _TN)
        accu[...] += _dot(hh, db_ref[...], _TN)
        accd[...] += _dot(act_ref[...], dx_ref[...], _TN)

        @pl.when(k == nk - 1)
        def _():
            dwg_ref[...] = accg[...].astype(BF16)
            dwu_ref[...] = accu[...].astype(BF16)
            dwd_ref[...] = (0.5 * accd[...]).astype(BF16)

    row = pl.BlockSpec((tk, d), lambda j, k: (k, 0))
    hid = pl.BlockSpec((None, tk, fb), lambda j, k: (j, k, 0))
    w_in = pl.BlockSpec((None, d, fb), lambda j, k: (j, 0, 0))
    w_out = pl.BlockSpec((None, fb, d), lambda j, k: (j, 0, 0))
    return _hosted_call(
        name, body, (nb, nk), [row, row, hid, hid, hid], [w_in, w_in, w_out],
        [jax.ShapeDtypeStruct((nb, d, fb), BF16), jax.ShapeDtypeStruct((nb, d, fb), BF16),
         jax.ShapeDtypeStruct((nb, fb, d), BF16)],
        [pltpu.VMEM((d, fb), F32), pltpu.VMEM((d, fb), F32), pltpu.VMEM((fb, d), F32)],
        (h, dxb, da, db, act), exchange)


def _pool_diff(ext_ref, rows, cols, width, t_idx):
    s = ext_ref[POOL_HALO:POOL_HALO + rows, cols]
    for sft in range(1, width):
        s = s + ext_ref[POOL_HALO - sft:POOL_HALO - sft + rows, cols]
    cnt = jnp.minimum(t_idx + 1, width).astype(F32)
    return s / cnt - ext_ref[POOL_HALO:POOL_HALO + rows, cols]


def _pool_fwd(proj, pool_w, pool_scale):
    n = proj.shape[0]
    dp = pool_scale.shape[1]
    c = dp // len(POOL_WINDOWS)
    tm = _tile(n, 512, POOL_HALO)
    hb = tm // POOL_HALO

    def body(u_ref, halo_ref, pw_ref, sc_ref, y_ref, ext):
        i = pl.program_id(0)
        ext[0:POOL_HALO, :] = jnp.where(i > 0, halo_ref[...], 0.0)
        ext[POOL_HALO:, :] = u_ref[...]
        t_idx = i * tm + lax.broadcasted_iota(jnp.int32, (tm, 1), 0)
        for g, width in enumerate(POOL_WINDOWS):
            cols = slice(g * c, (g + 1) * c)
            dgrp = _pool_diff(ext, tm, cols, width, t_idx)
            y_ref[:, cols] = (_dot(dgrp, pw_ref[g]) * sc_ref[:, cols]).astype(BF16)

    return pl.pallas_call(
        body, name="pool_fwd", grid=(n // tm,),
        in_specs=[pl.BlockSpec((tm, dp), lambda i: (i, 0)),
                  pl.BlockSpec((POOL_HALO, dp), lambda i: (jnp.maximum(i * hb - 1, 0), 0)),
                  pl.BlockSpec((len(POOL_WINDOWS), c, c), lambda i: (0, 0, 0)),
                  pl.BlockSpec((1, dp), lambda i: (0, 0))],
        out_specs=pl.BlockSpec((tm, dp), lambda i: (i, 0)),
        out_shape=jax.ShapeDtypeStruct((n, dp), BF16),
        scratch_shapes=[pltpu.VMEM((tm + POOL_HALO, dp), F32)],
        compiler_params=_cparams(("parallel",)),
    )(proj, proj, pool_w, pool_scale)


def _pool_bwd(proj, dymix, pool_w, pool_scale):
    n = proj.shape[0]
    dp = pool_scale.shape[1]
    ng = len(POOL_WINDOWS)
    c = dp // ng
    tm = _tile(n, 512, POOL_HALO)
    hb = tm // POOL_HALO
    nsteps = n // tm
    last_halo = n // POOL_HALO - 1

    def body(u_ref, halo_ref, dy_ref, dyn_ref, pw_ref, sc_ref, du_ref, dpw_ref, dsc_ref, ext, dyext, e_s):
        i = pl.program_id(0)

        @pl.when(i == 0)
        def _():
            dpw_ref[...] = jnp.zeros_like(dpw_ref)
            dsc_ref[...] = jnp.zeros_like(dsc_ref)

        ext[0:POOL_HALO, :] = jnp.where(i > 0, halo_ref[...], 0.0)
        ext[POOL_HALO:, :] = u_ref[...]
        dyext[0:tm, :] = dy_ref[...]
        dyext[tm:, :] = jnp.where(i < nsteps - 1, dyn_ref[...], 0.0)
        t_idx = i * tm + lax.broadcasted_iota(jnp.int32, (tm, 1), 0)
        te_idx = i * tm + lax.broadcasted_iota(jnp.int32, (tm + POOL_HALO, 1), 0)
        for g, width in enumerate(POOL_WINDOWS):
            cols = slice(g * c, (g + 1) * c)
            dgrp = _pool_diff(ext, tm, cols, width, t_idx)
            w_g = pw_ref[g]
            dys = dyext[:, cols] * sc_ref[:, cols]
            ypre = _dot(dgrp, w_g)
            dsc_ref[:, cols] += jnp.sum(dyext[0:tm, cols] * ypre, axis=0, keepdims=True)
            dpw_ref[g] += _dot(dgrp, dys[0:tm], _TN)
            dd = _dot(dys, w_g, _NT)
            e_s[...] = dd / jnp.minimum(te_idx + 1, width).astype(F32)
            acc = e_s[0:tm, :]
            for sft in range(1, width):
                acc = acc + e_s[sft:sft + tm, :]
            du_ref[:, cols] = (acc - dd[0:tm]).astype(BF16)

    return pl.pallas_call(
        body, name="pool_bwd", grid=(nsteps,),
        in_specs=[pl.BlockSpec((tm, dp), lambda i: (i, 0)),
                  pl.BlockSpec((POOL_HALO, dp), lambda i: (jnp.maximum(i * hb - 1, 0), 0)),
                  pl.BlockSpec((tm, dp), lambda i: (i, 0)),
                  pl.BlockSpec((POOL_HALO, dp), lambda i: (jnp.minimum((i + 1) * hb, last_halo), 0)),
                  pl.BlockSpec((ng, c, c), lambda i: (0, 0, 0)),
                  pl.BlockSpec((1, dp), lambda i: (0, 0))],
        out_specs=[pl.BlockSpec((tm, dp), lambda i: (i, 0)),
                   pl.BlockSpec((ng, c, c), lambda i: (0, 0, 0)),
                   pl.BlockSpec((1, dp), lambda i: (0, 0))],
        out_shape=[jax.ShapeDtypeStruct((n, dp), BF16), jax.ShapeDtypeStruct((ng, c, c), F32),
                   jax.ShapeDtypeStruct((1, dp), F32)],
        scratch_shapes=[pltpu.VMEM((tm + POOL_HALO, dp), F32), pltpu.VMEM((tm + POOL_HALO, dp), F32),
                        pltpu.VMEM((tm + POOL_HALO, c), F32)],
        compiler_params=_cparams(("arbitrary",)),
    )(proj, proj, dymix, dymix, pool_w, pool_scale)


def _chunk_cumsum(x):
    row = lax.broadcasted_iota(jnp.int32, x.shape, 0) % CHUNK
    s = 1
    while s < CHUNK:
        x = x + jnp.where(row >= s, pltpu.roll(x, s, 0), 0.0)
        s *= 2
    return x


def _chunk_ends(cum_ref, bend_ref, nc):
    for c in range(nc):
        last = cum_ref[c * CHUNK + CHUNK - 1:(c + 1) * CHUNK, :]
        bend_ref[c * CHUNK:(c + 1) * CHUNK, :] = jnp.broadcast_to(last, (CHUNK, last.shape[1]))


def _gate_logits(alr_ref, wa_ref, ba_ref):
    z = _dot(alr_ref[...], wa_ref[...]) + ba_ref[...]
    la = (jnp.minimum(z, 0.0) - jnp.log(1.0 + jnp.exp(-jnp.abs(z)))) * (1.0 / GATE_TEMP)
    return z, la


def _gla_dims(proj, head_norm):
    n, pw = proj.shape
    dv4 = head_norm.shape[1]
    dk4 = dv4 // 2
    return n, pw, dv4, dk4, dk4 // HEADS, dv4 // HEADS


def _gla_in_specs(t, dk4, dv4, rev):
    alr_blk = (2 * dv4 + 2 * dv4) // LANE
    return [pl.BlockSpec((t, dk4), lambda i: (rev(i), 2)),
            pl.BlockSpec((t, dk4), lambda i: (rev(i), 3)),
            pl.BlockSpec((t, dv4), lambda i: (rev(i), 2)),
            pl.BlockSpec((t, dv4), lambda i: (rev(i), 3)),
            pl.BlockSpec((t, LANE), lambda i: (rev(i), alr_blk))]


def _gla_fwd(proj, wa2p, b_a, head_norm):
    n, _, dv4, dk4, dk, dv = _gla_dims(proj, head_norm)
    t = _tile(n, GLA_ROWS, CHUNK)
    nc = t // CHUNK
    qscale = dk ** -0.5

    def body(q_ref, k_ref, v_ref, g_ref, alr_ref, wa_ref, ba_ref, hn_ref, y_ref, st_ref,
             state, cum_s, bend_s, o_s):
        i = pl.program_id(0)

        @pl.when(i == 0)
        def _():
            state[...] = jnp.zeros_like(state)

        _, la = _gate_logits(alr_ref, wa_ref, ba_ref)
        cum_s[...] = _chunk_cumsum(la)
        _chunk_ends(cum_s, bend_s, nc)
        kd = (k_ref[...] * jnp.exp(bend_s[...] - cum_s[...])).astype(BF16)
        qs = (q_ref[...] * qscale).astype(BF16)
        vv = v_ref[...].astype(BF16)
        units = [(c, h) for c in range(nc) for h in range(HEADS)]
        for c, h in units:
            rows = slice(c * CHUNK, (c + 1) * CHUNK)
            st_ref[c, h] = _dot(vv[rows, h * dv:(h + 1) * dv], kd[rows, h * dk:(h + 1) * dk], _TN)
        for c, h in units:
            e = jnp.exp(bend_s[c * CHUNK:c * CHUNK + 1, h * dk:(h + 1) * dk])
            s_new = state[h] * e + st_ref[c, h]
            state[h] = s_new
            st_ref[c, h] = s_new
        for c, h in units:
            rows = slice(c * CHUNK, (c + 1) * CHUNK)
            o_s[rows, h * dv:(h + 1) * dv] = _dot(qs[rows, h * dk:(h + 1) * dk], st_ref[c, h], _NT)
        for h in range(HEADS):
            hv = slice(h * dv, (h + 1) * dv)
            o = o_s[:, hv]
            gg = g_ref[:, hv]
            y_ref[:, hv] = (((o * _rms_scale(o)) * hn_ref[:, hv]) * (gg * jax.nn.sigmoid(gg))).astype(BF16)

    full = lambda shape: pl.BlockSpec(shape, lambda i: tuple(0 for _ in shape))
    return pl.pallas_call(
        body, name="gla_fwd", grid=(n // t,),
        in_specs=_gla_in_specs(t, dk4, dv4, lambda i: i)
        + [full((LANE, dk4)), full((1, dk4)), full((1, dv4))],
        out_specs=[pl.BlockSpec((t, dv4), lambda i: (i, 0)),
                   pl.BlockSpec((nc, HEADS, dv, dk), lambda i: (i, 0, 0, 0))],
        out_shape=[jax.ShapeDtypeStruct((n, dv4), BF16),
                   jax.ShapeDtypeStruct((n // CHUNK, HEADS, dv, dk), F32)],
        scratch_shapes=[pltpu.VMEM((HEADS, dv, dk), F32), pltpu.VMEM((t, dk4), F32),
                        pltpu.VMEM((t, dk4), F32), pltpu.VMEM((t, dv4), F32)],
        compiler_params=_cparams(("arbitrary",)),
    )(proj, proj, proj, proj, proj, wa2p, b_a, head_norm)


def _gla_bwd(proj, dymix, du, states, wa2p, b_a, head_norm):
    n, pw, dv4, dk4, dk, dv = _gla_dims(proj, head_norm)
    t = _tile(n, GLA_ROWS, CHUNK)
    nc = t // CHUNK
    nsteps = n // t
    qscale = dk ** -0.5
    rev = lambda i: nsteps - 1 - i

    def body(q_ref, k_ref, v_ref, g_ref, alr_ref, dy_ref, du_ref, st_ref, prev_ref, wa_ref, ba_ref, hn_ref,
             dp_ref, dwa_ref, dba_ref, dhn_ref, carry, cum_s, bend_s, gst_s, dkd_s, dee_s, o_s, do_s):
        i = pl.program_id(0)

        @pl.when(i == 0)
        def _():
            carry[...] = jnp.zeros_like(carry)
            dwa_ref[...] = jnp.zeros_like(dwa_ref)
            dba_ref[...] = jnp.zeros_like(dba_ref)
            dhn_ref[...] = jnp.zeros_like(dhn_ref)

        first_step = i == nsteps - 1
        z, la = _gate_logits(alr_ref, wa_ref, ba_ref)
        cum_s[...] = _chunk_cumsum(la)
        _chunk_ends(cum_s, bend_s, nc)
        dec = jnp.exp(bend_s[...] - cum_s[...])
        kd_f = k_ref[...] * dec
        kd = kd_f.astype(BF16)
        qs = (q_ref[...] * qscale).astype(BF16)
        vv = v_ref[...].astype(BF16)
        dp_ref[:, 0:dv4] = du_ref[...]
        units = [(c, h) for c in range(nc) for h in range(HEADS)]

        for c, h in units:
            rows = slice(c * CHUNK, (c + 1) * CHUNK)
            o_s[rows, h * dv:(h + 1) * dv] = _dot(qs[rows, h * dk:(h + 1) * dk], st_ref[c, h], _NT)
        for h in range(HEADS):
            hv = slice(h * dv, (h + 1) * dv)
            o = o_s[:, hv]
            r = _rms_scale(o)
            oh = o * r
            sl, dsl = _silu_parts(g_ref[:, hv])
            dyh = dy_ref[:, hv]
            hn = hn_ref[:, hv]
            tt = dyh * sl
            dhn_ref[:, hv] += jnp.sum(tt * oh, axis=0, keepdims=True)
            dp_ref[:, 3 * dv4 + h * dv:3 * dv4 + (h + 1) * dv] = (dyh * (oh * hn) * dsl).astype(BF16)
            tt = tt * hn
            do_s[:, hv] = (r * (tt - oh * jnp.mean(tt * oh, axis=-1, keepdims=True))).astype(BF16)
        for c, h in units:
            rows = slice(c * CHUNK, (c + 1) * CHUNK)
            hk = slice(h * dk, (h + 1) * dk)
            do = do_s[rows, h * dv:(h + 1) * dv]
            gst_s[c, h] = _dot(do, qs[rows, hk], _TN)
            dp_ref[rows, dv4 + h * dk:dv4 + (h + 1) * dk] = (_dot(do, st_ref[c, h]) * qscale).astype(BF16)

        for c, h in reversed(units):
            g_n = carry[h] + gst_s[c, h]
            gst_s[c, h] = g_n
            carry[h] = g_n * jnp.exp(bend_s[c * CHUNK:c * CHUNK + 1, h * dk:(h + 1) * dk])

        for c, h in units:
            rows = slice(c * CHUNK, (c + 1) * CHUNK)
            hk = slice(h * dk, (h + 1) * dk)
            hv = slice(h * dv, (h + 1) * dv)
            g_n = gst_s[c, h]
            if c > 0:
                s_prev = st_ref[c - 1, h]
            else:
                s_prev = jnp.where(first_step, 0.0, prev_ref[0, h])
            dkd_s[rows, hk] = _dot(vv[rows, hv], g_n)
            dp_ref[rows, 2 * dv4 + h * dv:2 * dv4 + (h + 1) * dv] = _dot(kd[rows, hk], g_n, _NT).astype(BF16)
            dee = jnp.sum(g_n * s_prev, axis=0, keepdims=True) * jnp.exp(bend_s[c * CHUNK:c * CHUNK + 1, hk])
            dee_s[rows, hk] = jnp.broadcast_to(dee, (CHUNK, dk))

        dkd = dkd_s[...]
        dp_ref[:, dv4 + dk4:dv4 + 2 * dk4] = (dkd * dec).astype(BF16)
        w = dkd * kd_f
        dla = (_chunk_cumsum(w) - w) + dee_s[...]
        dz = dla * (1.0 / GATE_TEMP) * (1.0 - jax.nn.sigmoid(z))
        dp_ref[:, 4 * dv4:4 * dv4 + LANE] = _dot(dz, wa_ref[...], _NT).astype(BF16)
        dwa_ref[...] += _dot(alr_ref[...], dz, _TN)
        dba_ref[...] += jnp.sum(dz, axis=0, keepdims=True)

    full = lambda shape: pl.BlockSpec(shape, lambda i: tuple(0 for _ in shape))
    return pl.pallas_call(
        body, name="gla_bwd", grid=(nsteps,),
        in_specs=_gla_in_specs(t, dk4, dv4, rev)
        + [pl.BlockSpec((t, dv4), lambda i: (rev(i), 1)),
           pl.BlockSpec((t, dv4), lambda i: (rev(i), 0)),
           pl.BlockSpec((nc, HEADS, dv, dk), lambda i: (rev(i), 0, 0, 0)),
           pl.BlockSpec((1, HEADS, dv, dk), lambda i: (jnp.maximum(rev(i) * nc - 1, 0), 0, 0, 0)),
           full((LANE, dk4)), full((1, dk4)), full((1, dv4))],
        out_specs=[pl.BlockSpec((t, pw), lambda i: (rev(i), 0)),
                   full((LANE, dk4)), full((1, dk4)), full((1, dv4))],
        out_shape=[jax.ShapeDtypeStruct((n, pw), BF16), jax.ShapeDtypeStruct((LANE, dk4), F32),
                   jax.ShapeDtypeStruct((1, dk4), F32), jax.ShapeDtypeStruct((1, dv4), F32)],
        scratch_shapes=[pltpu.VMEM((HEADS, dv, dk), F32), pltpu.VMEM((t, dk4), F32),
                        pltpu.VMEM((t, dk4), F32), pltpu.VMEM((nc, HEADS, dv, dk), F32),
                        pltpu.VMEM((t, dk4), F32), pltpu.VMEM((t, dk4), F32),
                        pltpu.VMEM((t, dv4), F32), pltpu.VMEM((t, dv4), BF16)],
        compiler_params=_cparams(("arbitrary",)),
    )(proj, proj, proj, proj, proj, dymix, du, states, states, wa2p, b_a, head_norm)


def _softmax_rows(q, k, scale):
    s = _dot(q, k, _NT) * scale
    p = jnp.exp(s - jnp.max(s, axis=-1, keepdims=True))
    return p / jnp.sum(p, axis=-1, keepdims=True)


def _attn_fwd(q, kv):
    n, d = q.shape
    m = kv.shape[0]
    dh = d // HEADS
    tm = _tile(n, 512, 16)
    scale = dh ** -0.5

    def body(q_ref, k_ref, v_ref, o_ref):
        for h in range(HEADS):
            hs = slice(h * dh, (h + 1) * dh)
            p = _softmax_rows(q_ref[:, hs], k_ref[:, hs], scale)
            o_ref[:, hs] = _dot(p, v_ref[:, hs]).astype(BF16)

    return pl.pallas_call(
        body, name="attn_fwd", grid=(n // tm,),
        in_specs=[pl.BlockSpec((tm, d), lambda i: (i, 0)), pl.BlockSpec((m, d), lambda i: (0, 0)),
                  pl.BlockSpec((m, d), lambda i: (0, 1))],
        out_specs=pl.BlockSpec((tm, d), lambda i: (i, 0)),
        out_shape=jax.ShapeDtypeStruct((n, d), BF16),
        compiler_params=_cparams(("parallel",)),
    )(q, kv, kv)


def _attn_bwd(q, kv, do):
    n, d = q.shape
    m = kv.shape[0]
    dh = d // HEADS
    tm = _tile(n, 512, 16)
    scale = dh ** -0.5

    def body(q_ref, k_ref, v_ref, do_ref, dq_ref, dk_ref, dv_ref):
        i = pl.program_id(0)

        @pl.when(i == 0)
        def _():
            dk_ref[...] = jnp.zeros_like(dk_ref)
            dv_ref[...] = jnp.zeros_like(dv_ref)

        for h in range(HEADS):
            hs = slice(h * dh, (h + 1) * dh)
            qh, kh, vh, doh = q_ref[:, hs], k_ref[:, hs], v_ref[:, hs], do_ref[:, hs]
            p = _softmax_rows(qh, kh, scale)
            dv_ref[:, hs] += _dot(p, doh, _TN)
            dp = _dot(doh, vh, _NT)
            ds = p * (dp - jnp.sum(dp * p, axis=-1, keepdims=True)) * scale
            dq_ref[:, hs] = _dot(ds, kh).astype(BF16)
            dk_ref[:, hs] += _dot(ds, qh, _TN)

    row = pl.BlockSpec((tm, d), lambda i: (i, 0))
    memb = pl.BlockSpec((m, d), lambda i: (0, 0))
    return pl.pallas_call(
        body, name="attn_bwd", grid=(n // tm,),
        in_specs=[row, memb, pl.BlockSpec((m, d), lambda i: (0, 1)), row],
        out_specs=[row, memb, memb],
        out_shape=[jax.ShapeDtypeStruct((n, d), BF16), jax.ShapeDtypeStruct((m, d), F32),
                   jax.ShapeDtypeStruct((m, d), F32)],
        compiler_params=_cparams(("arbitrary",)),
    )(q, kv, kv, do)


def _adamw(name, w, m, v, parts, own_block):
    r, c = w.shape
    tr = _tile(r, max(16, ELEMENTWISE_BLOCK // c), 16)
    row = pl.BlockSpec((tr, c), lambda i, o: (i, 0))
    ops, specs = [w, m, v], [row, row, row]
    for p in parts:
        if p.ndim == 2:
            ops.append(p)
            specs.append(row)
        elif p.shape[0] == 4:
            ops.append(p)
            specs.append(pl.BlockSpec((None, tr, c), lambda i, o: (o[0], i, 0)))
        else:
            for s in range(p.shape[0]):
                ops.append(p)
                specs.append(pl.BlockSpec((None, tr, c), lambda i, o, s=s: (s, i, 0)))
    n_parts = len(ops) - 3
    c1 = 1.0 - ADAM_B1 ** ADAM_STEP
    c2 = 1.0 - ADAM_B2 ** ADAM_STEP

    def body(o_ref, *refs):
        del o_ref
        w_ref, m_ref, v_ref = refs[:3]
        g_refs = refs[3:3 + n_parts]
        go_ref, d_ref, mo_ref, vo_ref = refs[3 + n_parts:]
        g = g_refs[0][...].astype(F32)
        for g_ref in g_refs[1:]:
            g = g + g_ref[...].astype(F32)
        m_new = ADAM_B1 * m_ref[...] + (1.0 - ADAM_B1) * g
        v_new = ADAM_B2 * v_ref[...] + (1.0 - ADAM_B2) * (g * g)
        m_hat = m_new / c1
        v_hat = v_new / c2
        go_ref[...] = g
        d_ref[...] = -ADAM_LR * (m_hat / (jnp.sqrt(v_hat) + ADAM_EPS) + ADAM_WD * w_ref[...])
        mo_ref[...] = m_new
        vo_ref[...] = v_new

    shp = jax.ShapeDtypeStruct((r, c), F32)
    return pl.pallas_call(
        body, name=name,
        grid_spec=pltpu.PrefetchScalarGridSpec(
            num_scalar_prefetch=1, grid=(r // tr,), in_specs=specs, out_specs=[row] * 4),
        out_shape=[shp] * 4,
        compiler_params=_cparams(("parallel",)),
    )(own_block, *ops)


def _pair_add(name, g, recvd, core):
    _, r, c = g.shape
    tr = _tile(r, max(16, ELEMENTWISE_BLOCK // c), 16)

    def body(core_ref, a_ref, b_ref, o_ref):
        del core_ref
        o_ref[...] = (a_ref[...].astype(F32) + b_ref[...].astype(F32)).astype(o_ref.dtype)

    blk = pl.BlockSpec((None, tr, c), lambda s, i, core_ref: (s, i, 0))
    mine = pl.BlockSpec((None, tr, c), lambda s, i, core_ref: (2 * s + core_ref[0], i, 0))
    return pl.pallas_call(
        body, name=name,
        grid_spec=pltpu.PrefetchScalarGridSpec(
            num_scalar_prefetch=1, grid=(4, r // tr), in_specs=[mine, blk], out_specs=blk),
        out_shape=jax.ShapeDtypeStruct(recvd.shape, g.dtype),
        compiler_params=_cparams(("parallel", "parallel")),
    )(core, g, recvd)


def _position():
    return lax.axis_index("x"), lax.axis_index("y"), lax.axis_index("c")


_HBM = pl.BlockSpec(memory_space=pltpu.HBM)

_Exchange = collections.namedtuple("_Exchange", "ins out_shape sems start finish in_place", defaults=(False,))
_NO_EXCHANGE = _Exchange((), (), (), lambda ins, outs, sems: None, lambda ins, outs, sems: None)


def _placed(shard):
    x, y, c = _position()
    start = (4 * x + 2 * y + c,) + (0,) * shard.ndim
    return lax.dynamic_update_slice(lax.empty((N_DEV,) + shard.shape, shard.dtype), shard[None], start)


def _all_gather(name, placed):
    n = len(placed)

    def body(*refs):
        ins, outs = refs[:n], refs[n:2 * n]
        send_sems, recv_sems = refs[2 * n:]
        x, y, c = _position()
        me, sibling = (x, y, c), (x, y, 1 - c)
        chips = [(1 - x, y), (x, 1 - y), (1 - x, 1 - y)]

        def copy(a, k, owner, to):
            blk = 4 * owner[0] + 2 * owner[1] + owner[2]
            return pltpu.make_async_remote_copy(
                src_ref=ins[a].at[blk], dst_ref=outs[a].at[blk],
                send_sem=send_sems.at[7 * a + k], recv_sem=recv_sems.at[7 * a + k],
                device_id=to, device_id_type=MESH)

        sends = []
        for a in range(n):
            sends.append(copy(a, 0, me, sibling))
            sends += [copy(a, 1 + j, me, (*chip, c)) for j, chip in enumerate(chips)]
        for cp in sends:
            cp.start()
        for j, chip in enumerate(chips):
            for a in range(n):
                copy(a, 1 + j, (*chip, c), me).wait_recv()
                fwd = copy(a, 4 + j, (*chip, c), sibling)
                fwd.start()
                sends.append(fwd)
        for a in range(n):
            copy(a, 0, sibling, me).wait_recv()
            for j, chip in enumerate(chips):
                copy(a, 4 + j, (*chip, 1 - c), me).wait_recv()
        for cp in sends:
            cp.wait_send()

    return pl.pallas_call(
        body, name=name,
        in_specs=[_HBM] * n, out_specs=[_HBM] * n,
        out_shape=[jax.ShapeDtypeStruct(p.shape, p.dtype) for p in placed],
        input_output_aliases={a: a for a in range(n)},
        scratch_shapes=[pltpu.SemaphoreType.DMA((7 * n,)), pltpu.SemaphoreType.DMA((7 * n,))],
    )(*placed)


def _pair_exchange(grads):
    n = len(grads)

    def start(ins, recvd, sems):
        send_sems, recv_sems = sems
        x, y, c = _position()
        for a in range(n):
            for chip in range(4):
                pltpu.make_async_remote_copy(
                    src_ref=ins[a].at[2 * chip + (1 - c)], dst_ref=recvd[a].at[chip],
                    send_sem=send_sems.at[a], recv_sem=recv_sems.at[a],
                    device_id=(x, y, 1 - c), device_id_type=MESH).start()

    def finish(ins, recvd, sems):
        del ins
        send_sems, recv_sems = sems
        x, y, c = _position()
        for a in range(n):
            pltpu.make_async_remote_copy(
                src_ref=recvd[a], dst_ref=recvd[a], send_sem=send_sems.at[a], recv_sem=recv_sems.at[a],
                device_id=(x, y, 1 - c), device_id_type=MESH).wait()

    return _Exchange(
        ins=tuple(grads),
        out_shape=tuple(jax.ShapeDtypeStruct((4,) + g.shape[1:], g.dtype) for g in grads),
        sems=(pltpu.SemaphoreType.DMA((n,)), pltpu.SemaphoreType.DMA((n,))),
        start=start, finish=finish)


def _run_exchange(name, exchange):
    n_in, n_out = len(exchange.ins), len(exchange.out_shape)

    def body(*refs):
        ins, outs, sems = refs[:n_in], refs[n_in:n_in + n_out], refs[n_in + n_out:]
        exchange.start(ins, outs, sems)
        exchange.finish(ins, outs, sems)

    return pl.pallas_call(
        body, name=name, in_specs=[_HBM] * n_in, out_specs=[_HBM] * n_out,
        out_shape=list(exchange.out_shape), scratch_shapes=list(exchange.sems),
        input_output_aliases={k: k for k in range(n_in)} if exchange.in_place else {},
    )(*exchange.ins)


def _chip_scatter_exchange(sums):
    n = len(sums)
    offsets = [(1, 0), (0, 1), (1, 1)]

    def start(ins, recvd, sems):
        send_sems, recv_sems = sems
        x, y, c = _position()
        for a in range(n):
            for r, (ox, oy) in enumerate(offsets):
                px = 1 - x if ox else x
                py = 1 - y if oy else y
                pltpu.make_async_remote_copy(
                    src_ref=ins[a].at[2 * px + py], dst_ref=recvd[a].at[r],
                    send_sem=send_sems.at[a], recv_sem=recv_sems.at[a],
                    device_id=(px, py, c), device_id_type=MESH).start()

    def finish(ins, recvd, sems):
        del ins
        send_sems, recv_sems = sems
        x, y, c = _position()
        for a in range(n):
            pltpu.make_async_remote_copy(
                src_ref=recvd[a], dst_ref=recvd[a], send_sem=send_sems.at[a], recv_sem=recv_sems.at[a],
                device_id=(x, y, c), device_id_type=MESH).wait()

    return _Exchange(
        ins=tuple(sums),
        out_shape=tuple(jax.ShapeDtypeStruct((3,) + s.shape[1:], s.dtype) for s in sums),
        sems=(pltpu.SemaphoreType.DMA((n,)), pltpu.SemaphoreType.DMA((n,))),
        start=start, finish=finish)


def _gather_spread_exchange(placed):
    n = len(placed)

    def peers():
        x, y, c = _position()
        return (x, y, c), [(x, y, 1 - c), (1 - x, y, c), (x, 1 - y, c), (1 - x, 1 - y, c)]

    def copy(ins, outs, sems, a, k, owner, to):
        blk = 4 * owner[0] + 2 * owner[1] + owner[2]
        return pltpu.make_async_remote_copy(
            src_ref=ins[a].at[blk], dst_ref=outs[a].at[blk],
            send_sem=sems[0].at[4 * a + k], recv_sem=sems[1].at[4 * a + k],
            device_id=to, device_id_type=MESH)

    def start(ins, outs, sems):
        me, others = peers()
        for a in range(n):
            for k, to in enumerate(others):
                copy(ins, outs, sems, a, k, me, to).start()

    def finish(ins, outs, sems):
        _, others = peers()
        for a in range(n):
            for k, peer in enumerate(others):
                cp = copy(ins, outs, sems, a, k, peer, peer)
                cp.wait_recv()
                cp.wait_send()

    return _Exchange(
        ins=tuple(placed), out_shape=tuple(jax.ShapeDtypeStruct(p.shape, p.dtype) for p in placed),
        sems=(pltpu.SemaphoreType.DMA((4 * n,)), pltpu.SemaphoreType.DMA((4 * n,))),
        start=start, finish=finish, in_place=True)


def _gather_forward_exchange(partial):
    n = len(partial)

    def copy(ins, outs, sems, a, j, pc):
        x, y, c = _position()
        chips = [(1 - x, y), (x, 1 - y), (1 - x, 1 - y)]
        blk = 4 * chips[j][0] + 2 * chips[j][1] + pc
        return pltpu.make_async_remote_copy(
            src_ref=ins[a].at[blk], dst_ref=outs[a].at[blk],
            send_sem=sems[0].at[3 * a + j], recv_sem=sems[1].at[3 * a + j],
            device_id=(x, y, 1 - c), device_id_type=MESH)

    def start(ins, outs, sems):
        c = lax.axis_index("c")
        for a in range(n):
            for j in range(3):
                copy(ins, outs, sems, a, j, c).start()

    def finish(ins, outs, sems):
        c = lax.axis_index("c")
        for a in range(n):
            for j in range(3):
                copy(ins, outs, sems, a, j, 1 - c).wait_recv()
                copy(ins, outs, sems, a, j, c).wait_send()

    return _Exchange(
        ins=tuple(partial), out_shape=tuple(jax.ShapeDtypeStruct(p.shape, p.dtype) for p in partial),
        sems=(pltpu.SemaphoreType.DMA((3 * n,)), pltpu.SemaphoreType.DMA((3 * n,))),
        start=start, finish=finish, in_place=True)


def _all_reduce_small(vec):
    r = vec.shape[0]

    def body(v_ref, o_ref, gbuf, send_sems, recv_sems):
        x, y, c = _position()
        me = 4 * x + 2 * y + c
        gbuf[me] = v_ref[...]
        copies = []
        for k in range(1, N_DEV):
            ox, oy, oc = (k >> 2) & 1, (k >> 1) & 1, k & 1
            peer = (1 - x if ox else x, 1 - y if oy else y, 1 - c if oc else c)
            cp = pltpu.make_async_remote_copy(
                src_ref=gbuf.at[me], dst_ref=gbuf.at[me], send_sem=send_sems.at[k - 1],
                recv_sem=recv_sems.at[k - 1], device_id=peer, device_id_type=MESH)
            cp.start()
            copies.append(cp)
        for cp in copies:
            cp.wait()
        total = gbuf[0]
        for j in range(1, N_DEV):
            total = total + gbuf[j]
        o_ref[...] = total

    return pl.pallas_call(
        body, name="all_reduce_small",
        in_specs=[pl.BlockSpec(memory_space=pltpu.VMEM)],
        out_specs=pl.BlockSpec(memory_space=pltpu.VMEM),
        out_shape=jax.ShapeDtypeStruct(vec.shape, F32),
        scratch_shapes=[pltpu.VMEM((N_DEV, r, LANE), F32), pltpu.SemaphoreType.DMA((N_DEV - 1,)),
                        pltpu.SemaphoreType.DMA((N_DEV - 1,))],
    )(vec)


def _cols_from_blocks(g):
    nb, r, cs = g.shape
    return jnp.transpose(g, (1, 0, 2)).reshape(r, nb * cs)


def _cols_to_blocks(w):
    r, cfull = w.shape
    return jnp.transpose(w.reshape(r, N_DEV, cfull // N_DEV), (1, 0, 2))


def kernel(x, mem, ffn1_norm, ffn1_w_gate, ffn1_w_up, ffn1_w_down, mix_norm, w_in, pool_w, pool_scale, gla_w_a2, gla_b_a, gla_head_norm, w_out, xattn_norm, mem_norm, xattn_w_q, xattn_w_kv, xattn_w_o, ffn2_norm, ffn2_w_gate, ffn2_w_up, ffn2_w_down, final_norm, loss_target, m_ffn1_norm, m_ffn1_w_gate, m_ffn1_w_up, m_ffn1_w_down, m_mix_norm, m_w_in, m_pool_w, m_pool_scale, m_gla_w_a2, m_gla_b_a, m_gla_head_norm, m_w_out, m_xattn_norm, m_mem_norm, m_xattn_w_q, m_xattn_w_kv, m_xattn_w_o, m_ffn2_norm, m_ffn2_w_gate, m_ffn2_w_up, m_ffn2_w_down, m_final_norm, v_ffn1_norm, v_ffn1_w_gate, v_ffn1_w_up, v_ffn1_w_down, v_mix_norm, v_w_in, v_pool_w, v_pool_scale, v_gla_w_a2, v_gla_b_a, v_gla_head_norm, v_w_out, v_xattn_norm, v_mem_norm, v_xattn_w_q, v_xattn_w_kv, v_xattn_w_o, v_ffn2_norm, v_ffn2_w_gate, v_ffn2_w_up, v_ffn2_w_down, v_final_norm):
    weights = dict(ffn1_norm=ffn1_norm, ffn1_w_gate=ffn1_w_gate, ffn1_w_up=ffn1_w_up, ffn1_w_down=ffn1_w_down, mix_norm=mix_norm, w_in=w_in, pool_w=pool_w, pool_scale=pool_scale, gla_w_a2=gla_w_a2, gla_b_a=gla_b_a, gla_head_norm=gla_head_norm, w_out=w_out, xattn_norm=xattn_norm, mem_norm=mem_norm, xattn_w_q=xattn_w_q, xattn_w_kv=xattn_w_kv, xattn_w_o=xattn_w_o, ffn2_norm=ffn2_norm, ffn2_w_gate=ffn2_w_gate, ffn2_w_up=ffn2_w_up, ffn2_w_down=ffn2_w_down, final_norm=final_norm)
    mom1 = dict(ffn1_norm=m_ffn1_norm, ffn1_w_gate=m_ffn1_w_gate, ffn1_w_up=m_ffn1_w_up, ffn1_w_down=m_ffn1_w_down, mix_norm=m_mix_norm, w_in=m_w_in, pool_w=m_pool_w, pool_scale=m_pool_scale, gla_w_a2=m_gla_w_a2, gla_b_a=m_gla_b_a, gla_head_norm=m_gla_head_norm, w_out=m_w_out, xattn_norm=m_xattn_norm, mem_norm=m_mem_norm, xattn_w_q=m_xattn_w_q, xattn_w_kv=m_xattn_w_kv, xattn_w_o=m_xattn_w_o, ffn2_norm=m_ffn2_norm, ffn2_w_gate=m_ffn2_w_gate, ffn2_w_up=m_ffn2_w_up, ffn2_w_down=m_ffn2_w_down, final_norm=m_final_norm)
    mom2 = dict(ffn1_norm=v_ffn1_norm, ffn1_w_gate=v_ffn1_w_gate, ffn1_w_up=v_ffn1_w_up, ffn1_w_down=v_ffn1_w_down, mix_norm=v_mix_norm, w_in=v_w_in, pool_w=v_pool_w, pool_scale=v_pool_scale, gla_w_a2=v_gla_w_a2, gla_b_a=v_gla_b_a, gla_head_norm=v_gla_head_norm, w_out=v_w_out, xattn_norm=v_xattn_norm, mem_norm=v_mem_norm, xattn_w_q=v_xattn_w_q, xattn_w_kv=v_xattn_w_kv, xattn_w_o=v_xattn_w_o, ffn2_norm=v_ffn2_norm, ffn2_w_gate=v_ffn2_w_gate, ffn2_w_up=v_ffn2_w_up, ffn2_w_down=v_ffn2_w_down, final_norm=v_final_norm)
    order = list(weights.keys())

    n, d = x.shape[1], x.shape[2]
    mlen = mem.shape[1]
    x0 = x.reshape(n, d)
    memf = mem.reshape(mlen, d)
    target = loss_target.reshape(n, d)
    dpool = d // 2
    in_cols = w_in.shape[2] * N_DEV
    proj_cols = 2 * d + LANE
    rank = gla_w_a2.shape[1]

    def shard(name):
        return _placed(weights[name][0].astype(BF16))

    wg1, wu1, wd1 = _all_gather("ag_ffn1", [shard("ffn1_w_gate"), shard("ffn1_w_up"), shard("ffn1_w_down")])
    later = ["w_in", "pool_w", "gla_w_a2", "w_out", "xattn_w_q", "xattn_w_kv", "xattn_w_o",
             "ffn2_w_gate", "ffn2_w_up", "ffn2_w_down"]
    h1 = _rms_fwd("rms_ffn1", x0, ffn1_norm)
    (x1, a1, b1, h2), spread = _ffn_fwd(
        "ffn1_fwd", x0, h1, wg1, wu1, wd1, mix_norm, _gather_spread_exchange([shard(k) for k in later]))
    win_g, = _run_exchange("ag_forward_w_in", _gather_forward_exchange(spread[:1]))
    winp = jnp.pad(_cols_from_blocks(win_g), ((0, 0), (0, proj_cols - in_cols)))
    proj, (pw_g, wa2_g, wout_g, wq_g, wkv_g, wo_g, wg2, wu2, wd2) = _mm_nn(
        "mix_in", [(h2, winp)], F32, exchange=_gather_forward_exchange(spread[1:]), tn=1408)

    pw = jnp.transpose(pw_g, (1, 0, 2, 3)).reshape(len(POOL_WINDOWS), dpool // 4, dpool // 4)
    wa2p = jnp.pad(_cols_from_blocks(wa2_g), ((0, LANE - rank), (0, 0)))
    wout = wout_g.reshape(d, d)
    wq = wq_g.reshape(d, d)
    wkv = _cols_from_blocks(wkv_g)
    wo = wo_g.reshape(d, d)
    fnorm = final_norm.reshape(1, d)

    ypool = _pool_fwd(proj, pw, pool_scale)
    ygla, states = _gla_fwd(proj, wa2p, gla_b_a, gla_head_norm)
    x2, h3 = _mm_nn("mix_out", [(ypool, wout[:dpool]), (ygla, wout[dpool:])], F32, res=x1,
                    norm_gain=xattn_norm, tk=1024)
    mh = _rms_fwd("rms_mem", memf, mem_norm)
    q = _mm_nn("xattn_q", [(h3, wq)], BF16)
    kv = _mm_nn("xattn_kv", [(mh, wkv)], BF16)
    att = _attn_fwd(q, kv)
    x3, h4 = _mm_nn("xattn_o", [(att, wo)], F32, res=x2, norm_gain=ffn2_norm)
    (x4, a2, b2), _ = _ffn_fwd("ffn2_fwd", x3, h4, wg2, wu2, wd2, None, _NO_EXCHANGE)
    dx4, dx4b, g_final, loss_part = _loss_head(x4, target, fnorm)

    grads = {}
    core = lax.axis_index("c").astype(jnp.int32).reshape(1)
    chip = (2 * lax.axis_index("x") + lax.axis_index("y")).astype(jnp.int32).reshape(1)
    chip_sums = {}

    def pair_blocks(names):
        return [grads[k].reshape(N_DEV, -1, grads[k].shape[-1]) for k in names]

    def pair_sums(names, blocks, recvd):
        for k, g, r in zip(names, blocks, recvd):
            chip_sums[k] = _pair_add("rs_add_" + k, g, r, core)

    ffn2_names = ["ffn2_w_gate", "ffn2_w_up", "ffn2_w_down"]
    (da2, db2, act2), _ = _ffn_bwd_act("ffn2_bwd_act", dx4b, wd2, a2, b2, _NO_EXCHANGE)
    (grads["ffn2_w_gate"], grads["ffn2_w_up"], grads["ffn2_w_down"]), _ = _ffn_bwd_weights(
        "ffn2_bwd_weights", h4, dx4b, da2, db2, act2, _NO_EXCHANGE)
    blocks = pair_blocks(ffn2_names)
    dh4, _ = _ffn_bwd_dh("ffn2_bwd_dh", da2, db2, wg2, wu2, _NO_EXCHANGE)
    (dx3, dx3b, g_ffn2_norm), recvd = _rms_bwd(
        "rms_ffn2_bwd", x3, ffn2_norm, dh4, dx4, _pair_exchange(blocks))
    pair_sums(ffn2_names, blocks, recvd)

    xattn_names = ["xattn_w_q", "xattn_w_kv", "xattn_w_o"]
    datt = _mm_nt("xattn_do", [(dx3b, wo)], BF16)
    grads["xattn_w_o"] = _mm_tn("xattn_dwo", [(att, dx3b)], BF16).reshape(N_DEV, d // N_DEV, d)
    dq, dk, dv = _attn_bwd(q, kv, datt)
    grads["xattn_w_q"] = _mm_tn("xattn_dwq", [(h3, dq)], BF16).reshape(N_DEV, d // N_DEV, d)
    dkv = jnp.concatenate([dk, dv], axis=1)
    dmh = _mm_nt("xattn_dmh", [(dkv, wkv)], F32)
    grads["xattn_w_kv"] = _cols_to_blocks(_mm_tn("xattn_dwkv", [(mh, dkv)], BF16))
    g_mem_norm = _rms_gain_grad("rms_mem_bwd", memf, mem_norm, dmh)
    blocks = pair_blocks(xattn_names)
    (dx2, dx2b, g_xattn_norm), recvd = _mm_nt_rms_bwd(
        "xattn_dh", dq, wq, x2, xattn_norm, dx3, _pair_exchange(blocks), tk=2048)
    pair_sums(xattn_names, blocks, recvd)

    dymix = _mm_nt("mix_dy", [(dx2b, wout)], F32)
    grads["w_out"] = jnp.concatenate(
        [_mm_tn("mix_dwout_pool", [(ypool, dx2b)], BF16), _mm_tn("mix_dwout_gla", [(ygla, dx2b)], BF16)],
        axis=0).reshape(N_DEV, d // N_DEV, d)
    du, g_pool_w, g_pool_scale = _pool_bwd(proj, dymix, pw, pool_scale)
    dproj, g_wa2p, g_b_a, g_head_norm = _gla_bwd(proj, dymix, du, states, wa2p, gla_b_a, gla_head_norm)
    dwin, recvd = _mm_tn("mix_dwin", [(h2, dproj)], BF16,
                         exchange=_chip_scatter_exchange([chip_sums[ffn2_names[0]]]), tn=1408)
    chip_recvd = {ffn2_names[0]: recvd[0]}
    grads["w_in"] = _cols_to_blocks(dwin[:, :in_cols])
    grads["pool_w"] = jnp.transpose(
        g_pool_w.reshape(len(POOL_WINDOWS), N_DEV, dpool // 4 // N_DEV, dpool // 4), (1, 0, 2, 3))
    grads["gla_w_a2"] = _cols_to_blocks(g_wa2p[:rank])
    mix_names = ["w_in", "pool_w", "gla_w_a2", "w_out"]
    blocks = pair_blocks(mix_names)
    (dx1, dx1b, g_mix_norm), recvd = _mm_nt_rms_bwd(
        "mix_dh", dproj, winp, x1, mix_norm, dx2, _pair_exchange(blocks), tk=1408)
    pair_sums(mix_names, blocks, recvd)

    ffn1_names = ["ffn1_w_gate", "ffn1_w_up", "ffn1_w_down"]
    (da1, db1, act1), recvd = _ffn_bwd_act(
        "ffn1_bwd_act", dx1b, wd1, a1, b1, _chip_scatter_exchange([chip_sums[k] for k in ffn2_names[1:]]))
    chip_recvd.update(zip(ffn2_names[1:], recvd))
    (grads["ffn1_w_gate"], grads["ffn1_w_up"], grads["ffn1_w_down"]), recvd = _ffn_bwd_weights(
        "ffn1_bwd_weights", h1, dx1b, da1, db1, act1,
        _chip_scatter_exchange([chip_sums[k] for k in xattn_names + mix_names]))
    chip_recvd.update(zip(xattn_names + mix_names, recvd))
    blocks = pair_blocks(ffn1_names)
    pair_sums(ffn1_names, blocks, _run_exchange("rs_pair_ffn1", _pair_exchange(blocks)))
    dh1, recvd = _ffn_bwd_dh(
        "ffn1_bwd_dh", da1, db1, wg1, wu1, _chip_scatter_exchange([chip_sums[k] for k in ffn1_names]))
    chip_recvd.update(zip(ffn1_names, recvd))
    (dx0, _, g_ffn1_norm), _ = _rms_bwd("rms_ffn1_bwd", x0, ffn1_norm, dh1, dx1, _NO_EXCHANGE)

    small = [("ffn1_norm", g_ffn1_norm), ("mix_norm", g_mix_norm), ("pool_scale", g_pool_scale),
             ("gla_b_a", g_b_a), ("gla_head_norm", g_head_norm), ("xattn_norm", g_xattn_norm),
             ("mem_norm", g_mem_norm), ("ffn2_norm", g_ffn2_norm), ("final_norm", g_final)]
    packed = jnp.concatenate([g.reshape(-1) for _, g in small] + [loss_part.reshape(-1)])
    slab = 8 * LANE
    padded = -(-packed.shape[0] // slab) * slab
    packed = jnp.pad(packed, (0, padded - packed.shape[0])).reshape(padded // LANE, LANE)
    reduced = _all_reduce_small(packed).reshape(-1)
    small_grads = {}
    off = 0
    for name, g in small:
        small_grads[name] = reduced[off:off + g.size]
        off += g.size
    loss = reduced[off]

    out_g, out_d, out_m, out_v = {}, {}, {}, {}
    for k in order:
        w = weights[k]
        if k in small_grads:
            w2 = w.reshape(1, -1)
            parts = [small_grads[k].reshape(1, -1)]
            own_block = jnp.zeros((1,), jnp.int32)
        else:
            w2 = w.reshape(-1, w.shape[-1])
            parts = [chip_sums[k], chip_recvd[k]]
            own_block = chip
        res = _adamw("adamw_" + k, w2, mom1[k].reshape(w2.shape), mom2[k].reshape(w2.shape), parts, own_block)
        out_g[k], out_d[k], out_m[k], out_v[k] = [r.reshape(w.shape) for r in res]

    return (loss, dx0.reshape(x.shape), *[out_g[k] for k in order], *[out_d[k] for k in order],
            *[out_m[k] for k in order], *[out_v[k] for k in order])
```

```python
import collections

import jax
import jax.numpy as jnp
from jax import lax
from jax.experimental import pallas as pl
from jax.experimental.pallas import tpu as pltpu

F32 = jnp.float32
BF16 = jnp.bfloat16
MESH = pl.DeviceIdType.MESH

N_DEV = 8
CHUNK = 64
POOL_WINDOWS = (2, 4, 8, 16)
POOL_HALO = 16
HEADS = 4
GATE_TEMP = 16.0
RMS_EPS = 1e-6
LANE = 128
V7X_VMEM_BYTES = 64 * 1024 * 1024
VMEM_LIMIT = V7X_VMEM_BYTES - 8 * 1024 * 1024
GLA_ROWS = 8 * CHUNK
ELEMENTWISE_BLOCK = 512 * 1024
FFN_SUBTILES = 2
PROJ_SUBTILE_ROWS = 256

ADAM_LR = 0.001
ADAM_B1 = 0.9
ADAM_B2 = 0.999
ADAM_EPS = 1e-08
ADAM_WD = 0.01
ADAM_STEP = 10

_NN = (((1,), (0,)), ((), ()))
_NT = (((1,), (1,)), ((), ()))
_TN = (((0,), (0,)), ((), ()))


def _cparams(sem=None):
    return pltpu.CompilerParams(dimension_semantics=sem, vmem_limit_bytes=VMEM_LIMIT)


def _tile(n, pref, align):
    t = (min(pref, n) // align) * align
    while t >= align:
        if n % t == 0:
            return t
        t -= align
    return n


def _dot(a, b, dims=_NN):
    return lax.dot_general(a.astype(BF16), b.astype(BF16), dims, preferred_element_type=F32)


def _silu_parts(z):
    sig = jax.nn.sigmoid(z)
    return z * sig, sig * (1.0 + z * (1.0 - sig))


def _rms_scale(xf):
    return lax.rsqrt(jnp.mean(xf * xf, axis=-1, keepdims=True) + RMS_EPS)


def _matmul(name, pairs, a_spec, b_spec, out_shape, out_spec, grid, acc_shape, dims,
            res=None, res_spec=None, scale=None, norm_gain=None, exchange=None):
    n = len(pairs)
    nk = grid[2]

    def body(ins, outs, scr):
        a_refs, b_refs = ins[:n], ins[n:2 * n]
        pos = 2 * n
        res_ref = gain_ref = h_ref = None
        if res is not None:
            res_ref = ins[pos]
            pos += 1
        if norm_gain is not None:
            gain_ref = ins[pos]
            h_ref = outs[1]
        o_ref = outs[0]
        acc, = scr
        k = pl.program_id(2)

        @pl.when(k == 0)
        def _():
            acc[...] = jnp.zeros_like(acc)

        part = None
        for a_ref, b_ref in zip(a_refs, b_refs):
            d = _dot(a_ref[...], b_ref[...], dims)
            part = d if part is None else part + d
        acc[...] += part

        @pl.when(k == nk - 1)
        def _():
            r = acc[...]
            if scale is not None:
                r = r * scale
            if res_ref is not None:
                r = r + res_ref[...]
            o_ref[...] = r.astype(o_ref.dtype)
            if h_ref is not None:
                h_ref[...] = ((r * _rms_scale(r)) * gain_ref[...]).astype(BF16)

    ops = [p[0] for p in pairs] + [p[1] for p in pairs]
    specs = [a_spec] * n + [b_spec] * n
    if res is not None:
        ops.append(res)
        specs.append(res_spec)
    out_specs, out_shapes = [out_spec], [out_shape]
    if norm_gain is not None:
        ops.append(norm_gain)
        specs.append(pl.BlockSpec(norm_gain.shape, lambda i, j, k: (0, 0)))
        out_specs.append(out_spec)
        out_shapes.append(jax.ShapeDtypeStruct(out_shape.shape, BF16))
    outs, carried = _hosted_call(
        name, body, grid, specs, out_specs, out_shapes, [pltpu.VMEM(acc_shape, F32)], ops,
        _NO_EXCHANGE if exchange is None else exchange)
    result = outs[0] if norm_gain is None else tuple(outs)
    return result if exchange is None else (result, carried)


def _mm_nn(name, pairs, out_dtype, res=None, norm_gain=None, exchange=None, tm=1024, tn=1024, tk=2048):
    m, kd = pairs[0][0].shape
    nd = pairs[0][1].shape[1]
    if norm_gain is not None:
        tm, tn = 512, nd
    tm, tn, tk = _tile(m, tm, 16), _tile(nd, tn, LANE), _tile(kd, tk, LANE)
    return _matmul(
        name, pairs,
        pl.BlockSpec((tm, tk), lambda i, j, k: (i, k)),
        pl.BlockSpec((tk, tn), lambda i, j, k: (k, j)),
        jax.ShapeDtypeStruct((m, nd), out_dtype),
        pl.BlockSpec((tm, tn), lambda i, j, k: (i, j)),
        (m // tm, nd // tn, kd // tk), (tm, tn), _NN,
        res=res, res_spec=pl.BlockSpec((tm, tn), lambda i, j, k: (i, j)), norm_gain=norm_gain,
        exchange=exchange)


def _mm_nt(name, pairs, out_dtype, tm=1024, tn=1024, tk=2048):
    m, kd = pairs[0][0].shape
    nd = pairs[0][1].shape[0]
    tm, tn, tk = _tile(m, tm, 16), _tile(nd, tn, LANE), _tile(kd, tk, LANE)
    return _matmul(
        name, pairs,
        pl.BlockSpec((tm, tk), lambda i, j, k: (i, k)),
        pl.BlockSpec((tn, tk), lambda i, j, k: (j, k)),
        jax.ShapeDtypeStruct((m, nd), out_dtype),
        pl.BlockSpec((tm, tn), lambda i, j, k: (i, j)),
        (m // tm, nd // tn, kd // tk), (tm, tn), _NT)


def _mm_tn(name, pairs, out_dtype, exchange=None, tm=1024, tn=2048, tk=1024):
    kd, m = pairs[0][0].shape
    nd = pairs[0][1].shape[1]
    tm, tn, tk = _tile(m, tm, LANE), _tile(nd, tn, LANE), _tile(kd, tk, 16)
    return _matmul(
        name, pairs,
        pl.BlockSpec((tk, tm), lambda i, j, k: (k, i)),
        pl.BlockSpec((tk, tn), lambda i, j, k: (k, j)),
        jax.ShapeDtypeStruct((m, nd), out_dtype),
        pl.BlockSpec((tm, tn), lambda i, j, k: (i, j)),
        (m // tm, nd // tn, kd // tk), (tm, tn), _TN, exchange=exchange)


def _rms_fwd(name, x, gain):
    m, d = x.shape
    tm = _tile(m, 512, 16)

    def body(x_ref, g_ref, o_ref):
        xf = x_ref[...]
        r = lax.rsqrt(jnp.mean(xf * xf, axis=-1, keepdims=True) + RMS_EPS)
        o_ref[...] = ((xf * r) * g_ref[...]).astype(o_ref.dtype)

    return pl.pallas_call(
        body, name=name, grid=(m // tm,),
        in_specs=[pl.BlockSpec((tm, d), lambda i: (i, 0)), pl.BlockSpec((1, d), lambda i: (0, 0))],
        out_specs=pl.BlockSpec((tm, d), lambda i: (i, 0)),
        out_shape=jax.ShapeDtypeStruct((m, d), BF16),
        compiler_params=_cparams(("parallel",)),
    )(x, gain)


def _rms_bwd_rows(xf, gain, dhf):
    r = _rms_scale(xf)
    xh = xf * r
    t = dhf * gain
    dx = r * (t - xh * jnp.mean(t * xh, axis=-1, keepdims=True))
    return dx, jnp.sum(dhf * xh, axis=0, keepdims=True)


def _rms_gain_grad(name, x, gain, dh):
    m, d = x.shape
    tm = _tile(m, 512, 16)

    def body(x_ref, g_ref, dh_ref, dg_ref):
        @pl.when(pl.program_id(0) == 0)
        def _():
            dg_ref[...] = jnp.zeros_like(dg_ref)

        dg_ref[...] += _rms_bwd_rows(x_ref[...], g_ref[...], dh_ref[...])[1]

    row = pl.BlockSpec((tm, d), lambda i: (i, 0))
    vec = pl.BlockSpec((1, d), lambda i: (0, 0))
    return pl.pallas_call(
        body, name=name, grid=(m // tm,), in_specs=[row, vec, row], out_specs=vec,
        out_shape=jax.ShapeDtypeStruct((1, d), F32),
        compiler_params=_cparams(("arbitrary",)),
    )(x, gain, dh)


def _rms_bwd(name, x, gain, dh, dres, exchange):
    m, d = x.shape
    tm = _tile(m, 512, 16)

    def body(ins, outs, scr):
        del scr
        x_ref, g_ref, dh_ref, res_ref = ins
        dx_ref, dxb_ref, dg_ref = outs

        @pl.when(pl.program_id(0) == 0)
        def _():
            dg_ref[...] = jnp.zeros_like(dg_ref)

        dx, dg = _rms_bwd_rows(x_ref[...], g_ref[...], dh_ref[...])
        dx = dx + res_ref[...]
        dx_ref[...] = dx
        dxb_ref[...] = dx.astype(BF16)
        dg_ref[...] += dg

    row = pl.BlockSpec((tm, d), lambda i: (i, 0))
    vec = pl.BlockSpec((1, d), lambda i: (0, 0))
    return _hosted_call(
        name, body, (m // tm,), [row, vec, row, row], [row, row, vec],
        [jax.ShapeDtypeStruct((m, d), F32), jax.ShapeDtypeStruct((m, d), BF16),
         jax.ShapeDtypeStruct((1, d), F32)], [], (x, gain, dh, dres), exchange)


def _loss_head(x, target, gain):
    m, d = x.shape
    tm = _tile(m, 512, 16)

    def body(x_ref, t_ref, g_ref, dx_ref, dxb_ref, dg_ref, loss_ref):
        i = pl.program_id(0)

        @pl.when(i == 0)
        def _():
            dg_ref[...] = jnp.zeros_like(dg_ref)
            loss_ref[...] = jnp.zeros_like(loss_ref)

        xf = x_ref[...]
        r = lax.rsqrt(jnp.mean(xf * xf, axis=-1, keepdims=True) + RMS_EPS)
        xh = xf * r
        g = g_ref[...]
        err = xh * g - t_ref[...]
        loss_ref[...] += jnp.full(loss_ref.shape, (0.5 / d) * jnp.sum(err * err), F32)
        dy = err * (1.0 / d)
        t = dy * g
        dx = r * (t - xh * jnp.mean(t * xh, axis=-1, keepdims=True))
        dx_ref[...] = dx
        dxb_ref[...] = dx.astype(BF16)
        dg_ref[...] += jnp.sum(dy * xh, axis=0, keepdims=True)

    row = pl.BlockSpec((tm, d), lambda i: (i, 0))
    vec = pl.BlockSpec((1, d), lambda i: (0, 0))
    return pl.pallas_call(
        body, name="loss_head", grid=(m // tm,),
        in_specs=[row, row, vec],
        out_specs=[row, row, vec, pl.BlockSpec((1, LANE), lambda i: (0, 0))],
        out_shape=[jax.ShapeDtypeStruct((m, d), F32), jax.ShapeDtypeStruct((m, d), BF16),
                   jax.ShapeDtypeStruct((1, d), F32), jax.ShapeDtypeStruct((1, LANE), F32)],
        compiler_params=_cparams(("arbitrary",)),
    )(x, target, gain)


def _hosted_call(name, body, grid, in_specs, out_specs, out_shape, scratch, operands, exchange):
    n_in, n_out, n_scr = len(in_specs), len(out_specs), len(scratch)
    n_xin, n_xout = len(exchange.ins), len(exchange.out_shape)

    def full_body(*refs):
        pos = 0
        parts = []
        for cnt in (n_in, n_xin, n_out, n_xout, n_scr):
            parts.append(refs[pos:pos + cnt])
            pos += cnt
        ins, x_ins, outs, x_outs, scr = parts
        sems = refs[pos:]
        first = pl.program_id(0) == 0
        last = pl.program_id(0) == grid[0] - 1
        for ax in range(1, len(grid)):
            first = jnp.logical_and(first, pl.program_id(ax) == 0)
            last = jnp.logical_and(last, pl.program_id(ax) == grid[ax] - 1)

        @pl.when(first)
        def _():
            exchange.start(x_ins, x_outs, sems)

        body(ins, outs, scr)

        @pl.when(last)
        def _():
            exchange.finish(x_ins, x_outs, sems)

    aliases = {n_in + k: n_out + k for k in range(n_xin)} if exchange.in_place else {}
    res = pl.pallas_call(
        full_body, name=name, grid=grid,
        in_specs=list(in_specs) + [_HBM] * n_xin, out_specs=list(out_specs) + [_HBM] * n_xout,
        out_shape=list(out_shape) + list(exchange.out_shape),
        scratch_shapes=list(scratch) + list(exchange.sems),
        input_output_aliases=aliases,
        compiler_params=_cparams(("arbitrary",) * len(grid)),
    )(*operands, *exchange.ins)
    return res[:n_out], res[n_out:]


def _ffn_fwd(name, x, h, wg, wu, wd, next_gain, exchange):
    n, d = x.shape
    nb, _, fb = wg.shape
    tm = _tile(n, 512, 16 * FFN_SUBTILES)
    ts = tm // FFN_SUBTILES
    with_norm = next_gain is not None

    def body(ins, outs, scr):
        x_ref, h_ref, wg_ref, wu_ref, wd_ref = ins[:5]
        xo_ref, a_ref, b_ref = outs[:3]
        acc, = scr
        j = pl.program_id(1)

        @pl.when(j == 0)
        def _():
            acc[...] = jnp.zeros_like(acc)

        for s in range(FFN_SUBTILES):
            rows = slice(s * ts, (s + 1) * ts)
            hh = h_ref[rows, :]
            a = _dot(hh, wg_ref[...])
            b = _dot(hh, wu_ref[...])
            a_ref[rows, :] = a.astype(BF16)
            b_ref[rows, :] = b.astype(BF16)
            act = (a * jax.nn.sigmoid(a)) * b
            acc[rows, :] += _dot(act, wd_ref[...])

        @pl.when(j == nb - 1)
        def _():
            xo = x_ref[...] + 0.5 * acc[...]
            xo_ref[...] = xo
            if with_norm:
                outs[3][...] = ((xo * _rms_scale(xo)) * ins[5][...]).astype(BF16)

    row = pl.BlockSpec((tm, d), lambda i, j: (i, 0))
    w_in = pl.BlockSpec((None, d, fb), lambda i, j: (j, 0, 0))
    hid = pl.BlockSpec((None, tm, fb), lambda i, j: (j, i, 0))
    return _hosted_call(
        name, body, (n // tm, nb),
        [row, row, w_in, w_in, pl.BlockSpec((None, fb, d), lambda i, j: (j, 0, 0))]
        + ([pl.BlockSpec((1, d), lambda i, j: (0, 0))] if with_norm else []),
        [row, hid, hid] + ([row] if with_norm else []),
        [jax.ShapeDtypeStruct((n, d), F32), jax.ShapeDtypeStruct((nb, n, fb), BF16),
         jax.ShapeDtypeStruct((nb, n, fb), BF16)]
        + ([jax.ShapeDtypeStruct((n, d), BF16)] if with_norm else []),
        [pltpu.VMEM((tm, d), F32)], (x, h, wg, wu, wd) + ((next_gain,) if with_norm else ()), exchange)


def _ffn_bwd_act(name, dxb, wd, a, b, exchange):
    n, d = dxb.shape
    nb, fb, _ = wd.shape
    subtiles = 2 * FFN_SUBTILES
    tm = _tile(n, 1024, 16 * subtiles)
    ts = tm // subtiles

    def body(ins, outs, scr):
        del scr
        dx_ref, wd_ref, a_ref, b_ref = ins
        da_ref, db_ref, act_ref = outs
        for s in range(subtiles):
            rows = slice(s * ts, (s + 1) * ts)
            dact = 0.5 * _dot(dx_ref[rows, :], wd_ref[...], _NT)
            af = a_ref[rows, :].astype(F32)
            bf = b_ref[rows, :].astype(F32)
            sl, dsl = _silu_parts(af)
            act_ref[rows, :] = (sl * bf).astype(BF16)
            db_ref[rows, :] = (dact * sl).astype(BF16)
            da_ref[rows, :] = (dact * bf * dsl).astype(BF16)

    hid = pl.BlockSpec((None, tm, fb), lambda i, j: (j, i, 0))
    shp = jax.ShapeDtypeStruct((nb, n, fb), BF16)
    return _hosted_call(
        name, body, (n // tm, nb),
        [pl.BlockSpec((tm, d), lambda i, j: (i, 0)), pl.BlockSpec((None, fb, d), lambda i, j: (j, 0, 0)), hid, hid],
        [hid, hid, hid], [shp, shp, shp], [], (dxb, wd, a, b), exchange)


def _proj_rms_bwd(name, pairs, a_spec, b_spec, nk, tm, x, gain, dres, exchange):
    n, d = x.shape
    n_pairs = len(pairs)
    slab = _tile(tm, 128, 16)
    sub = _tile(tm, PROJ_SUBTILE_ROWS, 16)

    def body(ins, outs, scr):
        a_refs, b_refs = ins[:n_pairs], ins[n_pairs:2 * n_pairs]
        x_ref, g_ref, res_ref = ins[2 * n_pairs:]
        dx_ref, dxb_ref, dg_ref = outs
        del scr
        i, k = pl.program_id(0), pl.program_id(1)

        @pl.when(jnp.logical_and(i == 0, k == 0))
        def _():
            dg_ref[...] = jnp.zeros_like(dg_ref)

        @pl.when(k == 0)
        def _():
            dx_ref[...] = jnp.zeros_like(dx_ref)

        for s in range(tm // sub):
            rows = slice(s * sub, (s + 1) * sub)
            part = None
            for a_ref, b_ref in zip(a_refs, b_refs):
                t = _dot(a_ref[rows, :], b_ref[...], _NT)
                part = t if part is None else part + t
            dx_ref[rows, :] += part

        @pl.when(k == nk - 1)
        def _():
            for s in range(tm // slab):
                rows = slice(s * slab, (s + 1) * slab)
                dx, dg = _rms_bwd_rows(x_ref[rows, :], g_ref[...], dx_ref[rows, :])
                dx = dx + res_ref[rows, :]
                dx_ref[rows, :] = dx
                dxb_ref[rows, :] = dx.astype(BF16)
                dg_ref[...] += dg

    row = pl.BlockSpec((tm, d), lambda i, k: (i, 0))
    vec = pl.BlockSpec((1, d), lambda i, k: (0, 0))
    return _hosted_call(
        name, body, (n // tm, nk), [a_spec] * n_pairs + [b_spec] * n_pairs + [row, vec, row],
        [row, row, vec],
        [jax.ShapeDtypeStruct((n, d), F32), jax.ShapeDtypeStruct((n, d), BF16),
         jax.ShapeDtypeStruct((1, d), F32)],
        [], tuple(p[0] for p in pairs) + tuple(p[1] for p in pairs) + (x, gain, dres), exchange)


def _ffn_bwd_dh(name, da, db, wg, wu, exchange):
    nb, n, fb = da.shape
    d = wg.shape[1]
    tm = _tile(n, 1024, 16)
    return _matmul(
        name, [(da, wg), (db, wu)],
        pl.BlockSpec((None, tm, fb), lambda i, j, k: (k, i, 0)),
        pl.BlockSpec((None, d, fb), lambda i, j, k: (k, 0, 0)),
        jax.ShapeDtypeStruct((n, d), F32), pl.BlockSpec((tm, d), lambda i, j, k: (i, 0)),
        (n // tm, 1, nb), (tm, d), _NT, exchange=exchange)


def _mm_nt_rms_bwd(name, a, w, x, gain, dres, exchange, tk=2048):
    n, kd = a.shape
    d = w.shape[0]
    tm, tk = _tile(n, 512, 16), _tile(kd, tk, LANE)
    return _proj_rms_bwd(
        name, [(a, w)],
        pl.BlockSpec((tm, tk), lambda i, k: (i, k)),
        pl.BlockSpec((d, tk), lambda i, k: (0, k)), kd // tk, tm, x, gain, dres, exchange)


def _ffn_bwd_weights(name, h, dxb, da, db, act, exchange):
    n, d = h.shape
    nb, _, fb = da.shape
    tk = _tile(n, 512, 16)
    nk = n // tk

    def body(ins, outs, scr):
        h_ref, dx_ref, da_ref, db_ref, act_ref = ins
        dwg_ref, dwu_ref, dwd_ref = outs
        accg, accu, accd = scr
        k = pl.program_id(1)

        @pl.when(k == 0)
        def _():
            accg[...] = jnp.zeros_like(accg)
            accu[...] = jnp.zeros_like(accu)
            accd[...] = jnp.zeros_like(accd)

        hh = h_ref[...]
        accg[...] += _dot(hh, da_ref[...], _TN)
        accu[...] += _dot(hh, db_ref[...], _TN)
        accd[...] += _dot(act_ref[...], dx_ref[...], _TN)

        @pl.when(k == nk - 1)
        def _():
            dwg_ref[...] = accg[...].astype(BF16)
            dwu_ref[...] = accu[...].astype(BF16)
            dwd_ref[...] = (0.5 * accd[...]).astype(BF16)

    row = pl.BlockSpec((tk, d), lambda j, k: (k, 0))
    hid = pl.BlockSpec((None, tk, fb), lambda j, k: (j, k, 0))
    w_in = pl.BlockSpec((None, d, fb), lambda j, k: (j, 0, 0))
    w_out = pl.BlockSpec((None, fb, d), lambda j, k: (j, 0, 0))
    return _hosted_call(
        name, body, (nb, nk), [row, row, hid, hid, hid], [w_in, w_in, w_out],
        [jax.ShapeDtypeStruct((nb, d, fb), BF16), jax.ShapeDtypeStruct((nb, d, fb), BF16),
         jax.ShapeDtypeStruct((nb, fb, d), BF16)],
        [pltpu.VMEM((d, fb), F32), pltpu.VMEM((d, fb), F32), pltpu.VMEM((fb, d), F32)],
        (h, dxb, da, db, act), exchange)


def _pool_diff(ext_ref, rows, cols, width, t_idx):
    s = ext_ref[POOL_HALO:POOL_HALO + rows, cols]
    for sft in range(1, width):
        s = s + ext_ref[POOL_HALO - sft:POOL_HALO - sft + rows, cols]
    cnt = jnp.minimum(t_idx + 1, width).astype(F32)
    return s / cnt - ext_ref[POOL_HALO:POOL_HALO + rows, cols]


def _pool_fwd(proj, pool_w, pool_scale):
    n = proj.shape[0]
    dp = pool_scale.shape[1]
    c = dp // len(POOL_WINDOWS)
    tm = _tile(n, 512, POOL_HALO)
    hb = tm // POOL_HALO

    def body(u_ref, halo_ref, pw_ref, sc_ref, y_ref, ext):
        i = pl.program_id(0)
        ext[0:POOL_HALO, :] = jnp.where(i > 0, halo_ref[...], 0.0)
        ext[POOL_HALO:, :] = u_ref[...]
        t_idx = i * tm + lax.broadcasted_iota(jnp.int32, (tm, 1), 0)
        for g, width in enumerate(POOL_WINDOWS):
            cols = slice(g * c, (g + 1) * c)
            dgrp = _pool_diff(ext, tm, cols, width, t_idx)
            y_ref[:, cols] = (_dot(dgrp, pw_ref[g]) * sc_ref[:, cols]).astype(BF16)

    return pl.pallas_call(
        body, name="pool_fwd", grid=(n // tm,),
        in_specs=[pl.BlockSpec((tm, dp), lambda i: (i, 0)),
                  pl.BlockSpec((POOL_HALO, dp), lambda i: (jnp.maximum(i * hb - 1, 0), 0)),
                  pl.BlockSpec((len(POOL_WINDOWS), c, c), lambda i: (0, 0, 0)),
                  pl.BlockSpec((1, dp), lambda i: (0, 0))],
        out_specs=pl.BlockSpec((tm, dp), lambda i: (i, 0)),
        out_shape=jax.ShapeDtypeStruct((n, dp), BF16),
        scratch_shapes=[pltpu.VMEM((tm + POOL_HALO, dp), F32)],
        compiler_params=_cparams(("parallel",)),
    )(proj, proj, pool_w, pool_scale)


def _pool_bwd(proj, dymix, pool_w, pool_scale):
    n = proj.shape[0]
    dp = pool_scale.shape[1]
    ng = len(POOL_WINDOWS)
    c = dp // ng
    tm = _tile(n, 512, POOL_HALO)
    hb = tm // POOL_HALO
    nsteps = n // tm
    last_halo = n // POOL_HALO - 1

    def body(u_ref, halo_ref, dy_ref, dyn_ref, pw_ref, sc_ref, du_ref, dpw_ref, dsc_ref, ext, dyext, e_s):
        i = pl.program_id(0)

        @pl.when(i == 0)
        def _():
            dpw_ref[...] = jnp.zeros_like(dpw_ref)
            dsc_ref[...] = jnp.zeros_like(dsc_ref)

        ext[0:POOL_HALO, :] = jnp.where(i > 0, halo_ref[...], 0.0)
        ext[POOL_HALO:, :] = u_ref[...]
        dyext[0:tm, :] = dy_ref[...]
        dyext[tm:, :] = jnp.where(i < nsteps - 1, dyn_ref[...], 0.0)
        t_idx = i * tm + lax.broadcasted_iota(jnp.int32, (tm, 1), 0)
        te_idx = i * tm + lax.broadcasted_iota(jnp.int32, (tm + POOL_HALO, 1), 0)
        for g, width in enumerate(POOL_WINDOWS):
            cols = slice(g * c, (g + 1) * c)
            dgrp = _pool_diff(ext, tm, cols, width, t_idx)
            w_g = pw_ref[g]
            dys = dyext[:, cols] * sc_ref[:, cols]
            ypre = _dot(dgrp, w_g)
            dsc_ref[:, cols] += jnp.sum(dyext[0:tm, cols] * ypre, axis=0, keepdims=True)
            dpw_ref[g] += _dot(dgrp, dys[0:tm], _TN)
            dd = _dot(dys, w_g, _NT)
            e_s[...] = dd / jnp.minimum(te_idx + 1, width).astype(F32)
            acc = e_s[0:tm, :]
            for sft in range(1, width):
                acc = acc + e_s[sft:sft + tm, :]
            du_ref[:, cols] = (acc - dd[0:tm]).astype(BF16)

    return pl.pallas_call(
        body, name="pool_bwd", grid=(nsteps,),
        in_specs=[pl.BlockSpec((tm, dp), lambda i: (i, 0)),
                  pl.BlockSpec((POOL_HALO, dp), lambda i: (jnp.maximum(i * hb - 1, 0), 0)),
                  pl.BlockSpec((tm, dp), lambda i: (i, 0)),
                  pl.BlockSpec((POOL_HALO, dp), lambda i: (jnp.minimum((i + 1) * hb, last_halo), 0)),
                  pl.BlockSpec((ng, c, c), lambda i: (0, 0, 0)),
                  pl.BlockSpec((1, dp), lambda i: (0, 0))],
        out_specs=[pl.BlockSpec((tm, dp), lambda i: (i, 0)),
                   pl.BlockSpec((ng, c, c), lambda i: (0, 0, 0)),
                   pl.BlockSpec((1, dp), lambda i: (0, 0))],
        out_shape=[jax.ShapeDtypeStruct((n, dp), BF16), jax.ShapeDtypeStruct((ng, c, c), F32),
                   jax.ShapeDtypeStruct((1, dp), F32)],
        scratch_shapes=[pltpu.VMEM((tm + POOL_HALO, dp), F32), pltpu.VMEM((tm + POOL_HALO, dp), F32),
                        pltpu.VMEM((tm + POOL_HALO, c), F32)],
        compiler_params=_cparams(("arbitrary",)),
    )(proj, proj, dymix, dymix, pool_w, pool_scale)


def _chunk_cumsum(x):
    row = lax.broadcasted_iota(jnp.int32, x.shape, 0) % CHUNK
    s = 1
    while s < CHUNK:
        x = x + jnp.where(row >= s, pltpu.roll(x, s, 0), 0.0)
        s *= 2
    return x


def _chunk_ends(cum_ref, bend_ref, nc):
    for c in range(nc):
        last = cum_ref[c * CHUNK + CHUNK - 1:(c + 1) * CHUNK, :]
        bend_ref[c * CHUNK:(c + 1) * CHUNK, :] = jnp.broadcast_to(last, (CHUNK, last.shape[1]))


def _gate_logits(alr_ref, wa_ref, ba_ref):
    z = _dot(alr_ref[...], wa_ref[...]) + ba_ref[...]
    la = (jnp.minimum(z, 0.0) - jnp.log(1.0 + jnp.exp(-jnp.abs(z)))) * (1.0 / GATE_TEMP)
    return z, la


def _gla_dims(proj, head_norm):
    n, pw = proj.shape
    dv4 = head_norm.shape[1]
    dk4 = dv4 // 2
    return n, pw, dv4, dk4, dk4 // HEADS, dv4 // HEADS


def _gla_in_specs(t, dk4, dv4, rev):
    alr_blk = (2 * dv4 + 2 * dv4) // LANE
    return [pl.BlockSpec((t, dk4), lambda i: (rev(i), 2)),
            pl.BlockSpec((t, dk4), lambda i: (rev(i), 3)),
            pl.BlockSpec((t, dv4), lambda i: (rev(i), 2)),
            pl.BlockSpec((t, dv4), lambda i: (rev(i), 3)),
            pl.BlockSpec((t, LANE), lambda i: (rev(i), alr_blk))]


def _gla_fwd(proj, wa2p, b_a, head_norm):
    n, _, dv4, dk4, dk, dv = _gla_dims(proj, head_norm)
    t = _tile(n, GLA_ROWS, CHUNK)
    nc = t // CHUNK
    qscale = dk ** -0.5

    def body(q_ref, k_ref, v_ref, g_ref, alr_ref, wa_ref, ba_ref, hn_ref, y_ref, st_ref,
             state, cum_s, bend_s, o_s):
        i = pl.program_id(0)

        @pl.when(i == 0)
        def _():
            state[...] = jnp.zeros_like(state)

        _, la = _gate_logits(alr_ref, wa_ref, ba_ref)
        cum_s[...] = _chunk_cumsum(la)
        _chunk_ends(cum_s, bend_s, nc)
        kd = (k_ref[...] * jnp.exp(bend_s[...] - cum_s[...])).astype(BF16)
        qs = (q_ref[...] * qscale).astype(BF16)
        vv = v_ref[...].astype(BF16)
        units = [(c, h) for c in range(nc) for h in range(HEADS)]
        for c, h in units:
            rows = slice(c * CHUNK, (c + 1) * CHUNK)
            st_ref[c, h] = _dot(vv[rows, h * dv:(h + 1) * dv], kd[rows, h * dk:(h + 1) * dk], _TN)
        for c, h in units:
            e = jnp.exp(bend_s[c * CHUNK:c * CHUNK + 1, h * dk:(h + 1) * dk])
            s_new = state[h] * e + st_ref[c, h]
            state[h] = s_new
            st_ref[c, h] = s_new
        for c, h in units:
            rows = slice(c * CHUNK, (c + 1) * CHUNK)
            o_s[rows, h * dv:(h + 1) * dv] = _dot(qs[rows, h * dk:(h + 1) * dk], st_ref[c, h], _NT)
        for h in range(HEADS):
            hv = slice(h * dv, (h + 1) * dv)
            o = o_s[:, hv]
            gg = g_ref[:, hv]
            y_ref[:, hv] = (((o * _rms_scale(o)) * hn_ref[:, hv]) * (gg * jax.nn.sigmoid(gg))).astype(BF16)

    full = lambda shape: pl.BlockSpec(shape, lambda i: tuple(0 for _ in shape))
    return pl.pallas_call(
        body, name="gla_fwd", grid=(n // t,),
        in_specs=_gla_in_specs(t, dk4, dv4, lambda i: i)
        + [full((LANE, dk4)), full((1, dk4)), full((1, dv4))],
        out_specs=[pl.BlockSpec((t, dv4), lambda i: (i, 0)),
                   pl.BlockSpec((nc, HEADS, dv, dk), lambda i: (i, 0, 0, 0))],
        out_shape=[jax.ShapeDtypeStruct((n, dv4), BF16),
                   jax.ShapeDtypeStruct((n // CHUNK, HEADS, dv, dk), F32)],
        scratch_shapes=[pltpu.VMEM((HEADS, dv, dk), F32), pltpu.VMEM((t, dk4), F32),
                        pltpu.VMEM((t, dk4), F32), pltpu.VMEM((t, dv4), F32)],
        compiler_params=_cparams(("arbitrary",)),
    )(proj, proj, proj, proj, proj, wa2p, b_a, head_norm)


def _gla_bwd(proj, dymix, du, states, wa2p, b_a, head_norm):
    n, pw, dv4, dk4, dk, dv = _gla_dims(proj, head_norm)
    t = _tile(n, GLA_ROWS, CHUNK)
    nc = t // CHUNK
    nsteps = n // t
    qscale = dk ** -0.5
    rev = lambda i: nsteps - 1 - i

    def body(q_ref, k_ref, v_ref, g_ref, alr_ref, dy_ref, du_ref, st_ref, prev_ref, wa_ref, ba_ref, hn_ref,
             dp_ref, dwa_ref, dba_ref, dhn_ref, carry, cum_s, bend_s, gst_s, dkd_s, dee_s, o_s, do_s):
        i = pl.program_id(0)

        @pl.when(i == 0)
        def _():
            carry[...] = jnp.zeros_like(carry)
            dwa_ref[...] = jnp.zeros_like(dwa_ref)
            dba_ref[...] = jnp.zeros_like(dba_ref)
            dhn_ref[...] = jnp.zeros_like(dhn_ref)

        first_step = i == nsteps - 1
        z, la = _gate_logits(alr_ref, wa_ref, ba_ref)
        cum_s[...] = _chunk_cumsum(la)
        _chunk_ends(cum_s, bend_s, nc)
        dec = jnp.exp(bend_s[...] - cum_s[...])
        kd_f = k_ref[...] * dec
        kd = kd_f.astype(BF16)
        qs = (q_ref[...] * qscale).astype(BF16)
        vv = v_ref[...].astype(BF16)
        dp_ref[:, 0:dv4] = du_ref[...]
        units = [(c, h) for c in range(nc) for h in range(HEADS)]

        for c, h in units:
            rows = slice(c * CHUNK, (c + 1) * CHUNK)
            o_s[rows, h * dv:(h + 1) * dv] = _dot(qs[rows, h * dk:(h + 1) * dk], st_ref[c, h], _NT)
        for h in range(HEADS):
            hv = slice(h * dv, (h + 1) * dv)
            o = o_s[:, hv]
            r = _rms_scale(o)
            oh = o * r
            sl, dsl = _silu_parts(g_ref[:, hv])
            dyh = dy_ref[:, hv]
            hn = hn_ref[:, hv]
            tt = dyh * sl
            dhn_ref[:, hv] += jnp.sum(tt * oh, axis=0, keepdims=True)
            dp_ref[:, 3 * dv4 + h * dv:3 * dv4 + (h + 1) * dv] = (dyh * (oh * hn) * dsl).astype(BF16)
            tt = tt * hn
            do_s[:, hv] = (r * (tt - oh * jnp.mean(tt * oh, axis=-1, keepdims=True))).astype(BF16)
        for c, h in units:
            rows = slice(c * CHUNK, (c + 1) * CHUNK)
            hk = slice(h * dk, (h + 1) * dk)
            do = do_s[rows, h * dv:(h + 1) * dv]
            gst_s[c, h] = _dot(do, qs[rows, hk], _TN)
            dp_ref[rows, dv4 + h * dk:dv4 + (h + 1) * dk] = (_dot(do, st_ref[c, h]) * qscale).astype(BF16)

        for c, h in reversed(units):
            g_n = carry[h] + gst_s[c, h]
            gst_s[c, h] = g_n
            carry[h] = g_n * jnp.exp(bend_s[c * CHUNK:c * CHUNK + 1, h * dk:(h + 1) * dk])

        for c, h in units:
            rows = slice(c * CHUNK, (c + 1) * CHUNK)
            hk = slice(h * dk, (h + 1) * dk)
            hv = slice(h * dv, (h + 1) * dv)
            g_n = gst_s[c, h]
            if c > 0:
                s_prev = st_ref[c - 1, h]
            else:
                s_prev = jnp.where(first_step, 0.0, prev_ref[0, h])
            dkd_s[rows, hk] = _dot(vv[rows, hv], g_n)
            dp_ref[rows, 2 * dv4 + h * dv:2 * dv4 + (h + 1) * dv] = _dot(kd[rows, hk], g_n, _NT).astype(BF16)
            dee = jnp.sum(g_n * s_prev, axis=0, keepdims=True) * jnp.exp(bend_s[c * CHUNK:c * CHUNK + 1, hk])
            dee_s[rows, hk] = jnp.broadcast_to(dee, (CHUNK, dk))

        dkd = dkd_s[...]
        dp_ref[:, dv4 + dk4:dv4 + 2 * dk4] = (dkd * dec).astype(BF16)
        w = dkd * kd_f
        dla = (_chunk_cumsum(w) - w) + dee_s[...]
        dz = dla * (1.0 / GATE_TEMP) * (1.0 - jax.nn.sigmoid(z))
        dp_ref[:, 4 * dv4:4 * dv4 + LANE] = _dot(dz, wa_ref[...], _NT).astype(BF16)
        dwa_ref[...] += _dot(alr_ref[...], dz, _TN)
        dba_ref[...] += jnp.sum(dz, axis=0, keepdims=True)

    full = lambda shape: pl.BlockSpec(shape, lambda i: tuple(0 for _ in shape))
    return pl.pallas_call(
        body, name="gla_bwd", grid=(nsteps,),
        in_specs=_gla_in_specs(t, dk4, dv4, rev)
        + [pl.BlockSpec((t, dv4), lambda i: (rev(i), 1)),
           pl.BlockSpec((t, dv4), lambda i: (rev(i), 0)),
           pl.BlockSpec((nc, HEADS, dv, dk), lambda i: (rev(i), 0, 0, 0)),
           pl.BlockSpec((1, HEADS, dv, dk), lambda i: (jnp.maximum(rev(i) * nc - 1, 0), 0, 0, 0)),
           full((LANE, dk4)), full((1, dk4)), full((1, dv4))],
        out_specs=[pl.BlockSpec((t, pw), lambda i: (rev(i), 0)),
                   full((LANE, dk4)), full((1, dk4)), full((1, dv4))],
        out_shape=[jax.ShapeDtypeStruct((n, pw), BF16), jax.ShapeDtypeStruct((LANE, dk4), F32),
                   jax.ShapeDtypeStruct((1, dk4), F32), jax.ShapeDtypeStruct((1, dv4), F32)],
        scratch_shapes=[pltpu.VMEM((HEADS, dv, dk), F32), pltpu.VMEM((t, dk4), F32),
                        pltpu.VMEM((t, dk4), F32), pltpu.VMEM((nc, HEADS, dv, dk), F32),
                        pltpu.VMEM((t, dk4), F32), pltpu.VMEM((t, dk4), F32),
                        pltpu.VMEM((t, dv4), F32), pltpu.VMEM((t, dv4), BF16)],
        compiler_params=_cparams(("arbitrary",)),
    )(proj, proj, proj, proj, proj, dymix, du, states, states, wa2p, b_a, head_norm)


def _softmax(s):
    p = jnp.exp(s - jnp.max(s, axis=-1, keepdims=True))
    return p / jnp.sum(p, axis=-1, keepdims=True)


def _attn_fwd(q, kv):
    n, d = q.shape
    m = kv.shape[0]
    dh = d // HEADS
    tm = _tile(n, 512, 16)
    scale = dh ** -0.5

    def body(q_ref, k_ref, v_ref, o_ref, s_s, p_s):
        for h in range(HEADS):
            hs = slice(h * dh, (h + 1) * dh)
            s_s[:, h * m:(h + 1) * m] = _dot(q_ref[:, hs], k_ref[:, hs], _NT) * scale
        for h in range(HEADS):
            hm = slice(h * m, (h + 1) * m)
            p_s[:, hm] = _softmax(s_s[:, hm]).astype(BF16)
        for h in range(HEADS):
            hs = slice(h * dh, (h + 1) * dh)
            o_ref[:, hs] = _dot(p_s[:, h * m:(h + 1) * m], v_ref[:, hs]).astype(BF16)

    return pl.pallas_call(
        body, name="attn_fwd", grid=(n // tm,),
        in_specs=[pl.BlockSpec((tm, d), lambda i: (i, 0)), pl.BlockSpec((m, d), lambda i: (0, 0)),
                  pl.BlockSpec((m, d), lambda i: (0, 1))],
        out_specs=pl.BlockSpec((tm, d), lambda i: (i, 0)),
        out_shape=jax.ShapeDtypeStruct((n, d), BF16),
        scratch_shapes=[pltpu.VMEM((tm, HEADS * m), F32), pltpu.VMEM((tm, HEADS * m), BF16)],
        compiler_params=_cparams(("parallel",)),
    )(q, kv, kv)


def _attn_bwd(q, kv, do):
    n, d = q.shape
    m = kv.shape[0]
    dh = d // HEADS
    tm = _tile(n, 512, 16)
    scale = dh ** -0.5

    def body(q_ref, k_ref, v_ref, do_ref, dq_ref, dk_ref, dv_ref, s_s, dp_s, p_s, ds_s):
        i = pl.program_id(0)

        @pl.when(i == 0)
        def _():
            dk_ref[...] = jnp.zeros_like(dk_ref)
            dv_ref[...] = jnp.zeros_like(dv_ref)

        for h in range(HEADS):
            hs, hm = slice(h * dh, (h + 1) * dh), slice(h * m, (h + 1) * m)
            s_s[:, hm] = _dot(q_ref[:, hs], k_ref[:, hs], _NT) * scale
            dp_s[:, hm] = _dot(do_ref[:, hs], v_ref[:, hs], _NT)
        for h in range(HEADS):
            hm = slice(h * m, (h + 1) * m)
            p = _softmax(s_s[:, hm])
            dp = dp_s[:, hm]
            p_s[:, hm] = p.astype(BF16)
            ds_s[:, hm] = (p * (dp - jnp.sum(dp * p, axis=-1, keepdims=True)) * scale).astype(BF16)
        for h in range(HEADS):
            hs, hm = slice(h * dh, (h + 1) * dh), slice(h * m, (h + 1) * m)
            dv_ref[:, hs] += _dot(p_s[:, hm], do_ref[:, hs], _TN)
            dq_ref[:, hs] = _dot(ds_s[:, hm], k_ref[:, hs]).astype(BF16)
            dk_ref[:, hs] += _dot(ds_s[:, hm], q_ref[:, hs], _TN)

    row = pl.BlockSpec((tm, d), lambda i: (i, 0))
    memb = pl.BlockSpec((m, d), lambda i: (0, 0))
    return pl.pallas_call(
        body, name="attn_bwd", grid=(n // tm,),
        in_specs=[row, memb, pl.BlockSpec((m, d), lambda i: (0, 1)), row],
        out_specs=[row, memb, memb],
        out_shape=[jax.ShapeDtypeStruct((n, d), BF16), jax.ShapeDtypeStruct((m, d), F32),
                   jax.ShapeDtypeStruct((m, d), F32)],
        scratch_shapes=[pltpu.VMEM((tm, HEADS * m), F32), pltpu.VMEM((tm, HEADS * m), F32),
                        pltpu.VMEM((tm, HEADS * m), BF16), pltpu.VMEM((tm, HEADS * m), BF16)],
        compiler_params=_cparams(("arbitrary",)),
    )(q, kv, kv, do)


def _adamw(name, w, m, v, parts, own_block):
    r, c = w.shape
    tr = _tile(r, max(16, ELEMENTWISE_BLOCK // c), 16)
    row = pl.BlockSpec((tr, c), lambda i, o: (i, 0))
    ops, specs = [w, m, v], [row, row, row]
    for p in parts:
        if p.ndim == 2:
            ops.append(p)
            specs.append(row)
        elif p.shape[0] == 4:
            ops.append(p)
            specs.append(pl.BlockSpec((None, tr, c), lambda i, o: (o[0], i, 0)))
        else:
            for s in range(p.shape[0]):
                ops.append(p)
                specs.append(pl.BlockSpec((None, tr, c), lambda i, o, s=s: (s, i, 0)))
    n_parts = len(ops) - 3
    c1 = 1.0 - ADAM_B1 ** ADAM_STEP
    c2 = 1.0 - ADAM_B2 ** ADAM_STEP

    def body(o_ref, *refs):
        del o_ref
        w_ref, m_ref, v_ref = refs[:3]
        g_refs = refs[3:3 + n_parts]
        go_ref, d_ref, mo_ref, vo_ref = refs[3 + n_parts:]
        g = g_refs[0][...].astype(F32)
        for g_ref in g_refs[1:]:
            g = g + g_ref[...].astype(F32)
        m_new = ADAM_B1 * m_ref[...] + (1.0 - ADAM_B1) * g
        v_new = ADAM_B2 * v_ref[...] + (1.0 - ADAM_B2) * (g * g)
        m_hat = m_new / c1
        v_hat = v_new / c2
        go_ref[...] = g
        d_ref[...] = -ADAM_LR * (m_hat / (jnp.sqrt(v_hat) + ADAM_EPS) + ADAM_WD * w_ref[...])
        mo_ref[...] = m_new
        vo_ref[...] = v_new

    shp = jax.ShapeDtypeStruct((r, c), F32)
    return pl.pallas_call(
        body, name=name,
        grid_spec=pltpu.PrefetchScalarGridSpec(
            num_scalar_prefetch=1, grid=(r // tr,), in_specs=specs, out_specs=[row] * 4),
        out_shape=[shp] * 4,
        compiler_params=_cparams(("parallel",)),
    )(own_block, *ops)


def _pair_add(name, g, recvd, core):
    _, r, c = g.shape
    tr = _tile(r, max(16, ELEMENTWISE_BLOCK // c), 16)

    def body(core_ref, a_ref, b_ref, o_ref):
        del core_ref
        o_ref[...] = (a_ref[...].astype(F32) + b_ref[...].astype(F32)).astype(o_ref.dtype)

    blk = pl.BlockSpec((None, tr, c), lambda s, i, core_ref: (s, i, 0))
    mine = pl.BlockSpec((None, tr, c), lambda s, i, core_ref: (2 * s + core_ref[0], i, 0))
    return pl.pallas_call(
        body, name=name,
        grid_spec=pltpu.PrefetchScalarGridSpec(
            num_scalar_prefetch=1, grid=(4, r // tr), in_specs=[mine, blk], out_specs=blk),
        out_shape=jax.ShapeDtypeStruct(recvd.shape, g.dtype),
        compiler_params=_cparams(("parallel", "parallel")),
    )(core, g, recvd)


def _position():
    return lax.axis_index("x"), lax.axis_index("y"), lax.axis_index("c")


_HBM = pl.BlockSpec(memory_space=pltpu.HBM)

_Exchange = collections.namedtuple("_Exchange", "ins out_shape sems start finish in_place", defaults=(False,))
_NO_EXCHANGE = _Exchange((), (), (), lambda ins, outs, sems: None, lambda ins, outs, sems: None)


def _placed(shard):
    x, y, c = _position()
    start = (4 * x + 2 * y + c,) + (0,) * shard.ndim
    return lax.dynamic_update_slice(lax.empty((N_DEV,) + shard.shape, shard.dtype), shard[None], start)


def _all_gather(name, placed):
    n = len(placed)

    def body(*refs):
        ins, outs = refs[:n], refs[n:2 * n]
        send_sems, recv_sems = refs[2 * n:]
        x, y, c = _position()
        me, sibling = (x, y, c), (x, y, 1 - c)
        chips = [(1 - x, y), (x, 1 - y), (1 - x, 1 - y)]

        def copy(a, k, owner, to):
            blk = 4 * owner[0] + 2 * owner[1] + owner[2]
            return pltpu.make_async_remote_copy(
                src_ref=ins[a].at[blk], dst_ref=outs[a].at[blk],
                send_sem=send_sems.at[7 * a + k], recv_sem=recv_sems.at[7 * a + k],
                device_id=to, device_id_type=MESH)

        sends = []
        for a in range(n):
            sends.append(copy(a, 0, me, sibling))
            sends += [copy(a, 1 + j, me, (*chip, c)) for j, chip in enumerate(chips)]
        for cp in sends:
            cp.start()
        for j, chip in enumerate(chips):
            for a in range(n):
                copy(a, 1 + j, (*chip, c), me).wait_recv()
                fwd = copy(a, 4 + j, (*chip, c), sibling)
                fwd.start()
                sends.append(fwd)
        for a in range(n):
            copy(a, 0, sibling, me).wait_recv()
            for j, chip in enumerate(chips):
                copy(a, 4 + j, (*chip, 1 - c), me).wait_recv()
        for cp in sends:
            cp.wait_send()

    return pl.pallas_call(
        body, name=name,
        in_specs=[_HBM] * n, out_specs=[_HBM] * n,
        out_shape=[jax.ShapeDtypeStruct(p.shape, p.dtype) for p in placed],
        input_output_aliases={a: a for a in range(n)},
        scratch_shapes=[pltpu.SemaphoreType.DMA((7 * n,)), pltpu.SemaphoreType.DMA((7 * n,))],
    )(*placed)


def _pair_exchange(grads):
    n = len(grads)

    def start(ins, recvd, sems):
        send_sems, recv_sems = sems
        x, y, c = _position()
        for a in range(n):
            for chip in range(4):
                pltpu.make_async_remote_copy(
                    src_ref=ins[a].at[2 * chip + (1 - c)], dst_ref=recvd[a].at[chip],
                    send_sem=send_sems.at[a], recv_sem=recv_sems.at[a],
                    device_id=(x, y, 1 - c), device_id_type=MESH).start()

    def finish(ins, recvd, sems):
        del ins
        send_sems, recv_sems = sems
        x, y, c = _position()
        for a in range(n):
            pltpu.make_async_remote_copy(
                src_ref=recvd[a], dst_ref=recvd[a], send_sem=send_sems.at[a], recv_sem=recv_sems.at[a],
                device_id=(x, y, 1 - c), device_id_type=MESH).wait()

    return _Exchange(
        ins=tuple(grads),
        out_shape=tuple(jax.ShapeDtypeStruct((4,) + g.shape[1:], g.dtype) for g in grads),
        sems=(pltpu.SemaphoreType.DMA((n,)), pltpu.SemaphoreType.DMA((n,))),
        start=start, finish=finish)


def _run_exchange(name, exchange):
    n_in, n_out = len(exchange.ins), len(exchange.out_shape)

    def body(*refs):
        ins, outs, sems = refs[:n_in], refs[n_in:n_in + n_out], refs[n_in + n_out:]
        exchange.start(ins, outs, sems)
        exchange.finish(ins, outs, sems)

    return pl.pallas_call(
        body, name=name, in_specs=[_HBM] * n_in, out_specs=[_HBM] * n_out,
        out_shape=list(exchange.out_shape), scratch_shapes=list(exchange.sems),
        input_output_aliases={k: k for k in range(n_in)} if exchange.in_place else {},
    )(*exchange.ins)


def _chip_scatter_exchange(sums):
    n = len(sums)
    offsets = [(1, 0), (0, 1), (1, 1)]

    def start(ins, recvd, sems):
        send_sems, recv_sems = sems
        x, y, c = _position()
        for a in range(n):
            for r, (ox, oy) in enumerate(offsets):
                px = 1 - x if ox else x
                py = 1 - y if oy else y
                pltpu.make_async_remote_copy(
                    src_ref=ins[a].at[2 * px + py], dst_ref=recvd[a].at[r],
                    send_sem=send_sems.at[a], recv_sem=recv_sems.at[a],
                    device_id=(px, py, c), device_id_type=MESH).start()

    def finish(ins, recvd, sems):
        del ins
        send_sems, recv_sems = sems
        x, y, c = _position()
        for a in range(n):
            pltpu.make_async_remote_copy(
                src_ref=recvd[a], dst_ref=recvd[a], send_sem=send_sems.at[a], recv_sem=recv_sems.at[a],
                device_id=(x, y, c), device_id_type=MESH).wait()

    return _Exchange(
        ins=tuple(sums),
        out_shape=tuple(jax.ShapeDtypeStruct((3,) + s.shape[1:], s.dtype) for s in sums),
        sems=(pltpu.SemaphoreType.DMA((n,)), pltpu.SemaphoreType.DMA((n,))),
        start=start, finish=finish)


def _gather_spread_exchange(placed):
    n = len(placed)

    def peers():
        x, y, c = _position()
        return (x, y, c), [(x, y, 1 - c), (1 - x, y, c), (x, 1 - y, c), (1 - x, 1 - y, c)]

    def copy(ins, outs, sems, a, k, owner, to):
        blk = 4 * owner[0] + 2 * owner[1] + owner[2]
        return pltpu.make_async_remote_copy(
            src_ref=ins[a].at[blk], dst_ref=outs[a].at[blk],
            send_sem=sems[0].at[4 * a + k], recv_sem=sems[1].at[4 * a + k],
            device_id=to, device_id_type=MESH)

    def start(ins, outs, sems):
        me, others = peers()
        for a in range(n):
            for k, to in enumerate(others):
                copy(ins, outs, sems, a, k, me, to).start()

    def finish(ins, outs, sems):
        _, others = peers()
        for a in range(n):
            for k, peer in enumerate(others):
                cp = copy(ins, outs, sems, a, k, peer, peer)
                cp.wait_recv()
                cp.wait_send()

    return _Exchange(
        ins=tuple(placed), out_shape=tuple(jax.ShapeDtypeStruct(p.shape, p.dtype) for p in placed),
        sems=(pltpu.SemaphoreType.DMA((4 * n,)), pltpu.SemaphoreType.DMA((4 * n,))),
        start=start, finish=finish, in_place=True)


def _gather_forward_exchange(partial):
    n = len(partial)

    def copy(ins, outs, sems, a, j, pc):
        x, y, c = _position()
        chips = [(1 - x, y), (x, 1 - y), (1 - x, 1 - y)]
        blk = 4 * chips[j][0] + 2 * chips[j][1] + pc
        return pltpu.make_async_remote_copy(
            src_ref=ins[a].at[blk], dst_ref=outs[a].at[blk],
            send_sem=sems[0].at[3 * a + j], recv_sem=sems[1].at[3 * a + j],
            device_id=(x, y, 1 - c), device_id_type=MESH)

    def start(ins, outs, sems):
        c = lax.axis_index("c")
        for a in range(n):
            for j in range(3):
                copy(ins, outs, sems, a, j, c).start()

    def finish(ins, outs, sems):
        c = lax.axis_index("c")
        for a in range(n):
            for j in range(3):
                copy(ins, outs, sems, a, j, 1 - c).wait_recv()
                copy(ins, outs, sems, a, j, c).wait_send()

    return _Exchange(
        ins=tuple(partial), out_shape=tuple(jax.ShapeDtypeStruct(p.shape, p.dtype) for p in partial),
        sems=(pltpu.SemaphoreType.DMA((3 * n,)), pltpu.SemaphoreType.DMA((3 * n,))),
        start=start, finish=finish, in_place=True)


def _all_reduce_small(vec):
    r = vec.shape[0]

    def body(v_ref, o_ref, gbuf, send_sems, recv_sems):
        x, y, c = _position()
        me = 4 * x + 2 * y + c
        gbuf[me] = v_ref[...]
        copies = []
        for k in range(1, N_DEV):
            ox, oy, oc = (k >> 2) & 1, (k >> 1) & 1, k & 1
            peer = (1 - x if ox else x, 1 - y if oy else y, 1 - c if oc else c)
            cp = pltpu.make_async_remote_copy(
                src_ref=gbuf.at[me], dst_ref=gbuf.at[me], send_sem=send_sems.at[k - 1],
                recv_sem=recv_sems.at[k - 1], device_id=peer, device_id_type=MESH)
            cp.start()
            copies.append(cp)
        for cp in copies:
            cp.wait()
        total = gbuf[0]
        for j in range(1, N_DEV):
            total = total + gbuf[j]
        o_ref[...] = total

    return pl.pallas_call(
        body, name="all_reduce_small",
        in_specs=[pl.BlockSpec(memory_space=pltpu.VMEM)],
        out_specs=pl.BlockSpec(memory_space=pltpu.VMEM),
        out_shape=jax.ShapeDtypeStruct(vec.shape, F32),
        scratch_shapes=[pltpu.VMEM((N_DEV, r, LANE), F32), pltpu.SemaphoreType.DMA((N_DEV - 1,)),
                        pltpu.SemaphoreType.DMA((N_DEV - 1,))],
    )(vec)


def _cols_from_blocks(g):
    nb, r, cs = g.shape
    return jnp.transpose(g, (1, 0, 2)).reshape(r, nb * cs)


def _cols_to_blocks(w):
    r, cfull = w.shape
    return jnp.transpose(w.reshape(r, N_DEV, cfull // N_DEV), (1, 0, 2))


def kernel(x, mem, ffn1_norm, ffn1_w_gate, ffn1_w_up, ffn1_w_down, mix_norm, w_in, pool_w, pool_scale, gla_w_a2, gla_b_a, gla_head_norm, w_out, xattn_norm, mem_norm, xattn_w_q, xattn_w_kv, xattn_w_o, ffn2_norm, ffn2_w_gate, ffn2_w_up, ffn2_w_down, final_norm, loss_target, m_ffn1_norm, m_ffn1_w_gate, m_ffn1_w_up, m_ffn1_w_down, m_mix_norm, m_w_in, m_pool_w, m_pool_scale, m_gla_w_a2, m_gla_b_a, m_gla_head_norm, m_w_out, m_xattn_norm, m_mem_norm, m_xattn_w_q, m_xattn_w_kv, m_xattn_w_o, m_ffn2_norm, m_ffn2_w_gate, m_ffn2_w_up, m_ffn2_w_down, m_final_norm, v_ffn1_norm, v_ffn1_w_gate, v_ffn1_w_up, v_ffn1_w_down, v_mix_norm, v_w_in, v_pool_w, v_pool_scale, v_gla_w_a2, v_gla_b_a, v_gla_head_norm, v_w_out, v_xattn_norm, v_mem_norm, v_xattn_w_q, v_xattn_w_kv, v_xattn_w_o, v_ffn2_norm, v_ffn2_w_gate, v_ffn2_w_up, v_ffn2_w_down, v_final_norm):
    weights = dict(ffn1_norm=ffn1_norm, ffn1_w_gate=ffn1_w_gate, ffn1_w_up=ffn1_w_up, ffn1_w_down=ffn1_w_down, mix_norm=mix_norm, w_in=w_in, pool_w=pool_w, pool_scale=pool_scale, gla_w_a2=gla_w_a2, gla_b_a=gla_b_a, gla_head_norm=gla_head_norm, w_out=w_out, xattn_norm=xattn_norm, mem_norm=mem_norm, xattn_w_q=xattn_w_q, xattn_w_kv=xattn_w_kv, xattn_w_o=xattn_w_o, ffn2_norm=ffn2_norm, ffn2_w_gate=ffn2_w_gate, ffn2_w_up=ffn2_w_up, ffn2_w_down=ffn2_w_down, final_norm=final_norm)
    mom1 = dict(ffn1_norm=m_ffn1_norm, ffn1_w_gate=m_ffn1_w_gate, ffn1_w_up=m_ffn1_w_up, ffn1_w_down=m_ffn1_w_down, mix_norm=m_mix_norm, w_in=m_w_in, pool_w=m_pool_w, pool_scale=m_pool_scale, gla_w_a2=m_gla_w_a2, gla_b_a=m_gla_b_a, gla_head_norm=m_gla_head_norm, w_out=m_w_out, xattn_norm=m_xattn_norm, mem_norm=m_mem_norm, xattn_w_q=m_xattn_w_q, xattn_w_kv=m_xattn_w_kv, xattn_w_o=m_xattn_w_o, ffn2_norm=m_ffn2_norm, ffn2_w_gate=m_ffn2_w_gate, ffn2_w_up=m_ffn2_w_up, ffn2_w_down=m_ffn2_w_down, final_norm=m_final_norm)
    mom2 = dict(ffn1_norm=v_ffn1_norm, ffn1_w_gate=v_ffn1_w_gate, ffn1_w_up=v_ffn1_w_up, ffn1_w_down=v_ffn1_w_down, mix_norm=v_mix_norm, w_in=v_w_in, pool_w=v_pool_w, pool_scale=v_pool_scale, gla_w_a2=v_gla_w_a2, gla_b_a=v_gla_b_a, gla_head_norm=v_gla_head_norm, w_out=v_w_out, xattn_norm=v_xattn_norm, mem_norm=v_mem_norm, xattn_w_q=v_xattn_w_q, xattn_w_kv=v_xattn_w_kv, xattn_w_o=v_xattn_w_o, ffn2_norm=v_ffn2_norm, ffn2_w_gate=v_ffn2_w_gate, ffn2_w_up=v_ffn2_w_up, ffn2_w_down=v_ffn2_w_down, final_norm=v_final_norm)
    order = list(weights.keys())

    n, d = x.shape[1], x.shape[2]
    mlen = mem.shape[1]
    x0 = x.reshape(n, d)
    memf = mem.reshape(mlen, d)
    target = loss_target.reshape(n, d)
    dpool = d // 2
    in_cols = w_in.shape[2] * N_DEV
    proj_cols = 2 * d + LANE
    rank = gla_w_a2.shape[1]

    def shard(name):
        return _placed(weights[name][0].astype(BF16))

    wg1, wu1, wd1 = _all_gather("ag_ffn1", [shard("ffn1_w_gate"), shard("ffn1_w_up"), shard("ffn1_w_down")])
    later = ["w_in", "pool_w", "gla_w_a2", "w_out", "xattn_w_q", "xattn_w_kv", "xattn_w_o",
             "ffn2_w_gate", "ffn2_w_up", "ffn2_w_down"]
    h1 = _rms_fwd("rms_ffn1", x0, ffn1_norm)
    (x1, a1, b1, h2), spread = _ffn_fwd(
        "ffn1_fwd", x0, h1, wg1, wu1, wd1, mix_norm, _gather_spread_exchange([shard(k) for k in later]))
    win_g, = _run_exchange("ag_forward_w_in", _gather_forward_exchange(spread[:1]))
    winp = jnp.pad(_cols_from_blocks(win_g), ((0, 0), (0, proj_cols - in_cols)))
    proj, (pw_g, wa2_g, wout_g, wq_g, wkv_g, wo_g, wg2, wu2, wd2) = _mm_nn(
        "mix_in", [(h2, winp)], F32, exchange=_gather_forward_exchange(spread[1:]), tn=1408)

    pw = jnp.transpose(pw_g, (1, 0, 2, 3)).reshape(len(POOL_WINDOWS), dpool // 4, dpool // 4)
    wa2p = jnp.pad(_cols_from_blocks(wa2_g), ((0, LANE - rank), (0, 0)))
    wout = wout_g.reshape(d, d)
    wq = wq_g.reshape(d, d)
    wkv = _cols_from_blocks(wkv_g)
    wo = wo_g.reshape(d, d)
    fnorm = final_norm.reshape(1, d)

    ypool = _pool_fwd(proj, pw, pool_scale)
    ygla, states = _gla_fwd(proj, wa2p, gla_b_a, gla_head_norm)
    x2, h3 = _mm_nn("mix_out", [(ypool, wout[:dpool]), (ygla, wout[dpool:])], F32, res=x1,
                    norm_gain=xattn_norm, tk=1024)
    mh = _rms_fwd("rms_mem", memf, mem_norm)
    q = _mm_nn("xattn_q", [(h3, wq)], BF16)
    kv = _mm_nn("xattn_kv", [(mh, wkv)], BF16)
    att = _attn_fwd(q, kv)
    x3, h4 = _mm_nn("xattn_o", [(att, wo)], F32, res=x2, norm_gain=ffn2_norm)
    (x4, a2, b2), _ = _ffn_fwd("ffn2_fwd", x3, h4, wg2, wu2, wd2, None, _NO_EXCHANGE)
    dx4, dx4b, g_final, loss_part = _loss_head(x4, target, fnorm)

    grads = {}
    core = lax.axis_index("c").astype(jnp.int32).reshape(1)
    chip = (2 * lax.axis_index("x") + lax.axis_index("y")).astype(jnp.int32).reshape(1)
    chip_sums = {}

    def pair_blocks(names):
        return [grads[k].reshape(N_DEV, -1, grads[k].shape[-1]) for k in names]

    def pair_sums(names, blocks, recvd):
        for k, g, r in zip(names, blocks, recvd):
            chip_sums[k] = _pair_add("rs_add_" + k, g, r, core)

    ffn2_names = ["ffn2_w_gate", "ffn2_w_up", "ffn2_w_down"]
    (da2, db2, act2), _ = _ffn_bwd_act("ffn2_bwd_act", dx4b, wd2, a2, b2, _NO_EXCHANGE)
    (grads["ffn2_w_gate"], grads["ffn2_w_up"], grads["ffn2_w_down"]), _ = _ffn_bwd_weights(
        "ffn2_bwd_weights", h4, dx4b, da2, db2, act2, _NO_EXCHANGE)
    blocks = pair_blocks(ffn2_names)
    dh4, _ = _ffn_bwd_dh("ffn2_bwd_dh", da2, db2, wg2, wu2, _NO_EXCHANGE)
    (dx3, dx3b, g_ffn2_norm), recvd = _rms_bwd(
        "rms_ffn2_bwd", x3, ffn2_norm, dh4, dx4, _pair_exchange(blocks))
    pair_sums(ffn2_names, blocks, recvd)

    xattn_names = ["xattn_w_q", "xattn_w_kv", "xattn_w_o"]
    datt = _mm_nt("xattn_do", [(dx3b, wo)], BF16)
    grads["xattn_w_o"] = _mm_tn("xattn_dwo", [(att, dx3b)], BF16).reshape(N_DEV, d // N_DEV, d)
    dq, dk, dv = _attn_bwd(q, kv, datt)
    grads["xattn_w_q"] = _mm_tn("xattn_dwq", [(h3, dq)], BF16).reshape(N_DEV, d // N_DEV, d)
    dkv = jnp.concatenate([dk, dv], axis=1)
    dmh = _mm_nt("xattn_dmh", [(dkv, wkv)], F32)
    grads["xattn_w_kv"] = _cols_to_blocks(_mm_tn("xattn_dwkv", [(mh, dkv)], BF16))
    g_mem_norm = _rms_gain_grad("rms_mem_bwd", memf, mem_norm, dmh)
    blocks = pair_blocks(xattn_names)
    (dx2, dx2b, g_xattn_norm), recvd = _mm_nt_rms_bwd(
        "xattn_dh", dq, wq, x2, xattn_norm, dx3, _pair_exchange(blocks), tk=2048)
    pair_sums(xattn_names, blocks, recvd)

    dymix = _mm_nt("mix_dy", [(dx2b, wout)], F32)
    grads["w_out"] = jnp.concatenate(
        [_mm_tn("mix_dwout_pool", [(ypool, dx2b)], BF16), _mm_tn("mix_dwout_gla", [(ygla, dx2b)], BF16)],
        axis=0).reshape(N_DEV, d // N_DEV, d)
    du, g_pool_w, g_pool_scale = _pool_bwd(proj, dymix, pw, pool_scale)
    dproj, g_wa2p, g_b_a, g_head_norm = _gla_bwd(proj, dymix, du, states, wa2p, gla_b_a, gla_head_norm)
    dwin, recvd = _mm_tn("mix_dwin", [(h2, dproj)], BF16,
                         exchange=_chip_scatter_exchange([chip_sums[ffn2_names[0]]]), tn=1408)
    chip_recvd = {ffn2_names[0]: recvd[0]}
    grads["w_in"] = _cols_to_blocks(dwin[:, :in_cols])
    grads["pool_w"] = jnp.transpose(
        g_pool_w.reshape(len(POOL_WINDOWS), N_DEV, dpool // 4 // N_DEV, dpool // 4), (1, 0, 2, 3))
    grads["gla_w_a2"] = _cols_to_blocks(g_wa2p[:rank])
    mix_names = ["w_in", "pool_w", "gla_w_a2", "w_out"]
    blocks = pair_blocks(mix_names)
    (dx1, dx1b, g_mix_norm), recvd = _mm_nt_rms_bwd(
        "mix_dh", dproj, winp, x1, mix_norm, dx2, _pair_exchange(blocks), tk=1408)
    pair_sums(mix_names, blocks, recvd)

    ffn1_names = ["ffn1_w_gate", "ffn1_w_up", "ffn1_w_down"]
    (da1, db1, act1), recvd = _ffn_bwd_act(
        "ffn1_bwd_act", dx1b, wd1, a1, b1, _chip_scatter_exchange([chip_sums[k] for k in ffn2_names[1:]]))
    chip_recvd.update(zip(ffn2_names[1:], recvd))
    (grads["ffn1_w_gate"], grads["ffn1_w_up"], grads["ffn1_w_down"]), recvd = _ffn_bwd_weights(
        "ffn1_bwd_weights", h1, dx1b, da1, db1, act1,
        _chip_scatter_exchange([chip_sums[k] for k in xattn_names + mix_names]))
    chip_recvd.update(zip(xattn_names + mix_names, recvd))
    blocks = pair_blocks(ffn1_names)
    pair_sums(ffn1_names, blocks, _run_exchange("rs_pair_ffn1", _pair_exchange(blocks)))
    dh1, recvd = _ffn_bwd_dh(
        "ffn1_bwd_dh", da1, db1, wg1, wu1, _chip_scatter_exchange([chip_sums[k] for k in ffn1_names]))
    chip_recvd.update(zip(ffn1_names, recvd))
    (dx0, _, g_ffn1_norm), _ = _rms_bwd("rms_ffn1_bwd", x0, ffn1_norm, dh1, dx1, _NO_EXCHANGE)

    small = [("ffn1_norm", g_ffn1_norm), ("mix_norm", g_mix_norm), ("pool_scale", g_pool_scale),
             ("gla_b_a", g_b_a), ("gla_head_norm", g_head_norm), ("xattn_norm", g_xattn_norm),
             ("mem_norm", g_mem_norm), ("ffn2_norm", g_ffn2_norm), ("final_norm", g_final)]
    packed = jnp.concatenate([g.reshape(-1) for _, g in small] + [loss_part.reshape(-1)])
    slab = 8 * LANE
    padded = -(-packed.shape[0] // slab) * slab
    packed = jnp.pad(packed, (0, padded - packed.shape[0])).reshape(padded // LANE, LANE)
    reduced = _all_reduce_small(packed).reshape(-1)
    small_grads = {}
    off = 0
    for name, g in small:
        small_grads[name] = reduced[off:off + g.size]
        off += g.size
    loss = reduced[off]

    out_g, out_d, out_m, out_v = {}, {}, {}, {}
    for k in order:
        w = weights[k]
        if k in small_grads:
            w2 = w.reshape(1, -1)
            parts = [small_grads[k].reshape(1, -1)]
            own_block = jnp.zeros((1,), jnp.int32)
        else:
            w2 = w.reshape(-1, w.shape[-1])
            parts = [chip_sums[k], chip_recvd[k]]
            own_block = chip
        res = _adamw("adamw_" + k, w2, mom1[k].reshape(w2.shape), mom2[k].reshape(w2.shape), parts, own_block)
        out_g[k], out_d[k], out_m[k], out_v[k] = [r.reshape(w.shape) for r in res]

    return (loss, dx0.reshape(x.shape), *[out_g[k] for k in order], *[out_d[k] for k in order],
            *[out_m[k] for k in order], *[out_v[k] for k in order])
```

```python
import collections

import jax
import jax.numpy as jnp
from jax import lax
from jax.experimental import pallas as pl
from jax.experimental.pallas import tpu as pltpu

F32 = jnp.float32
BF16 = jnp.bfloat16
MESH = pl.DeviceIdType.MESH

N_DEV = 8
CHUNK = 64
POOL_WINDOWS = (2, 4, 8, 16)
POOL_HALO = 16
HEADS = 4
GATE_TEMP = 16.0
RMS_EPS = 1e-6
LANE = 128
V7X_VMEM_BYTES = 64 * 1024 * 1024
VMEM_LIMIT = V7X_VMEM_BYTES - 8 * 1024 * 1024
GLA_ROWS = 8 * CHUNK
ELEMENTWISE_BLOCK = 512 * 1024
FFN_SUBTILES = 2
PROJ_SUBTILE_ROWS = 256

ADAM_LR = 0.001
ADAM_B1 = 0.9
ADAM_B2 = 0.999
ADAM_EPS = 1e-08
ADAM_WD = 0.01
ADAM_STEP = 10

_NN = (((1,), (0,)), ((), ()))
_NT = (((1,), (1,)), ((), ()))
_TN = (((0,), (0,)), ((), ()))


def _cparams(sem=None):
    return pltpu.CompilerParams(dimension_semantics=sem, vmem_limit_bytes=VMEM_LIMIT)


def _tile(n, pref, align):
    t = (min(pref, n) // align) * align
    while t >= align:
        if n % t == 0:
            return t
        t -= align
    return n


def _dot(a, b, dims=_NN):
    return lax.dot_general(a.astype(BF16), b.astype(BF16), dims, preferred_element_type=F32)


def _silu_parts(z):
    sig = jax.nn.sigmoid(z)
    return z * sig, sig * (1.0 + z * (1.0 - sig))


def _rms_scale(xf):
    return lax.rsqrt(jnp.mean(xf * xf, axis=-1, keepdims=True) + RMS_EPS)


def _matmul(name, pairs, a_spec, b_spec, out_shape, out_spec, grid, acc_shape, dims,
            res=None, res_spec=None, scale=None, norm_gain=None, exchange=None):
    n = len(pairs)
    nk = grid[2]

    def body(ins, outs, scr):
        a_refs, b_refs = ins[:n], ins[n:2 * n]
        pos = 2 * n
        res_ref = gain_ref = h_ref = None
        if res is not None:
            res_ref = ins[pos]
            pos += 1
        if norm_gain is not None:
            gain_ref = ins[pos]
            h_ref = outs[1]
        o_ref = outs[0]
        acc, = scr
        k = pl.program_id(2)

        @pl.when(k == 0)
        def _():
            acc[...] = jnp.zeros_like(acc)

        part = None
        for a_ref, b_ref in zip(a_refs, b_refs):
            d = _dot(a_ref[...], b_ref[...], dims)
            part = d if part is None else part + d
        acc[...] += part

        @pl.when(k == nk - 1)
        def _():
            r = acc[...]
            if scale is not None:
                r = r * scale
            if res_ref is not None:
                r = r + res_ref[...]
            o_ref[...] = r.astype(o_ref.dtype)
            if h_ref is not None:
                h_ref[...] = ((r * _rms_scale(r)) * gain_ref[...]).astype(BF16)

    ops = [p[0] for p in pairs] + [p[1] for p in pairs]
    specs = [a_spec] * n + [b_spec] * n
    if res is not None:
        ops.append(res)
        specs.append(res_spec)
    out_specs, out_shapes = [out_spec], [out_shape]
    if norm_gain is not None:
        ops.append(norm_gain)
        specs.append(pl.BlockSpec(norm_gain.shape, lambda i, j, k: (0, 0)))
        out_specs.append(out_spec)
        out_shapes.append(jax.ShapeDtypeStruct(out_shape.shape, BF16))
    outs, carried = _hosted_call(
        name, body, grid, specs, out_specs, out_shapes, [pltpu.VMEM(acc_shape, F32)], ops,
        _NO_EXCHANGE if exchange is None else exchange)
    result = outs[0] if norm_gain is None else tuple(outs)
    return result if exchange is None else (result, carried)


def _mm_nn(name, pairs, out_dtype, res=None, norm_gain=None, exchange=None, tm=1024, tn=1024, tk=2048):
    m, kd = pairs[0][0].shape
    nd = pairs[0][1].shape[1]
    if norm_gain is not None:
        tm, tn = 512, nd
    tm, tn, tk = _tile(m, tm, 16), _tile(nd, tn, LANE), _tile(kd, tk, LANE)
    return _matmul(
        name, pairs,
        pl.BlockSpec((tm, tk), lambda i, j, k: (i, k)),
        pl.BlockSpec((tk, tn), lambda i, j, k: (k, j)),
        jax.ShapeDtypeStruct((m, nd), out_dtype),
        pl.BlockSpec((tm, tn), lambda i, j, k: (i, j)),
        (m // tm, nd // tn, kd // tk), (tm, tn), _NN,
        res=res, res_spec=pl.BlockSpec((tm, tn), lambda i, j, k: (i, j)), norm_gain=norm_gain,
        exchange=exchange)


def _mm_nt(name, pairs, out_dtype, tm=1024, tn=1024, tk=2048):
    m, kd = pairs[0][0].shape
    nd = pairs[0][1].shape[0]
    tm, tn, tk = _tile(m, tm, 16), _tile(nd, tn, LANE), _tile(kd, tk, LANE)
    return _matmul(
        name, pairs,
        pl.BlockSpec((tm, tk), lambda i, j, k: (i, k)),
        pl.BlockSpec((tn, tk), lambda i, j, k: (j, k)),
        jax.ShapeDtypeStruct((m, nd), out_dtype),
        pl.BlockSpec((tm, tn), lambda i, j, k: (i, j)),
        (m // tm, nd // tn, kd // tk), (tm, tn), _NT)


def _mm_tn(name, pairs, out_dtype, exchange=None, tm=1024, tn=2048, tk=1024):
    kd, m = pairs[0][0].shape
    nd = pairs[0][1].shape[1]
    tm, tn, tk = _tile(m, tm, LANE), _tile(nd, tn, LANE), _tile(kd, tk, 16)
    return _matmul(
        name, pairs,
        pl.BlockSpec((tk, tm), lambda i, j, k: (k, i)),
        pl.BlockSpec((tk, tn), lambda i, j, k: (k, j)),
        jax.ShapeDtypeStruct((m, nd), out_dtype),
        pl.BlockSpec((tm, tn), lambda i, j, k: (i, j)),
        (m // tm, nd // tn, kd // tk), (tm, tn), _TN, exchange=exchange)


def _rms_fwd(name, x, gain):
    m, d = x.shape
    tm = _tile(m, 512, 16)

    def body(x_ref, g_ref, o_ref):
        xf = x_ref[...]
        r = lax.rsqrt(jnp.mean(xf * xf, axis=-1, keepdims=True) + RMS_EPS)
        o_ref[...] = ((xf * r) * g_ref[...]).astype(o_ref.dtype)

    return pl.pallas_call(
        body, name=name, grid=(m // tm,),
        in_specs=[pl.BlockSpec((tm, d), lambda i: (i, 0)), pl.BlockSpec((1, d), lambda i: (0, 0))],
        out_specs=pl.BlockSpec((tm, d), lambda i: (i, 0)),
        out_shape=jax.ShapeDtypeStruct((m, d), BF16),
        compiler_params=_cparams(("parallel",)),
    )(x, gain)


def _rms_bwd_rows(xf, gain, dhf):
    r = _rms_scale(xf)
    xh = xf * r
    t = dhf * gain
    dx = r * (t - xh * jnp.mean(t * xh, axis=-1, keepdims=True))
    return dx, jnp.sum(dhf * xh, axis=0, keepdims=True)


def _rms_gain_grad(name, x, gain, dh):
    m, d = x.shape
    tm = _tile(m, 512, 16)

    def body(x_ref, g_ref, dh_ref, dg_ref):
        @pl.when(pl.program_id(0) == 0)
        def _():
            dg_ref[...] = jnp.zeros_like(dg_ref)

        dg_ref[...] += _rms_bwd_rows(x_ref[...], g_ref[...], dh_ref[...])[1]

    row = pl.BlockSpec((tm, d), lambda i: (i, 0))
    vec = pl.BlockSpec((1, d), lambda i: (0, 0))
    return pl.pallas_call(
        body, name=name, grid=(m // tm,), in_specs=[row, vec, row], out_specs=vec,
        out_shape=jax.ShapeDtypeStruct((1, d), F32),
        compiler_params=_cparams(("arbitrary",)),
    )(x, gain, dh)


def _rms_bwd(name, x, gain, dh, dres, exchange):
    m, d = x.shape
    tm = _tile(m, 512, 16)

    def body(ins, outs, scr):
        del scr
        x_ref, g_ref, dh_ref, res_ref = ins
        dx_ref, dxb_ref, dg_ref = outs

        @pl.when(pl.program_id(0) == 0)
        def _():
            dg_ref[...] = jnp.zeros_like(dg_ref)

        dx, dg = _rms_bwd_rows(x_ref[...], g_ref[...], dh_ref[...])
        dx = dx + res_ref[...]
        dx_ref[...] = dx
        dxb_ref[...] = dx.astype(BF16)
        dg_ref[...] += dg

    row = pl.BlockSpec((tm, d), lambda i: (i, 0))
    vec = pl.BlockSpec((1, d), lambda i: (0, 0))
    return _hosted_call(
        name, body, (m // tm,), [row, vec, row, row], [row, row, vec],
        [jax.ShapeDtypeStruct((m, d), F32), jax.ShapeDtypeStruct((m, d), BF16),
         jax.ShapeDtypeStruct((1, d), F32)], [], (x, gain, dh, dres), exchange)


def _loss_head(x, target, gain):
    m, d = x.shape
    tm = _tile(m, 512, 16)

    def body(x_ref, t_ref, g_ref, dx_ref, dxb_ref, dg_ref, loss_ref):
        i = pl.program_id(0)

        @pl.when(i == 0)
        def _():
            dg_ref[...] = jnp.zeros_like(dg_ref)
            loss_ref[...] = jnp.zeros_like(loss_ref)

        xf = x_ref[...]
        r = lax.rsqrt(jnp.mean(xf * xf, axis=-1, keepdims=True) + RMS_EPS)
        xh = xf * r
        g = g_ref[...]
        err = xh * g - t_ref[...]
        loss_ref[...] += jnp.full(loss_ref.shape, (0.5 / d) * jnp.sum(err * err), F32)
        dy = err * (1.0 / d)
        t = dy * g
        dx = r * (t - xh * jnp.mean(t * xh, axis=-1, keepdims=True))
        dx_ref[...] = dx
        dxb_ref[...] = dx.astype(BF16)
        dg_ref[...] += jnp.sum(dy * xh, axis=0, keepdims=True)

    row = pl.BlockSpec((tm, d), lambda i: (i, 0))
    vec = pl.BlockSpec((1, d), lambda i: (0, 0))
    return pl.pallas_call(
        body, name="loss_head", grid=(m // tm,),
        in_specs=[row, row, vec],
        out_specs=[row, row, vec, pl.BlockSpec((1, LANE), lambda i: (0, 0))],
        out_shape=[jax.ShapeDtypeStruct((m, d), F32), jax.ShapeDtypeStruct((m, d), BF16),
                   jax.ShapeDtypeStruct((1, d), F32), jax.ShapeDtypeStruct((1, LANE), F32)],
        compiler_params=_cparams(("arbitrary",)),
    )(x, target, gain)


def _hosted_call(name, body, grid, in_specs, out_specs, out_shape, scratch, operands, exchange):
    n_in, n_out, n_scr = len(in_specs), len(out_specs), len(scratch)
    n_xin, n_xout = len(exchange.ins), len(exchange.out_shape)

    def full_body(*refs):
        pos = 0
        parts = []
        for cnt in (n_in, n_xin, n_out, n_xout, n_scr):
            parts.append(refs[pos:pos + cnt])
            pos += cnt
        ins, x_ins, outs, x_outs, scr = parts
        sems = refs[pos:]
        first = pl.program_id(0) == 0
        last = pl.program_id(0) == grid[0] - 1
        for ax in range(1, len(grid)):
            first = jnp.logical_and(first, pl.program_id(ax) == 0)
            last = jnp.logical_and(last, pl.program_id(ax) == grid[ax] - 1)

        @pl.when(first)
        def _():
            exchange.start(x_ins, x_outs, sems)

        body(ins, outs, scr)

        @pl.when(last)
        def _():
            exchange.finish(x_ins, x_outs, sems)

    aliases = {n_in + k: n_out + k for k in range(n_xin)} if exchange.in_place else {}
    res = pl.pallas_call(
        full_body, name=name, grid=grid,
        in_specs=list(in_specs) + [_HBM] * n_xin, out_specs=list(out_specs) + [_HBM] * n_xout,
        out_shape=list(out_shape) + list(exchange.out_shape),
        scratch_shapes=list(scratch) + list(exchange.sems),
        input_output_aliases=aliases,
        compiler_params=_cparams(("arbitrary",) * len(grid)),
    )(*operands, *exchange.ins)
    return res[:n_out], res[n_out:]


def _ffn_fwd(name, x, h, wg, wu, wd, next_gain, exchange):
    n, d = x.shape
    nb, _, fb = wg.shape
    tm = _tile(n, 512, 16 * FFN_SUBTILES)
    ts = tm // FFN_SUBTILES
    with_norm = next_gain is not None

    def body(ins, outs, scr):
        x_ref, h_ref, wg_ref, wu_ref, wd_ref = ins[:5]
        xo_ref, a_ref, b_ref = outs[:3]
        acc, = scr
        j = pl.program_id(1)

        @pl.when(j == 0)
        def _():
            acc[...] = jnp.zeros_like(acc)

        for s in range(FFN_SUBTILES):
            rows = slice(s * ts, (s + 1) * ts)
            hh = h_ref[rows, :]
            a = _dot(hh, wg_ref[...])
            b = _dot(hh, wu_ref[...])
            a_ref[rows, :] = a.astype(BF16)
            b_ref[rows, :] = b.astype(BF16)
            act = (a * jax.nn.sigmoid(a)) * b
            acc[rows, :] += _dot(act, wd_ref[...])

        @pl.when(j == nb - 1)
        def _():
            xo = x_ref[...] + 0.5 * acc[...]
            xo_ref[...] = xo
            if with_norm:
                outs[3][...] = ((xo * _rms_scale(xo)) * ins[5][...]).astype(BF16)

    row = pl.BlockSpec((tm, d), lambda i, j: (i, 0))
    w_in = pl.BlockSpec((None, d, fb), lambda i, j: (j, 0, 0))
    hid = pl.BlockSpec((None, tm, fb), lambda i, j: (j, i, 0))
    return _hosted_call(
        name, body, (n // tm, nb),
        [row, row, w_in, w_in, pl.BlockSpec((None, fb, d), lambda i, j: (j, 0, 0))]
        + ([pl.BlockSpec((1, d), lambda i, j: (0, 0))] if with_norm else []),
        [row, hid, hid] + ([row] if with_norm else []),
        [jax.ShapeDtypeStruct((n, d), F32), jax.ShapeDtypeStruct((nb, n, fb), BF16),
         jax.ShapeDtypeStruct((nb, n, fb), BF16)]
        + ([jax.ShapeDtypeStruct((n, d), BF16)] if with_norm else []),
        [pltpu.VMEM((tm, d), F32)], (x, h, wg, wu, wd) + ((next_gain,) if with_norm else ()), exchange)


def _ffn_bwd_act(name, dxb, wd, a, b, exchange):
    n, d = dxb.shape
    nb, fb, _ = wd.shape
    subtiles = 2 * FFN_SUBTILES
    tm = _tile(n, 1024, 16 * subtiles)
    ts = tm // subtiles

    def body(ins, outs, scr):
        del scr
        dx_ref, wd_ref, a_ref, b_ref = ins
        da_ref, db_ref, act_ref = outs
        for s in range(subtiles):
            rows = slice(s * ts, (s + 1) * ts)
            dact = 0.5 * _dot(dx_ref[rows, :], wd_ref[...], _NT)
            af = a_ref[rows, :].astype(F32)
            bf = b_ref[rows, :].astype(F32)
            sl, dsl = _silu_parts(af)
            act_ref[rows, :] = (sl * bf).astype(BF16)
            db_ref[rows, :] = (dact * sl).astype(BF16)
            da_ref[rows, :] = (dact * bf * dsl).astype(BF16)

    hid = pl.BlockSpec((None, tm, fb), lambda i, j: (j, i, 0))
    shp = jax.ShapeDtypeStruct((nb, n, fb), BF16)
    return _hosted_call(
        name, body, (n // tm, nb),
        [pl.BlockSpec((tm, d), lambda i, j: (i, 0)), pl.BlockSpec((None, fb, d), lambda i, j: (j, 0, 0)), hid, hid],
        [hid, hid, hid], [shp, shp, shp], [], (dxb, wd, a, b), exchange)


def _proj_rms_bwd(name, pairs, a_spec, b_spec, nk, tm, x, gain, dres, exchange):
    n, d = x.shape
    n_pairs = len(pairs)
    slab = _tile(tm, 128, 16)
    sub = _tile(tm, PROJ_SUBTILE_ROWS, 16)

    def body(ins, outs, scr):
        a_refs, b_refs = ins[:n_pairs], ins[n_pairs:2 * n_pairs]
        x_ref, g_ref, res_ref = ins[2 * n_pairs:]
        dx_ref, dxb_ref, dg_ref = outs
        del scr
        i, k = pl.program_id(0), pl.program_id(1)

        @pl.when(jnp.logical_and(i == 0, k == 0))
        def _():
            dg_ref[...] = jnp.zeros_like(dg_ref)

        @pl.when(k == 0)
        def _():
            dx_ref[...] = jnp.zeros_like(dx_ref)

        for s in range(tm // sub):
            rows = slice(s * sub, (s + 1) * sub)
            part = None
            for a_ref, b_ref in zip(a_refs, b_refs):
                t = _dot(a_ref[rows, :], b_ref[...], _NT)
                part = t if part is None else part + t
            dx_ref[rows, :] += part

        @pl.when(k == nk - 1)
        def _():
            for s in range(tm // slab):
                rows = slice(s * slab, (s + 1) * slab)
                dx, dg = _rms_bwd_rows(x_ref[rows, :], g_ref[...], dx_ref[rows, :])
                dx = dx + res_ref[rows, :]
                dx_ref[rows, :] = dx
                dxb_ref[rows, :] = dx.astype(BF16)
                dg_ref[...] += dg

    row = pl.BlockSpec((tm, d), lambda i, k: (i, 0))
    vec = pl.BlockSpec((1, d), lambda i, k: (0, 0))
    return _hosted_call(
        name, body, (n // tm, nk), [a_spec] * n_pairs + [b_spec] * n_pairs + [row, vec, row],
        [row, row, vec],
        [jax.ShapeDtypeStruct((n, d), F32), jax.ShapeDtypeStruct((n, d), BF16),
         jax.ShapeDtypeStruct((1, d), F32)],
        [], tuple(p[0] for p in pairs) + tuple(p[1] for p in pairs) + (x, gain, dres), exchange)


def _ffn_bwd_dh(name, da, db, wg, wu, exchange):
    nb, n, fb = da.shape
    d = wg.shape[1]
    tm = _tile(n, 1024, 16)
    return _matmul(
        name, [(da, wg), (db, wu)],
        pl.BlockSpec((None, tm, fb), lambda i, j, k: (k, i, 0)),
        pl.BlockSpec((None, d, fb), lambda i, j, k: (k, 0, 0)),
        jax.ShapeDtypeStruct((n, d), F32), pl.BlockSpec((tm, d), lambda i, j, k: (i, 0)),
        (n // tm, 1, nb), (tm, d), _NT, exchange=exchange)


def _mm_nt_rms_bwd(name, a, w, x, gain, dres, exchange, tk=2048):
    n, kd = a.shape
    d = w.shape[0]
    tm, tk = _tile(n, 512, 16), _tile(kd, tk, LANE)
    return _proj_rms_bwd(
        name, [(a, w)],
        pl.BlockSpec((tm, tk), lambda i, k: (i, k)),
        pl.BlockSpec((d, tk), lambda i, k: (0, k)), kd // tk, tm, x, gain, dres, exchange)


def _ffn_bwd_weights(name, h, dxb, da, db, act, exchange):
    n, d = h.shape
    nb, _, fb = da.shape
    tk = _tile(n, 512, 16)
    nk = n // tk

    def body(ins, outs, scr):
        h_ref, dx_ref, da_ref, db_ref, act_ref = ins
        dwg_ref, dwu_ref, dwd_ref = outs
        accg, accu, accd = scr
        k = pl.program_id(1)

        @pl.when(k == 0)
        def _():
            accg[...] = jnp.zeros_like(accg)
            accu[...] = jnp.zeros_like(accu)
            accd[...] = jnp.zeros_like(accd)

        hh = h_ref[...]
        accg[...] += _dot(hh, da_ref[...], _TN)
        accu[...] += _dot(hh, db_ref[...], _TN)
        accd[...] += _dot(act_ref[...], dx_ref[...], _TN)

        @pl.when(k == nk - 1)
        def _():
            dwg_ref[...] = accg[...].astype(BF16)
            dwu_ref[...] = accu[...].astype(BF16)
            dwd_ref[...] = (0.5 * accd[...]).astype(BF16)

    row = pl.BlockSpec((tk, d), lambda j, k: (k, 0))
    hid = pl.BlockSpec((None, tk, fb), lambda j, k: (j, k, 0))
    w_in = pl.BlockSpec((None, d, fb), lambda j, k: (j, 0, 0))
    w_out = pl.BlockSpec((None, fb, d), lambda j, k: (j, 0, 0))
    return _hosted_call(
        name, body, (nb, nk), [row, row, hid, hid, hid], [w_in, w_in, w_out],
        [jax.ShapeDtypeStruct((nb, d, fb), BF16), jax.ShapeDtypeStruct((nb, d, fb), BF16),
         jax.ShapeDtypeStruct((nb, fb, d), BF16)],
        [pltpu.VMEM((d, fb), F32), pltpu.VMEM((d, fb), F32), pltpu.VMEM((fb, d), F32)],
        (h, dxb, da, db, act), exchange)


def _pool_diff(ext_ref, rows, cols, width, t_idx):
    s = ext_ref[POOL_HALO:POOL_HALO + rows, cols]
    for sft in range(1, width):
        s = s + ext_ref[POOL_HALO - sft:POOL_HALO - sft + rows, cols]
    cnt = jnp.minimum(t_idx + 1, width).astype(F32)
    return s / cnt - ext_ref[POOL_HALO:POOL_HALO + rows, cols]


def _pool_fwd(proj, pool_w, pool_scale):
    n = proj.shape[0]
    dp = pool_scale.shape[1]
    c = dp // len(POOL_WINDOWS)
    tm = _tile(n, 512, POOL_HALO)
    hb = tm // POOL_HALO

    def body(u_ref, halo_ref, pw_ref, sc_ref, y_ref, ext):
        i = pl.program_id(0)
        ext[0:POOL_HALO, :] = jnp.where(i > 0, halo_ref[...], 0.0)
        ext[POOL_HALO:, :] = u_ref[...]
        t_idx = i * tm + lax.broadcasted_iota(jnp.int32, (tm, 1), 0)
        for g, width in enumerate(POOL_WINDOWS):
            cols = slice(g * c, (g + 1) * c)
            dgrp = _pool_diff(ext, tm, cols, width, t_idx)
            y_ref[:, cols] = (_dot(dgrp, pw_ref[g]) * sc_ref[:, cols]).astype(BF16)

    return pl.pallas_call(
        body, name="pool_fwd", grid=(n // tm,),
        in_specs=[pl.BlockSpec((tm, dp), lambda i: (i, 0)),
                  pl.BlockSpec((POOL_HALO, dp), lambda i: (jnp.maximum(i * hb - 1, 0), 0)),
                  pl.BlockSpec((len(POOL_WINDOWS), c, c), lambda i: (0, 0, 0)),
                  pl.BlockSpec((1, dp), lambda i: (0, 0))],
        out_specs=pl.BlockSpec((tm, dp), lambda i: (i, 0)),
        out_shape=jax.ShapeDtypeStruct((n, dp), BF16),
        scratch_shapes=[pltpu.VMEM((tm + POOL_HALO, dp), F32)],
        compiler_params=_cparams(("parallel",)),
    )(proj, proj, pool_w, pool_scale)


def _pool_bwd(proj, dymix, pool_w, pool_scale):
    n = proj.shape[0]
    dp = pool_scale.shape[1]
    ng = len(POOL_WINDOWS)
    c = dp // ng
    tm = _tile(n, 512, POOL_HALO)
    hb = tm // POOL_HALO
    nsteps = n // tm
    last_halo = n // POOL_HALO - 1

    def body(u_ref, halo_ref, dy_ref, dyn_ref, pw_ref, sc_ref, du_ref, dpw_ref, dsc_ref, ext, dyext, e_s):
        i = pl.program_id(0)

        @pl.when(i == 0)
        def _():
            dpw_ref[...] = jnp.zeros_like(dpw_ref)
            dsc_ref[...] = jnp.zeros_like(dsc_ref)

        ext[0:POOL_HALO, :] = jnp.where(i > 0, halo_ref[...], 0.0)
        ext[POOL_HALO:, :] = u_ref[...]
        dyext[0:tm, :] = dy_ref[...]
        dyext[tm:, :] = jnp.where(i < nsteps - 1, dyn_ref[...], 0.0)
        t_idx = i * tm + lax.broadcasted_iota(jnp.int32, (tm, 1), 0)
        te_idx = i * tm + lax.broadcasted_iota(jnp.int32, (tm + POOL_HALO, 1), 0)
        for g, width in enumerate(POOL_WINDOWS):
            cols = slice(g * c, (g + 1) * c)
            dgrp = _pool_diff(ext, tm, cols, width, t_idx)
            w_g = pw_ref[g]
            dys = dyext[:, cols] * sc_ref[:, cols]
            ypre = _dot(dgrp, w_g)
            dsc_ref[:, cols] += jnp.sum(dyext[0:tm, cols] * ypre, axis=0, keepdims=True)
            dpw_ref[g] += _dot(dgrp, dys[0:tm], _TN)
            dd = _dot(dys, w_g, _NT)
            e_s[...] = dd / jnp.minimum(te_idx + 1, width).astype(F32)
            acc = e_s[0:tm, :]
            for sft in range(1, width):
                acc = acc + e_s[sft:sft + tm, :]
            du_ref[:, cols] = (acc - dd[0:tm]).astype(BF16)

    return pl.pallas_call(
        body, name="pool_bwd", grid=(nsteps,),
        in_specs=[pl.BlockSpec((tm, dp), lambda i: (i, 0)),
                  pl.BlockSpec((POOL_HALO, dp), lambda i: (jnp.maximum(i * hb - 1, 0), 0)),
                  pl.BlockSpec((tm, dp), lambda i: (i, 0)),
                  pl.BlockSpec((POOL_HALO, dp), lambda i: (jnp.minimum((i + 1) * hb, last_halo), 0)),
                  pl.BlockSpec((ng, c, c), lambda i: (0, 0, 0)),
                  pl.BlockSpec((1, dp), lambda i: (0, 0))],
        out_specs=[pl.BlockSpec((tm, dp), lambda i: (i, 0)),
                   pl.BlockSpec((ng, c, c), lambda i: (0, 0, 0)),
                   pl.BlockSpec((1, dp), lambda i: (0, 0))],
        out_shape=[jax.ShapeDtypeStruct((n, dp), BF16), jax.ShapeDtypeStruct((ng, c, c), F32),
                   jax.ShapeDtypeStruct((1, dp), F32)],
        scratch_shapes=[pltpu.VMEM((tm + POOL_HALO, dp), F32), pltpu.VMEM((tm + POOL_HALO, dp), F32),
                        pltpu.VMEM((tm + POOL_HALO, c), F32)],
        compiler_params=_cparams(("arbitrary",)),
    )(proj, proj, dymix, dymix, pool_w, pool_scale)


def _chunk_cumsum(x):
    row = lax.broadcasted_iota(jnp.int32, x.shape, 0) % CHUNK
    s = 1
    while s < CHUNK:
        x = x + jnp.where(row >= s, pltpu.roll(x, s, 0), 0.0)
        s *= 2
    return x


def _chunk_ends(cum_ref, bend_ref, nc):
    for c in range(nc):
        last = cum_ref[c * CHUNK + CHUNK - 1:(c + 1) * CHUNK, :]
        bend_ref[c * CHUNK:(c + 1) * CHUNK, :] = jnp.broadcast_to(last, (CHUNK, last.shape[1]))


def _gate_logits(alr_ref, wa_ref, ba_ref):
    z = _dot(alr_ref[...], wa_ref[...]) + ba_ref[...]
    la = (jnp.minimum(z, 0.0) - jnp.log(1.0 + jnp.exp(-jnp.abs(z)))) * (1.0 / GATE_TEMP)
    return z, la


def _gla_dims(proj, head_norm):
    n, pw = proj.shape
    dv4 = head_norm.shape[1]
    dk4 = dv4 // 2
    return n, pw, dv4, dk4, dk4 // HEADS, dv4 // HEADS


def _gla_in_specs(t, dk4, dv4, rev):
    alr_blk = (2 * dv4 + 2 * dv4) // LANE
    return [pl.BlockSpec((t, dk4), lambda i: (rev(i), 2)),
            pl.BlockSpec((t, dk4), lambda i: (rev(i), 3)),
            pl.BlockSpec((t, dv4), lambda i: (rev(i), 2)),
            pl.BlockSpec((t, dv4), lambda i: (rev(i), 3)),
            pl.BlockSpec((t, LANE), lambda i: (rev(i), alr_blk))]


def _gla_fwd(proj, wa2p, b_a, head_norm):
    n, _, dv4, dk4, dk, dv = _gla_dims(proj, head_norm)
    t = _tile(n, GLA_ROWS, CHUNK)
    nc = t // CHUNK
    qscale = dk ** -0.5

    def body(q_ref, k_ref, v_ref, g_ref, alr_ref, wa_ref, ba_ref, hn_ref, y_ref, st_ref,
             state, cum_s, bend_s, o_s):
        i = pl.program_id(0)

        @pl.when(i == 0)
        def _():
            state[...] = jnp.zeros_like(state)

        _, la = _gate_logits(alr_ref, wa_ref, ba_ref)
        cum_s[...] = _chunk_cumsum(la)
        _chunk_ends(cum_s, bend_s, nc)
        kd = (k_ref[...] * jnp.exp(bend_s[...] - cum_s[...])).astype(BF16)
        qs = (q_ref[...] * qscale).astype(BF16)
        vv = v_ref[...].astype(BF16)
        units = [(c, h) for c in range(nc) for h in range(HEADS)]
        for c, h in units:
            rows = slice(c * CHUNK, (c + 1) * CHUNK)
            st_ref[c, h] = _dot(vv[rows, h * dv:(h + 1) * dv], kd[rows, h * dk:(h + 1) * dk], _TN)
        for c, h in units:
            e = jnp.exp(bend_s[c * CHUNK:c * CHUNK + 1, h * dk:(h + 1) * dk])
            s_new = state[h] * e + st_ref[c, h]
            state[h] = s_new
            st_ref[c, h] = s_new
        for c, h in units:
            rows = slice(c * CHUNK, (c + 1) * CHUNK)
            o_s[rows, h * dv:(h + 1) * dv] = _dot(qs[rows, h * dk:(h + 1) * dk], st_ref[c, h], _NT)
        for h in range(HEADS):
            hv = slice(h * dv, (h + 1) * dv)
            o = o_s[:, hv]
            gg = g_ref[:, hv]
            y_ref[:, hv] = (((o * _rms_scale(o)) * hn_ref[:, hv]) * (gg * jax.nn.sigmoid(gg))).astype(BF16)

    full = lambda shape: pl.BlockSpec(shape, lambda i: tuple(0 for _ in shape))
    return pl.pallas_call(
        body, name="gla_fwd", grid=(n // t,),
        in_specs=_gla_in_specs(t, dk4, dv4, lambda i: i)
        + [full((LANE, dk4)), full((1, dk4)), full((1, dv4))],
        out_specs=[pl.BlockSpec((t, dv4), lambda i: (i, 0)),
                   pl.BlockSpec((nc, HEADS, dv, dk), lambda i: (i, 0, 0, 0))],
        out_shape=[jax.ShapeDtypeStruct((n, dv4), BF16),
                   jax.ShapeDtypeStruct((n // CHUNK, HEADS, dv, dk), F32)],
        scratch_shapes=[pltpu.VMEM((HEADS, dv, dk), F32), pltpu.VMEM((t, dk4), F32),
                        pltpu.VMEM((t, dk4), F32), pltpu.VMEM((t, dv4), F32)],
        compiler_params=_cparams(("arbitrary",)),
    )(proj, proj, proj, proj, proj, wa2p, b_a, head_norm)


def _gla_bwd(proj, dymix, du, states, wa2p, b_a, head_norm):
    n, pw, dv4, dk4, dk, dv = _gla_dims(proj, head_norm)
    t = _tile(n, GLA_ROWS, CHUNK)
    nc = t // CHUNK
    nsteps = n // t
    qscale = dk ** -0.5
    rev = lambda i: nsteps - 1 - i

    def body(q_ref, k_ref, v_ref, g_ref, alr_ref, dy_ref, du_ref, st_ref, prev_ref, wa_ref, ba_ref, hn_ref,
             dp_ref, dwa_ref, dba_ref, dhn_ref, carry, cum_s, bend_s, gst_s, dkd_s, dee_s, o_s, do_s):
        i = pl.program_id(0)

        @pl.when(i == 0)
        def _():
            carry[...] = jnp.zeros_like(carry)
            dwa_ref[...] = jnp.zeros_like(dwa_ref)
            dba_ref[...] = jnp.zeros_like(dba_ref)
            dhn_ref[...] = jnp.zeros_like(dhn_ref)

        first_step = i == nsteps - 1
        z, la = _gate_logits(alr_ref, wa_ref, ba_ref)
        cum_s[...] = _chunk_cumsum(la)
        _chunk_ends(cum_s, bend_s, nc)
        dec = jnp.exp(bend_s[...] - cum_s[...])
        kd_f = k_ref[...] * dec
        kd = kd_f.astype(BF16)
        qs = (q_ref[...] * qscale).astype(BF16)
        vv = v_ref[...].astype(BF16)
        dp_ref[:, 0:dv4] = du_ref[...]
        units = [(c, h) for c in range(nc) for h in range(HEADS)]

        for c, h in units:
            rows = slice(c * CHUNK, (c + 1) * CHUNK)
            o_s[rows, h * dv:(h + 1) * dv] = _dot(qs[rows, h * dk:(h + 1) * dk], st_ref[c, h], _NT)
        for h in range(HEADS):
            hv = slice(h * dv, (h + 1) * dv)
            o = o_s[:, hv]
            r = _rms_scale(o)
            oh = o * r
            sl, dsl = _silu_parts(g_ref[:, hv])
            dyh = dy_ref[:, hv]
            hn = hn_ref[:, hv]
            tt = dyh * sl
            dhn_ref[:, hv] += jnp.sum(tt * oh, axis=0, keepdims=True)
            dp_ref[:, 3 * dv4 + h * dv:3 * dv4 + (h + 1) * dv] = (dyh * (oh * hn) * dsl).astype(BF16)
            tt = tt * hn
            do_s[:, hv] = (r * (tt - oh * jnp.mean(tt * oh, axis=-1, keepdims=True))).astype(BF16)
        for c, h in units:
            rows = slice(c * CHUNK, (c + 1) * CHUNK)
            hk = slice(h * dk, (h + 1) * dk)
            do = do_s[rows, h * dv:(h + 1) * dv]
            gst_s[c, h] = _dot(do, qs[rows, hk], _TN)
            dp_ref[rows, dv4 + h * dk:dv4 + (h + 1) * dk] = (_dot(do, st_ref[c, h]) * qscale).astype(BF16)

        for c, h in reversed(units):
            g_n = carry[h] + gst_s[c, h]
            gst_s[c, h] = g_n
            carry[h] = g_n * jnp.exp(bend_s[c * CHUNK:c * CHUNK + 1, h * dk:(h + 1) * dk])

        for c, h in units:
            rows = slice(c * CHUNK, (c + 1) * CHUNK)
            hk = slice(h * dk, (h + 1) * dk)
            hv = slice(h * dv, (h + 1) * dv)
            g_n = gst_s[c, h]
            if c > 0:
                s_prev = st_ref[c - 1, h]
            else:
                s_prev = jnp.where(first_step, 0.0, prev_ref[0, h])
            dkd_s[rows, hk] = _dot(vv[rows, hv], g_n)
            dp_ref[rows, 2 * dv4 + h * dv:2 * dv4 + (h + 1) * dv] = _dot(kd[rows, hk], g_n, _NT).astype(BF16)
            dee = jnp.sum(g_n * s_prev, axis=0, keepdims=True) * jnp.exp(bend_s[c * CHUNK:c * CHUNK + 1, hk])
            dee_s[rows, hk] = jnp.broadcast_to(dee, (CHUNK, dk))

        dkd = dkd_s[...]
        dp_ref[:, dv4 + dk4:dv4 + 2 * dk4] = (dkd * dec).astype(BF16)
        w = dkd * kd_f
        dla = (_chunk_cumsum(w) - w) + dee_s[...]
        dz = dla * (1.0 / GATE_TEMP) * (1.0 - jax.nn.sigmoid(z))
        dp_ref[:, 4 * dv4:4 * dv4 + LANE] = _dot(dz, wa_ref[...], _NT).astype(BF16)
        dwa_ref[...] += _dot(alr_ref[...], dz, _TN)
        dba_ref[...] += jnp.sum(dz, axis=0, keepdims=True)

    full = lambda shape: pl.BlockSpec(shape, lambda i: tuple(0 for _ in shape))
    return pl.pallas_call(
        body, name="gla_bwd", grid=(nsteps,),
        in_specs=_gla_in_specs(t, dk4, dv4, rev)
        + [pl.BlockSpec((t, dv4), lambda i: (rev(i), 1)),
           pl.BlockSpec((t, dv4), lambda i: (rev(i), 0)),
           pl.BlockSpec((nc, HEADS, dv, dk), lambda i: (rev(i), 0, 0, 0)),
           pl.BlockSpec((1, HEADS, dv, dk), lambda i: (jnp.maximum(rev(i) * nc - 1, 0), 0, 0, 0)),
           full((LANE, dk4)), full((1, dk4)), full((1, dv4))],
        out_specs=[pl.BlockSpec((t, pw), lambda i: (rev(i), 0)),
                   full((LANE, dk4)), full((1, dk4)), full((1, dv4))],
        out_shape=[jax.ShapeDtypeStruct((n, pw), BF16), jax.ShapeDtypeStruct((LANE, dk4), F32),
                   jax.ShapeDtypeStruct((1, dk4), F32), jax.ShapeDtypeStruct((1, dv4), F32)],
        scratch_shapes=[pltpu.VMEM((HEADS, dv, dk), F32), pltpu.VMEM((t, dk4), F32),
                        pltpu.VMEM((t, dk4), F32), pltpu.VMEM((nc, HEADS, dv, dk), F32),
                        pltpu.VMEM((t, dk4), F32), pltpu.VMEM((t, dk4), F32),
                        pltpu.VMEM((t, dv4), F32), pltpu.VMEM((t, dv4), BF16)],
        compiler_params=_cparams(("arbitrary",)),
    )(proj, proj, proj, proj, proj, dymix, du, states, states, wa2p, b_a, head_norm)


def _softmax(s):
    p = jnp.exp(s - jnp.max(s, axis=-1, keepdims=True))
    return p / jnp.sum(p, axis=-1, keepdims=True)


def _attn_fwd(q, kv):
    n, d = q.shape
    m = kv.shape[0]
    dh = d // HEADS
    tm = _tile(n, 512, 16)
    scale = dh ** -0.5

    def body(q_ref, k_ref, v_ref, o_ref, s_s, p_s):
        for h in range(HEADS):
            hs = slice(h * dh, (h + 1) * dh)
            s_s[:, h * m:(h + 1) * m] = _dot(q_ref[:, hs], k_ref[:, hs], _NT) * scale
        for h in range(HEADS):
            hm = slice(h * m, (h + 1) * m)
            p_s[:, hm] = _softmax(s_s[:, hm]).astype(BF16)
        for h in range(HEADS):
            hs = slice(h * dh, (h + 1) * dh)
            o_ref[:, hs] = _dot(p_s[:, h * m:(h + 1) * m], v_ref[:, hs]).astype(BF16)

    return pl.pallas_call(
        body, name="attn_fwd", grid=(n // tm,),
        in_specs=[pl.BlockSpec((tm, d), lambda i: (i, 0)), pl.BlockSpec((m, d), lambda i: (0, 0)),
                  pl.BlockSpec((m, d), lambda i: (0, 1))],
        out_specs=pl.BlockSpec((tm, d), lambda i: (i, 0)),
        out_shape=jax.ShapeDtypeStruct((n, d), BF16),
        scratch_shapes=[pltpu.VMEM((tm, HEADS * m), F32), pltpu.VMEM((tm, HEADS * m), BF16)],
        compiler_params=_cparams(("parallel",)),
    )(q, kv, kv)


def _attn_bwd(q, kv, do):
    n, d = q.shape
    m = kv.shape[0]
    dh = d // HEADS
    tm = _tile(n, 512, 16)
    scale = dh ** -0.5

    def body(q_ref, k_ref, v_ref, do_ref, dq_ref, dk_ref, dv_ref, s_s, dp_s, p_s, ds_s):
        i = pl.program_id(0)

        @pl.when(i == 0)
        def _():
            dk_ref[...] = jnp.zeros_like(dk_ref)
            dv_ref[...] = jnp.zeros_like(dv_ref)

        for h in range(HEADS):
            hs, hm = slice(h * dh, (h + 1) * dh), slice(h * m, (h + 1) * m)
            s_s[:, hm] = _dot(q_ref[:, hs], k_ref[:, hs], _NT) * scale
            dp_s[:, hm] = _dot(do_ref[:, hs], v_ref[:, hs], _NT)
        for h in range(HEADS):
            hm = slice(h * m, (h + 1) * m)
            p = _softmax(s_s[:, hm])
            dp = dp_s[:, hm]
            p_s[:, hm] = p.astype(BF16)
            ds_s[:, hm] = (p * (dp - jnp.sum(dp * p, axis=-1, keepdims=True)) * scale).astype(BF16)
        for h in range(HEADS):
            hs, hm = slice(h * dh, (h + 1) * dh), slice(h * m, (h + 1) * m)
            dv_ref[:, hs] += _dot(p_s[:, hm], do_ref[:, hs], _TN)
            dq_ref[:, hs] = _dot(ds_s[:, hm], k_ref[:, hs]).astype(BF16)
            dk_ref[:, hs] += _dot(ds_s[:, hm], q_ref[:, hs], _TN)

    row = pl.BlockSpec((tm, d), lambda i: (i, 0))
    memb = pl.BlockSpec((m, d), lambda i: (0, 0))
    return pl.pallas_call(
        body, name="attn_bwd", grid=(n // tm,),
        in_specs=[row, memb, pl.BlockSpec((m, d), lambda i: (0, 1)), row],
        out_specs=[row, memb, memb],
        out_shape=[jax.ShapeDtypeStruct((n, d), BF16), jax.ShapeDtypeStruct((m, d), F32),
                   jax.ShapeDtypeStruct((m, d), F32)],
        scratch_shapes=[pltpu.VMEM((tm, HEADS * m), F32), pltpu.VMEM((tm, HEADS * m), F32),
                        pltpu.VMEM((tm, HEADS * m), BF16), pltpu.VMEM((tm, HEADS * m), BF16)],
        compiler_params=_cparams(("arbitrary",)),
    )(q, kv, kv, do)


def _adamw(name, w, m, v, parts, own_block):
    r, c = w.shape
    tr = _tile(r, max(16, ELEMENTWISE_BLOCK // c), 16)
    row = pl.BlockSpec((tr, c), lambda i, o: (i, 0))
    ops, specs = [w, m, v], [row, row, row]
    for p in parts:
        if p.ndim == 2:
            ops.append(p)
            specs.append(row)
        elif p.shape[0] == 4:
            ops.append(p)
            specs.append(pl.BlockSpec((None, tr, c), lambda i, o: (o[0], i, 0)))
        else:
            for s in range(p.shape[0]):
                ops.append(p)
                specs.append(pl.BlockSpec((None, tr, c), lambda i, o, s=s: (s, i, 0)))
    n_parts = len(ops) - 3
    c1 = 1.0 - ADAM_B1 ** ADAM_STEP
    c2 = 1.0 - ADAM_B2 ** ADAM_STEP

    def body(o_ref, *refs):
        del o_ref
        w_ref, m_ref, v_ref = refs[:3]
        g_refs = refs[3:3 + n_parts]
        go_ref, d_ref, mo_ref, vo_ref = refs[3 + n_parts:]
        g = g_refs[0][...].astype(F32)
        for g_ref in g_refs[1:]:
            g = g + g_ref[...].astype(F32)
        m_new = ADAM_B1 * m_ref[...] + (1.0 - ADAM_B1) * g
        v_new = ADAM_B2 * v_ref[...] + (1.0 - ADAM_B2) * (g * g)
        m_hat = m_new / c1
        v_hat = v_new / c2
        go_ref[...] = g
        d_ref[...] = -ADAM_LR * (m_hat / (jnp.sqrt(v_hat) + ADAM_EPS) + ADAM_WD * w_ref[...])
        mo_ref[...] = m_new
        vo_ref[...] = v_new

    shp = jax.ShapeDtypeStruct((r, c), F32)
    return pl.pallas_call(
        body, name=name,
        grid_spec=pltpu.PrefetchScalarGridSpec(
            num_scalar_prefetch=1, grid=(r // tr,), in_specs=specs, out_specs=[row] * 4),
        out_shape=[shp] * 4,
        compiler_params=_cparams(("parallel",)),
    )(own_block, *ops)


def _pair_add(name, g, recvd, core):
    _, r, c = g.shape
    tr = _tile(r, max(16, ELEMENTWISE_BLOCK // c), 16)

    def body(core_ref, a_ref, b_ref, o_ref):
        del core_ref
        o_ref[...] = (a_ref[...].astype(F32) + b_ref[...].astype(F32)).astype(o_ref.dtype)

    blk = pl.BlockSpec((None, tr, c), lambda s, i, core_ref: (s, i, 0))
    mine = pl.BlockSpec((None, tr, c), lambda s, i, core_ref: (2 * s + core_ref[0], i, 0))
    return pl.pallas_call(
        body, name=name,
        grid_spec=pltpu.PrefetchScalarGridSpec(
            num_scalar_prefetch=1, grid=(4, r // tr), in_specs=[mine, blk], out_specs=blk),
        out_shape=jax.ShapeDtypeStruct(recvd.shape, g.dtype),
        compiler_params=_cparams(("parallel", "parallel")),
    )(core, g, recvd)


def _position():
    return lax.axis_index("x"), lax.axis_index("y"), lax.axis_index("c")


_HBM = pl.BlockSpec(memory_space=pltpu.HBM)

_Exchange = collections.namedtuple("_Exchange", "ins out_shape sems start finish in_place", defaults=(False,))
_NO_EXCHANGE = _Exchange((), (), (), lambda ins, outs, sems: None, lambda ins, outs, sems: None)


def _placed(shard):
    x, y, c = _position()
    start = (4 * x + 2 * y + c,) + (0,) * shard.ndim
    return lax.dynamic_update_slice(lax.empty((N_DEV,) + shard.shape, shard.dtype), shard[None], start)


def _all_gather(name, placed):
    n = len(placed)

    def body(*refs):
        ins, outs = refs[:n], refs[n:2 * n]
        send_sems, recv_sems = refs[2 * n:]
        x, y, c = _position()
        me, sibling = (x, y, c), (x, y, 1 - c)
        chips = [(1 - x, y), (x, 1 - y), (1 - x, 1 - y)]

        def copy(a, k, owner, to):
            blk = 4 * owner[0] + 2 * owner[1] + owner[2]
            return pltpu.make_async_remote_copy(
                src_ref=ins[a].at[blk], dst_ref=outs[a].at[blk],
                send_sem=send_sems.at[7 * a + k], recv_sem=recv_sems.at[7 * a + k],
                device_id=to, device_id_type=MESH)

        sends = []
        for a in range(n):
            sends.append(copy(a, 0, me, sibling))
            sends += [copy(a, 1 + j, me, (*chip, c)) for j, chip in enumerate(chips)]
        for cp in sends:
            cp.start()
        for j, chip in enumerate(chips):
            for a in range(n):
                copy(a, 1 + j, (*chip, c), me).wait_recv()
                fwd = copy(a, 4 + j, (*chip, c), sibling)
                fwd.start()
                sends.append(fwd)
        for a in range(n):
            copy(a, 0, sibling, me).wait_recv()
            for j, chip in enumerate(chips):
                copy(a, 4 + j, (*chip, 1 - c), me).wait_recv()
        for cp in sends:
            cp.wait_send()

    return pl.pallas_call(
        body, name=name,
        in_specs=[_HBM] * n, out_specs=[_HBM] * n,
        out_shape=[jax.ShapeDtypeStruct(p.shape, p.dtype) for p in placed],
        input_output_aliases={a: a for a in range(n)},
        scratch_shapes=[pltpu.SemaphoreType.DMA((7 * n,)), pltpu.SemaphoreType.DMA((7 * n,))],
    )(*placed)


def _pair_exchange(grads):
    n = len(grads)

    def start(ins, recvd, sems):
        send_sems, recv_sems = sems
        x, y, c = _position()
        for a in range(n):
            for chip in range(4):
                pltpu.make_async_remote_copy(
                    src_ref=ins[a].at[2 * chip + (1 - c)], dst_ref=recvd[a].at[chip],
                    send_sem=send_sems.at[a], recv_sem=recv_sems.at[a],
                    device_id=(x, y, 1 - c), device_id_type=MESH).start()

    def finish(ins, recvd, sems):
        del ins
        send_sems, recv_sems = sems
        x, y, c = _position()
        for a in range(n):
            pltpu.make_async_remote_copy(
                src_ref=recvd[a], dst_ref=recvd[a], send_sem=send_sems.at[a], recv_sem=recv_sems.at[a],
                device_id=(x, y, 1 - c), device_id_type=MESH).wait()

    return _Exchange(
        ins=tuple(grads),
        out_shape=tuple(jax.ShapeDtypeStruct((4,) + g.shape[1:], g.dtype) for g in grads),
        sems=(pltpu.SemaphoreType.DMA((n,)), pltpu.SemaphoreType.DMA((n,))),
        start=start, finish=finish)


def _both(first, second):
    assert first.in_place and second.in_place
    ni, no, ns = len(first.ins), len(first.out_shape), len(first.sems)

    def start(ins, outs, sems):
        first.start(ins[:ni], outs[:no], sems[:ns])
        second.start(ins[ni:], outs[no:], sems[ns:])

    def finish(ins, outs, sems):
        first.finish(ins[:ni], outs[:no], sems[:ns])
        second.finish(ins[ni:], outs[no:], sems[ns:])

    return _Exchange(first.ins + second.ins, first.out_shape + second.out_shape, first.sems + second.sems,
                     start, finish, in_place=True)


def _run_exchange(name, exchange):
    n_in, n_out = len(exchange.ins), len(exchange.out_shape)

    def body(*refs):
        ins, outs, sems = refs[:n_in], refs[n_in:n_in + n_out], refs[n_in + n_out:]
        exchange.start(ins, outs, sems)
        exchange.finish(ins, outs, sems)

    return pl.pallas_call(
        body, name=name, in_specs=[_HBM] * n_in, out_specs=[_HBM] * n_out,
        out_shape=list(exchange.out_shape), scratch_shapes=list(exchange.sems),
        input_output_aliases={k: k for k in range(n_in)} if exchange.in_place else {},
    )(*exchange.ins)


def _chip_scatter_exchange(sums):
    n = len(sums)
    offsets = [(1, 0), (0, 1), (1, 1)]

    def start(ins, recvd, sems):
        send_sems, recv_sems = sems
        x, y, c = _position()
        for a in range(n):
            for r, (ox, oy) in enumerate(offsets):
                px = 1 - x if ox else x
                py = 1 - y if oy else y
                pltpu.make_async_remote_copy(
                    src_ref=ins[a].at[2 * px + py], dst_ref=recvd[a].at[r],
                    send_sem=send_sems.at[a], recv_sem=recv_sems.at[a],
                    device_id=(px, py, c), device_id_type=MESH).start()

    def finish(ins, recvd, sems):
        del ins
        send_sems, recv_sems = sems
        x, y, c = _position()
        for a in range(n):
            pltpu.make_async_remote_copy(
                src_ref=recvd[a], dst_ref=recvd[a], send_sem=send_sems.at[a], recv_sem=recv_sems.at[a],
                device_id=(x, y, c), device_id_type=MESH).wait()

    return _Exchange(
        ins=tuple(sums),
        out_shape=tuple(jax.ShapeDtypeStruct((3,) + s.shape[1:], s.dtype) for s in sums),
        sems=(pltpu.SemaphoreType.DMA((n,)), pltpu.SemaphoreType.DMA((n,))),
        start=start, finish=finish)


def _gather_spread_exchange(placed):
    n = len(placed)

    def peers():
        x, y, c = _position()
        return (x, y, c), [(x, y, 1 - c), (1 - x, y, c), (x, 1 - y, c), (1 - x, 1 - y, c)]

    def copy(ins, outs, sems, a, k, owner, to):
        blk = 4 * owner[0] + 2 * owner[1] + owner[2]
        return pltpu.make_async_remote_copy(
            src_ref=ins[a].at[blk], dst_ref=outs[a].at[blk],
            send_sem=sems[0].at[4 * a + k], recv_sem=sems[1].at[4 * a + k],
            device_id=to, device_id_type=MESH)

    def start(ins, outs, sems):
        me, others = peers()
        for a in range(n):
            for k, to in enumerate(others):
                copy(ins, outs, sems, a, k, me, to).start()

    def finish(ins, outs, sems):
        _, others = peers()
        for a in range(n):
            for k, peer in enumerate(others):
                cp = copy(ins, outs, sems, a, k, peer, peer)
                cp.wait_recv()
                cp.wait_send()

    return _Exchange(
        ins=tuple(placed), out_shape=tuple(jax.ShapeDtypeStruct(p.shape, p.dtype) for p in placed),
        sems=(pltpu.SemaphoreType.DMA((4 * n,)), pltpu.SemaphoreType.DMA((4 * n,))),
        start=start, finish=finish, in_place=True)


def _gather_forward_exchange(partial):
    n = len(partial)

    def copy(ins, outs, sems, a, j, pc):
        x, y, c = _position()
        chips = [(1 - x, y), (x, 1 - y), (1 - x, 1 - y)]
        blk = 4 * chips[j][0] + 2 * chips[j][1] + pc
        return pltpu.make_async_remote_copy(
            src_ref=ins[a].at[blk], dst_ref=outs[a].at[blk],
            send_sem=sems[0].at[3 * a + j], recv_sem=sems[1].at[3 * a + j],
            device_id=(x, y, 1 - c), device_id_type=MESH)

    def start(ins, outs, sems):
        c = lax.axis_index("c")
        for a in range(n):
            for j in range(3):
                copy(ins, outs, sems, a, j, c).start()

    def finish(ins, outs, sems):
        c = lax.axis_index("c")
        for a in range(n):
            for j in range(3):
                copy(ins, outs, sems, a, j, 1 - c).wait_recv()
                copy(ins, outs, sems, a, j, c).wait_send()

    return _Exchange(
        ins=tuple(partial), out_shape=tuple(jax.ShapeDtypeStruct(p.shape, p.dtype) for p in partial),
        sems=(pltpu.SemaphoreType.DMA((3 * n,)), pltpu.SemaphoreType.DMA((3 * n,))),
        start=start, finish=finish, in_place=True)


def _all_reduce_small(vec):
    r = vec.shape[0]

    def body(v_ref, o_ref, gbuf, send_sems, recv_sems):
        x, y, c = _position()
        me = 4 * x + 2 * y + c
        gbuf[me] = v_ref[...]
        copies = []
        for k in range(1, N_DEV):
            ox, oy, oc = (k >> 2) & 1, (k >> 1) & 1, k & 1
            peer = (1 - x if ox else x, 1 - y if oy else y, 1 - c if oc else c)
            cp = pltpu.make_async_remote_copy(
                src_ref=gbuf.at[me], dst_ref=gbuf.at[me], send_sem=send_sems.at[k - 1],
                recv_sem=recv_sems.at[k - 1], device_id=peer, device_id_type=MESH)
            cp.start()
            copies.append(cp)
        for cp in copies:
            cp.wait()
        total = gbuf[0]
        for j in range(1, N_DEV):
            total = total + gbuf[j]
        o_ref[...] = total

    return pl.pallas_call(
        body, name="all_reduce_small",
        in_specs=[pl.BlockSpec(memory_space=pltpu.VMEM)],
        out_specs=pl.BlockSpec(memory_space=pltpu.VMEM),
        out_shape=jax.ShapeDtypeStruct(vec.shape, F32),
        scratch_shapes=[pltpu.VMEM((N_DEV, r, LANE), F32), pltpu.SemaphoreType.DMA((N_DEV - 1,)),
                        pltpu.SemaphoreType.DMA((N_DEV - 1,))],
    )(vec)


def _cols_from_blocks(g):
    nb, r, cs = g.shape
    return jnp.transpose(g, (1, 0, 2)).reshape(r, nb * cs)


def _cols_to_blocks(w):
    r, cfull = w.shape
    return jnp.transpose(w.reshape(r, N_DEV, cfull // N_DEV), (1, 0, 2))


def kernel(x, mem, ffn1_norm, ffn1_w_gate, ffn1_w_up, ffn1_w_down, mix_norm, w_in, pool_w, pool_scale, gla_w_a2, gla_b_a, gla_head_norm, w_out, xattn_norm, mem_norm, xattn_w_q, xattn_w_kv, xattn_w_o, ffn2_norm, ffn2_w_gate, ffn2_w_up, ffn2_w_down, final_norm, loss_target, m_ffn1_norm, m_ffn1_w_gate, m_ffn1_w_up, m_ffn1_w_down, m_mix_norm, m_w_in, m_pool_w, m_pool_scale, m_gla_w_a2, m_gla_b_a, m_gla_head_norm, m_w_out, m_xattn_norm, m_mem_norm, m_xattn_w_q, m_xattn_w_kv, m_xattn_w_o, m_ffn2_norm, m_ffn2_w_gate, m_ffn2_w_up, m_ffn2_w_down, m_final_norm, v_ffn1_norm, v_ffn1_w_gate, v_ffn1_w_up, v_ffn1_w_down, v_mix_norm, v_w_in, v_pool_w, v_pool_scale, v_gla_w_a2, v_gla_b_a, v_gla_head_norm, v_w_out, v_xattn_norm, v_mem_norm, v_xattn_w_q, v_xattn_w_kv, v_xattn_w_o, v_ffn2_norm, v_ffn2_w_gate, v_ffn2_w_up, v_ffn2_w_down, v_final_norm):
    weights = dict(ffn1_norm=ffn1_norm, ffn1_w_gate=ffn1_w_gate, ffn1_w_up=ffn1_w_up, ffn1_w_down=ffn1_w_down, mix_norm=mix_norm, w_in=w_in, pool_w=pool_w, pool_scale=pool_scale, gla_w_a2=gla_w_a2, gla_b_a=gla_b_a, gla_head_norm=gla_head_norm, w_out=w_out, xattn_norm=xattn_norm, mem_norm=mem_norm, xattn_w_q=xattn_w_q, xattn_w_kv=xattn_w_kv, xattn_w_o=xattn_w_o, ffn2_norm=ffn2_norm, ffn2_w_gate=ffn2_w_gate, ffn2_w_up=ffn2_w_up, ffn2_w_down=ffn2_w_down, final_norm=final_norm)
    mom1 = dict(ffn1_norm=m_ffn1_norm, ffn1_w_gate=m_ffn1_w_gate, ffn1_w_up=m_ffn1_w_up, ffn1_w_down=m_ffn1_w_down, mix_norm=m_mix_norm, w_in=m_w_in, pool_w=m_pool_w, pool_scale=m_pool_scale, gla_w_a2=m_gla_w_a2, gla_b_a=m_gla_b_a, gla_head_norm=m_gla_head_norm, w_out=m_w_out, xattn_norm=m_xattn_norm, mem_norm=m_mem_norm, xattn_w_q=m_xattn_w_q, xattn_w_kv=m_xattn_w_kv, xattn_w_o=m_xattn_w_o, ffn2_norm=m_ffn2_norm, ffn2_w_gate=m_ffn2_w_gate, ffn2_w_up=m_ffn2_w_up, ffn2_w_down=m_ffn2_w_down, final_norm=m_final_norm)
    mom2 = dict(ffn1_norm=v_ffn1_norm, ffn1_w_gate=v_ffn1_w_gate, ffn1_w_up=v_ffn1_w_up, ffn1_w_down=v_ffn1_w_down, mix_norm=v_mix_norm, w_in=v_w_in, pool_w=v_pool_w, pool_scale=v_pool_scale, gla_w_a2=v_gla_w_a2, gla_b_a=v_gla_b_a, gla_head_norm=v_gla_head_norm, w_out=v_w_out, xattn_norm=v_xattn_norm, mem_norm=v_mem_norm, xattn_w_q=v_xattn_w_q, xattn_w_kv=v_xattn_w_kv, xattn_w_o=v_xattn_w_o, ffn2_norm=v_ffn2_norm, ffn2_w_gate=v_ffn2_w_gate, ffn2_w_up=v_ffn2_w_up, ffn2_w_down=v_ffn2_w_down, final_norm=v_final_norm)
    order = list(weights.keys())

    n, d = x.shape[1], x.shape[2]
    mlen = mem.shape[1]
    x0 = x.reshape(n, d)
    memf = mem.reshape(mlen, d)
    target = loss_target.reshape(n, d)
    dpool = d // 2
    in_cols = w_in.shape[2] * N_DEV
    proj_cols = 2 * d + LANE
    rank = gla_w_a2.shape[1]

    def shard(name):
        return _placed(weights[name][0].astype(BF16))

    wg1, wu1, wd1 = _all_gather("ag_ffn1", [shard("ffn1_w_gate"), shard("ffn1_w_up"), shard("ffn1_w_down")])
    with_ffn1 = ["w_in", "pool_w", "gla_w_a2", "w_out", "ffn2_w_gate", "ffn2_w_up", "ffn2_w_down"]
    with_mix_in = ["xattn_w_q", "xattn_w_kv", "xattn_w_o"]
    h1 = _rms_fwd("rms_ffn1", x0, ffn1_norm)
    (x1, a1, b1, h2), spread = _ffn_fwd(
        "ffn1_fwd", x0, h1, wg1, wu1, wd1, mix_norm, _gather_spread_exchange([shard(k) for k in with_ffn1]))
    win_g, = _run_exchange("ag_forward_w_in", _gather_forward_exchange(spread[:1]))
    winp = jnp.pad(_cols_from_blocks(win_g), ((0, 0), (0, proj_cols - in_cols)))
    proj, carried = _mm_nn(
        "mix_in", [(h2, winp)], F32, tn=1408,
        exchange=_both(_gather_forward_exchange(spread[1:]),
                       _gather_spread_exchange([shard(k) for k in with_mix_in])))
    pw_g, wa2_g, wout_g, wg2, wu2, wd2 = carried[:6]

    pw = jnp.transpose(pw_g, (1, 0, 2, 3)).reshape(len(POOL_WINDOWS), dpool // 4, dpool // 4)
    wa2p = jnp.pad(_cols_from_blocks(wa2_g), ((0, LANE - rank), (0, 0)))
    wout = wout_g.reshape(d, d)
    fnorm = final_norm.reshape(1, d)

    ypool = _pool_fwd(proj, pw, pool_scale)
    ygla, states = _gla_fwd(proj, wa2p, gla_b_a, gla_head_norm)
    (x2, h3), (wq_g, wkv_g, wo_g) = _mm_nn(
        "mix_out", [(ypool, wout[:dpool]), (ygla, wout[dpool:])], F32, res=x1, norm_gain=xattn_norm,
        exchange=_gather_forward_exchange(carried[6:]), tk=1024)
    wq = wq_g.reshape(d, d)
    wkv = _cols_from_blocks(wkv_g)
    wo = wo_g.reshape(d, d)
    mh = _rms_fwd("rms_mem", memf, mem_norm)
    q = _mm_nn("xattn_q", [(h3, wq)], BF16)
    kv = _mm_nn("xattn_kv", [(mh, wkv)], BF16)
    att = _attn_fwd(q, kv)
    x3, h4 = _mm_nn("xattn_o", [(att, wo)], F32, res=x2, norm_gain=ffn2_norm)
    (x4, a2, b2), _ = _ffn_fwd("ffn2_fwd", x3, h4, wg2, wu2, wd2, None, _NO_EXCHANGE)
    dx4, dx4b, g_final, loss_part = _loss_head(x4, target, fnorm)

    grads = {}
    core = lax.axis_index("c").astype(jnp.int32).reshape(1)
    chip = (2 * lax.axis_index("x") + lax.axis_index("y")).astype(jnp.int32).reshape(1)
    chip_sums = {}

    def pair_blocks(names):
        return [grads[k].reshape(N_DEV, -1, grads[k].shape[-1]) for k in names]

    def pair_sums(names, blocks, recvd):
        for k, g, r in zip(names, blocks, recvd):
            chip_sums[k] = _pair_add("rs_add_" + k, g, r, core)

    ffn2_names = ["ffn2_w_gate", "ffn2_w_up", "ffn2_w_down"]
    (da2, db2, act2), _ = _ffn_bwd_act("ffn2_bwd_act", dx4b, wd2, a2, b2, _NO_EXCHANGE)
    (grads["ffn2_w_gate"], grads["ffn2_w_up"], grads["ffn2_w_down"]), _ = _ffn_bwd_weights(
        "ffn2_bwd_weights", h4, dx4b, da2, db2, act2, _NO_EXCHANGE)
    blocks = pair_blocks(ffn2_names)
    dh4, _ = _ffn_bwd_dh("ffn2_bwd_dh", da2, db2, wg2, wu2, _NO_EXCHANGE)
    (dx3, dx3b, g_ffn2_norm), recvd = _rms_bwd(
        "rms_ffn2_bwd", x3, ffn2_norm, dh4, dx4, _pair_exchange(blocks))
    pair_sums(ffn2_names, blocks, recvd)

    xattn_names = ["xattn_w_q", "xattn_w_kv", "xattn_w_o"]
    datt = _mm_nt("xattn_do", [(dx3b, wo)], BF16)
    grads["xattn_w_o"] = _mm_tn("xattn_dwo", [(att, dx3b)], BF16).reshape(N_DEV, d // N_DEV, d)
    dq, dk, dv = _attn_bwd(q, kv, datt)
    grads["xattn_w_q"] = _mm_tn("xattn_dwq", [(h3, dq)], BF16).reshape(N_DEV, d // N_DEV, d)
    dkv = jnp.concatenate([dk, dv], axis=1)
    dmh = _mm_nt("xattn_dmh", [(dkv, wkv)], F32)
    grads["xattn_w_kv"] = _cols_to_blocks(_mm_tn("xattn_dwkv", [(mh, dkv)], BF16))
    g_mem_norm = _rms_gain_grad("rms_mem_bwd", memf, mem_norm, dmh)
    blocks = pair_blocks(xattn_names)
    (dx2, dx2b, g_xattn_norm), recvd = _mm_nt_rms_bwd(
        "xattn_dh", dq, wq, x2, xattn_norm, dx3, _pair_exchange(blocks), tk=2048)
    pair_sums(xattn_names, blocks, recvd)

    dymix = _mm_nt("mix_dy", [(dx2b, wout)], F32)
    grads["w_out"] = jnp.concatenate(
        [_mm_tn("mix_dwout_pool", [(ypool, dx2b)], BF16), _mm_tn("mix_dwout_gla", [(ygla, dx2b)], BF16)],
        axis=0).reshape(N_DEV, d // N_DEV, d)
    du, g_pool_w, g_pool_scale = _pool_bwd(proj, dymix, pw, pool_scale)
    dproj, g_wa2p, g_b_a, g_head_norm = _gla_bwd(proj, dymix, du, states, wa2p, gla_b_a, gla_head_norm)
    dwin, recvd = _mm_tn("mix_dwin", [(h2, dproj)], BF16,
                         exchange=_chip_scatter_exchange([chip_sums[ffn2_names[0]]]), tn=1408)
    chip_recvd = {ffn2_names[0]: recvd[0]}
    grads["w_in"] = _cols_to_blocks(dwin[:, :in_cols])
    grads["pool_w"] = jnp.transpose(
        g_pool_w.reshape(len(POOL_WINDOWS), N_DEV, dpool // 4 // N_DEV, dpool // 4), (1, 0, 2, 3))
    grads["gla_w_a2"] = _cols_to_blocks(g_wa2p[:rank])
    mix_names = ["w_in", "pool_w", "gla_w_a2", "w_out"]
    blocks = pair_blocks(mix_names)
    (dx1, dx1b, g_mix_norm), recvd = _mm_nt_rms_bwd(
        "mix_dh", dproj, winp, x1, mix_norm, dx2, _pair_exchange(blocks), tk=1408)
    pair_sums(mix_names, blocks, recvd)

    ffn1_names = ["ffn1_w_gate", "ffn1_w_up", "ffn1_w_down"]
    (da1, db1, act1), recvd = _ffn_bwd_act(
        "ffn1_bwd_act", dx1b, wd1, a1, b1, _chip_scatter_exchange([chip_sums[k] for k in ffn2_names[1:]]))
    chip_recvd.update(zip(ffn2_names[1:], recvd))
    (grads["ffn1_w_gate"], grads["ffn1_w_up"], grads["ffn1_w_down"]), recvd = _ffn_bwd_weights(
        "ffn1_bwd_weights", h1, dx1b, da1, db1, act1,
        _chip_scatter_exchange([chip_sums[k] for k in xattn_names + mix_names]))
    chip_recvd.update(zip(xattn_names + mix_names, recvd))
    blocks = pair_blocks(ffn1_names)
    pair_sums(ffn1_names, blocks, _run_exchange("rs_pair_ffn1", _pair_exchange(blocks)))
    dh1, recvd = _ffn_bwd_dh(
        "ffn1_bwd_dh", da1, db1, wg1, wu1, _chip_scatter_exchange([chip_sums[k] for k in ffn1_names]))
    chip_recvd.update(zip(ffn1_names, recvd))
    (dx0, _, g_ffn1_norm), _ = _rms_bwd("rms_ffn1_bwd", x0, ffn1_norm, dh1, dx1, _NO_EXCHANGE)

    small = [("ffn1_norm", g_ffn1_norm), ("mix_norm", g_mix_norm), ("pool_scale", g_pool_scale),
             ("gla_b_a", g_b_a), ("gla_head_norm", g_head_norm), ("xattn_norm", g_xattn_norm),
             ("mem_norm", g_mem_norm), ("ffn2_norm", g_ffn2_norm), ("final_norm", g_final)]
    packed = jnp.concatenate([g.reshape(-1) for _, g in small] + [loss_part.reshape(-1)])
    slab = 8 * LANE
    padded = -(-packed.shape[0] // slab) * slab
    packed = jnp.pad(packed, (0, padded - packed.shape[0])).reshape(padded // LANE, LANE)
    reduced = _all_reduce_small(packed).reshape(-1)
    small_grads = {}
    off = 0
    for name, g in small:
        small_grads[name] = reduced[off:off + g.size]
        off += g.size
    loss = reduced[off]

    out_g, out_d, out_m, out_v = {}, {}, {}, {}
    for k in order:
        w = weights[k]
        if k in small_grads:
            w2 = w.reshape(1, -1)
            parts = [small_grads[k].reshape(1, -1)]
            own_block = jnp.zeros((1,), jnp.int32)
        else:
            w2 = w.reshape(-1, w.shape[-1])
            parts = [chip_sums[k], chip_recvd[k]]
            own_block = chip
        res = _adamw("adamw_" + k, w2, mom1[k].reshape(w2.shape), mom2[k].reshape(w2.shape), parts, own_block)
        out_g[k], out_d[k], out_m[k], out_v[k] = [r.reshape(w.shape) for r in res]

    return (loss, dx0.reshape(x.shape), *[out_g[k] for k in order], *[out_d[k] for k in order],
            *[out_m[k] for k in order], *[out_v[k] for k in order])
```

```python
import collections

import jax
import jax.numpy as jnp
from jax import lax
from jax.experimental import pallas as pl
from jax.experimental.pallas import tpu as pltpu

F32 = jnp.float32
BF16 = jnp.bfloat16
MESH = pl.DeviceIdType.MESH

N_DEV = 8
CHUNK = 64
POOL_WINDOWS = (2, 4, 8, 16)
POOL_HALO = 16
HEADS = 4
GATE_TEMP = 16.0
RMS_EPS = 1e-6
LANE = 128
V7X_VMEM_BYTES = 64 * 1024 * 1024
VMEM_LIMIT = V7X_VMEM_BYTES - 8 * 1024 * 1024
GLA_ROWS = 8 * CHUNK
ELEMENTWISE_BLOCK = 512 * 1024
FFN_SUBTILES = 2
PROJ_SUBTILE_ROWS = 256

ADAM_LR = 0.001
ADAM_B1 = 0.9
ADAM_B2 = 0.999
ADAM_EPS = 1e-08
ADAM_WD = 0.01
ADAM_STEP = 10

_NN = (((1,), (0,)), ((), ()))
_NT = (((1,), (1,)), ((), ()))
_TN = (((0,), (0,)), ((), ()))


def _cparams(sem=None):
    return pltpu.CompilerParams(dimension_semantics=sem, vmem_limit_bytes=VMEM_LIMIT)


def _tile(n, pref, align):
    t = (min(pref, n) // align) * align
    while t >= align:
        if n % t == 0:
            return t
        t -= align
    return n


def _dot(a, b, dims=_NN):
    return lax.dot_general(a.astype(BF16), b.astype(BF16), dims, preferred_element_type=F32)


def _silu_parts(z):
    sig = jax.nn.sigmoid(z)
    return z * sig, sig * (1.0 + z * (1.0 - sig))


def _rms_scale(xf):
    return lax.rsqrt(jnp.mean(xf * xf, axis=-1, keepdims=True) + RMS_EPS)


def _matmul(name, pairs, a_spec, b_spec, out_shape, out_spec, grid, acc_shape, dims,
            res=None, res_spec=None, scale=None, norm_gain=None, exchange=None):
    n = len(pairs)
    nk = grid[2]

    def body(ins, outs, scr):
        a_refs, b_refs = ins[:n], ins[n:2 * n]
        pos = 2 * n
        res_ref = gain_ref = h_ref = None
        if res is not None:
            res_ref = ins[pos]
            pos += 1
        if norm_gain is not None:
            gain_ref = ins[pos]
            h_ref = outs[1]
        o_ref = outs[0]
        acc, = scr
        k = pl.program_id(2)

        @pl.when(k == 0)
        def _():
            acc[...] = jnp.zeros_like(acc)

        part = None
        for a_ref, b_ref in zip(a_refs, b_refs):
            d = _dot(a_ref[...], b_ref[...], dims)
            part = d if part is None else part + d
        acc[...] += part

        @pl.when(k == nk - 1)
        def _():
            r = acc[...]
            if scale is not None:
                r = r * scale
            if res_ref is not None:
                r = r + res_ref[...]
            o_ref[...] = r.astype(o_ref.dtype)
            if h_ref is not None:
                h_ref[...] = ((r * _rms_scale(r)) * gain_ref[...]).astype(BF16)

    ops = [p[0] for p in pairs] + [p[1] for p in pairs]
    specs = [a_spec] * n + [b_spec] * n
    if res is not None:
        ops.append(res)
        specs.append(res_spec)
    out_specs, out_shapes = [out_spec], [out_shape]
    if norm_gain is not None:
        ops.append(norm_gain)
        specs.append(pl.BlockSpec(norm_gain.shape, lambda i, j, k: (0, 0)))
        out_specs.append(out_spec)
        out_shapes.append(jax.ShapeDtypeStruct(out_shape.shape, BF16))
    outs, carried = _hosted_call(
        name, body, grid, specs, out_specs, out_shapes, [pltpu.VMEM(acc_shape, F32)], ops,
        _NO_EXCHANGE if exchange is None else exchange)
    result = outs[0] if norm_gain is None else tuple(outs)
    return result if exchange is None else (result, carried)


def _mm_nn(name, pairs, out_dtype, res=None, norm_gain=None, exchange=None, tm=1024, tn=1024, tk=2048):
    m, kd = pairs[0][0].shape
    nd = pairs[0][1].shape[1]
    if norm_gain is not None:
        tm, tn = 512, nd
    tm, tn, tk = _tile(m, tm, 16), _tile(nd, tn, LANE), _tile(kd, tk, LANE)
    return _matmul(
        name, pairs,
        pl.BlockSpec((tm, tk), lambda i, j, k: (i, k)),
        pl.BlockSpec((tk, tn), lambda i, j, k: (k, j)),
        jax.ShapeDtypeStruct((m, nd), out_dtype),
        pl.BlockSpec((tm, tn), lambda i, j, k: (i, j)),
        (m // tm, nd // tn, kd // tk), (tm, tn), _NN,
        res=res, res_spec=pl.BlockSpec((tm, tn), lambda i, j, k: (i, j)), norm_gain=norm_gain,
        exchange=exchange)


def _mm_nt(name, pairs, out_dtype, tm=1024, tn=1024, tk=2048):
    m, kd = pairs[0][0].shape
    nd = pairs[0][1].shape[0]
    tm, tn, tk = _tile(m, tm, 16), _tile(nd, tn, LANE), _tile(kd, tk, LANE)
    return _matmul(
        name, pairs,
        pl.BlockSpec((tm, tk), lambda i, j, k: (i, k)),
        pl.BlockSpec((tn, tk), lambda i, j, k: (j, k)),
        jax.ShapeDtypeStruct((m, nd), out_dtype),
        pl.BlockSpec((tm, tn), lambda i, j, k: (i, j)),
        (m // tm, nd // tn, kd // tk), (tm, tn), _NT)


def _mm_tn(name, pairs, out_dtype, exchange=None, tm=1024, tn=2048, tk=1024):
    kd, m = pairs[0][0].shape
    nd = pairs[0][1].shape[1]
    tm, tn, tk = _tile(m, tm, LANE), _tile(nd, tn, LANE), _tile(kd, tk, 16)
    return _matmul(
        name, pairs,
        pl.BlockSpec((tk, tm), lambda i, j, k: (k, i)),
        pl.BlockSpec((tk, tn), lambda i, j, k: (k, j)),
        jax.ShapeDtypeStruct((m, nd), out_dtype),
        pl.BlockSpec((tm, tn), lambda i, j, k: (i, j)),
        (m // tm, nd // tn, kd // tk), (tm, tn), _TN, exchange=exchange)


def _rms_fwd(name, x, gain, exchange=None):
    m, d = x.shape
    tm = _tile(m, 512, 16)

    def body(ins, outs, scr):
        del scr
        xf = ins[0][...]
        outs[0][...] = ((xf * _rms_scale(xf)) * ins[1][...]).astype(BF16)

    (h,), carried = _hosted_call(
        name, body, (m // tm,),
        [pl.BlockSpec((tm, d), lambda i: (i, 0)), pl.BlockSpec((1, d), lambda i: (0, 0))],
        [pl.BlockSpec((tm, d), lambda i: (i, 0))], [jax.ShapeDtypeStruct((m, d), BF16)], [], (x, gain),
        _NO_EXCHANGE if exchange is None else exchange)
    return h if exchange is None else (h, carried)


def _rms_bwd_rows(xf, gain, dhf):
    r = _rms_scale(xf)
    xh = xf * r
    t = dhf * gain
    dx = r * (t - xh * jnp.mean(t * xh, axis=-1, keepdims=True))
    return dx, jnp.sum(dhf * xh, axis=0, keepdims=True)


def _rms_gain_grad(name, x, gain, dh):
    m, d = x.shape
    tm = _tile(m, 512, 16)

    def body(x_ref, g_ref, dh_ref, dg_ref):
        @pl.when(pl.program_id(0) == 0)
        def _():
            dg_ref[...] = jnp.zeros_like(dg_ref)

        dg_ref[...] += _rms_bwd_rows(x_ref[...], g_ref[...], dh_ref[...])[1]

    row = pl.BlockSpec((tm, d), lambda i: (i, 0))
    vec = pl.BlockSpec((1, d), lambda i: (0, 0))
    return pl.pallas_call(
        body, name=name, grid=(m // tm,), in_specs=[row, vec, row], out_specs=vec,
        out_shape=jax.ShapeDtypeStruct((1, d), F32),
        compiler_params=_cparams(("arbitrary",)),
    )(x, gain, dh)


def _rms_bwd(name, x, gain, dh, dres, exchange):
    m, d = x.shape
    tm = _tile(m, 512, 16)

    def body(ins, outs, scr):
        del scr
        x_ref, g_ref, dh_ref, res_ref = ins
        dx_ref, dxb_ref, dg_ref = outs

        @pl.when(pl.program_id(0) == 0)
        def _():
            dg_ref[...] = jnp.zeros_like(dg_ref)

        dx, dg = _rms_bwd_rows(x_ref[...], g_ref[...], dh_ref[...])
        dx = dx + res_ref[...]
        dx_ref[...] = dx
        dxb_ref[...] = dx.astype(BF16)
        dg_ref[...] += dg

    row = pl.BlockSpec((tm, d), lambda i: (i, 0))
    vec = pl.BlockSpec((1, d), lambda i: (0, 0))
    return _hosted_call(
        name, body, (m // tm,), [row, vec, row, row], [row, row, vec],
        [jax.ShapeDtypeStruct((m, d), F32), jax.ShapeDtypeStruct((m, d), BF16),
         jax.ShapeDtypeStruct((1, d), F32)], [], (x, gain, dh, dres), exchange)


def _loss_head(x, target, gain):
    m, d = x.shape
    tm = _tile(m, 512, 16)

    def body(x_ref, t_ref, g_ref, dx_ref, dxb_ref, dg_ref, loss_ref):
        i = pl.program_id(0)

        @pl.when(i == 0)
        def _():
            dg_ref[...] = jnp.zeros_like(dg_ref)
            loss_ref[...] = jnp.zeros_like(loss_ref)

        xf = x_ref[...]
        r = lax.rsqrt(jnp.mean(xf * xf, axis=-1, keepdims=True) + RMS_EPS)
        xh = xf * r
        g = g_ref[...]
        err = xh * g - t_ref[...]
        loss_ref[...] += jnp.full(loss_ref.shape, (0.5 / d) * jnp.sum(err * err), F32)
        dy = err * (1.0 / d)
        t = dy * g
        dx = r * (t - xh * jnp.mean(t * xh, axis=-1, keepdims=True))
        dx_ref[...] = dx
        dxb_ref[...] = dx.astype(BF16)
        dg_ref[...] += jnp.sum(dy * xh, axis=0, keepdims=True)

    row = pl.BlockSpec((tm, d), lambda i: (i, 0))
    vec = pl.BlockSpec((1, d), lambda i: (0, 0))
    return pl.pallas_call(
        body, name="loss_head", grid=(m // tm,),
        in_specs=[row, row, vec],
        out_specs=[row, row, vec, pl.BlockSpec((1, LANE), lambda i: (0, 0))],
        out_shape=[jax.ShapeDtypeStruct((m, d), F32), jax.ShapeDtypeStruct((m, d), BF16),
                   jax.ShapeDtypeStruct((1, d), F32), jax.ShapeDtypeStruct((1, LANE), F32)],
        compiler_params=_cparams(("arbitrary",)),
    )(x, target, gain)


def _hosted_call(name, body, grid, in_specs, out_specs, out_shape, scratch, operands, exchange):
    n_in, n_out, n_scr = len(in_specs), len(out_specs), len(scratch)
    n_xin, n_xout = len(exchange.ins), len(exchange.out_shape)

    def full_body(*refs):
        pos = 0
        parts = []
        for cnt in (n_in, n_xin, n_out, n_xout, n_scr):
            parts.append(refs[pos:pos + cnt])
            pos += cnt
        ins, x_ins, outs, x_outs, scr = parts
        sems = refs[pos:]
        first = pl.program_id(0) == 0
        last = pl.program_id(0) == grid[0] - 1
        for ax in range(1, len(grid)):
            first = jnp.logical_and(first, pl.program_id(ax) == 0)
            last = jnp.logical_and(last, pl.program_id(ax) == grid[ax] - 1)

        @pl.when(first)
        def _():
            exchange.start(x_ins, x_outs, sems)

        body(ins, outs, scr)

        @pl.when(last)
        def _():
            exchange.finish(x_ins, x_outs, sems)

    aliases = {n_in + k: n_out + k for k in range(n_xin)} if exchange.in_place else {}
    res = pl.pallas_call(
        full_body, name=name, grid=grid,
        in_specs=list(in_specs) + [_HBM] * n_xin, out_specs=list(out_specs) + [_HBM] * n_xout,
        out_shape=list(out_shape) + list(exchange.out_shape),
        scratch_shapes=list(scratch) + list(exchange.sems),
        input_output_aliases=aliases,
        compiler_params=_cparams(("arbitrary",) * len(grid)),
    )(*operands, *exchange.ins)
    return res[:n_out], res[n_out:]


def _ffn_fwd(name, x, h, wg, wu, wd, next_gain, exchange):
    n, d = x.shape
    nb, _, fb = wg.shape
    tm = _tile(n, 512, 16 * FFN_SUBTILES)
    ts = tm // FFN_SUBTILES
    with_norm = next_gain is not None

    def body(ins, outs, scr):
        x_ref, h_ref, wg_ref, wu_ref, wd_ref = ins[:5]
        xo_ref, a_ref, b_ref = outs[:3]
        acc, = scr
        j = pl.program_id(1)

        @pl.when(j == 0)
        def _():
            acc[...] = jnp.zeros_like(acc)

        for s in range(FFN_SUBTILES):
            rows = slice(s * ts, (s + 1) * ts)
            hh = h_ref[rows, :]
            a = _dot(hh, wg_ref[...])
            b = _dot(hh, wu_ref[...])
            a_ref[rows, :] = a.astype(BF16)
            b_ref[rows, :] = b.astype(BF16)
            act = (a * jax.nn.sigmoid(a)) * b
            acc[rows, :] += _dot(act, wd_ref[...])

        @pl.when(j == nb - 1)
        def _():
            xo = x_ref[...] + 0.5 * acc[...]
            xo_ref[...] = xo
            if with_norm:
                outs[3][...] = ((xo * _rms_scale(xo)) * ins[5][...]).astype(BF16)

    row = pl.BlockSpec((tm, d), lambda i, j: (i, 0))
    w_in = pl.BlockSpec((None, d, fb), lambda i, j: (j, 0, 0))
    hid = pl.BlockSpec((None, tm, fb), lambda i, j: (j, i, 0))
    return _hosted_call(
        name, body, (n // tm, nb),
        [row, row, w_in, w_in, pl.BlockSpec((None, fb, d), lambda i, j: (j, 0, 0))]
        + ([pl.BlockSpec((1, d), lambda i, j: (0, 0))] if with_norm else []),
        [row, hid, hid] + ([row] if with_norm else []),
        [jax.ShapeDtypeStruct((n, d), F32), jax.ShapeDtypeStruct((nb, n, fb), BF16),
         jax.ShapeDtypeStruct((nb, n, fb), BF16)]
        + ([jax.ShapeDtypeStruct((n, d), BF16)] if with_norm else []),
        [pltpu.VMEM((tm, d), F32)], (x, h, wg, wu, wd) + ((next_gain,) if with_norm else ()), exchange)


def _ffn_bwd_act(name, dxb, wd, a, b, exchange):
    n, d = dxb.shape
    nb, fb, _ = wd.shape
    subtiles = 2 * FFN_SUBTILES
    tm = _tile(n, 1024, 16 * subtiles)
    ts = tm // subtiles

    def body(ins, outs, scr):
        del scr
        dx_ref, wd_ref, a_ref, b_ref = ins
        da_ref, db_ref, act_ref = outs
        for s in range(subtiles):
            rows = slice(s * ts, (s + 1) * ts)
            dact = 0.5 * _dot(dx_ref[rows, :], wd_ref[...], _NT)
            af = a_ref[rows, :].astype(F32)
            bf = b_ref[rows, :].astype(F32)
            sl, dsl = _silu_parts(af)
            act_ref[rows, :] = (sl * bf).astype(BF16)
            db_ref[rows, :] = (dact * sl).astype(BF16)
            da_ref[rows, :] = (dact * bf * dsl).astype(BF16)

    hid = pl.BlockSpec((None, tm, fb), lambda i, j: (j, i, 0))
    shp = jax.ShapeDtypeStruct((nb, n, fb), BF16)
    return _hosted_call(
        name, body, (n // tm, nb),
        [pl.BlockSpec((tm, d), lambda i, j: (i, 0)), pl.BlockSpec((None, fb, d), lambda i, j: (j, 0, 0)), hid, hid],
        [hid, hid, hid], [shp, shp, shp], [], (dxb, wd, a, b), exchange)


def _proj_rms_bwd(name, pairs, a_spec, b_spec, nk, tm, x, gain, dres, exchange):
    n, d = x.shape
    n_pairs = len(pairs)
    slab = _tile(tm, 128, 16)
    sub = _tile(tm, PROJ_SUBTILE_ROWS, 16)

    def body(ins, outs, scr):
        a_refs, b_refs = ins[:n_pairs], ins[n_pairs:2 * n_pairs]
        x_ref, g_ref, res_ref = ins[2 * n_pairs:]
        dx_ref, dxb_ref, dg_ref = outs
        del scr
        i, k = pl.program_id(0), pl.program_id(1)

        @pl.when(jnp.logical_and(i == 0, k == 0))
        def _():
            dg_ref[...] = jnp.zeros_like(dg_ref)

        @pl.when(k == 0)
        def _():
            dx_ref[...] = jnp.zeros_like(dx_ref)

        for s in range(tm // sub):
            rows = slice(s * sub, (s + 1) * sub)
            part = None
            for a_ref, b_ref in zip(a_refs, b_refs):
                t = _dot(a_ref[rows, :], b_ref[...], _NT)
                part = t if part is None else part + t
            dx_ref[rows, :] += part

        @pl.when(k == nk - 1)
        def _():
            for s in range(tm // slab):
                rows = slice(s * slab, (s + 1) * slab)
                dx, dg = _rms_bwd_rows(x_ref[rows, :], g_ref[...], dx_ref[rows, :])
                dx = dx + res_ref[rows, :]
                dx_ref[rows, :] = dx
                dxb_ref[rows, :] = dx.astype(BF16)
                dg_ref[...] += dg

    row = pl.BlockSpec((tm, d), lambda i, k: (i, 0))
    vec = pl.BlockSpec((1, d), lambda i, k: (0, 0))
    return _hosted_call(
        name, body, (n // tm, nk), [a_spec] * n_pairs + [b_spec] * n_pairs + [row, vec, row],
        [row, row, vec],
        [jax.ShapeDtypeStruct((n, d), F32), jax.ShapeDtypeStruct((n, d), BF16),
         jax.ShapeDtypeStruct((1, d), F32)],
        [], tuple(p[0] for p in pairs) + tuple(p[1] for p in pairs) + (x, gain, dres), exchange)


def _ffn_bwd_dh(name, da, db, wg, wu, exchange):
    nb, n, fb = da.shape
    d = wg.shape[1]
    tm = _tile(n, 1024, 16)
    return _matmul(
        name, [(da, wg), (db, wu)],
        pl.BlockSpec((None, tm, fb), lambda i, j, k: (k, i, 0)),
        pl.BlockSpec((None, d, fb), lambda i, j, k: (k, 0, 0)),
        jax.ShapeDtypeStruct((n, d), F32), pl.BlockSpec((tm, d), lambda i, j, k: (i, 0)),
        (n // tm, 1, nb), (tm, d), _NT, exchange=exchange)


def _mm_nt_rms_bwd(name, a, w, x, gain, dres, exchange, tk=2048):
    n, kd = a.shape
    d = w.shape[0]
    tm, tk = _tile(n, 512, 16), _tile(kd, tk, LANE)
    return _proj_rms_bwd(
        name, [(a, w)],
        pl.BlockSpec((tm, tk), lambda i, k: (i, k)),
        pl.BlockSpec((d, tk), lambda i, k: (0, k)), kd // tk, tm, x, gain, dres, exchange)


def _ffn_bwd_weights(name, h, dxb, da, db, act, exchange):
    n, d = h.shape
    nb, _, fb = da.shape
    tk = _tile(n, 512, 16)
    nk = n // tk

    def body(ins, outs, scr):
        h_ref, dx_ref, da_ref, db_ref, act_ref = ins
        dwg_ref, dwu_ref, dwd_ref = outs
        accg, accu, accd = scr
        k = pl.program_id(1)

        @pl.when(k == 0)
        def _():
            accg[...] = jnp.zeros_like(accg)
            accu[...] = jnp.zeros_like(accu)
            accd[...] = jnp.zeros_like(accd)

        hh = h_ref[...]
        accg[...] += _dot(hh, da_ref[...], _TN)
        accu[...] += _dot(hh, db_ref[...], _TN)
        accd[...] += _dot(act_ref[...], dx_ref[...], _TN)

        @pl.when(k == nk - 1)
        def _():
            dwg_ref[...] = accg[...].astype(BF16)
            dwu_ref[...] = accu[...].astype(BF16)
            dwd_ref[...] = (0.5 * accd[...]).astype(BF16)

    row = pl.BlockSpec((tk, d), lambda j, k: (k, 0))
    hid = pl.BlockSpec((None, tk, fb), lambda j, k: (j, k, 0))
    w_in = pl.BlockSpec((None, d, fb), lambda j, k: (j, 0, 0))
    w_out = pl.BlockSpec((None, fb, d), lambda j, k: (j, 0, 0))
    return _hosted_call(
        name, body, (nb, nk), [row, row, hid, hid, hid], [w_in, w_in, w_out],
        [jax.ShapeDtypeStruct((nb, d, fb), BF16), jax.ShapeDtypeStruct((nb, d, fb), BF16),
         jax.ShapeDtypeStruct((nb, fb, d), BF16)],
        [pltpu.VMEM((d, fb), F32), pltpu.VMEM((d, fb), F32), pltpu.VMEM((fb, d), F32)],
        (h, dxb, da, db, act), exchange)


def _pool_diff(ext_ref, rows, cols, width, t_idx):
    s = ext_ref[POOL_HALO:POOL_HALO + rows, cols]
    for sft in range(1, width):
        s = s + ext_ref[POOL_HALO - sft:POOL_HALO - sft + rows, cols]
    cnt = jnp.minimum(t_idx + 1, width).astype(F32)
    return s / cnt - ext_ref[POOL_HALO:POOL_HALO + rows, cols]


def _pool_fwd(proj, pool_w, pool_scale):
    n = proj.shape[0]
    dp = pool_scale.shape[1]
    c = dp // len(POOL_WINDOWS)
    tm = _tile(n, 512, POOL_HALO)
    hb = tm // POOL_HALO

    def body(u_ref, halo_ref, pw_ref, sc_ref, y_ref, ext):
        i = pl.program_id(0)
        ext[0:POOL_HALO, :] = jnp.where(i > 0, halo_ref[...], 0.0)
        ext[POOL_HALO:, :] = u_ref[...]
        t_idx = i * tm + lax.broadcasted_iota(jnp.int32, (tm, 1), 0)
        for g, width in enumerate(POOL_WINDOWS):
            cols = slice(g * c, (g + 1) * c)
            dgrp = _pool_diff(ext, tm, cols, width, t_idx)
            y_ref[:, cols] = (_dot(dgrp, pw_ref[g]) * sc_ref[:, cols]).astype(BF16)

    return pl.pallas_call(
        body, name="pool_fwd", grid=(n // tm,),
        in_specs=[pl.BlockSpec((tm, dp), lambda i: (i, 0)),
                  pl.BlockSpec((POOL_HALO, dp), lambda i: (jnp.maximum(i * hb - 1, 0), 0)),
                  pl.BlockSpec((len(POOL_WINDOWS), c, c), lambda i: (0, 0, 0)),
                  pl.BlockSpec((1, dp), lambda i: (0, 0))],
        out_specs=pl.BlockSpec((tm, dp), lambda i: (i, 0)),
        out_shape=jax.ShapeDtypeStruct((n, dp), BF16),
        scratch_shapes=[pltpu.VMEM((tm + POOL_HALO, dp), F32)],
        compiler_params=_cparams(("parallel",)),
    )(proj, proj, pool_w, pool_scale)


def _pool_bwd(proj, dymix, pool_w, pool_scale):
    n = proj.shape[0]
    dp = pool_scale.shape[1]
    ng = len(POOL_WINDOWS)
    c = dp // ng
    tm = _tile(n, 512, POOL_HALO)
    hb = tm // POOL_HALO
    nsteps = n // tm
    last_halo = n // POOL_HALO - 1

    def body(u_ref, halo_ref, dy_ref, dyn_ref, pw_ref, sc_ref, du_ref, dpw_ref, dsc_ref, ext, dyext, e_s):
        i = pl.program_id(0)

        @pl.when(i == 0)
        def _():
            dpw_ref[...] = jnp.zeros_like(dpw_ref)
            dsc_ref[...] = jnp.zeros_like(dsc_ref)

        ext[0:POOL_HALO, :] = jnp.where(i > 0, halo_ref[...], 0.0)
        ext[POOL_HALO:, :] = u_ref[...]
        dyext[0:tm, :] = dy_ref[...]
        dyext[tm:, :] = jnp.where(i < nsteps - 1, dyn_ref[...], 0.0)
        t_idx = i * tm + lax.broadcasted_iota(jnp.int32, (tm, 1), 0)
        te_idx = i * tm + lax.broadcasted_iota(jnp.int32, (tm + POOL_HALO, 1), 0)
        for g, width in enumerate(POOL_WINDOWS):
            cols = slice(g * c, (g + 1) * c)
            dgrp = _pool_diff(ext, tm, cols, width, t_idx)
            w_g = pw_ref[g]
            dys = dyext[:, cols] * sc_ref[:, cols]
            ypre = _dot(dgrp, w_g)
            dsc_ref[:, cols] += jnp.sum(dyext[0:tm, cols] * ypre, axis=0, keepdims=True)
            dpw_ref[g] += _dot(dgrp, dys[0:tm], _TN)
            dd = _dot(dys, w_g, _NT)
            e_s[...] = dd / jnp.minimum(te_idx + 1, width).astype(F32)
            acc = e_s[0:tm, :]
            for sft in range(1, width):
                acc = acc + e_s[sft:sft + tm, :]
            du_ref[:, cols] = (acc - dd[0:tm]).astype(BF16)

    return pl.pallas_call(
        body, name="pool_bwd", grid=(nsteps,),
        in_specs=[pl.BlockSpec((tm, dp), lambda i: (i, 0)),
                  pl.BlockSpec((POOL_HALO, dp), lambda i: (jnp.maximum(i * hb - 1, 0), 0)),
                  pl.BlockSpec((tm, dp), lambda i: (i, 0)),
                  pl.BlockSpec((POOL_HALO, dp), lambda i: (jnp.minimum((i + 1) * hb, last_halo), 0)),
                  pl.BlockSpec((ng, c, c), lambda i: (0, 0, 0)),
                  pl.BlockSpec((1, dp), lambda i: (0, 0))],
        out_specs=[pl.BlockSpec((tm, dp), lambda i: (i, 0)),
                   pl.BlockSpec((ng, c, c), lambda i: (0, 0, 0)),
                   pl.BlockSpec((1, dp), lambda i: (0, 0))],
        out_shape=[jax.ShapeDtypeStruct((n, dp), BF16), jax.ShapeDtypeStruct((ng, c, c), F32),
                   jax.ShapeDtypeStruct((1, dp), F32)],
        scratch_shapes=[pltpu.VMEM((tm + POOL_HALO, dp), F32), pltpu.VMEM((tm + POOL_HALO, dp), F32),
                        pltpu.VMEM((tm + POOL_HALO, c), F32)],
        compiler_params=_cparams(("arbitrary",)),
    )(proj, proj, dymix, dymix, pool_w, pool_scale)


def _chunk_cumsum(x):
    row = lax.broadcasted_iota(jnp.int32, x.shape, 0) % CHUNK
    s = 1
    while s < CHUNK:
        x = x + jnp.where(row >= s, pltpu.roll(x, s, 0), 0.0)
        s *= 2
    return x


def _chunk_ends(cum_ref, bend_ref, nc):
    for c in range(nc):
        last = cum_ref[c * CHUNK + CHUNK - 1:(c + 1) * CHUNK, :]
        bend_ref[c * CHUNK:(c + 1) * CHUNK, :] = jnp.broadcast_to(last, (CHUNK, last.shape[1]))


def _gate_logits(alr_ref, wa_ref, ba_ref):
    z = _dot(alr_ref[...], wa_ref[...]) + ba_ref[...]
    la = (jnp.minimum(z, 0.0) - jnp.log(1.0 + jnp.exp(-jnp.abs(z)))) * (1.0 / GATE_TEMP)
    return z, la


def _gla_dims(proj, head_norm):
    n, pw = proj.shape
    dv4 = head_norm.shape[1]
    dk4 = dv4 // 2
    return n, pw, dv4, dk4, dk4 // HEADS, dv4 // HEADS


def _gla_in_specs(t, dk4, dv4, rev):
    alr_blk = (2 * dv4 + 2 * dv4) // LANE
    return [pl.BlockSpec((t, dk4), lambda i: (rev(i), 2)),
            pl.BlockSpec((t, dk4), lambda i: (rev(i), 3)),
            pl.BlockSpec((t, dv4), lambda i: (rev(i), 2)),
            pl.BlockSpec((t, dv4), lambda i: (rev(i), 3)),
            pl.BlockSpec((t, LANE), lambda i: (rev(i), alr_blk))]


def _gla_fwd(proj, wa2p, b_a, head_norm):
    n, _, dv4, dk4, dk, dv = _gla_dims(proj, head_norm)
    t = _tile(n, GLA_ROWS, CHUNK)
    nc = t // CHUNK
    qscale = dk ** -0.5

    def body(q_ref, k_ref, v_ref, g_ref, alr_ref, wa_ref, ba_ref, hn_ref, y_ref, st_ref,
             state, cum_s, bend_s, o_s):
        i = pl.program_id(0)

        @pl.when(i == 0)
        def _():
            state[...] = jnp.zeros_like(state)

        _, la = _gate_logits(alr_ref, wa_ref, ba_ref)
        cum_s[...] = _chunk_cumsum(la)
        _chunk_ends(cum_s, bend_s, nc)
        kd = (k_ref[...] * jnp.exp(bend_s[...] - cum_s[...])).astype(BF16)
        qs = (q_ref[...] * qscale).astype(BF16)
        vv = v_ref[...].astype(BF16)
        units = [(c, h) for c in range(nc) for h in range(HEADS)]
        for c, h in units:
            rows = slice(c * CHUNK, (c + 1) * CHUNK)
            st_ref[c, h] = _dot(vv[rows, h * dv:(h + 1) * dv], kd[rows, h * dk:(h + 1) * dk], _TN)
        for c, h in units:
            e = jnp.exp(bend_s[c * CHUNK:c * CHUNK + 1, h * dk:(h + 1) * dk])
            s_new = state[h] * e + st_ref[c, h]
            state[h] = s_new
            st_ref[c, h] = s_new
        for c, h in units:
            rows = slice(c * CHUNK, (c + 1) * CHUNK)
            o_s[rows, h * dv:(h + 1) * dv] = _dot(qs[rows, h * dk:(h + 1) * dk], st_ref[c, h], _NT)
        for h in range(HEADS):
            hv = slice(h * dv, (h + 1) * dv)
            o = o_s[:, hv]
            gg = g_ref[:, hv]
            y_ref[:, hv] = (((o * _rms_scale(o)) * hn_ref[:, hv]) * (gg * jax.nn.sigmoid(gg))).astype(BF16)

    full = lambda shape: pl.BlockSpec(shape, lambda i: tuple(0 for _ in shape))
    return pl.pallas_call(
        body, name="gla_fwd", grid=(n // t,),
        in_specs=_gla_in_specs(t, dk4, dv4, lambda i: i)
        + [full((LANE, dk4)), full((1, dk4)), full((1, dv4))],
        out_specs=[pl.BlockSpec((t, dv4), lambda i: (i, 0)),
                   pl.BlockSpec((nc, HEADS, dv, dk), lambda i: (i, 0, 0, 0))],
        out_shape=[jax.ShapeDtypeStruct((n, dv4), BF16),
                   jax.ShapeDtypeStruct((n // CHUNK, HEADS, dv, dk), F32)],
        scratch_shapes=[pltpu.VMEM((HEADS, dv, dk), F32), pltpu.VMEM((t, dk4), F32),
                        pltpu.VMEM((t, dk4), F32), pltpu.VMEM((t, dv4), F32)],
        compiler_params=_cparams(("arbitrary",)),
    )(proj, proj, proj, proj, proj, wa2p, b_a, head_norm)


def _gla_bwd(proj, dymix, du, states, wa2p, b_a, head_norm):
    n, pw, dv4, dk4, dk, dv = _gla_dims(proj, head_norm)
    t = _tile(n, GLA_ROWS, CHUNK)
    nc = t // CHUNK
    nsteps = n // t
    qscale = dk ** -0.5
    rev = lambda i: nsteps - 1 - i

    def body(q_ref, k_ref, v_ref, g_ref, alr_ref, dy_ref, du_ref, st_ref, prev_ref, wa_ref, ba_ref, hn_ref,
             dp_ref, dwa_ref, dba_ref, dhn_ref, carry, cum_s, bend_s, gst_s, dkd_s, dee_s, o_s, do_s):
        i = pl.program_id(0)

        @pl.when(i == 0)
        def _():
            carry[...] = jnp.zeros_like(carry)
            dwa_ref[...] = jnp.zeros_like(dwa_ref)
            dba_ref[...] = jnp.zeros_like(dba_ref)
            dhn_ref[...] = jnp.zeros_like(dhn_ref)

        first_step = i == nsteps - 1
        z, la = _gate_logits(alr_ref, wa_ref, ba_ref)
        cum_s[...] = _chunk_cumsum(la)
        _chunk_ends(cum_s, bend_s, nc)
        dec = jnp.exp(bend_s[...] - cum_s[...])
        kd_f = k_ref[...] * dec
        kd = kd_f.astype(BF16)
        qs = (q_ref[...] * qscale).astype(BF16)
        vv = v_ref[...].astype(BF16)
        dp_ref[:, 0:dv4] = du_ref[...]
        units = [(c, h) for c in range(nc) for h in range(HEADS)]

        for c, h in units:
            rows = slice(c * CHUNK, (c + 1) * CHUNK)
            o_s[rows, h * dv:(h + 1) * dv] = _dot(qs[rows, h * dk:(h + 1) * dk], st_ref[c, h], _NT)
        for h in range(HEADS):
            hv = slice(h * dv, (h + 1) * dv)
            o = o_s[:, hv]
            r = _rms_scale(o)
            oh = o * r
            sl, dsl = _silu_parts(g_ref[:, hv])
            dyh = dy_ref[:, hv]
            hn = hn_ref[:, hv]
            tt = dyh * sl
            dhn_ref[:, hv] += jnp.sum(tt * oh, axis=0, keepdims=True)
            dp_ref[:, 3 * dv4 + h * dv:3 * dv4 + (h + 1) * dv] = (dyh * (oh * hn) * dsl).astype(BF16)
            tt = tt * hn
            do_s[:, hv] = (r * (tt - oh * jnp.mean(tt * oh, axis=-1, keepdims=True))).astype(BF16)
        for c, h in units:
            rows = slice(c * CHUNK, (c + 1) * CHUNK)
            hk = slice(h * dk, (h + 1) * dk)
            do = do_s[rows, h * dv:(h + 1) * dv]
            gst_s[c, h] = _dot(do, qs[rows, hk], _TN)
            dp_ref[rows, dv4 + h * dk:dv4 + (h + 1) * dk] = (_dot(do, st_ref[c, h]) * qscale).astype(BF16)

        for c, h in reversed(units):
            g_n = carry[h] + gst_s[c, h]
            gst_s[c, h] = g_n
            carry[h] = g_n * jnp.exp(bend_s[c * CHUNK:c * CHUNK + 1, h * dk:(h + 1) * dk])

        for c, h in units:
            rows = slice(c * CHUNK, (c + 1) * CHUNK)
            hk = slice(h * dk, (h + 1) * dk)
            hv = slice(h * dv, (h + 1) * dv)
            g_n = gst_s[c, h]
            if c > 0:
                s_prev = st_ref[c - 1, h]
            else:
                s_prev = jnp.where(first_step, 0.0, prev_ref[0, h])
            dkd_s[rows, hk] = _dot(vv[rows, hv], g_n)
            dp_ref[rows, 2 * dv4 + h * dv:2 * dv4 + (h + 1) * dv] = _dot(kd[rows, hk], g_n, _NT).astype(BF16)
            dee = jnp.sum(g_n * s_prev, axis=0, keepdims=True) * jnp.exp(bend_s[c * CHUNK:c * CHUNK + 1, hk])
            dee_s[rows, hk] = jnp.broadcast_to(dee, (CHUNK, dk))

        dkd = dkd_s[...]
        dp_ref[:, dv4 + dk4:dv4 + 2 * dk4] = (dkd * dec).astype(BF16)
        w = dkd * kd_f
        dla = (_chunk_cumsum(w) - w) + dee_s[...]
        dz = dla * (1.0 / GATE_TEMP) * (1.0 - jax.nn.sigmoid(z))
        dp_ref[:, 4 * dv4:4 * dv4 + LANE] = _dot(dz, wa_ref[...], _NT).astype(BF16)
        dwa_ref[...] += _dot(alr_ref[...], dz, _TN)
        dba_ref[...] += jnp.sum(dz, axis=0, keepdims=True)

    full = lambda shape: pl.BlockSpec(shape, lambda i: tuple(0 for _ in shape))
    return pl.pallas_call(
        body, name="gla_bwd", grid=(nsteps,),
        in_specs=_gla_in_specs(t, dk4, dv4, rev)
        + [pl.BlockSpec((t, dv4), lambda i: (rev(i), 1)),
           pl.BlockSpec((t, dv4), lambda i: (rev(i), 0)),
           pl.BlockSpec((nc, HEADS, dv, dk), lambda i: (rev(i), 0, 0, 0)),
           pl.BlockSpec((1, HEADS, dv, dk), lambda i: (jnp.maximum(rev(i) * nc - 1, 0), 0, 0, 0)),
           full((LANE, dk4)), full((1, dk4)), full((1, dv4))],
        out_specs=[pl.BlockSpec((t, pw), lambda i: (rev(i), 0)),
                   full((LANE, dk4)), full((1, dk4)), full((1, dv4))],
        out_shape=[jax.ShapeDtypeStruct((n, pw), BF16), jax.ShapeDtypeStruct((LANE, dk4), F32),
                   jax.ShapeDtypeStruct((1, dk4), F32), jax.ShapeDtypeStruct((1, dv4), F32)],
        scratch_shapes=[pltpu.VMEM((HEADS, dv, dk), F32), pltpu.VMEM((t, dk4), F32),
                        pltpu.VMEM((t, dk4), F32), pltpu.VMEM((nc, HEADS, dv, dk), F32),
                        pltpu.VMEM((t, dk4), F32), pltpu.VMEM((t, dk4), F32),
                        pltpu.VMEM((t, dv4), F32), pltpu.VMEM((t, dv4), BF16)],
        compiler_params=_cparams(("arbitrary",)),
    )(proj, proj, proj, proj, proj, dymix, du, states, states, wa2p, b_a, head_norm)


def _softmax(s):
    p = jnp.exp(s - jnp.max(s, axis=-1, keepdims=True))
    return p / jnp.sum(p, axis=-1, keepdims=True)


def _attn_fwd(q, kv):
    n, d = q.shape
    m = kv.shape[0]
    dh = d // HEADS
    tm = _tile(n, 512, 16)
    scale = dh ** -0.5

    def body(q_ref, k_ref, v_ref, o_ref, s_s, p_s):
        for h in range(HEADS):
            hs = slice(h * dh, (h + 1) * dh)
            s_s[:, h * m:(h + 1) * m] = _dot(q_ref[:, hs], k_ref[:, hs], _NT) * scale
        for h in range(HEADS):
            hm = slice(h * m, (h + 1) * m)
            p_s[:, hm] = _softmax(s_s[:, hm]).astype(BF16)
        for h in range(HEADS):
            hs = slice(h * dh, (h + 1) * dh)
            o_ref[:, hs] = _dot(p_s[:, h * m:(h + 1) * m], v_ref[:, hs]).astype(BF16)

    return pl.pallas_call(
        body, name="attn_fwd", grid=(n // tm,),
        in_specs=[pl.BlockSpec((tm, d), lambda i: (i, 0)), pl.BlockSpec((m, d), lambda i: (0, 0)),
                  pl.BlockSpec((m, d), lambda i: (0, 1))],
        out_specs=pl.BlockSpec((tm, d), lambda i: (i, 0)),
        out_shape=jax.ShapeDtypeStruct((n, d), BF16),
        scratch_shapes=[pltpu.VMEM((tm, HEADS * m), F32), pltpu.VMEM((tm, HEADS * m), BF16)],
        compiler_params=_cparams(("parallel",)),
    )(q, kv, kv)


def _attn_bwd(q, kv, do):
    n, d = q.shape
    m = kv.shape[0]
    dh = d // HEADS
    tm = _tile(n, 512, 16)
    scale = dh ** -0.5

    def body(q_ref, k_ref, v_ref, do_ref, dq_ref, dk_ref, dv_ref, s_s, dp_s, p_s, ds_s):
        i = pl.program_id(0)

        @pl.when(i == 0)
        def _():
            dk_ref[...] = jnp.zeros_like(dk_ref)
            dv_ref[...] = jnp.zeros_like(dv_ref)

        for h in range(HEADS):
            hs, hm = slice(h * dh, (h + 1) * dh), slice(h * m, (h + 1) * m)
            s_s[:, hm] = _dot(q_ref[:, hs], k_ref[:, hs], _NT) * scale
            dp_s[:, hm] = _dot(do_ref[:, hs], v_ref[:, hs], _NT)
        for h in range(HEADS):
            hm = slice(h * m, (h + 1) * m)
            p = _softmax(s_s[:, hm])
            dp = dp_s[:, hm]
            p_s[:, hm] = p.astype(BF16)
            ds_s[:, hm] = (p * (dp - jnp.sum(dp * p, axis=-1, keepdims=True)) * scale).astype(BF16)
        for h in range(HEADS):
            hs, hm = slice(h * dh, (h + 1) * dh), slice(h * m, (h + 1) * m)
            dv_ref[:, hs] += _dot(p_s[:, hm], do_ref[:, hs], _TN)
            dq_ref[:, hs] = _dot(ds_s[:, hm], k_ref[:, hs]).astype(BF16)
            dk_ref[:, hs] += _dot(ds_s[:, hm], q_ref[:, hs], _TN)

    row = pl.BlockSpec((tm, d), lambda i: (i, 0))
    memb = pl.BlockSpec((m, d), lambda i: (0, 0))
    return pl.pallas_call(
        body, name="attn_bwd", grid=(n // tm,),
        in_specs=[row, memb, pl.BlockSpec((m, d), lambda i: (0, 1)), row],
        out_specs=[row, memb, memb],
        out_shape=[jax.ShapeDtypeStruct((n, d), BF16), jax.ShapeDtypeStruct((m, d), F32),
                   jax.ShapeDtypeStruct((m, d), F32)],
        scratch_shapes=[pltpu.VMEM((tm, HEADS * m), F32), pltpu.VMEM((tm, HEADS * m), F32),
                        pltpu.VMEM((tm, HEADS * m), BF16), pltpu.VMEM((tm, HEADS * m), BF16)],
        compiler_params=_cparams(("arbitrary",)),
    )(q, kv, kv, do)


def _adamw(name, w, m, v, parts, own_block):
    r, c = w.shape
    tr = _tile(r, max(16, ELEMENTWISE_BLOCK // c), 16)
    row = pl.BlockSpec((tr, c), lambda i, o: (i, 0))
    ops, specs = [w, m, v], [row, row, row]
    for p in parts:
        if p.ndim == 2:
            ops.append(p)
            specs.append(row)
        elif p.shape[0] == 4:
            ops.append(p)
            specs.append(pl.BlockSpec((None, tr, c), lambda i, o: (o[0], i, 0)))
        else:
            for s in range(p.shape[0]):
                ops.append(p)
                specs.append(pl.BlockSpec((None, tr, c), lambda i, o, s=s: (s, i, 0)))
    n_parts = len(ops) - 3
    c1 = 1.0 - ADAM_B1 ** ADAM_STEP
    c2 = 1.0 - ADAM_B2 ** ADAM_STEP

    def body(o_ref, *refs):
        del o_ref
        w_ref, m_ref, v_ref = refs[:3]
        g_refs = refs[3:3 + n_parts]
        go_ref, d_ref, mo_ref, vo_ref = refs[3 + n_parts:]
        g = g_refs[0][...].astype(F32)
        for g_ref in g_refs[1:]:
            g = g + g_ref[...].astype(F32)
        m_new = ADAM_B1 * m_ref[...] + (1.0 - ADAM_B1) * g
        v_new = ADAM_B2 * v_ref[...] + (1.0 - ADAM_B2) * (g * g)
        m_hat = m_new / c1
        v_hat = v_new / c2
        go_ref[...] = g
        d_ref[...] = -ADAM_LR * (m_hat / (jnp.sqrt(v_hat) + ADAM_EPS) + ADAM_WD * w_ref[...])
        mo_ref[...] = m_new
        vo_ref[...] = v_new

    shp = jax.ShapeDtypeStruct((r, c), F32)
    return pl.pallas_call(
        body, name=name,
        grid_spec=pltpu.PrefetchScalarGridSpec(
            num_scalar_prefetch=1, grid=(r // tr,), in_specs=specs, out_specs=[row] * 4),
        out_shape=[shp] * 4,
        compiler_params=_cparams(("parallel",)),
    )(own_block, *ops)


def _pair_add(name, g, recvd, core):
    _, r, c = g.shape
    tr = _tile(r, max(16, ELEMENTWISE_BLOCK // c), 16)

    def body(core_ref, a_ref, b_ref, o_ref):
        del core_ref
        o_ref[...] = (a_ref[...].astype(F32) + b_ref[...].astype(F32)).astype(o_ref.dtype)

    blk = pl.BlockSpec((None, tr, c), lambda s, i, core_ref: (s, i, 0))
    mine = pl.BlockSpec((None, tr, c), lambda s, i, core_ref: (2 * s + core_ref[0], i, 0))
    return pl.pallas_call(
        body, name=name,
        grid_spec=pltpu.PrefetchScalarGridSpec(
            num_scalar_prefetch=1, grid=(4, r // tr), in_specs=[mine, blk], out_specs=blk),
        out_shape=jax.ShapeDtypeStruct(recvd.shape, g.dtype),
        compiler_params=_cparams(("parallel", "parallel")),
    )(core, g, recvd)


def _position():
    return lax.axis_index("x"), lax.axis_index("y"), lax.axis_index("c")


_HBM = pl.BlockSpec(memory_space=pltpu.HBM)

_Exchange = collections.namedtuple("_Exchange", "ins out_shape sems start finish in_place", defaults=(False,))
_NO_EXCHANGE = _Exchange((), (), (), lambda ins, outs, sems: None, lambda ins, outs, sems: None)


def _placed(shard):
    x, y, c = _position()
    start = (4 * x + 2 * y + c,) + (0,) * shard.ndim
    return lax.dynamic_update_slice(lax.empty((N_DEV,) + shard.shape, shard.dtype), shard[None], start)


def _gather_exchange(placed):
    n = len(placed)

    def parties():
        x, y, c = _position()
        return (x, y, c), (x, y, 1 - c), [(1 - x, y), (x, 1 - y), (1 - x, 1 - y)], c

    def copy(ins, outs, sems, a, k, owner, to):
        blk = 4 * owner[0] + 2 * owner[1] + owner[2]
        return pltpu.make_async_remote_copy(
            src_ref=ins[a].at[blk], dst_ref=outs[a].at[blk],
            send_sem=sems[0].at[7 * a + k], recv_sem=sems[1].at[7 * a + k],
            device_id=to, device_id_type=MESH)

    def start(ins, outs, sems):
        me, sibling, chips, c = parties()
        for a in range(n):
            copy(ins, outs, sems, a, 0, me, sibling).start()
            for j, chip in enumerate(chips):
                copy(ins, outs, sems, a, 1 + j, me, (*chip, c)).start()

    def finish(ins, outs, sems):
        me, sibling, chips, c = parties()
        for j, chip in enumerate(chips):
            for a in range(n):
                copy(ins, outs, sems, a, 1 + j, (*chip, c), me).wait_recv()
                copy(ins, outs, sems, a, 4 + j, (*chip, c), sibling).start()
        for a in range(n):
            copy(ins, outs, sems, a, 0, sibling, me).wait_recv()
            copy(ins, outs, sems, a, 0, me, sibling).wait_send()
            for j, chip in enumerate(chips):
                copy(ins, outs, sems, a, 4 + j, (*chip, 1 - c), me).wait_recv()
                copy(ins, outs, sems, a, 1 + j, me, (*chip, c)).wait_send()
                copy(ins, outs, sems, a, 4 + j, (*chip, c), sibling).wait_send()

    return _Exchange(
        ins=tuple(placed), out_shape=tuple(jax.ShapeDtypeStruct(p.shape, p.dtype) for p in placed),
        sems=(pltpu.SemaphoreType.DMA((7 * n,)), pltpu.SemaphoreType.DMA((7 * n,))),
        start=start, finish=finish, in_place=True)


def _pair_exchange(grads):
    n = len(grads)

    def start(ins, recvd, sems):
        send_sems, recv_sems = sems
        x, y, c = _position()
        for a in range(n):
            for chip in range(4):
                pltpu.make_async_remote_copy(
                    src_ref=ins[a].at[2 * chip + (1 - c)], dst_ref=recvd[a].at[chip],
                    send_sem=send_sems.at[a], recv_sem=recv_sems.at[a],
                    device_id=(x, y, 1 - c), device_id_type=MESH).start()

    def finish(ins, recvd, sems):
        del ins
        send_sems, recv_sems = sems
        x, y, c = _position()
        for a in range(n):
            pltpu.make_async_remote_copy(
                src_ref=recvd[a], dst_ref=recvd[a], send_sem=send_sems.at[a], recv_sem=recv_sems.at[a],
                device_id=(x, y, 1 - c), device_id_type=MESH).wait()

    return _Exchange(
        ins=tuple(grads),
        out_shape=tuple(jax.ShapeDtypeStruct((4,) + g.shape[1:], g.dtype) for g in grads),
        sems=(pltpu.SemaphoreType.DMA((n,)), pltpu.SemaphoreType.DMA((n,))),
        start=start, finish=finish)


def _run_exchange(name, exchange):
    n_in, n_out = len(exchange.ins), len(exchange.out_shape)

    def body(*refs):
        ins, outs, sems = refs[:n_in], refs[n_in:n_in + n_out], refs[n_in + n_out:]
        exchange.start(ins, outs, sems)
        exchange.finish(ins, outs, sems)

    return pl.pallas_call(
        body, name=name, in_specs=[_HBM] * n_in, out_specs=[_HBM] * n_out,
        out_shape=list(exchange.out_shape), scratch_shapes=list(exchange.sems),
        input_output_aliases={k: k for k in range(n_in)} if exchange.in_place else {},
    )(*exchange.ins)


def _chip_scatter_exchange(sums):
    n = len(sums)
    offsets = [(1, 0), (0, 1), (1, 1)]

    def start(ins, recvd, sems):
        send_sems, recv_sems = sems
        x, y, c = _position()
        for a in range(n):
            for r, (ox, oy) in enumerate(offsets):
                px = 1 - x if ox else x
                py = 1 - y if oy else y
                pltpu.make_async_remote_copy(
                    src_ref=ins[a].at[2 * px + py], dst_ref=recvd[a].at[r],
                    send_sem=send_sems.at[a], recv_sem=recv_sems.at[a],
                    device_id=(px, py, c), device_id_type=MESH).start()

    def finish(ins, recvd, sems):
        del ins
        send_sems, recv_sems = sems
        x, y, c = _position()
        for a in range(n):
            pltpu.make_async_remote_copy(
                src_ref=recvd[a], dst_ref=recvd[a], send_sem=send_sems.at[a], recv_sem=recv_sems.at[a],
                device_id=(x, y, c), device_id_type=MESH).wait()

    return _Exchange(
        ins=tuple(sums),
        out_shape=tuple(jax.ShapeDtypeStruct((3,) + s.shape[1:], s.dtype) for s in sums),
        sems=(pltpu.SemaphoreType.DMA((n,)), pltpu.SemaphoreType.DMA((n,))),
        start=start, finish=finish)


def _gather_spread_exchange(placed):
    n = len(placed)

    def peers():
        x, y, c = _position()
        return (x, y, c), [(x, y, 1 - c), (1 - x, y, c), (x, 1 - y, c), (1 - x, 1 - y, c)]

    def copy(ins, outs, sems, a, k, owner, to):
        blk = 4 * owner[0] + 2 * owner[1] + owner[2]
        return pltpu.make_async_remote_copy(
            src_ref=ins[a].at[blk], dst_ref=outs[a].at[blk],
            send_sem=sems[0].at[4 * a + k], recv_sem=sems[1].at[4 * a + k],
            device_id=to, device_id_type=MESH)

    def start(ins, outs, sems):
        me, others = peers()
        for a in range(n):
            for k, to in enumerate(others):
                copy(ins, outs, sems, a, k, me, to).start()

    def finish(ins, outs, sems):
        _, others = peers()
        for a in range(n):
            for k, peer in enumerate(others):
                cp = copy(ins, outs, sems, a, k, peer, peer)
                cp.wait_recv()
                cp.wait_send()

    return _Exchange(
        ins=tuple(placed), out_shape=tuple(jax.ShapeDtypeStruct(p.shape, p.dtype) for p in placed),
        sems=(pltpu.SemaphoreType.DMA((4 * n,)), pltpu.SemaphoreType.DMA((4 * n,))),
        start=start, finish=finish, in_place=True)


def _gather_forward_exchange(partial):
    n = len(partial)

    def copy(ins, outs, sems, a, j, pc):
        x, y, c = _position()
        chips = [(1 - x, y), (x, 1 - y), (1 - x, 1 - y)]
        blk = 4 * chips[j][0] + 2 * chips[j][1] + pc
        return pltpu.make_async_remote_copy(
            src_ref=ins[a].at[blk], dst_ref=outs[a].at[blk],
            send_sem=sems[0].at[3 * a + j], recv_sem=sems[1].at[3 * a + j],
            device_id=(x, y, 1 - c), device_id_type=MESH)

    def start(ins, outs, sems):
        c = lax.axis_index("c")
        for a in range(n):
            for j in range(3):
                copy(ins, outs, sems, a, j, c).start()

    def finish(ins, outs, sems):
        c = lax.axis_index("c")
        for a in range(n):
            for j in range(3):
                copy(ins, outs, sems, a, j, 1 - c).wait_recv()
                copy(ins, outs, sems, a, j, c).wait_send()

    return _Exchange(
        ins=tuple(partial), out_shape=tuple(jax.ShapeDtypeStruct(p.shape, p.dtype) for p in partial),
        sems=(pltpu.SemaphoreType.DMA((3 * n,)), pltpu.SemaphoreType.DMA((3 * n,))),
        start=start, finish=finish, in_place=True)


def _all_reduce_small(vec):
    r = vec.shape[0]

    def body(v_ref, o_ref, gbuf, send_sems, recv_sems):
        x, y, c = _position()
        me = 4 * x + 2 * y + c
        gbuf[me] = v_ref[...]
        copies = []
        for k in range(1, N_DEV):
            ox, oy, oc = (k >> 2) & 1, (k >> 1) & 1, k & 1
            peer = (1 - x if ox else x, 1 - y if oy else y, 1 - c if oc else c)
            cp = pltpu.make_async_remote_copy(
                src_ref=gbuf.at[me], dst_ref=gbuf.at[me], send_sem=send_sems.at[k - 1],
                recv_sem=recv_sems.at[k - 1], device_id=peer, device_id_type=MESH)
            cp.start()
            copies.append(cp)
        for cp in copies:
            cp.wait()
        total = gbuf[0]
        for j in range(1, N_DEV):
            total = total + gbuf[j]
        o_ref[...] = total

    return pl.pallas_call(
        body, name="all_reduce_small",
        in_specs=[pl.BlockSpec(memory_space=pltpu.VMEM)],
        out_specs=pl.BlockSpec(memory_space=pltpu.VMEM),
        out_shape=jax.ShapeDtypeStruct(vec.shape, F32),
        scratch_shapes=[pltpu.VMEM((N_DEV, r, LANE), F32), pltpu.SemaphoreType.DMA((N_DEV - 1,)),
                        pltpu.SemaphoreType.DMA((N_DEV - 1,))],
    )(vec)


def _cols_from_blocks(g):
    nb, r, cs = g.shape
    return jnp.transpose(g, (1, 0, 2)).reshape(r, nb * cs)


def _cols_to_blocks(w):
    r, cfull = w.shape
    return jnp.transpose(w.reshape(r, N_DEV, cfull // N_DEV), (1, 0, 2))


def kernel(x, mem, ffn1_norm, ffn1_w_gate, ffn1_w_up, ffn1_w_down, mix_norm, w_in, pool_w, pool_scale, gla_w_a2, gla_b_a, gla_head_norm, w_out, xattn_norm, mem_norm, xattn_w_q, xattn_w_kv, xattn_w_o, ffn2_norm, ffn2_w_gate, ffn2_w_up, ffn2_w_down, final_norm, loss_target, m_ffn1_norm, m_ffn1_w_gate, m_ffn1_w_up, m_ffn1_w_down, m_mix_norm, m_w_in, m_pool_w, m_pool_scale, m_gla_w_a2, m_gla_b_a, m_gla_head_norm, m_w_out, m_xattn_norm, m_mem_norm, m_xattn_w_q, m_xattn_w_kv, m_xattn_w_o, m_ffn2_norm, m_ffn2_w_gate, m_ffn2_w_up, m_ffn2_w_down, m_final_norm, v_ffn1_norm, v_ffn1_w_gate, v_ffn1_w_up, v_ffn1_w_down, v_mix_norm, v_w_in, v_pool_w, v_pool_scale, v_gla_w_a2, v_gla_b_a, v_gla_head_norm, v_w_out, v_xattn_norm, v_mem_norm, v_xattn_w_q, v_xattn_w_kv, v_xattn_w_o, v_ffn2_norm, v_ffn2_w_gate, v_ffn2_w_up, v_ffn2_w_down, v_final_norm):
    weights = dict(ffn1_norm=ffn1_norm, ffn1_w_gate=ffn1_w_gate, ffn1_w_up=ffn1_w_up, ffn1_w_down=ffn1_w_down, mix_norm=mix_norm, w_in=w_in, pool_w=pool_w, pool_scale=pool_scale, gla_w_a2=gla_w_a2, gla_b_a=gla_b_a, gla_head_norm=gla_head_norm, w_out=w_out, xattn_norm=xattn_norm, mem_norm=mem_norm, xattn_w_q=xattn_w_q, xattn_w_kv=xattn_w_kv, xattn_w_o=xattn_w_o, ffn2_norm=ffn2_norm, ffn2_w_gate=ffn2_w_gate, ffn2_w_up=ffn2_w_up, ffn2_w_down=ffn2_w_down, final_norm=final_norm)
    mom1 = dict(ffn1_norm=m_ffn1_norm, ffn1_w_gate=m_ffn1_w_gate, ffn1_w_up=m_ffn1_w_up, ffn1_w_down=m_ffn1_w_down, mix_norm=m_mix_norm, w_in=m_w_in, pool_w=m_pool_w, pool_scale=m_pool_scale, gla_w_a2=m_gla_w_a2, gla_b_a=m_gla_b_a, gla_head_norm=m_gla_head_norm, w_out=m_w_out, xattn_norm=m_xattn_norm, mem_norm=m_mem_norm, xattn_w_q=m_xattn_w_q, xattn_w_kv=m_xattn_w_kv, xattn_w_o=m_xattn_w_o, ffn2_norm=m_ffn2_norm, ffn2_w_gate=m_ffn2_w_gate, ffn2_w_up=m_ffn2_w_up, ffn2_w_down=m_ffn2_w_down, final_norm=m_final_norm)
    mom2 = dict(ffn1_norm=v_ffn1_norm, ffn1_w_gate=v_ffn1_w_gate, ffn1_w_up=v_ffn1_w_up, ffn1_w_down=v_ffn1_w_down, mix_norm=v_mix_norm, w_in=v_w_in, pool_w=v_pool_w, pool_scale=v_pool_scale, gla_w_a2=v_gla_w_a2, gla_b_a=v_gla_b_a, gla_head_norm=v_gla_head_norm, w_out=v_w_out, xattn_norm=v_xattn_norm, mem_norm=v_mem_norm, xattn_w_q=v_xattn_w_q, xattn_w_kv=v_xattn_w_kv, xattn_w_o=v_xattn_w_o, ffn2_norm=v_ffn2_norm, ffn2_w_gate=v_ffn2_w_gate, ffn2_w_up=v_ffn2_w_up, ffn2_w_down=v_ffn2_w_down, final_norm=v_final_norm)
    order = list(weights.keys())

    n, d = x.shape[1], x.shape[2]
    mlen = mem.shape[1]
    x0 = x.reshape(n, d)
    memf = mem.reshape(mlen, d)
    target = loss_target.reshape(n, d)
    dpool = d // 2
    in_cols = w_in.shape[2] * N_DEV
    proj_cols = 2 * d + LANE
    rank = gla_w_a2.shape[1]

    def shard(name):
        return _placed(weights[name][0].astype(BF16))

    h1, (wg1, wu1, wd1) = _rms_fwd(
        "rms_ffn1", x0, ffn1_norm,
        _gather_exchange([shard("ffn1_w_gate"), shard("ffn1_w_up"), shard("ffn1_w_down")]))
    later = ["w_in", "pool_w", "gla_w_a2", "w_out", "xattn_w_q", "xattn_w_kv", "xattn_w_o",
             "ffn2_w_gate", "ffn2_w_up", "ffn2_w_down"]
    (x1, a1, b1, h2), spread = _ffn_fwd(
        "ffn1_fwd", x0, h1, wg1, wu1, wd1, mix_norm, _gather_spread_exchange([shard(k) for k in later]))
    win_g, = _run_exchange("ag_forward_w_in", _gather_forward_exchange(spread[:1]))
    winp = jnp.pad(_cols_from_blocks(win_g), ((0, 0), (0, proj_cols - in_cols)))
    proj, (pw_g, wa2_g, wout_g, wq_g, wkv_g, wo_g, wg2, wu2, wd2) = _mm_nn(
        "mix_in", [(h2, winp)], F32, exchange=_gather_forward_exchange(spread[1:]), tn=1408)

    pw = jnp.transpose(pw_g, (1, 0, 2, 3)).reshape(len(POOL_WINDOWS), dpool // 4, dpool // 4)
    wa2p = jnp.pad(_cols_from_blocks(wa2_g), ((0, LANE - rank), (0, 0)))
    wout = wout_g.reshape(d, d)
    wq = wq_g.reshape(d, d)
    wkv = _cols_from_blocks(wkv_g)
    wo = wo_g.reshape(d, d)
    fnorm = final_norm.reshape(1, d)

    ypool = _pool_fwd(proj, pw, pool_scale)
    ygla, states = _gla_fwd(proj, wa2p, gla_b_a, gla_head_norm)
    x2, h3 = _mm_nn("mix_out", [(ypool, wout[:dpool]), (ygla, wout[dpool:])], F32, res=x1,
                    norm_gain=xattn_norm, tk=1024)
    mh = _rms_fwd("rms_mem", memf, mem_norm)
    q = _mm_nn("xattn_q", [(h3, wq)], BF16)
    kv = _mm_nn("xattn_kv", [(mh, wkv)], BF16)
    att = _attn_fwd(q, kv)
    x3, h4 = _mm_nn("xattn_o", [(att, wo)], F32, res=x2, norm_gain=ffn2_norm)
    (x4, a2, b2), _ = _ffn_fwd("ffn2_fwd", x3, h4, wg2, wu2, wd2, None, _NO_EXCHANGE)
    dx4, dx4b, g_final, loss_part = _loss_head(x4, target, fnorm)

    grads = {}
    core = lax.axis_index("c").astype(jnp.int32).reshape(1)
    chip = (2 * lax.axis_index("x") + lax.axis_index("y")).astype(jnp.int32).reshape(1)
    chip_sums = {}

    def pair_blocks(names):
        return [grads[k].reshape(N_DEV, -1, grads[k].shape[-1]) for k in names]

    def pair_sums(names, blocks, recvd):
        for k, g, r in zip(names, blocks, recvd):
            chip_sums[k] = _pair_add("rs_add_" + k, g, r, core)

    ffn2_names = ["ffn2_w_gate", "ffn2_w_up", "ffn2_w_down"]
    (da2, db2, act2), _ = _ffn_bwd_act("ffn2_bwd_act", dx4b, wd2, a2, b2, _NO_EXCHANGE)
    (grads["ffn2_w_gate"], grads["ffn2_w_up"], grads["ffn2_w_down"]), _ = _ffn_bwd_weights(
        "ffn2_bwd_weights", h4, dx4b, da2, db2, act2, _NO_EXCHANGE)
    blocks = pair_blocks(ffn2_names)
    dh4, _ = _ffn_bwd_dh("ffn2_bwd_dh", da2, db2, wg2, wu2, _NO_EXCHANGE)
    (dx3, dx3b, g_ffn2_norm), recvd = _rms_bwd(
        "rms_ffn2_bwd", x3, ffn2_norm, dh4, dx4, _pair_exchange(blocks))
    pair_sums(ffn2_names, blocks, recvd)

    xattn_names = ["xattn_w_q", "xattn_w_kv", "xattn_w_o"]
    datt = _mm_nt("xattn_do", [(dx3b, wo)], BF16)
    grads["xattn_w_o"] = _mm_tn("xattn_dwo", [(att, dx3b)], BF16).reshape(N_DEV, d // N_DEV, d)
    dq, dk, dv = _attn_bwd(q, kv, datt)
    grads["xattn_w_q"] = _mm_tn("xattn_dwq", [(h3, dq)], BF16).reshape(N_DEV, d // N_DEV, d)
    dkv = jnp.concatenate([dk, dv], axis=1)
    dmh = _mm_nt("xattn_dmh", [(dkv, wkv)], F32)
    grads["xattn_w_kv"] = _cols_to_blocks(_mm_tn("xattn_dwkv", [(mh, dkv)], BF16))
    g_mem_norm = _rms_gain_grad("rms_mem_bwd", memf, mem_norm, dmh)
    blocks = pair_blocks(xattn_names)
    (dx2, dx2b, g_xattn_norm), recvd = _mm_nt_rms_bwd(
        "xattn_dh", dq, wq, x2, xattn_norm, dx3, _pair_exchange(blocks), tk=2048)
    pair_sums(xattn_names, blocks, recvd)

    dymix = _mm_nt("mix_dy", [(dx2b, wout)], F32)
    grads["w_out"] = jnp.concatenate(
        [_mm_tn("mix_dwout_pool", [(ypool, dx2b)], BF16), _mm_tn("mix_dwout_gla", [(ygla, dx2b)], BF16)],
        axis=0).reshape(N_DEV, d // N_DEV, d)
    du, g_pool_w, g_pool_scale = _pool_bwd(proj, dymix, pw, pool_scale)
    dproj, g_wa2p, g_b_a, g_head_norm = _gla_bwd(proj, dymix, du, states, wa2p, gla_b_a, gla_head_norm)
    dwin, recvd = _mm_tn("mix_dwin", [(h2, dproj)], BF16,
                         exchange=_chip_scatter_exchange([chip_sums[ffn2_names[0]]]), tn=1408)
    chip_recvd = {ffn2_names[0]: recvd[0]}
    grads["w_in"] = _cols_to_blocks(dwin[:, :in_cols])
    grads["pool_w"] = jnp.transpose(
        g_pool_w.reshape(len(POOL_WINDOWS), N_DEV, dpool // 4 // N_DEV, dpool // 4), (1, 0, 2, 3))
    grads["gla_w_a2"] = _cols_to_blocks(g_wa2p[:rank])
    mix_names = ["w_in", "pool_w", "gla_w_a2", "w_out"]
    blocks = pair_blocks(mix_names)
    (dx1, dx1b, g_mix_norm), recvd = _mm_nt_rms_bwd(
        "mix_dh", dproj, winp, x1, mix_norm, dx2, _pair_exchange(blocks), tk=1408)
    pair_sums(mix_names, blocks, recvd)

    ffn1_names = ["ffn1_w_gate", "ffn1_w_up", "ffn1_w_down"]
    (da1, db1, act1), recvd = _ffn_bwd_act(
        "ffn1_bwd_act", dx1b, wd1, a1, b1, _chip_scatter_exchange([chip_sums[k] for k in ffn2_names[1:]]))
    chip_recvd.update(zip(ffn2_names[1:], recvd))
    (grads["ffn1_w_gate"], grads["ffn1_w_up"], grads["ffn1_w_down"]), recvd = _ffn_bwd_weights(
        "ffn1_bwd_weights", h1, dx1b, da1, db1, act1,
        _chip_scatter_exchange([chip_sums[k] for k in xattn_names + mix_names]))
    chip_recvd.update(zip(xattn_names + mix_names, recvd))
    blocks = pair_blocks(ffn1_names)
    pair_sums(ffn1_names, blocks, _run_exchange("rs_pair_ffn1", _pair_exchange(blocks)))
    dh1, recvd = _ffn_bwd_dh(
        "ffn1_bwd_dh", da1, db1, wg1, wu1, _chip_scatter_exchange([chip_sums[k] for k in ffn1_names]))
    chip_recvd.update(zip(ffn1_names, recvd))
    (dx0, _, g_ffn1_norm), _ = _rms_bwd("rms_ffn1_bwd", x0, ffn1_norm, dh1, dx1, _NO_EXCHANGE)

    small = [("ffn1_norm", g_ffn1_norm), ("mix_norm", g_mix_norm), ("pool_scale", g_pool_scale),
             ("gla_b_a", g_b_a), ("gla_head_norm", g_head_norm), ("xattn_norm", g_xattn_norm),
             ("mem_norm", g_mem_norm), ("ffn2_norm", g_ffn2_norm), ("final_norm", g_final)]
    packed = jnp.concatenate([g.reshape(-1) for _, g in small] + [loss_part.reshape(-1)])
    slab = 8 * LANE
    padded = -(-packed.shape[0] // slab) * slab
    packed = jnp.pad(packed, (0, padded - packed.shape[0])).reshape(padded // LANE, LANE)
    reduced = _all_reduce_small(packed).reshape(-1)
    small_grads = {}
    off = 0
    for name, g in small:
        small_grads[name] = reduced[off:off + g.size]
        off += g.size
    loss = reduced[off]

    out_g, out_d, out_m, out_v = {}, {}, {}, {}
    for k in order:
        w = weights[k]
        if k in small_grads:
            w2 = w.reshape(1, -1)
            parts = [small_grads[k].reshape(1, -1)]
            own_block = jnp.zeros((1,), jnp.int32)
        else:
            w2 = w.reshape(-1, w.shape[-1])
            parts = [chip_sums[k], chip_recvd[k]]
            own_block = chip
        res = _adamw("adamw_" + k, w2, mom1[k].reshape(w2.shape), mom2[k].reshape(w2.shape), parts, own_block)
        out_g[k], out_d[k], out_m[k], out_v[k] = [r.reshape(w.shape) for r in res]

    return (loss, dx0.reshape(x.shape), *[out_g[k] for k in order], *[out_d[k] for k in order],
            *[out_m[k] for k in order], *[out_v[k] for k in order])
```
